```python
import math
import jax, jax.numpy as jnp
from jax import lax
import numpy as np

D_MODEL = 1024
BATCH = 8
SEQ = 2048
DEPTH = 1

CHUNK = 64
D_MIX = 2 * D_MODEL
D_SSD = D_MIX // 2
D_CONF = D_MIX - D_SSD
SSD_HEAD_DIM = 64
SSD_HEADS = D_SSD // SSD_HEAD_DIM
SSD_GROUPS = 2
SSD_STATE = 128
SSD_CONV = 4
CONF_CONV = 31
FFN_CONV = 3
D_FF = 2816
D_XBC = D_SSD + 2 * SSD_GROUPS * SSD_STATE
IN_Z = D_SSD
IN_XBC = IN_Z + D_XBC
IN_DT = IN_XBC + SSD_HEADS
D_IN = IN_DT + 2 * D_CONF
N_MOD = 6

kernel_name = "hymba_ssd_conformer_convffn_block"


def rmsnorm(x, w, eps=1e-6):
    xf = x.astype(jnp.float32)
    y = xf * lax.rsqrt(jnp.mean(xf * xf, axis=-1, keepdims=True) + eps)
    return (y * w.astype(jnp.float32)).astype(x.dtype)


def layernorm(x, w, b, eps=1e-5):
    xf = x.astype(jnp.float32)
    mu = jnp.mean(xf, axis=-1, keepdims=True)
    var = jnp.mean(jnp.square(xf - mu), axis=-1, keepdims=True)
    y = (xf - mu) * lax.rsqrt(var + eps)
    return (y * w.astype(jnp.float32) + b.astype(jnp.float32)).astype(x.dtype)


def causal_dwconv(x, w, b):
    k = w.shape[0]
    y = lax.conv_general_dilated(
        x, w[:, None, :].astype(x.dtype), window_strides=(1,),
        padding=[(k - 1, 0)], dimension_numbers=("NWC", "WIO", "NWC"),
        feature_group_count=x.shape[-1])
    return y + b


def segsum(a):
    t = a.shape[-1]
    cs = jnp.cumsum(a, axis=-1)
    diff = cs[..., :, None] - cs[..., None, :]
    mask = jnp.tril(jnp.ones((t, t), dtype=bool))
    return jnp.where(mask, diff, -jnp.inf)


def ssd_scan(x, dt, a, b_in, c_in):
    bsz, seq, nh, hp = x.shape
    g, n = b_in.shape[2], b_in.shape[3]
    r = nh // g
    nc = seq // CHUNK
    xdt = (x.astype(jnp.float32) * dt[..., None]).reshape(bsz, nc, CHUNK, g, r, hp)
    a_dt = (a * dt).reshape(bsz, nc, CHUNK, g, r)
    a_dt = jnp.transpose(a_dt, (0, 3, 4, 1, 2))
    bc = b_in.astype(jnp.float32).reshape(bsz, nc, CHUNK, g, n)
    cc = c_in.astype(jnp.float32).reshape(bsz, nc, CHUNK, g, n)
    a_cs = jnp.cumsum(a_dt, axis=-1)
    decay = jnp.exp(segsum(a_dt))
    cb = jnp.einsum("bclgn,bcsgn->bgcls", cc, bc)
    y_diag = jnp.einsum("bgcls,bgrcls,bcsgrp->bclgrp", cb, decay, xdt)
    decay_states = jnp.exp(a_cs[..., -1:] - a_cs)
    states = jnp.einsum("bclgn,bgrcl,bclgrp->bcgrpn", bc, decay_states, xdt)
    chunk_decay = jnp.exp(a_cs[..., -1])

    def step(h, inp):
        s, d = inp
        return h * d[..., None, None] + s, h

    h0 = jnp.zeros((bsz, g, r, hp, n), jnp.float32)
    _, prev = lax.scan(step, h0, (jnp.moveaxis(states, 1, 0), jnp.moveaxis(chunk_decay, -1, 0)))
    prev = jnp.moveaxis(prev, 0, 1)
    y_off = jnp.einsum("bclgn,bcgrpn,bgrcl->bclgrp", cc, prev, jnp.exp(a_cs))
    return (y_diag + y_off).reshape(bsz, seq, nh, hp)


def _fwd_setup_inputs(seed: int = 0) -> dict:
    key = jax.random.key(seed)
    ks = jax.random.split(key, 24)
    f32 = jnp.float32
    nrm = lambda k, shp, s: jax.random.normal(k, shp, f32) * s
    dt0 = jnp.exp(jax.random.uniform(ks[8], (DEPTH, SSD_HEADS), f32, math.log(1e-3), math.log(1e-1)))
    return {
        "x": nrm(ks[0], (BATCH, SEQ, D_MODEL), 1.0),
        "c": nrm(ks[1], (BATCH, D_MODEL), 1.0),
        "ada_w": nrm(ks[2], (DEPTH, D_MODEL, N_MOD * D_MODEL), 0.5 * D_MODEL ** -0.5),
        "ada_b": nrm(ks[3], (DEPTH, N_MOD * D_MODEL), 0.02),
        "norm1_w": 1.0 + nrm(ks[4], (DEPTH, D_MODEL), 0.02),
        "w_in": nrm(ks[5], (DEPTH, D_MODEL, D_IN), D_MODEL ** -0.5),
        "ssd_conv_w": nrm(ks[6], (DEPTH, SSD_CONV, D_XBC), SSD_CONV ** -0.5),
        "ssd_conv_b": nrm(ks[7], (DEPTH, D_XBC), 0.02),
        "dt_bias": dt0 + jnp.log(-jnp.expm1(-dt0)),
        "a_log": jnp.log(jax.random.uniform(ks[9], (DEPTH, SSD_HEADS), f32, 1.0, 16.0)),
        "d_skip": 1.0 + nrm(ks[10], (DEPTH, SSD_HEADS), 0.02),
        "ssd_norm_w": 1.0 + nrm(ks[11], (DEPTH, D_SSD), 0.02),
        "conf_conv_w": nrm(ks[12], (DEPTH, CONF_CONV, D_CONF), CONF_CONV ** -0.5),
        "conf_conv_b": nrm(ks[13], (DEPTH, D_CONF), 0.02),
        "conf_ln_w": 1.0 + nrm(ks[14], (DEPTH, D_CONF), 0.02),
        "conf_ln_b": nrm(ks[15], (DEPTH, D_CONF), 0.02),
        "w_out": nrm(ks[16], (DEPTH, D_MIX, D_MODEL), D_MIX ** -0.5),
        "norm2_w": 1.0 + nrm(ks[17], (DEPTH, D_MODEL), 0.02),
        "w_up": nrm(ks[18], (DEPTH, D_MODEL, 2 * D_FF), D_MODEL ** -0.5),
        "ffn_conv_w": nrm(ks[19], (DEPTH, FFN_CONV, 2 * D_FF), FFN_CONV ** -0.5),
        "ffn_conv_b": nrm(ks[20], (DEPTH, 2 * D_FF), 0.02),
        "w_down": nrm(ks[21], (DEPTH, D_FF, D_MODEL), D_FF ** -0.5),
        "final_norm_w": 1.0 + nrm(ks[22], (D_MODEL,), 0.02),
    }


def _fwd_reference(x, c, ada_w, ada_b, norm1_w, w_in, ssd_conv_w, ssd_conv_b, dt_bias,
              a_log, d_skip, ssd_norm_w, conf_conv_w, conf_conv_b, conf_ln_w,
              conf_ln_b, w_out, norm2_w, w_up, ffn_conv_w, ffn_conv_b, w_down,
              final_norm_w):
    bsz, seq, _ = x.shape
    c_act = jax.nn.silu(c)
    for i in range(DEPTH):
        mod = c_act @ ada_w[i] + ada_b[i]
        sh1, sc1, g1, sh2, sc2, g2 = [m[:, None, :] for m in jnp.split(mod, N_MOD, axis=-1)]

        h = rmsnorm(x, norm1_w[i]) * (1.0 + sc1) + sh1
        proj = h @ w_in[i]
        z = proj[..., :IN_Z]
        xbc = proj[..., IN_Z:IN_XBC]
        dt_raw = proj[..., IN_XBC:IN_DT]
        conf = proj[..., IN_DT:]

        xbc = jax.nn.silu(causal_dwconv(xbc, ssd_conv_w[i], ssd_conv_b[i]))
        xs = xbc[..., :D_SSD].reshape(bsz, seq, SSD_HEADS, SSD_HEAD_DIM)
        bs = xbc[..., D_SSD:D_SSD + SSD_GROUPS * SSD_STATE].reshape(bsz, seq, SSD_GROUPS, SSD_STATE)
        cs = xbc[..., D_SSD + SSD_GROUPS * SSD_STATE:].reshape(bsz, seq, SSD_GROUPS, SSD_STATE)
        dt = jax.nn.softplus(dt_raw.astype(jnp.float32) + dt_bias[i].astype(jnp.float32))
        a = -jnp.exp(a_log[i].astype(jnp.float32))
        y = ssd_scan(xs, dt, a, bs, cs)
        y = y + xs.astype(jnp.float32) * d_skip[i].astype(jnp.float32)[:, None]
        y = y.reshape(bsz, seq, D_SSD).astype(x.dtype)
        y_ssd = rmsnorm(y * jax.nn.silu(z), ssd_norm_w[i])

        u = conf[..., :D_CONF] * jax.nn.sigmoid(conf[..., D_CONF:])
        u = causal_dwconv(u, conf_conv_w[i], conf_conv_b[i])
        u = jax.nn.silu(layernorm(u, conf_ln_w[i], conf_ln_b[i]))

        mix = jnp.concatenate([y_ssd, u], axis=-1) @ w_out[i]
        x = x + g1 * mix

        h = rmsnorm(x, norm2_w[i]) * (1.0 + sc2) + sh2
        up = causal_dwconv(h @ w_up[i], ffn_conv_w[i], ffn_conv_b[i])
        gate, val = up[..., :D_FF], up[..., D_FF:]
        x = x + g2 * ((jax.nn.silu(gate) * val) @ w_down[i])
    return rmsnorm(x, final_norm_w)


import jax as _jax
import jax.numpy as _jnp

TWIN_FORMAT = 'train_step'
FWD_PARAMS = ['x', 'c', 'ada_w', 'ada_b', 'norm1_w', 'w_in', 'ssd_conv_w', 'ssd_conv_b', 'dt_bias', 'a_log', 'd_skip', 'ssd_norm_w', 'conf_conv_w', 'conf_conv_b', 'conf_ln_w', 'conf_ln_b', 'w_out', 'norm2_w', 'w_up', 'ffn_conv_w', 'ffn_conv_b', 'w_down', 'final_norm_w']
TWIN_WEIGHTS = ['ada_w', 'ada_b', 'norm1_w', 'w_in', 'ssd_conv_w', 'ssd_conv_b', 'dt_bias', 'a_log', 'd_skip', 'ssd_norm_w', 'conf_conv_w', 'conf_conv_b', 'conf_ln_w', 'conf_ln_b', 'w_out', 'norm2_w', 'w_up', 'ffn_conv_w', 'ffn_conv_b', 'w_down', 'final_norm_w']
TWIN_DIFF_INPUT = 'x'
TWIN_INPUTS = ['x', 'c', 'ada_w', 'ada_b', 'norm1_w', 'w_in', 'ssd_conv_w', 'ssd_conv_b', 'dt_bias', 'a_log', 'd_skip', 'ssd_norm_w', 'conf_conv_w', 'conf_conv_b', 'conf_ln_w', 'conf_ln_b', 'w_out', 'norm2_w', 'w_up', 'ffn_conv_w', 'ffn_conv_b', 'w_down', 'final_norm_w', 'loss_target', 'm_ada_w', 'm_ada_b', 'm_norm1_w', 'm_w_in', 'm_ssd_conv_w', 'm_ssd_conv_b', 'm_dt_bias', 'm_a_log', 'm_d_skip', 'm_ssd_norm_w', 'm_conf_conv_w', 'm_conf_conv_b', 'm_conf_ln_w', 'm_conf_ln_b', 'm_w_out', 'm_norm2_w', 'm_w_up', 'm_ffn_conv_w', 'm_ffn_conv_b', 'm_w_down', 'm_final_norm_w', 'v_ada_w', 'v_ada_b', 'v_norm1_w', 'v_w_in', 'v_ssd_conv_w', 'v_ssd_conv_b', 'v_dt_bias', 'v_a_log', 'v_d_skip', 'v_ssd_norm_w', 'v_conf_conv_w', 'v_conf_conv_b', 'v_conf_ln_w', 'v_conf_ln_b', 'v_w_out', 'v_norm2_w', 'v_w_up', 'v_ffn_conv_w', 'v_ffn_conv_b', 'v_w_down', 'v_final_norm_w']
TWIN_OUTPUTS = ['loss', 'grad_x', 'grad_ada_w', 'grad_ada_b', 'grad_norm1_w', 'grad_w_in', 'grad_ssd_conv_w', 'grad_ssd_conv_b', 'grad_dt_bias', 'grad_a_log', 'grad_d_skip', 'grad_ssd_norm_w', 'grad_conf_conv_w', 'grad_conf_conv_b', 'grad_conf_ln_w', 'grad_conf_ln_b', 'grad_w_out', 'grad_norm2_w', 'grad_w_up', 'grad_ffn_conv_w', 'grad_ffn_conv_b', 'grad_w_down', 'grad_final_norm_w', 'delta_ada_w', 'delta_ada_b', 'delta_norm1_w', 'delta_w_in', 'delta_ssd_conv_w', 'delta_ssd_conv_b', 'delta_dt_bias', 'delta_a_log', 'delta_d_skip', 'delta_ssd_norm_w', 'delta_conf_conv_w', 'delta_conf_conv_b', 'delta_conf_ln_w', 'delta_conf_ln_b', 'delta_w_out', 'delta_norm2_w', 'delta_w_up', 'delta_ffn_conv_w', 'delta_ffn_conv_b', 'delta_w_down', 'delta_final_norm_w', 'new_m_ada_w', 'new_m_ada_b', 'new_m_norm1_w', 'new_m_w_in', 'new_m_ssd_conv_w', 'new_m_ssd_conv_b', 'new_m_dt_bias', 'new_m_a_log', 'new_m_d_skip', 'new_m_ssd_norm_w', 'new_m_conf_conv_w', 'new_m_conf_conv_b', 'new_m_conf_ln_w', 'new_m_conf_ln_b', 'new_m_w_out', 'new_m_norm2_w', 'new_m_w_up', 'new_m_ffn_conv_w', 'new_m_ffn_conv_b', 'new_m_w_down', 'new_m_final_norm_w', 'new_v_ada_w', 'new_v_ada_b', 'new_v_norm1_w', 'new_v_w_in', 'new_v_ssd_conv_w', 'new_v_ssd_conv_b', 'new_v_dt_bias', 'new_v_a_log', 'new_v_d_skip', 'new_v_ssd_norm_w', 'new_v_conf_conv_w', 'new_v_conf_conv_b', 'new_v_conf_ln_w', 'new_v_conf_ln_b', 'new_v_w_out', 'new_v_norm2_w', 'new_v_w_up', 'new_v_ffn_conv_w', 'new_v_ffn_conv_b', 'new_v_w_down', 'new_v_final_norm_w']
TWIN_LEAF_KINDS = {'loss': 'loss', 'grad_x': 'grad_x', 'grad_ada_w': 'grad_w', 'grad_ada_b': 'grad_w', 'grad_norm1_w': 'grad_w', 'grad_w_in': 'grad_w', 'grad_ssd_conv_w': 'grad_w', 'grad_ssd_conv_b': 'grad_w', 'grad_dt_bias': 'grad_w', 'grad_a_log': 'grad_w', 'grad_d_skip': 'grad_w', 'grad_ssd_norm_w': 'grad_w', 'grad_conf_conv_w': 'grad_w', 'grad_conf_conv_b': 'grad_w', 'grad_conf_ln_w': 'grad_w', 'grad_conf_ln_b': 'grad_w', 'grad_w_out': 'grad_w', 'grad_norm2_w': 'grad_w', 'grad_w_up': 'grad_w', 'grad_ffn_conv_w': 'grad_w', 'grad_ffn_conv_b': 'grad_w', 'grad_w_down': 'grad_w', 'grad_final_norm_w': 'grad_w', 'delta_ada_w': 'delta_w', 'delta_ada_b': 'delta_w', 'delta_norm1_w': 'delta_w', 'delta_w_in': 'delta_w', 'delta_ssd_conv_w': 'delta_w', 'delta_ssd_conv_b': 'delta_w', 'delta_dt_bias': 'delta_w', 'delta_a_log': 'delta_w', 'delta_d_skip': 'delta_w', 'delta_ssd_norm_w': 'delta_w', 'delta_conf_conv_w': 'delta_w', 'delta_conf_conv_b': 'delta_w', 'delta_conf_ln_w': 'delta_w', 'delta_conf_ln_b': 'delta_w', 'delta_w_out': 'delta_w', 'delta_norm2_w': 'delta_w', 'delta_w_up': 'delta_w', 'delta_ffn_conv_w': 'delta_w', 'delta_ffn_conv_b': 'delta_w', 'delta_w_down': 'delta_w', 'delta_final_norm_w': 'delta_w', 'new_m_ada_w': 'new_m', 'new_m_ada_b': 'new_m', 'new_m_norm1_w': 'new_m', 'new_m_w_in': 'new_m', 'new_m_ssd_conv_w': 'new_m', 'new_m_ssd_conv_b': 'new_m', 'new_m_dt_bias': 'new_m', 'new_m_a_log': 'new_m', 'new_m_d_skip': 'new_m', 'new_m_ssd_norm_w': 'new_m', 'new_m_conf_conv_w': 'new_m', 'new_m_conf_conv_b': 'new_m', 'new_m_conf_ln_w': 'new_m', 'new_m_conf_ln_b': 'new_m', 'new_m_w_out': 'new_m', 'new_m_norm2_w': 'new_m', 'new_m_w_up': 'new_m', 'new_m_ffn_conv_w': 'new_m', 'new_m_ffn_conv_b': 'new_m', 'new_m_w_down': 'new_m', 'new_m_final_norm_w': 'new_m', 'new_v_ada_w': 'new_v', 'new_v_ada_b': 'new_v', 'new_v_norm1_w': 'new_v', 'new_v_w_in': 'new_v', 'new_v_ssd_conv_w': 'new_v', 'new_v_ssd_conv_b': 'new_v', 'new_v_dt_bias': 'new_v', 'new_v_a_log': 'new_v', 'new_v_d_skip': 'new_v', 'new_v_ssd_norm_w': 'new_v', 'new_v_conf_conv_w': 'new_v', 'new_v_conf_conv_b': 'new_v', 'new_v_conf_ln_w': 'new_v', 'new_v_conf_ln_b': 'new_v', 'new_v_w_out': 'new_v', 'new_v_norm2_w': 'new_v', 'new_v_w_up': 'new_v', 'new_v_ffn_conv_w': 'new_v', 'new_v_ffn_conv_b': 'new_v', 'new_v_w_down': 'new_v', 'new_v_final_norm_w': 'new_v'}


def _forward(args):
    return _fwd_reference(*[args[k] for k in FWD_PARAMS])


def _output_shape():
    out = _jax.eval_shape(lambda: _forward(_fwd_setup_inputs(0)))
    return out.shape, out.dtype

N_MICROBATCH = 1
ADAM_LR = 0.001
ADAM_B1 = 0.9
ADAM_B2 = 0.999
ADAM_EPS = 1e-08
ADAM_WD = 0.01
ADAM_STEP = 10
PER_EXAMPLE_BATCH_AXIS = {'x': 0, 'c': 0, 'loss_target': 0}
SHARED_INPUTS = []
_WEIGHT_DTYPES = {'ada_w': _jnp.float32, 'ada_b': _jnp.float32, 'norm1_w': _jnp.float32, 'w_in': _jnp.float32, 'ssd_conv_w': _jnp.float32, 'ssd_conv_b': _jnp.float32, 'dt_bias': _jnp.float32, 'a_log': _jnp.float32, 'd_skip': _jnp.float32, 'ssd_norm_w': _jnp.float32, 'conf_conv_w': _jnp.float32, 'conf_conv_b': _jnp.float32, 'conf_ln_w': _jnp.float32, 'conf_ln_b': _jnp.float32, 'w_out': _jnp.float32, 'norm2_w': _jnp.float32, 'w_up': _jnp.float32, 'ffn_conv_w': _jnp.float32, 'ffn_conv_b': _jnp.float32, 'w_down': _jnp.float32, 'final_norm_w': _jnp.float32}
MOMENT_SCALE = {'ada_w': 4.178712e-02, 'ada_b': 7.006129e-02, 'norm1_w': 4.113448e-02, 'w_in': 2.093659e-02, 'ssd_conv_w': 2.328536e-02, 'ssd_conv_b': 2.929341e-02, 'dt_bias': 7.896925e-02, 'a_log': 9.224119e-02, 'd_skip': 2.532010e-01, 'ssd_norm_w': 2.944042e-02, 'conf_conv_w': 1.667867e-02, 'conf_conv_b': 3.084372e-02, 'conf_ln_w': 1.949304e-02, 'conf_ln_b': 1.707243e-02, 'w_out': 3.163699e-02, 'norm2_w': 3.629374e-02, 'w_up': 1.637896e-02, 'ffn_conv_w': 1.642241e-02, 'ffn_conv_b': 1.490976e-02, 'w_down': 2.682461e-02, 'final_norm_w': 1.603441e+01}


def _to_microbatches(a, axis):
    t = _jnp.moveaxis(a, axis, 0)
    t = t.reshape((N_MICROBATCH, t.shape[0] // N_MICROBATCH) + t.shape[1:])
    return _jnp.moveaxis(t, 1, axis + 1)


def setup_inputs(seed: int = 0) -> dict:
    inp = _fwd_setup_inputs(seed)
    key = _jax.random.fold_in(_jax.random.key(seed), 7919)
    shape, _ = _output_shape()
    out = dict(inp)
    out["loss_target"] = _jax.random.normal(_jax.random.fold_in(key, 0), shape, _jnp.float32)
    for i, name in enumerate(TWIN_WEIGHTS):
        w = inp[name].astype(_jnp.float32)
        if MOMENT_SCALE is None:
            s = _jnp.sqrt(_jnp.mean(_jnp.square(w)) + 1e-30)
        else:
            s = MOMENT_SCALE[name]
        km, kv = _jax.random.split(_jax.random.fold_in(key, i + 1))
        out[name] = w
        out["m_" + name] = s * _jax.random.normal(km, w.shape, _jnp.float32)
        out["v_" + name] = (s * s) * _jax.random.uniform(kv, w.shape, _jnp.float32, 0.5, 1.5)
    if N_MICROBATCH > 1:
        for name, axis in PER_EXAMPLE_BATCH_AXIS.items():
            out[name] = _to_microbatches(out[name], axis)
    return {'x': out['x'], 'c': out['c'], 'ada_w': out['ada_w'], 'ada_b': out['ada_b'], 'norm1_w': out['norm1_w'], 'w_in': out['w_in'], 'ssd_conv_w': out['ssd_conv_w'], 'ssd_conv_b': out['ssd_conv_b'], 'dt_bias': out['dt_bias'], 'a_log': out['a_log'], 'd_skip': out['d_skip'], 'ssd_norm_w': out['ssd_norm_w'], 'conf_conv_w': out['conf_conv_w'], 'conf_conv_b': out['conf_conv_b'], 'conf_ln_w': out['conf_ln_w'], 'conf_ln_b': out['conf_ln_b'], 'w_out': out['w_out'], 'norm2_w': out['norm2_w'], 'w_up': out['w_up'], 'ffn_conv_w': out['ffn_conv_w'], 'ffn_conv_b': out['ffn_conv_b'], 'w_down': out['w_down'], 'final_norm_w': out['final_norm_w'], 'loss_target': out['loss_target'], 'm_ada_w': out['m_ada_w'], 'm_ada_b': out['m_ada_b'], 'm_norm1_w': out['m_norm1_w'], 'm_w_in': out['m_w_in'], 'm_ssd_conv_w': out['m_ssd_conv_w'], 'm_ssd_conv_b': out['m_ssd_conv_b'], 'm_dt_bias': out['m_dt_bias'], 'm_a_log': out['m_a_log'], 'm_d_skip': out['m_d_skip'], 'm_ssd_norm_w': out['m_ssd_norm_w'], 'm_conf_conv_w': out['m_conf_conv_w'], 'm_conf_conv_b': out['m_conf_conv_b'], 'm_conf_ln_w': out['m_conf_ln_w'], 'm_conf_ln_b': out['m_conf_ln_b'], 'm_w_out': out['m_w_out'], 'm_norm2_w': out['m_norm2_w'], 'm_w_up': out['m_w_up'], 'm_ffn_conv_w': out['m_ffn_conv_w'], 'm_ffn_conv_b': out['m_ffn_conv_b'], 'm_w_down': out['m_w_down'], 'm_final_norm_w': out['m_final_norm_w'], 'v_ada_w': out['v_ada_w'], 'v_ada_b': out['v_ada_b'], 'v_norm1_w': out['v_norm1_w'], 'v_w_in': out['v_w_in'], 'v_ssd_conv_w': out['v_ssd_conv_w'], 'v_ssd_conv_b': out['v_ssd_conv_b'], 'v_dt_bias': out['v_dt_bias'], 'v_a_log': out['v_a_log'], 'v_d_skip': out['v_d_skip'], 'v_ssd_norm_w': out['v_ssd_norm_w'], 'v_conf_conv_w': out['v_conf_conv_w'], 'v_conf_conv_b': out['v_conf_conv_b'], 'v_conf_ln_w': out['v_conf_ln_w'], 'v_conf_ln_b': out['v_conf_ln_b'], 'v_w_out': out['v_w_out'], 'v_norm2_w': out['v_norm2_w'], 'v_w_up': out['v_w_up'], 'v_ffn_conv_w': out['v_ffn_conv_w'], 'v_ffn_conv_b': out['v_ffn_conv_b'], 'v_w_down': out['v_w_down'], 'v_final_norm_w': out['v_final_norm_w']}


def _loss(weights, diff, rest, loss_target):
    with _jax.named_scope("forward"):
        args = {**rest, TWIN_DIFF_INPUT: diff, **{k: w.astype(_WEIGHT_DTYPES[k]) for k, w in weights.items()}}
        y = _forward(args)
    with _jax.named_scope("loss_head"):
        err = _jnp.square(y.astype(_jnp.float32) - loss_target)
        return 0.5 * _jnp.sum(_jnp.mean(err, axis=-1)) if err.ndim else 0.5 * err


def _adamw(w, g, m, v):
    m = ADAM_B1 * m + (1.0 - ADAM_B1) * g
    v = ADAM_B2 * v + (1.0 - ADAM_B2) * _jnp.square(g)
    m_hat = m / (1.0 - ADAM_B1 ** ADAM_STEP)
    v_hat = v / (1.0 - ADAM_B2 ** ADAM_STEP)
    delta = -ADAM_LR * (m_hat / (_jnp.sqrt(v_hat) + ADAM_EPS) + ADAM_WD * w)
    return delta, m, v


def reference(x, c, ada_w, ada_b, norm1_w, w_in, ssd_conv_w, ssd_conv_b, dt_bias, a_log, d_skip, ssd_norm_w, conf_conv_w, conf_conv_b, conf_ln_w, conf_ln_b, w_out, norm2_w, w_up, ffn_conv_w, ffn_conv_b, w_down, final_norm_w, loss_target, m_ada_w, m_ada_b, m_norm1_w, m_w_in, m_ssd_conv_w, m_ssd_conv_b, m_dt_bias, m_a_log, m_d_skip, m_ssd_norm_w, m_conf_conv_w, m_conf_conv_b, m_conf_ln_w, m_conf_ln_b, m_w_out, m_norm2_w, m_w_up, m_ffn_conv_w, m_ffn_conv_b, m_w_down, m_final_norm_w, v_ada_w, v_ada_b, v_norm1_w, v_w_in, v_ssd_conv_w, v_ssd_conv_b, v_dt_bias, v_a_log, v_d_skip, v_ssd_norm_w, v_conf_conv_w, v_conf_conv_b, v_conf_ln_w, v_conf_ln_b, v_w_out, v_norm2_w, v_w_up, v_ffn_conv_w, v_ffn_conv_b, v_w_down, v_final_norm_w):
    given = dict(x=x, c=c, ada_w=ada_w, ada_b=ada_b, norm1_w=norm1_w, w_in=w_in, ssd_conv_w=ssd_conv_w, ssd_conv_b=ssd_conv_b, dt_bias=dt_bias, a_log=a_log, d_skip=d_skip, ssd_norm_w=ssd_norm_w, conf_conv_w=conf_conv_w, conf_conv_b=conf_conv_b, conf_ln_w=conf_ln_w, conf_ln_b=conf_ln_b, w_out=w_out, norm2_w=norm2_w, w_up=w_up, ffn_conv_w=ffn_conv_w, ffn_conv_b=ffn_conv_b, w_down=w_down, final_norm_w=final_norm_w, loss_target=loss_target, m_ada_w=m_ada_w, m_ada_b=m_ada_b, m_norm1_w=m_norm1_w, m_w_in=m_w_in, m_ssd_conv_w=m_ssd_conv_w, m_ssd_conv_b=m_ssd_conv_b, m_dt_bias=m_dt_bias, m_a_log=m_a_log, m_d_skip=m_d_skip, m_ssd_norm_w=m_ssd_norm_w, m_conf_conv_w=m_conf_conv_w, m_conf_conv_b=m_conf_conv_b, m_conf_ln_w=m_conf_ln_w, m_conf_ln_b=m_conf_ln_b, m_w_out=m_w_out, m_norm2_w=m_norm2_w, m_w_up=m_w_up, m_ffn_conv_w=m_ffn_conv_w, m_ffn_conv_b=m_ffn_conv_b, m_w_down=m_w_down, m_final_norm_w=m_final_norm_w, v_ada_w=v_ada_w, v_ada_b=v_ada_b, v_norm1_w=v_norm1_w, v_w_in=v_w_in, v_ssd_conv_w=v_ssd_conv_w, v_ssd_conv_b=v_ssd_conv_b, v_dt_bias=v_dt_bias, v_a_log=v_a_log, v_d_skip=v_d_skip, v_ssd_norm_w=v_ssd_norm_w, v_conf_conv_w=v_conf_conv_w, v_conf_conv_b=v_conf_conv_b, v_conf_ln_w=v_conf_ln_w, v_conf_ln_b=v_conf_ln_b, v_w_out=v_w_out, v_norm2_w=v_norm2_w, v_w_up=v_w_up, v_ffn_conv_w=v_ffn_conv_w, v_ffn_conv_b=v_ffn_conv_b, v_w_down=v_w_down, v_final_norm_w=v_final_norm_w)
    weights = {n: given[n] for n in TWIN_WEIGHTS}
    shared = {n: given[n] for n in SHARED_INPUTS}
    per_example = {n: given[n] for n in ['x', 'c']}
    grad_fn = _jax.value_and_grad(_loss, argnums=(0, 1))

    def one_microbatch(ex, loss_target):
        ex = dict(ex)
        diff = ex.pop(TWIN_DIFF_INPUT)
        return grad_fn(weights, diff, {**shared, **ex}, loss_target)

    if N_MICROBATCH == 1:
        loss, (grad_w, grad_x) = one_microbatch(per_example, given["loss_target"])
    else:
        def body(carry, xs):
            loss_sum, grad_sum = carry
            l_k, (gw_k, gx_k) = one_microbatch(xs[0], xs[1])
            with _jax.named_scope("update"):
                return (loss_sum + l_k, _jax.tree.map(_jnp.add, grad_sum, gw_k)), gx_k

        init = (_jnp.zeros((), _jnp.float32), _jax.tree.map(_jnp.zeros_like, weights))
        (loss, grad_w), grad_x = _jax.lax.scan(body, init, (per_example, given["loss_target"]))
    with _jax.named_scope("update"):
        delta_w, new_m, new_v = {}, {}, {}
        for n in TWIN_WEIGHTS:
            delta_w[n], new_m[n], new_v[n] = _adamw(weights[n], grad_w[n], given["m_" + n], given["v_" + n])
    return (loss, grad_x, *[grad_w[n] for n in TWIN_WEIGHTS], *[delta_w[n] for n in TWIN_WEIGHTS],
            *[new_m[n] for n in TWIN_WEIGHTS], *[new_v[n] for n in TWIN_WEIGHTS])
```

```python
import functools
import math

import jax
import jax.numpy as jnp
from jax import lax
from jax.experimental import pallas as pl
from jax.experimental.pallas import tpu as pltpu

F32 = jnp.float32
BF16 = jnp.bfloat16
HI = lax.Precision.HIGHEST

N_DEV = 8
D = 1024
D_SSD = 1024
HEAD = 64
N_HEADS = 16
N_STATE = 128
D_XBC = 1536
D_CONF = 1024
D_FF = 2816
K_SSD, K_CONF, K_FFN = 4, 31, 3
D_INP = 5632
LANE = 128
TR = 256
Q = 256
CB = 256
VMEM_LIMIT = 56 * 1024 * 1024

ADAM_LR, ADAM_B1, ADAM_B2, ADAM_EPS, ADAM_WD, ADAM_STEP = 0.001, 0.9, 0.999, 1e-08, 0.01, 10


def _cparams(sem=None):
    return pltpu.CompilerParams(vmem_limit_bytes=VMEM_LIMIT, dimension_semantics=sem)


def _sds(shape, dtype):
    return jax.ShapeDtypeStruct(shape, dtype)


def _sigmoid(x):
    return 1.0 / (1.0 + jnp.exp(-x))


def _silu(x):
    return x * _sigmoid(x)


def _dsilu(x):
    s = _sigmoid(x)
    return s * (1.0 + x * (1.0 - s))


def _softplus(x):
    return jnp.maximum(x, 0.0) + jnp.log(1.0 + jnp.exp(-jnp.abs(x)))


def _dot(a, b):
    return jnp.dot(a.astype(BF16), b.astype(BF16), preferred_element_type=F32)


def _dot_nt(a, b):
    return lax.dot_general(a.astype(BF16), b.astype(BF16), (((1,), (1,)), ((), ())), preferred_element_type=F32)


def _dot_tn(a, b):
    return lax.dot_general(a.astype(BF16), b.astype(BF16), (((0,), (0,)), ((), ())), preferred_element_type=F32)


def _dot_hi(a, b):
    return jnp.dot(a, b, precision=HI, preferred_element_type=F32)


def _dot_tn_hi(a, b):
    return lax.dot_general(a, b, (((0,), (0,)), ((), ())), precision=HI, preferred_element_type=F32)


def _colsum(x):
    return jnp.sum(x, axis=0, keepdims=True)


def _const_spec(shape):
    return pl.BlockSpec(shape, lambda *_: (0,) * len(shape))


def _col_tile(n):
    for t in (1408, 1024, 768, 512, 256, 128):
        if n % t == 0 and t <= n:
            return t
    return n


def mm_nn(a, w, name):
    L, K = a.shape
    N = w.shape[1]
    tn = _col_tile(N)

    def body(a_ref, w_ref, o_ref):
        o_ref[...] = jnp.dot(a_ref[...], w_ref[...], preferred_element_type=F32)

    return pl.pallas_call(
        body, name=name, grid=(N // tn, L // TR), out_shape=_sds((L, N), F32),
        in_specs=[pl.BlockSpec((TR, K), lambda j, i: (i, 0)), pl.BlockSpec((K, tn), lambda j, i: (0, j))],
        out_specs=pl.BlockSpec((TR, tn), lambda j, i: (i, j)),
        compiler_params=_cparams(("parallel", "parallel")))(a, w)


def mm_nt(pairs, name):
    L = pairs[0][0].shape[0]
    K = pairs[0][1].shape[0]
    tk = _col_tile(K)
    n = len(pairs)

    def body(*refs):
        o_ref = refs[-1]
        acc = None
        for p in range(n):
            t = lax.dot_general(refs[2 * p][...], refs[2 * p + 1][...], (((1,), (1,)), ((), ())),
                                preferred_element_type=F32)
            acc = t if acc is None else acc + t
        o_ref[...] = acc

    in_specs, args = [], []
    for a, w in pairs:
        in_specs += [pl.BlockSpec((TR, a.shape[1]), lambda j, i: (i, 0)),
                     pl.BlockSpec((tk, w.shape[1]), lambda j, i: (j, 0))]
        args += [a, w]
    return pl.pallas_call(
        body, name=name, grid=(K // tk, L // TR), out_shape=_sds((L, K), F32), in_specs=in_specs,
        out_specs=pl.BlockSpec((TR, tk), lambda j, i: (i, j)),
        compiler_params=_cparams(("parallel", "parallel")))(*args)


def mm_tn(a, g, name):
    L, M = a.shape
    N = g.shape[1]
    tn = _col_tile(N) if N > 1024 else N
    if M * tn * 4 > 8 * 1024 * 1024:
        tn = 512
    tl = 512 if L % 512 == 0 else TR

    def body(a_ref, g_ref, o_ref):
        @pl.when(pl.program_id(1) == 0)
        def _():
            o_ref[...] = jnp.zeros((M, tn), F32)

        o_ref[...] += lax.dot_general(a_ref[...], g_ref[...], (((0,), (0,)), ((), ())), preferred_element_type=F32)

    return pl.pallas_call(
        body, name=name, grid=(N // tn, L // tl), out_shape=_sds((M, N), F32),
        in_specs=[pl.BlockSpec((tl, M), lambda j, l: (l, 0)), pl.BlockSpec((tl, tn), lambda j, l: (l, j))],
        out_specs=pl.BlockSpec((M, tn), lambda j, l: (0, j)),
        compiler_params=_cparams(("parallel", "arbitrary")))(a, g)


def _row_spec(width=D):
    return pl.BlockSpec((TR, width), lambda i: (i, 0))


def _row_col_spec(width, col):
    return pl.BlockSpec((TR, width), lambda i: (i, col))


def norm_mod(x, mod, w, shift_row, name, res=None, gate_row=None):
    L = x.shape[0]
    has_res = res is not None

    def body(*refs):
        if has_res:
            x_ref, res_ref, mod_ref, w_ref, h_ref, xo_ref = refs
            xin = x_ref[...] + mod_ref[gate_row:gate_row + 1, :] * res_ref[...]
            xo_ref[...] = xin
        else:
            x_ref, mod_ref, w_ref, h_ref = refs
            xin = x_ref[...]
        r = lax.rsqrt(jnp.mean(xin * xin, axis=-1, keepdims=True) + 1e-6)
        h = (xin * r * w_ref[...]) * (1.0 + mod_ref[shift_row + 1:shift_row + 2, :]) + mod_ref[shift_row:shift_row + 1, :]
        h_ref[...] = h.astype(BF16)

    ins = [x] + ([res] if has_res else []) + [mod, w]
    in_specs = [_row_spec()] + ([_row_spec()] if has_res else []) + [_const_spec((8, D)), _const_spec((1, D))]
    out_shape = [_sds((L, D), BF16)] + ([_sds((L, D), F32)] if has_res else [])
    out_specs = [_row_spec()] + ([_row_spec()] if has_res else [])
    out = pl.pallas_call(body, name=name, grid=(L // TR,), out_shape=out_shape, in_specs=in_specs,
                         out_specs=out_specs, compiler_params=_cparams(("parallel",)))(*ins)
    return out if has_res else out[0]


def ln_silu(uc, lnw, lnb, name):
    L = uc.shape[0]

    def body(u_ref, w_ref, b_ref, o_ref):
        u = u_ref[...]
        mu = jnp.mean(u, axis=-1, keepdims=True)
        var = jnp.mean(jnp.square(u - mu), axis=-1, keepdims=True)
        v = (u - mu) * lax.rsqrt(var + 1e-5) * w_ref[...] + b_ref[...]
        o_ref[...] = _silu(v).astype(BF16)

    return pl.pallas_call(body, name=name, grid=(L // TR,), out_shape=_sds((L, D_CONF), BF16),
                          in_specs=[_row_spec(), _const_spec((1, D)), _const_spec((1, D))], out_specs=_row_spec(),
                          compiler_params=_cparams(("parallel",)))(uc, lnw, lnb)


def ln_silu_bwd(du_all, uc, lnw, lnb, name):
    L = uc.shape[0]

    def body(du_ref, u_ref, w_ref, b_ref, o_ref, acc_ref):
        @pl.when(pl.program_id(0) == 0)
        def _():
            acc_ref[...] = jnp.zeros((8, D), F32)

        u = u_ref[...]
        mu = jnp.mean(u, axis=-1, keepdims=True)
        rl = lax.rsqrt(jnp.mean(jnp.square(u - mu), axis=-1, keepdims=True) + 1e-5)
        n = (u - mu) * rl
        v = n * w_ref[...] + b_ref[...]
        dv = du_ref[...] * _dsilu(v)
        acc_ref[0:1, :] += _colsum(dv * n)
        acc_ref[1:2, :] += _colsum(dv)
        dn = dv * w_ref[...]
        o_ref[...] = rl * (dn - jnp.mean(dn, axis=-1, keepdims=True) - n * jnp.mean(dn * n, axis=-1, keepdims=True))

    return pl.pallas_call(body, name=name, grid=(L // TR,), out_shape=[_sds((L, D), F32), _sds((8, D), F32)],
                          in_specs=[_row_col_spec(D, 1), _row_spec(), _const_spec((1, D)), _const_spec((1, D))],
                          out_specs=[_row_spec(), _const_spec((8, D))],
                          compiler_params=_cparams(("arbitrary",)))(du_all, uc, lnw, lnb)


def final_loss(ff, x1, mod, fw, target, name):
    L = ff.shape[0]

    def body(ff_ref, x1_ref, mod_ref, fw_ref, t_ref, dx_ref, dff_ref, acc_ref):
        @pl.when(pl.program_id(0) == 0)
        def _():
            acc_ref[...] = jnp.zeros((8, D), F32)

        ff_v = ff_ref[...]
        g2 = mod_ref[5:6, :]
        x2 = x1_ref[...] + g2 * ff_v
        r = lax.rsqrt(jnp.mean(x2 * x2, axis=-1, keepdims=True) + 1e-6)
        n = x2 * r
        err = n * fw_ref[...] - t_ref[...]
        dy = err * (1.0 / D)
        dn = dy * fw_ref[...]
        dx2 = r * (dn - n * jnp.mean(dn * n, axis=-1, keepdims=True))
        acc_ref[0:1, :] += _colsum(dy * n)
        acc_ref[1:2, :] += _colsum(dx2 * ff_v)
        acc_ref[2:3, :] += _colsum(err * err)
        dx_ref[...] = dx2
        dff_ref[...] = (dx2 * g2).astype(BF16)

    return pl.pallas_call(
        body, name=name, grid=(L // TR,), out_shape=[_sds((L, D), F32), _sds((L, D), BF16), _sds((8, D), F32)],
        in_specs=[_row_spec(), _row_spec(), _const_spec((8, D)), _const_spec((1, D)), _row_spec()],
        out_specs=[_row_spec(), _row_spec(), _const_spec((8, D))],
        compiler_params=_cparams(("arbitrary",)))(ff, x1, mod, fw, target)


def norm_mod_bwd(dh, xin, dres, mod, w, shift_row, name, mix=None, gate_row=None):
    L = dh.shape[0]
    has_mix = mix is not None

    def body(*refs):
        if has_mix:
            dh_ref, x_ref, dres_ref, mod_ref, w_ref, mix_ref, dx_ref, dmix_ref, acc_ref = refs
        else:
            dh_ref, x_ref, dres_ref, mod_ref, w_ref, dx_ref, acc_ref = refs

        @pl.when(pl.program_id(0) == 0)
        def _():
            acc_ref[...] = jnp.zeros((8, D), F32)

        dh_v = dh_ref[...]
        x = x_ref[...]
        r = lax.rsqrt(jnp.mean(x * x, axis=-1, keepdims=True) + 1e-6)
        n = x * r
        nw = n * w_ref[...]
        sc1 = 1.0 + mod_ref[shift_row + 1:shift_row + 2, :]
        acc_ref[0:1, :] += _colsum(dh_v)
        acc_ref[1:2, :] += _colsum(dh_v * nw)
        dnw = dh_v * sc1
        acc_ref[2:3, :] += _colsum(dnw * n)
        dn = dnw * w_ref[...]
        dx = r * (dn - n * jnp.mean(dn * n, axis=-1, keepdims=True)) + dres_ref[...]
        dx_ref[...] = dx
        if has_mix:
            acc_ref[3:4, :] += _colsum(dx * mix_ref[...])
            dmix_ref[...] = (dx * mod_ref[gate_row:gate_row + 1, :]).astype(BF16)

    ins = [dh, xin, dres, mod, w] + ([mix] if has_mix else [])
    in_specs = [_row_spec(), _row_spec(), _row_spec(), _const_spec((8, D)), _const_spec((1, D))] + ([_row_spec()] if has_mix else [])
    out_shape = [_sds((L, D), F32)] + ([_sds((L, D), BF16)] if has_mix else []) + [_sds((8, D), F32)]
    out_specs = [_row_spec()] + ([_row_spec()] if has_mix else []) + [_const_spec((8, D))]
    return pl.pallas_call(body, name=name, grid=(L // TR,), out_shape=out_shape, in_specs=in_specs,
                          out_specs=out_specs, compiler_params=_cparams(("arbitrary",)))(*ins)


def _halo(k):
    return 8 if k <= 9 else 32


def _prev_spec(h, col0):
    return pl.BlockSpec((h, CB), lambda j, i: (jnp.maximum(i * (TR // h) - 1, 0), j + col0))


def _next_spec(h, col0, n_tiles):
    return pl.BlockSpec((h, CB), lambda j, i: (jnp.minimum(i + 1, n_tiles - 1) * (TR // h), j + col0))


def _tile_spec(col0):
    return pl.BlockSpec((TR, CB), lambda j, i: (i, j + col0))


def _w_spec(kp, col0):
    return pl.BlockSpec((kp, CB), lambda j, i: (0, j + col0))


def _causal_taps(ext_ref, w_ref, k_taps, first, rows):
    acc = None
    for k in range(k_taps):
        t = w_ref[k:k + 1, :] * ext_ref[pl.ds(first - (k_taps - 1) + k, rows), :]
        acc = t if acc is None else acc + t
    return acc


def _anticausal_taps(d_ref, w_ref, k_taps, rows):
    acc = None
    for k in range(k_taps):
        t = w_ref[k:k + 1, :] * d_ref[pl.ds(k_taps - 1 - k, rows), :]
        acc = t if acc is None else acc + t
    return acc


def _acc_conv_wgrad(dw_ref, db_ref, d_tile, ext_ref, k_taps, first):
    for k in range(k_taps):
        dw_ref[k:k + 1, :] += _colsum(d_tile * ext_ref[pl.ds(first - (k_taps - 1) + k, TR), :])
    db_ref[...] += _colsum(d_tile)


def conv_silu_fwd(x, col0, width, w, b, name):
    L = x.shape[0]
    k_taps = w.shape[0]
    h = _halo(k_taps)

    def body(xp_ref, x_ref, w_ref, b_ref, o_ref, ext_ref):
        i = pl.program_id(1)
        ext_ref[0:h, :] = jnp.where(i > 0, xp_ref[...], 0.0)
        ext_ref[h:h + TR, :] = x_ref[...]
        o_ref[...] = _silu(_causal_taps(ext_ref, w_ref, k_taps, h, TR) + b_ref[...])

    return pl.pallas_call(
        body, name=name, grid=(width // CB, L // TR), out_shape=_sds((L, width), F32),
        in_specs=[_prev_spec(h, col0), _tile_spec(col0), _w_spec(k_taps, 0), pl.BlockSpec((1, CB), lambda j, i: (0, j))],
        out_specs=_tile_spec(0), scratch_shapes=[pltpu.VMEM((h + TR, CB), F32)],
        compiler_params=_cparams(("parallel", "parallel")))(x, x, w, b)


def conv_silu_bwd(x, col0, width, w, b, dpost, name):
    L = x.shape[0]
    k_taps = w.shape[0]
    h = _halo(k_taps)
    nt = L // TR

    def body(xp_ref, x_ref, xn_ref, d_ref, dn_ref, w_ref, b_ref, dx_ref, dw_ref, db_ref, ext_ref, dpre_ref):
        i = pl.program_id(1)

        @pl.when(i == 0)
        def _():
            dw_ref[...] = jnp.zeros((8, CB), F32)
            db_ref[...] = jnp.zeros((1, CB), F32)

        ext_ref[0:h, :] = jnp.where(i > 0, xp_ref[...], 0.0)
        ext_ref[h:h + TR, :] = x_ref[...]
        ext_ref[h + TR:h + TR + h, :] = xn_ref[...]
        pre = _causal_taps(ext_ref, w_ref, k_taps, h, TR + h) + b_ref[...]
        dpre_ref[0:TR, :] = d_ref[...] * _dsilu(pre[0:TR, :])
        dpre_ref[TR:TR + h, :] = jnp.where(i < nt - 1, dn_ref[...], 0.0) * _dsilu(pre[TR:TR + h, :])
        dx_ref[...] = _anticausal_taps(dpre_ref, w_ref, k_taps, TR).astype(BF16)
        _acc_conv_wgrad(dw_ref, db_ref, dpre_ref[0:TR, :], ext_ref, k_taps, h)

    return pl.pallas_call(
        body, name=name, grid=(width // CB, nt),
        out_shape=[_sds((L, width), BF16), _sds((8, width), F32), _sds((1, width), F32)],
        in_specs=[_prev_spec(h, col0), _tile_spec(col0), _next_spec(h, col0, nt), _tile_spec(0), _next_spec(h, 0, nt),
                  _w_spec(k_taps, 0), pl.BlockSpec((1, CB), lambda j, i: (0, j))],
        out_specs=[_tile_spec(0), _w_spec(8, 0), pl.BlockSpec((1, CB), lambda j, i: (0, j))],
        scratch_shapes=[pltpu.VMEM((h + TR + h, CB), F32), pltpu.VMEM((TR + h, CB), F32)],
        compiler_params=_cparams(("parallel", "arbitrary")))(x, x, x, dpost, dpost, w, b)


def conf_conv_fwd(proj, col_a, col_g, w, b, name):
    L = proj.shape[0]
    k_taps = w.shape[0]
    h = _halo(k_taps)

    def body(ap_ref, a_ref, gp_ref, g_ref, w_ref, b_ref, o_ref, ext_ref):
        i = pl.program_id(1)
        ext_ref[0:h, :] = jnp.where(i > 0, ap_ref[...] * _sigmoid(gp_ref[...]), 0.0)
        ext_ref[h:h + TR, :] = a_ref[...] * _sigmoid(g_ref[...])
        o_ref[...] = _causal_taps(ext_ref, w_ref, k_taps, h, TR) + b_ref[...]

    return pl.pallas_call(
        body, name=name, grid=(D_CONF // CB, L // TR), out_shape=_sds((L, D_CONF), F32),
        in_specs=[_prev_spec(h, col_a), _tile_spec(col_a), _prev_spec(h, col_g), _tile_spec(col_g), _w_spec(k_taps, 0),
                  pl.BlockSpec((1, CB), lambda j, i: (0, j))],
        out_specs=_tile_spec(0), scratch_shapes=[pltpu.VMEM((h + TR, CB), F32)],
        compiler_params=_cparams(("parallel", "parallel")))(proj, proj, proj, proj, w, b)


def conf_conv_bwd(proj, col_a, col_g, w, duc, name):
    L = proj.shape[0]
    k_taps = w.shape[0]
    h = _halo(k_taps)
    nt = L // TR

    def body(ap_ref, a_ref, gp_ref, g_ref, d_ref, dn_ref, w_ref, da_ref, dg_ref, dw_ref, db_ref, ext_ref, dext_ref):
        i = pl.program_id(1)

        @pl.when(i == 0)
        def _():
            dw_ref[...] = jnp.zeros((32, CB), F32)
            db_ref[...] = jnp.zeros((1, CB), F32)

        a = a_ref[...]
        s = _sigmoid(g_ref[...])
        ext_ref[0:h, :] = jnp.where(i > 0, ap_ref[...] * _sigmoid(gp_ref[...]), 0.0)
        ext_ref[h:h + TR, :] = a * s
        dext_ref[0:TR, :] = d_ref[...]
        dext_ref[TR:TR + h, :] = jnp.where(i < nt - 1, dn_ref[...], 0.0)
        du0 = _anticausal_taps(dext_ref, w_ref, k_taps, TR)
        da_ref[...] = (du0 * s).astype(BF16)
        dg_ref[...] = (du0 * a * s * (1.0 - s)).astype(BF16)
        _acc_conv_wgrad(dw_ref, db_ref, d_ref[...], ext_ref, k_taps, h)

    return pl.pallas_call(
        body, name=name, grid=(D_CONF // CB, nt),
        out_shape=[_sds((L, D_CONF), BF16), _sds((L, D_CONF), BF16), _sds((32, D_CONF), F32), _sds((1, D_CONF), F32)],
        in_specs=[_prev_spec(h, col_a), _tile_spec(col_a), _prev_spec(h, col_g), _tile_spec(col_g), _tile_spec(0),
                  _next_spec(h, 0, nt), _w_spec(k_taps, 0)],
        out_specs=[_tile_spec(0), _tile_spec(0), _w_spec(32, 0), pl.BlockSpec((1, CB), lambda j, i: (0, j))],
        scratch_shapes=[pltpu.VMEM((h + TR, CB), F32), pltpu.VMEM((TR + h, CB), F32)],
        compiler_params=_cparams(("parallel", "arbitrary")))(proj, proj, proj, proj, duc, duc, w)


def ffn_conv_fwd(up, w, b, name):
    L = up.shape[0]
    k_taps = w.shape[0]
    h = _halo(k_taps)
    cv = D_FF // CB

    def body(gp_ref, g_ref, vp_ref, v_ref, wg_ref, wv_ref, bg_ref, bv_ref, o_ref, eg_ref, ev_ref):
        i = pl.program_id(1)
        eg_ref[0:h, :] = jnp.where(i > 0, gp_ref[...], 0.0)
        eg_ref[h:h + TR, :] = g_ref[...]
        ev_ref[0:h, :] = jnp.where(i > 0, vp_ref[...], 0.0)
        ev_ref[h:h + TR, :] = v_ref[...]
        pg = _causal_taps(eg_ref, wg_ref, k_taps, h, TR) + bg_ref[...]
        pv = _causal_taps(ev_ref, wv_ref, k_taps, h, TR) + bv_ref[...]
        o_ref[...] = (_silu(pg) * pv).astype(BF16)

    bspec = lambda c0: pl.BlockSpec((1, CB), lambda j, i: (0, j + c0))
    return pl.pallas_call(
        body, name=name, grid=(cv, L // TR), out_shape=_sds((L, D_FF), BF16),
        in_specs=[_prev_spec(h, 0), _tile_spec(0), _prev_spec(h, cv), _tile_spec(cv), _w_spec(k_taps, 0), _w_spec(k_taps, cv),
                  bspec(0), bspec(cv)],
        out_specs=_tile_spec(0), scratch_shapes=[pltpu.VMEM((h + TR, CB), F32), pltpu.VMEM((h + TR, CB), F32)],
        compiler_params=_cparams(("parallel", "parallel")))(up, up, up, up, w, w, b, b)


def ffn_conv_bwd(up, w, b, dact, name):
    L = up.shape[0]
    k_taps = w.shape[0]
    h = _halo(k_taps)
    nt = L // TR
    cv = D_FF // CB

    def body(gp_ref, g_ref, gn_ref, vp_ref, v_ref, vn_ref, d_ref, dn_ref, wg_ref, wv_ref, bg_ref, bv_ref,
             dg_ref, dv_ref, dwg_ref, dwv_ref, dbg_ref, dbv_ref, eg_ref, ev_ref, pg_ref, pv_ref):
        i = pl.program_id(1)

        @pl.when(i == 0)
        def _():
            dwg_ref[...] = jnp.zeros((8, CB), F32)
            dwv_ref[...] = jnp.zeros((8, CB), F32)
            dbg_ref[...] = jnp.zeros((1, CB), F32)
            dbv_ref[...] = jnp.zeros((1, CB), F32)

        for e_ref, p_ref, c_ref, n_ref in ((eg_ref, gp_ref, g_ref, gn_ref), (ev_ref, vp_ref, v_ref, vn_ref)):
            e_ref[0:h, :] = jnp.where(i > 0, p_ref[...], 0.0)
            e_ref[h:h + TR, :] = c_ref[...]
            e_ref[h + TR:h + TR + h, :] = n_ref[...]
        pg = _causal_taps(eg_ref, wg_ref, k_taps, h, TR + h) + bg_ref[...]
        pv = _causal_taps(ev_ref, wv_ref, k_taps, h, TR + h) + bv_ref[...]
        dact_t = d_ref[...]
        dact_n = jnp.where(i < nt - 1, dn_ref[...], 0.0)
        pg_ref[0:TR, :] = dact_t * pv[0:TR, :] * _dsilu(pg[0:TR, :])
        pg_ref[TR:TR + h, :] = dact_n * pv[TR:TR + h, :] * _dsilu(pg[TR:TR + h, :])
        pv_ref[0:TR, :] = dact_t * _silu(pg[0:TR, :])
        pv_ref[TR:TR + h, :] = dact_n * _silu(pg[TR:TR + h, :])
        dg_ref[...] = _anticausal_taps(pg_ref, wg_ref, k_taps, TR).astype(BF16)
        dv_ref[...] = _anticausal_taps(pv_ref, wv_ref, k_taps, TR).astype(BF16)
        _acc_conv_wgrad(dwg_ref, dbg_ref, pg_ref[0:TR, :], eg_ref, k_taps, h)
        _acc_conv_wgrad(dwv_ref, dbv_ref, pv_ref[0:TR, :], ev_ref, k_taps, h)

    bspec = lambda c0: pl.BlockSpec((1, CB), lambda j, i: (0, j + c0))
    ext = pltpu.VMEM((h + TR + h, CB), F32)
    dpre = pltpu.VMEM((TR + h, CB), F32)
    return pl.pallas_call(
        body, name=name, grid=(cv, nt),
        out_shape=[_sds((L, D_FF), BF16), _sds((L, D_FF), BF16), _sds((8, D_FF), F32), _sds((8, D_FF), F32),
                   _sds((1, D_FF), F32), _sds((1, D_FF), F32)],
        in_specs=[_prev_spec(h, 0), _tile_spec(0), _next_spec(h, 0, nt), _prev_spec(h, cv), _tile_spec(cv), _next_spec(h, cv, nt),
                  _tile_spec(0), _next_spec(h, 0, nt), _w_spec(k_taps, 0), _w_spec(k_taps, cv), bspec(0), bspec(cv)],
        out_specs=[_tile_spec(0), _tile_spec(0), _w_spec(8, 0), _w_spec(8, 0), bspec(0), bspec(0)],
        scratch_shapes=[ext, ext, dpre, dpre],
        compiler_params=_cparams(("parallel", "arbitrary")))(up, up, up, up, up, up, dact, dact, w, w, b, b)


def _ssd_common(xbc_ref, dt_ref, dtb_ref, alog_ref, cs_ref):
    xs = xbc_ref[:, 0:D_SSD]
    sp_in = dt_ref[...] + dtb_ref[...]
    dtf = _softplus(sp_in)
    a_f = -jnp.exp(alog_ref[...])
    a_dt = dtf * a_f
    row = lax.broadcasted_iota(jnp.int32, (Q, Q), 0)
    col = lax.broadcasted_iota(jnp.int32, (Q, Q), 1)
    causal = row >= col
    cs = _dot_hi(causal.astype(F32), a_dt)
    cs_ref[...] = cs
    cs_last = cs_ref[Q - 1:Q, :]
    return xs, sp_in, dtf, a_f, cs, cs_last, causal


def _head_decay(cs_j, cst_ref, e, causal):
    lane = lax.broadcasted_iota(jnp.int32, (Q, LANE), 1)
    rolled = pltpu.roll(cs_j, HEAD, 1)
    own = (lane < HEAD) if e == 0 else (lane >= HEAD)
    col_b = jnp.where(own, cs_j, rolled)
    col_b = jnp.concatenate([col_b] * (Q // LANE), axis=1)
    row_b = cst_ref[e * HEAD:e * HEAD + 1, :]
    return jnp.where(causal, jnp.exp(jnp.minimum(col_b - row_b, 0.0)), 0.0)


def ssd_fwd(xbc, proj, dtb_f, alog_f, dsk_f, snw, name):
    L = xbc.shape[0]
    nc = L // Q

    def body(xbc_ref, z_ref, dt_ref, dtb_ref, alog_ref, dsk_ref, snw_ref, y_ref, yn_ref, sp_ref, s_ref, cs_ref, cst_ref, yd_ref):
        @pl.when(pl.program_id(0) == 0)
        def _():
            s_ref[...] = jnp.zeros((N_STATE, D_SSD), F32)

        xs, _, dtf, a_f, cs, cs_last, causal = _ssd_common(xbc_ref, dt_ref, dtb_ref, alog_ref, cs_ref)
        e_cs = jnp.exp(cs)
        xdt = xs * dtf
        zst = jnp.exp(cs_last - cs) * xdt
        sp_ref[0] = s_ref[...]
        lane = lax.broadcasted_iota(jnp.int32, (Q, LANE), 1)
        for g in range(2):
            gl = slice(g * 512, g * 512 + 512)
            b_g = xbc_ref[:, D_SSD + g * N_STATE:D_SSD + (g + 1) * N_STATE]
            c_g = xbc_ref[:, D_SSD + 2 * N_STATE + g * N_STATE:D_SSD + 2 * N_STATE + (g + 1) * N_STATE]
            s_prev = s_ref[:, gl]
            cb = _dot_nt(c_g, b_g)
            yd_ref[:, gl] = e_cs[:, gl] * _dot(c_g, s_prev)
            for j in range(4):
                tl = slice(g * 512 + j * LANE, g * 512 + (j + 1) * LANE)
                cs_j = cs[:, tl]
                cst_ref[...] = cs_j.T
                x_j = xdt[:, tl]
                o0 = _dot(cb * _head_decay(cs_j, cst_ref, 0, causal), x_j)
                o1 = _dot(cb * _head_decay(cs_j, cst_ref, 1, causal), x_j)
                yd_ref[:, tl] += jnp.where(lane < HEAD, o0, o1)
            s_ref[:, gl] = jnp.exp(cs_last[:, gl]) * s_prev + _dot_tn(b_g, zst[:, gl])
        y = yd_ref[...] + xs * dsk_ref[...]
        y_ref[...] = y
        yz = y * _silu(z_ref[...])
        r = lax.rsqrt(jnp.mean(yz * yz, axis=-1, keepdims=True) + 1e-6)
        yn_ref[...] = (yz * r * snw_ref[...]).astype(BF16)

    chunk = lambda w, c: pl.BlockSpec((Q, w), lambda i: (i, c))
    return pl.pallas_call(
        body, name=name, grid=(nc,),
        out_shape=[_sds((L, D_SSD), F32), _sds((L, D_SSD), BF16), _sds((nc, N_STATE, D_SSD), F32)],
        in_specs=[chunk(D_XBC, 0), chunk(D, 0), chunk(D, 1)] + [_const_spec((1, D))] * 4,
        out_specs=[chunk(D, 0), chunk(D, 0), pl.BlockSpec((1, N_STATE, D_SSD), lambda i: (i, 0, 0))],
        scratch_shapes=[pltpu.VMEM((N_STATE, D_SSD), F32), pltpu.VMEM((Q, D_SSD), F32), pltpu.VMEM((LANE, Q), F32),
                        pltpu.VMEM((Q, D_SSD), F32)],
        compiler_params=_cparams(("arbitrary",)))(xbc, proj, proj, dtb_f, alog_f, dsk_f, snw)


def ssd_bwd(dmixin, y, xbc, proj, s_prev_all, dtb_f, alog_f, dsk_f, snw, name):
    L = xbc.shape[0]
    nc = L // Q

    def body(dyn_ref, y_ref, xbc_ref, z_ref, dt_ref, sp_ref, dtb_ref, alog_ref, dsk_ref, snw_ref,
             dz_ref, ddt_ref, dxbc_ref, acc_ref, acc16_ref, ds_ref, cs_ref, cst_ref, dcs_ref, dx_ref):
        step = pl.program_id(0)

        @pl.when(step == 0)
        def _():
            ds_ref[...] = jnp.zeros((N_STATE, D_SSD), F32)
            acc_ref[...] = jnp.zeros((8, D), F32)

        z = z_ref[...]
        y = y_ref[...]
        sz = _sigmoid(z)
        siluz = z * sz
        yz = y * siluz
        r = lax.rsqrt(jnp.mean(yz * yz, axis=-1, keepdims=True) + 1e-6)
        n = yz * r
        dyn = dyn_ref[...]
        acc_ref[0:1, :] += _colsum(dyn * n)
        dn = dyn * snw_ref[...]
        dyz = r * (dn - n * jnp.mean(dn * n, axis=-1, keepdims=True))
        dy = dyz * siluz
        dz_ref[...] = (dyz * y * (sz * (1.0 + z * (1.0 - sz)))).astype(BF16)

        xs, sp_in, dtf, a_f, cs, cs_last, causal = _ssd_common(xbc_ref, dt_ref, dtb_ref, alog_ref, cs_ref)
        acc_ref[3:4, :] += _colsum(dy * xs)
        e_cs = jnp.exp(cs)
        xdt = xs * dtf
        dst = jnp.exp(cs_last - cs)
        zst = dst * xdt
        e_last = jnp.exp(cs_last)
        lane = lax.broadcasted_iota(jnp.int32, (Q, LANE), 1)
        ones = jnp.ones((Q, LANE), F32)
        dcs_last_parts = []
        for g in range(2):
            gl = slice(g * 512, g * 512 + 512)
            b_g = xbc_ref[:, D_SSD + g * N_STATE:D_SSD + (g + 1) * N_STATE]
            c_g = xbc_ref[:, D_SSD + 2 * N_STATE + g * N_STATE:D_SSD + 2 * N_STATE + (g + 1) * N_STATE]
            s_prev = sp_ref[0, :, gl]
            ds_g = ds_ref[:, gl]
            dy_g = dy[:, gl]
            cb = _dot_nt(c_g, b_g)
            y_off = e_cs[:, gl] * _dot(c_g, s_prev)
            edy = e_cs[:, gl] * dy_g
            d_c = _dot_nt(edy, s_prev)
            d_z = _dot(b_g, ds_g)
            d_b = _dot_nt(zst[:, gl], ds_g)
            t_g = d_z * zst[:, gl]
            dcs_ref[:, gl] = dy_g * y_off - t_g
            dx_ref[:, gl] = d_z * dst[:, gl]
            dcs_last_parts.append(_colsum(t_g) + _colsum(ds_g * s_prev) * e_last[:, gl])
            ds_ref[:, gl] = e_last[:, gl] * ds_g + _dot_tn(c_g, edy)
            dcb = jnp.zeros((Q, Q), F32)
            for j in range(4):
                tl = slice(g * 512 + j * LANE, g * 512 + (j + 1) * LANE)
                cs_j = cs[:, tl]
                cst_ref[...] = cs_j.T
                x_j = xdt[:, tl]
                dy_j = dy[:, tl]
                dx_j = jnp.zeros((Q, LANE), F32)
                dcs_j = jnp.zeros((Q, LANE), F32)
                for e in range(2):
                    own = (lane < HEAD) if e == 0 else (lane >= HEAD)
                    w_h = _head_decay(cs_j, cst_ref, e, causal)
                    g_h = cb * w_h
                    dy_m = jnp.where(own, dy_j, 0.0)
                    d_g = _dot_nt(dy_m, x_j)
                    dx_j = dx_j + _dot_tn(g_h, dy_m)
                    dcb = dcb + d_g * w_h
                    p_h = d_g * g_h
                    dcs_j = dcs_j + jnp.where(own, _dot_hi(p_h, ones) - _dot_tn_hi(p_h, ones), 0.0)
                dcs_ref[:, tl] += dcs_j * (1.0 / HEAD)
                dx_ref[:, tl] += dx_j
            d_c = d_c + _dot(dcb, b_g)
            d_b = d_b + _dot_tn(dcb, c_g)
            dxbc_ref[:, D_SSD + g * N_STATE:D_SSD + (g + 1) * N_STATE] = d_b
            dxbc_ref[:, D_SSD + 2 * N_STATE + g * N_STATE:D_SSD + 2 * N_STATE + (g + 1) * N_STATE] = d_c
        dcs_last = jnp.concatenate(dcs_last_parts, axis=1)
        anticausal = lax.broadcasted_iota(jnp.int32, (Q, Q), 0) <= lax.broadcasted_iota(jnp.int32, (Q, Q), 1)
        d_adt = _dot_hi(anticausal.astype(F32), dcs_ref[...]) + dcs_last
        dx = dx_ref[...]
        acc_ref[2:3, :] += _colsum(d_adt * dtf) * a_f
        d_dtf = d_adt * a_f + dx * xs
        dxbc_ref[:, 0:D_SSD] = dx * dtf + dy * dsk_ref[...]
        d_raw = d_dtf * _sigmoid(sp_in)
        acc_ref[1:2, :] += _colsum(d_raw)
        head_of_lane = lax.broadcasted_iota(jnp.int32, (D_SSD, LANE), 0) // HEAD
        fold = (head_of_lane == lax.broadcasted_iota(jnp.int32, (D_SSD, LANE), 1)).astype(F32)
        ddt_ref[...] = _dot_hi(d_raw, fold).astype(BF16)

        @pl.when(step == nc - 1)
        def _():
            acc16_ref[...] = _dot_hi(acc_ref[...], fold)

    rchunk = lambda w, c: pl.BlockSpec((Q, w), lambda i: (nc - 1 - i, c))
    return pl.pallas_call(
        body, name=name, grid=(nc,),
        out_shape=[_sds((L, D_SSD), BF16), _sds((L, LANE), BF16), _sds((L, D_XBC), F32), _sds((8, D), F32), _sds((8, LANE), F32)],
        in_specs=[rchunk(D, 0), rchunk(D, 0), rchunk(D_XBC, 0), rchunk(D, 0), rchunk(D, 1),
                  pl.BlockSpec((1, N_STATE, D_SSD), lambda i: (nc - 1 - i, 0, 0))] + [_const_spec((1, D))] * 4,
        out_specs=[rchunk(D, 0), rchunk(LANE, 0), rchunk(D_XBC, 0), _const_spec((8, D)), _const_spec((8, LANE))],
        scratch_shapes=[pltpu.VMEM((N_STATE, D_SSD), F32), pltpu.VMEM((Q, D_SSD), F32), pltpu.VMEM((LANE, Q), F32),
                        pltpu.VMEM((Q, D_SSD), F32), pltpu.VMEM((Q, D_SSD), F32)],
        compiler_params=_cparams(("arbitrary",)))(dmixin, y, xbc, proj, proj, s_prev_all, dtb_f, alog_f, dsk_f, snw)


def adamw(w, g, m, v, name):
    rows, cols = w.shape
    tr = rows
    for t in (512, 256, 128, 64, 32, 16, 8):
        if rows % t == 0:
            tr = t
            break
    c1 = 1.0 - ADAM_B1 ** ADAM_STEP
    c2 = 1.0 - ADAM_B2 ** ADAM_STEP

    def body(w_ref, g_ref, m_ref, v_ref, d_ref, mo_ref, vo_ref):
        g_v = g_ref[...]
        m_n = ADAM_B1 * m_ref[...] + (1.0 - ADAM_B1) * g_v
        v_n = ADAM_B2 * v_ref[...] + (1.0 - ADAM_B2) * jnp.square(g_v)
        mo_ref[...] = m_n
        vo_ref[...] = v_n
        d_ref[...] = -ADAM_LR * ((m_n / c1) / (jnp.sqrt(v_n / c2) + ADAM_EPS) + ADAM_WD * w_ref[...])

    spec = pl.BlockSpec((tr, cols), lambda i: (i, 0))
    return pl.pallas_call(body, name=name, grid=(rows // tr,), out_shape=[_sds((rows, cols), F32)] * 3,
                          in_specs=[spec] * 4, out_specs=[spec] * 3, compiler_params=_cparams(("parallel",)))(w, g, m, v)


def sum_slots(parts, name, out_dtype=F32):
    _, rows, cols = parts.shape
    tr = rows
    for t in (1008, 512, 296, 256, 128, 64, 32, 16, 8):
        if rows % t == 0:
            tr = t
            break

    def body(p_ref, o_ref):
        acc = p_ref[0].astype(F32)
        for s in range(1, N_DEV):
            acc = acc + p_ref[s].astype(F32)
        o_ref[...] = acc.astype(out_dtype)

    return pl.pallas_call(body, name=name, grid=(rows // tr,), out_shape=_sds((rows, cols), out_dtype),
                          in_specs=[pl.BlockSpec((N_DEV, tr, cols), lambda i: (0, i, 0))],
                          out_specs=pl.BlockSpec((tr, cols), lambda i: (i, 0)),
                          compiler_params=_cparams(("parallel",)))(parts)


def ada_mod(c_all, ada_w_shard, ada_b_cols, name):
    def body(c_ref, w_ref, b_ref, o_ref, ca_ref):
        ca = _silu(c_ref[...])
        ca_ref[...] = ca
        o_ref[...] = _dot(ca, w_ref[...]) + b_ref[...]

    vm = pl.BlockSpec(memory_space=pltpu.VMEM)
    return pl.pallas_call(body, name=name, out_shape=[_sds((N_DEV, ada_w_shard.shape[1]), F32), _sds((N_DEV, D), F32)],
                          in_specs=[vm, vm, vm], out_specs=[vm, vm], compiler_params=_cparams())(c_all, ada_w_shard, ada_b_cols)


def ada_wgrad(c_act_all, dmod_cols, name):
    def body(c_ref, d_ref, o_ref):
        o_ref[...] = _dot_tn_hi(c_ref[...], d_ref[...])

    vm = pl.BlockSpec(memory_space=pltpu.VMEM)
    return pl.pallas_call(body, name=name, out_shape=_sds((D, dmod_cols.shape[1]), F32), in_specs=[vm, vm], out_specs=vm,
                          compiler_params=_cparams())(c_act_all, dmod_cols)


def exchange(src, name, gather):
    shape = src.shape if gather else src.shape[1:]

    def body(src_ref, out_ref, send_sems, recv_sems, local_sem):
        x, y, c = lax.axis_index("x"), lax.axis_index("y"), lax.axis_index("c")
        me = 4 * x + 2 * y + c

        def peer(k):
            bx, by, bc = (k >> 2) & 1, (k >> 1) & 1, k & 1
            px, py, pc = (x + bx) % 2, (y + by) % 2, (c + bc) % 2
            return (px, py, pc), 4 * px + 2 * py + pc

        def copy(k):
            dev, idx = peer(k)
            return pltpu.make_async_remote_copy(
                src_ref=src_ref if gather else src_ref.at[idx], dst_ref=out_ref.at[me],
                send_sem=send_sems.at[k - 1], recv_sem=recv_sems.at[k - 1],
                device_id=dev, device_id_type=pl.DeviceIdType.MESH)

        mine = pltpu.make_async_copy(src_ref if gather else src_ref.at[me], out_ref.at[me], local_sem)
        mine.start()
        sends = [copy(k) for k in range(1, N_DEV)]
        for cp in sends:
            cp.start()
        for k in range(1, N_DEV):
            dev, idx = peer(k)
            pltpu.make_async_remote_copy(
                src_ref=src_ref if gather else src_ref.at[idx], dst_ref=out_ref.at[idx],
                send_sem=send_sems.at[k - 1], recv_sem=recv_sems.at[k - 1],
                device_id=dev, device_id_type=pl.DeviceIdType.MESH).wait_recv()
        for cp in sends:
            cp.wait_send()
        mine.wait()

    hbm = pl.BlockSpec(memory_space=pl.ANY)
    return pl.pallas_call(
        body, name=name, out_shape=_sds((N_DEV,) + tuple(shape), src.dtype), in_specs=[hbm], out_specs=hbm,
        scratch_shapes=[pltpu.SemaphoreType.DMA((N_DEV - 1,)), pltpu.SemaphoreType.DMA((N_DEV - 1,)), pltpu.SemaphoreType.DMA],
        compiler_params=pltpu.CompilerParams(has_side_effects=True))(src)


def _pad128(n):
    return -(-n // LANE) * LANE


def _pack(arrays, rows_multiple=8):
    flat = []
    for a in arrays:
        f = a.reshape(-1)
        flat.append(jnp.pad(f, (0, _pad128(f.shape[0]) - f.shape[0])))
    v = jnp.concatenate(flat)
    rows = -(-v.shape[0] // LANE)
    rows_p = -(-rows // rows_multiple) * rows_multiple
    v = jnp.pad(v, (0, rows_p * LANE - v.shape[0]))
    return v.reshape(rows_p, LANE)


def _unpack(packed, shapes):
    v = packed.reshape(-1)
    out, off = [], 0
    for s in shapes:
        n = math.prod(s)
        out.append(v[off:off + n].reshape(s))
        off += _pad128(n)
    return out


def _to_flat_shards(g, col_sharded):
    r, c = g.shape
    if col_sharded:
        g = g.reshape(r, N_DEV, c // N_DEV).transpose(1, 0, 2)
    return g.reshape(N_DEV, r * c // N_DEV // LANE, LANE)


def _from_flat_shards(flat, r, c, col_sharded):
    if col_sharded:
        return flat.reshape(N_DEV, r, c // N_DEV).transpose(1, 0, 2).reshape(r, c)
    return flat.reshape(r, c)


def _rep_heads(v):
    return jnp.repeat(v.reshape(N_HEADS), HEAD).reshape(1, D_SSD)


def local_fwd_bwd(x, target, mod, wts, small):
    n1w, n2w, fnw = small["norm1_w"], small["norm2_w"], small["final_norm_w"]
    dtb_f, alog_f, dsk_f = _rep_heads(small["dt_bias"]), _rep_heads(small["a_log"]), _rep_heads(small["d_skip"])
    snw = small["ssd_norm_w"]

    h1 = norm_mod(x, mod, n1w, 0, "norm1")
    proj = mm_nn(h1, wts["w_in_p"], "in_proj")
    xbc = conv_silu_fwd(proj, 4096 // CB, D_XBC, small["ssd_conv_w"], small["ssd_conv_b"], "ssd_conv")
    y, ysn, s_prev = ssd_fwd(xbc, proj, dtb_f, alog_f, dsk_f, snw, "ssd_scan")
    uc = conf_conv_fwd(proj, 2048 // CB, 3072 // CB, small["conf_conv_w"], small["conf_conv_b"], "conf_conv")
    u = ln_silu(uc, small["conf_ln_w"], small["conf_ln_b"], "conf_ln")
    mixin = jnp.concatenate([ysn, u], axis=1)
    mix = mm_nn(mixin, wts["w_out"], "out_proj")
    h2, x1 = norm_mod(x, mod, n2w, 3, "norm2", res=mix, gate_row=2)
    up = mm_nn(h2, wts["w_up"], "up_proj")
    act = ffn_conv_fwd(up, small["ffn_conv_w"], small["ffn_conv_b"], "ffn_conv")
    ff = mm_nn(act, wts["w_down"], "down_proj")
    dx2, dff, acc_f = final_loss(ff, x1, mod, fnw, target, "final_loss")

    g_w_down = mm_tn(act, dff, "wgrad_down")
    dact = mm_nt([(dff, wts["w_down"])], "dact")
    dupg, dupv, dwg, dwv, dbg, dbv = ffn_conv_bwd(up, small["ffn_conv_w"], small["ffn_conv_b"], dact, "ffn_conv_bwd")
    dup = jnp.concatenate([dupg, dupv], axis=1)
    g_w_up = mm_tn(h2, dup, "wgrad_up")
    dh2 = mm_nt([(dup, wts["w_up"])], "dh2")
    dx1, dmix, acc_2 = norm_mod_bwd(dh2, x1, dx2, mod, n2w, 3, "norm2_bwd", mix=mix, gate_row=2)

    g_w_out = mm_tn(mixin, dmix, "wgrad_out")
    dmixin = mm_nt([(dmix, wts["w_out"])], "dmixin")
    duc, acc_ln = ln_silu_bwd(dmixin, uc, small["conf_ln_w"], small["conf_ln_b"], "conf_ln_bwd")
    dcfa, dcfg, dw_cc, db_cc = conf_conv_bwd(proj, 2048 // CB, 3072 // CB, small["conf_conv_w"], duc, "conf_conv_bwd")
    dz, ddt, dxbc_post, acc_s, acc_s16 = ssd_bwd(dmixin, y, xbc, proj, s_prev, dtb_f, alog_f, dsk_f, snw, "ssd_scan_bwd")
    dxbc, dw_sc, db_sc = conv_silu_bwd(proj, 4096 // CB, D_XBC, small["ssd_conv_w"], small["ssd_conv_b"], dxbc_post, "ssd_conv_bwd")
    dh1 = mm_nt([(dz, wts["w_z"]), (ddt, wts["w_dt16"]), (dcfa, wts["w_cfa"]), (dcfg, wts["w_cfg"]), (dxbc, wts["w_xbc"])], "dh1")
    grad_x, acc_1 = norm_mod_bwd(dh1, x, dx1, mod, n1w, 0, "norm1_bwd")
    g_in = mm_tn(h1, jnp.concatenate([dz, dxbc, dcfa, dcfg], axis=1), "wgrad_in")
    g_dt = mm_tn(h1, ddt, "wgrad_in_dt")
    g_w_in = jnp.concatenate([g_in[:, :2560], g_dt[:, :N_HEADS], g_in[:, 2560:]], axis=1)

    dmod = jnp.concatenate([acc_1[0:2], acc_2[3:4], acc_2[0:2], acc_f[1:2]], axis=0)
    small_grads = [
        dmod, acc_1[2:3], dw_sc[0:K_SSD], db_sc, acc_s16[1:2], acc_s16[2:3], acc_s16[3:4], acc_s[0:1], dw_cc[0:K_CONF], db_cc,
        acc_ln[0:1], acc_ln[1:2], acc_2[2:3], jnp.concatenate([dwg[0:K_FFN], dwv[0:K_FFN]], axis=1),
        jnp.concatenate([dbg, dbv], axis=1), acc_f[0:1],
    ]
    big_grads = {"w_in": g_w_in, "w_out": g_w_out, "w_up": g_w_up, "w_down": g_w_down}
    return acc_f[2:3], grad_x, big_grads, small_grads


BIG = (("w_in", D, 4624, True), ("w_out", 2048, D, False), ("w_up", D, 2 * D_FF, True), ("w_down", D_FF, D, False))
SMALL_GRAD_SHAPES = ((6, D), (1, D), (K_SSD, D_XBC), (1, D_XBC), (1, LANE), (1, LANE), (1, LANE), (1, D), (K_CONF, D_CONF),
                     (1, D_CONF), (1, D), (1, D), (1, D), (K_FFN, 2 * D_FF), (1, 2 * D_FF), (1, D))


def kernel(x, c, ada_w, ada_b, norm1_w, w_in, ssd_conv_w, ssd_conv_b, dt_bias, a_log, d_skip, ssd_norm_w, conf_conv_w, conf_conv_b, conf_ln_w, conf_ln_b, w_out, norm2_w, w_up, ffn_conv_w, ffn_conv_b, w_down, final_norm_w, loss_target, m_ada_w, m_ada_b, m_norm1_w, m_w_in, m_ssd_conv_w, m_ssd_conv_b, m_dt_bias, m_a_log, m_d_skip, m_ssd_norm_w, m_conf_conv_w, m_conf_conv_b, m_conf_ln_w, m_conf_ln_b, m_w_out, m_norm2_w, m_w_up, m_ffn_conv_w, m_ffn_conv_b, m_w_down, m_final_norm_w, v_ada_w, v_ada_b, v_norm1_w, v_w_in, v_ssd_conv_w, v_ssd_conv_b, v_dt_bias, v_a_log, v_d_skip, v_ssd_norm_w, v_conf_conv_w, v_conf_conv_b, v_conf_ln_w, v_conf_ln_b, v_w_out, v_norm2_w, v_w_up, v_ffn_conv_w, v_ffn_conv_b, v_w_down, v_final_norm_w):
    me = 4 * lax.axis_index("x") + 2 * lax.axis_index("y") + lax.axis_index("c")
    weights = dict(ada_w=ada_w, ada_b=ada_b, norm1_w=norm1_w, w_in=w_in, ssd_conv_w=ssd_conv_w, ssd_conv_b=ssd_conv_b,
                   dt_bias=dt_bias, a_log=a_log, d_skip=d_skip, ssd_norm_w=ssd_norm_w, conf_conv_w=conf_conv_w,
                   conf_conv_b=conf_conv_b, conf_ln_w=conf_ln_w, conf_ln_b=conf_ln_b, w_out=w_out, norm2_w=norm2_w, w_up=w_up,
                   ffn_conv_w=ffn_conv_w, ffn_conv_b=ffn_conv_b, w_down=w_down, final_norm_w=final_norm_w)
    moms_m = dict(ada_w=m_ada_w, ada_b=m_ada_b, norm1_w=m_norm1_w, w_in=m_w_in, ssd_conv_w=m_ssd_conv_w, ssd_conv_b=m_ssd_conv_b,
                  dt_bias=m_dt_bias, a_log=m_a_log, d_skip=m_d_skip, ssd_norm_w=m_ssd_norm_w, conf_conv_w=m_conf_conv_w,
                  conf_conv_b=m_conf_conv_b, conf_ln_w=m_conf_ln_w, conf_ln_b=m_conf_ln_b, w_out=m_w_out, norm2_w=m_norm2_w,
                  w_up=m_w_up, ffn_conv_w=m_ffn_conv_w, ffn_conv_b=m_ffn_conv_b, w_down=m_w_down, final_norm_w=m_final_norm_w)
    moms_v = dict(ada_w=v_ada_w, ada_b=v_ada_b, norm1_w=v_norm1_w, w_in=v_w_in, ssd_conv_w=v_ssd_conv_w, ssd_conv_b=v_ssd_conv_b,
                  dt_bias=v_dt_bias, a_log=v_a_log, d_skip=v_d_skip, ssd_norm_w=v_ssd_norm_w, conf_conv_w=v_conf_conv_w,
                  conf_conv_b=v_conf_conv_b, conf_ln_w=v_conf_ln_w, conf_ln_b=v_conf_ln_b, w_out=v_w_out, norm2_w=v_norm2_w,
                  w_up=v_w_up, ffn_conv_w=v_ffn_conv_w, ffn_conv_b=v_ffn_conv_b, w_down=v_w_down, final_norm_w=v_final_norm_w)
    names = list(weights)

    c_all = exchange(c.reshape(8, LANE), "gather_c", gather=True).reshape(N_DEV, D)
    ada_cols = ada_w.shape[2]
    ada_b_cols = lax.dynamic_slice(ada_b, (0, me * ada_cols), (1, ada_cols))
    mod_cols, c_act_all = ada_mod(c_all, ada_w[0], ada_b_cols, "ada_mod")
    mod_parts = exchange(jnp.pad(mod_cols, ((0, 0), (0, D - ada_cols))).reshape(N_DEV, 8, LANE), "scatter_mod", gather=False)
    mod = mod_parts.reshape(N_DEV, D)[:, :ada_cols].reshape(6, D)
    mod = jnp.pad(mod, ((0, 2), (0, 0)))

    shard = _pack([weights[n][0].astype(BF16) for n, _, _, _ in BIG], rows_multiple=16)
    gathered = exchange(shard, "gather_weights", gather=True)
    full, off = {}, 0
    for n, r, cc, col_sharded in BIG:
        rows = r * cc // N_DEV // LANE
        full[n] = _from_flat_shards(gathered[:, off:off + rows], r, cc, col_sharded)
        off += rows
    w_in_f = full["w_in"]
    w_z, w_xbc, w_dt, w_cfa, w_cfg = (w_in_f[:, :1024], w_in_f[:, 1024:2560], w_in_f[:, 2560:2576], w_in_f[:, 2576:3600],
                                      w_in_f[:, 3600:])
    wts = dict(
        w_in_p=jnp.concatenate([w_z, jnp.repeat(w_dt, HEAD, axis=1), w_cfa, w_cfg, w_xbc], axis=1),
        w_z=w_z, w_xbc=w_xbc, w_dt16=jnp.pad(w_dt, ((0, 0), (0, LANE - N_HEADS))), w_cfa=w_cfa, w_cfg=w_cfg,
        w_out=full["w_out"], w_up=full["w_up"], w_down=full["w_down"])
    small = {n: (weights[n][0] if weights[n].ndim == 3 else weights[n].reshape(1, -1)) for n in names
             if n not in ("ada_w", "w_in", "w_out", "w_up", "w_down")}
    small["ssd_conv_w"] = None
    small["conf_conv_w"] = None
    small["ffn_conv_w"] = None
    conv_shards = _pack([ssd_conv_w[0], conf_conv_w[0], ffn_conv_w[0]])
    conv_all = exchange(conv_shards, "gather_conv_w", gather=True)
    sc_l, cc_l, fc_l = [], [], []
    for p in range(N_DEV):
        a, b_, c_ = _unpack(conv_all[p], [(K_SSD, D_XBC // N_DEV), (K_CONF, D_CONF // N_DEV), (K_FFN, 2 * D_FF // N_DEV)])
        sc_l.append(a)
        cc_l.append(b_)
        fc_l.append(c_)
    small["ssd_conv_w"] = jnp.concatenate(sc_l, axis=1)
    small["conf_conv_w"] = jnp.concatenate(cc_l, axis=1)
    small["ffn_conv_w"] = jnp.concatenate(fc_l, axis=1)

    sq_err, grad_x, big_grads, small_grads = local_fwd_bwd(x[0], loss_target[0], mod, wts, small)
    loss = lax.psum(0.5 / D * jnp.sum(sq_err), ("x", "y", "c"))

    send = jnp.concatenate([_to_flat_shards(big_grads[n], col_sharded) for n, _, _, col_sharded in BIG], axis=1).astype(BF16)
    landed = exchange(send, "scatter_grads", gather=False)
    g_flat = sum_slots(landed, "sum_grads")
    small_all = exchange(_pack(small_grads), "gather_small_grads", gather=True)
    small_red = _unpack(sum_slots(small_all, "sum_small_grads"), SMALL_GRAD_SHAPES)
    (g_adab, g_n1w, g_scw, g_scb, g_dtb, g_alog, g_dsk, g_snw, g_ccw, g_ccb, g_lnw, g_lnb, g_n2w, g_fcw, g_fcb, g_fnw) = small_red
    dmod_all = small_all[:, :6 * D // LANE].reshape(N_DEV, 6 * D)
    g_ada_w = ada_wgrad(c_act_all, lax.dynamic_slice(dmod_all, (0, me * ada_cols), (N_DEV, ada_cols)), "ada_wgrad")

    grads = {
        "ada_w": g_ada_w[None], "ada_b": g_adab.reshape(1, 6 * D), "norm1_w": g_n1w,
        "ssd_conv_w": lax.dynamic_slice(g_scw, (0, me * (D_XBC // N_DEV)), (K_SSD, D_XBC // N_DEV))[None],
        "ssd_conv_b": g_scb, "dt_bias": g_dtb[:, :N_HEADS], "a_log": g_alog[:, :N_HEADS], "d_skip": g_dsk[:, :N_HEADS],
        "ssd_norm_w": g_snw,
        "conf_conv_w": lax.dynamic_slice(g_ccw, (0, me * (D_CONF // N_DEV)), (K_CONF, D_CONF // N_DEV))[None],
        "conf_conv_b": g_ccb, "conf_ln_w": g_lnw, "conf_ln_b": g_lnb, "norm2_w": g_n2w,
        "ffn_conv_w": lax.dynamic_slice(g_fcw, (0, me * (2 * D_FF // N_DEV)), (K_FFN, 2 * D_FF // N_DEV))[None],
        "ffn_conv_b": g_fcb, "final_norm_w": g_fnw.reshape(D),
    }
    off = 0
    for n, r, cc, col_sharded in BIG:
        rows = r * cc // N_DEV // LANE
        grads[n] = g_flat[off:off + rows].reshape(weights[n].shape)
        off += rows

    delta, new_m, new_v = {}, {}, {}
    for n in ("ada_w", "w_in", "w_out", "w_up", "w_down"):
        shp = weights[n].shape
        d_, m_, v_ = adamw(weights[n][0], grads[n][0], moms_m[n][0], moms_v[n][0], "adamw_" + n)
        delta[n], new_m[n], new_v[n] = d_.reshape(shp), m_.reshape(shp), v_.reshape(shp)
    rest = [n for n in names if n not in delta]
    shapes = [weights[n].shape for n in rest]
    d_, m_, v_ = adamw(_pack([weights[n] for n in rest]), _pack([grads[n] for n in rest]), _pack([moms_m[n] for n in rest]),
                       _pack([moms_v[n] for n in rest]), "adamw_small")
    for n, dd, mm, vv in zip(rest, _unpack(d_, shapes), _unpack(m_, shapes), _unpack(v_, shapes)):
        delta[n], new_m[n], new_v[n] = dd, mm, vv
    grads = {n: grads[n].reshape(weights[n].shape) for n in names}
    return (loss, grad_x[None], *[grads[n] for n in names], *[delta[n] for n in names], *[new_m[n] for n in names],
            *[new_v[n] for n in names])
```

```python
import functools

import jax
import jax.numpy as jnp
from jax import lax
from jax.experimental import pallas as pl
from jax.experimental.pallas import tpu as pltpu

F32 = jnp.float32
BF16 = jnp.bfloat16
HI = lax.Precision.HIGHEST

N_DEV = 8
D = 1024
D_SSD = 1024
HEAD = 64
N_HEADS = 16
N_STATE = 128
D_XBC = 1536
D_CONF = 1024
D_FF = 2816
K_SSD, K_CONF, K_FFN = 4, 31, 3
D_INP = 5632
LANE = 128
TR = 256
Q = 256
CB = 256
VMEM_LIMIT = 56 * 1024 * 1024

ADAM_LR, ADAM_B1, ADAM_B2, ADAM_EPS, ADAM_WD, ADAM_STEP = 0.001, 0.9, 0.999, 1e-08, 0.01, 10


def _cparams(sem=None):
    return pltpu.CompilerParams(vmem_limit_bytes=VMEM_LIMIT, dimension_semantics=sem)


def _sds(shape, dtype):
    return jax.ShapeDtypeStruct(shape, dtype)


def _sigmoid(x):
    return 1.0 / (1.0 + jnp.exp(-x))


def _silu(x):
    return x * _sigmoid(x)


def _dsilu(x):
    s = _sigmoid(x)
    return s * (1.0 + x * (1.0 - s))


def _softplus(x):
    return jnp.maximum(x, 0.0) + jnp.log(1.0 + jnp.exp(-jnp.abs(x)))


def _dot(a, b):
    return jnp.dot(a.astype(BF16), b.astype(BF16), preferred_element_type=F32)


def _dot_nt(a, b):
    return lax.dot_general(a.astype(BF16), b.astype(BF16), (((1,), (1,)), ((), ())), preferred_element_type=F32)


def _dot_tn(a, b):
    return lax.dot_general(a.astype(BF16), b.astype(BF16), (((0,), (0,)), ((), ())), preferred_element_type=F32)


def _dot_hi(a, b):
    return jnp.dot(a, b, precision=HI, preferred_element_type=F32)


def _dot_tn_hi(a, b):
    return lax.dot_general(a, b, (((0,), (0,)), ((), ())), precision=HI, preferred_element_type=F32)


def _colsum(x):
    return jnp.sum(x, axis=0, keepdims=True)


def _const_spec(shape):
    return pl.BlockSpec(shape, lambda *_: (0,) * len(shape))


def _col_tile(n):
    for t in (1408, 1024, 768, 512, 256, 128):
        if n % t == 0 and t <= n:
            return t
    return n


def mm_nn(pairs, name):
    L = pairs[0][0].shape[0]
    N = pairs[0][1].shape[1]
    tn = _col_tile(N)
    n = len(pairs)

    def body(*refs):
        acc = None
        for p in range(n):
            t = jnp.dot(refs[2 * p][...], refs[2 * p + 1][...], preferred_element_type=F32)
            acc = t if acc is None else acc + t
        refs[-1][...] = acc

    in_specs, args = [], []
    for a, w, rb in pairs:
        in_specs += [pl.BlockSpec((TR, a.shape[1]), lambda j, i: (i, 0)),
                     pl.BlockSpec((a.shape[1], tn), functools.partial(lambda j, i, rb: (rb, j), rb=rb))]
        args += [a, w]
    return pl.pallas_call(
        body, name=name, grid=(N // tn, L // TR), out_shape=_sds((L, N), F32), in_specs=in_specs,
        out_specs=pl.BlockSpec((TR, tn), lambda j, i: (i, j)),
        compiler_params=_cparams(("parallel", "parallel")))(*args)


def mm_nt(pairs, name):
    L = pairs[0][0].shape[0]
    K = pairs[0][1].shape[0]
    tk = _col_tile(K)
    n = len(pairs)

    def body(*refs):
        o_ref = refs[-1]
        acc = None
        for p in range(n):
            t = lax.dot_general(refs[2 * p][...], refs[2 * p + 1][...], (((1,), (1,)), ((), ())),
                                preferred_element_type=F32)
            acc = t if acc is None else acc + t
        o_ref[...] = acc

    in_specs, args = [], []
    for a, w, cb in pairs:
        in_specs += [pl.BlockSpec((TR, a.shape[1]), lambda j, i: (i, 0)),
                     pl.BlockSpec((tk, a.shape[1]), functools.partial(lambda j, i, cb: (j, cb), cb=cb))]
        args += [a, w]
    return pl.pallas_call(
        body, name=name, grid=(K // tk, L // TR), out_shape=_sds((L, K), F32), in_specs=in_specs,
        out_specs=pl.BlockSpec((TR, tk), lambda j, i: (i, j)),
        compiler_params=_cparams(("parallel", "parallel")))(*args)


def mm_tn(a, g, name):
    L, M = a.shape
    N = g.shape[1]
    tn = _col_tile(N) if N > 1024 else N
    if M * tn * 4 > 8 * 1024 * 1024:
        tn = 512
    tl = 512 if L % 512 == 0 else TR

    def body(a_ref, g_ref, o_ref):
        @pl.when(pl.program_id(1) == 0)
        def _():
            o_ref[...] = jnp.zeros((M, tn), F32)

        o_ref[...] += lax.dot_general(a_ref[...], g_ref[...], (((0,), (0,)), ((), ())), preferred_element_type=F32)

    return pl.pallas_call(
        body, name=name, grid=(N // tn, L // tl), out_shape=_sds((M, N), F32),
        in_specs=[pl.BlockSpec((tl, M), lambda j, l: (l, 0)), pl.BlockSpec((tl, tn), lambda j, l: (l, j))],
        out_specs=pl.BlockSpec((M, tn), lambda j, l: (0, j)),
        compiler_params=_cparams(("parallel", "arbitrary")))(a, g)


def _row_spec(width=D):
    return pl.BlockSpec((TR, width), lambda i: (i, 0))


def _row_col_spec(width, col):
    return pl.BlockSpec((TR, width), lambda i: (i, col))


def norm_mod(x, mod, w, shift_row, name, res=None, gate_row=None):
    L = x.shape[0]
    has_res = res is not None

    def body(*refs):
        if has_res:
            x_ref, res_ref, mod_ref, w_ref, h_ref, xo_ref = refs
            xin = x_ref[...] + mod_ref[gate_row:gate_row + 1, :] * res_ref[...]
            xo_ref[...] = xin
        else:
            x_ref, mod_ref, w_ref, h_ref = refs
            xin = x_ref[...]
        r = lax.rsqrt(jnp.mean(xin * xin, axis=-1, keepdims=True) + 1e-6)
        h = (xin * r * w_ref[...]) * (1.0 + mod_ref[shift_row + 1:shift_row + 2, :]) + mod_ref[shift_row:shift_row + 1, :]
        h_ref[...] = h.astype(BF16)

    ins = [x] + ([res] if has_res else []) + [mod, w]
    in_specs = [_row_spec()] + ([_row_spec()] if has_res else []) + [_const_spec((8, D)), _const_spec((1, D))]
    out_shape = [_sds((L, D), BF16)] + ([_sds((L, D), F32)] if has_res else [])
    out_specs = [_row_spec()] + ([_row_spec()] if has_res else [])
    out = pl.pallas_call(body, name=name, grid=(L // TR,), out_shape=out_shape, in_specs=in_specs,
                         out_specs=out_specs, compiler_params=_cparams(("parallel",)))(*ins)
    return out if has_res else out[0]


def ln_silu(uc, lnw, lnb, name):
    L = uc.shape[0]

    def body(u_ref, w_ref, b_ref, o_ref):
        u = u_ref[...]
        mu = jnp.mean(u, axis=-1, keepdims=True)
        var = jnp.mean(jnp.square(u - mu), axis=-1, keepdims=True)
        v = (u - mu) * lax.rsqrt(var + 1e-5) * w_ref[...] + b_ref[...]
        o_ref[...] = _silu(v).astype(BF16)

    return pl.pallas_call(body, name=name, grid=(L // TR,), out_shape=_sds((L, D_CONF), BF16),
                          in_specs=[_row_spec(), _const_spec((1, D)), _const_spec((1, D))], out_specs=_row_spec(),
                          compiler_params=_cparams(("parallel",)))(uc, lnw, lnb)


def ln_silu_bwd(du_all, uc, lnw, lnb, name):
    L = uc.shape[0]

    def body(du_ref, u_ref, w_ref, b_ref, o_ref, acc_ref):
        @pl.when(pl.program_id(0) == 0)
        def _():
            acc_ref[...] = jnp.zeros((8, D), F32)

        u = u_ref[...]
        mu = jnp.mean(u, axis=-1, keepdims=True)
        rl = lax.rsqrt(jnp.mean(jnp.square(u - mu), axis=-1, keepdims=True) + 1e-5)
        n = (u - mu) * rl
        v = n * w_ref[...] + b_ref[...]
        dv = du_ref[...] * _dsilu(v)
        acc_ref[0:1, :] += _colsum(dv * n)
        acc_ref[1:2, :] += _colsum(dv)
        dn = dv * w_ref[...]
        o_ref[...] = rl * (dn - jnp.mean(dn, axis=-1, keepdims=True) - n * jnp.mean(dn * n, axis=-1, keepdims=True))

    return pl.pallas_call(body, name=name, grid=(L // TR,), out_shape=[_sds((L, D), F32), _sds((8, D), F32)],
                          in_specs=[_row_col_spec(D, 1), _row_spec(), _const_spec((1, D)), _const_spec((1, D))],
                          out_specs=[_row_spec(), _const_spec((8, D))],
                          compiler_params=_cparams(("arbitrary",)))(du_all, uc, lnw, lnb)


def final_loss(ff, x1, mod, fw, target, name):
    L = ff.shape[0]

    def body(ff_ref, x1_ref, mod_ref, fw_ref, t_ref, dx_ref, dff_ref, acc_ref):
        @pl.when(pl.program_id(0) == 0)
        def _():
            acc_ref[...] = jnp.zeros((8, D), F32)

        ff_v = ff_ref[...]
        g2 = mod_ref[5:6, :]
        x2 = x1_ref[...] + g2 * ff_v
        r = lax.rsqrt(jnp.mean(x2 * x2, axis=-1, keepdims=True) + 1e-6)
        n = x2 * r
        err = n * fw_ref[...] - t_ref[...]
        dy = err * (1.0 / D)
        dn = dy * fw_ref[...]
        dx2 = r * (dn - n * jnp.mean(dn * n, axis=-1, keepdims=True))
        acc_ref[0:1, :] += _colsum(dy * n)
        acc_ref[1:2, :] += _colsum(dx2 * ff_v)
        acc_ref[2:3, :] += _colsum(err * err)
        dx_ref[...] = dx2
        dff_ref[...] = (dx2 * g2).astype(BF16)

    return pl.pallas_call(
        body, name=name, grid=(L // TR,), out_shape=[_sds((L, D), F32), _sds((L, D), BF16), _sds((8, D), F32)],
        in_specs=[_row_spec(), _row_spec(), _const_spec((8, D)), _const_spec((1, D)), _row_spec()],
        out_specs=[_row_spec(), _row_spec(), _const_spec((8, D))],
        compiler_params=_cparams(("arbitrary",)))(ff, x1, mod, fw, target)


def norm_mod_bwd(dh, xin, dres, mod, w, shift_row, name, mix=None, gate_row=None):
    L = dh.shape[0]
    has_mix = mix is not None

    def body(*refs):
        if has_mix:
            dh_ref, x_ref, dres_ref, mod_ref, w_ref, mix_ref, dx_ref, dmix_ref, acc_ref = refs
        else:
            dh_ref, x_ref, dres_ref, mod_ref, w_ref, dx_ref, acc_ref = refs

        @pl.when(pl.program_id(0) == 0)
        def _():
            acc_ref[...] = jnp.zeros((8, D), F32)

        dh_v = dh_ref[...]
        x = x_ref[...]
        r = lax.rsqrt(jnp.mean(x * x, axis=-1, keepdims=True) + 1e-6)
        n = x * r
        nw = n * w_ref[...]
        sc1 = 1.0 + mod_ref[shift_row + 1:shift_row + 2, :]
        acc_ref[0:1, :] += _colsum(dh_v)
        acc_ref[1:2, :] += _colsum(dh_v * nw)
        dnw = dh_v * sc1
        acc_ref[2:3, :] += _colsum(dnw * n)
        dn = dnw * w_ref[...]
        dx = r * (dn - n * jnp.mean(dn * n, axis=-1, keepdims=True)) + dres_ref[...]
        dx_ref[...] = dx
        if has_mix:
            acc_ref[3:4, :] += _colsum(dx * mix_ref[...])
            dmix_ref[...] = (dx * mod_ref[gate_row:gate_row + 1, :]).astype(BF16)

    ins = [dh, xin, dres, mod, w] + ([mix] if has_mix else [])
    in_specs = [_row_spec(), _row_spec(), _row_spec(), _const_spec((8, D)), _const_spec((1, D))] + ([_row_spec()] if has_mix else [])
    out_shape = [_sds((L, D), F32)] + ([_sds((L, D), BF16)] if has_mix else []) + [_sds((8, D), F32)]
    out_specs = [_row_spec()] + ([_row_spec()] if has_mix else []) + [_const_spec((8, D))]
    return pl.pallas_call(body, name=name, grid=(L // TR,), out_shape=out_shape, in_specs=in_specs,
                          out_specs=out_specs, compiler_params=_cparams(("arbitrary",)))(*ins)


def _halo(k):
    return 8 if k <= 9 else 32


def _prev_spec(h, col0):
    return pl.BlockSpec((h, CB), lambda j, i: (jnp.maximum(i * (TR // h) - 1, 0), j + col0))


def _next_spec(h, col0, n_tiles):
    return pl.BlockSpec((h, CB), lambda j, i: (jnp.minimum(i + 1, n_tiles - 1) * (TR // h), j + col0))


def _tile_spec(col0):
    return pl.BlockSpec((TR, CB), lambda j, i: (i, j + col0))


def _w_spec(kp, col0):
    return pl.BlockSpec((kp, CB), lambda j, i: (0, j + col0))


def _causal_taps(ext_ref, w_ref, k_taps, first, rows):
    acc = None
    for k in range(k_taps):
        t = w_ref[k:k + 1, :] * ext_ref[pl.ds(first - (k_taps - 1) + k, rows), :]
        acc = t if acc is None else acc + t
    return acc


def _anticausal_taps(d_ref, w_ref, k_taps, rows):
    acc = None
    for k in range(k_taps):
        t = w_ref[k:k + 1, :] * d_ref[pl.ds(k_taps - 1 - k, rows), :]
        acc = t if acc is None else acc + t
    return acc


def _acc_conv_wgrad(dw_ref, d_tile, ext_ref, k_taps, first):
    for k in range(k_taps):
        dw_ref[k:k + 1, :] += _colsum(d_tile * ext_ref[pl.ds(first - (k_taps - 1) + k, TR), :])
    dw_ref[k_taps:k_taps + 1, :] += _colsum(d_tile)


def conv_silu_fwd(x, col0, width, w, b, name):
    L = x.shape[0]
    k_taps = w.shape[0]
    h = _halo(k_taps)

    def body(xp_ref, x_ref, w_ref, b_ref, o_ref, ext_ref):
        i = pl.program_id(1)
        ext_ref[0:h, :] = jnp.where(i > 0, xp_ref[...], 0.0)
        ext_ref[h:h + TR, :] = x_ref[...]
        o_ref[...] = _silu(_causal_taps(ext_ref, w_ref, k_taps, h, TR) + b_ref[...])

    return pl.pallas_call(
        body, name=name, grid=(width // CB, L // TR), out_shape=_sds((L, width), F32),
        in_specs=[_prev_spec(h, col0), _tile_spec(col0), _w_spec(k_taps, 0), pl.BlockSpec((1, CB), lambda j, i: (0, j))],
        out_specs=_tile_spec(0), scratch_shapes=[pltpu.VMEM((h + TR, CB), F32)],
        compiler_params=_cparams(("parallel", "parallel")))(x, x, w, b)


def conv_silu_bwd(x, col0, width, w, b, dpost, name):
    L = x.shape[0]
    k_taps = w.shape[0]
    h = _halo(k_taps)
    nt = L // TR

    def body(xp_ref, x_ref, xn_ref, d_ref, dn_ref, w_ref, b_ref, dx_ref, dw_ref, ext_ref, dpre_ref):
        i = pl.program_id(1)

        @pl.when(i == 0)
        def _():
            dw_ref[...] = jnp.zeros((8, CB), F32)

        ext_ref[0:h, :] = jnp.where(i > 0, xp_ref[...], 0.0)
        ext_ref[h:h + TR, :] = x_ref[...]
        ext_ref[h + TR:h + TR + h, :] = xn_ref[...]
        pre = _causal_taps(ext_ref, w_ref, k_taps, h, TR + h) + b_ref[...]
        dpre_ref[0:TR, :] = d_ref[...] * _dsilu(pre[0:TR, :])
        dpre_ref[TR:TR + h, :] = jnp.where(i < nt - 1, dn_ref[...], 0.0) * _dsilu(pre[TR:TR + h, :])
        dx_ref[...] = _anticausal_taps(dpre_ref, w_ref, k_taps, TR).astype(BF16)
        _acc_conv_wgrad(dw_ref, dpre_ref[0:TR, :], ext_ref, k_taps, h)

    return pl.pallas_call(
        body, name=name, grid=(width // CB, nt),
        out_shape=[_sds((L, width), BF16), _sds((8, width), F32)],
        in_specs=[_prev_spec(h, col0), _tile_spec(col0), _next_spec(h, col0, nt), _tile_spec(0), _next_spec(h, 0, nt),
                  _w_spec(k_taps, 0), pl.BlockSpec((1, CB), lambda j, i: (0, j))],
        out_specs=[_tile_spec(0), _w_spec(8, 0)],
        scratch_shapes=[pltpu.VMEM((h + TR + h, CB), F32), pltpu.VMEM((TR + h, CB), F32)],
        compiler_params=_cparams(("parallel", "arbitrary")))(x, x, x, dpost, dpost, w, b)


def conf_conv_fwd(proj, col_a, col_g, w, b, name):
    L = proj.shape[0]
    k_taps = w.shape[0]
    h = _halo(k_taps)

    def body(ap_ref, a_ref, gp_ref, g_ref, w_ref, b_ref, o_ref, ext_ref):
        i = pl.program_id(1)
        ext_ref[0:h, :] = jnp.where(i > 0, ap_ref[...] * _sigmoid(gp_ref[...]), 0.0)
        ext_ref[h:h + TR, :] = a_ref[...] * _sigmoid(g_ref[...])
        o_ref[...] = _causal_taps(ext_ref, w_ref, k_taps, h, TR) + b_ref[...]

    return pl.pallas_call(
        body, name=name, grid=(D_CONF // CB, L // TR), out_shape=_sds((L, D_CONF), F32),
        in_specs=[_prev_spec(h, col_a), _tile_spec(col_a), _prev_spec(h, col_g), _tile_spec(col_g), _w_spec(k_taps, 0),
                  pl.BlockSpec((1, CB), lambda j, i: (0, j))],
        out_specs=_tile_spec(0), scratch_shapes=[pltpu.VMEM((h + TR, CB), F32)],
        compiler_params=_cparams(("parallel", "parallel")))(proj, proj, proj, proj, w, b)


def conf_conv_bwd(proj, col_a, col_g, w, duc, name):
    L = proj.shape[0]
    k_taps = w.shape[0]
    h = _halo(k_taps)
    nt = L // TR

    def body(ap_ref, a_ref, gp_ref, g_ref, d_ref, dn_ref, w_ref, da_ref, dg_ref, dw_ref, ext_ref, dext_ref):
        i = pl.program_id(1)

        @pl.when(i == 0)
        def _():
            dw_ref[...] = jnp.zeros((32, CB), F32)

        a = a_ref[...]
        s = _sigmoid(g_ref[...])
        ext_ref[0:h, :] = jnp.where(i > 0, ap_ref[...] * _sigmoid(gp_ref[...]), 0.0)
        ext_ref[h:h + TR, :] = a * s
        dext_ref[0:TR, :] = d_ref[...]
        dext_ref[TR:TR + h, :] = jnp.where(i < nt - 1, dn_ref[...], 0.0)
        du0 = _anticausal_taps(dext_ref, w_ref, k_taps, TR)
        da_ref[...] = (du0 * s).astype(BF16)
        dg_ref[...] = (du0 * a * s * (1.0 - s)).astype(BF16)
        _acc_conv_wgrad(dw_ref, d_ref[...], ext_ref, k_taps, h)

    return pl.pallas_call(
        body, name=name, grid=(D_CONF // CB, nt),
        out_shape=[_sds((L, D_CONF), BF16), _sds((L, D_CONF), BF16), _sds((32, D_CONF), F32)],
        in_specs=[_prev_spec(h, col_a), _tile_spec(col_a), _prev_spec(h, col_g), _tile_spec(col_g), _tile_spec(0),
                  _next_spec(h, 0, nt), _w_spec(k_taps, 0)],
        out_specs=[_tile_spec(0), _tile_spec(0), _w_spec(32, 0)],
        scratch_shapes=[pltpu.VMEM((h + TR, CB), F32), pltpu.VMEM((TR + h, CB), F32)],
        compiler_params=_cparams(("parallel", "arbitrary")))(proj, proj, proj, proj, duc, duc, w)


def ffn_conv_fwd(up, w, b, name):
    L = up.shape[0]
    k_taps = w.shape[0]
    h = _halo(k_taps)
    cv = D_FF // CB

    def body(gp_ref, g_ref, vp_ref, v_ref, wg_ref, wv_ref, bg_ref, bv_ref, o_ref, eg_ref, ev_ref):
        i = pl.program_id(1)
        eg_ref[0:h, :] = jnp.where(i > 0, gp_ref[...], 0.0)
        eg_ref[h:h + TR, :] = g_ref[...]
        ev_ref[0:h, :] = jnp.where(i > 0, vp_ref[...], 0.0)
        ev_ref[h:h + TR, :] = v_ref[...]
        pg = _causal_taps(eg_ref, wg_ref, k_taps, h, TR) + bg_ref[...]
        pv = _causal_taps(ev_ref, wv_ref, k_taps, h, TR) + bv_ref[...]
        o_ref[...] = (_silu(pg) * pv).astype(BF16)

    bspec = lambda c0: pl.BlockSpec((1, CB), lambda j, i: (0, j + c0))
    return pl.pallas_call(
        body, name=name, grid=(cv, L // TR), out_shape=_sds((L, D_FF), BF16),
        in_specs=[_prev_spec(h, 0), _tile_spec(0), _prev_spec(h, cv), _tile_spec(cv), _w_spec(k_taps, 0), _w_spec(k_taps, cv),
                  bspec(0), bspec(cv)],
        out_specs=_tile_spec(0), scratch_shapes=[pltpu.VMEM((h + TR, CB), F32), pltpu.VMEM((h + TR, CB), F32)],
        compiler_params=_cparams(("parallel", "parallel")))(up, up, up, up, w, w, b, b)


def ffn_conv_bwd(up, w, b, dact, name):
    L = up.shape[0]
    k_taps = w.shape[0]
    h = _halo(k_taps)
    nt = L // TR
    cv = D_FF // CB

    def body(gp_ref, g_ref, gn_ref, vp_ref, v_ref, vn_ref, d_ref, dn_ref, wg_ref, wv_ref, bg_ref, bv_ref,
             dg_ref, dv_ref, dwg_ref, dwv_ref, eg_ref, ev_ref, pg_ref, pv_ref):
        i = pl.program_id(1)

        @pl.when(i == 0)
        def _():
            dwg_ref[...] = jnp.zeros((8, CB), F32)
            dwv_ref[...] = jnp.zeros((8, CB), F32)

        for e_ref, p_ref, c_ref, n_ref in ((eg_ref, gp_ref, g_ref, gn_ref), (ev_ref, vp_ref, v_ref, vn_ref)):
            e_ref[0:h, :] = jnp.where(i > 0, p_ref[...], 0.0)
            e_ref[h:h + TR, :] = c_ref[...]
            e_ref[h + TR:h + TR + h, :] = n_ref[...]
        pg = _causal_taps(eg_ref, wg_ref, k_taps, h, TR + h) + bg_ref[...]
        pv = _causal_taps(ev_ref, wv_ref, k_taps, h, TR + h) + bv_ref[...]
        dact_t = d_ref[...]
        dact_n = jnp.where(i < nt - 1, dn_ref[...], 0.0)
        pg_ref[0:TR, :] = dact_t * pv[0:TR, :] * _dsilu(pg[0:TR, :])
        pg_ref[TR:TR + h, :] = dact_n * pv[TR:TR + h, :] * _dsilu(pg[TR:TR + h, :])
        pv_ref[0:TR, :] = dact_t * _silu(pg[0:TR, :])
        pv_ref[TR:TR + h, :] = dact_n * _silu(pg[TR:TR + h, :])
        dg_ref[...] = _anticausal_taps(pg_ref, wg_ref, k_taps, TR).astype(BF16)
        dv_ref[...] = _anticausal_taps(pv_ref, wv_ref, k_taps, TR).astype(BF16)
        _acc_conv_wgrad(dwg_ref, pg_ref[0:TR, :], eg_ref, k_taps, h)
        _acc_conv_wgrad(dwv_ref, pv_ref[0:TR, :], ev_ref, k_taps, h)

    bspec = lambda c0: pl.BlockSpec((1, CB), lambda j, i: (0, j + c0))
    ext = pltpu.VMEM((h + TR + h, CB), F32)
    dpre = pltpu.VMEM((TR + h, CB), F32)
    return pl.pallas_call(
        body, name=name, grid=(cv, nt),
        out_shape=[_sds((L, D_FF), BF16), _sds((L, D_FF), BF16), _sds((8, D_FF), F32), _sds((8, D_FF), F32)],
        in_specs=[_prev_spec(h, 0), _tile_spec(0), _next_spec(h, 0, nt), _prev_spec(h, cv), _tile_spec(cv), _next_spec(h, cv, nt),
                  _tile_spec(0), _next_spec(h, 0, nt), _w_spec(k_taps, 0), _w_spec(k_taps, cv), bspec(0), bspec(cv)],
        out_specs=[_tile_spec(0), _tile_spec(0), _w_spec(8, 0), _w_spec(8, 0)],
        scratch_shapes=[ext, ext, dpre, dpre],
        compiler_params=_cparams(("parallel", "arbitrary")))(up, up, up, up, up, up, dact, dact, w, w, b, b)


def _ssd_common(xbc_ref, dt_ref, dtb_ref, alog_ref, cs_ref):
    xs = xbc_ref[:, 0:D_SSD]
    sp_in = dt_ref[...] + dtb_ref[...]
    dtf = _softplus(sp_in)
    a_f = -jnp.exp(alog_ref[...])
    a_dt = dtf * a_f
    row = lax.broadcasted_iota(jnp.int32, (Q, Q), 0)
    col = lax.broadcasted_iota(jnp.int32, (Q, Q), 1)
    causal = row >= col
    cs = _dot_hi(causal.astype(F32), a_dt)
    cs_ref[...] = cs
    cs_last = cs_ref[Q - 1:Q, :]
    return xs, sp_in, dtf, a_f, cs, cs_last, causal


def _head_decay(cs_j, cst_ref, e, causal):
    lane = lax.broadcasted_iota(jnp.int32, (Q, LANE), 1)
    rolled = pltpu.roll(cs_j, HEAD, 1)
    own = (lane < HEAD) if e == 0 else (lane >= HEAD)
    col_b = jnp.where(own, cs_j, rolled)
    col_b = jnp.concatenate([col_b] * (Q // LANE), axis=1)
    row_b = cst_ref[e * HEAD:e * HEAD + 1, :]
    return jnp.where(causal, jnp.exp(jnp.minimum(col_b - row_b, 0.0)), 0.0)


def ssd_fwd(xbc, proj, dtb_f, alog_f, dsk_f, snw, name):
    L = xbc.shape[0]
    nc = L // Q

    def body(xbc_ref, z_ref, dt_ref, dtb_ref, alog_ref, dsk_ref, snw_ref, y_ref, yn_ref, sp_ref, s_ref, cs_ref, cst_ref, yd_ref):
        @pl.when(pl.program_id(0) == 0)
        def _():
            s_ref[...] = jnp.zeros((N_STATE, D_SSD), F32)

        xs, _, dtf, a_f, cs, cs_last, causal = _ssd_common(xbc_ref, dt_ref, dtb_ref, alog_ref, cs_ref)
        e_cs = jnp.exp(cs)
        xdt = xs * dtf
        zst = jnp.exp(cs_last - cs) * xdt
        sp_ref[0] = s_ref[...]
        lane = lax.broadcasted_iota(jnp.int32, (Q, LANE), 1)
        for g in range(2):
            gl = slice(g * 512, g * 512 + 512)
            b_g = xbc_ref[:, D_SSD + g * N_STATE:D_SSD + (g + 1) * N_STATE]
            c_g = xbc_ref[:, D_SSD + 2 * N_STATE + g * N_STATE:D_SSD + 2 * N_STATE + (g + 1) * N_STATE]
            s_prev = s_ref[:, gl]
            cb = _dot_nt(c_g, b_g)
            yd_ref[:, gl] = e_cs[:, gl] * _dot(c_g, s_prev)
            for j in range(4):
                tl = slice(g * 512 + j * LANE, g * 512 + (j + 1) * LANE)
                cs_j = cs[:, tl]
                cst_ref[...] = cs_j.T
                x_j = xdt[:, tl]
                o0 = _dot(cb * _head_decay(cs_j, cst_ref, 0, causal), x_j)
                o1 = _dot(cb * _head_decay(cs_j, cst_ref, 1, causal), x_j)
                yd_ref[:, tl] += jnp.where(lane < HEAD, o0, o1)
            s_ref[:, gl] = jnp.exp(cs_last[:, gl]) * s_prev + _dot_tn(b_g, zst[:, gl])
        y = yd_ref[...] + xs * dsk_ref[...]
        y_ref[...] = y
        yz = y * _silu(z_ref[...])
        r = lax.rsqrt(jnp.mean(yz * yz, axis=-1, keepdims=True) + 1e-6)
        yn_ref[...] = (yz * r * snw_ref[...]).astype(BF16)

    chunk = lambda w, c: pl.BlockSpec((Q, w), lambda i: (i, c))
    return pl.pallas_call(
        body, name=name, grid=(nc,),
        out_shape=[_sds((L, D_SSD), F32), _sds((L, D_SSD), BF16), _sds((nc, N_STATE, D_SSD), F32)],
        in_specs=[chunk(D_XBC, 0), chunk(D, 0), chunk(D, 1)] + [_const_spec((1, D))] * 4,
        out_specs=[chunk(D, 0), chunk(D, 0), pl.BlockSpec((1, N_STATE, D_SSD), lambda i: (i, 0, 0))],
        scratch_shapes=[pltpu.VMEM((N_STATE, D_SSD), F32), pltpu.VMEM((Q, D_SSD), F32), pltpu.VMEM((LANE, Q), F32),
                        pltpu.VMEM((Q, D_SSD), F32)],
        compiler_params=_cparams(("arbitrary",)))(xbc, proj, proj, dtb_f, alog_f, dsk_f, snw)


def ssd_bwd(dmixin, y, xbc, proj, s_prev_all, dtb_f, alog_f, dsk_f, snw, name):
    L = xbc.shape[0]
    nc = L // Q

    def body(dyn_ref, y_ref, xbc_ref, z_ref, dt_ref, sp_ref, dtb_ref, alog_ref, dsk_ref, snw_ref,
             dz_ref, ddt_ref, dxbc_ref, acc_ref, acc16_ref, ds_ref, cs_ref, cst_ref, dcs_ref, dx_ref):
        step = pl.program_id(0)

        @pl.when(step == 0)
        def _():
            ds_ref[...] = jnp.zeros((N_STATE, D_SSD), F32)
            acc_ref[...] = jnp.zeros((8, D), F32)

        z = z_ref[...]
        y = y_ref[...]
        sz = _sigmoid(z)
        siluz = z * sz
        yz = y * siluz
        r = lax.rsqrt(jnp.mean(yz * yz, axis=-1, keepdims=True) + 1e-6)
        n = yz * r
        dyn = dyn_ref[...]
        acc_ref[0:1, :] += _colsum(dyn * n)
        dn = dyn * snw_ref[...]
        dyz = r * (dn - n * jnp.mean(dn * n, axis=-1, keepdims=True))
        dy = dyz * siluz
        dz_ref[...] = (dyz * y * (sz * (1.0 + z * (1.0 - sz)))).astype(BF16)

        xs, sp_in, dtf, a_f, cs, cs_last, causal = _ssd_common(xbc_ref, dt_ref, dtb_ref, alog_ref, cs_ref)
        acc_ref[3:4, :] += _colsum(dy * xs)
        e_cs = jnp.exp(cs)
        xdt = xs * dtf
        dst = jnp.exp(cs_last - cs)
        zst = dst * xdt
        e_last = jnp.exp(cs_last)
        lane = lax.broadcasted_iota(jnp.int32, (Q, LANE), 1)
        ones = jnp.ones((Q, LANE), F32)
        dcs_last_parts = []
        for g in range(2):
            gl = slice(g * 512, g * 512 + 512)
            b_g = xbc_ref[:, D_SSD + g * N_STATE:D_SSD + (g + 1) * N_STATE]
            c_g = xbc_ref[:, D_SSD + 2 * N_STATE + g * N_STATE:D_SSD + 2 * N_STATE + (g + 1) * N_STATE]
            s_prev = sp_ref[0, :, gl]
            ds_g = ds_ref[:, gl]
            dy_g = dy[:, gl]
            cb = _dot_nt(c_g, b_g)
            y_off = e_cs[:, gl] * _dot(c_g, s_prev)
            edy = e_cs[:, gl] * dy_g
            d_c = _dot_nt(edy, s_prev)
            d_z = _dot(b_g, ds_g)
            d_b = _dot_nt(zst[:, gl], ds_g)
            t_g = d_z * zst[:, gl]
            dcs_ref[:, gl] = dy_g * y_off - t_g
            dx_ref[:, gl] = d_z * dst[:, gl]
            dcs_last_parts.append(_colsum(t_g) + _colsum(ds_g * s_prev) * e_last[:, gl])
            ds_ref[:, gl] = e_last[:, gl] * ds_g + _dot_tn(c_g, edy)
            dcb = jnp.zeros((Q, Q), F32)
            for j in range(4):
                tl = slice(g * 512 + j * LANE, g * 512 + (j + 1) * LANE)
                cs_j = cs[:, tl]
                cst_ref[...] = cs_j.T
                x_j = xdt[:, tl]
                dy_j = dy[:, tl]
                dx_j = jnp.zeros((Q, LANE), F32)
                dcs_j = jnp.zeros((Q, LANE), F32)
                for e in range(2):
                    own = (lane < HEAD) if e == 0 else (lane >= HEAD)
                    w_h = _head_decay(cs_j, cst_ref, e, causal)
                    g_h = cb * w_h
                    dy_m = jnp.where(own, dy_j, 0.0)
                    d_g = _dot_nt(dy_m, x_j)
                    dx_j = dx_j + _dot_tn(g_h, dy_m)
                    dcb = dcb + d_g * w_h
                    p_h = d_g * g_h
                    dcs_j = dcs_j + jnp.where(own, _dot_hi(p_h, ones) - _dot_tn_hi(p_h, ones), 0.0)
                dcs_ref[:, tl] += dcs_j * (1.0 / HEAD)
                dx_ref[:, tl] += dx_j
            d_c = d_c + _dot(dcb, b_g)
            d_b = d_b + _dot_tn(dcb, c_g)
            dxbc_ref[:, D_SSD + g * N_STATE:D_SSD + (g + 1) * N_STATE] = d_b
            dxbc_ref[:, D_SSD + 2 * N_STATE + g * N_STATE:D_SSD + 2 * N_STATE + (g + 1) * N_STATE] = d_c
        dcs_last = jnp.concatenate(dcs_last_parts, axis=1)
        anticausal = lax.broadcasted_iota(jnp.int32, (Q, Q), 0) <= lax.broadcasted_iota(jnp.int32, (Q, Q), 1)
        d_adt = _dot_hi(anticausal.astype(F32), dcs_ref[...]) + dcs_last
        dx = dx_ref[...]
        acc_ref[2:3, :] += _colsum(d_adt * dtf) * a_f
        d_dtf = d_adt * a_f + dx * xs
        dxbc_ref[:, 0:D_SSD] = dx * dtf + dy * dsk_ref[...]
        d_raw = d_dtf * _sigmoid(sp_in)
        acc_ref[1:2, :] += _colsum(d_raw)
        head_of_lane = lax.broadcasted_iota(jnp.int32, (D_SSD, LANE), 0) // HEAD
        fold = (head_of_lane == lax.broadcasted_iota(jnp.int32, (D_SSD, LANE), 1)).astype(F32)
        ddt_ref[...] = _dot_hi(d_raw, fold).astype(BF16)

        @pl.when(step == nc - 1)
        def _():
            acc16_ref[...] = _dot_hi(acc_ref[...], fold)

    rchunk = lambda w, c: pl.BlockSpec((Q, w), lambda i: (nc - 1 - i, c))
    return pl.pallas_call(
        body, name=name, grid=(nc,),
        out_shape=[_sds((L, D_SSD), BF16), _sds((L, LANE), BF16), _sds((L, D_XBC), F32), _sds((8, D), F32), _sds((8, LANE), F32)],
        in_specs=[rchunk(D, 0), rchunk(D, 0), rchunk(D_XBC, 0), rchunk(D, 0), rchunk(D, 1),
                  pl.BlockSpec((1, N_STATE, D_SSD), lambda i: (nc - 1 - i, 0, 0))] + [_const_spec((1, D))] * 4,
        out_specs=[rchunk(D, 0), rchunk(LANE, 0), rchunk(D_XBC, 0), _const_spec((8, D)), _const_spec((8, LANE))],
        scratch_shapes=[pltpu.VMEM((N_STATE, D_SSD), F32), pltpu.VMEM((Q, D_SSD), F32), pltpu.VMEM((LANE, Q), F32),
                        pltpu.VMEM((Q, D_SSD), F32), pltpu.VMEM((Q, D_SSD), F32)],
        compiler_params=_cparams(("arbitrary",)))(dmixin, y, xbc, proj, proj, s_prev_all, dtb_f, alog_f, dsk_f, snw)


def _adamw_math(w, g, m, v):
    m_n = ADAM_B1 * m + (1.0 - ADAM_B1) * g
    v_n = ADAM_B2 * v + (1.0 - ADAM_B2) * jnp.square(g)
    c1 = 1.0 - ADAM_B1 ** ADAM_STEP
    c2 = 1.0 - ADAM_B2 ** ADAM_STEP
    return -ADAM_LR * ((m_n / c1) / (jnp.sqrt(v_n / c2) + ADAM_EPS) + ADAM_WD * w), m_n, v_n


def _sum_slots(p_ref):
    acc = p_ref[0].astype(F32)
    for s in range(1, N_DEV):
        acc = acc + p_ref[s].astype(F32)
    return acc


def adamw_slots(w, slots, m, v, name):
    rows, cols = w.shape
    tr = next(t for t in (256, 128, 64, 32, 16) if rows % t == 0)

    def body(w_ref, s_ref, m_ref, v_ref, g_ref, d_ref, mo_ref, vo_ref):
        g_v = _sum_slots(s_ref)
        g_ref[...] = g_v
        d_ref[...], mo_ref[...], vo_ref[...] = _adamw_math(w_ref[...], g_v, m_ref[...], v_ref[...])

    spec = pl.BlockSpec((tr, cols), lambda i: (i, 0))
    return pl.pallas_call(body, name=name, grid=(rows // tr,), out_shape=[_sds((rows, cols), F32)] * 4,
                          in_specs=[spec, pl.BlockSpec((N_DEV, tr, cols), lambda i: (0, i, 0)), spec, spec], out_specs=[spec] * 4,
                          compiler_params=_cparams(("parallel",)))(w, slots, m, v)


def adamw_many(ws, gs, ms, vs, name):
    n = len(ws)

    def body(*refs):
        for p in range(n):
            d_v, m_v, v_v = _adamw_math(refs[p][...], refs[n + p][...], refs[2 * n + p][...], refs[3 * n + p][...])
            refs[4 * n + p][...] = d_v
            refs[5 * n + p][...] = m_v
            refs[6 * n + p][...] = v_v

    vm = pl.BlockSpec(memory_space=pltpu.VMEM)
    out = pl.pallas_call(body, name=name, out_shape=[_sds(w.shape, F32) for w in ws] * 3, in_specs=[vm] * (4 * n),
                         out_specs=[vm] * (3 * n), compiler_params=_cparams())(*ws, *gs, *ms, *vs)
    return out[:n], out[n:2 * n], out[2 * n:]


def sum_slots_many(parts, name):
    n = len(parts)

    def body(*refs):
        for p in range(n):
            refs[n + p][...] = _sum_slots(refs[p])

    vm = pl.BlockSpec(memory_space=pltpu.VMEM)
    return pl.pallas_call(body, name=name, out_shape=[_sds(p.shape[1:], F32) for p in parts], in_specs=[vm] * n,
                          out_specs=[vm] * n, compiler_params=_cparams())(*parts)


def ada_mod(c_all, ada_w_shard, ada_b_cols, name):
    def body(c_ref, w_ref, b_ref, o_ref, ca_ref):
        ca = _silu(c_ref[...])
        ca_ref[...] = ca
        o_ref[...] = _dot(ca, w_ref[...]) + b_ref[...]

    vm = pl.BlockSpec(memory_space=pltpu.VMEM)
    return pl.pallas_call(body, name=name, out_shape=[_sds((N_DEV, ada_w_shard.shape[1]), F32), _sds((N_DEV, D), F32)],
                          in_specs=[vm, vm, vm], out_specs=[vm, vm], compiler_params=_cparams())(c_all, ada_w_shard, ada_b_cols)


def ada_wgrad(c_act_all, dmod_cols, name):
    def body(c_ref, d_ref, o_ref):
        o_ref[...] = _dot_tn_hi(c_ref[...], d_ref[...])

    vm = pl.BlockSpec(memory_space=pltpu.VMEM)
    return pl.pallas_call(body, name=name, out_shape=_sds((D, dmod_cols.shape[1]), F32), in_specs=[vm, vm], out_specs=vm,
                          compiler_params=_cparams())(c_act_all, dmod_cols)


def exchange(srcs, name, gather):
    n = len(srcs)
    shapes = [tuple(s.shape) if gather else tuple(s.shape[1:]) for s in srcs]

    def body(*refs):
        src_refs, out_refs = refs[:n], refs[n:2 * n]
        send_sems, recv_sems, local_sems = refs[2 * n:]
        x, y, c = lax.axis_index("x"), lax.axis_index("y"), lax.axis_index("c")
        me = 4 * x + 2 * y + c

        def peer(k):
            bx, by, bc = (k >> 2) & 1, (k >> 1) & 1, k & 1
            px, py, pc = (x + bx) % 2, (y + by) % 2, (c + bc) % 2
            return (px, py, pc), 4 * px + 2 * py + pc

        def copy(a, k, landing):
            dev, idx = peer(k)
            return pltpu.make_async_remote_copy(
                src_ref=src_refs[a] if gather else src_refs[a].at[idx], dst_ref=out_refs[a].at[idx if landing else me],
                send_sem=send_sems.at[a, k - 1], recv_sem=recv_sems.at[a, k - 1],
                device_id=dev, device_id_type=pl.DeviceIdType.MESH)

        mine = [pltpu.make_async_copy(src_refs[a] if gather else src_refs[a].at[me], out_refs[a].at[me], local_sems.at[a])
                for a in range(n)]
        for cp in mine:
            cp.start()
        sends = [copy(a, k, False) for a in range(n) for k in range(1, N_DEV)]
        for cp in sends:
            cp.start()
        for a in range(n):
            for k in range(1, N_DEV):
                copy(a, k, True).wait_recv()
        for cp in sends:
            cp.wait_send()
        for cp in mine:
            cp.wait()

    hbm = pl.BlockSpec(memory_space=pl.ANY)
    return pl.pallas_call(
        body, name=name, out_shape=[_sds((N_DEV,) + shp, s.dtype) for shp, s in zip(shapes, srcs)], in_specs=[hbm] * n,
        out_specs=[hbm] * n,
        scratch_shapes=[pltpu.SemaphoreType.DMA((n, N_DEV - 1)), pltpu.SemaphoreType.DMA((n, N_DEV - 1)),
                        pltpu.SemaphoreType.DMA((n,))],
        compiler_params=pltpu.CompilerParams(has_side_effects=True))(*srcs)


def _cols_to_slabs(g):
    r, c = g.shape
    return g.reshape(r, N_DEV, c // N_DEV).transpose(1, 0, 2)


def _slabs_to_cols(s):
    _, r, cs = s.shape
    return s.transpose(1, 0, 2).reshape(r, N_DEV * cs)


def _rep_heads(v):
    return jnp.repeat(v.reshape(N_HEADS), HEAD).reshape(1, D_SSD)


def local_fwd_bwd(x, target, mod, wts, small):
    n1w, n2w, fnw = small["norm1_w"], small["norm2_w"], small["final_norm_w"]
    dtb_f, alog_f, dsk_f = _rep_heads(small["dt_bias"]), _rep_heads(small["a_log"]), _rep_heads(small["d_skip"])
    snw = small["ssd_norm_w"]

    h1 = norm_mod(x, mod, n1w, 0, "norm1")
    proj = mm_nn([(h1, wts["w_in_p"], 0)], "in_proj")
    xbc = conv_silu_fwd(proj, 4096 // CB, D_XBC, small["ssd_conv_w"], small["ssd_conv_b"], "ssd_conv")
    y, ysn, s_prev = ssd_fwd(xbc, proj, dtb_f, alog_f, dsk_f, snw, "ssd_scan")
    uc = conf_conv_fwd(proj, 2048 // CB, 3072 // CB, small["conf_conv_w"], small["conf_conv_b"], "conf_conv")
    u = ln_silu(uc, small["conf_ln_w"], small["conf_ln_b"], "conf_ln")
    mix = mm_nn([(ysn, wts["w_out"], 0), (u, wts["w_out"], 1)], "out_proj")
    h2, x1 = norm_mod(x, mod, n2w, 3, "norm2", res=mix, gate_row=2)
    up = mm_nn([(h2, wts["w_up"], 0)], "up_proj")
    act = ffn_conv_fwd(up, small["ffn_conv_w"], small["ffn_conv_b"], "ffn_conv")
    ff = mm_nn([(act, wts["w_down"], 0)], "down_proj")
    dx2, dff, acc_f = final_loss(ff, x1, mod, fnw, target, "final_loss")

    g_w_down = mm_tn(act, dff, "wgrad_down")
    dact = mm_nt([(dff, wts["w_down"], 0)], "dact")
    dupg, dupv, dwg, dwv = ffn_conv_bwd(up, small["ffn_conv_w"], small["ffn_conv_b"], dact, "ffn_conv_bwd")
    g_w_up = jnp.concatenate([mm_tn(h2, dupg, "wgrad_up_gate"), mm_tn(h2, dupv, "wgrad_up_val")], axis=1)
    dh2 = mm_nt([(dupg, wts["w_up"], 0), (dupv, wts["w_up"], 1)], "dh2")
    dx1, dmix, acc_2 = norm_mod_bwd(dh2, x1, dx2, mod, n2w, 3, "norm2_bwd", mix=mix, gate_row=2)

    g_w_out = jnp.concatenate([mm_tn(ysn, dmix, "wgrad_out_ssd"), mm_tn(u, dmix, "wgrad_out_conf")], axis=0)
    dmixin = mm_nt([(dmix, wts["w_out"], 0)], "dmixin")
    duc, acc_ln = ln_silu_bwd(dmixin, uc, small["conf_ln_w"], small["conf_ln_b"], "conf_ln_bwd")
    dcfa, dcfg, dw_cc = conf_conv_bwd(proj, 2048 // CB, 3072 // CB, small["conf_conv_w"], duc, "conf_conv_bwd")
    dz, ddt, dxbc_post, acc_s, acc_s16 = ssd_bwd(dmixin, y, xbc, proj, s_prev, dtb_f, alog_f, dsk_f, snw, "ssd_scan_bwd")
    dxbc, dw_sc = conv_silu_bwd(proj, 4096 // CB, D_XBC, small["ssd_conv_w"], small["ssd_conv_b"], dxbc_post, "ssd_conv_bwd")
    dh1 = mm_nt([(dz, wts["w_z"], 0), (ddt, wts["w_dt16"], 0), (dcfa, wts["w_cfa"], 0), (dcfg, wts["w_cfg"], 0),
                 (dxbc, wts["w_xbc"], 0)], "dh1")
    grad_x, acc_1 = norm_mod_bwd(dh1, x, dx1, mod, n1w, 0, "norm1_bwd")
    g_w_in = jnp.concatenate([mm_tn(h1, dz, "wgrad_in_z"), mm_tn(h1, dxbc, "wgrad_in_xbc"),
                              mm_tn(h1, ddt, "wgrad_in_dt")[:, :N_HEADS], mm_tn(h1, dcfa, "wgrad_in_cfa"),
                              mm_tn(h1, dcfg, "wgrad_in_cfg")], axis=1)

    big_grads = {"w_in": g_w_in, "w_out": g_w_out, "w_up": g_w_up, "w_down": g_w_down}
    small_accs = dict(acc_1=acc_1, acc_2=acc_2, acc_f=acc_f, acc_ln=acc_ln, acc_s=acc_s, acc_s16=acc_s16, dw_sc=dw_sc,
                      dw_cc=dw_cc, dwg=dwg, dwv=dwv)
    return grad_x, big_grads, small_accs


def kernel(x, c, ada_w, ada_b, norm1_w, w_in, ssd_conv_w, ssd_conv_b, dt_bias, a_log, d_skip, ssd_norm_w, conf_conv_w, conf_conv_b, conf_ln_w, conf_ln_b, w_out, norm2_w, w_up, ffn_conv_w, ffn_conv_b, w_down, final_norm_w, loss_target, m_ada_w, m_ada_b, m_norm1_w, m_w_in, m_ssd_conv_w, m_ssd_conv_b, m_dt_bias, m_a_log, m_d_skip, m_ssd_norm_w, m_conf_conv_w, m_conf_conv_b, m_conf_ln_w, m_conf_ln_b, m_w_out, m_norm2_w, m_w_up, m_ffn_conv_w, m_ffn_conv_b, m_w_down, m_final_norm_w, v_ada_w, v_ada_b, v_norm1_w, v_w_in, v_ssd_conv_w, v_ssd_conv_b, v_dt_bias, v_a_log, v_d_skip, v_ssd_norm_w, v_conf_conv_w, v_conf_conv_b, v_conf_ln_w, v_conf_ln_b, v_w_out, v_norm2_w, v_w_up, v_ffn_conv_w, v_ffn_conv_b, v_w_down, v_final_norm_w):
    me = 4 * lax.axis_index("x") + 2 * lax.axis_index("y") + lax.axis_index("c")
    weights = dict(ada_w=ada_w, ada_b=ada_b, norm1_w=norm1_w, w_in=w_in, ssd_conv_w=ssd_conv_w, ssd_conv_b=ssd_conv_b,
                   dt_bias=dt_bias, a_log=a_log, d_skip=d_skip, ssd_norm_w=ssd_norm_w, conf_conv_w=conf_conv_w,
                   conf_conv_b=conf_conv_b, conf_ln_w=conf_ln_w, conf_ln_b=conf_ln_b, w_out=w_out, norm2_w=norm2_w, w_up=w_up,
                   ffn_conv_w=ffn_conv_w, ffn_conv_b=ffn_conv_b, w_down=w_down, final_norm_w=final_norm_w)
    moms_m = dict(ada_w=m_ada_w, ada_b=m_ada_b, norm1_w=m_norm1_w, w_in=m_w_in, ssd_conv_w=m_ssd_conv_w, ssd_conv_b=m_ssd_conv_b,
                  dt_bias=m_dt_bias, a_log=m_a_log, d_skip=m_d_skip, ssd_norm_w=m_ssd_norm_w, conf_conv_w=m_conf_conv_w,
                  conf_conv_b=m_conf_conv_b, conf_ln_w=m_conf_ln_w, conf_ln_b=m_conf_ln_b, w_out=m_w_out, norm2_w=m_norm2_w,
                  w_up=m_w_up, ffn_conv_w=m_ffn_conv_w, ffn_conv_b=m_ffn_conv_b, w_down=m_w_down, final_norm_w=m_final_norm_w)
    moms_v = dict(ada_w=v_ada_w, ada_b=v_ada_b, norm1_w=v_norm1_w, w_in=v_w_in, ssd_conv_w=v_ssd_conv_w, ssd_conv_b=v_ssd_conv_b,
                  dt_bias=v_dt_bias, a_log=v_a_log, d_skip=v_d_skip, ssd_norm_w=v_ssd_norm_w, conf_conv_w=v_conf_conv_w,
                  conf_conv_b=v_conf_conv_b, conf_ln_w=v_conf_ln_w, conf_ln_b=v_conf_ln_b, w_out=v_w_out, norm2_w=v_norm2_w,
                  w_up=v_w_up, ffn_conv_w=v_ffn_conv_w, ffn_conv_b=v_ffn_conv_b, w_down=v_w_down, final_norm_w=v_final_norm_w)
    names = list(weights)

    def to2d(a):
        return a[0] if a.ndim == 3 else a.reshape(1, -1)

    c_all, scw_all, ccw_all, fcw_all = exchange([c.reshape(8, LANE), ssd_conv_w[0], conf_conv_w[0], ffn_conv_w[0]],
                                                "gather_small", gather=True)
    c_all = c_all.reshape(N_DEV, D)

    ada_cols = ada_w.shape[2]
    ada_b_cols = lax.dynamic_slice(ada_b, (0, me * ada_cols), (1, ada_cols))
    mod_cols, c_act_all = ada_mod(c_all, ada_w[0], ada_b_cols, "ada_mod")
    mod_parts, = exchange([jnp.pad(mod_cols, ((0, 0), (0, D - ada_cols))).reshape(N_DEV, 8, LANE)], "scatter_mod", gather=False)
    mod = mod_parts.reshape(N_DEV, D)[:, :ada_cols].reshape(6, D)
    mod = jnp.pad(mod, ((0, 2), (0, 0)))

    big = ("w_in", "w_out", "w_up", "w_down")
    w_in_s, w_out_s, w_up_s, w_down_s = exchange([weights[n][0].astype(BF16) for n in big], "gather_weights", gather=True)
    w_in_f = _slabs_to_cols(w_in_s)
    w_z, w_xbc, w_dt, w_cfa, w_cfg = (w_in_f[:, :1024], w_in_f[:, 1024:2560], w_in_f[:, 2560:2576], w_in_f[:, 2576:3600],
                                      w_in_f[:, 3600:])
    wts = dict(
        w_in_p=jnp.concatenate([w_z, jnp.repeat(w_dt, HEAD, axis=1), w_cfa, w_cfg, w_xbc], axis=1),
        w_z=w_z, w_xbc=w_xbc, w_dt16=jnp.pad(w_dt, ((0, 0), (0, LANE - N_HEADS))), w_cfa=w_cfa, w_cfg=w_cfg,
        w_out=w_out_s.reshape(2 * D, D), w_up=_slabs_to_cols(w_up_s), w_down=w_down_s.reshape(D_FF, D))
    small = {n: to2d(weights[n]) for n in names if n not in ("ada_w",) + big}
    small["ssd_conv_w"] = _slabs_to_cols(scw_all)
    small["conf_conv_w"] = _slabs_to_cols(ccw_all)
    small["ffn_conv_w"] = _slabs_to_cols(fcw_all)

    grad_x, big_grads, accs = local_fwd_bwd(x[0], loss_target[0], mod, wts, small)
    loss = lax.psum(0.5 / D * jnp.sum(accs["acc_f"][2:3]), ("x", "y", "c"))

    sends = [_cols_to_slabs(big_grads["w_in"]).astype(BF16), big_grads["w_out"].reshape(N_DEV, 2 * D // N_DEV, D).astype(BF16),
             _cols_to_slabs(big_grads["w_up"]).astype(BF16), big_grads["w_down"].reshape(N_DEV, D_FF // N_DEV, D).astype(BF16)]
    landed = exchange(sends, "scatter_grads", gather=False)
    grads, delta, new_m, new_v = {}, {}, {}, {}
    for n, slots in zip(big, landed):
        grads[n], delta[n], new_m[n], new_v[n] = adamw_slots(weights[n][0], slots, moms_m[n][0], moms_v[n][0], "adamw_" + n)

    order = ("acc_1", "acc_2", "acc_f", "acc_ln", "acc_s", "acc_s16", "dw_sc", "dw_cc", "dwg", "dwv")
    gathered = dict(zip(order, exchange([accs[k] for k in order], "gather_small_grads", gather=True)))
    red = dict(zip(order, sum_slots_many([gathered[k] for k in order], "sum_small_grads")))

    def mod_rows(a1, a2, af):
        return jnp.concatenate([a1[..., 0:2, :], a2[..., 3:4, :], a2[..., 0:2, :], af[..., 1:2, :]], axis=-2)

    dmod_all = mod_rows(gathered["acc_1"], gathered["acc_2"], gathered["acc_f"]).reshape(N_DEV, 6 * D)
    grads["ada_w"] = ada_wgrad(c_act_all, lax.dynamic_slice(dmod_all, (0, me * ada_cols), (N_DEV, ada_cols)), "ada_wgrad")

    def my_cols(full, k_taps):
        cols = full.shape[1] // N_DEV
        return lax.dynamic_slice(full, (0, me * cols), (k_taps, cols))

    fcw = jnp.concatenate([red["dwg"], red["dwv"]], axis=1)
    grads.update(
        ada_b=mod_rows(red["acc_1"], red["acc_2"], red["acc_f"]).reshape(1, 6 * D), norm1_w=red["acc_1"][2:3],
        ssd_conv_w=my_cols(red["dw_sc"], K_SSD), ssd_conv_b=red["dw_sc"][K_SSD:K_SSD + 1],
        dt_bias=red["acc_s16"][1:2, :N_HEADS], a_log=red["acc_s16"][2:3, :N_HEADS], d_skip=red["acc_s16"][3:4, :N_HEADS],
        ssd_norm_w=red["acc_s"][0:1], conf_conv_w=my_cols(red["dw_cc"], K_CONF), conf_conv_b=red["dw_cc"][K_CONF:K_CONF + 1],
        conf_ln_w=red["acc_ln"][0:1], conf_ln_b=red["acc_ln"][1:2], norm2_w=red["acc_2"][2:3],
        ffn_conv_w=my_cols(fcw, K_FFN), ffn_conv_b=fcw[K_FFN:K_FFN + 1], final_norm_w=red["acc_f"][0:1])

    rest = [n for n in names if n not in big]
    d_l, m_l, v_l = adamw_many([to2d(weights[n]) for n in rest], [grads[n] for n in rest], [to2d(moms_m[n]) for n in rest],
                               [to2d(moms_v[n]) for n in rest], "adamw_small")
    for n, dd, mm, vv in zip(rest, d_l, m_l, v_l):
        delta[n], new_m[n], new_v[n] = dd, mm, vv
    shape_of = lambda d_: {n: d_[n].reshape(weights[n].shape) for n in names}
    grads, delta, new_m, new_v = shape_of(grads), shape_of(delta), shape_of(new_m), shape_of(new_v)
    return (loss, grad_x[None], *[grads[n] for n in names], *[delta[n] for n in names], *[new_m[n] for n in names],
            *[new_v[n] for n in names])
```

```python
import functools

import jax
import jax.numpy as jnp
from jax import lax
from jax.experimental import pallas as pl
from jax.experimental.pallas import tpu as pltpu

F32 = jnp.float32
BF16 = jnp.bfloat16
HI = lax.Precision.HIGHEST

N_DEV = 8
D = 1024
D_SSD = 1024
HEAD = 64
N_HEADS = 16
N_STATE = 128
D_XBC = 1536
D_CONF = 1024
D_FF = 2816
K_SSD, K_CONF, K_FFN = 4, 31, 3
D_INP = 5632
LANE = 128
TR = 256
Q = 256
CB = 256
VMEM_LIMIT = 56 * 1024 * 1024

ADAM_LR, ADAM_B1, ADAM_B2, ADAM_EPS, ADAM_WD, ADAM_STEP = 0.001, 0.9, 0.999, 1e-08, 0.01, 10


def _cparams(sem=None):
    return pltpu.CompilerParams(vmem_limit_bytes=VMEM_LIMIT, dimension_semantics=sem)


def _sds(shape, dtype):
    return jax.ShapeDtypeStruct(shape, dtype)


def _sigmoid(x):
    return 1.0 / (1.0 + jnp.exp(-x))


def _silu(x):
    return x * _sigmoid(x)


def _dsilu(x):
    s = _sigmoid(x)
    return s * (1.0 + x * (1.0 - s))


def _softplus(x):
    return jnp.maximum(x, 0.0) + jnp.log(1.0 + jnp.exp(-jnp.abs(x)))


def _dot(a, b):
    return jnp.dot(a.astype(BF16), b.astype(BF16), preferred_element_type=F32)


def _dot_nt(a, b):
    return lax.dot_general(a.astype(BF16), b.astype(BF16), (((1,), (1,)), ((), ())), preferred_element_type=F32)


def _dot_tn(a, b):
    return lax.dot_general(a.astype(BF16), b.astype(BF16), (((0,), (0,)), ((), ())), preferred_element_type=F32)


def _dot_hi(a, b):
    return jnp.dot(a, b, precision=HI, preferred_element_type=F32)


def _dot_tn_hi(a, b):
    return lax.dot_general(a, b, (((0,), (0,)), ((), ())), precision=HI, preferred_element_type=F32)


def _colsum(x):
    return jnp.sum(x, axis=0, keepdims=True)


def _const_spec(shape):
    return pl.BlockSpec(shape, lambda *_: (0,) * len(shape))


def _col_tile(n):
    for t in (1408, 1024, 768, 512, 256, 128):
        if n % t == 0 and t <= n:
            return t
    return n


def mm_nn(pairs, name):
    L = pairs[0][0].shape[0]
    N = pairs[0][1].shape[1]
    tn = _col_tile(N)
    n = len(pairs)

    def body(*refs):
        acc = None
        for p in range(n):
            t = jnp.dot(refs[2 * p][...], refs[2 * p + 1][...], preferred_element_type=F32)
            acc = t if acc is None else acc + t
        refs[-1][...] = acc

    in_specs, args = [], []
    for a, w, rb in pairs:
        in_specs += [pl.BlockSpec((TR, a.shape[1]), lambda j, i: (i, 0)),
                     pl.BlockSpec((a.shape[1], tn), functools.partial(lambda j, i, rb: (rb, j), rb=rb))]
        args += [a, w]
    return pl.pallas_call(
        body, name=name, grid=(N // tn, L // TR), out_shape=_sds((L, N), F32), in_specs=in_specs,
        out_specs=pl.BlockSpec((TR, tn), lambda j, i: (i, j)),
        compiler_params=_cparams(("parallel", "parallel")))(*args)


def mm_nt(pairs, name):
    L = pairs[0][0].shape[0]
    K = pairs[0][1].shape[0]
    tk = _col_tile(K)
    n = len(pairs)

    def body(*refs):
        o_ref = refs[-1]
        acc = None
        for p in range(n):
            t = lax.dot_general(refs[2 * p][...], refs[2 * p + 1][...], (((1,), (1,)), ((), ())),
                                preferred_element_type=F32)
            acc = t if acc is None else acc + t
        o_ref[...] = acc

    in_specs, args = [], []
    for a, w, cb in pairs:
        in_specs += [pl.BlockSpec((TR, a.shape[1]), lambda j, i: (i, 0)),
                     pl.BlockSpec((tk, a.shape[1]), functools.partial(lambda j, i, cb: (j, cb), cb=cb))]
        args += [a, w]
    return pl.pallas_call(
        body, name=name, grid=(K // tk, L // TR), out_shape=_sds((L, K), F32), in_specs=in_specs,
        out_specs=pl.BlockSpec((TR, tk), lambda j, i: (i, j)),
        compiler_params=_cparams(("parallel", "parallel")))(*args)


def mm_tn(a, g, name):
    L, M = a.shape
    N = g.shape[1]
    tn = _col_tile(N) if N > 1024 else N
    if M * tn * 4 > 8 * 1024 * 1024:
        tn = 512
    tl = 512 if L % 512 == 0 else TR

    def body(a_ref, g_ref, o_ref):
        @pl.when(pl.program_id(1) == 0)
        def _():
            o_ref[...] = jnp.zeros((M, tn), F32)

        o_ref[...] += lax.dot_general(a_ref[...], g_ref[...], (((0,), (0,)), ((), ())), preferred_element_type=F32)

    return pl.pallas_call(
        body, name=name, grid=(N // tn, L // tl), out_shape=_sds((M, N), F32),
        in_specs=[pl.BlockSpec((tl, M), lambda j, l: (l, 0)), pl.BlockSpec((tl, tn), lambda j, l: (l, j))],
        out_specs=pl.BlockSpec((M, tn), lambda j, l: (0, j)),
        compiler_params=_cparams(("parallel", "arbitrary")))(a, g)


def _row_spec(width=D):
    return pl.BlockSpec((TR, width), lambda i: (i, 0))


def _row_col_spec(width, col):
    return pl.BlockSpec((TR, width), lambda i: (i, col))


def norm_mod(x, mod, w, shift_row, name, res=None, gate_row=None):
    L = x.shape[0]
    has_res = res is not None

    def body(*refs):
        if has_res:
            x_ref, res_ref, mod_ref, w_ref, h_ref, xo_ref = refs
            xin = x_ref[...] + mod_ref[gate_row:gate_row + 1, :] * res_ref[...]
            xo_ref[...] = xin
        else:
            x_ref, mod_ref, w_ref, h_ref = refs
            xin = x_ref[...]
        r = lax.rsqrt(jnp.mean(xin * xin, axis=-1, keepdims=True) + 1e-6)
        h = (xin * r * w_ref[...]) * (1.0 + mod_ref[shift_row + 1:shift_row + 2, :]) + mod_ref[shift_row:shift_row + 1, :]
        h_ref[...] = h.astype(BF16)

    ins = [x] + ([res] if has_res else []) + [mod, w]
    in_specs = [_row_spec()] + ([_row_spec()] if has_res else []) + [_const_spec((8, D)), _const_spec((1, D))]
    out_shape = [_sds((L, D), BF16)] + ([_sds((L, D), F32)] if has_res else [])
    out_specs = [_row_spec()] + ([_row_spec()] if has_res else [])
    out = pl.pallas_call(body, name=name, grid=(L // TR,), out_shape=out_shape, in_specs=in_specs,
                         out_specs=out_specs, compiler_params=_cparams(("parallel",)))(*ins)
    return out if has_res else out[0]


def ln_silu(uc, lnw, lnb, name):
    L = uc.shape[0]

    def body(u_ref, w_ref, b_ref, o_ref):
        u = u_ref[...]
        mu = jnp.mean(u, axis=-1, keepdims=True)
        var = jnp.mean(jnp.square(u - mu), axis=-1, keepdims=True)
        v = (u - mu) * lax.rsqrt(var + 1e-5) * w_ref[...] + b_ref[...]
        o_ref[...] = _silu(v).astype(BF16)

    return pl.pallas_call(body, name=name, grid=(L // TR,), out_shape=_sds((L, D_CONF), BF16),
                          in_specs=[_row_spec(), _const_spec((1, D)), _const_spec((1, D))], out_specs=_row_spec(),
                          compiler_params=_cparams(("parallel",)))(uc, lnw, lnb)


def ln_silu_bwd(du_all, uc, lnw, lnb, name):
    L = uc.shape[0]

    def body(du_ref, u_ref, w_ref, b_ref, o_ref, acc_ref):
        @pl.when(pl.program_id(0) == 0)
        def _():
            acc_ref[...] = jnp.zeros((8, D), F32)

        u = u_ref[...]
        mu = jnp.mean(u, axis=-1, keepdims=True)
        rl = lax.rsqrt(jnp.mean(jnp.square(u - mu), axis=-1, keepdims=True) + 1e-5)
        n = (u - mu) * rl
        v = n * w_ref[...] + b_ref[...]
        dv = du_ref[...] * _dsilu(v)
        acc_ref[0:1, :] += _colsum(dv * n)
        acc_ref[1:2, :] += _colsum(dv)
        dn = dv * w_ref[...]
        o_ref[...] = rl * (dn - jnp.mean(dn, axis=-1, keepdims=True) - n * jnp.mean(dn * n, axis=-1, keepdims=True))

    return pl.pallas_call(body, name=name, grid=(L // TR,), out_shape=[_sds((L, D), F32), _sds((8, D), F32)],
                          in_specs=[_row_col_spec(D, 1), _row_spec(), _const_spec((1, D)), _const_spec((1, D))],
                          out_specs=[_row_spec(), _const_spec((8, D))],
                          compiler_params=_cparams(("arbitrary",)))(du_all, uc, lnw, lnb)


def final_loss(ff, x1, mod, fw, target, name):
    L = ff.shape[0]

    def body(ff_ref, x1_ref, mod_ref, fw_ref, t_ref, dx_ref, dff_ref, acc_ref):
        @pl.when(pl.program_id(0) == 0)
        def _():
            acc_ref[...] = jnp.zeros((8, D), F32)

        ff_v = ff_ref[...]
        g2 = mod_ref[5:6, :]
        x2 = x1_ref[...] + g2 * ff_v
        r = lax.rsqrt(jnp.mean(x2 * x2, axis=-1, keepdims=True) + 1e-6)
        n = x2 * r
        err = n * fw_ref[...] - t_ref[...]
        dy = err * (1.0 / D)
        dn = dy * fw_ref[...]
        dx2 = r * (dn - n * jnp.mean(dn * n, axis=-1, keepdims=True))
        acc_ref[0:1, :] += _colsum(dy * n)
        acc_ref[1:2, :] += _colsum(dx2 * ff_v)
        acc_ref[2:3, :] += _colsum(err * err)
        dx_ref[...] = dx2
        dff_ref[...] = (dx2 * g2).astype(BF16)

    return pl.pallas_call(
        body, name=name, grid=(L // TR,), out_shape=[_sds((L, D), F32), _sds((L, D), BF16), _sds((8, D), F32)],
        in_specs=[_row_spec(), _row_spec(), _const_spec((8, D)), _const_spec((1, D)), _row_spec()],
        out_specs=[_row_spec(), _row_spec(), _const_spec((8, D))],
        compiler_params=_cparams(("arbitrary",)))(ff, x1, mod, fw, target)


def norm_mod_bwd(dh, xin, dres, mod, w, shift_row, name, mix=None, gate_row=None):
    L = dh.shape[0]
    has_mix = mix is not None

    def body(*refs):
        if has_mix:
            dh_ref, x_ref, dres_ref, mod_ref, w_ref, mix_ref, dx_ref, dmix_ref, acc_ref = refs
        else:
            dh_ref, x_ref, dres_ref, mod_ref, w_ref, dx_ref, acc_ref = refs

        @pl.when(pl.program_id(0) == 0)
        def _():
            acc_ref[...] = jnp.zeros((8, D), F32)

        dh_v = dh_ref[...]
        x = x_ref[...]
        r = lax.rsqrt(jnp.mean(x * x, axis=-1, keepdims=True) + 1e-6)
        n = x * r
        nw = n * w_ref[...]
        sc1 = 1.0 + mod_ref[shift_row + 1:shift_row + 2, :]
        acc_ref[0:1, :] += _colsum(dh_v)
        acc_ref[1:2, :] += _colsum(dh_v * nw)
        dnw = dh_v * sc1
        acc_ref[2:3, :] += _colsum(dnw * n)
        dn = dnw * w_ref[...]
        dx = r * (dn - n * jnp.mean(dn * n, axis=-1, keepdims=True)) + dres_ref[...]
        dx_ref[...] = dx
        if has_mix:
            acc_ref[3:4, :] += _colsum(dx * mix_ref[...])
            dmix_ref[...] = (dx * mod_ref[gate_row:gate_row + 1, :]).astype(BF16)

    ins = [dh, xin, dres, mod, w] + ([mix] if has_mix else [])
    in_specs = [_row_spec(), _row_spec(), _row_spec(), _const_spec((8, D)), _const_spec((1, D))] + ([_row_spec()] if has_mix else [])
    out_shape = [_sds((L, D), F32)] + ([_sds((L, D), BF16)] if has_mix else []) + [_sds((8, D), F32)]
    out_specs = [_row_spec()] + ([_row_spec()] if has_mix else []) + [_const_spec((8, D))]
    return pl.pallas_call(body, name=name, grid=(L // TR,), out_shape=out_shape, in_specs=in_specs,
                          out_specs=out_specs, compiler_params=_cparams(("arbitrary",)))(*ins)


def _halo(k):
    return 8 if k <= 9 else 32


def _prev_spec(h, col0):
    return pl.BlockSpec((h, CB), lambda j, i: (jnp.maximum(i * (TR // h) - 1, 0), j + col0))


def _next_spec(h, col0, n_tiles):
    return pl.BlockSpec((h, CB), lambda j, i: (jnp.minimum(i + 1, n_tiles - 1) * (TR // h), j + col0))


def _tile_spec(col0):
    return pl.BlockSpec((TR, CB), lambda j, i: (i, j + col0))


def _w_spec(kp, col0):
    return pl.BlockSpec((kp, CB), lambda j, i: (0, j + col0))


def _causal_taps(ext_ref, w_ref, k_taps, first, rows):
    acc = None
    for k in range(k_taps):
        t = w_ref[k:k + 1, :] * ext_ref[pl.ds(first - (k_taps - 1) + k, rows), :]
        acc = t if acc is None else acc + t
    return acc


def _anticausal_taps(d_ref, w_ref, k_taps, rows):
    acc = None
    for k in range(k_taps):
        t = w_ref[k:k + 1, :] * d_ref[pl.ds(k_taps - 1 - k, rows), :]
        acc = t if acc is None else acc + t
    return acc


def _acc_conv_wgrad(dw_ref, d_tile, ext_ref, k_taps, first):
    for k in range(k_taps):
        dw_ref[k:k + 1, :] += _colsum(d_tile * ext_ref[pl.ds(first - (k_taps - 1) + k, TR), :])
    dw_ref[k_taps:k_taps + 1, :] += _colsum(d_tile)


def conv_silu_fwd(x, col0, width, w, b, name):
    L = x.shape[0]
    k_taps = w.shape[0]
    h = _halo(k_taps)

    def body(xp_ref, x_ref, w_ref, b_ref, o_ref, ext_ref):
        i = pl.program_id(1)
        ext_ref[0:h, :] = jnp.where(i > 0, xp_ref[...], 0.0)
        ext_ref[h:h + TR, :] = x_ref[...]
        o_ref[...] = _silu(_causal_taps(ext_ref, w_ref, k_taps, h, TR) + b_ref[...])

    return pl.pallas_call(
        body, name=name, grid=(width // CB, L // TR), out_shape=_sds((L, width), F32),
        in_specs=[_prev_spec(h, col0), _tile_spec(col0), _w_spec(k_taps, 0), pl.BlockSpec((1, CB), lambda j, i: (0, j))],
        out_specs=_tile_spec(0), scratch_shapes=[pltpu.VMEM((h + TR, CB), F32)],
        compiler_params=_cparams(("parallel", "parallel")))(x, x, w, b)


def conv_silu_bwd(x, col0, width, w, b, dpost, name):
    L = x.shape[0]
    k_taps = w.shape[0]
    h = _halo(k_taps)
    nt = L // TR

    def body(xp_ref, x_ref, xn_ref, d_ref, dn_ref, w_ref, b_ref, dx_ref, dw_ref, ext_ref, dpre_ref):
        i = pl.program_id(1)

        @pl.when(i == 0)
        def _():
            dw_ref[...] = jnp.zeros((8, CB), F32)

        ext_ref[0:h, :] = jnp.where(i > 0, xp_ref[...], 0.0)
        ext_ref[h:h + TR, :] = x_ref[...]
        ext_ref[h + TR:h + TR + h, :] = xn_ref[...]
        pre = _causal_taps(ext_ref, w_ref, k_taps, h, TR + h) + b_ref[...]
        dpre_ref[0:TR, :] = d_ref[...] * _dsilu(pre[0:TR, :])
        dpre_ref[TR:TR + h, :] = jnp.where(i < nt - 1, dn_ref[...], 0.0) * _dsilu(pre[TR:TR + h, :])
        dx_ref[...] = _anticausal_taps(dpre_ref, w_ref, k_taps, TR).astype(BF16)
        _acc_conv_wgrad(dw_ref, dpre_ref[0:TR, :], ext_ref, k_taps, h)

    return pl.pallas_call(
        body, name=name, grid=(width // CB, nt),
        out_shape=[_sds((L, width), BF16), _sds((8, width), F32)],
        in_specs=[_prev_spec(h, col0), _tile_spec(col0), _next_spec(h, col0, nt), _tile_spec(0), _next_spec(h, 0, nt),
                  _w_spec(k_taps, 0), pl.BlockSpec((1, CB), lambda j, i: (0, j))],
        out_specs=[_tile_spec(0), _w_spec(8, 0)],
        scratch_shapes=[pltpu.VMEM((h + TR + h, CB), F32), pltpu.VMEM((TR + h, CB), F32)],
        compiler_params=_cparams(("parallel", "arbitrary")))(x, x, x, dpost, dpost, w, b)


def conf_conv_fwd(proj, col_a, col_g, w, b, name):
    L = proj.shape[0]
    k_taps = w.shape[0]
    h = _halo(k_taps)

    def body(ap_ref, a_ref, gp_ref, g_ref, w_ref, b_ref, o_ref, ext_ref):
        i = pl.program_id(1)
        ext_ref[0:h, :] = jnp.where(i > 0, ap_ref[...] * _sigmoid(gp_ref[...]), 0.0)
        ext_ref[h:h + TR, :] = a_ref[...] * _sigmoid(g_ref[...])
        o_ref[...] = _causal_taps(ext_ref, w_ref, k_taps, h, TR) + b_ref[...]

    return pl.pallas_call(
        body, name=name, grid=(D_CONF // CB, L // TR), out_shape=_sds((L, D_CONF), F32),
        in_specs=[_prev_spec(h, col_a), _tile_spec(col_a), _prev_spec(h, col_g), _tile_spec(col_g), _w_spec(k_taps, 0),
                  pl.BlockSpec((1, CB), lambda j, i: (0, j))],
        out_specs=_tile_spec(0), scratch_shapes=[pltpu.VMEM((h + TR, CB), F32)],
        compiler_params=_cparams(("parallel", "parallel")))(proj, proj, proj, proj, w, b)


def conf_conv_bwd(proj, col_a, col_g, w, duc, name):
    L = proj.shape[0]
    k_taps = w.shape[0]
    h = _halo(k_taps)
    nt = L // TR

    def body(ap_ref, a_ref, gp_ref, g_ref, d_ref, dn_ref, w_ref, da_ref, dg_ref, dw_ref, ext_ref, dext_ref):
        i = pl.program_id(1)

        @pl.when(i == 0)
        def _():
            dw_ref[...] = jnp.zeros((32, CB), F32)

        a = a_ref[...]
        s = _sigmoid(g_ref[...])
        ext_ref[0:h, :] = jnp.where(i > 0, ap_ref[...] * _sigmoid(gp_ref[...]), 0.0)
        ext_ref[h:h + TR, :] = a * s
        dext_ref[0:TR, :] = d_ref[...]
        dext_ref[TR:TR + h, :] = jnp.where(i < nt - 1, dn_ref[...], 0.0)
        du0 = _anticausal_taps(dext_ref, w_ref, k_taps, TR)
        da_ref[...] = (du0 * s).astype(BF16)
        dg_ref[...] = (du0 * a * s * (1.0 - s)).astype(BF16)
        _acc_conv_wgrad(dw_ref, d_ref[...], ext_ref, k_taps, h)

    return pl.pallas_call(
        body, name=name, grid=(D_CONF // CB, nt),
        out_shape=[_sds((L, D_CONF), BF16), _sds((L, D_CONF), BF16), _sds((32, D_CONF), F32)],
        in_specs=[_prev_spec(h, col_a), _tile_spec(col_a), _prev_spec(h, col_g), _tile_spec(col_g), _tile_spec(0),
                  _next_spec(h, 0, nt), _w_spec(k_taps, 0)],
        out_specs=[_tile_spec(0), _tile_spec(0), _w_spec(32, 0)],
        scratch_shapes=[pltpu.VMEM((h + TR, CB), F32), pltpu.VMEM((TR + h, CB), F32)],
        compiler_params=_cparams(("parallel", "arbitrary")))(proj, proj, proj, proj, duc, duc, w)


def ffn_conv_fwd(up, w, b, name):
    L = up.shape[0]
    k_taps = w.shape[0]
    h = _halo(k_taps)
    cv = D_FF // CB

    def body(gp_ref, g_ref, vp_ref, v_ref, wg_ref, wv_ref, bg_ref, bv_ref, o_ref, eg_ref, ev_ref):
        i = pl.program_id(1)
        eg_ref[0:h, :] = jnp.where(i > 0, gp_ref[...], 0.0)
        eg_ref[h:h + TR, :] = g_ref[...]
        ev_ref[0:h, :] = jnp.where(i > 0, vp_ref[...], 0.0)
        ev_ref[h:h + TR, :] = v_ref[...]
        pg = _causal_taps(eg_ref, wg_ref, k_taps, h, TR) + bg_ref[...]
        pv = _causal_taps(ev_ref, wv_ref, k_taps, h, TR) + bv_ref[...]
        o_ref[...] = (_silu(pg) * pv).astype(BF16)

    bspec = lambda c0: pl.BlockSpec((1, CB), lambda j, i: (0, j + c0))
    return pl.pallas_call(
        body, name=name, grid=(cv, L // TR), out_shape=_sds((L, D_FF), BF16),
        in_specs=[_prev_spec(h, 0), _tile_spec(0), _prev_spec(h, cv), _tile_spec(cv), _w_spec(k_taps, 0), _w_spec(k_taps, cv),
                  bspec(0), bspec(cv)],
        out_specs=_tile_spec(0), scratch_shapes=[pltpu.VMEM((h + TR, CB), F32), pltpu.VMEM((h + TR, CB), F32)],
        compiler_params=_cparams(("parallel", "parallel")))(up, up, up, up, w, w, b, b)


def ffn_conv_bwd(up, w, b, dact, name):
    L = up.shape[0]
    k_taps = w.shape[0]
    h = _halo(k_taps)
    nt = L // TR
    cv = D_FF // CB

    def body(gp_ref, g_ref, gn_ref, vp_ref, v_ref, vn_ref, d_ref, dn_ref, wg_ref, wv_ref, bg_ref, bv_ref,
             dg_ref, dv_ref, dwg_ref, dwv_ref, eg_ref, ev_ref, pg_ref, pv_ref):
        i = pl.program_id(1)

        @pl.when(i == 0)
        def _():
            dwg_ref[...] = jnp.zeros((8, CB), F32)
            dwv_ref[...] = jnp.zeros((8, CB), F32)

        for e_ref, p_ref, c_ref, n_ref in ((eg_ref, gp_ref, g_ref, gn_ref), (ev_ref, vp_ref, v_ref, vn_ref)):
            e_ref[0:h, :] = jnp.where(i > 0, p_ref[...], 0.0)
            e_ref[h:h + TR, :] = c_ref[...]
            e_ref[h + TR:h + TR + h, :] = n_ref[...]
        pg = _causal_taps(eg_ref, wg_ref, k_taps, h, TR + h) + bg_ref[...]
        pv = _causal_taps(ev_ref, wv_ref, k_taps, h, TR + h) + bv_ref[...]
        dact_t = d_ref[...]
        dact_n = jnp.where(i < nt - 1, dn_ref[...], 0.0)
        pg_ref[0:TR, :] = dact_t * pv[0:TR, :] * _dsilu(pg[0:TR, :])
        pg_ref[TR:TR + h, :] = dact_n * pv[TR:TR + h, :] * _dsilu(pg[TR:TR + h, :])
        pv_ref[0:TR, :] = dact_t * _silu(pg[0:TR, :])
        pv_ref[TR:TR + h, :] = dact_n * _silu(pg[TR:TR + h, :])
        dg_ref[...] = _anticausal_taps(pg_ref, wg_ref, k_taps, TR).astype(BF16)
        dv_ref[...] = _anticausal_taps(pv_ref, wv_ref, k_taps, TR).astype(BF16)
        _acc_conv_wgrad(dwg_ref, pg_ref[0:TR, :], eg_ref, k_taps, h)
        _acc_conv_wgrad(dwv_ref, pv_ref[0:TR, :], ev_ref, k_taps, h)

    bspec = lambda c0: pl.BlockSpec((1, CB), lambda j, i: (0, j + c0))
    ext = pltpu.VMEM((h + TR + h, CB), F32)
    dpre = pltpu.VMEM((TR + h, CB), F32)
    return pl.pallas_call(
        body, name=name, grid=(cv, nt),
        out_shape=[_sds((L, D_FF), BF16), _sds((L, D_FF), BF16), _sds((8, D_FF), F32), _sds((8, D_FF), F32)],
        in_specs=[_prev_spec(h, 0), _tile_spec(0), _next_spec(h, 0, nt), _prev_spec(h, cv), _tile_spec(cv), _next_spec(h, cv, nt),
                  _tile_spec(0), _next_spec(h, 0, nt), _w_spec(k_taps, 0), _w_spec(k_taps, cv), bspec(0), bspec(cv)],
        out_specs=[_tile_spec(0), _tile_spec(0), _w_spec(8, 0), _w_spec(8, 0)],
        scratch_shapes=[ext, ext, dpre, dpre],
        compiler_params=_cparams(("parallel", "arbitrary")))(up, up, up, up, up, up, dact, dact, w, w, b, b)


def _ssd_common(xbc_ref, dt_ref, dtb_ref, alog_ref, cs_ref):
    xs = xbc_ref[:, 0:D_SSD]
    sp_in = dt_ref[...] + dtb_ref[...]
    dtf = _softplus(sp_in)
    a_f = -jnp.exp(alog_ref[...])
    a_dt = dtf * a_f
    row = lax.broadcasted_iota(jnp.int32, (Q, Q), 0)
    col = lax.broadcasted_iota(jnp.int32, (Q, Q), 1)
    causal = row >= col
    cs = _dot_hi(causal.astype(F32), a_dt)
    cs_ref[...] = cs
    cs_last = cs_ref[Q - 1:Q, :]
    return xs, sp_in, dtf, a_f, cs, cs_last, causal


def _head_decay(cs_j, cst_ref, e, causal):
    lane = lax.broadcasted_iota(jnp.int32, (Q, LANE), 1)
    rolled = pltpu.roll(cs_j, HEAD, 1)
    own = (lane < HEAD) if e == 0 else (lane >= HEAD)
    col_b = jnp.where(own, cs_j, rolled)
    col_b = jnp.concatenate([col_b] * (Q // LANE), axis=1)
    row_b = cst_ref[e * HEAD:e * HEAD + 1, :]
    return jnp.where(causal, jnp.exp(jnp.minimum(col_b - row_b, 0.0)), 0.0)


def ssd_fwd(xbc, proj, dtb_f, alog_f, dsk_f, snw, name):
    L = xbc.shape[0]
    nc = L // Q

    def body(xbc_ref, z_ref, dt_ref, dtb_ref, alog_ref, dsk_ref, snw_ref, y_ref, yn_ref, sp_ref, s_ref, cs_ref, cst_ref, yd_ref):
        @pl.when(pl.program_id(0) == 0)
        def _():
            s_ref[...] = jnp.zeros((N_STATE, D_SSD), F32)

        xs, _, dtf, a_f, cs, cs_last, causal = _ssd_common(xbc_ref, dt_ref, dtb_ref, alog_ref, cs_ref)
        e_cs = jnp.exp(cs)
        xdt = xs * dtf
        zst = jnp.exp(cs_last - cs) * xdt
        sp_ref[0] = s_ref[...]
        lane = lax.broadcasted_iota(jnp.int32, (Q, LANE), 1)
        for g in range(2):
            gl = slice(g * 512, g * 512 + 512)
            b_g = xbc_ref[:, D_SSD + g * N_STATE:D_SSD + (g + 1) * N_STATE]
            c_g = xbc_ref[:, D_SSD + 2 * N_STATE + g * N_STATE:D_SSD + 2 * N_STATE + (g + 1) * N_STATE]
            s_prev = s_ref[:, gl]
            cb = _dot_nt(c_g, b_g)
            yd_ref[:, gl] = e_cs[:, gl] * _dot(c_g, s_prev)
            for j in range(4):
                tl = slice(g * 512 + j * LANE, g * 512 + (j + 1) * LANE)
                cs_j = cs[:, tl]
                cst_ref[...] = cs_j.T
                x_j = xdt[:, tl]
                o0 = _dot(cb * _head_decay(cs_j, cst_ref, 0, causal), x_j)
                o1 = _dot(cb * _head_decay(cs_j, cst_ref, 1, causal), x_j)
                yd_ref[:, tl] += jnp.where(lane < HEAD, o0, o1)
            s_ref[:, gl] = jnp.exp(cs_last[:, gl]) * s_prev + _dot_tn(b_g, zst[:, gl])
        y = yd_ref[...] + xs * dsk_ref[...]
        y_ref[...] = y
        yz = y * _silu(z_ref[...])
        r = lax.rsqrt(jnp.mean(yz * yz, axis=-1, keepdims=True) + 1e-6)
        yn_ref[...] = (yz * r * snw_ref[...]).astype(BF16)

    chunk = lambda w, c: pl.BlockSpec((Q, w), lambda i: (i, c))
    return pl.pallas_call(
        body, name=name, grid=(nc,),
        out_shape=[_sds((L, D_SSD), F32), _sds((L, D_SSD), BF16), _sds((nc, N_STATE, D_SSD), F32)],
        in_specs=[chunk(D_XBC, 0), chunk(D, 0), chunk(D, 1)] + [_const_spec((1, D))] * 4,
        out_specs=[chunk(D, 0), chunk(D, 0), pl.BlockSpec((1, N_STATE, D_SSD), lambda i: (i, 0, 0))],
        scratch_shapes=[pltpu.VMEM((N_STATE, D_SSD), F32), pltpu.VMEM((Q, D_SSD), F32), pltpu.VMEM((LANE, Q), F32),
                        pltpu.VMEM((Q, D_SSD), F32)],
        compiler_params=_cparams(("arbitrary",)))(xbc, proj, proj, dtb_f, alog_f, dsk_f, snw)


def ssd_bwd(dmixin, y, xbc, proj, s_prev_all, dtb_f, alog_f, dsk_f, snw, name):
    L = xbc.shape[0]
    nc = L // Q

    def body(dyn_ref, y_ref, xbc_ref, z_ref, dt_ref, sp_ref, dtb_ref, alog_ref, dsk_ref, snw_ref,
             dz_ref, ddt_ref, dxbc_ref, acc_ref, acc16_ref, ds_ref, cs_ref, cst_ref, dcs_ref, dx_ref):
        step = pl.program_id(0)

        @pl.when(step == 0)
        def _():
            ds_ref[...] = jnp.zeros((N_STATE, D_SSD), F32)
            acc_ref[...] = jnp.zeros((8, D), F32)

        z = z_ref[...]
        y = y_ref[...]
        sz = _sigmoid(z)
        siluz = z * sz
        yz = y * siluz
        r = lax.rsqrt(jnp.mean(yz * yz, axis=-1, keepdims=True) + 1e-6)
        n = yz * r
        dyn = dyn_ref[...]
        acc_ref[0:1, :] += _colsum(dyn * n)
        dn = dyn * snw_ref[...]
        dyz = r * (dn - n * jnp.mean(dn * n, axis=-1, keepdims=True))
        dy = dyz * siluz
        dz_ref[...] = (dyz * y * (sz * (1.0 + z * (1.0 - sz)))).astype(BF16)

        xs, sp_in, dtf, a_f, cs, cs_last, causal = _ssd_common(xbc_ref, dt_ref, dtb_ref, alog_ref, cs_ref)
        acc_ref[3:4, :] += _colsum(dy * xs)
        e_cs = jnp.exp(cs)
        xdt = xs * dtf
        dst = jnp.exp(cs_last - cs)
        zst = dst * xdt
        e_last = jnp.exp(cs_last)
        lane = lax.broadcasted_iota(jnp.int32, (Q, LANE), 1)
        ones = jnp.ones((Q, LANE), F32)
        dcs_last_parts = []
        for g in range(2):
            gl = slice(g * 512, g * 512 + 512)
            b_g = xbc_ref[:, D_SSD + g * N_STATE:D_SSD + (g + 1) * N_STATE]
            c_g = xbc_ref[:, D_SSD + 2 * N_STATE + g * N_STATE:D_SSD + 2 * N_STATE + (g + 1) * N_STATE]
            s_prev = sp_ref[0, :, gl]
            ds_g = ds_ref[:, gl]
            dy_g = dy[:, gl]
            cb = _dot_nt(c_g, b_g)
            y_off = e_cs[:, gl] * _dot(c_g, s_prev)
            edy = e_cs[:, gl] * dy_g
            d_c = _dot_nt(edy, s_prev)
            d_z = _dot(b_g, ds_g)
            d_b = _dot_nt(zst[:, gl], ds_g)
            t_g = d_z * zst[:, gl]
            dcs_ref[:, gl] = dy_g * y_off - t_g
            dx_ref[:, gl] = d_z * dst[:, gl]
            dcs_last_parts.append(_colsum(t_g) + _colsum(ds_g * s_prev) * e_last[:, gl])
            ds_ref[:, gl] = e_last[:, gl] * ds_g + _dot_tn(c_g, edy)
            dcb = jnp.zeros((Q, Q), F32)
            for j in range(4):
                tl = slice(g * 512 + j * LANE, g * 512 + (j + 1) * LANE)
                cs_j = cs[:, tl]
                cst_ref[...] = cs_j.T
                x_j = xdt[:, tl]
                dy_j = dy[:, tl]
                dx_j = jnp.zeros((Q, LANE), F32)
                dcs_j = jnp.zeros((Q, LANE), F32)
                for e in range(2):
                    own = (lane < HEAD) if e == 0 else (lane >= HEAD)
                    w_h = _head_decay(cs_j, cst_ref, e, causal)
                    g_h = cb * w_h
                    dy_m = jnp.where(own, dy_j, 0.0)
                    d_g = _dot_nt(dy_m, x_j)
                    dx_j = dx_j + _dot_tn(g_h, dy_m)
                    dcb = dcb + d_g * w_h
                    p_h = d_g * g_h
                    dcs_j = dcs_j + jnp.where(own, _dot_hi(p_h, ones) - _dot_tn_hi(p_h, ones), 0.0)
                dcs_ref[:, tl] += dcs_j * (1.0 / HEAD)
                dx_ref[:, tl] += dx_j
            d_c = d_c + _dot(dcb, b_g)
            d_b = d_b + _dot_tn(dcb, c_g)
            dxbc_ref[:, D_SSD + g * N_STATE:D_SSD + (g + 1) * N_STATE] = d_b
            dxbc_ref[:, D_SSD + 2 * N_STATE + g * N_STATE:D_SSD + 2 * N_STATE + (g + 1) * N_STATE] = d_c
        dcs_last = jnp.concatenate(dcs_last_parts, axis=1)
        anticausal = lax.broadcasted_iota(jnp.int32, (Q, Q), 0) <= lax.broadcasted_iota(jnp.int32, (Q, Q), 1)
        d_adt = _dot_hi(anticausal.astype(F32), dcs_ref[...]) + dcs_last
        dx = dx_ref[...]
        acc_ref[2:3, :] += _colsum(d_adt * dtf) * a_f
        d_dtf = d_adt * a_f + dx * xs
        dxbc_ref[:, 0:D_SSD] = dx * dtf + dy * dsk_ref[...]
        d_raw = d_dtf * _sigmoid(sp_in)
        acc_ref[1:2, :] += _colsum(d_raw)
        head_of_lane = lax.broadcasted_iota(jnp.int32, (D_SSD, LANE), 0) // HEAD
        fold = (head_of_lane == lax.broadcasted_iota(jnp.int32, (D_SSD, LANE), 1)).astype(F32)
        ddt_ref[...] = _dot_hi(d_raw, fold).astype(BF16)

        @pl.when(step == nc - 1)
        def _():
            acc16_ref[...] = _dot_hi(acc_ref[...], fold)

    rchunk = lambda w, c: pl.BlockSpec((Q, w), lambda i: (nc - 1 - i, c))
    return pl.pallas_call(
        body, name=name, grid=(nc,),
        out_shape=[_sds((L, D_SSD), BF16), _sds((L, LANE), BF16), _sds((L, D_XBC), F32), _sds((8, D), F32), _sds((8, LANE), F32)],
        in_specs=[rchunk(D, 0), rchunk(D, 0), rchunk(D_XBC, 0), rchunk(D, 0), rchunk(D, 1),
                  pl.BlockSpec((1, N_STATE, D_SSD), lambda i: (nc - 1 - i, 0, 0))] + [_const_spec((1, D))] * 4,
        out_specs=[rchunk(D, 0), rchunk(LANE, 0), rchunk(D_XBC, 0), _const_spec((8, D)), _const_spec((8, LANE))],
        scratch_shapes=[pltpu.VMEM((N_STATE, D_SSD), F32), pltpu.VMEM((Q, D_SSD), F32), pltpu.VMEM((LANE, Q), F32),
                        pltpu.VMEM((Q, D_SSD), F32), pltpu.VMEM((Q, D_SSD), F32)],
        compiler_params=_cparams(("arbitrary",)))(dmixin, y, xbc, proj, proj, s_prev_all, dtb_f, alog_f, dsk_f, snw)


def _adamw_math(w, g, m, v):
    m_n = ADAM_B1 * m + (1.0 - ADAM_B1) * g
    v_n = ADAM_B2 * v + (1.0 - ADAM_B2) * jnp.square(g)
    c1 = 1.0 - ADAM_B1 ** ADAM_STEP
    c2 = 1.0 - ADAM_B2 ** ADAM_STEP
    return -ADAM_LR * ((m_n / c1) / (jnp.sqrt(v_n / c2) + ADAM_EPS) + ADAM_WD * w), m_n, v_n


def _sum_slots(p_ref):
    acc = p_ref[0].astype(F32)
    for s in range(1, N_DEV):
        acc = acc + p_ref[s].astype(F32)
    return acc


def adamw_slots(w, slots, m, v, name):
    rows, cols = w.shape
    tr = next(t for t in (256, 128, 64, 32, 16) if rows % t == 0)

    def body(w_ref, s_ref, m_ref, v_ref, g_ref, d_ref, mo_ref, vo_ref):
        g_v = _sum_slots(s_ref)
        g_ref[...] = g_v
        d_ref[...], mo_ref[...], vo_ref[...] = _adamw_math(w_ref[...], g_v, m_ref[...], v_ref[...])

    spec = pl.BlockSpec((tr, cols), lambda i: (i, 0))
    return pl.pallas_call(body, name=name, grid=(rows // tr,), out_shape=[_sds((rows, cols), F32)] * 4,
                          in_specs=[spec, pl.BlockSpec((N_DEV, tr, cols), lambda i: (0, i, 0)), spec, spec], out_specs=[spec] * 4,
                          compiler_params=_cparams(("parallel",)))(w, slots, m, v)


def adamw_many(ws, gs, ms, vs, name):
    n = len(ws)

    def body(*refs):
        for p in range(n):
            d_v, m_v, v_v = _adamw_math(refs[p][...], refs[n + p][...], refs[2 * n + p][...], refs[3 * n + p][...])
            refs[4 * n + p][...] = d_v
            refs[5 * n + p][...] = m_v
            refs[6 * n + p][...] = v_v

    vm = pl.BlockSpec(memory_space=pltpu.VMEM)
    out = pl.pallas_call(body, name=name, out_shape=[_sds(w.shape, F32) for w in ws] * 3, in_specs=[vm] * (4 * n),
                         out_specs=[vm] * (3 * n), compiler_params=_cparams())(*ws, *gs, *ms, *vs)
    return out[:n], out[n:2 * n], out[2 * n:]


def sum_slots_many(parts, name):
    n = len(parts)

    def body(*refs):
        for p in range(n):
            refs[n + p][...] = _sum_slots(refs[p])

    vm = pl.BlockSpec(memory_space=pltpu.VMEM)
    return pl.pallas_call(body, name=name, out_shape=[_sds(p.shape[1:], F32) for p in parts], in_specs=[vm] * n,
                          out_specs=[vm] * n, compiler_params=_cparams())(*parts)


def ada_mod(c_all, ada_w_shard, ada_b_cols, name):
    def body(c_ref, w_ref, b_ref, o_ref, ca_ref):
        ca = _silu(c_ref[...])
        ca_ref[...] = ca
        o_ref[...] = _dot(ca, w_ref[...]) + b_ref[...]

    vm = pl.BlockSpec(memory_space=pltpu.VMEM)
    return pl.pallas_call(body, name=name, out_shape=[_sds((N_DEV, ada_w_shard.shape[1]), F32), _sds((N_DEV, D), F32)],
                          in_specs=[vm, vm, vm], out_specs=[vm, vm], compiler_params=_cparams())(c_all, ada_w_shard, ada_b_cols)


def ada_wgrad(c_act_all, dmod_cols, name):
    def body(c_ref, d_ref, o_ref):
        o_ref[...] = _dot_tn_hi(c_ref[...], d_ref[...])

    vm = pl.BlockSpec(memory_space=pltpu.VMEM)
    return pl.pallas_call(body, name=name, out_shape=_sds((D, dmod_cols.shape[1]), F32), in_specs=[vm, vm], out_specs=vm,
                          compiler_params=_cparams())(c_act_all, dmod_cols)


def exchange(srcs, name, gather):
    n = len(srcs)
    shapes = [tuple(s.shape) if gather else tuple(s.shape[1:]) for s in srcs]

    def body(*refs):
        src_refs, out_refs = refs[:n], refs[n:2 * n]
        send_sems, recv_sems, local_sems = refs[2 * n:]
        x, y, c = lax.axis_index("x"), lax.axis_index("y"), lax.axis_index("c")
        me = 4 * x + 2 * y + c

        def peer(k):
            bx, by, bc = (k >> 2) & 1, (k >> 1) & 1, k & 1
            px, py, pc = (x + bx) % 2, (y + by) % 2, (c + bc) % 2
            return (px, py, pc), 4 * px + 2 * py + pc

        def copy(a, k, landing):
            dev, idx = peer(k)
            return pltpu.make_async_remote_copy(
                src_ref=src_refs[a] if gather else src_refs[a].at[idx], dst_ref=out_refs[a].at[idx if landing else me],
                send_sem=send_sems.at[a, k - 1], recv_sem=recv_sems.at[a, k - 1],
                device_id=dev, device_id_type=pl.DeviceIdType.MESH)

        mine = [pltpu.make_async_copy(src_refs[a] if gather else src_refs[a].at[me], out_refs[a].at[me], local_sems.at[a])
                for a in range(n)]
        for cp in mine:
            cp.start()
        sends = [copy(a, k, False) for a in range(n) for k in range(1, N_DEV)]
        for cp in sends:
            cp.start()
        for a in range(n):
            for k in range(1, N_DEV):
                copy(a, k, True).wait_recv()
        for cp in sends:
            cp.wait_send()
        for cp in mine:
            cp.wait()

    hbm = pl.BlockSpec(memory_space=pl.ANY)
    return pl.pallas_call(
        body, name=name, out_shape=[_sds((N_DEV,) + shp, s.dtype) for shp, s in zip(shapes, srcs)], in_specs=[hbm] * n,
        out_specs=[hbm] * n,
        scratch_shapes=[pltpu.SemaphoreType.DMA((n, N_DEV - 1)), pltpu.SemaphoreType.DMA((n, N_DEV - 1)),
                        pltpu.SemaphoreType.DMA((n,))],
        compiler_params=pltpu.CompilerParams(has_side_effects=True))(*srcs)


def _peer(k):
    x, y, c = lax.axis_index("x"), lax.axis_index("y"), lax.axis_index("c")
    px, py, pc = (x + ((k >> 2) & 1)) % 2, (y + ((k >> 1) & 1)) % 2, (c + (k & 1)) % 2
    return (px, py, pc), 4 * px + 2 * py + pc


def _my_slot():
    return 4 * lax.axis_index("x") + 2 * lax.axis_index("y") + lax.axis_index("c")


_HBM = pl.BlockSpec(memory_space=pltpu.HBM)
_SEM = pl.BlockSpec(memory_space=pltpu.SEMAPHORE)
_EFFECT = pltpu.SideEffectType.DATAFLOW_SIDE_EFFECTING


def exchange_start(srcs, name, gather):
    n = len(srcs)
    shapes = [tuple(s.shape) if gather else tuple(s.shape[1:]) for s in srcs]
    lands = [lax.empty((N_DEV,) + shp, s.dtype) for shp, s in zip(shapes, srcs)]

    def body(*refs):
        src_refs, land_refs = refs[:n], refs[n:2 * n]
        sems = refs[2 * n:4 * n]
        token = refs[-1]
        me = _my_slot()
        for a in range(n):
            for k in range(1, N_DEV):
                dev, idx = _peer(k)
                pltpu.make_async_remote_copy(
                    src_ref=src_refs[a] if gather else src_refs[a].at[idx], dst_ref=land_refs[a].at[me],
                    send_sem=sems[2 * a].at[k - 1], recv_sem=sems[2 * a + 1].at[k - 1],
                    device_id=dev, device_id_type=pl.DeviceIdType.MESH).start()
        token[...] = jnp.zeros_like(token)

    out_shape = ([pltpu.SemaphoreType.DMA((N_DEV - 1,))] * (2 * n) + [pltpu.HBM(s.shape, s.dtype) for s in srcs]
                 + [pltpu.HBM(l.shape, l.dtype) for l in lands] + [_sds((8, LANE), F32)])
    out = pl.pallas_call(
        body, name=name, out_shape=out_shape, in_specs=[_HBM] * (2 * n),
        out_specs=[_SEM] * (2 * n) + [_HBM] * (2 * n) + [pl.BlockSpec(memory_space=pltpu.VMEM)],
        input_output_aliases={i: 2 * n + i for i in range(2 * n)},
        compiler_params=pltpu.CompilerParams(has_side_effects=_EFFECT))(
            *[pltpu.with_memory_space_constraint(s, pltpu.HBM) for s in srcs],
            *[pltpu.with_memory_space_constraint(l, pltpu.HBM) for l in lands])
    parts = [(out[2 * a], out[2 * a + 1], out[2 * n + a], out[3 * n + a]) for a in range(n)]
    return parts, out[-1]


def exchange_wait(parts, after, name, gather):
    n = len(parts)

    def body(*refs):
        src_refs, land_refs = refs[:n], refs[n:2 * n]
        sems = refs[2 * n:4 * n]
        for a in range(n):
            for k in range(1, N_DEV):
                dev, idx = _peer(k)
                copy = pltpu.make_async_remote_copy(
                    src_ref=src_refs[a] if gather else src_refs[a].at[idx], dst_ref=land_refs[a].at[idx],
                    send_sem=sems[2 * a].at[k - 1], recv_sem=sems[2 * a + 1].at[k - 1],
                    device_id=dev, device_id_type=pl.DeviceIdType.MESH)
                copy.wait_send()
                copy.wait_recv()

    srcs = [p[2] for p in parts]
    lands = [p[3] for p in parts]
    sems = [s for p in parts for s in p[:2]]
    out = pl.pallas_call(
        body, name=name, out_shape=[pltpu.HBM(a.shape, a.dtype) for a in srcs + lands],
        in_specs=[_HBM] * (2 * n) + [_SEM] * (2 * n) + [pl.BlockSpec(memory_space=pl.ANY)], out_specs=[_HBM] * (2 * n),
        input_output_aliases={i: i for i in range(2 * n)},
        compiler_params=pltpu.CompilerParams(has_side_effects=_EFFECT))(*srcs, *lands, *sems, after)
    return out[n:]


def _cols_to_slabs(g):
    r, c = g.shape
    return g.reshape(r, N_DEV, c // N_DEV).transpose(1, 0, 2)


def _slabs_to_cols(s):
    _, r, cs = s.shape
    return s.transpose(1, 0, 2).reshape(r, N_DEV * cs)


def _rep_heads(v):
    return jnp.repeat(v.reshape(N_HEADS), HEAD).reshape(1, D_SSD)


def local_fwd_bwd(x, target, mod, get_w, put_grad, small):
    n1w, n2w, fnw = small["norm1_w"], small["norm2_w"], small["final_norm_w"]
    dtb_f, alog_f, dsk_f = _rep_heads(small["dt_bias"]), _rep_heads(small["a_log"]), _rep_heads(small["d_skip"])
    snw = small["ssd_norm_w"]

    def after(v, token):
        return v + token[0:1, 0:1]

    h1 = norm_mod(x, mod, n1w, 0, "norm1")
    w_in = get_w("w_in", h1)
    proj = mm_nn([(h1, w_in["w_in_p"], 0)], "in_proj")
    xbc = conv_silu_fwd(proj, 4096 // CB, D_XBC, small["ssd_conv_w"], small["ssd_conv_b"], "ssd_conv")
    y, ysn, s_prev = ssd_fwd(xbc, proj, dtb_f, alog_f, dsk_f, snw, "ssd_scan")
    uc = conf_conv_fwd(proj, 2048 // CB, 3072 // CB, small["conf_conv_w"], small["conf_conv_b"], "conf_conv")
    u = ln_silu(uc, small["conf_ln_w"], small["conf_ln_b"], "conf_ln")
    w_out = get_w("w_out", u)
    mix = mm_nn([(ysn, w_out, 0), (u, w_out, 1)], "out_proj")
    h2, x1 = norm_mod(x, mod, n2w, 3, "norm2", res=mix, gate_row=2)
    w_up = get_w("w_up", h2)
    up = mm_nn([(h2, w_up, 0)], "up_proj")
    act = ffn_conv_fwd(up, small["ffn_conv_w"], small["ffn_conv_b"], "ffn_conv")
    w_down = get_w("w_down", act)
    ff = mm_nn([(act, w_down, 0)], "down_proj")
    dx2, dff, acc_f = final_loss(ff, x1, mod, fnw, target, "final_loss")

    token = put_grad("w_down", mm_tn(act, dff, "wgrad_down"))
    dact = mm_nt([(dff, w_down, 0)], "dact")
    dupg, dupv, dwg, dwv = ffn_conv_bwd(up, small["ffn_conv_w"], after(small["ffn_conv_b"], token), dact, "ffn_conv_bwd")
    token = put_grad("w_up", jnp.concatenate([mm_tn(h2, dupg, "wgrad_up_gate"), mm_tn(h2, dupv, "wgrad_up_val")], axis=1))
    dh2 = mm_nt([(dupg, w_up, 0), (dupv, w_up, 1)], "dh2")
    dx1, dmix, acc_2 = norm_mod_bwd(dh2, x1, dx2, mod, after(n2w, token), 3, "norm2_bwd", mix=mix, gate_row=2)

    token = put_grad("w_out", jnp.concatenate([mm_tn(ysn, dmix, "wgrad_out_ssd"), mm_tn(u, dmix, "wgrad_out_conf")], axis=0))
    dmixin = mm_nt([(dmix, w_out, 0)], "dmixin")
    duc, acc_ln = ln_silu_bwd(dmixin, uc, after(small["conf_ln_w"], token), small["conf_ln_b"], "conf_ln_bwd")
    dcfa, dcfg, dw_cc = conf_conv_bwd(proj, 2048 // CB, 3072 // CB, small["conf_conv_w"], duc, "conf_conv_bwd")
    dz, ddt, dxbc_post, acc_s, acc_s16 = ssd_bwd(dmixin, y, xbc, proj, s_prev, dtb_f, alog_f, dsk_f, snw, "ssd_scan_bwd")
    dxbc, dw_sc = conv_silu_bwd(proj, 4096 // CB, D_XBC, small["ssd_conv_w"], small["ssd_conv_b"], dxbc_post, "ssd_conv_bwd")
    token = put_grad("w_in", jnp.concatenate(
        [mm_tn(h1, dz, "wgrad_in_z"), mm_tn(h1, dxbc, "wgrad_in_xbc"), mm_tn(h1, ddt, "wgrad_in_dt")[:, :N_HEADS],
         mm_tn(h1, dcfa, "wgrad_in_cfa"), mm_tn(h1, dcfg, "wgrad_in_cfg")], axis=1))
    dh1 = mm_nt([(dz, w_in["w_z"], 0), (ddt, w_in["w_dt16"], 0), (dcfa, w_in["w_cfa"], 0), (dcfg, w_in["w_cfg"], 0),
                 (dxbc, w_in["w_xbc"], 0)], "dh1")
    grad_x, acc_1 = norm_mod_bwd(dh1, x, dx1, mod, after(n1w, token), 0, "norm1_bwd")

    small_accs = dict(acc_1=acc_1, acc_2=acc_2, acc_f=acc_f, acc_ln=acc_ln, acc_s=acc_s, acc_s16=acc_s16, dw_sc=dw_sc,
                      dw_cc=dw_cc, dwg=dwg, dwv=dwv)
    return grad_x, small_accs


def kernel(x, c, ada_w, ada_b, norm1_w, w_in, ssd_conv_w, ssd_conv_b, dt_bias, a_log, d_skip, ssd_norm_w, conf_conv_w, conf_conv_b, conf_ln_w, conf_ln_b, w_out, norm2_w, w_up, ffn_conv_w, ffn_conv_b, w_down, final_norm_w, loss_target, m_ada_w, m_ada_b, m_norm1_w, m_w_in, m_ssd_conv_w, m_ssd_conv_b, m_dt_bias, m_a_log, m_d_skip, m_ssd_norm_w, m_conf_conv_w, m_conf_conv_b, m_conf_ln_w, m_conf_ln_b, m_w_out, m_norm2_w, m_w_up, m_ffn_conv_w, m_ffn_conv_b, m_w_down, m_final_norm_w, v_ada_w, v_ada_b, v_norm1_w, v_w_in, v_ssd_conv_w, v_ssd_conv_b, v_dt_bias, v_a_log, v_d_skip, v_ssd_norm_w, v_conf_conv_w, v_conf_conv_b, v_conf_ln_w, v_conf_ln_b, v_w_out, v_norm2_w, v_w_up, v_ffn_conv_w, v_ffn_conv_b, v_w_down, v_final_norm_w):
    me = 4 * lax.axis_index("x") + 2 * lax.axis_index("y") + lax.axis_index("c")
    weights = dict(ada_w=ada_w, ada_b=ada_b, norm1_w=norm1_w, w_in=w_in, ssd_conv_w=ssd_conv_w, ssd_conv_b=ssd_conv_b,
                   dt_bias=dt_bias, a_log=a_log, d_skip=d_skip, ssd_norm_w=ssd_norm_w, conf_conv_w=conf_conv_w,
                   conf_conv_b=conf_conv_b, conf_ln_w=conf_ln_w, conf_ln_b=conf_ln_b, w_out=w_out, norm2_w=norm2_w, w_up=w_up,
                   ffn_conv_w=ffn_conv_w, ffn_conv_b=ffn_conv_b, w_down=w_down, final_norm_w=final_norm_w)
    moms_m = dict(ada_w=m_ada_w, ada_b=m_ada_b, norm1_w=m_norm1_w, w_in=m_w_in, ssd_conv_w=m_ssd_conv_w, ssd_conv_b=m_ssd_conv_b,
                  dt_bias=m_dt_bias, a_log=m_a_log, d_skip=m_d_skip, ssd_norm_w=m_ssd_norm_w, conf_conv_w=m_conf_conv_w,
                  conf_conv_b=m_conf_conv_b, conf_ln_w=m_conf_ln_w, conf_ln_b=m_conf_ln_b, w_out=m_w_out, norm2_w=m_norm2_w,
                  w_up=m_w_up, ffn_conv_w=m_ffn_conv_w, ffn_conv_b=m_ffn_conv_b, w_down=m_w_down, final_norm_w=m_final_norm_w)
    moms_v = dict(ada_w=v_ada_w, ada_b=v_ada_b, norm1_w=v_norm1_w, w_in=v_w_in, ssd_conv_w=v_ssd_conv_w, ssd_conv_b=v_ssd_conv_b,
                  dt_bias=v_dt_bias, a_log=v_a_log, d_skip=v_d_skip, ssd_norm_w=v_ssd_norm_w, conf_conv_w=v_conf_conv_w,
                  conf_conv_b=v_conf_conv_b, conf_ln_w=v_conf_ln_w, conf_ln_b=v_conf_ln_b, w_out=v_w_out, norm2_w=v_norm2_w,
                  w_up=v_w_up, ffn_conv_w=v_ffn_conv_w, ffn_conv_b=v_ffn_conv_b, w_down=v_w_down, final_norm_w=v_final_norm_w)
    names = list(weights)

    def to2d(a):
        return a[0] if a.ndim == 3 else a.reshape(1, -1)

    big = ("w_in", "w_out", "w_up", "w_down")
    shards = [weights[n][0].astype(BF16) for n in big]
    gather_parts, token = exchange_start(shards, "gather_weights_start", gather=True)

    c_all, scw_all, ccw_all, fcw_all = exchange([c.reshape(8, LANE) + token[0:1, 0:1], ssd_conv_w[0], conf_conv_w[0], ffn_conv_w[0]],
                                                "gather_small", gather=True)
    c_all = c_all.reshape(N_DEV, D)

    ada_cols = ada_w.shape[2]
    ada_b_cols = lax.dynamic_slice(ada_b, (0, me * ada_cols), (1, ada_cols))
    mod_cols, c_act_all = ada_mod(c_all, ada_w[0], ada_b_cols, "ada_mod")
    mod_parts, = exchange([jnp.pad(mod_cols, ((0, 0), (0, D - ada_cols))).reshape(N_DEV, 8, LANE)], "scatter_mod", gather=False)
    mod = mod_parts.reshape(N_DEV, D)[:, :ada_cols].reshape(6, D)
    mod = jnp.pad(mod, ((0, 2), (0, 0)))

    small = {n: to2d(weights[n]) for n in names if n not in ("ada_w",) + big}
    small["ssd_conv_w"] = _slabs_to_cols(scw_all)
    small["conf_conv_w"] = _slabs_to_cols(ccw_all)
    small["ffn_conv_w"] = _slabs_to_cols(fcw_all)

    def with_own(landed, own):
        return lax.dynamic_update_slice(landed, own[None], (me,) + (0,) * own.ndim)

    def get_w(n, after):
        a = big.index(n)
        landed, = exchange_wait([gather_parts[a]], after, "gather_" + n + "_wait", gather=True)
        slabs = with_own(landed, shards[a])
        if n == "w_out":
            return slabs.reshape(2 * D, D)
        if n == "w_down":
            return slabs.reshape(D_FF, D)
        full = _slabs_to_cols(slabs)
        if n == "w_up":
            return full
        w_z, w_xbc, w_dt, w_cfa, w_cfg = full[:, :1024], full[:, 1024:2560], full[:, 2560:2576], full[:, 2576:3600], full[:, 3600:]
        return dict(w_in_p=jnp.concatenate([w_z, jnp.repeat(w_dt, HEAD, axis=1), w_cfa, w_cfg, w_xbc], axis=1), w_z=w_z,
                    w_xbc=w_xbc, w_dt16=jnp.pad(w_dt, ((0, 0), (0, LANE - N_HEADS))), w_cfa=w_cfa, w_cfg=w_cfg)

    scatter_parts, sent = {}, {}

    def put_grad(n, g):
        slabs = _cols_to_slabs(g) if n in ("w_in", "w_up") else g.reshape(N_DEV, g.shape[0] // N_DEV, g.shape[1])
        sent[n] = slabs.astype(BF16)
        (scatter_parts[n],), token = exchange_start([sent[n]], "scatter_" + n + "_start", gather=False)
        return token

    grad_x, accs = local_fwd_bwd(x[0], loss_target[0], mod, get_w, put_grad, small)
    loss = lax.psum(0.5 / D * jnp.sum(accs["acc_f"][2:3]), ("x", "y", "c"))

    landed = exchange_wait([scatter_parts[n] for n in big], grad_x, "scatter_grads_wait", gather=False)
    grads, delta, new_m, new_v = {}, {}, {}, {}
    for n, slots in zip(big, landed):
        slots = with_own(slots, lax.dynamic_index_in_dim(sent[n], me, 0, keepdims=False))
        grads[n], delta[n], new_m[n], new_v[n] = adamw_slots(weights[n][0], slots, moms_m[n][0], moms_v[n][0], "adamw_" + n)

    order = ("acc_1", "acc_2", "acc_f", "acc_ln", "acc_s", "acc_s16", "dw_sc", "dw_cc", "dwg", "dwv")
    gathered = dict(zip(order, exchange([accs[k] for k in order], "gather_small_grads", gather=True)))
    red = dict(zip(order, sum_slots_many([gathered[k] for k in order], "sum_small_grads")))

    def mod_rows(a1, a2, af):
        return jnp.concatenate([a1[..., 0:2, :], a2[..., 3:4, :], a2[..., 0:2, :], af[..., 1:2, :]], axis=-2)

    dmod_all = mod_rows(gathered["acc_1"], gathered["acc_2"], gathered["acc_f"]).reshape(N_DEV, 6 * D)
    grads["ada_w"] = ada_wgrad(c_act_all, lax.dynamic_slice(dmod_all, (0, me * ada_cols), (N_DEV, ada_cols)), "ada_wgrad")

    def my_cols(full, k_taps):
        cols = full.shape[1] // N_DEV
        return lax.dynamic_slice(full, (0, me * cols), (k_taps, cols))

    fcw = jnp.concatenate([red["dwg"], red["dwv"]], axis=1)
    grads.update(
        ada_b=mod_rows(red["acc_1"], red["acc_2"], red["acc_f"]).reshape(1, 6 * D), norm1_w=red["acc_1"][2:3],
        ssd_conv_w=my_cols(red["dw_sc"], K_SSD), ssd_conv_b=red["dw_sc"][K_SSD:K_SSD + 1],
        dt_bias=red["acc_s16"][1:2, :N_HEADS], a_log=red["acc_s16"][2:3, :N_HEADS], d_skip=red["acc_s16"][3:4, :N_HEADS],
        ssd_norm_w=red["acc_s"][0:1], conf_conv_w=my_cols(red["dw_cc"], K_CONF), conf_conv_b=red["dw_cc"][K_CONF:K_CONF + 1],
        conf_ln_w=red["acc_ln"][0:1], conf_ln_b=red["acc_ln"][1:2], norm2_w=red["acc_2"][2:3],
        ffn_conv_w=my_cols(fcw, K_FFN), ffn_conv_b=fcw[K_FFN:K_FFN + 1], final_norm_w=red["acc_f"][0:1])

    rest = [n for n in names if n not in big]
    d_l, m_l, v_l = adamw_many([to2d(weights[n]) for n in rest], [grads[n] for n in rest], [to2d(moms_m[n]) for n in rest],
                               [to2d(moms_v[n]) for n in rest], "adamw_small")
    for n, dd, mm, vv in zip(rest, d_l, m_l, v_l):
        delta[n], new_m[n], new_v[n] = dd, mm, vv
    shape_of = lambda d_: {n: d_[n].reshape(weights[n].shape) for n in names}
    grads, delta, new_m, new_v = shape_of(grads), shape_of(delta), shape_of(new_m), shape_of(new_v)
    return (loss, grad_x[None], *[grads[n] for n in names], *[delta[n] for n in names], *[new_m[n] for n in names],
            *[new_v[n] for n in names])
```

```python
import functools

import jax
import jax.numpy as jnp
from jax import lax
from jax.experimental import pallas as pl
from jax.experimental.pallas import tpu as pltpu

F32 = jnp.float32
BF16 = jnp.bfloat16
HI = lax.Precision.HIGHEST

N_DEV = 8
D = 1024
D_SSD = 1024
HEAD = 64
N_HEADS = 16
N_STATE = 128
D_XBC = 1536
D_CONF = 1024
D_FF = 2816
K_SSD, K_CONF, K_FFN = 4, 31, 3
D_INP = 5632
LANE = 128
TR = 256
Q = 256
CB = 256
VMEM_LIMIT = 56 * 1024 * 1024

ADAM_LR, ADAM_B1, ADAM_B2, ADAM_EPS, ADAM_WD, ADAM_STEP = 0.001, 0.9, 0.999, 1e-08, 0.01, 10


def _cparams(sem=None):
    return pltpu.CompilerParams(vmem_limit_bytes=VMEM_LIMIT, dimension_semantics=sem)


def _sds(shape, dtype):
    return jax.ShapeDtypeStruct(shape, dtype)


def _sigmoid(x):
    return 1.0 / (1.0 + jnp.exp(-x))


def _silu(x):
    return x * _sigmoid(x)


def _dsilu(x):
    s = _sigmoid(x)
    return s * (1.0 + x * (1.0 - s))


def _softplus(x):
    return jnp.maximum(x, 0.0) + jnp.log(1.0 + jnp.exp(-jnp.abs(x)))


def _dot(a, b):
    return jnp.dot(a.astype(BF16), b.astype(BF16), preferred_element_type=F32)


def _dot_nt(a, b):
    return lax.dot_general(a.astype(BF16), b.astype(BF16), (((1,), (1,)), ((), ())), preferred_element_type=F32)


def _dot_tn(a, b):
    return lax.dot_general(a.astype(BF16), b.astype(BF16), (((0,), (0,)), ((), ())), preferred_element_type=F32)


def _dot_hi(a, b):
    return jnp.dot(a, b, precision=HI, preferred_element_type=F32)


def _dot_tn_hi(a, b):
    return lax.dot_general(a, b, (((0,), (0,)), ((), ())), precision=HI, preferred_element_type=F32)


def _colsum(x):
    return jnp.sum(x, axis=0, keepdims=True)


def _const_spec(shape):
    return pl.BlockSpec(shape, lambda *_: (0,) * len(shape))


def _col_tile(n):
    for t in (1408, 1024, 768, 512, 256, 128):
        if n % t == 0 and t <= n:
            return t
    return n


def mm_nn(pairs, name):
    L = pairs[0][0].shape[0]
    N = pairs[0][1].shape[1]
    tn = _col_tile(N)
    n = len(pairs)

    def body(*refs):
        acc = None
        for p in range(n):
            t = jnp.dot(refs[2 * p][...], refs[2 * p + 1][...], preferred_element_type=F32)
            acc = t if acc is None else acc + t
        refs[-1][...] = acc

    in_specs, args = [], []
    for a, w, rb in pairs:
        in_specs += [pl.BlockSpec((TR, a.shape[1]), lambda j, i: (i, 0)),
                     pl.BlockSpec((a.shape[1], tn), functools.partial(lambda j, i, rb: (rb, j), rb=rb))]
        args += [a, w]
    return pl.pallas_call(
        body, name=name, grid=(N // tn, L // TR), out_shape=_sds((L, N), F32), in_specs=in_specs,
        out_specs=pl.BlockSpec((TR, tn), lambda j, i: (i, j)),
        compiler_params=_cparams(("parallel", "parallel")))(*args)


def mm_nt(pairs, name):
    L = pairs[0][0].shape[0]
    K = pairs[0][1].shape[0]
    tk = _col_tile(K)
    n = len(pairs)

    def body(*refs):
        o_ref = refs[-1]
        acc = None
        for p in range(n):
            t = lax.dot_general(refs[2 * p][...], refs[2 * p + 1][...], (((1,), (1,)), ((), ())),
                                preferred_element_type=F32)
            acc = t if acc is None else acc + t
        o_ref[...] = acc

    in_specs, args = [], []
    for a, w, cb in pairs:
        in_specs += [pl.BlockSpec((TR, a.shape[1]), lambda j, i: (i, 0)),
                     pl.BlockSpec((tk, a.shape[1]), functools.partial(lambda j, i, cb: (j, cb), cb=cb))]
        args += [a, w]
    return pl.pallas_call(
        body, name=name, grid=(K // tk, L // TR), out_shape=_sds((L, K), F32), in_specs=in_specs,
        out_specs=pl.BlockSpec((TR, tk), lambda j, i: (i, j)),
        compiler_params=_cparams(("parallel", "parallel")))(*args)


def mm_tn(a, g, name):
    L, M = a.shape
    N = g.shape[1]
    tn = _col_tile(N) if N > 1024 else N
    if M * tn * 4 > 8 * 1024 * 1024:
        tn = 512
    tl = 512 if L % 512 == 0 else TR

    def body(a_ref, g_ref, o_ref):
        @pl.when(pl.program_id(1) == 0)
        def _():
            o_ref[...] = jnp.zeros((M, tn), F32)

        o_ref[...] += lax.dot_general(a_ref[...], g_ref[...], (((0,), (0,)), ((), ())), preferred_element_type=F32)

    return pl.pallas_call(
        body, name=name, grid=(N // tn, L // tl), out_shape=_sds((M, N), F32),
        in_specs=[pl.BlockSpec((tl, M), lambda j, l: (l, 0)), pl.BlockSpec((tl, tn), lambda j, l: (l, j))],
        out_specs=pl.BlockSpec((M, tn), lambda j, l: (0, j)),
        compiler_params=_cparams(("parallel", "arbitrary")))(a, g)


def _row_spec(width=D):
    return pl.BlockSpec((TR, width), lambda i: (i, 0))


def _row_col_spec(width, col):
    return pl.BlockSpec((TR, width), lambda i: (i, col))


def norm_mod(x, mod, w, shift_row, name, res=None, gate_row=None):
    L = x.shape[0]
    has_res = res is not None

    def body(*refs):
        if has_res:
            x_ref, res_ref, mod_ref, w_ref, h_ref, xo_ref = refs
            xin = x_ref[...] + mod_ref[gate_row:gate_row + 1, :] * res_ref[...]
            xo_ref[...] = xin
        else:
            x_ref, mod_ref, w_ref, h_ref = refs
            xin = x_ref[...]
        r = lax.rsqrt(jnp.mean(xin * xin, axis=-1, keepdims=True) + 1e-6)
        h = (xin * r * w_ref[...]) * (1.0 + mod_ref[shift_row + 1:shift_row + 2, :]) + mod_ref[shift_row:shift_row + 1, :]
        h_ref[...] = h.astype(BF16)

    ins = [x] + ([res] if has_res else []) + [mod, w]
    in_specs = [_row_spec()] + ([_row_spec()] if has_res else []) + [_const_spec((8, D)), _const_spec((1, D))]
    out_shape = [_sds((L, D), BF16)] + ([_sds((L, D), F32)] if has_res else [])
    out_specs = [_row_spec()] + ([_row_spec()] if has_res else [])
    out = pl.pallas_call(body, name=name, grid=(L // TR,), out_shape=out_shape, in_specs=in_specs,
                         out_specs=out_specs, compiler_params=_cparams(("parallel",)))(*ins)
    return out if has_res else out[0]


def ln_silu(uc, lnw, lnb, name):
    L = uc.shape[0]

    def body(u_ref, w_ref, b_ref, o_ref):
        u = u_ref[...]
        mu = jnp.mean(u, axis=-1, keepdims=True)
        var = jnp.mean(jnp.square(u - mu), axis=-1, keepdims=True)
        v = (u - mu) * lax.rsqrt(var + 1e-5) * w_ref[...] + b_ref[...]
        o_ref[...] = _silu(v).astype(BF16)

    return pl.pallas_call(body, name=name, grid=(L // TR,), out_shape=_sds((L, D_CONF), BF16),
                          in_specs=[_row_spec(), _const_spec((1, D)), _const_spec((1, D))], out_specs=_row_spec(),
                          compiler_params=_cparams(("parallel",)))(uc, lnw, lnb)


def ln_silu_bwd(du_all, uc, lnw, lnb, name):
    L = uc.shape[0]

    def body(du_ref, u_ref, w_ref, b_ref, o_ref, acc_ref):
        @pl.when(pl.program_id(0) == 0)
        def _():
            acc_ref[...] = jnp.zeros((8, D), F32)

        u = u_ref[...]
        mu = jnp.mean(u, axis=-1, keepdims=True)
        rl = lax.rsqrt(jnp.mean(jnp.square(u - mu), axis=-1, keepdims=True) + 1e-5)
        n = (u - mu) * rl
        v = n * w_ref[...] + b_ref[...]
        dv = du_ref[...] * _dsilu(v)
        acc_ref[0:1, :] += _colsum(dv * n)
        acc_ref[1:2, :] += _colsum(dv)
        dn = dv * w_ref[...]
        o_ref[...] = rl * (dn - jnp.mean(dn, axis=-1, keepdims=True) - n * jnp.mean(dn * n, axis=-1, keepdims=True))

    return pl.pallas_call(body, name=name, grid=(L // TR,), out_shape=[_sds((L, D), F32), _sds((8, D), F32)],
                          in_specs=[_row_col_spec(D, 1), _row_spec(), _const_spec((1, D)), _const_spec((1, D))],
                          out_specs=[_row_spec(), _const_spec((8, D))],
                          compiler_params=_cparams(("arbitrary",)))(du_all, uc, lnw, lnb)


def final_loss(ff, x1, mod, fw, target, name):
    L = ff.shape[0]

    def body(ff_ref, x1_ref, mod_ref, fw_ref, t_ref, dx_ref, dff_ref, acc_ref):
        @pl.when(pl.program_id(0) == 0)
        def _():
            acc_ref[...] = jnp.zeros((8, D), F32)

        ff_v = ff_ref[...]
        g2 = mod_ref[5:6, :]
        x2 = x1_ref[...] + g2 * ff_v
        r = lax.rsqrt(jnp.mean(x2 * x2, axis=-1, keepdims=True) + 1e-6)
        n = x2 * r
        err = n * fw_ref[...] - t_ref[...]
        dy = err * (1.0 / D)
        dn = dy * fw_ref[...]
        dx2 = r * (dn - n * jnp.mean(dn * n, axis=-1, keepdims=True))
        acc_ref[0:1, :] += _colsum(dy * n)
        acc_ref[1:2, :] += _colsum(dx2 * ff_v)
        acc_ref[2:3, :] += _colsum(err * err)
        dx_ref[...] = dx2
        dff_ref[...] = (dx2 * g2).astype(BF16)

    return pl.pallas_call(
        body, name=name, grid=(L // TR,), out_shape=[_sds((L, D), F32), _sds((L, D), BF16), _sds((8, D), F32)],
        in_specs=[_row_spec(), _row_spec(), _const_spec((8, D)), _const_spec((1, D)), _row_spec()],
        out_specs=[_row_spec(), _row_spec(), _const_spec((8, D))],
        compiler_params=_cparams(("arbitrary",)))(ff, x1, mod, fw, target)


def norm_mod_bwd(dh, xin, dres, mod, w, shift_row, name, mix=None, gate_row=None):
    L = dh.shape[0]
    has_mix = mix is not None

    def body(*refs):
        if has_mix:
            dh_ref, x_ref, dres_ref, mod_ref, w_ref, mix_ref, dx_ref, dmix_ref, acc_ref = refs
        else:
            dh_ref, x_ref, dres_ref, mod_ref, w_ref, dx_ref, acc_ref = refs

        @pl.when(pl.program_id(0) == 0)
        def _():
            acc_ref[...] = jnp.zeros((8, D), F32)

        dh_v = dh_ref[...]
        x = x_ref[...]
        r = lax.rsqrt(jnp.mean(x * x, axis=-1, keepdims=True) + 1e-6)
        n = x * r
        nw = n * w_ref[...]
        sc1 = 1.0 + mod_ref[shift_row + 1:shift_row + 2, :]
        acc_ref[0:1, :] += _colsum(dh_v)
        acc_ref[1:2, :] += _colsum(dh_v * nw)
        dnw = dh_v * sc1
        acc_ref[2:3, :] += _colsum(dnw * n)
        dn = dnw * w_ref[...]
        dx = r * (dn - n * jnp.mean(dn * n, axis=-1, keepdims=True)) + dres_ref[...]
        dx_ref[...] = dx
        if has_mix:
            acc_ref[3:4, :] += _colsum(dx * mix_ref[...])
            dmix_ref[...] = (dx * mod_ref[gate_row:gate_row + 1, :]).astype(BF16)

    ins = [dh, xin, dres, mod, w] + ([mix] if has_mix else [])
    in_specs = [_row_spec(), _row_spec(), _row_spec(), _const_spec((8, D)), _const_spec((1, D))] + ([_row_spec()] if has_mix else [])
    out_shape = [_sds((L, D), F32)] + ([_sds((L, D), BF16)] if has_mix else []) + [_sds((8, D), F32)]
    out_specs = [_row_spec()] + ([_row_spec()] if has_mix else []) + [_const_spec((8, D))]
    return pl.pallas_call(body, name=name, grid=(L // TR,), out_shape=out_shape, in_specs=in_specs,
                          out_specs=out_specs, compiler_params=_cparams(("arbitrary",)))(*ins)


def _halo(k):
    return 8 if k <= 9 else 32


def _prev_spec(h, col0):
    return pl.BlockSpec((h, CB), lambda j, i: (jnp.maximum(i * (TR // h) - 1, 0), j + col0))


def _next_spec(h, col0, n_tiles):
    return pl.BlockSpec((h, CB), lambda j, i: (jnp.minimum(i + 1, n_tiles - 1) * (TR // h), j + col0))


def _tile_spec(col0):
    return pl.BlockSpec((TR, CB), lambda j, i: (i, j + col0))


def _w_spec(kp, col0):
    return pl.BlockSpec((kp, CB), lambda j, i: (0, j + col0))


def _causal_taps(ext_ref, w_ref, k_taps, first, rows):
    acc = None
    for k in range(k_taps):
        t = w_ref[k:k + 1, :] * ext_ref[pl.ds(first - (k_taps - 1) + k, rows), :]
        acc = t if acc is None else acc + t
    return acc


def _anticausal_taps(d_ref, w_ref, k_taps, rows):
    acc = None
    for k in range(k_taps):
        t = w_ref[k:k + 1, :] * d_ref[pl.ds(k_taps - 1 - k, rows), :]
        acc = t if acc is None else acc + t
    return acc


def _acc_conv_wgrad(dw_ref, d_tile, ext_ref, k_taps, first):
    for k in range(k_taps):
        dw_ref[k:k + 1, :] += _colsum(d_tile * ext_ref[pl.ds(first - (k_taps - 1) + k, TR), :])
    dw_ref[k_taps:k_taps + 1, :] += _colsum(d_tile)


def conv_silu_fwd(x, col0, width, w, b, name):
    L = x.shape[0]
    k_taps = w.shape[0]
    h = _halo(k_taps)

    def body(xp_ref, x_ref, w_ref, b_ref, o_ref, ext_ref):
        i = pl.program_id(1)
        ext_ref[0:h, :] = jnp.where(i > 0, xp_ref[...], 0.0)
        ext_ref[h:h + TR, :] = x_ref[...]
        o_ref[...] = _silu(_causal_taps(ext_ref, w_ref, k_taps, h, TR) + b_ref[...])

    return pl.pallas_call(
        body, name=name, grid=(width // CB, L // TR), out_shape=_sds((L, width), F32),
        in_specs=[_prev_spec(h, col0), _tile_spec(col0), _w_spec(k_taps, 0), pl.BlockSpec((1, CB), lambda j, i: (0, j))],
        out_specs=_tile_spec(0), scratch_shapes=[pltpu.VMEM((h + TR, CB), F32)],
        compiler_params=_cparams(("parallel", "parallel")))(x, x, w, b)


def conv_silu_bwd(x, col0, width, w, b, dpost, name):
    L = x.shape[0]
    k_taps = w.shape[0]
    h = _halo(k_taps)
    nt = L // TR

    def body(xp_ref, x_ref, xn_ref, d_ref, dn_ref, w_ref, b_ref, dx_ref, dw_ref, ext_ref, dpre_ref):
        i = pl.program_id(1)

        @pl.when(i == 0)
        def _():
            dw_ref[...] = jnp.zeros((8, CB), F32)

        ext_ref[0:h, :] = jnp.where(i > 0, xp_ref[...], 0.0)
        ext_ref[h:h + TR, :] = x_ref[...]
        ext_ref[h + TR:h + TR + h, :] = xn_ref[...]
        pre = _causal_taps(ext_ref, w_ref, k_taps, h, TR + h) + b_ref[...]
        dpre_ref[0:TR, :] = d_ref[...] * _dsilu(pre[0:TR, :])
        dpre_ref[TR:TR + h, :] = jnp.where(i < nt - 1, dn_ref[...], 0.0) * _dsilu(pre[TR:TR + h, :])
        dx_ref[...] = _anticausal_taps(dpre_ref, w_ref, k_taps, TR).astype(BF16)
        _acc_conv_wgrad(dw_ref, dpre_ref[0:TR, :], ext_ref, k_taps, h)

    return pl.pallas_call(
        body, name=name, grid=(width // CB, nt),
        out_shape=[_sds((L, width), BF16), _sds((8, width), F32)],
        in_specs=[_prev_spec(h, col0), _tile_spec(col0), _next_spec(h, col0, nt), _tile_spec(0), _next_spec(h, 0, nt),
                  _w_spec(k_taps, 0), pl.BlockSpec((1, CB), lambda j, i: (0, j))],
        out_specs=[_tile_spec(0), _w_spec(8, 0)],
        scratch_shapes=[pltpu.VMEM((h + TR + h, CB), F32), pltpu.VMEM((TR + h, CB), F32)],
        compiler_params=_cparams(("parallel", "arbitrary")))(x, x, x, dpost, dpost, w, b)


def conf_conv_fwd(proj, col_a, col_g, w, b, name):
    L = proj.shape[0]
    k_taps = w.shape[0]
    h = _halo(k_taps)

    def body(ap_ref, a_ref, gp_ref, g_ref, w_ref, b_ref, o_ref, ext_ref):
        i = pl.program_id(1)
        ext_ref[0:h, :] = jnp.where(i > 0, ap_ref[...] * _sigmoid(gp_ref[...]), 0.0)
        ext_ref[h:h + TR, :] = a_ref[...] * _sigmoid(g_ref[...])
        o_ref[...] = _causal_taps(ext_ref, w_ref, k_taps, h, TR) + b_ref[...]

    return pl.pallas_call(
        body, name=name, grid=(D_CONF // CB, L // TR), out_shape=_sds((L, D_CONF), F32),
        in_specs=[_prev_spec(h, col_a), _tile_spec(col_a), _prev_spec(h, col_g), _tile_spec(col_g), _w_spec(k_taps, 0),
                  pl.BlockSpec((1, CB), lambda j, i: (0, j))],
        out_specs=_tile_spec(0), scratch_shapes=[pltpu.VMEM((h + TR, CB), F32)],
        compiler_params=_cparams(("parallel", "parallel")))(proj, proj, proj, proj, w, b)


def conf_conv_bwd(proj, col_a, col_g, w, duc, name):
    L = proj.shape[0]
    k_taps = w.shape[0]
    h = _halo(k_taps)
    nt = L // TR

    def body(ap_ref, a_ref, gp_ref, g_ref, d_ref, dn_ref, w_ref, da_ref, dg_ref, dw_ref, ext_ref, dext_ref):
        i = pl.program_id(1)

        @pl.when(i == 0)
        def _():
            dw_ref[...] = jnp.zeros((32, CB), F32)

        a = a_ref[...]
        s = _sigmoid(g_ref[...])
        ext_ref[0:h, :] = jnp.where(i > 0, ap_ref[...] * _sigmoid(gp_ref[...]), 0.0)
        ext_ref[h:h + TR, :] = a * s
        dext_ref[0:TR, :] = d_ref[...]
        dext_ref[TR:TR + h, :] = jnp.where(i < nt - 1, dn_ref[...], 0.0)
        du0 = _anticausal_taps(dext_ref, w_ref, k_taps, TR)
        da_ref[...] = (du0 * s).astype(BF16)
        dg_ref[...] = (du0 * a * s * (1.0 - s)).astype(BF16)
        _acc_conv_wgrad(dw_ref, d_ref[...], ext_ref, k_taps, h)

    return pl.pallas_call(
        body, name=name, grid=(D_CONF // CB, nt),
        out_shape=[_sds((L, D_CONF), BF16), _sds((L, D_CONF), BF16), _sds((32, D_CONF), F32)],
        in_specs=[_prev_spec(h, col_a), _tile_spec(col_a), _prev_spec(h, col_g), _tile_spec(col_g), _tile_spec(0),
                  _next_spec(h, 0, nt), _w_spec(k_taps, 0)],
        out_specs=[_tile_spec(0), _tile_spec(0), _w_spec(32, 0)],
        scratch_shapes=[pltpu.VMEM((h + TR, CB), F32), pltpu.VMEM((TR + h, CB), F32)],
        compiler_params=_cparams(("parallel", "arbitrary")))(proj, proj, proj, proj, duc, duc, w)


def ffn_conv_fwd(up, w, b, name):
    L = up.shape[0]
    k_taps = w.shape[0]
    h = _halo(k_taps)
    cv = D_FF // CB

    def body(gp_ref, g_ref, vp_ref, v_ref, wg_ref, wv_ref, bg_ref, bv_ref, o_ref, eg_ref, ev_ref):
        i = pl.program_id(1)
        eg_ref[0:h, :] = jnp.where(i > 0, gp_ref[...], 0.0)
        eg_ref[h:h + TR, :] = g_ref[...]
        ev_ref[0:h, :] = jnp.where(i > 0, vp_ref[...], 0.0)
        ev_ref[h:h + TR, :] = v_ref[...]
        pg = _causal_taps(eg_ref, wg_ref, k_taps, h, TR) + bg_ref[...]
        pv = _causal_taps(ev_ref, wv_ref, k_taps, h, TR) + bv_ref[...]
        o_ref[...] = (_silu(pg) * pv).astype(BF16)

    bspec = lambda c0: pl.BlockSpec((1, CB), lambda j, i: (0, j + c0))
    return pl.pallas_call(
        body, name=name, grid=(cv, L // TR), out_shape=_sds((L, D_FF), BF16),
        in_specs=[_prev_spec(h, 0), _tile_spec(0), _prev_spec(h, cv), _tile_spec(cv), _w_spec(k_taps, 0), _w_spec(k_taps, cv),
                  bspec(0), bspec(cv)],
        out_specs=_tile_spec(0), scratch_shapes=[pltpu.VMEM((h + TR, CB), F32), pltpu.VMEM((h + TR, CB), F32)],
        compiler_params=_cparams(("parallel", "parallel")))(up, up, up, up, w, w, b, b)


def ffn_conv_bwd(up, w, b, dact, name):
    L = up.shape[0]
    k_taps = w.shape[0]
    h = _halo(k_taps)
    nt = L // TR
    cv = D_FF // CB

    def body(gp_ref, g_ref, gn_ref, vp_ref, v_ref, vn_ref, d_ref, dn_ref, wg_ref, wv_ref, bg_ref, bv_ref,
             dg_ref, dv_ref, dwg_ref, dwv_ref, eg_ref, ev_ref, pg_ref, pv_ref):
        i = pl.program_id(1)

        @pl.when(i == 0)
        def _():
            dwg_ref[...] = jnp.zeros((8, CB), F32)
            dwv_ref[...] = jnp.zeros((8, CB), F32)

        for e_ref, p_ref, c_ref, n_ref in ((eg_ref, gp_ref, g_ref, gn_ref), (ev_ref, vp_ref, v_ref, vn_ref)):
            e_ref[0:h, :] = jnp.where(i > 0, p_ref[...], 0.0)
            e_ref[h:h + TR, :] = c_ref[...]
            e_ref[h + TR:h + TR + h, :] = n_ref[...]
        pg = _causal_taps(eg_ref, wg_ref, k_taps, h, TR + h) + bg_ref[...]
        pv = _causal_taps(ev_ref, wv_ref, k_taps, h, TR + h) + bv_ref[...]
        dact_t = d_ref[...]
        dact_n = jnp.where(i < nt - 1, dn_ref[...], 0.0)
        pg_ref[0:TR, :] = dact_t * pv[0:TR, :] * _dsilu(pg[0:TR, :])
        pg_ref[TR:TR + h, :] = dact_n * pv[TR:TR + h, :] * _dsilu(pg[TR:TR + h, :])
        pv_ref[0:TR, :] = dact_t * _silu(pg[0:TR, :])
        pv_ref[TR:TR + h, :] = dact_n * _silu(pg[TR:TR + h, :])
        dg_ref[...] = _anticausal_taps(pg_ref, wg_ref, k_taps, TR).astype(BF16)
        dv_ref[...] = _anticausal_taps(pv_ref, wv_ref, k_taps, TR).astype(BF16)
        _acc_conv_wgrad(dwg_ref, pg_ref[0:TR, :], eg_ref, k_taps, h)
        _acc_conv_wgrad(dwv_ref, pv_ref[0:TR, :], ev_ref, k_taps, h)

    bspec = lambda c0: pl.BlockSpec((1, CB), lambda j, i: (0, j + c0))
    ext = pltpu.VMEM((h + TR + h, CB), F32)
    dpre = pltpu.VMEM((TR + h, CB), F32)
    return pl.pallas_call(
        body, name=name, grid=(cv, nt),
        out_shape=[_sds((L, D_FF), BF16), _sds((L, D_FF), BF16), _sds((8, D_FF), F32), _sds((8, D_FF), F32)],
        in_specs=[_prev_spec(h, 0), _tile_spec(0), _next_spec(h, 0, nt), _prev_spec(h, cv), _tile_spec(cv), _next_spec(h, cv, nt),
                  _tile_spec(0), _next_spec(h, 0, nt), _w_spec(k_taps, 0), _w_spec(k_taps, cv), bspec(0), bspec(cv)],
        out_specs=[_tile_spec(0), _tile_spec(0), _w_spec(8, 0), _w_spec(8, 0)],
        scratch_shapes=[ext, ext, dpre, dpre],
        compiler_params=_cparams(("parallel", "arbitrary")))(up, up, up, up, up, up, dact, dact, w, w, b, b)


def _ssd_common(xbc_ref, dt_ref, dtb_ref, alog_ref, cs_ref):
    xs = xbc_ref[:, 0:D_SSD]
    sp_in = dt_ref[...] + dtb_ref[...]
    dtf = _softplus(sp_in)
    a_f = -jnp.exp(alog_ref[...])
    a_dt = dtf * a_f
    row = lax.broadcasted_iota(jnp.int32, (Q, Q), 0)
    col = lax.broadcasted_iota(jnp.int32, (Q, Q), 1)
    causal = row >= col
    cs = _dot_hi(causal.astype(F32), a_dt)
    cs_ref[...] = cs
    cs_last = cs_ref[Q - 1:Q, :]
    return xs, sp_in, dtf, a_f, cs, cs_last, causal


def _head_decay(cs_j, cst_ref, e, causal):
    lane = lax.broadcasted_iota(jnp.int32, (Q, LANE), 1)
    rolled = pltpu.roll(cs_j, HEAD, 1)
    own = (lane < HEAD) if e == 0 else (lane >= HEAD)
    col_b = jnp.where(own, cs_j, rolled)
    col_b = jnp.concatenate([col_b] * (Q // LANE), axis=1)
    row_b = cst_ref[e * HEAD:e * HEAD + 1, :]
    return jnp.where(causal, jnp.exp(jnp.minimum(col_b - row_b, 0.0)), 0.0)


def ssd_fwd(xbc, proj, dtb_f, alog_f, dsk_f, snw, name):
    L = xbc.shape[0]
    nc = L // Q

    def body(xbc_ref, z_ref, dt_ref, dtb_ref, alog_ref, dsk_ref, snw_ref, y_ref, yn_ref, sp_ref, s_ref, cs_ref, cst_ref, yd_ref):
        @pl.when(pl.program_id(0) == 0)
        def _():
            s_ref[...] = jnp.zeros((N_STATE, D_SSD), F32)

        xs, _, dtf, a_f, cs, cs_last, causal = _ssd_common(xbc_ref, dt_ref, dtb_ref, alog_ref, cs_ref)
        e_cs = jnp.exp(cs)
        xdt = xs * dtf
        zst = jnp.exp(cs_last - cs) * xdt
        sp_ref[0] = s_ref[...]
        lane = lax.broadcasted_iota(jnp.int32, (Q, LANE), 1)
        for g in range(2):
            gl = slice(g * 512, g * 512 + 512)
            b_g = xbc_ref[:, D_SSD + g * N_STATE:D_SSD + (g + 1) * N_STATE]
            c_g = xbc_ref[:, D_SSD + 2 * N_STATE + g * N_STATE:D_SSD + 2 * N_STATE + (g + 1) * N_STATE]
            s_prev = s_ref[:, gl]
            cb = _dot_nt(c_g, b_g)
            yd_ref[:, gl] = e_cs[:, gl] * _dot(c_g, s_prev)
            for j in range(4):
                tl = slice(g * 512 + j * LANE, g * 512 + (j + 1) * LANE)
                cs_j = cs[:, tl]
                cst_ref[...] = cs_j.T
                x_j = xdt[:, tl]
                o0 = _dot(cb * _head_decay(cs_j, cst_ref, 0, causal), x_j)
                o1 = _dot(cb * _head_decay(cs_j, cst_ref, 1, causal), x_j)
                yd_ref[:, tl] += jnp.where(lane < HEAD, o0, o1)
            s_ref[:, gl] = jnp.exp(cs_last[:, gl]) * s_prev + _dot_tn(b_g, zst[:, gl])
        y = yd_ref[...] + xs * dsk_ref[...]
        y_ref[...] = y
        yz = y * _silu(z_ref[...])
        r = lax.rsqrt(jnp.mean(yz * yz, axis=-1, keepdims=True) + 1e-6)
        yn_ref[...] = (yz * r * snw_ref[...]).astype(BF16)

    chunk = lambda w, c: pl.BlockSpec((Q, w), lambda i: (i, c))
    return pl.pallas_call(
        body, name=name, grid=(nc,),
        out_shape=[_sds((L, D_SSD), F32), _sds((L, D_SSD), BF16), _sds((nc, N_STATE, D_SSD), F32)],
        in_specs=[chunk(D_XBC, 0), chunk(D, 0), chunk(D, 1)] + [_const_spec((1, D))] * 4,
        out_specs=[chunk(D, 0), chunk(D, 0), pl.BlockSpec((1, N_STATE, D_SSD), lambda i: (i, 0, 0))],
        scratch_shapes=[pltpu.VMEM((N_STATE, D_SSD), F32), pltpu.VMEM((Q, D_SSD), F32), pltpu.VMEM((LANE, Q), F32),
                        pltpu.VMEM((Q, D_SSD), F32)],
        compiler_params=_cparams(("arbitrary",)))(xbc, proj, proj, dtb_f, alog_f, dsk_f, snw)


def ssd_bwd(dmixin, y, xbc, proj, s_prev_all, dtb_f, alog_f, dsk_f, snw, name):
    L = xbc.shape[0]
    nc = L // Q

    def body(dyn_ref, y_ref, xbc_ref, z_ref, dt_ref, sp_ref, dtb_ref, alog_ref, dsk_ref, snw_ref,
             dz_ref, ddt_ref, dxbc_ref, acc_ref, acc16_ref, ds_ref, cs_ref, cst_ref, dcs_ref, dx_ref):
        step = pl.program_id(0)

        @pl.when(step == 0)
        def _():
            ds_ref[...] = jnp.zeros((N_STATE, D_SSD), F32)
            acc_ref[...] = jnp.zeros((8, D), F32)

        z = z_ref[...]
        y = y_ref[...]
        sz = _sigmoid(z)
        siluz = z * sz
        yz = y * siluz
        r = lax.rsqrt(jnp.mean(yz * yz, axis=-1, keepdims=True) + 1e-6)
        n = yz * r
        dyn = dyn_ref[...]
        acc_ref[0:1, :] += _colsum(dyn * n)
        dn = dyn * snw_ref[...]
        dyz = r * (dn - n * jnp.mean(dn * n, axis=-1, keepdims=True))
        dy = dyz * siluz
        dz_ref[...] = (dyz * y * (sz * (1.0 + z * (1.0 - sz)))).astype(BF16)

        xs, sp_in, dtf, a_f, cs, cs_last, causal = _ssd_common(xbc_ref, dt_ref, dtb_ref, alog_ref, cs_ref)
        acc_ref[3:4, :] += _colsum(dy * xs)
        e_cs = jnp.exp(cs)
        xdt = xs * dtf
        dst = jnp.exp(cs_last - cs)
        zst = dst * xdt
        e_last = jnp.exp(cs_last)
        lane = lax.broadcasted_iota(jnp.int32, (Q, LANE), 1)
        ones = jnp.ones((Q, LANE), F32)
        dcs_last_parts = []
        for g in range(2):
            gl = slice(g * 512, g * 512 + 512)
            b_g = xbc_ref[:, D_SSD + g * N_STATE:D_SSD + (g + 1) * N_STATE]
            c_g = xbc_ref[:, D_SSD + 2 * N_STATE + g * N_STATE:D_SSD + 2 * N_STATE + (g + 1) * N_STATE]
            s_prev = sp_ref[0, :, gl]
            ds_g = ds_ref[:, gl]
            dy_g = dy[:, gl]
            cb = _dot_nt(c_g, b_g)
            y_off = e_cs[:, gl] * _dot(c_g, s_prev)
            edy = e_cs[:, gl] * dy_g
            d_c = _dot_nt(edy, s_prev)
            d_z = _dot(b_g, ds_g)
            d_b = _dot_nt(zst[:, gl], ds_g)
            t_g = d_z * zst[:, gl]
            dcs_ref[:, gl] = dy_g * y_off - t_g
            dx_ref[:, gl] = d_z * dst[:, gl]
            dcs_last_parts.append(_colsum(t_g) + _colsum(ds_g * s_prev) * e_last[:, gl])
            ds_ref[:, gl] = e_last[:, gl] * ds_g + _dot_tn(c_g, edy)
            dcb = jnp.zeros((Q, Q), F32)
            for j in range(4):
                tl = slice(g * 512 + j * LANE, g * 512 + (j + 1) * LANE)
                cs_j = cs[:, tl]
                cst_ref[...] = cs_j.T
                x_j = xdt[:, tl]
                dy_j = dy[:, tl]
                dx_j = jnp.zeros((Q, LANE), F32)
                dcs_j = jnp.zeros((Q, LANE), F32)
                for e in range(2):
                    own = (lane < HEAD) if e == 0 else (lane >= HEAD)
                    w_h = _head_decay(cs_j, cst_ref, e, causal)
                    g_h = cb * w_h
                    dy_m = jnp.where(own, dy_j, 0.0)
                    d_g = _dot_nt(dy_m, x_j)
                    dx_j = dx_j + _dot_tn(g_h, dy_m)
                    dcb = dcb + d_g * w_h
                    p_h = d_g * g_h
                    dcs_j = dcs_j + jnp.where(own, _dot_hi(p_h, ones) - _dot_tn_hi(p_h, ones), 0.0)
                dcs_ref[:, tl] += dcs_j * (1.0 / HEAD)
                dx_ref[:, tl] += dx_j
            d_c = d_c + _dot(dcb, b_g)
            d_b = d_b + _dot_tn(dcb, c_g)
            dxbc_ref[:, D_SSD + g * N_STATE:D_SSD + (g + 1) * N_STATE] = d_b
            dxbc_ref[:, D_SSD + 2 * N_STATE + g * N_STATE:D_SSD + 2 * N_STATE + (g + 1) * N_STATE] = d_c
        dcs_last = jnp.concatenate(dcs_last_parts, axis=1)
        anticausal = lax.broadcasted_iota(jnp.int32, (Q, Q), 0) <= lax.broadcasted_iota(jnp.int32, (Q, Q), 1)
        d_adt = _dot_hi(anticausal.astype(F32), dcs_ref[...]) + dcs_last
        dx = dx_ref[...]
        acc_ref[2:3, :] += _colsum(d_adt * dtf) * a_f
        d_dtf = d_adt * a_f + dx * xs
        dxbc_ref[:, 0:D_SSD] = dx * dtf + dy * dsk_ref[...]
        d_raw = d_dtf * _sigmoid(sp_in)
        acc_ref[1:2, :] += _colsum(d_raw)
        head_of_lane = lax.broadcasted_iota(jnp.int32, (D_SSD, LANE), 0) // HEAD
        fold = (head_of_lane == lax.broadcasted_iota(jnp.int32, (D_SSD, LANE), 1)).astype(F32)
        ddt_ref[...] = _dot_hi(d_raw, fold).astype(BF16)

        @pl.when(step == nc - 1)
        def _():
            acc16_ref[...] = _dot_hi(acc_ref[...], fold)

    rchunk = lambda w, c: pl.BlockSpec((Q, w), lambda i: (nc - 1 - i, c))
    return pl.pallas_call(
        body, name=name, grid=(nc,),
        out_shape=[_sds((L, D_SSD), BF16), _sds((L, LANE), BF16), _sds((L, D_XBC), F32), _sds((8, D), F32), _sds((8, LANE), F32)],
        in_specs=[rchunk(D, 0), rchunk(D, 0), rchunk(D_XBC, 0), rchunk(D, 0), rchunk(D, 1),
                  pl.BlockSpec((1, N_STATE, D_SSD), lambda i: (nc - 1 - i, 0, 0))] + [_const_spec((1, D))] * 4,
        out_specs=[rchunk(D, 0), rchunk(LANE, 0), rchunk(D_XBC, 0), _const_spec((8, D)), _const_spec((8, LANE))],
        scratch_shapes=[pltpu.VMEM((N_STATE, D_SSD), F32), pltpu.VMEM((Q, D_SSD), F32), pltpu.VMEM((LANE, Q), F32),
                        pltpu.VMEM((Q, D_SSD), F32), pltpu.VMEM((Q, D_SSD), F32)],
        compiler_params=_cparams(("arbitrary",)))(dmixin, y, xbc, proj, proj, s_prev_all, dtb_f, alog_f, dsk_f, snw)


def _adamw_math(w, g, m, v):
    m_n = ADAM_B1 * m + (1.0 - ADAM_B1) * g
    v_n = ADAM_B2 * v + (1.0 - ADAM_B2) * jnp.square(g)
    c1 = 1.0 - ADAM_B1 ** ADAM_STEP
    c2 = 1.0 - ADAM_B2 ** ADAM_STEP
    return -ADAM_LR * ((m_n / c1) / (jnp.sqrt(v_n / c2) + ADAM_EPS) + ADAM_WD * w), m_n, v_n


def _sum_slots(p_ref):
    acc = p_ref[0].astype(F32)
    for s in range(1, N_DEV):
        acc = acc + p_ref[s].astype(F32)
    return acc


def adamw_slots(w, slots, m, v, name):
    rows, cols = w.shape
    tr = next(t for t in (256, 128, 64, 32, 16) if rows % t == 0)

    def body(w_ref, s_ref, m_ref, v_ref, g_ref, d_ref, mo_ref, vo_ref):
        g_v = _sum_slots(s_ref)
        g_ref[...] = g_v
        d_ref[...], mo_ref[...], vo_ref[...] = _adamw_math(w_ref[...], g_v, m_ref[...], v_ref[...])

    spec = pl.BlockSpec((tr, cols), lambda i: (i, 0))
    return pl.pallas_call(body, name=name, grid=(rows // tr,), out_shape=[_sds((rows, cols), F32)] * 4,
                          in_specs=[spec, pl.BlockSpec((N_DEV, tr, cols), lambda i: (0, i, 0)), spec, spec], out_specs=[spec] * 4,
                          compiler_params=_cparams(("parallel",)))(w, slots, m, v)


def adamw_many(ws, gs, ms, vs, name):
    n = len(ws)

    def body(*refs):
        for p in range(n):
            d_v, m_v, v_v = _adamw_math(refs[p][...], refs[n + p][...], refs[2 * n + p][...], refs[3 * n + p][...])
            refs[4 * n + p][...] = d_v
            refs[5 * n + p][...] = m_v
            refs[6 * n + p][...] = v_v

    vm = pl.BlockSpec(memory_space=pltpu.VMEM)
    out = pl.pallas_call(body, name=name, out_shape=[_sds(w.shape, F32) for w in ws] * 3, in_specs=[vm] * (4 * n),
                         out_specs=[vm] * (3 * n), compiler_params=_cparams())(*ws, *gs, *ms, *vs)
    return out[:n], out[n:2 * n], out[2 * n:]


def sum_slots_many(parts, name):
    n = len(parts)

    def body(*refs):
        for p in range(n):
            refs[n + p][...] = _sum_slots(refs[p])

    vm = pl.BlockSpec(memory_space=pltpu.VMEM)
    return pl.pallas_call(body, name=name, out_shape=[_sds(p.shape[1:], F32) for p in parts], in_specs=[vm] * n,
                          out_specs=[vm] * n, compiler_params=_cparams())(*parts)


def ada_mod(c_all, ada_w_shard, ada_b_cols, name):
    def body(c_ref, w_ref, b_ref, o_ref, ca_ref):
        ca = _silu(c_ref[...])
        ca_ref[...] = ca
        o_ref[...] = _dot(ca, w_ref[...]) + b_ref[...]

    vm = pl.BlockSpec(memory_space=pltpu.VMEM)
    return pl.pallas_call(body, name=name, out_shape=[_sds((N_DEV, ada_w_shard.shape[1]), F32), _sds((N_DEV, D), F32)],
                          in_specs=[vm, vm, vm], out_specs=[vm, vm], compiler_params=_cparams())(c_all, ada_w_shard, ada_b_cols)


def ada_wgrad(c_act_all, dmod_cols, name):
    def body(c_ref, d_ref, o_ref):
        o_ref[...] = _dot_tn_hi(c_ref[...], d_ref[...])

    vm = pl.BlockSpec(memory_space=pltpu.VMEM)
    return pl.pallas_call(body, name=name, out_shape=_sds((D, dmod_cols.shape[1]), F32), in_specs=[vm, vm], out_specs=vm,
                          compiler_params=_cparams())(c_act_all, dmod_cols)


def exchange(srcs, name, gather):
    n = len(srcs)
    shapes = [tuple(s.shape) if gather else tuple(s.shape[1:]) for s in srcs]

    def body(*refs):
        src_refs, out_refs = refs[:n], refs[n:2 * n]
        send_sems, recv_sems, local_sems = refs[2 * n:]
        x, y, c = lax.axis_index("x"), lax.axis_index("y"), lax.axis_index("c")
        me = 4 * x + 2 * y + c

        def peer(k):
            bx, by, bc = (k >> 2) & 1, (k >> 1) & 1, k & 1
            px, py, pc = (x + bx) % 2, (y + by) % 2, (c + bc) % 2
            return (px, py, pc), 4 * px + 2 * py + pc

        def copy(a, k, landing):
            dev, idx = peer(k)
            return pltpu.make_async_remote_copy(
                src_ref=src_refs[a] if gather else src_refs[a].at[idx], dst_ref=out_refs[a].at[idx if landing else me],
                send_sem=send_sems.at[a, k - 1], recv_sem=recv_sems.at[a, k - 1],
                device_id=dev, device_id_type=pl.DeviceIdType.MESH)

        mine = [pltpu.make_async_copy(src_refs[a] if gather else src_refs[a].at[me], out_refs[a].at[me], local_sems.at[a])
                for a in range(n)]
        for cp in mine:
            cp.start()
        sends = [copy(a, k, False) for a in range(n) for k in range(1, N_DEV)]
        for cp in sends:
            cp.start()
        for a in range(n):
            for k in range(1, N_DEV):
                copy(a, k, True).wait_recv()
        for cp in sends:
            cp.wait_send()
        for cp in mine:
            cp.wait()

    hbm = pl.BlockSpec(memory_space=pl.ANY)
    return pl.pallas_call(
        body, name=name, out_shape=[_sds((N_DEV,) + shp, s.dtype) for shp, s in zip(shapes, srcs)], in_specs=[hbm] * n,
        out_specs=[hbm] * n,
        scratch_shapes=[pltpu.SemaphoreType.DMA((n, N_DEV - 1)), pltpu.SemaphoreType.DMA((n, N_DEV - 1)),
                        pltpu.SemaphoreType.DMA((n,))],
        compiler_params=pltpu.CompilerParams(has_side_effects=True))(*srcs)


def _peer(k):
    x, y, c = lax.axis_index("x"), lax.axis_index("y"), lax.axis_index("c")
    px, py, pc = (x + ((k >> 2) & 1)) % 2, (y + ((k >> 1) & 1)) % 2, (c + (k & 1)) % 2
    return (px, py, pc), 4 * px + 2 * py + pc


def _my_slot():
    return 4 * lax.axis_index("x") + 2 * lax.axis_index("y") + lax.axis_index("c")


_HBM = pl.BlockSpec(memory_space=pltpu.HBM)
_SEM = pl.BlockSpec(memory_space=pltpu.SEMAPHORE)
_EFFECT = pltpu.SideEffectType.DATAFLOW_SIDE_EFFECTING


def exchange_start(srcs, name, gather):
    n = len(srcs)
    shapes = [tuple(s.shape) if gather else tuple(s.shape[1:]) for s in srcs]
    lands = [lax.empty((N_DEV,) + shp, s.dtype) for shp, s in zip(shapes, srcs)]

    def body(*refs):
        src_refs, land_refs = refs[:n], refs[n:2 * n]
        sems = refs[2 * n:4 * n]
        token = refs[-1]
        me = _my_slot()
        for a in range(n):
            for k in range(1, N_DEV):
                dev, idx = _peer(k)
                pltpu.make_async_remote_copy(
                    src_ref=src_refs[a] if gather else src_refs[a].at[idx], dst_ref=land_refs[a].at[me],
                    send_sem=sems[2 * a].at[k - 1], recv_sem=sems[2 * a + 1].at[k - 1],
                    device_id=dev, device_id_type=pl.DeviceIdType.MESH).start()
        token[...] = jnp.zeros_like(token)

    out_shape = ([pltpu.SemaphoreType.DMA((N_DEV - 1,))] * (2 * n) + [pltpu.HBM(s.shape, s.dtype) for s in srcs]
                 + [pltpu.HBM(l.shape, l.dtype) for l in lands] + [_sds((8, LANE), F32)])
    out = pl.pallas_call(
        body, name=name, out_shape=out_shape, in_specs=[_HBM] * (2 * n),
        out_specs=[_SEM] * (2 * n) + [_HBM] * (2 * n) + [pl.BlockSpec(memory_space=pltpu.VMEM)],
        input_output_aliases={i: 2 * n + i for i in range(2 * n)},
        compiler_params=pltpu.CompilerParams(has_side_effects=_EFFECT))(
            *[pltpu.with_memory_space_constraint(s, pltpu.HBM) for s in srcs],
            *[pltpu.with_memory_space_constraint(l, pltpu.HBM) for l in lands])
    parts = [(out[2 * a], out[2 * a + 1], out[2 * n + a], out[3 * n + a]) for a in range(n)]
    return parts, out[-1]


def exchange_wait(parts, after, name, gather):
    n = len(parts)

    def body(*refs):
        src_refs, land_refs = refs[:n], refs[n:2 * n]
        sems = refs[2 * n:4 * n]
        for a in range(n):
            for k in range(1, N_DEV):
                dev, idx = _peer(k)
                copy = pltpu.make_async_remote_copy(
                    src_ref=src_refs[a] if gather else src_refs[a].at[idx], dst_ref=land_refs[a].at[idx],
                    send_sem=sems[2 * a].at[k - 1], recv_sem=sems[2 * a + 1].at[k - 1],
                    device_id=dev, device_id_type=pl.DeviceIdType.MESH)
                copy.wait_send()
                copy.wait_recv()

    srcs = [p[2] for p in parts]
    lands = [p[3] for p in parts]
    sems = [s for p in parts for s in p[:2]]
    out = pl.pallas_call(
        body, name=name, out_shape=[pltpu.HBM(a.shape, a.dtype) for a in srcs + lands],
        in_specs=[_HBM] * (2 * n) + [_SEM] * (2 * n) + [pl.BlockSpec(memory_space=pl.ANY)], out_specs=[_HBM] * (2 * n),
        input_output_aliases={i: i for i in range(2 * n)},
        compiler_params=pltpu.CompilerParams(has_side_effects=_EFFECT))(*srcs, *lands, *sems, after)
    return out[n:]


def _cols_to_slabs(g):
    r, c = g.shape
    return g.reshape(r, N_DEV, c // N_DEV).transpose(1, 0, 2)


def _slabs_to_cols(s):
    _, r, cs = s.shape
    return s.transpose(1, 0, 2).reshape(r, N_DEV * cs)


def _rep_heads(v):
    return jnp.repeat(v.reshape(N_HEADS), HEAD).reshape(1, D_SSD)


def local_fwd_bwd(x, target, mod, get_w, put_grad, small):
    n1w, n2w, fnw = small["norm1_w"], small["norm2_w"], small["final_norm_w"]
    dtb_f, alog_f, dsk_f = _rep_heads(small["dt_bias"]), _rep_heads(small["a_log"]), _rep_heads(small["d_skip"])
    snw = small["ssd_norm_w"]

    def after(v, token):
        return v + token[0:1, 0:1]

    h1 = norm_mod(x, mod, n1w, 0, "norm1")
    w_in = get_w("w_in", h1)
    proj = mm_nn([(h1, w_in["w_in_p"], 0)], "in_proj")
    xbc = conv_silu_fwd(proj, 4096 // CB, D_XBC, small["ssd_conv_w"], small["ssd_conv_b"], "ssd_conv")
    y, ysn, s_prev = ssd_fwd(xbc, proj, dtb_f, alog_f, dsk_f, snw, "ssd_scan")
    uc = conf_conv_fwd(proj, 2048 // CB, 3072 // CB, small["conf_conv_w"], small["conf_conv_b"], "conf_conv")
    u = ln_silu(uc, small["conf_ln_w"], small["conf_ln_b"], "conf_ln")
    w_out = get_w("w_out", u)
    mix = mm_nn([(ysn, w_out, 0), (u, w_out, 1)], "out_proj")
    h2, x1 = norm_mod(x, mod, n2w, 3, "norm2", res=mix, gate_row=2)
    w_up = get_w("w_up", h2)
    up = mm_nn([(h2, w_up, 0)], "up_proj")
    act = ffn_conv_fwd(up, small["ffn_conv_w"], small["ffn_conv_b"], "ffn_conv")
    w_down = get_w("w_down", act)
    ff = mm_nn([(act, w_down, 0)], "down_proj")
    dx2, dff, acc_f = final_loss(ff, x1, mod, fnw, target, "final_loss")

    token = put_grad("w_down", mm_tn(act, dff, "wgrad_down"))
    dact = mm_nt([(dff, w_down, 0)], "dact")
    dupg, dupv, dwg, dwv = ffn_conv_bwd(up, small["ffn_conv_w"], after(small["ffn_conv_b"], token), dact, "ffn_conv_bwd")
    token = put_grad("w_up", jnp.concatenate([mm_tn(h2, dupg, "wgrad_up_gate"), mm_tn(h2, dupv, "wgrad_up_val")], axis=1))
    dh2 = mm_nt([(dupg, w_up, 0), (dupv, w_up, 1)], "dh2")
    dx1, dmix, acc_2 = norm_mod_bwd(dh2, x1, dx2, mod, after(n2w, token), 3, "norm2_bwd", mix=mix, gate_row=2)

    token = put_grad("w_out", jnp.concatenate([mm_tn(ysn, dmix, "wgrad_out_ssd"), mm_tn(u, dmix, "wgrad_out_conf")], axis=0))
    dmixin = mm_nt([(dmix, w_out, 0)], "dmixin")
    duc, acc_ln = ln_silu_bwd(dmixin, uc, after(small["conf_ln_w"], token), small["conf_ln_b"], "conf_ln_bwd")
    dcfa, dcfg, dw_cc = conf_conv_bwd(proj, 2048 // CB, 3072 // CB, small["conf_conv_w"], duc, "conf_conv_bwd")
    dz, ddt, dxbc_post, acc_s, acc_s16 = ssd_bwd(dmixin, y, xbc, proj, s_prev, dtb_f, alog_f, dsk_f, snw, "ssd_scan_bwd")
    dxbc, dw_sc = conv_silu_bwd(proj, 4096 // CB, D_XBC, small["ssd_conv_w"], small["ssd_conv_b"], dxbc_post, "ssd_conv_bwd")
    token = put_grad("w_in", jnp.concatenate(
        [mm_tn(h1, dz, "wgrad_in_z"), mm_tn(h1, dxbc, "wgrad_in_xbc"), mm_tn(h1, ddt, "wgrad_in_dt")[:, :N_HEADS],
         mm_tn(h1, dcfa, "wgrad_in_cfa"), mm_tn(h1, dcfg, "wgrad_in_cfg")], axis=1))
    dh1 = mm_nt([(dz, w_in["w_z"], 0), (ddt, w_in["w_dt16"], 0), (dcfa, w_in["w_cfa"], 0), (dcfg, w_in["w_cfg"], 0),
                 (dxbc, w_in["w_xbc"], 0)], "dh1")
    grad_x, acc_1 = norm_mod_bwd(dh1, x, dx1, mod, after(n1w, token), 0, "norm1_bwd")

    small_accs = dict(acc_1=acc_1, acc_2=acc_2, acc_f=acc_f, acc_ln=acc_ln, acc_s=acc_s, acc_s16=acc_s16, dw_sc=dw_sc,
                      dw_cc=dw_cc, dwg=dwg, dwv=dwv)
    return grad_x, small_accs


def kernel(x, c, ada_w, ada_b, norm1_w, w_in, ssd_conv_w, ssd_conv_b, dt_bias, a_log, d_skip, ssd_norm_w, conf_conv_w, conf_conv_b, conf_ln_w, conf_ln_b, w_out, norm2_w, w_up, ffn_conv_w, ffn_conv_b, w_down, final_norm_w, loss_target, m_ada_w, m_ada_b, m_norm1_w, m_w_in, m_ssd_conv_w, m_ssd_conv_b, m_dt_bias, m_a_log, m_d_skip, m_ssd_norm_w, m_conf_conv_w, m_conf_conv_b, m_conf_ln_w, m_conf_ln_b, m_w_out, m_norm2_w, m_w_up, m_ffn_conv_w, m_ffn_conv_b, m_w_down, m_final_norm_w, v_ada_w, v_ada_b, v_norm1_w, v_w_in, v_ssd_conv_w, v_ssd_conv_b, v_dt_bias, v_a_log, v_d_skip, v_ssd_norm_w, v_conf_conv_w, v_conf_conv_b, v_conf_ln_w, v_conf_ln_b, v_w_out, v_norm2_w, v_w_up, v_ffn_conv_w, v_ffn_conv_b, v_w_down, v_final_norm_w):
    me = 4 * lax.axis_index("x") + 2 * lax.axis_index("y") + lax.axis_index("c")
    weights = dict(ada_w=ada_w, ada_b=ada_b, norm1_w=norm1_w, w_in=w_in, ssd_conv_w=ssd_conv_w, ssd_conv_b=ssd_conv_b,
                   dt_bias=dt_bias, a_log=a_log, d_skip=d_skip, ssd_norm_w=ssd_norm_w, conf_conv_w=conf_conv_w,
                   conf_conv_b=conf_conv_b, conf_ln_w=conf_ln_w, conf_ln_b=conf_ln_b, w_out=w_out, norm2_w=norm2_w, w_up=w_up,
                   ffn_conv_w=ffn_conv_w, ffn_conv_b=ffn_conv_b, w_down=w_down, final_norm_w=final_norm_w)
    moms_m = dict(ada_w=m_ada_w, ada_b=m_ada_b, norm1_w=m_norm1_w, w_in=m_w_in, ssd_conv_w=m_ssd_conv_w, ssd_conv_b=m_ssd_conv_b,
                  dt_bias=m_dt_bias, a_log=m_a_log, d_skip=m_d_skip, ssd_norm_w=m_ssd_norm_w, conf_conv_w=m_conf_conv_w,
                  conf_conv_b=m_conf_conv_b, conf_ln_w=m_conf_ln_w, conf_ln_b=m_conf_ln_b, w_out=m_w_out, norm2_w=m_norm2_w,
                  w_up=m_w_up, ffn_conv_w=m_ffn_conv_w, ffn_conv_b=m_ffn_conv_b, w_down=m_w_down, final_norm_w=m_final_norm_w)
    moms_v = dict(ada_w=v_ada_w, ada_b=v_ada_b, norm1_w=v_norm1_w, w_in=v_w_in, ssd_conv_w=v_ssd_conv_w, ssd_conv_b=v_ssd_conv_b,
                  dt_bias=v_dt_bias, a_log=v_a_log, d_skip=v_d_skip, ssd_norm_w=v_ssd_norm_w, conf_conv_w=v_conf_conv_w,
                  conf_conv_b=v_conf_conv_b, conf_ln_w=v_conf_ln_w, conf_ln_b=v_conf_ln_b, w_out=v_w_out, norm2_w=v_norm2_w,
                  w_up=v_w_up, ffn_conv_w=v_ffn_conv_w, ffn_conv_b=v_ffn_conv_b, w_down=v_w_down, final_norm_w=v_final_norm_w)
    names = list(weights)

    def to2d(a):
        return a[0] if a.ndim == 3 else a.reshape(1, -1)

    big = ("w_in", "w_out", "w_up", "w_down")

    c_all, scw_all, ccw_all, fcw_all = exchange([c.reshape(8, LANE), ssd_conv_w[0], conf_conv_w[0], ffn_conv_w[0]],
                                                "gather_small", gather=True)
    c_all = c_all.reshape(N_DEV, D)

    ada_cols = ada_w.shape[2]
    ada_b_cols = lax.dynamic_slice(ada_b, (0, me * ada_cols), (1, ada_cols))
    mod_cols, c_act_all = ada_mod(c_all, ada_w[0], ada_b_cols, "ada_mod")
    mod_parts, = exchange([jnp.pad(mod_cols, ((0, 0), (0, D - ada_cols))).reshape(N_DEV, 8, LANE)], "scatter_mod", gather=False)
    mod = mod_parts.reshape(N_DEV, D)[:, :ada_cols].reshape(6, D)
    mod = jnp.pad(mod, ((0, 2), (0, 0)))

    shards, mod = lax.optimization_barrier(([weights[n][0].astype(BF16) for n in big], mod))
    gather_parts, token = exchange_start(shards, "gather_weights_start", gather=True)
    mod = mod + token[0:1, 0:1]

    small = {n: to2d(weights[n]) for n in names if n not in ("ada_w",) + big}
    small["ssd_conv_w"] = _slabs_to_cols(scw_all)
    small["conf_conv_w"] = _slabs_to_cols(ccw_all)
    small["ffn_conv_w"] = _slabs_to_cols(fcw_all)

    def with_own(landed, own):
        return lax.dynamic_update_slice(landed, own[None], (me,) + (0,) * own.ndim)

    def get_w(n, after):
        a = big.index(n)
        landed, = exchange_wait([gather_parts[a]], after, "gather_" + n + "_wait", gather=True)
        slabs = with_own(landed, shards[a])
        if n == "w_out":
            return slabs.reshape(2 * D, D)
        if n == "w_down":
            return slabs.reshape(D_FF, D)
        full = _slabs_to_cols(slabs)
        if n == "w_up":
            return full
        w_z, w_xbc, w_dt, w_cfa, w_cfg = full[:, :1024], full[:, 1024:2560], full[:, 2560:2576], full[:, 2576:3600], full[:, 3600:]
        return dict(w_in_p=jnp.concatenate([w_z, jnp.repeat(w_dt, HEAD, axis=1), w_cfa, w_cfg, w_xbc], axis=1), w_z=w_z,
                    w_xbc=w_xbc, w_dt16=jnp.pad(w_dt, ((0, 0), (0, LANE - N_HEADS))), w_cfa=w_cfa, w_cfg=w_cfg)

    scatter_parts, sent = {}, {}

    def put_grad(n, g):
        slabs = _cols_to_slabs(g) if n in ("w_in", "w_up") else g.reshape(N_DEV, g.shape[0] // N_DEV, g.shape[1])
        sent[n] = slabs.astype(BF16)
        (scatter_parts[n],), token = exchange_start([sent[n]], "scatter_" + n + "_start", gather=False)
        return token

    grad_x, accs = local_fwd_bwd(x[0], loss_target[0], mod, get_w, put_grad, small)
    loss = lax.psum(0.5 / D * jnp.sum(accs["acc_f"][2:3]), ("x", "y", "c"))

    landed = exchange_wait([scatter_parts[n] for n in big], grad_x, "scatter_grads_wait", gather=False)
    grads, delta, new_m, new_v = {}, {}, {}, {}
    for n, slots in zip(big, landed):
        slots = with_own(slots, lax.dynamic_index_in_dim(sent[n], me, 0, keepdims=False))
        grads[n], delta[n], new_m[n], new_v[n] = adamw_slots(weights[n][0], slots, moms_m[n][0], moms_v[n][0], "adamw_" + n)

    order = ("acc_1", "acc_2", "acc_f", "acc_ln", "acc_s", "acc_s16", "dw_sc", "dw_cc", "dwg", "dwv")
    gathered = dict(zip(order, exchange([accs[k] for k in order], "gather_small_grads", gather=True)))
    red = dict(zip(order, sum_slots_many([gathered[k] for k in order], "sum_small_grads")))

    def mod_rows(a1, a2, af):
        return jnp.concatenate([a1[..., 0:2, :], a2[..., 3:4, :], a2[..., 0:2, :], af[..., 1:2, :]], axis=-2)

    dmod_all = mod_rows(gathered["acc_1"], gathered["acc_2"], gathered["acc_f"]).reshape(N_DEV, 6 * D)
    grads["ada_w"] = ada_wgrad(c_act_all, lax.dynamic_slice(dmod_all, (0, me * ada_cols), (N_DEV, ada_cols)), "ada_wgrad")

    def my_cols(full, k_taps):
        cols = full.shape[1] // N_DEV
        return lax.dynamic_slice(full, (0, me * cols), (k_taps, cols))

    fcw = jnp.concatenate([red["dwg"], red["dwv"]], axis=1)
    grads.update(
        ada_b=mod_rows(red["acc_1"], red["acc_2"], red["acc_f"]).reshape(1, 6 * D), norm1_w=red["acc_1"][2:3],
        ssd_conv_w=my_cols(red["dw_sc"], K_SSD), ssd_conv_b=red["dw_sc"][K_SSD:K_SSD + 1],
        dt_bias=red["acc_s16"][1:2, :N_HEADS], a_log=red["acc_s16"][2:3, :N_HEADS], d_skip=red["acc_s16"][3:4, :N_HEADS],
        ssd_norm_w=red["acc_s"][0:1], conf_conv_w=my_cols(red["dw_cc"], K_CONF), conf_conv_b=red["dw_cc"][K_CONF:K_CONF + 1],
        conf_ln_w=red["acc_ln"][0:1], conf_ln_b=red["acc_ln"][1:2], norm2_w=red["acc_2"][2:3],
        ffn_conv_w=my_cols(fcw, K_FFN), ffn_conv_b=fcw[K_FFN:K_FFN + 1], final_norm_w=red["acc_f"][0:1])

    rest = [n for n in names if n not in big]
    d_l, m_l, v_l = adamw_many([to2d(weights[n]) for n in rest], [grads[n] for n in rest], [to2d(moms_m[n]) for n in rest],
                               [to2d(moms_v[n]) for n in rest], "adamw_small")
    for n, dd, mm, vv in zip(rest, d_l, m_l, v_l):
        delta[n], new_m[n], new_v[n] = dd, mm, vv
    shape_of = lambda d_: {n: d_[n].reshape(weights[n].shape) for n in names}
    grads, delta, new_m, new_v = shape_of(grads), shape_of(delta), shape_of(new_m), shape_of(new_v)
    return (loss, grad_x[None], *[grads[n] for n in names], *[delta[n] for n in names], *[new_m[n] for n in names],
            *[new_v[n] for n in names])
```

```python
import functools

import jax
import jax.numpy as jnp
from jax import lax
from jax.experimental import pallas as pl
from jax.experimental.pallas import tpu as pltpu

F32 = jnp.float32
BF16 = jnp.bfloat16
HI = lax.Precision.HIGHEST

N_DEV = 8
D = 1024
D_SSD = 1024
HEAD = 64
N_HEADS = 16
N_STATE = 128
D_XBC = 1536
D_CONF = 1024
D_FF = 2816
K_SSD, K_CONF, K_FFN = 4, 31, 3
D_INP = 5632
LANE = 128
TR = 256
Q = 256
CB = 256
VMEM_LIMIT = 56 * 1024 * 1024

ADAM_LR, ADAM_B1, ADAM_B2, ADAM_EPS, ADAM_WD, ADAM_STEP = 0.001, 0.9, 0.999, 1e-08, 0.01, 10


def _cparams(sem=None):
    return pltpu.CompilerParams(vmem_limit_bytes=VMEM_LIMIT, dimension_semantics=sem)


def _sds(shape, dtype):
    return jax.ShapeDtypeStruct(shape, dtype)


def _sigmoid(x):
    return 1.0 / (1.0 + jnp.exp(-x))


def _silu(x):
    return x * _sigmoid(x)


def _dsilu(x):
    s = _sigmoid(x)
    return s * (1.0 + x * (1.0 - s))


def _softplus(x):
    return jnp.maximum(x, 0.0) + jnp.log(1.0 + jnp.exp(-jnp.abs(x)))


def _dot(a, b):
    return jnp.dot(a.astype(BF16), b.astype(BF16), preferred_element_type=F32)


def _dot_nt(a, b):
    return lax.dot_general(a.astype(BF16), b.astype(BF16), (((1,), (1,)), ((), ())), preferred_element_type=F32)


def _dot_tn(a, b):
    return lax.dot_general(a.astype(BF16), b.astype(BF16), (((0,), (0,)), ((), ())), preferred_element_type=F32)


def _dot_hi(a, b):
    return jnp.dot(a, b, precision=HI, preferred_element_type=F32)


def _bf16_terms(a, terms):
    parts, rem = [], a
    for t in range(terms):
        p = rem.astype(BF16)
        parts.append(p)
        if t + 1 < terms:
            rem = rem - p.astype(F32)
    return parts


def _dot_exact(a, b, terms, exact, dims=(((1,), (0,)), ((), ()))):
    if exact == "a":
        a_b = a.astype(BF16)
        outs = [lax.dot_general(a_b, p, dims, preferred_element_type=F32) for p in _bf16_terms(b, terms)]
    else:
        b_b = b.astype(BF16)
        outs = [lax.dot_general(p, b_b, dims, preferred_element_type=F32) for p in _bf16_terms(a, terms)]
    acc = outs[-1]
    for o in reversed(outs[:-1]):
        acc = acc + o
    return acc


def _dot_tn_hi(a, b):
    return lax.dot_general(a, b, (((0,), (0,)), ((), ())), precision=HI, preferred_element_type=F32)


def _colsum(x):
    return jnp.sum(x, axis=0, keepdims=True)


def _const_spec(shape):
    return pl.BlockSpec(shape, lambda *_: (0,) * len(shape))


def _col_tile(n):
    for t in (1408, 1024, 768, 512, 256, 128):
        if n % t == 0 and t <= n:
            return t
    return n


def mm_nn(pairs, name):
    L = pairs[0][0].shape[0]
    N = pairs[0][1].shape[1]
    tn = _col_tile(N)
    n = len(pairs)

    def body(*refs):
        acc = None
        for p in range(n):
            t = jnp.dot(refs[2 * p][...], refs[2 * p + 1][...], preferred_element_type=F32)
            acc = t if acc is None else acc + t
        refs[-1][...] = acc

    in_specs, args = [], []
    for a, w, rb in pairs:
        in_specs += [pl.BlockSpec((TR, a.shape[1]), lambda j, i: (i, 0)),
                     pl.BlockSpec((a.shape[1], tn), functools.partial(lambda j, i, rb: (rb, j), rb=rb))]
        args += [a, w]
    return pl.pallas_call(
        body, name=name, grid=(N // tn, L // TR), out_shape=_sds((L, N), F32), in_specs=in_specs,
        out_specs=pl.BlockSpec((TR, tn), lambda j, i: (i, j)),
        compiler_params=_cparams(("parallel", "parallel")))(*args)


def mm_nt(pairs, name):
    L = pairs[0][0].shape[0]
    K = pairs[0][1].shape[0]
    tk = _col_tile(K)
    n = len(pairs)

    def body(*refs):
        o_ref = refs[-1]
        acc = None
        for p in range(n):
            t = lax.dot_general(refs[2 * p][...], refs[2 * p + 1][...], (((1,), (1,)), ((), ())),
                                preferred_element_type=F32)
            acc = t if acc is None else acc + t
        o_ref[...] = acc

    in_specs, args = [], []
    for a, w, cb in pairs:
        in_specs += [pl.BlockSpec((TR, a.shape[1]), lambda j, i: (i, 0)),
                     pl.BlockSpec((tk, a.shape[1]), functools.partial(lambda j, i, cb: (j, cb), cb=cb))]
        args += [a, w]
    return pl.pallas_call(
        body, name=name, grid=(K // tk, L // TR), out_shape=_sds((L, K), F32), in_specs=in_specs,
        out_specs=pl.BlockSpec((TR, tk), lambda j, i: (i, j)),
        compiler_params=_cparams(("parallel", "parallel")))(*args)


def mm_tn(a, g, name):
    L, M = a.shape
    N = g.shape[1]
    tn = _col_tile(N) if N > 1024 else N
    if M * tn * 4 > 8 * 1024 * 1024:
        tn = 512
    tl = 512 if L % 512 == 0 else TR

    def body(a_ref, g_ref, o_ref):
        @pl.when(pl.program_id(1) == 0)
        def _():
            o_ref[...] = jnp.zeros((M, tn), F32)

        o_ref[...] += lax.dot_general(a_ref[...], g_ref[...], (((0,), (0,)), ((), ())), preferred_element_type=F32)

    return pl.pallas_call(
        body, name=name, grid=(N // tn, L // tl), out_shape=_sds((M, N), F32),
        in_specs=[pl.BlockSpec((tl, M), lambda j, l: (l, 0)), pl.BlockSpec((tl, tn), lambda j, l: (l, j))],
        out_specs=pl.BlockSpec((M, tn), lambda j, l: (0, j)),
        compiler_params=_cparams(("parallel", "arbitrary")))(a, g)


def _row_spec(width=D):
    return pl.BlockSpec((TR, width), lambda i: (i, 0))


def _row_col_spec(width, col):
    return pl.BlockSpec((TR, width), lambda i: (i, col))


def norm_mod(x, mod, w, shift_row, name, res=None, gate_row=None):
    L = x.shape[0]
    has_res = res is not None

    def body(*refs):
        if has_res:
            x_ref, res_ref, mod_ref, w_ref, h_ref, xo_ref = refs
            xin = x_ref[...] + mod_ref[gate_row:gate_row + 1, :] * res_ref[...]
            xo_ref[...] = xin
        else:
            x_ref, mod_ref, w_ref, h_ref = refs
            xin = x_ref[...]
        r = lax.rsqrt(jnp.mean(xin * xin, axis=-1, keepdims=True) + 1e-6)
        h = (xin * r * w_ref[...]) * (1.0 + mod_ref[shift_row + 1:shift_row + 2, :]) + mod_ref[shift_row:shift_row + 1, :]
        h_ref[...] = h.astype(BF16)

    ins = [x] + ([res] if has_res else []) + [mod, w]
    in_specs = [_row_spec()] + ([_row_spec()] if has_res else []) + [_const_spec((8, D)), _const_spec((1, D))]
    out_shape = [_sds((L, D), BF16)] + ([_sds((L, D), F32)] if has_res else [])
    out_specs = [_row_spec()] + ([_row_spec()] if has_res else [])
    out = pl.pallas_call(body, name=name, grid=(L // TR,), out_shape=out_shape, in_specs=in_specs,
                         out_specs=out_specs, compiler_params=_cparams(("parallel",)))(*ins)
    return out if has_res else out[0]


def ln_silu(uc, lnw, lnb, name):
    L = uc.shape[0]

    def body(u_ref, w_ref, b_ref, o_ref):
        u = u_ref[...]
        mu = jnp.mean(u, axis=-1, keepdims=True)
        var = jnp.mean(jnp.square(u - mu), axis=-1, keepdims=True)
        v = (u - mu) * lax.rsqrt(var + 1e-5) * w_ref[...] + b_ref[...]
        o_ref[...] = _silu(v).astype(BF16)

    return pl.pallas_call(body, name=name, grid=(L // TR,), out_shape=_sds((L, D_CONF), BF16),
                          in_specs=[_row_spec(), _const_spec((1, D)), _const_spec((1, D))], out_specs=_row_spec(),
                          compiler_params=_cparams(("parallel",)))(uc, lnw, lnb)


def ln_silu_bwd(du_all, uc, lnw, lnb, name):
    L = uc.shape[0]

    def body(du_ref, u_ref, w_ref, b_ref, o_ref, acc_ref):
        @pl.when(pl.program_id(0) == 0)
        def _():
            acc_ref[...] = jnp.zeros((8, D), F32)

        u = u_ref[...]
        mu = jnp.mean(u, axis=-1, keepdims=True)
        rl = lax.rsqrt(jnp.mean(jnp.square(u - mu), axis=-1, keepdims=True) + 1e-5)
        n = (u - mu) * rl
        v = n * w_ref[...] + b_ref[...]
        dv = du_ref[...] * _dsilu(v)
        acc_ref[0:1, :] += _colsum(dv * n)
        acc_ref[1:2, :] += _colsum(dv)
        dn = dv * w_ref[...]
        o_ref[...] = rl * (dn - jnp.mean(dn, axis=-1, keepdims=True) - n * jnp.mean(dn * n, axis=-1, keepdims=True))

    return pl.pallas_call(body, name=name, grid=(L // TR,), out_shape=[_sds((L, D), F32), _sds((8, D), F32)],
                          in_specs=[_row_col_spec(D, 1), _row_spec(), _const_spec((1, D)), _const_spec((1, D))],
                          out_specs=[_row_spec(), _const_spec((8, D))],
                          compiler_params=_cparams(("arbitrary",)))(du_all, uc, lnw, lnb)


def final_loss(ff, x1, mod, fw, target, name):
    L = ff.shape[0]

    def body(ff_ref, x1_ref, mod_ref, fw_ref, t_ref, dx_ref, dff_ref, acc_ref):
        @pl.when(pl.program_id(0) == 0)
        def _():
            acc_ref[...] = jnp.zeros((8, D), F32)

        ff_v = ff_ref[...]
        g2 = mod_ref[5:6, :]
        x2 = x1_ref[...] + g2 * ff_v
        r = lax.rsqrt(jnp.mean(x2 * x2, axis=-1, keepdims=True) + 1e-6)
        n = x2 * r
        err = n * fw_ref[...] - t_ref[...]
        dy = err * (1.0 / D)
        dn = dy * fw_ref[...]
        dx2 = r * (dn - n * jnp.mean(dn * n, axis=-1, keepdims=True))
        acc_ref[0:1, :] += _colsum(dy * n)
        acc_ref[1:2, :] += _colsum(dx2 * ff_v)
        acc_ref[2:3, :] += _colsum(err * err)
        dx_ref[...] = dx2
        dff_ref[...] = (dx2 * g2).astype(BF16)

    return pl.pallas_call(
        body, name=name, grid=(L // TR,), out_shape=[_sds((L, D), F32), _sds((L, D), BF16), _sds((8, D), F32)],
        in_specs=[_row_spec(), _row_spec(), _const_spec((8, D)), _const_spec((1, D)), _row_spec()],
        out_specs=[_row_spec(), _row_spec(), _const_spec((8, D))],
        compiler_params=_cparams(("arbitrary",)))(ff, x1, mod, fw, target)


def norm_mod_bwd(dh, xin, dres, mod, w, shift_row, name, mix=None, gate_row=None):
    L = dh.shape[0]
    has_mix = mix is not None

    def body(*refs):
        if has_mix:
            dh_ref, x_ref, dres_ref, mod_ref, w_ref, mix_ref, dx_ref, dmix_ref, acc_ref = refs
        else:
            dh_ref, x_ref, dres_ref, mod_ref, w_ref, dx_ref, acc_ref = refs

        @pl.when(pl.program_id(0) == 0)
        def _():
            acc_ref[...] = jnp.zeros((8, D), F32)

        dh_v = dh_ref[...]
        x = x_ref[...]
        r = lax.rsqrt(jnp.mean(x * x, axis=-1, keepdims=True) + 1e-6)
        n = x * r
        nw = n * w_ref[...]
        sc1 = 1.0 + mod_ref[shift_row + 1:shift_row + 2, :]
        acc_ref[0:1, :] += _colsum(dh_v)
        acc_ref[1:2, :] += _colsum(dh_v * nw)
        dnw = dh_v * sc1
        acc_ref[2:3, :] += _colsum(dnw * n)
        dn = dnw * w_ref[...]
        dx = r * (dn - n * jnp.mean(dn * n, axis=-1, keepdims=True)) + dres_ref[...]
        dx_ref[...] = dx
        if has_mix:
            acc_ref[3:4, :] += _colsum(dx * mix_ref[...])
            dmix_ref[...] = (dx * mod_ref[gate_row:gate_row + 1, :]).astype(BF16)

    ins = [dh, xin, dres, mod, w] + ([mix] if has_mix else [])
    in_specs = [_row_spec(), _row_spec(), _row_spec(), _const_spec((8, D)), _const_spec((1, D))] + ([_row_spec()] if has_mix else [])
    out_shape = [_sds((L, D), F32)] + ([_sds((L, D), BF16)] if has_mix else []) + [_sds((8, D), F32)]
    out_specs = [_row_spec()] + ([_row_spec()] if has_mix else []) + [_const_spec((8, D))]
    return pl.pallas_call(body, name=name, grid=(L // TR,), out_shape=out_shape, in_specs=in_specs,
                          out_specs=out_specs, compiler_params=_cparams(("arbitrary",)))(*ins)


def _halo(k):
    return 8 if k <= 9 else 32


def _prev_spec(h, col0):
    return pl.BlockSpec((h, CB), lambda j, i: (jnp.maximum(i * (TR // h) - 1, 0), j + col0))


def _next_spec(h, col0, n_tiles):
    return pl.BlockSpec((h, CB), lambda j, i: (jnp.minimum(i + 1, n_tiles - 1) * (TR // h), j + col0))


def _tile_spec(col0):
    return pl.BlockSpec((TR, CB), lambda j, i: (i, j + col0))


def _w_spec(kp, col0):
    return pl.BlockSpec((kp, CB), lambda j, i: (0, j + col0))


SUBLANES = 8


def _shifted_windows(v, taps, rows):
    for r in range(SUBLANES):
        group = [(o, k) for o, k in taps if o % SUBLANES == r]
        if not group:
            continue
        s = v if r == 0 else pltpu.roll(v, v.shape[0] - r, 0)
        for o, k in group:
            yield k, s[o - r:o - r + rows, :]


def _causal_taps(ext_ref, w_ref, k_taps, first, rows):
    acc = None
    for k, win in _shifted_windows(ext_ref[...], [(first - (k_taps - 1) + k, k) for k in range(k_taps)], rows):
        t = w_ref[k:k + 1, :] * win
        acc = t if acc is None else acc + t
    return acc


def _anticausal_taps(d_ref, w_ref, k_taps, rows):
    acc = None
    for k, win in _shifted_windows(d_ref[...], [(k_taps - 1 - k, k) for k in range(k_taps)], rows):
        t = w_ref[k:k + 1, :] * win
        acc = t if acc is None else acc + t
    return acc


def _acc_conv_wgrad(dw_ref, d_tile, ext_ref, k_taps, first):
    for k, win in _shifted_windows(ext_ref[...], [(first - (k_taps - 1) + k, k) for k in range(k_taps)], TR):
        dw_ref[k:k + 1, :] += _colsum(d_tile * win)
    dw_ref[k_taps:k_taps + 1, :] += _colsum(d_tile)


def conv_silu_fwd(x, col0, width, w, b, name):
    L = x.shape[0]
    k_taps = w.shape[0]
    h = _halo(k_taps)

    def body(xp_ref, x_ref, w_ref, b_ref, o_ref, ext_ref):
        i = pl.program_id(1)
        ext_ref[0:h, :] = jnp.where(i > 0, xp_ref[...], 0.0)
        ext_ref[h:h + TR, :] = x_ref[...]
        o_ref[...] = _silu(_causal_taps(ext_ref, w_ref, k_taps, h, TR) + b_ref[...])

    return pl.pallas_call(
        body, name=name, grid=(width // CB, L // TR), out_shape=_sds((L, width), F32),
        in_specs=[_prev_spec(h, col0), _tile_spec(col0), _w_spec(k_taps, 0), pl.BlockSpec((1, CB), lambda j, i: (0, j))],
        out_specs=_tile_spec(0), scratch_shapes=[pltpu.VMEM((h + TR, CB), F32)],
        compiler_params=_cparams(("parallel", "parallel")))(x, x, w, b)


def conv_silu_bwd(x, col0, width, w, b, dpost, name):
    L = x.shape[0]
    k_taps = w.shape[0]
    h = _halo(k_taps)
    nt = L // TR

    def body(xp_ref, x_ref, xn_ref, d_ref, dn_ref, w_ref, b_ref, dx_ref, dw_ref, ext_ref, dpre_ref):
        i = pl.program_id(1)

        @pl.when(i == 0)
        def _():
            dw_ref[...] = jnp.zeros((8, CB), F32)

        ext_ref[0:h, :] = jnp.where(i > 0, xp_ref[...], 0.0)
        ext_ref[h:h + TR, :] = x_ref[...]
        ext_ref[h + TR:h + TR + h, :] = xn_ref[...]
        pre = _causal_taps(ext_ref, w_ref, k_taps, h, TR + h) + b_ref[...]
        dpre_ref[0:TR, :] = d_ref[...] * _dsilu(pre[0:TR, :])
        dpre_ref[TR:TR + h, :] = jnp.where(i < nt - 1, dn_ref[...], 0.0) * _dsilu(pre[TR:TR + h, :])
        dx_ref[...] = _anticausal_taps(dpre_ref, w_ref, k_taps, TR).astype(BF16)
        _acc_conv_wgrad(dw_ref, dpre_ref[0:TR, :], ext_ref, k_taps, h)

    return pl.pallas_call(
        body, name=name, grid=(width // CB, nt),
        out_shape=[_sds((L, width), BF16), _sds((8, width), F32)],
        in_specs=[_prev_spec(h, col0), _tile_spec(col0), _next_spec(h, col0, nt), _tile_spec(0), _next_spec(h, 0, nt),
                  _w_spec(k_taps, 0), pl.BlockSpec((1, CB), lambda j, i: (0, j))],
        out_specs=[_tile_spec(0), _w_spec(8, 0)],
        scratch_shapes=[pltpu.VMEM((h + TR + h, CB), F32), pltpu.VMEM((TR + h, CB), F32)],
        compiler_params=_cparams(("parallel", "arbitrary")))(x, x, x, dpost, dpost, w, b)


def conf_conv_fwd(proj, col_a, col_g, w, b, name):
    L = proj.shape[0]
    k_taps = w.shape[0]
    h = _halo(k_taps)

    def body(ap_ref, a_ref, gp_ref, g_ref, w_ref, b_ref, o_ref, ext_ref):
        i = pl.program_id(1)
        ext_ref[0:h, :] = jnp.where(i > 0, ap_ref[...] * _sigmoid(gp_ref[...]), 0.0)
        ext_ref[h:h + TR, :] = a_ref[...] * _sigmoid(g_ref[...])
        o_ref[...] = _causal_taps(ext_ref, w_ref, k_taps, h, TR) + b_ref[...]

    return pl.pallas_call(
        body, name=name, grid=(D_CONF // CB, L // TR), out_shape=_sds((L, D_CONF), F32),
        in_specs=[_prev_spec(h, col_a), _tile_spec(col_a), _prev_spec(h, col_g), _tile_spec(col_g), _w_spec(k_taps, 0),
                  pl.BlockSpec((1, CB), lambda j, i: (0, j))],
        out_specs=_tile_spec(0), scratch_shapes=[pltpu.VMEM((h + TR, CB), F32)],
        compiler_params=_cparams(("parallel", "parallel")))(proj, proj, proj, proj, w, b)


def conf_conv_bwd(proj, col_a, col_g, w, duc, name):
    L = proj.shape[0]
    k_taps = w.shape[0]
    h = _halo(k_taps)
    nt = L // TR

    def body(ap_ref, a_ref, gp_ref, g_ref, d_ref, dn_ref, w_ref, da_ref, dg_ref, dw_ref, ext_ref, dext_ref):
        i = pl.program_id(1)

        @pl.when(i == 0)
        def _():
            dw_ref[...] = jnp.zeros((32, CB), F32)

        a = a_ref[...]
        s = _sigmoid(g_ref[...])
        ext_ref[0:h, :] = jnp.where(i > 0, ap_ref[...] * _sigmoid(gp_ref[...]), 0.0)
        ext_ref[h:h + TR, :] = a * s
        dext_ref[0:TR, :] = d_ref[...]
        dext_ref[TR:TR + h, :] = jnp.where(i < nt - 1, dn_ref[...], 0.0)
        du0 = _anticausal_taps(dext_ref, w_ref, k_taps, TR)
        da_ref[...] = (du0 * s).astype(BF16)
        dg_ref[...] = (du0 * a * s * (1.0 - s)).astype(BF16)
        _acc_conv_wgrad(dw_ref, d_ref[...], ext_ref, k_taps, h)

    return pl.pallas_call(
        body, name=name, grid=(D_CONF // CB, nt),
        out_shape=[_sds((L, D_CONF), BF16), _sds((L, D_CONF), BF16), _sds((32, D_CONF), F32)],
        in_specs=[_prev_spec(h, col_a), _tile_spec(col_a), _prev_spec(h, col_g), _tile_spec(col_g), _tile_spec(0),
                  _next_spec(h, 0, nt), _w_spec(k_taps, 0)],
        out_specs=[_tile_spec(0), _tile_spec(0), _w_spec(32, 0)],
        scratch_shapes=[pltpu.VMEM((h + TR, CB), F32), pltpu.VMEM((TR + h, CB), F32)],
        compiler_params=_cparams(("parallel", "arbitrary")))(proj, proj, proj, proj, duc, duc, w)


def ffn_conv_fwd(up, w, b, name):
    L = up.shape[0]
    k_taps = w.shape[0]
    h = _halo(k_taps)
    cv = D_FF // CB

    def body(gp_ref, g_ref, vp_ref, v_ref, wg_ref, wv_ref, bg_ref, bv_ref, o_ref, eg_ref, ev_ref):
        i = pl.program_id(1)
        eg_ref[0:h, :] = jnp.where(i > 0, gp_ref[...], 0.0)
        eg_ref[h:h + TR, :] = g_ref[...]
        ev_ref[0:h, :] = jnp.where(i > 0, vp_ref[...], 0.0)
        ev_ref[h:h + TR, :] = v_ref[...]
        pg = _causal_taps(eg_ref, wg_ref, k_taps, h, TR) + bg_ref[...]
        pv = _causal_taps(ev_ref, wv_ref, k_taps, h, TR) + bv_ref[...]
        o_ref[...] = (_silu(pg) * pv).astype(BF16)

    bspec = lambda c0: pl.BlockSpec((1, CB), lambda j, i: (0, j + c0))
    return pl.pallas_call(
        body, name=name, grid=(cv, L // TR), out_shape=_sds((L, D_FF), BF16),
        in_specs=[_prev_spec(h, 0), _tile_spec(0), _prev_spec(h, cv), _tile_spec(cv), _w_spec(k_taps, 0), _w_spec(k_taps, cv),
                  bspec(0), bspec(cv)],
        out_specs=_tile_spec(0), scratch_shapes=[pltpu.VMEM((h + TR, CB), F32), pltpu.VMEM((h + TR, CB), F32)],
        compiler_params=_cparams(("parallel", "parallel")))(up, up, up, up, w, w, b, b)


def ffn_conv_bwd(up, w, b, dact, name):
    L = up.shape[0]
    k_taps = w.shape[0]
    h = _halo(k_taps)
    nt = L // TR
    cv = D_FF // CB

    def body(gp_ref, g_ref, gn_ref, vp_ref, v_ref, vn_ref, d_ref, dn_ref, wg_ref, wv_ref, bg_ref, bv_ref,
             dg_ref, dv_ref, dwg_ref, dwv_ref, eg_ref, ev_ref, pg_ref, pv_ref):
        i = pl.program_id(1)

        @pl.when(i == 0)
        def _():
            dwg_ref[...] = jnp.zeros((8, CB), F32)
            dwv_ref[...] = jnp.zeros((8, CB), F32)

        for e_ref, p_ref, c_ref, n_ref in ((eg_ref, gp_ref, g_ref, gn_ref), (ev_ref, vp_ref, v_ref, vn_ref)):
            e_ref[0:h, :] = jnp.where(i > 0, p_ref[...], 0.0)
            e_ref[h:h + TR, :] = c_ref[...]
            e_ref[h + TR:h + TR + h, :] = n_ref[...]
        pg = _causal_taps(eg_ref, wg_ref, k_taps, h, TR + h) + bg_ref[...]
        pv = _causal_taps(ev_ref, wv_ref, k_taps, h, TR + h) + bv_ref[...]
        dact_t = d_ref[...]
        dact_n = jnp.where(i < nt - 1, dn_ref[...], 0.0)
        pg_ref[0:TR, :] = dact_t * pv[0:TR, :] * _dsilu(pg[0:TR, :])
        pg_ref[TR:TR + h, :] = dact_n * pv[TR:TR + h, :] * _dsilu(pg[TR:TR + h, :])
        pv_ref[0:TR, :] = dact_t * _silu(pg[0:TR, :])
        pv_ref[TR:TR + h, :] = dact_n * _silu(pg[TR:TR + h, :])
        dg_ref[...] = _anticausal_taps(pg_ref, wg_ref, k_taps, TR).astype(BF16)
        dv_ref[...] = _anticausal_taps(pv_ref, wv_ref, k_taps, TR).astype(BF16)
        _acc_conv_wgrad(dwg_ref, pg_ref[0:TR, :], eg_ref, k_taps, h)
        _acc_conv_wgrad(dwv_ref, pv_ref[0:TR, :], ev_ref, k_taps, h)

    bspec = lambda c0: pl.BlockSpec((1, CB), lambda j, i: (0, j + c0))
    ext = pltpu.VMEM((h + TR + h, CB), F32)
    dpre = pltpu.VMEM((TR + h, CB), F32)
    return pl.pallas_call(
        body, name=name, grid=(cv, nt),
        out_shape=[_sds((L, D_FF), BF16), _sds((L, D_FF), BF16), _sds((8, D_FF), F32), _sds((8, D_FF), F32)],
        in_specs=[_prev_spec(h, 0), _tile_spec(0), _next_spec(h, 0, nt), _prev_spec(h, cv), _tile_spec(cv), _next_spec(h, cv, nt),
                  _tile_spec(0), _next_spec(h, 0, nt), _w_spec(k_taps, 0), _w_spec(k_taps, cv), bspec(0), bspec(cv)],
        out_specs=[_tile_spec(0), _tile_spec(0), _w_spec(8, 0), _w_spec(8, 0)],
        scratch_shapes=[ext, ext, dpre, dpre],
        compiler_params=_cparams(("parallel", "arbitrary")))(up, up, up, up, up, up, dact, dact, w, w, b, b)


def _ssd_common(xbc_ref, dt_ref, dtb_ref, alog_ref, cs_ref):
    xs = xbc_ref[:, 0:D_SSD]
    sp_in = dt_ref[...] + dtb_ref[...]
    dtf = _softplus(sp_in)
    a_f = -jnp.exp(alog_ref[...])
    a_dt = dtf * a_f
    row = lax.broadcasted_iota(jnp.int32, (Q, Q), 0)
    col = lax.broadcasted_iota(jnp.int32, (Q, Q), 1)
    causal = row >= col
    cs = _dot_exact(causal.astype(F32), a_dt, 3, "a")
    cs_ref[...] = cs
    cs_last = cs_ref[Q - 1:Q, :]
    return xs, sp_in, dtf, a_f, cs, cs_last, causal


def _head_decay(cs_j, cst_ref, e, causal):
    lane = lax.broadcasted_iota(jnp.int32, (Q, LANE), 1)
    rolled = pltpu.roll(cs_j, HEAD, 1)
    own = (lane < HEAD) if e == 0 else (lane >= HEAD)
    col_b = jnp.where(own, cs_j, rolled)
    col_b = jnp.concatenate([col_b] * (Q // LANE), axis=1)
    row_b = cst_ref[e * HEAD:e * HEAD + 1, :]
    return jnp.where(causal, jnp.exp(jnp.minimum(col_b - row_b, 0.0)), 0.0)


def ssd_fwd(xbc, proj, dtb_f, alog_f, dsk_f, snw, name):
    L = xbc.shape[0]
    nc = L // Q

    def body(xbc_ref, z_ref, dt_ref, dtb_ref, alog_ref, dsk_ref, snw_ref, y_ref, yn_ref, sp_ref, s_ref, cs_ref, cst_ref, yd_ref):
        @pl.when(pl.program_id(0) == 0)
        def _():
            s_ref[...] = jnp.zeros((N_STATE, D_SSD), F32)

        xs, _, dtf, a_f, cs, cs_last, causal = _ssd_common(xbc_ref, dt_ref, dtb_ref, alog_ref, cs_ref)
        e_cs = jnp.exp(cs)
        xdt = xs * dtf
        zst = jnp.exp(cs_last - cs) * xdt
        sp_ref[0] = s_ref[...]
        lane = lax.broadcasted_iota(jnp.int32, (Q, LANE), 1)
        for g in range(2):
            gl = slice(g * 512, g * 512 + 512)
            b_g = xbc_ref[:, D_SSD + g * N_STATE:D_SSD + (g + 1) * N_STATE]
            c_g = xbc_ref[:, D_SSD + 2 * N_STATE + g * N_STATE:D_SSD + 2 * N_STATE + (g + 1) * N_STATE]
            s_prev = s_ref[:, gl]
            cb = _dot_nt(c_g, b_g)
            yd_ref[:, gl] = e_cs[:, gl] * _dot(c_g, s_prev)
            for j in range(4):
                tl = slice(g * 512 + j * LANE, g * 512 + (j + 1) * LANE)
                cs_j = cs[:, tl]
                cst_ref[...] = cs_j.T
                x_j = xdt[:, tl]
                o0 = _dot(cb * _head_decay(cs_j, cst_ref, 0, causal), x_j)
                o1 = _dot(cb * _head_decay(cs_j, cst_ref, 1, causal), x_j)
                yd_ref[:, tl] += jnp.where(lane < HEAD, o0, o1)
            s_ref[:, gl] = jnp.exp(cs_last[:, gl]) * s_prev + _dot_tn(b_g, zst[:, gl])
        y = yd_ref[...] + xs * dsk_ref[...]
        y_ref[...] = y
        yz = y * _silu(z_ref[...])
        r = lax.rsqrt(jnp.mean(yz * yz, axis=-1, keepdims=True) + 1e-6)
        yn_ref[...] = (yz * r * snw_ref[...]).astype(BF16)

    chunk = lambda w, c: pl.BlockSpec((Q, w), lambda i: (i, c))
    return pl.pallas_call(
        body, name=name, grid=(nc,),
        out_shape=[_sds((L, D_SSD), F32), _sds((L, D_SSD), BF16), _sds((nc, N_STATE, D_SSD), F32)],
        in_specs=[chunk(D_XBC, 0), chunk(D, 0), chunk(D, 1)] + [_const_spec((1, D))] * 4,
        out_specs=[chunk(D, 0), chunk(D, 0), pl.BlockSpec((1, N_STATE, D_SSD), lambda i: (i, 0, 0))],
        scratch_shapes=[pltpu.VMEM((N_STATE, D_SSD), F32), pltpu.VMEM((Q, D_SSD), F32), pltpu.VMEM((LANE, Q), F32),
                        pltpu.VMEM((Q, D_SSD), F32)],
        compiler_params=_cparams(("arbitrary",)))(xbc, proj, proj, dtb_f, alog_f, dsk_f, snw)


def ssd_bwd(dmixin, y, xbc, proj, s_prev_all, dtb_f, alog_f, dsk_f, snw, name):
    L = xbc.shape[0]
    nc = L // Q

    def body(dyn_ref, y_ref, xbc_ref, z_ref, dt_ref, sp_ref, dtb_ref, alog_ref, dsk_ref, snw_ref,
             dz_ref, ddt_ref, dxbc_ref, acc_ref, acc16_ref, ds_ref, cs_ref, cst_ref, dcs_ref, dx_ref):
        step = pl.program_id(0)

        @pl.when(step == 0)
        def _():
            ds_ref[...] = jnp.zeros((N_STATE, D_SSD), F32)
            acc_ref[...] = jnp.zeros((8, D), F32)

        z = z_ref[...]
        y = y_ref[...]
        sz = _sigmoid(z)
        siluz = z * sz
        yz = y * siluz
        r = lax.rsqrt(jnp.mean(yz * yz, axis=-1, keepdims=True) + 1e-6)
        n = yz * r
        dyn = dyn_ref[...]
        acc_ref[0:1, :] += _colsum(dyn * n)
        dn = dyn * snw_ref[...]
        dyz = r * (dn - n * jnp.mean(dn * n, axis=-1, keepdims=True))
        dy = dyz * siluz
        dz_ref[...] = (dyz * y * (sz * (1.0 + z * (1.0 - sz)))).astype(BF16)

        xs, sp_in, dtf, a_f, cs, cs_last, causal = _ssd_common(xbc_ref, dt_ref, dtb_ref, alog_ref, cs_ref)
        acc_ref[3:4, :] += _colsum(dy * xs)
        e_cs = jnp.exp(cs)
        xdt = xs * dtf
        dst = jnp.exp(cs_last - cs)
        zst = dst * xdt
        e_last = jnp.exp(cs_last)
        lane = lax.broadcasted_iota(jnp.int32, (Q, LANE), 1)
        ones = jnp.ones((Q, LANE), F32)
        dcs_last_parts = []
        for g in range(2):
            gl = slice(g * 512, g * 512 + 512)
            b_g = xbc_ref[:, D_SSD + g * N_STATE:D_SSD + (g + 1) * N_STATE]
            c_g = xbc_ref[:, D_SSD + 2 * N_STATE + g * N_STATE:D_SSD + 2 * N_STATE + (g + 1) * N_STATE]
            s_prev = sp_ref[0, :, gl]
            ds_g = ds_ref[:, gl]
            dy_g = dy[:, gl]
            cb = _dot_nt(c_g, b_g)
            y_off = e_cs[:, gl] * _dot(c_g, s_prev)
            edy = e_cs[:, gl] * dy_g
            d_c = _dot_nt(edy, s_prev)
            d_z = _dot(b_g, ds_g)
            d_b = _dot_nt(zst[:, gl], ds_g)
            t_g = d_z * zst[:, gl]
            dcs_ref[:, gl] = dy_g * y_off - t_g
            dx_ref[:, gl] = d_z * dst[:, gl]
            dcs_last_parts.append(_colsum(t_g) + _colsum(ds_g * s_prev) * e_last[:, gl])
            ds_ref[:, gl] = e_last[:, gl] * ds_g + _dot_tn(c_g, edy)
            dcb = jnp.zeros((Q, Q), F32)
            for j in range(4):
                tl = slice(g * 512 + j * LANE, g * 512 + (j + 1) * LANE)
                cs_j = cs[:, tl]
                cst_ref[...] = cs_j.T
                x_j = xdt[:, tl]
                dy_j = dy[:, tl]
                dx_j = jnp.zeros((Q, LANE), F32)
                dcs_j = jnp.zeros((Q, LANE), F32)
                for e in range(2):
                    own = (lane < HEAD) if e == 0 else (lane >= HEAD)
                    w_h = _head_decay(cs_j, cst_ref, e, causal)
                    g_h = cb * w_h
                    dy_m = jnp.where(own, dy_j, 0.0)
                    d_g = _dot_nt(dy_m, x_j)
                    dx_j = dx_j + _dot_tn(g_h, dy_m)
                    dcb = dcb + d_g * w_h
                    p_h = d_g * g_h
                    row_sums = _dot_exact(p_h, ones, 2, "b")
                    col_sums = _dot_exact(p_h, ones, 2, "b", (((0,), (0,)), ((), ())))
                    dcs_j = dcs_j + jnp.where(own, row_sums - col_sums, 0.0)
                dcs_ref[:, tl] += dcs_j * (1.0 / HEAD)
                dx_ref[:, tl] += dx_j
            d_c = d_c + _dot(dcb, b_g)
            d_b = d_b + _dot_tn(dcb, c_g)
            dxbc_ref[:, D_SSD + g * N_STATE:D_SSD + (g + 1) * N_STATE] = d_b
            dxbc_ref[:, D_SSD + 2 * N_STATE + g * N_STATE:D_SSD + 2 * N_STATE + (g + 1) * N_STATE] = d_c
        dcs_last = jnp.concatenate(dcs_last_parts, axis=1)
        anticausal = lax.broadcasted_iota(jnp.int32, (Q, Q), 0) <= lax.broadcasted_iota(jnp.int32, (Q, Q), 1)
        d_adt = _dot_exact(anticausal.astype(F32), dcs_ref[...], 3, "a") + dcs_last
        dx = dx_ref[...]
        acc_ref[2:3, :] += _colsum(d_adt * dtf) * a_f
        d_dtf = d_adt * a_f + dx * xs
        dxbc_ref[:, 0:D_SSD] = dx * dtf + dy * dsk_ref[...]
        d_raw = d_dtf * _sigmoid(sp_in)
        acc_ref[1:2, :] += _colsum(d_raw)
        head_of_lane = lax.broadcasted_iota(jnp.int32, (D_SSD, LANE), 0) // HEAD
        fold = (head_of_lane == lax.broadcasted_iota(jnp.int32, (D_SSD, LANE), 1)).astype(F32)
        ddt_ref[...] = _dot_exact(d_raw, fold, 2, "b").astype(BF16)

        @pl.when(step == nc - 1)
        def _():
            acc16_ref[...] = _dot_exact(acc_ref[...], fold, 3, "b")

    rchunk = lambda w, c: pl.BlockSpec((Q, w), lambda i: (nc - 1 - i, c))
    return pl.pallas_call(
        body, name=name, grid=(nc,),
        out_shape=[_sds((L, D_SSD), BF16), _sds((L, LANE), BF16), _sds((L, D_XBC), F32), _sds((8, D), F32), _sds((8, LANE), F32)],
        in_specs=[rchunk(D, 0), rchunk(D, 0), rchunk(D_XBC, 0), rchunk(D, 0), rchunk(D, 1),
                  pl.BlockSpec((1, N_STATE, D_SSD), lambda i: (nc - 1 - i, 0, 0))] + [_const_spec((1, D))] * 4,
        out_specs=[rchunk(D, 0), rchunk(LANE, 0), rchunk(D_XBC, 0), _const_spec((8, D)), _const_spec((8, LANE))],
        scratch_shapes=[pltpu.VMEM((N_STATE, D_SSD), F32), pltpu.VMEM((Q, D_SSD), F32), pltpu.VMEM((LANE, Q), F32),
                        pltpu.VMEM((Q, D_SSD), F32), pltpu.VMEM((Q, D_SSD), F32)],
        compiler_params=_cparams(("arbitrary",)))(dmixin, y, xbc, proj, proj, s_prev_all, dtb_f, alog_f, dsk_f, snw)


def _adamw_math(w, g, m, v):
    m_n = ADAM_B1 * m + (1.0 - ADAM_B1) * g
    v_n = ADAM_B2 * v + (1.0 - ADAM_B2) * jnp.square(g)
    c1 = 1.0 - ADAM_B1 ** ADAM_STEP
    c2 = 1.0 - ADAM_B2 ** ADAM_STEP
    return -ADAM_LR * ((m_n / c1) / (jnp.sqrt(v_n / c2) + ADAM_EPS) + ADAM_WD * w), m_n, v_n


def _sum_slots(p_ref):
    acc = p_ref[0].astype(F32)
    for s in range(1, N_DEV):
        acc = acc + p_ref[s].astype(F32)
    return acc


def adamw_slots(w, slots, m, v, name):
    rows, cols = w.shape
    tr = next(t for t in (256, 128, 64, 32, 16) if rows % t == 0)

    def body(w_ref, s_ref, m_ref, v_ref, g_ref, d_ref, mo_ref, vo_ref):
        g_v = _sum_slots(s_ref)
        g_ref[...] = g_v
        d_ref[...], mo_ref[...], vo_ref[...] = _adamw_math(w_ref[...], g_v, m_ref[...], v_ref[...])

    spec = pl.BlockSpec((tr, cols), lambda i: (i, 0))
    return pl.pallas_call(body, name=name, grid=(rows // tr,), out_shape=[_sds((rows, cols), F32)] * 4,
                          in_specs=[spec, pl.BlockSpec((N_DEV, tr, cols), lambda i: (0, i, 0)), spec, spec], out_specs=[spec] * 4,
                          compiler_params=_cparams(("parallel",)))(w, slots, m, v)


def adamw_many(ws, gs, ms, vs, name):
    n = len(ws)

    def body(*refs):
        for p in range(n):
            d_v, m_v, v_v = _adamw_math(refs[p][...], refs[n + p][...], refs[2 * n + p][...], refs[3 * n + p][...])
            refs[4 * n + p][...] = d_v
            refs[5 * n + p][...] = m_v
            refs[6 * n + p][...] = v_v

    vm = pl.BlockSpec(memory_space=pltpu.VMEM)
    out = pl.pallas_call(body, name=name, out_shape=[_sds(w.shape, F32) for w in ws] * 3, in_specs=[vm] * (4 * n),
                         out_specs=[vm] * (3 * n), compiler_params=_cparams())(*ws, *gs, *ms, *vs)
    return out[:n], out[n:2 * n], out[2 * n:]


def sum_slots_many(parts, name):
    n = len(parts)

    def body(*refs):
        for p in range(n):
            refs[n + p][...] = _sum_slots(refs[p])

    vm = pl.BlockSpec(memory_space=pltpu.VMEM)
    return pl.pallas_call(body, name=name, out_shape=[_sds(p.shape[1:], F32) for p in parts], in_specs=[vm] * n,
                          out_specs=[vm] * n, compiler_params=_cparams())(*parts)


def ada_mod(c_all, ada_w_shard, ada_b_cols, name):
    def body(c_ref, w_ref, b_ref, o_ref, ca_ref):
        ca = _silu(c_ref[...])
        ca_ref[...] = ca
        o_ref[...] = _dot(ca, w_ref[...]) + b_ref[...]

    vm = pl.BlockSpec(memory_space=pltpu.VMEM)
    return pl.pallas_call(body, name=name, out_shape=[_sds((N_DEV, ada_w_shard.shape[1]), F32), _sds((N_DEV, D), F32)],
                          in_specs=[vm, vm, vm], out_specs=[vm, vm], compiler_params=_cparams())(c_all, ada_w_shard, ada_b_cols)


def ada_wgrad(c_act_all, dmod_cols, name):
    def body(c_ref, d_ref, o_ref):
        o_ref[...] = _dot_tn_hi(c_ref[...], d_ref[...])

    vm = pl.BlockSpec(memory_space=pltpu.VMEM)
    return pl.pallas_call(body, name=name, out_shape=_sds((D, dmod_cols.shape[1]), F32), in_specs=[vm, vm], out_specs=vm,
                          compiler_params=_cparams())(c_act_all, dmod_cols)


def exchange(srcs, name, gather):
    n = len(srcs)
    shapes = [tuple(s.shape) if gather else tuple(s.shape[1:]) for s in srcs]

    def body(*refs):
        src_refs, out_refs = refs[:n], refs[n:2 * n]
        send_sems, recv_sems, local_sems = refs[2 * n:]
        x, y, c = lax.axis_index("x"), lax.axis_index("y"), lax.axis_index("c")
        me = 4 * x + 2 * y + c

        def peer(k):
            bx, by, bc = (k >> 2) & 1, (k >> 1) & 1, k & 1
            px, py, pc = (x + bx) % 2, (y + by) % 2, (c + bc) % 2
            return (px, py, pc), 4 * px + 2 * py + pc

        def copy(a, k, landing):
            dev, idx = peer(k)
            return pltpu.make_async_remote_copy(
                src_ref=src_refs[a] if gather else src_refs[a].at[idx], dst_ref=out_refs[a].at[idx if landing else me],
                send_sem=send_sems.at[a, k - 1], recv_sem=recv_sems.at[a, k - 1],
                device_id=dev, device_id_type=pl.DeviceIdType.MESH)

        mine = [pltpu.make_async_copy(src_refs[a] if gather else src_refs[a].at[me], out_refs[a].at[me], local_sems.at[a])
                for a in range(n)]
        for cp in mine:
            cp.start()
        sends = [copy(a, k, False) for a in range(n) for k in range(1, N_DEV)]
        for cp in sends:
            cp.start()
        for a in range(n):
            for k in range(1, N_DEV):
                copy(a, k, True).wait_recv()
        for cp in sends:
            cp.wait_send()
        for cp in mine:
            cp.wait()

    hbm = pl.BlockSpec(memory_space=pl.ANY)
    return pl.pallas_call(
        body, name=name, out_shape=[_sds((N_DEV,) + shp, s.dtype) for shp, s in zip(shapes, srcs)], in_specs=[hbm] * n,
        out_specs=[hbm] * n,
        scratch_shapes=[pltpu.SemaphoreType.DMA((n, N_DEV - 1)), pltpu.SemaphoreType.DMA((n, N_DEV - 1)),
                        pltpu.SemaphoreType.DMA((n,))],
        compiler_params=pltpu.CompilerParams(has_side_effects=True))(*srcs)


def _peer(k):
    x, y, c = lax.axis_index("x"), lax.axis_index("y"), lax.axis_index("c")
    px, py, pc = (x + ((k >> 2) & 1)) % 2, (y + ((k >> 1) & 1)) % 2, (c + (k & 1)) % 2
    return (px, py, pc), 4 * px + 2 * py + pc


def _my_slot():
    return 4 * lax.axis_index("x") + 2 * lax.axis_index("y") + lax.axis_index("c")


_HBM = pl.BlockSpec(memory_space=pltpu.HBM)
_SEM = pl.BlockSpec(memory_space=pltpu.SEMAPHORE)
_EFFECT = pltpu.SideEffectType.DATAFLOW_SIDE_EFFECTING


def exchange_start(srcs, name, gather):
    n = len(srcs)
    shapes = [tuple(s.shape) if gather else tuple(s.shape[1:]) for s in srcs]
    lands = [lax.empty((N_DEV,) + shp, s.dtype) for shp, s in zip(shapes, srcs)]

    def body(*refs):
        src_refs, land_refs = refs[:n], refs[n:2 * n]
        sems = refs[2 * n:4 * n]
        token = refs[-1]
        me = _my_slot()
        for a in range(n):
            for k in range(1, N_DEV):
                dev, idx = _peer(k)
                pltpu.make_async_remote_copy(
                    src_ref=src_refs[a] if gather else src_refs[a].at[idx], dst_ref=land_refs[a].at[me],
                    send_sem=sems[2 * a].at[k - 1], recv_sem=sems[2 * a + 1].at[k - 1],
                    device_id=dev, device_id_type=pl.DeviceIdType.MESH).start()
        token[...] = jnp.zeros_like(token)

    out_shape = ([pltpu.SemaphoreType.DMA((N_DEV - 1,))] * (2 * n) + [pltpu.HBM(s.shape, s.dtype) for s in srcs]
                 + [pltpu.HBM(l.shape, l.dtype) for l in lands] + [_sds((8, LANE), F32)])
    out = pl.pallas_call(
        body, name=name, out_shape=out_shape, in_specs=[_HBM] * (2 * n),
        out_specs=[_SEM] * (2 * n) + [_HBM] * (2 * n) + [pl.BlockSpec(memory_space=pltpu.VMEM)],
        input_output_aliases={i: 2 * n + i for i in range(2 * n)},
        compiler_params=pltpu.CompilerParams(has_side_effects=_EFFECT))(
            *[pltpu.with_memory_space_constraint(s, pltpu.HBM) for s in srcs],
            *[pltpu.with_memory_space_constraint(l, pltpu.HBM) for l in lands])
    parts = [(out[2 * a], out[2 * a + 1], out[2 * n + a], out[3 * n + a]) for a in range(n)]
    return parts, out[-1]


def exchange_wait(parts, after, name, gather):
    n = len(parts)

    def body(*refs):
        src_refs, land_refs = refs[:n], refs[n:2 * n]
        sems = refs[2 * n:4 * n]
        for a in range(n):
            for k in range(1, N_DEV):
                dev, idx = _peer(k)
                copy = pltpu.make_async_remote_copy(
                    src_ref=src_refs[a] if gather else src_refs[a].at[idx], dst_ref=land_refs[a].at[idx],
                    send_sem=sems[2 * a].at[k - 1], recv_sem=sems[2 * a + 1].at[k - 1],
                    device_id=dev, device_id_type=pl.DeviceIdType.MESH)
                copy.wait_send()
                copy.wait_recv()

    srcs = [p[2] for p in parts]
    lands = [p[3] for p in parts]
    sems = [s for p in parts for s in p[:2]]
    out = pl.pallas_call(
        body, name=name, out_shape=[pltpu.HBM(a.shape, a.dtype) for a in srcs + lands],
        in_specs=[_HBM] * (2 * n) + [_SEM] * (2 * n) + [pl.BlockSpec(memory_space=pl.ANY)], out_specs=[_HBM] * (2 * n),
        input_output_aliases={i: i for i in range(2 * n)},
        compiler_params=pltpu.CompilerParams(has_side_effects=_EFFECT))(*srcs, *lands, *sems, after)
    return out[n:]


def _cols_to_slabs(g):
    r, c = g.shape
    return g.reshape(r, N_DEV, c // N_DEV).transpose(1, 0, 2)


def _slabs_to_cols(s):
    _, r, cs = s.shape
    return s.transpose(1, 0, 2).reshape(r, N_DEV * cs)


def _rep_heads(v):
    return jnp.repeat(v.reshape(N_HEADS), HEAD).reshape(1, D_SSD)


def local_fwd_bwd(x, target, mod, get_w, put_grad, small):
    n1w, n2w, fnw = small["norm1_w"], small["norm2_w"], small["final_norm_w"]
    dtb_f, alog_f, dsk_f = _rep_heads(small["dt_bias"]), _rep_heads(small["a_log"]), _rep_heads(small["d_skip"])
    snw = small["ssd_norm_w"]

    def after(v, token):
        return v + token[0:1, 0:1]

    h1 = norm_mod(x, mod, n1w, 0, "norm1")
    w_in = get_w("w_in", h1)
    proj = mm_nn([(h1, w_in["w_in_p"], 0)], "in_proj")
    xbc = conv_silu_fwd(proj, 4096 // CB, D_XBC, small["ssd_conv_w"], small["ssd_conv_b"], "ssd_conv")
    y, ysn, s_prev = ssd_fwd(xbc, proj, dtb_f, alog_f, dsk_f, snw, "ssd_scan")
    uc = conf_conv_fwd(proj, 2048 // CB, 3072 // CB, small["conf_conv_w"], small["conf_conv_b"], "conf_conv")
    u = ln_silu(uc, small["conf_ln_w"], small["conf_ln_b"], "conf_ln")
    w_out = get_w("w_out", u)
    mix = mm_nn([(ysn, w_out, 0), (u, w_out, 1)], "out_proj")
    h2, x1 = norm_mod(x, mod, n2w, 3, "norm2", res=mix, gate_row=2)
    w_up = get_w("w_up", h2)
    up = mm_nn([(h2, w_up, 0)], "up_proj")
    act = ffn_conv_fwd(up, small["ffn_conv_w"], small["ffn_conv_b"], "ffn_conv")
    w_down = get_w("w_down", act)
    ff = mm_nn([(act, w_down, 0)], "down_proj")
    dx2, dff, acc_f = final_loss(ff, x1, mod, fnw, target, "final_loss")

    token = put_grad("w_down", mm_tn(act, dff, "wgrad_down"))
    dact = mm_nt([(dff, w_down, 0)], "dact")
    dupg, dupv, dwg, dwv = ffn_conv_bwd(up, small["ffn_conv_w"], after(small["ffn_conv_b"], token), dact, "ffn_conv_bwd")
    token = put_grad("w_up", jnp.concatenate([mm_tn(h2, dupg, "wgrad_up_gate"), mm_tn(h2, dupv, "wgrad_up_val")], axis=1))
    dh2 = mm_nt([(dupg, w_up, 0), (dupv, w_up, 1)], "dh2")
    dx1, dmix, acc_2 = norm_mod_bwd(dh2, x1, dx2, mod, after(n2w, token), 3, "norm2_bwd", mix=mix, gate_row=2)

    token = put_grad("w_out", jnp.concatenate([mm_tn(ysn, dmix, "wgrad_out_ssd"), mm_tn(u, dmix, "wgrad_out_conf")], axis=0))
    dmixin = mm_nt([(dmix, w_out, 0)], "dmixin")
    duc, acc_ln = ln_silu_bwd(dmixin, uc, after(small["conf_ln_w"], token), small["conf_ln_b"], "conf_ln_bwd")
    dcfa, dcfg, dw_cc = conf_conv_bwd(proj, 2048 // CB, 3072 // CB, small["conf_conv_w"], duc, "conf_conv_bwd")
    dz, ddt, dxbc_post, acc_s, acc_s16 = ssd_bwd(dmixin, y, xbc, proj, s_prev, dtb_f, alog_f, dsk_f, snw, "ssd_scan_bwd")
    dxbc, dw_sc = conv_silu_bwd(proj, 4096 // CB, D_XBC, small["ssd_conv_w"], small["ssd_conv_b"], dxbc_post, "ssd_conv_bwd")
    token = put_grad("w_in", jnp.concatenate(
        [mm_tn(h1, dz, "wgrad_in_z"), mm_tn(h1, dxbc, "wgrad_in_xbc"), mm_tn(h1, ddt, "wgrad_in_dt")[:, :N_HEADS],
         mm_tn(h1, dcfa, "wgrad_in_cfa"), mm_tn(h1, dcfg, "wgrad_in_cfg")], axis=1))
    dh1 = mm_nt([(dz, w_in["w_z"], 0), (ddt, w_in["w_dt16"], 0), (dcfa, w_in["w_cfa"], 0), (dcfg, w_in["w_cfg"], 0),
                 (dxbc, w_in["w_xbc"], 0)], "dh1")
    grad_x, acc_1 = norm_mod_bwd(dh1, x, dx1, mod, after(n1w, token), 0, "norm1_bwd")

    small_accs = dict(acc_1=acc_1, acc_2=acc_2, acc_f=acc_f, acc_ln=acc_ln, acc_s=acc_s, acc_s16=acc_s16, dw_sc=dw_sc,
                      dw_cc=dw_cc, dwg=dwg, dwv=dwv)
    return grad_x, small_accs


def kernel(x, c, ada_w, ada_b, norm1_w, w_in, ssd_conv_w, ssd_conv_b, dt_bias, a_log, d_skip, ssd_norm_w, conf_conv_w, conf_conv_b, conf_ln_w, conf_ln_b, w_out, norm2_w, w_up, ffn_conv_w, ffn_conv_b, w_down, final_norm_w, loss_target, m_ada_w, m_ada_b, m_norm1_w, m_w_in, m_ssd_conv_w, m_ssd_conv_b, m_dt_bias, m_a_log, m_d_skip, m_ssd_norm_w, m_conf_conv_w, m_conf_conv_b, m_conf_ln_w, m_conf_ln_b, m_w_out, m_norm2_w, m_w_up, m_ffn_conv_w, m_ffn_conv_b, m_w_down, m_final_norm_w, v_ada_w, v_ada_b, v_norm1_w, v_w_in, v_ssd_conv_w, v_ssd_conv_b, v_dt_bias, v_a_log, v_d_skip, v_ssd_norm_w, v_conf_conv_w, v_conf_conv_b, v_conf_ln_w, v_conf_ln_b, v_w_out, v_norm2_w, v_w_up, v_ffn_conv_w, v_ffn_conv_b, v_w_down, v_final_norm_w):
    me = 4 * lax.axis_index("x") + 2 * lax.axis_index("y") + lax.axis_index("c")
    weights = dict(ada_w=ada_w, ada_b=ada_b, norm1_w=norm1_w, w_in=w_in, ssd_conv_w=ssd_conv_w, ssd_conv_b=ssd_conv_b,
                   dt_bias=dt_bias, a_log=a_log, d_skip=d_skip, ssd_norm_w=ssd_norm_w, conf_conv_w=conf_conv_w,
                   conf_conv_b=conf_conv_b, conf_ln_w=conf_ln_w, conf_ln_b=conf_ln_b, w_out=w_out, norm2_w=norm2_w, w_up=w_up,
                   ffn_conv_w=ffn_conv_w, ffn_conv_b=ffn_conv_b, w_down=w_down, final_norm_w=final_norm_w)
    moms_m = dict(ada_w=m_ada_w, ada_b=m_ada_b, norm1_w=m_norm1_w, w_in=m_w_in, ssd_conv_w=m_ssd_conv_w, ssd_conv_b=m_ssd_conv_b,
                  dt_bias=m_dt_bias, a_log=m_a_log, d_skip=m_d_skip, ssd_norm_w=m_ssd_norm_w, conf_conv_w=m_conf_conv_w,
                  conf_conv_b=m_conf_conv_b, conf_ln_w=m_conf_ln_w, conf_ln_b=m_conf_ln_b, w_out=m_w_out, norm2_w=m_norm2_w,
                  w_up=m_w_up, ffn_conv_w=m_ffn_conv_w, ffn_conv_b=m_ffn_conv_b, w_down=m_w_down, final_norm_w=m_final_norm_w)
    moms_v = dict(ada_w=v_ada_w, ada_b=v_ada_b, norm1_w=v_norm1_w, w_in=v_w_in, ssd_conv_w=v_ssd_conv_w, ssd_conv_b=v_ssd_conv_b,
                  dt_bias=v_dt_bias, a_log=v_a_log, d_skip=v_d_skip, ssd_norm_w=v_ssd_norm_w, conf_conv_w=v_conf_conv_w,
                  conf_conv_b=v_conf_conv_b, conf_ln_w=v_conf_ln_w, conf_ln_b=v_conf_ln_b, w_out=v_w_out, norm2_w=v_norm2_w,
                  w_up=v_w_up, ffn_conv_w=v_ffn_conv_w, ffn_conv_b=v_ffn_conv_b, w_down=v_w_down, final_norm_w=v_final_norm_w)
    names = list(weights)

    def to2d(a):
        return a[0] if a.ndim == 3 else a.reshape(1, -1)

    big = ("w_in", "w_out", "w_up", "w_down")

    c_all, scw_all, ccw_all, fcw_all = exchange([c.reshape(8, LANE), ssd_conv_w[0], conf_conv_w[0], ffn_conv_w[0]],
                                                "gather_small", gather=True)
    c_all = c_all.reshape(N_DEV, D)

    ada_cols = ada_w.shape[2]
    ada_b_cols = lax.dynamic_slice(ada_b, (0, me * ada_cols), (1, ada_cols))
    mod_cols, c_act_all = ada_mod(c_all, ada_w[0], ada_b_cols, "ada_mod")
    mod_parts, = exchange([jnp.pad(mod_cols, ((0, 0), (0, D - ada_cols))).reshape(N_DEV, 8, LANE)], "scatter_mod", gather=False)
    mod = mod_parts.reshape(N_DEV, D)[:, :ada_cols].reshape(6, D)
    mod = jnp.pad(mod, ((0, 2), (0, 0)))

    shards, mod = lax.optimization_barrier(([weights[n][0].astype(BF16) for n in big], mod))
    gather_parts, token = exchange_start(shards, "gather_weights_start", gather=True)
    mod = mod + token[0:1, 0:1]

    small = {n: to2d(weights[n]) for n in names if n not in ("ada_w",) + big}
    small["ssd_conv_w"] = _slabs_to_cols(scw_all)
    small["conf_conv_w"] = _slabs_to_cols(ccw_all)
    small["ffn_conv_w"] = _slabs_to_cols(fcw_all)

    def with_own(landed, own):
        return lax.dynamic_update_slice(landed, own[None], (me,) + (0,) * own.ndim)

    def get_w(n, after):
        a = big.index(n)
        landed, = exchange_wait([gather_parts[a]], after, "gather_" + n + "_wait", gather=True)
        slabs = with_own(landed, shards[a])
        if n == "w_out":
            return slabs.reshape(2 * D, D)
        if n == "w_down":
            return slabs.reshape(D_FF, D)
        full = _slabs_to_cols(slabs)
        if n == "w_up":
            return full
        w_z, w_xbc, w_dt, w_cfa, w_cfg = full[:, :1024], full[:, 1024:2560], full[:, 2560:2576], full[:, 2576:3600], full[:, 3600:]
        return dict(w_in_p=jnp.concatenate([w_z, jnp.repeat(w_dt, HEAD, axis=1), w_cfa, w_cfg, w_xbc], axis=1), w_z=w_z,
                    w_xbc=w_xbc, w_dt16=jnp.pad(w_dt, ((0, 0), (0, LANE - N_HEADS))), w_cfa=w_cfa, w_cfg=w_cfg)

    scatter_parts, sent = {}, {}

    def put_grad(n, g):
        slabs = _cols_to_slabs(g) if n in ("w_in", "w_up") else g.reshape(N_DEV, g.shape[0] // N_DEV, g.shape[1])
        sent[n] = slabs.astype(BF16)
        (scatter_parts[n],), token = exchange_start([sent[n]], "scatter_" + n + "_start", gather=False)
        return token

    grad_x, accs = local_fwd_bwd(x[0], loss_target[0], mod, get_w, put_grad, small)
    loss = lax.psum(0.5 / D * jnp.sum(accs["acc_f"][2:3]), ("x", "y", "c"))

    landed = exchange_wait([scatter_parts[n] for n in big], grad_x, "scatter_grads_wait", gather=False)
    grads, delta, new_m, new_v = {}, {}, {}, {}
    for n, slots in zip(big, landed):
        slots = with_own(slots, lax.dynamic_index_in_dim(sent[n], me, 0, keepdims=False))
        grads[n], delta[n], new_m[n], new_v[n] = adamw_slots(weights[n][0], slots, moms_m[n][0], moms_v[n][0], "adamw_" + n)

    order = ("acc_1", "acc_2", "acc_f", "acc_ln", "acc_s", "acc_s16", "dw_sc", "dw_cc", "dwg", "dwv")
    gathered = dict(zip(order, exchange([accs[k] for k in order], "gather_small_grads", gather=True)))
    red = dict(zip(order, sum_slots_many([gathered[k] for k in order], "sum_small_grads")))

    def mod_rows(a1, a2, af):
        return jnp.concatenate([a1[..., 0:2, :], a2[..., 3:4, :], a2[..., 0:2, :], af[..., 1:2, :]], axis=-2)

    dmod_all = mod_rows(gathered["acc_1"], gathered["acc_2"], gathered["acc_f"]).reshape(N_DEV, 6 * D)
    grads["ada_w"] = ada_wgrad(c_act_all, lax.dynamic_slice(dmod_all, (0, me * ada_cols), (N_DEV, ada_cols)), "ada_wgrad")

    def my_cols(full, k_taps):
        cols = full.shape[1] // N_DEV
        return lax.dynamic_slice(full, (0, me * cols), (k_taps, cols))

    fcw = jnp.concatenate([red["dwg"], red["dwv"]], axis=1)
    grads.update(
        ada_b=mod_rows(red["acc_1"], red["acc_2"], red["acc_f"]).reshape(1, 6 * D), norm1_w=red["acc_1"][2:3],
        ssd_conv_w=my_cols(red["dw_sc"], K_SSD), ssd_conv_b=red["dw_sc"][K_SSD:K_SSD + 1],
        dt_bias=red["acc_s16"][1:2, :N_HEADS], a_log=red["acc_s16"][2:3, :N_HEADS], d_skip=red["acc_s16"][3:4, :N_HEADS],
        ssd_norm_w=red["acc_s"][0:1], conf_conv_w=my_cols(red["dw_cc"], K_CONF), conf_conv_b=red["dw_cc"][K_CONF:K_CONF + 1],
        conf_ln_w=red["acc_ln"][0:1], conf_ln_b=red["acc_ln"][1:2], norm2_w=red["acc_2"][2:3],
        ffn_conv_w=my_cols(fcw, K_FFN), ffn_conv_b=fcw[K_FFN:K_FFN + 1], final_norm_w=red["acc_f"][0:1])

    rest = [n for n in names if n not in big]
    d_l, m_l, v_l = adamw_many([to2d(weights[n]) for n in rest], [grads[n] for n in rest], [to2d(moms_m[n]) for n in rest],
                               [to2d(moms_v[n]) for n in rest], "adamw_small")
    for n, dd, mm, vv in zip(rest, d_l, m_l, v_l):
        delta[n], new_m[n], new_v[n] = dd, mm, vv
    shape_of = lambda d_: {n: d_[n].reshape(weights[n].shape) for n in names}
    grads, delta, new_m, new_v = shape_of(grads), shape_of(delta), shape_of(new_m), shape_of(new_v)
    return (loss, grad_x[None], *[grads[n] for n in names], *[delta[n] for n in names], *[new_m[n] for n in names],
            *[new_v[n] for n in names])
```

```python
import functools

import jax
import jax.numpy as jnp
from jax import lax
from jax.experimental import pallas as pl
from jax.experimental.pallas import tpu as pltpu

F32 = jnp.float32
BF16 = jnp.bfloat16
HI = lax.Precision.HIGHEST

N_DEV = 8
D = 1024
D_SSD = 1024
HEAD = 64
N_HEADS = 16
N_STATE = 128
D_XBC = 1536
D_CONF = 1024
D_FF = 2816
K_SSD, K_CONF, K_FFN = 4, 31, 3
D_INP = 5632
LANE = 128
TR = 256
TM = 512
Q = 256
CB = 256
TC = 1024
VMEM_LIMIT = 56 * 1024 * 1024

ADAM_LR, ADAM_B1, ADAM_B2, ADAM_EPS, ADAM_WD, ADAM_STEP = 0.001, 0.9, 0.999, 1e-08, 0.01, 10


def _cparams(sem=None):
    return pltpu.CompilerParams(vmem_limit_bytes=VMEM_LIMIT, dimension_semantics=sem)


def _sds(shape, dtype):
    return jax.ShapeDtypeStruct(shape, dtype)


def _sigmoid(x):
    return 1.0 / (1.0 + jnp.exp(-x))


def _silu(x):
    return x * _sigmoid(x)


def _dsilu(x):
    s = _sigmoid(x)
    return s * (1.0 + x * (1.0 - s))


def _softplus(x):
    return jnp.maximum(x, 0.0) + jnp.log(1.0 + jnp.exp(-jnp.abs(x)))


def _dot(a, b):
    return jnp.dot(a.astype(BF16), b.astype(BF16), preferred_element_type=F32)


def _dot_nt(a, b):
    return lax.dot_general(a.astype(BF16), b.astype(BF16), (((1,), (1,)), ((), ())), preferred_element_type=F32)


def _dot_tn(a, b):
    return lax.dot_general(a.astype(BF16), b.astype(BF16), (((0,), (0,)), ((), ())), preferred_element_type=F32)


def _dot_hi(a, b):
    return jnp.dot(a, b, precision=HI, preferred_element_type=F32)


def _bf16_terms(a, terms):
    parts, rem = [], a
    for t in range(terms):
        p = rem.astype(BF16)
        parts.append(p)
        if t + 1 < terms:
            rem = rem - p.astype(F32)
    return parts


def _dot_exact(a, b, terms, exact, dims=(((1,), (0,)), ((), ()))):
    if exact == "a":
        a_b = a.astype(BF16)
        outs = [lax.dot_general(a_b, p, dims, preferred_element_type=F32) for p in _bf16_terms(b, terms)]
    else:
        b_b = b.astype(BF16)
        outs = [lax.dot_general(p, b_b, dims, preferred_element_type=F32) for p in _bf16_terms(a, terms)]
    acc = outs[-1]
    for o in reversed(outs[:-1]):
        acc = acc + o
    return acc


def _dot_tn_hi(a, b):
    return lax.dot_general(a, b, (((0,), (0,)), ((), ())), precision=HI, preferred_element_type=F32)


def _colsum(x):
    return jnp.sum(x, axis=0, keepdims=True)


def _const_spec(shape):
    return pl.BlockSpec(shape, lambda *_: (0,) * len(shape))


def _col_tile(n):
    for t in (1408, 1024, 768, 512, 256, 128):
        if n % t == 0 and t <= n:
            return t
    return n


def mm_nn(pairs, name):
    L = pairs[0][0].shape[0]
    N = pairs[0][1].shape[1]
    tn = _col_tile(N)
    n = len(pairs)

    def body(*refs):
        acc = None
        for p in range(n):
            t = jnp.dot(refs[2 * p][...], refs[2 * p + 1][...], preferred_element_type=F32)
            acc = t if acc is None else acc + t
        refs[-1][...] = acc

    in_specs, args = [], []
    for a, w, rb in pairs:
        in_specs += [pl.BlockSpec((TM, a.shape[1]), lambda j, i: (i, 0)),
                     pl.BlockSpec((a.shape[1], tn), functools.partial(lambda j, i, rb: (rb, j), rb=rb))]
        args += [a, w]
    return pl.pallas_call(
        body, name=name, grid=(N // tn, L // TM), out_shape=_sds((L, N), F32), in_specs=in_specs,
        out_specs=pl.BlockSpec((TM, tn), lambda j, i: (i, j)),
        compiler_params=_cparams(("parallel", "parallel")))(*args)


def mm_nt(pairs, name):
    L = pairs[0][0].shape[0]
    K = pairs[0][1].shape[0]
    tk = _col_tile(K)
    n = len(pairs)

    def body(*refs):
        o_ref = refs[-1]
        acc = None
        for p in range(n):
            t = lax.dot_general(refs[2 * p][...], refs[2 * p + 1][...], (((1,), (1,)), ((), ())),
                                preferred_element_type=F32)
            acc = t if acc is None else acc + t
        o_ref[...] = acc

    in_specs, args = [], []
    for a, w, cb in pairs:
        in_specs += [pl.BlockSpec((TM, a.shape[1]), lambda j, i: (i, 0)),
                     pl.BlockSpec((tk, a.shape[1]), functools.partial(lambda j, i, cb: (j, cb), cb=cb))]
        args += [a, w]
    return pl.pallas_call(
        body, name=name, grid=(K // tk, L // TM), out_shape=_sds((L, K), F32), in_specs=in_specs,
        out_specs=pl.BlockSpec((TM, tk), lambda j, i: (i, j)),
        compiler_params=_cparams(("parallel", "parallel")))(*args)


def mm_tn(a, g, name):
    L, M = a.shape
    N = g.shape[1]
    tn = _col_tile(N) if N > 1024 else N
    if M * tn * 4 > 8 * 1024 * 1024:
        tn = 512
    tl = 512 if L % 512 == 0 else TR

    def body(a_ref, g_ref, o_ref):
        @pl.when(pl.program_id(1) == 0)
        def _():
            o_ref[...] = jnp.zeros((M, tn), F32)

        o_ref[...] += lax.dot_general(a_ref[...], g_ref[...], (((0,), (0,)), ((), ())), preferred_element_type=F32)

    return pl.pallas_call(
        body, name=name, grid=(N // tn, L // tl), out_shape=_sds((M, N), F32),
        in_specs=[pl.BlockSpec((tl, M), lambda j, l: (l, 0)), pl.BlockSpec((tl, tn), lambda j, l: (l, j))],
        out_specs=pl.BlockSpec((M, tn), lambda j, l: (0, j)),
        compiler_params=_cparams(("parallel", "arbitrary")))(a, g)


def _row_spec(width=D):
    return pl.BlockSpec((TR, width), lambda i: (i, 0))


def _row_col_spec(width, col):
    return pl.BlockSpec((TR, width), lambda i: (i, col))


def norm_mod(x, mod, w, shift_row, name, res=None, gate_row=None):
    L = x.shape[0]
    has_res = res is not None

    def body(*refs):
        if has_res:
            x_ref, res_ref, mod_ref, w_ref, h_ref, xo_ref = refs
            xin = x_ref[...] + mod_ref[gate_row:gate_row + 1, :] * res_ref[...]
            xo_ref[...] = xin
        else:
            x_ref, mod_ref, w_ref, h_ref = refs
            xin = x_ref[...]
        r = lax.rsqrt(jnp.mean(xin * xin, axis=-1, keepdims=True) + 1e-6)
        h = (xin * r * w_ref[...]) * (1.0 + mod_ref[shift_row + 1:shift_row + 2, :]) + mod_ref[shift_row:shift_row + 1, :]
        h_ref[...] = h.astype(BF16)

    ins = [x] + ([res] if has_res else []) + [mod, w]
    in_specs = [_row_spec()] + ([_row_spec()] if has_res else []) + [_const_spec((8, D)), _const_spec((1, D))]
    out_shape = [_sds((L, D), BF16)] + ([_sds((L, D), F32)] if has_res else [])
    out_specs = [_row_spec()] + ([_row_spec()] if has_res else [])
    out = pl.pallas_call(body, name=name, grid=(L // TR,), out_shape=out_shape, in_specs=in_specs,
                         out_specs=out_specs, compiler_params=_cparams(("parallel",)))(*ins)
    return out if has_res else out[0]


def ln_silu(uc, lnw, lnb, name):
    L = uc.shape[0]

    def body(u_ref, w_ref, b_ref, o_ref):
        u = u_ref[...]
        mu = jnp.mean(u, axis=-1, keepdims=True)
        var = jnp.mean(jnp.square(u - mu), axis=-1, keepdims=True)
        v = (u - mu) * lax.rsqrt(var + 1e-5) * w_ref[...] + b_ref[...]
        o_ref[...] = _silu(v).astype(BF16)

    return pl.pallas_call(body, name=name, grid=(L // TR,), out_shape=_sds((L, D_CONF), BF16),
                          in_specs=[_row_spec(), _const_spec((1, D)), _const_spec((1, D))], out_specs=_row_spec(),
                          compiler_params=_cparams(("parallel",)))(uc, lnw, lnb)


def ln_silu_bwd(du_all, uc, lnw, lnb, name):
    L = uc.shape[0]

    def body(du_ref, u_ref, w_ref, b_ref, o_ref, acc_ref):
        @pl.when(pl.program_id(0) == 0)
        def _():
            acc_ref[...] = jnp.zeros((8, D), F32)

        u = u_ref[...]
        mu = jnp.mean(u, axis=-1, keepdims=True)
        rl = lax.rsqrt(jnp.mean(jnp.square(u - mu), axis=-1, keepdims=True) + 1e-5)
        n = (u - mu) * rl
        v = n * w_ref[...] + b_ref[...]
        dv = du_ref[...] * _dsilu(v)
        acc_ref[0:1, :] += _colsum(dv * n)
        acc_ref[1:2, :] += _colsum(dv)
        dn = dv * w_ref[...]
        o_ref[...] = rl * (dn - jnp.mean(dn, axis=-1, keepdims=True) - n * jnp.mean(dn * n, axis=-1, keepdims=True))

    return pl.pallas_call(body, name=name, grid=(L // TR,), out_shape=[_sds((L, D), F32), _sds((8, D), F32)],
                          in_specs=[_row_col_spec(D, 1), _row_spec(), _const_spec((1, D)), _const_spec((1, D))],
                          out_specs=[_row_spec(), _const_spec((8, D))],
                          compiler_params=_cparams(("arbitrary",)))(du_all, uc, lnw, lnb)


def final_loss(ff, x1, mod, fw, target, name):
    L = ff.shape[0]

    def body(ff_ref, x1_ref, mod_ref, fw_ref, t_ref, dx_ref, dff_ref, acc_ref):
        @pl.when(pl.program_id(0) == 0)
        def _():
            acc_ref[...] = jnp.zeros((8, D), F32)

        ff_v = ff_ref[...]
        g2 = mod_ref[5:6, :]
        x2 = x1_ref[...] + g2 * ff_v
        r = lax.rsqrt(jnp.mean(x2 * x2, axis=-1, keepdims=True) + 1e-6)
        n = x2 * r
        err = n * fw_ref[...] - t_ref[...]
        dy = err * (1.0 / D)
        dn = dy * fw_ref[...]
        dx2 = r * (dn - n * jnp.mean(dn * n, axis=-1, keepdims=True))
        acc_ref[0:1, :] += _colsum(dy * n)
        acc_ref[1:2, :] += _colsum(dx2 * ff_v)
        acc_ref[2:3, :] += _colsum(err * err)
        dx_ref[...] = dx2
        dff_ref[...] = (dx2 * g2).astype(BF16)

    return pl.pallas_call(
        body, name=name, grid=(L // TR,), out_shape=[_sds((L, D), F32), _sds((L, D), BF16), _sds((8, D), F32)],
        in_specs=[_row_spec(), _row_spec(), _const_spec((8, D)), _const_spec((1, D)), _row_spec()],
        out_specs=[_row_spec(), _row_spec(), _const_spec((8, D))],
        compiler_params=_cparams(("arbitrary",)))(ff, x1, mod, fw, target)


def norm_mod_bwd(dh, xin, dres, mod, w, shift_row, name, mix=None, gate_row=None):
    L = dh.shape[0]
    has_mix = mix is not None

    def body(*refs):
        if has_mix:
            dh_ref, x_ref, dres_ref, mod_ref, w_ref, mix_ref, dx_ref, dmix_ref, acc_ref = refs
        else:
            dh_ref, x_ref, dres_ref, mod_ref, w_ref, dx_ref, acc_ref = refs

        @pl.when(pl.program_id(0) == 0)
        def _():
            acc_ref[...] = jnp.zeros((8, D), F32)

        dh_v = dh_ref[...]
        x = x_ref[...]
        r = lax.rsqrt(jnp.mean(x * x, axis=-1, keepdims=True) + 1e-6)
        n = x * r
        nw = n * w_ref[...]
        sc1 = 1.0 + mod_ref[shift_row + 1:shift_row + 2, :]
        acc_ref[0:1, :] += _colsum(dh_v)
        acc_ref[1:2, :] += _colsum(dh_v * nw)
        dnw = dh_v * sc1
        acc_ref[2:3, :] += _colsum(dnw * n)
        dn = dnw * w_ref[...]
        dx = r * (dn - n * jnp.mean(dn * n, axis=-1, keepdims=True)) + dres_ref[...]
        dx_ref[...] = dx
        if has_mix:
            acc_ref[3:4, :] += _colsum(dx * mix_ref[...])
            dmix_ref[...] = (dx * mod_ref[gate_row:gate_row + 1, :]).astype(BF16)

    ins = [dh, xin, dres, mod, w] + ([mix] if has_mix else [])
    in_specs = [_row_spec(), _row_spec(), _row_spec(), _const_spec((8, D)), _const_spec((1, D))] + ([_row_spec()] if has_mix else [])
    out_shape = [_sds((L, D), F32)] + ([_sds((L, D), BF16)] if has_mix else []) + [_sds((8, D), F32)]
    out_specs = [_row_spec()] + ([_row_spec()] if has_mix else []) + [_const_spec((8, D))]
    return pl.pallas_call(body, name=name, grid=(L // TR,), out_shape=out_shape, in_specs=in_specs,
                          out_specs=out_specs, compiler_params=_cparams(("arbitrary",)))(*ins)


def _halo(k):
    return 8 if k <= 9 else 32


def _prev_spec(h, col0):
    return pl.BlockSpec((h, CB), lambda j, i: (jnp.maximum(i * (TC // h) - 1, 0), j + col0))


def _next_spec(h, col0, n_tiles):
    return pl.BlockSpec((h, CB), lambda j, i: (jnp.minimum(i + 1, n_tiles - 1) * (TC // h), j + col0))


def _tile_spec(col0):
    return pl.BlockSpec((TC, CB), lambda j, i: (i, j + col0))


def _w_spec(kp, col0):
    return pl.BlockSpec((kp, CB), lambda j, i: (0, j + col0))


SUBLANES = 8


def _shifted_windows(v, taps, rows):
    for r in range(SUBLANES):
        group = [(o, k) for o, k in taps if o % SUBLANES == r]
        if not group:
            continue
        s = v if r == 0 else pltpu.roll(v, v.shape[0] - r, 0)
        for o, k in group:
            yield k, s[o - r:o - r + rows, :]


def _causal_taps(ext_ref, w_ref, k_taps, first, rows):
    acc = None
    for k, win in _shifted_windows(ext_ref[...], [(first - (k_taps - 1) + k, k) for k in range(k_taps)], rows):
        t = w_ref[k:k + 1, :] * win
        acc = t if acc is None else acc + t
    return acc


def _anticausal_taps(d_ref, w_ref, k_taps, rows):
    acc = None
    for k, win in _shifted_windows(d_ref[...], [(k_taps - 1 - k, k) for k in range(k_taps)], rows):
        t = w_ref[k:k + 1, :] * win
        acc = t if acc is None else acc + t
    return acc


def _acc_conv_wgrad(dw_ref, d_tile, ext_ref, k_taps, first):
    for k, win in _shifted_windows(ext_ref[...], [(first - (k_taps - 1) + k, k) for k in range(k_taps)], TC):
        dw_ref[k:k + 1, :] += _colsum(d_tile * win)
    dw_ref[k_taps:k_taps + 1, :] += _colsum(d_tile)


def conv_silu_fwd(x, col0, width, w, b, name):
    L = x.shape[0]
    k_taps = w.shape[0]
    h = _halo(k_taps)

    def body(xp_ref, x_ref, w_ref, b_ref, o_ref, ext_ref):
        i = pl.program_id(1)
        ext_ref[0:h, :] = jnp.where(i > 0, xp_ref[...], 0.0)
        ext_ref[h:h + TC, :] = x_ref[...]
        o_ref[...] = _silu(_causal_taps(ext_ref, w_ref, k_taps, h, TC) + b_ref[...])

    return pl.pallas_call(
        body, name=name, grid=(width // CB, L // TC), out_shape=_sds((L, width), F32),
        in_specs=[_prev_spec(h, col0), _tile_spec(col0), _w_spec(k_taps, 0), pl.BlockSpec((1, CB), lambda j, i: (0, j))],
        out_specs=_tile_spec(0), scratch_shapes=[pltpu.VMEM((h + TC, CB), F32)],
        compiler_params=_cparams(("parallel", "parallel")))(x, x, w, b)


def conv_silu_bwd(x, col0, width, w, b, dpost, name):
    L = x.shape[0]
    k_taps = w.shape[0]
    h = _halo(k_taps)
    nt = L // TC

    def body(xp_ref, x_ref, xn_ref, d_ref, dn_ref, w_ref, b_ref, dx_ref, dw_ref, ext_ref, dpre_ref):
        i = pl.program_id(1)

        @pl.when(i == 0)
        def _():
            dw_ref[...] = jnp.zeros((8, CB), F32)

        ext_ref[0:h, :] = jnp.where(i > 0, xp_ref[...], 0.0)
        ext_ref[h:h + TC, :] = x_ref[...]
        ext_ref[h + TC:h + TC + h, :] = xn_ref[...]
        pre = _causal_taps(ext_ref, w_ref, k_taps, h, TC + h) + b_ref[...]
        dpre_ref[0:TC, :] = d_ref[...] * _dsilu(pre[0:TC, :])
        dpre_ref[TC:TC + h, :] = jnp.where(i < nt - 1, dn_ref[...], 0.0) * _dsilu(pre[TC:TC + h, :])
        dx_ref[...] = _anticausal_taps(dpre_ref, w_ref, k_taps, TC).astype(BF16)
        _acc_conv_wgrad(dw_ref, dpre_ref[0:TC, :], ext_ref, k_taps, h)

    return pl.pallas_call(
        body, name=name, grid=(width // CB, nt),
        out_shape=[_sds((L, width), BF16), _sds((8, width), F32)],
        in_specs=[_prev_spec(h, col0), _tile_spec(col0), _next_spec(h, col0, nt), _tile_spec(0), _next_spec(h, 0, nt),
                  _w_spec(k_taps, 0), pl.BlockSpec((1, CB), lambda j, i: (0, j))],
        out_specs=[_tile_spec(0), _w_spec(8, 0)],
        scratch_shapes=[pltpu.VMEM((h + TC + h, CB), F32), pltpu.VMEM((TC + h, CB), F32)],
        compiler_params=_cparams(("parallel", "arbitrary")))(x, x, x, dpost, dpost, w, b)


def conf_conv_fwd(proj, col_a, col_g, w, b, name):
    L = proj.shape[0]
    k_taps = w.shape[0]
    h = _halo(k_taps)

    def body(ap_ref, a_ref, gp_ref, g_ref, w_ref, b_ref, o_ref, ext_ref):
        i = pl.program_id(1)
        ext_ref[0:h, :] = jnp.where(i > 0, ap_ref[...] * _sigmoid(gp_ref[...]), 0.0)
        ext_ref[h:h + TC, :] = a_ref[...] * _sigmoid(g_ref[...])
        o_ref[...] = _causal_taps(ext_ref, w_ref, k_taps, h, TC) + b_ref[...]

    return pl.pallas_call(
        body, name=name, grid=(D_CONF // CB, L // TC), out_shape=_sds((L, D_CONF), F32),
        in_specs=[_prev_spec(h, col_a), _tile_spec(col_a), _prev_spec(h, col_g), _tile_spec(col_g), _w_spec(k_taps, 0),
                  pl.BlockSpec((1, CB), lambda j, i: (0, j))],
        out_specs=_tile_spec(0), scratch_shapes=[pltpu.VMEM((h + TC, CB), F32)],
        compiler_params=_cparams(("parallel", "parallel")))(proj, proj, proj, proj, w, b)


def conf_conv_bwd(proj, col_a, col_g, w, duc, name):
    L = proj.shape[0]
    k_taps = w.shape[0]
    h = _halo(k_taps)
    nt = L // TC

    def body(ap_ref, a_ref, gp_ref, g_ref, d_ref, dn_ref, w_ref, da_ref, dg_ref, dw_ref, ext_ref, dext_ref):
        i = pl.program_id(1)

        @pl.when(i == 0)
        def _():
            dw_ref[...] = jnp.zeros((32, CB), F32)

        a = a_ref[...]
        s = _sigmoid(g_ref[...])
        ext_ref[0:h, :] = jnp.where(i > 0, ap_ref[...] * _sigmoid(gp_ref[...]), 0.0)
        ext_ref[h:h + TC, :] = a * s
        dext_ref[0:TC, :] = d_ref[...]
        dext_ref[TC:TC + h, :] = jnp.where(i < nt - 1, dn_ref[...], 0.0)
        du0 = _anticausal_taps(dext_ref, w_ref, k_taps, TC)
        da_ref[...] = (du0 * s).astype(BF16)
        dg_ref[...] = (du0 * a * s * (1.0 - s)).astype(BF16)
        _acc_conv_wgrad(dw_ref, d_ref[...], ext_ref, k_taps, h)

    return pl.pallas_call(
        body, name=name, grid=(D_CONF // CB, nt),
        out_shape=[_sds((L, D_CONF), BF16), _sds((L, D_CONF), BF16), _sds((32, D_CONF), F32)],
        in_specs=[_prev_spec(h, col_a), _tile_spec(col_a), _prev_spec(h, col_g), _tile_spec(col_g), _tile_spec(0),
                  _next_spec(h, 0, nt), _w_spec(k_taps, 0)],
        out_specs=[_tile_spec(0), _tile_spec(0), _w_spec(32, 0)],
        scratch_shapes=[pltpu.VMEM((h + TC, CB), F32), pltpu.VMEM((TC + h, CB), F32)],
        compiler_params=_cparams(("parallel", "arbitrary")))(proj, proj, proj, proj, duc, duc, w)


def ffn_conv_fwd(up, w, b, name):
    L = up.shape[0]
    k_taps = w.shape[0]
    h = _halo(k_taps)
    cv = D_FF // CB

    def body(gp_ref, g_ref, vp_ref, v_ref, wg_ref, wv_ref, bg_ref, bv_ref, o_ref, eg_ref, ev_ref):
        i = pl.program_id(1)
        eg_ref[0:h, :] = jnp.where(i > 0, gp_ref[...], 0.0)
        eg_ref[h:h + TC, :] = g_ref[...]
        ev_ref[0:h, :] = jnp.where(i > 0, vp_ref[...], 0.0)
        ev_ref[h:h + TC, :] = v_ref[...]
        pg = _causal_taps(eg_ref, wg_ref, k_taps, h, TC) + bg_ref[...]
        pv = _causal_taps(ev_ref, wv_ref, k_taps, h, TC) + bv_ref[...]
        o_ref[...] = (_silu(pg) * pv).astype(BF16)

    bspec = lambda c0: pl.BlockSpec((1, CB), lambda j, i: (0, j + c0))
    return pl.pallas_call(
        body, name=name, grid=(cv, L // TC), out_shape=_sds((L, D_FF), BF16),
        in_specs=[_prev_spec(h, 0), _tile_spec(0), _prev_spec(h, cv), _tile_spec(cv), _w_spec(k_taps, 0), _w_spec(k_taps, cv),
                  bspec(0), bspec(cv)],
        out_specs=_tile_spec(0), scratch_shapes=[pltpu.VMEM((h + TC, CB), F32), pltpu.VMEM((h + TC, CB), F32)],
        compiler_params=_cparams(("parallel", "parallel")))(up, up, up, up, w, w, b, b)


def ffn_conv_bwd(up, w, b, dact, name):
    L = up.shape[0]
    k_taps = w.shape[0]
    h = _halo(k_taps)
    nt = L // TC
    cv = D_FF // CB

    def body(gp_ref, g_ref, gn_ref, vp_ref, v_ref, vn_ref, d_ref, dn_ref, wg_ref, wv_ref, bg_ref, bv_ref,
             dg_ref, dv_ref, dwg_ref, dwv_ref, eg_ref, ev_ref, pg_ref, pv_ref):
        i = pl.program_id(1)

        @pl.when(i == 0)
        def _():
            dwg_ref[...] = jnp.zeros((8, CB), F32)
            dwv_ref[...] = jnp.zeros((8, CB), F32)

        for e_ref, p_ref, c_ref, n_ref in ((eg_ref, gp_ref, g_ref, gn_ref), (ev_ref, vp_ref, v_ref, vn_ref)):
            e_ref[0:h, :] = jnp.where(i > 0, p_ref[...], 0.0)
            e_ref[h:h + TC, :] = c_ref[...]
            e_ref[h + TC:h + TC + h, :] = n_ref[...]
        pg = _causal_taps(eg_ref, wg_ref, k_taps, h, TC + h) + bg_ref[...]
        pv = _causal_taps(ev_ref, wv_ref, k_taps, h, TC + h) + bv_ref[...]
        dact_t = d_ref[...]
        dact_n = jnp.where(i < nt - 1, dn_ref[...], 0.0)
        pg_ref[0:TC, :] = dact_t * pv[0:TC, :] * _dsilu(pg[0:TC, :])
        pg_ref[TC:TC + h, :] = dact_n * pv[TC:TC + h, :] * _dsilu(pg[TC:TC + h, :])
        pv_ref[0:TC, :] = dact_t * _silu(pg[0:TC, :])
        pv_ref[TC:TC + h, :] = dact_n * _silu(pg[TC:TC + h, :])
        dg_ref[...] = _anticausal_taps(pg_ref, wg_ref, k_taps, TC).astype(BF16)
        dv_ref[...] = _anticausal_taps(pv_ref, wv_ref, k_taps, TC).astype(BF16)
        _acc_conv_wgrad(dwg_ref, pg_ref[0:TC, :], eg_ref, k_taps, h)
        _acc_conv_wgrad(dwv_ref, pv_ref[0:TC, :], ev_ref, k_taps, h)

    bspec = lambda c0: pl.BlockSpec((1, CB), lambda j, i: (0, j + c0))
    ext = pltpu.VMEM((h + TC + h, CB), F32)
    dpre = pltpu.VMEM((TC + h, CB), F32)
    return pl.pallas_call(
        body, name=name, grid=(cv, nt),
        out_shape=[_sds((L, D_FF), BF16), _sds((L, D_FF), BF16), _sds((8, D_FF), F32), _sds((8, D_FF), F32)],
        in_specs=[_prev_spec(h, 0), _tile_spec(0), _next_spec(h, 0, nt), _prev_spec(h, cv), _tile_spec(cv), _next_spec(h, cv, nt),
                  _tile_spec(0), _next_spec(h, 0, nt), _w_spec(k_taps, 0), _w_spec(k_taps, cv), bspec(0), bspec(cv)],
        out_specs=[_tile_spec(0), _tile_spec(0), _w_spec(8, 0), _w_spec(8, 0)],
        scratch_shapes=[ext, ext, dpre, dpre],
        compiler_params=_cparams(("parallel", "arbitrary")))(up, up, up, up, up, up, dact, dact, w, w, b, b)


def _ssd_common(xbc_ref, dt_ref, dtb_ref, alog_ref, cs_ref):
    xs = xbc_ref[:, 0:D_SSD]
    sp_in = dt_ref[...] + dtb_ref[...]
    dtf = _softplus(sp_in)
    a_f = -jnp.exp(alog_ref[...])
    a_dt = dtf * a_f
    row = lax.broadcasted_iota(jnp.int32, (Q, Q), 0)
    col = lax.broadcasted_iota(jnp.int32, (Q, Q), 1)
    causal = row >= col
    cs = _dot_exact(causal.astype(F32), a_dt, 3, "a")
    cs_ref[...] = cs
    cs_last = cs_ref[Q - 1:Q, :]
    return xs, sp_in, dtf, a_f, cs, cs_last, causal


def _head_decay(cs_j, cst_ref, e, causal):
    lane = lax.broadcasted_iota(jnp.int32, (Q, LANE), 1)
    rolled = pltpu.roll(cs_j, HEAD, 1)
    own = (lane < HEAD) if e == 0 else (lane >= HEAD)
    col_b = jnp.where(own, cs_j, rolled)
    col_b = jnp.concatenate([col_b] * (Q // LANE), axis=1)
    row_b = cst_ref[e * HEAD:e * HEAD + 1, :]
    return jnp.where(causal, jnp.exp(jnp.minimum(col_b - row_b, 0.0)), 0.0)


def ssd_fwd(xbc, proj, dtb_f, alog_f, dsk_f, snw, name):
    L = xbc.shape[0]
    nc = L // Q

    def body(xbc_ref, z_ref, dt_ref, dtb_ref, alog_ref, dsk_ref, snw_ref, y_ref, yn_ref, sp_ref, s_ref, cs_ref, cst_ref, yd_ref):
        @pl.when(pl.program_id(0) == 0)
        def _():
            s_ref[...] = jnp.zeros((N_STATE, D_SSD), F32)

        xs, _, dtf, a_f, cs, cs_last, causal = _ssd_common(xbc_ref, dt_ref, dtb_ref, alog_ref, cs_ref)
        e_cs = jnp.exp(cs)
        xdt = xs * dtf
        zst = jnp.exp(cs_last - cs) * xdt
        sp_ref[0] = s_ref[...]
        lane = lax.broadcasted_iota(jnp.int32, (Q, LANE), 1)
        for g in range(2):
            gl = slice(g * 512, g * 512 + 512)
            b_g = xbc_ref[:, D_SSD + g * N_STATE:D_SSD + (g + 1) * N_STATE]
            c_g = xbc_ref[:, D_SSD + 2 * N_STATE + g * N_STATE:D_SSD + 2 * N_STATE + (g + 1) * N_STATE]
            s_prev = s_ref[:, gl]
            cb = _dot_nt(c_g, b_g)
            yd_ref[:, gl] = e_cs[:, gl] * _dot(c_g, s_prev)
            for j in range(4):
                tl = slice(g * 512 + j * LANE, g * 512 + (j + 1) * LANE)
                cs_j = cs[:, tl]
                cst_ref[...] = cs_j.T
                x_j = xdt[:, tl]
                o0 = _dot(cb * _head_decay(cs_j, cst_ref, 0, causal), x_j)
                o1 = _dot(cb * _head_decay(cs_j, cst_ref, 1, causal), x_j)
                yd_ref[:, tl] += jnp.where(lane < HEAD, o0, o1)
            s_ref[:, gl] = jnp.exp(cs_last[:, gl]) * s_prev + _dot_tn(b_g, zst[:, gl])
        y = yd_ref[...] + xs * dsk_ref[...]
        y_ref[...] = y
        yz = y * _silu(z_ref[...])
        r = lax.rsqrt(jnp.mean(yz * yz, axis=-1, keepdims=True) + 1e-6)
        yn_ref[...] = (yz * r * snw_ref[...]).astype(BF16)

    chunk = lambda w, c: pl.BlockSpec((Q, w), lambda i: (i, c))
    return pl.pallas_call(
        body, name=name, grid=(nc,),
        out_shape=[_sds((L, D_SSD), F32), _sds((L, D_SSD), BF16), _sds((nc, N_STATE, D_SSD), F32)],
        in_specs=[chunk(D_XBC, 0), chunk(D, 0), chunk(D, 1)] + [_const_spec((1, D))] * 4,
        out_specs=[chunk(D, 0), chunk(D, 0), pl.BlockSpec((1, N_STATE, D_SSD), lambda i: (i, 0, 0))],
        scratch_shapes=[pltpu.VMEM((N_STATE, D_SSD), F32), pltpu.VMEM((Q, D_SSD), F32), pltpu.VMEM((LANE, Q), F32),
                        pltpu.VMEM((Q, D_SSD), F32)],
        compiler_params=_cparams(("arbitrary",)))(xbc, proj, proj, dtb_f, alog_f, dsk_f, snw)


def ssd_bwd(dmixin, y, xbc, proj, s_prev_all, dtb_f, alog_f, dsk_f, snw, name):
    L = xbc.shape[0]
    nc = L // Q

    def body(dyn_ref, y_ref, xbc_ref, z_ref, dt_ref, sp_ref, dtb_ref, alog_ref, dsk_ref, snw_ref,
             dz_ref, ddt_ref, dxbc_ref, acc_ref, acc16_ref, ds_ref, cs_ref, cst_ref, dcs_ref, dx_ref):
        step = pl.program_id(0)

        @pl.when(step == 0)
        def _():
            ds_ref[...] = jnp.zeros((N_STATE, D_SSD), F32)
            acc_ref[...] = jnp.zeros((8, D), F32)

        z = z_ref[...]
        y = y_ref[...]
        sz = _sigmoid(z)
        siluz = z * sz
        yz = y * siluz
        r = lax.rsqrt(jnp.mean(yz * yz, axis=-1, keepdims=True) + 1e-6)
        n = yz * r
        dyn = dyn_ref[...]
        acc_ref[0:1, :] += _colsum(dyn * n)
        dn = dyn * snw_ref[...]
        dyz = r * (dn - n * jnp.mean(dn * n, axis=-1, keepdims=True))
        dy = dyz * siluz
        dz_ref[...] = (dyz * y * (sz * (1.0 + z * (1.0 - sz)))).astype(BF16)

        xs, sp_in, dtf, a_f, cs, cs_last, causal = _ssd_common(xbc_ref, dt_ref, dtb_ref, alog_ref, cs_ref)
        acc_ref[3:4, :] += _colsum(dy * xs)
        e_cs = jnp.exp(cs)
        xdt = xs * dtf
        dst = jnp.exp(cs_last - cs)
        zst = dst * xdt
        e_last = jnp.exp(cs_last)
        lane = lax.broadcasted_iota(jnp.int32, (Q, LANE), 1)
        ones = jnp.ones((Q, LANE), F32)
        dcs_last_parts = []
        for g in range(2):
            gl = slice(g * 512, g * 512 + 512)
            b_g = xbc_ref[:, D_SSD + g * N_STATE:D_SSD + (g + 1) * N_STATE]
            c_g = xbc_ref[:, D_SSD + 2 * N_STATE + g * N_STATE:D_SSD + 2 * N_STATE + (g + 1) * N_STATE]
            s_prev = sp_ref[0, :, gl]
            ds_g = ds_ref[:, gl]
            dy_g = dy[:, gl]
            cb = _dot_nt(c_g, b_g)
            y_off = e_cs[:, gl] * _dot(c_g, s_prev)
            edy = e_cs[:, gl] * dy_g
            d_c = _dot_nt(edy, s_prev)
            d_z = _dot(b_g, ds_g)
            d_b = _dot_nt(zst[:, gl], ds_g)
            t_g = d_z * zst[:, gl]
            dcs_ref[:, gl] = dy_g * y_off - t_g
            dx_ref[:, gl] = d_z * dst[:, gl]
            dcs_last_parts.append(_colsum(t_g) + _colsum(ds_g * s_prev) * e_last[:, gl])
            ds_ref[:, gl] = e_last[:, gl] * ds_g + _dot_tn(c_g, edy)
            dcb = jnp.zeros((Q, Q), F32)
            for j in range(4):
                tl = slice(g * 512 + j * LANE, g * 512 + (j + 1) * LANE)
                cs_j = cs[:, tl]
                cst_ref[...] = cs_j.T
                x_j = xdt[:, tl]
                dy_j = dy[:, tl]
                dx_j = jnp.zeros((Q, LANE), F32)
                dcs_j = jnp.zeros((Q, LANE), F32)
                for e in range(2):
                    own = (lane < HEAD) if e == 0 else (lane >= HEAD)
                    w_h = _head_decay(cs_j, cst_ref, e, causal)
                    g_h = cb * w_h
                    dy_m = jnp.where(own, dy_j, 0.0)
                    d_g = _dot_nt(dy_m, x_j)
                    dx_j = dx_j + _dot_tn(g_h, dy_m)
                    dcb = dcb + d_g * w_h
                    p_h = d_g * g_h
                    row_sums = _dot_exact(p_h, ones, 2, "b")
                    col_sums = _dot_exact(p_h, ones, 2, "b", (((0,), (0,)), ((), ())))
                    dcs_j = dcs_j + jnp.where(own, row_sums - col_sums, 0.0)
                dcs_ref[:, tl] += dcs_j * (1.0 / HEAD)
                dx_ref[:, tl] += dx_j
            d_c = d_c + _dot(dcb, b_g)
            d_b = d_b + _dot_tn(dcb, c_g)
            dxbc_ref[:, D_SSD + g * N_STATE:D_SSD + (g + 1) * N_STATE] = d_b
            dxbc_ref[:, D_SSD + 2 * N_STATE + g * N_STATE:D_SSD + 2 * N_STATE + (g + 1) * N_STATE] = d_c
        dcs_last = jnp.concatenate(dcs_last_parts, axis=1)
        anticausal = lax.broadcasted_iota(jnp.int32, (Q, Q), 0) <= lax.broadcasted_iota(jnp.int32, (Q, Q), 1)
        d_adt = _dot_exact(anticausal.astype(F32), dcs_ref[...], 3, "a") + dcs_last
        dx = dx_ref[...]
        acc_ref[2:3, :] += _colsum(d_adt * dtf) * a_f
        d_dtf = d_adt * a_f + dx * xs
        dxbc_ref[:, 0:D_SSD] = dx * dtf + dy * dsk_ref[...]
        d_raw = d_dtf * _sigmoid(sp_in)
        acc_ref[1:2, :] += _colsum(d_raw)
        head_of_lane = lax.broadcasted_iota(jnp.int32, (D_SSD, LANE), 0) // HEAD
        fold = (head_of_lane == lax.broadcasted_iota(jnp.int32, (D_SSD, LANE), 1)).astype(F32)
        ddt_ref[...] = _dot_exact(d_raw, fold, 2, "b").astype(BF16)

        @pl.when(step == nc - 1)
        def _():
            acc16_ref[...] = _dot_exact(acc_ref[...], fold, 3, "b")

    rchunk = lambda w, c: pl.BlockSpec((Q, w), lambda i: (nc - 1 - i, c))
    return pl.pallas_call(
        body, name=name, grid=(nc,),
        out_shape=[_sds((L, D_SSD), BF16), _sds((L, LANE), BF16), _sds((L, D_XBC), F32), _sds((8, D), F32), _sds((8, LANE), F32)],
        in_specs=[rchunk(D, 0), rchunk(D, 0), rchunk(D_XBC, 0), rchunk(D, 0), rchunk(D, 1),
                  pl.BlockSpec((1, N_STATE, D_SSD), lambda i: (nc - 1 - i, 0, 0))] + [_const_spec((1, D))] * 4,
        out_specs=[rchunk(D, 0), rchunk(LANE, 0), rchunk(D_XBC, 0), _const_spec((8, D)), _const_spec((8, LANE))],
        scratch_shapes=[pltpu.VMEM((N_STATE, D_SSD), F32), pltpu.VMEM((Q, D_SSD), F32), pltpu.VMEM((LANE, Q), F32),
                        pltpu.VMEM((Q, D_SSD), F32), pltpu.VMEM((Q, D_SSD), F32)],
        compiler_params=_cparams(("arbitrary",)))(dmixin, y, xbc, proj, proj, s_prev_all, dtb_f, alog_f, dsk_f, snw)


def _adamw_math(w, g, m, v):
    m_n = ADAM_B1 * m + (1.0 - ADAM_B1) * g
    v_n = ADAM_B2 * v + (1.0 - ADAM_B2) * jnp.square(g)
    c1 = 1.0 - ADAM_B1 ** ADAM_STEP
    c2 = 1.0 - ADAM_B2 ** ADAM_STEP
    return -ADAM_LR * ((m_n / c1) / (jnp.sqrt(v_n / c2) + ADAM_EPS) + ADAM_WD * w), m_n, v_n


def _sum_slots(p_ref):
    acc = p_ref[0].astype(F32)
    for s in range(1, N_DEV):
        acc = acc + p_ref[s].astype(F32)
    return acc


def adamw_slots(w, slots, m, v, name):
    rows, cols = w.shape
    tr = next(t for t in (256, 128, 64, 32, 16) if rows % t == 0)

    def body(w_ref, s_ref, m_ref, v_ref, g_ref, d_ref, mo_ref, vo_ref):
        g_v = _sum_slots(s_ref)
        g_ref[...] = g_v
        d_ref[...], mo_ref[...], vo_ref[...] = _adamw_math(w_ref[...], g_v, m_ref[...], v_ref[...])

    spec = pl.BlockSpec((tr, cols), lambda i: (i, 0))
    return pl.pallas_call(body, name=name, grid=(rows // tr,), out_shape=[_sds((rows, cols), F32)] * 4,
                          in_specs=[spec, pl.BlockSpec((N_DEV, tr, cols), lambda i: (0, i, 0)), spec, spec], out_specs=[spec] * 4,
                          compiler_params=_cparams(("parallel",)))(w, slots, m, v)


def adamw_many(ws, gs, ms, vs, name):
    n = len(ws)

    def body(*refs):
        for p in range(n):
            d_v, m_v, v_v = _adamw_math(refs[p][...], refs[n + p][...], refs[2 * n + p][...], refs[3 * n + p][...])
            refs[4 * n + p][...] = d_v
            refs[5 * n + p][...] = m_v
            refs[6 * n + p][...] = v_v

    vm = pl.BlockSpec(memory_space=pltpu.VMEM)
    out = pl.pallas_call(body, name=name, out_shape=[_sds(w.shape, F32) for w in ws] * 3, in_specs=[vm] * (4 * n),
                         out_specs=[vm] * (3 * n), compiler_params=_cparams())(*ws, *gs, *ms, *vs)
    return out[:n], out[n:2 * n], out[2 * n:]


def sum_slots_many(parts, name):
    n = len(parts)

    def body(*refs):
        for p in range(n):
            refs[n + p][...] = _sum_slots(refs[p])

    vm = pl.BlockSpec(memory_space=pltpu.VMEM)
    return pl.pallas_call(body, name=name, out_shape=[_sds(p.shape[1:], F32) for p in parts], in_specs=[vm] * n,
                          out_specs=[vm] * n, compiler_params=_cparams())(*parts)


def ada_mod(c_all, ada_w_shard, ada_b_cols, name):
    def body(c_ref, w_ref, b_ref, o_ref, ca_ref):
        ca = _silu(c_ref[...])
        ca_ref[...] = ca
        o_ref[...] = _dot(ca, w_ref[...]) + b_ref[...]

    vm = pl.BlockSpec(memory_space=pltpu.VMEM)
    return pl.pallas_call(body, name=name, out_shape=[_sds((N_DEV, ada_w_shard.shape[1]), F32), _sds((N_DEV, D), F32)],
                          in_specs=[vm, vm, vm], out_specs=[vm, vm], compiler_params=_cparams())(c_all, ada_w_shard, ada_b_cols)


def ada_wgrad(c_act_all, dmod_cols, name):
    def body(c_ref, d_ref, o_ref):
        o_ref[...] = _dot_tn_hi(c_ref[...], d_ref[...])

    vm = pl.BlockSpec(memory_space=pltpu.VMEM)
    return pl.pallas_call(body, name=name, out_shape=_sds((D, dmod_cols.shape[1]), F32), in_specs=[vm, vm], out_specs=vm,
                          compiler_params=_cparams())(c_act_all, dmod_cols)


def exchange(srcs, name, gather):
    n = len(srcs)
    shapes = [tuple(s.shape) if gather else tuple(s.shape[1:]) for s in srcs]

    def body(*refs):
        src_refs, out_refs = refs[:n], refs[n:2 * n]
        send_sems, recv_sems, local_sems = refs[2 * n:]
        x, y, c = lax.axis_index("x"), lax.axis_index("y"), lax.axis_index("c")
        me = 4 * x + 2 * y + c

        def peer(k):
            bx, by, bc = (k >> 2) & 1, (k >> 1) & 1, k & 1
            px, py, pc = (x + bx) % 2, (y + by) % 2, (c + bc) % 2
            return (px, py, pc), 4 * px + 2 * py + pc

        def copy(a, k, landing):
            dev, idx = peer(k)
            return pltpu.make_async_remote_copy(
                src_ref=src_refs[a] if gather else src_refs[a].at[idx], dst_ref=out_refs[a].at[idx if landing else me],
                send_sem=send_sems.at[a, k - 1], recv_sem=recv_sems.at[a, k - 1],
                device_id=dev, device_id_type=pl.DeviceIdType.MESH)

        mine = [pltpu.make_async_copy(src_refs[a] if gather else src_refs[a].at[me], out_refs[a].at[me], local_sems.at[a])
                for a in range(n)]
        for cp in mine:
            cp.start()
        sends = [copy(a, k, False) for a in range(n) for k in range(1, N_DEV)]
        for cp in sends:
            cp.start()
        for a in range(n):
            for k in range(1, N_DEV):
                copy(a, k, True).wait_recv()
        for cp in sends:
            cp.wait_send()
        for cp in mine:
            cp.wait()

    hbm = pl.BlockSpec(memory_space=pl.ANY)
    return pl.pallas_call(
        body, name=name, out_shape=[_sds((N_DEV,) + shp, s.dtype) for shp, s in zip(shapes, srcs)], in_specs=[hbm] * n,
        out_specs=[hbm] * n,
        scratch_shapes=[pltpu.SemaphoreType.DMA((n, N_DEV - 1)), pltpu.SemaphoreType.DMA((n, N_DEV - 1)),
                        pltpu.SemaphoreType.DMA((n,))],
        compiler_params=pltpu.CompilerParams(has_side_effects=True))(*srcs)


def gather_two_level(src, name):
    def body(src_ref, out_ref, send_sems, recv_sems, local_sem):
        x, y, c = lax.axis_index("x"), lax.axis_index("y"), lax.axis_index("c")
        me, sibling = (x, y, c), (x, y, 1 - c)
        chips = [(1 - x, y), (x, 1 - y), (1 - x, 1 - y)]

        def slot(px, py, pc):
            return out_ref.at[4 * px + 2 * py + pc]

        def copy(k, block, to, src=None):
            return pltpu.make_async_remote_copy(
                src_ref=slot(*block) if src is None else src, dst_ref=slot(*block), send_sem=send_sems.at[k],
                recv_sem=recv_sems.at[k], device_id=to, device_id_type=pl.DeviceIdType.MESH)

        mine = pltpu.make_async_copy(src_ref, slot(*me), local_sem)
        mine.start()
        first = [copy(0, me, sibling, src=src_ref)]
        first += [copy(1 + j, me, (*chip, c), src=src_ref) for j, chip in enumerate(chips)]
        for cp in first:
            cp.start()
        passed = [copy(4 + j, (*chip, c), sibling) for j, chip in enumerate(chips)]
        for j, chip in enumerate(chips):
            copy(1 + j, (*chip, c), me).wait_recv()
            passed[j].start()
        copy(0, sibling, me).wait_recv()
        for j, chip in enumerate(chips):
            copy(4 + j, (*chip, 1 - c), me).wait_recv()
        for cp in first + passed:
            cp.wait_send()
        mine.wait()

    hbm = pl.BlockSpec(memory_space=pl.ANY)
    return pl.pallas_call(
        body, name=name, out_shape=_sds((N_DEV,) + tuple(src.shape), src.dtype), in_specs=[hbm], out_specs=hbm,
        scratch_shapes=[pltpu.SemaphoreType.DMA((N_DEV - 1,)), pltpu.SemaphoreType.DMA((N_DEV - 1,)), pltpu.SemaphoreType.DMA],
        compiler_params=pltpu.CompilerParams(has_side_effects=True))(src)


def _peer(k):
    x, y, c = lax.axis_index("x"), lax.axis_index("y"), lax.axis_index("c")
    px, py, pc = (x + ((k >> 2) & 1)) % 2, (y + ((k >> 1) & 1)) % 2, (c + (k & 1)) % 2
    return (px, py, pc), 4 * px + 2 * py + pc


def _my_slot():
    return 4 * lax.axis_index("x") + 2 * lax.axis_index("y") + lax.axis_index("c")


_HBM = pl.BlockSpec(memory_space=pltpu.HBM)
_SEM = pl.BlockSpec(memory_space=pltpu.SEMAPHORE)
_EFFECT = pltpu.SideEffectType.DATAFLOW_SIDE_EFFECTING


def exchange_start(srcs, name, gather):
    n = len(srcs)
    shapes = [tuple(s.shape) if gather else tuple(s.shape[1:]) for s in srcs]
    lands = [lax.empty((N_DEV,) + shp, s.dtype) for shp, s in zip(shapes, srcs)]

    def body(*refs):
        src_refs, land_refs = refs[:n], refs[n:2 * n]
        sems = refs[2 * n:4 * n]
        token = refs[-1]
        me = _my_slot()
        for a in range(n):
            for k in range(1, N_DEV):
                dev, idx = _peer(k)
                pltpu.make_async_remote_copy(
                    src_ref=src_refs[a] if gather else src_refs[a].at[idx], dst_ref=land_refs[a].at[me],
                    send_sem=sems[2 * a].at[k - 1], recv_sem=sems[2 * a + 1].at[k - 1],
                    device_id=dev, device_id_type=pl.DeviceIdType.MESH).start()
        token[...] = jnp.zeros_like(token)

    out_shape = ([pltpu.SemaphoreType.DMA((N_DEV - 1,))] * (2 * n) + [pltpu.HBM(s.shape, s.dtype) for s in srcs]
                 + [pltpu.HBM(l.shape, l.dtype) for l in lands] + [_sds((8, LANE), F32)])
    out = pl.pallas_call(
        body, name=name, out_shape=out_shape, in_specs=[_HBM] * (2 * n),
        out_specs=[_SEM] * (2 * n) + [_HBM] * (2 * n) + [pl.BlockSpec(memory_space=pltpu.VMEM)],
        input_output_aliases={i: 2 * n + i for i in range(2 * n)},
        compiler_params=pltpu.CompilerParams(has_side_effects=_EFFECT))(
            *[pltpu.with_memory_space_constraint(s, pltpu.HBM) for s in srcs],
            *[pltpu.with_memory_space_constraint(l, pltpu.HBM) for l in lands])
    parts = [(out[2 * a], out[2 * a + 1], out[2 * n + a], out[3 * n + a]) for a in range(n)]
    return parts, out[-1]


def exchange_wait(parts, after, name, gather):
    n = len(parts)

    def body(*refs):
        src_refs, land_refs = refs[:n], refs[n:2 * n]
        sems = refs[2 * n:4 * n]
        for a in range(n):
            for k in range(1, N_DEV):
                dev, idx = _peer(k)
                copy = pltpu.make_async_remote_copy(
                    src_ref=src_refs[a] if gather else src_refs[a].at[idx], dst_ref=land_refs[a].at[idx],
                    send_sem=sems[2 * a].at[k - 1], recv_sem=sems[2 * a + 1].at[k - 1],
                    device_id=dev, device_id_type=pl.DeviceIdType.MESH)
                copy.wait_send()
                copy.wait_recv()

    srcs = [p[2] for p in parts]
    lands = [p[3] for p in parts]
    sems = [s for p in parts for s in p[:2]]
    out = pl.pallas_call(
        body, name=name, out_shape=[pltpu.HBM(a.shape, a.dtype) for a in srcs + lands],
        in_specs=[_HBM] * (2 * n) + [_SEM] * (2 * n) + [pl.BlockSpec(memory_space=pl.ANY)], out_specs=[_HBM] * (2 * n),
        input_output_aliases={i: i for i in range(2 * n)},
        compiler_params=pltpu.CompilerParams(has_side_effects=_EFFECT))(*srcs, *lands, *sems, after)
    return out[n:]


def _cols_to_slabs(g):
    r, c = g.shape
    return g.reshape(r, N_DEV, c // N_DEV).transpose(1, 0, 2)


def _slabs_to_cols(s):
    _, r, cs = s.shape
    return s.transpose(1, 0, 2).reshape(r, N_DEV * cs)


def _rep_heads(v):
    return jnp.repeat(v.reshape(N_HEADS), HEAD).reshape(1, D_SSD)


def local_fwd_bwd(x, target, mod, get_w, put_grad, small):
    n1w, n2w, fnw = small["norm1_w"], small["norm2_w"], small["final_norm_w"]
    dtb_f, alog_f, dsk_f = _rep_heads(small["dt_bias"]), _rep_heads(small["a_log"]), _rep_heads(small["d_skip"])
    snw = small["ssd_norm_w"]

    def after(v, token):
        return v + token[0:1, 0:1]

    h1 = norm_mod(x, mod, n1w, 0, "norm1")
    w_in = get_w("w_in", h1)
    proj = mm_nn([(h1, w_in["w_in_p"], 0)], "in_proj")
    xbc = conv_silu_fwd(proj, 4096 // CB, D_XBC, small["ssd_conv_w"], small["ssd_conv_b"], "ssd_conv")
    y, ysn, s_prev = ssd_fwd(xbc, proj, dtb_f, alog_f, dsk_f, snw, "ssd_scan")
    uc = conf_conv_fwd(proj, 2048 // CB, 3072 // CB, small["conf_conv_w"], small["conf_conv_b"], "conf_conv")
    u = ln_silu(uc, small["conf_ln_w"], small["conf_ln_b"], "conf_ln")
    w_out = get_w("w_out", u)
    mix = mm_nn([(ysn, w_out, 0), (u, w_out, 1)], "out_proj")
    h2, x1 = norm_mod(x, mod, n2w, 3, "norm2", res=mix, gate_row=2)
    w_up = get_w("w_up", h2)
    up = mm_nn([(h2, w_up, 0)], "up_proj")
    act = ffn_conv_fwd(up, small["ffn_conv_w"], small["ffn_conv_b"], "ffn_conv")
    w_down = get_w("w_down", act)
    ff = mm_nn([(act, w_down, 0)], "down_proj")
    dx2, dff, acc_f = final_loss(ff, x1, mod, fnw, target, "final_loss")

    token = put_grad("w_down", mm_tn(act, dff, "wgrad_down"))
    dact = mm_nt([(dff, w_down, 0)], "dact")
    dupg, dupv, dwg, dwv = ffn_conv_bwd(up, small["ffn_conv_w"], after(small["ffn_conv_b"], token), dact, "ffn_conv_bwd")
    token = put_grad("w_up", jnp.concatenate([mm_tn(h2, dupg, "wgrad_up_gate"), mm_tn(h2, dupv, "wgrad_up_val")], axis=1))
    dh2 = mm_nt([(dupg, w_up, 0), (dupv, w_up, 1)], "dh2")
    dx1, dmix, acc_2 = norm_mod_bwd(dh2, x1, dx2, mod, after(n2w, token), 3, "norm2_bwd", mix=mix, gate_row=2)

    token = put_grad("w_out", jnp.concatenate([mm_tn(ysn, dmix, "wgrad_out_ssd"), mm_tn(u, dmix, "wgrad_out_conf")], axis=0))
    dmixin = mm_nt([(dmix, w_out, 0)], "dmixin")
    duc, acc_ln = ln_silu_bwd(dmixin, uc, after(small["conf_ln_w"], token), small["conf_ln_b"], "conf_ln_bwd")
    dcfa, dcfg, dw_cc = conf_conv_bwd(proj, 2048 // CB, 3072 // CB, small["conf_conv_w"], duc, "conf_conv_bwd")
    dz, ddt, dxbc_post, acc_s, acc_s16 = ssd_bwd(dmixin, y, xbc, proj, s_prev, dtb_f, alog_f, dsk_f, snw, "ssd_scan_bwd")
    dxbc, dw_sc = conv_silu_bwd(proj, 4096 // CB, D_XBC, small["ssd_conv_w"], small["ssd_conv_b"], dxbc_post, "ssd_conv_bwd")
    token = put_grad("w_in", jnp.concatenate(
        [mm_tn(h1, dz, "wgrad_in_z"), mm_tn(h1, dxbc, "wgrad_in_xbc"), mm_tn(h1, ddt, "wgrad_in_dt")[:, :N_HEADS],
         mm_tn(h1, dcfa, "wgrad_in_cfa"), mm_tn(h1, dcfg, "wgrad_in_cfg")], axis=1))
    dh1 = mm_nt([(dz, w_in["w_z"], 0), (ddt, w_in["w_dt16"], 0), (dcfa, w_in["w_cfa"], 0), (dcfg, w_in["w_cfg"], 0),
                 (dxbc, w_in["w_xbc"], 0)], "dh1")
    grad_x, acc_1 = norm_mod_bwd(dh1, x, dx1, mod, after(n1w, token), 0, "norm1_bwd")

    small_accs = dict(acc_1=acc_1, acc_2=acc_2, acc_f=acc_f, acc_ln=acc_ln, acc_s=acc_s, acc_s16=acc_s16, dw_sc=dw_sc,
                      dw_cc=dw_cc, dwg=dwg, dwv=dwv)
    return grad_x, small_accs


def kernel(x, c, ada_w, ada_b, norm1_w, w_in, ssd_conv_w, ssd_conv_b, dt_bias, a_log, d_skip, ssd_norm_w, conf_conv_w, conf_conv_b, conf_ln_w, conf_ln_b, w_out, norm2_w, w_up, ffn_conv_w, ffn_conv_b, w_down, final_norm_w, loss_target, m_ada_w, m_ada_b, m_norm1_w, m_w_in, m_ssd_conv_w, m_ssd_conv_b, m_dt_bias, m_a_log, m_d_skip, m_ssd_norm_w, m_conf_conv_w, m_conf_conv_b, m_conf_ln_w, m_conf_ln_b, m_w_out, m_norm2_w, m_w_up, m_ffn_conv_w, m_ffn_conv_b, m_w_down, m_final_norm_w, v_ada_w, v_ada_b, v_norm1_w, v_w_in, v_ssd_conv_w, v_ssd_conv_b, v_dt_bias, v_a_log, v_d_skip, v_ssd_norm_w, v_conf_conv_w, v_conf_conv_b, v_conf_ln_w, v_conf_ln_b, v_w_out, v_norm2_w, v_w_up, v_ffn_conv_w, v_ffn_conv_b, v_w_down, v_final_norm_w):
    me = 4 * lax.axis_index("x") + 2 * lax.axis_index("y") + lax.axis_index("c")
    weights = dict(ada_w=ada_w, ada_b=ada_b, norm1_w=norm1_w, w_in=w_in, ssd_conv_w=ssd_conv_w, ssd_conv_b=ssd_conv_b,
                   dt_bias=dt_bias, a_log=a_log, d_skip=d_skip, ssd_norm_w=ssd_norm_w, conf_conv_w=conf_conv_w,
                   conf_conv_b=conf_conv_b, conf_ln_w=conf_ln_w, conf_ln_b=conf_ln_b, w_out=w_out, norm2_w=norm2_w, w_up=w_up,
                   ffn_conv_w=ffn_conv_w, ffn_conv_b=ffn_conv_b, w_down=w_down, final_norm_w=final_norm_w)
    moms_m = dict(ada_w=m_ada_w, ada_b=m_ada_b, norm1_w=m_norm1_w, w_in=m_w_in, ssd_conv_w=m_ssd_conv_w, ssd_conv_b=m_ssd_conv_b,
                  dt_bias=m_dt_bias, a_log=m_a_log, d_skip=m_d_skip, ssd_norm_w=m_ssd_norm_w, conf_conv_w=m_conf_conv_w,
                  conf_conv_b=m_conf_conv_b, conf_ln_w=m_conf_ln_w, conf_ln_b=m_conf_ln_b, w_out=m_w_out, norm2_w=m_norm2_w,
                  w_up=m_w_up, ffn_conv_w=m_ffn_conv_w, ffn_conv_b=m_ffn_conv_b, w_down=m_w_down, final_norm_w=m_final_norm_w)
    moms_v = dict(ada_w=v_ada_w, ada_b=v_ada_b, norm1_w=v_norm1_w, w_in=v_w_in, ssd_conv_w=v_ssd_conv_w, ssd_conv_b=v_ssd_conv_b,
                  dt_bias=v_dt_bias, a_log=v_a_log, d_skip=v_d_skip, ssd_norm_w=v_ssd_norm_w, conf_conv_w=v_conf_conv_w,
                  conf_conv_b=v_conf_conv_b, conf_ln_w=v_conf_ln_w, conf_ln_b=v_conf_ln_b, w_out=v_w_out, norm2_w=v_norm2_w,
                  w_up=v_w_up, ffn_conv_w=v_ffn_conv_w, ffn_conv_b=v_ffn_conv_b, w_down=v_w_down, final_norm_w=v_final_norm_w)
    names = list(weights)

    def to2d(a):
        return a[0] if a.ndim == 3 else a.reshape(1, -1)

    big = ("w_in", "w_out", "w_up", "w_down")

    c_all, scw_all, ccw_all, fcw_all = exchange([c.reshape(8, LANE), ssd_conv_w[0], conf_conv_w[0], ffn_conv_w[0]],
                                                "gather_small", gather=True)
    c_all = c_all.reshape(N_DEV, D)

    ada_cols = ada_w.shape[2]
    ada_b_cols = lax.dynamic_slice(ada_b, (0, me * ada_cols), (1, ada_cols))
    mod_cols, c_act_all = ada_mod(c_all, ada_w[0], ada_b_cols, "ada_mod")
    mod_parts, = exchange([jnp.pad(mod_cols, ((0, 0), (0, D - ada_cols))).reshape(N_DEV, 8, LANE)], "scatter_mod", gather=False)
    mod = mod_parts.reshape(N_DEV, D)[:, :ada_cols].reshape(6, D)
    mod = jnp.pad(mod, ((0, 2), (0, 0)))

    shards, mod = lax.optimization_barrier(([weights[n][0].astype(BF16) for n in big], mod))
    w_in_slabs = gather_two_level(shards[0], "gather_w_in")
    later, w_in_slabs = lax.optimization_barrier((shards[1:], w_in_slabs))
    gather_parts, token = exchange_start(later, "gather_weights_start", gather=True)
    mod = mod + token[0:1, 0:1]

    small = {n: to2d(weights[n]) for n in names if n not in ("ada_w",) + big}
    small["ssd_conv_w"] = _slabs_to_cols(scw_all)
    small["conf_conv_w"] = _slabs_to_cols(ccw_all)
    small["ffn_conv_w"] = _slabs_to_cols(fcw_all)

    def with_own(landed, own):
        return lax.dynamic_update_slice(landed, own[None], (me,) + (0,) * own.ndim)

    def get_w(n, after):
        if n == "w_in":
            slabs = w_in_slabs
        else:
            a = big.index(n)
            landed, = exchange_wait([gather_parts[a - 1]], after, "gather_" + n + "_wait", gather=True)
            slabs = with_own(landed, shards[a])
        if n == "w_out":
            return slabs.reshape(2 * D, D)
        if n == "w_down":
            return slabs.reshape(D_FF, D)
        full = _slabs_to_cols(slabs)
        if n == "w_up":
            return full
        w_z, w_xbc, w_dt, w_cfa, w_cfg = full[:, :1024], full[:, 1024:2560], full[:, 2560:2576], full[:, 2576:3600], full[:, 3600:]
        return dict(w_in_p=jnp.concatenate([w_z, jnp.repeat(w_dt, HEAD, axis=1), w_cfa, w_cfg, w_xbc], axis=1), w_z=w_z,
                    w_xbc=w_xbc, w_dt16=jnp.pad(w_dt, ((0, 0), (0, LANE - N_HEADS))), w_cfa=w_cfa, w_cfg=w_cfg)

    scatter_parts, sent = {}, {}

    def put_grad(n, g):
        slabs = _cols_to_slabs(g) if n in ("w_in", "w_up") else g.reshape(N_DEV, g.shape[0] // N_DEV, g.shape[1])
        sent[n] = slabs.astype(BF16)
        (scatter_parts[n],), token = exchange_start([sent[n]], "scatter_" + n + "_start", gather=False)
        return token

    grad_x, accs = local_fwd_bwd(x[0], loss_target[0], mod, get_w, put_grad, small)
    loss = lax.psum(0.5 / D * jnp.sum(accs["acc_f"][2:3]), ("x", "y", "c"))

    landed = exchange_wait([scatter_parts[n] for n in big], grad_x, "scatter_grads_wait", gather=False)
    grads, delta, new_m, new_v = {}, {}, {}, {}
    for n, slots in zip(big, landed):
        slots = with_own(slots, lax.dynamic_index_in_dim(sent[n], me, 0, keepdims=False))
        grads[n], delta[n], new_m[n], new_v[n] = adamw_slots(weights[n][0], slots, moms_m[n][0], moms_v[n][0], "adamw_" + n)

    order = ("acc_1", "acc_2", "acc_f", "acc_ln", "acc_s", "acc_s16", "dw_sc", "dw_cc", "dwg", "dwv")
    gathered = dict(zip(order, exchange([accs[k] for k in order], "gather_small_grads", gather=True)))
    red = dict(zip(order, sum_slots_many([gathered[k] for k in order], "sum_small_grads")))

    def mod_rows(a1, a2, af):
        return jnp.concatenate([a1[..., 0:2, :], a2[..., 3:4, :], a2[..., 0:2, :], af[..., 1:2, :]], axis=-2)

    dmod_all = mod_rows(gathered["acc_1"], gathered["acc_2"], gathered["acc_f"]).reshape(N_DEV, 6 * D)
    grads["ada_w"] = ada_wgrad(c_act_all, lax.dynamic_slice(dmod_all, (0, me * ada_cols), (N_DEV, ada_cols)), "ada_wgrad")

    def my_cols(full, k_taps):
        cols = full.shape[1] // N_DEV
        return lax.dynamic_slice(full, (0, me * cols), (k_taps, cols))

    fcw = jnp.concatenate([red["dwg"], red["dwv"]], axis=1)
    grads.update(
        ada_b=mod_rows(red["acc_1"], red["acc_2"], red["acc_f"]).reshape(1, 6 * D), norm1_w=red["acc_1"][2:3],
        ssd_conv_w=my_cols(red["dw_sc"], K_SSD), ssd_conv_b=red["dw_sc"][K_SSD:K_SSD + 1],
        dt_bias=red["acc_s16"][1:2, :N_HEADS], a_log=red["acc_s16"][2:3, :N_HEADS], d_skip=red["acc_s16"][3:4, :N_HEADS],
        ssd_norm_w=red["acc_s"][0:1], conf_conv_w=my_cols(red["dw_cc"], K_CONF), conf_conv_b=red["dw_cc"][K_CONF:K_CONF + 1],
        conf_ln_w=red["acc_ln"][0:1], conf_ln_b=red["acc_ln"][1:2], norm2_w=red["acc_2"][2:3],
        ffn_conv_w=my_cols(fcw, K_FFN), ffn_conv_b=fcw[K_FFN:K_FFN + 1], final_norm_w=red["acc_f"][0:1])

    rest = [n for n in names if n not in big]
    d_l, m_l, v_l = adamw_many([to2d(weights[n]) for n in rest], [grads[n] for n in rest], [to2d(moms_m[n]) for n in rest],
                               [to2d(moms_v[n]) for n in rest], "adamw_small")
    for n, dd, mm, vv in zip(rest, d_l, m_l, v_l):
        delta[n], new_m[n], new_v[n] = dd, mm, vv
    shape_of = lambda d_: {n: d_[n].reshape(weights[n].shape) for n in names}
    grads, delta, new_m, new_v = shape_of(grads), shape_of(delta), shape_of(new_m), shape_of(new_v)
    return (loss, grad_x[None], *[grads[n] for n in names], *[delta[n] for n in names], *[new_m[n] for n in names],
            *[new_v[n] for n in names])
```

```python
import functools

import jax
import jax.numpy as jnp
from jax import lax
from jax.experimental import pallas as pl
from jax.experimental.pallas import tpu as pltpu

F32 = jnp.float32
BF16 = jnp.bfloat16
HI = lax.Precision.HIGHEST

N_DEV = 8
D = 1024
D_SSD = 1024
HEAD = 64
N_HEADS = 16
N_STATE = 128
D_XBC = 1536
D_CONF = 1024
D_FF = 2816
K_SSD, K_CONF, K_FFN = 4, 31, 3
D_INP = 5632
LANE = 128
TR = 256
TM = 512
Q = 256
CB = 256
TC = 1024
VMEM_LIMIT = 56 * 1024 * 1024

ADAM_LR, ADAM_B1, ADAM_B2, ADAM_EPS, ADAM_WD, ADAM_STEP = 0.001, 0.9, 0.999, 1e-08, 0.01, 10


def _cparams(sem=None):
    return pltpu.CompilerParams(vmem_limit_bytes=VMEM_LIMIT, dimension_semantics=sem)


def _sds(shape, dtype):
    return jax.ShapeDtypeStruct(shape, dtype)


def _sigmoid(x):
    return 1.0 / (1.0 + jnp.exp(-x))


def _silu(x):
    return x * _sigmoid(x)


def _dsilu(x):
    s = _sigmoid(x)
    return s * (1.0 + x * (1.0 - s))


def _softplus(x):
    return jnp.maximum(x, 0.0) + jnp.log(1.0 + jnp.exp(-jnp.abs(x)))


def _dot(a, b):
    return jnp.dot(a.astype(BF16), b.astype(BF16), preferred_element_type=F32)


def _dot_nt(a, b):
    return lax.dot_general(a.astype(BF16), b.astype(BF16), (((1,), (1,)), ((), ())), preferred_element_type=F32)


def _dot_tn(a, b):
    return lax.dot_general(a.astype(BF16), b.astype(BF16), (((0,), (0,)), ((), ())), preferred_element_type=F32)


def _dot_hi(a, b):
    return jnp.dot(a, b, precision=HI, preferred_element_type=F32)


def _bf16_terms(a, terms):
    parts, rem = [], a
    for t in range(terms):
        p = rem.astype(BF16)
        parts.append(p)
        if t + 1 < terms:
            rem = rem - p.astype(F32)
    return parts


def _dot_exact(a, b, terms, exact, dims=(((1,), (0,)), ((), ()))):
    if exact == "a":
        a_b = a.astype(BF16)
        outs = [lax.dot_general(a_b, p, dims, preferred_element_type=F32) for p in _bf16_terms(b, terms)]
    else:
        b_b = b.astype(BF16)
        outs = [lax.dot_general(p, b_b, dims, preferred_element_type=F32) for p in _bf16_terms(a, terms)]
    acc = outs[-1]
    for o in reversed(outs[:-1]):
        acc = acc + o
    return acc


def _dot_tn_hi(a, b):
    return lax.dot_general(a, b, (((0,), (0,)), ((), ())), precision=HI, preferred_element_type=F32)


def _colsum(x):
    return jnp.sum(x, axis=0, keepdims=True)


def _const_spec(shape):
    return pl.BlockSpec(shape, lambda *_: (0,) * len(shape))


def _col_tile(n):
    for t in (1408, 1024, 768, 512, 256, 128):
        if n % t == 0 and t <= n:
            return t
    return n


def mm_nn(pairs, name):
    L = pairs[0][0].shape[0]
    N = pairs[0][1].shape[1]
    tn = _col_tile(N)
    n = len(pairs)

    def body(*refs):
        acc = None
        for p in range(n):
            t = jnp.dot(refs[2 * p][...], refs[2 * p + 1][...], preferred_element_type=F32)
            acc = t if acc is None else acc + t
        refs[-1][...] = acc

    in_specs, args = [], []
    for a, w, rb in pairs:
        in_specs += [pl.BlockSpec((TM, a.shape[1]), lambda j, i: (i, 0)),
                     pl.BlockSpec((a.shape[1], tn), functools.partial(lambda j, i, rb: (rb, j), rb=rb))]
        args += [a, w]
    return pl.pallas_call(
        body, name=name, grid=(N // tn, L // TM), out_shape=_sds((L, N), F32), in_specs=in_specs,
        out_specs=pl.BlockSpec((TM, tn), lambda j, i: (i, j)),
        compiler_params=_cparams(("parallel", "parallel")))(*args)


def mm_nt(pairs, name):
    L = pairs[0][0].shape[0]
    K = pairs[0][1].shape[0]
    tk = _col_tile(K)
    n = len(pairs)

    def body(*refs):
        o_ref = refs[-1]
        acc = None
        for p in range(n):
            t = lax.dot_general(refs[2 * p][...], refs[2 * p + 1][...], (((1,), (1,)), ((), ())),
                                preferred_element_type=F32)
            acc = t if acc is None else acc + t
        o_ref[...] = acc

    in_specs, args = [], []
    for a, w, cb in pairs:
        in_specs += [pl.BlockSpec((TM, a.shape[1]), lambda j, i: (i, 0)),
                     pl.BlockSpec((tk, a.shape[1]), functools.partial(lambda j, i, cb: (j, cb), cb=cb))]
        args += [a, w]
    return pl.pallas_call(
        body, name=name, grid=(K // tk, L // TM), out_shape=_sds((L, K), F32), in_specs=in_specs,
        out_specs=pl.BlockSpec((TM, tk), lambda j, i: (i, j)),
        compiler_params=_cparams(("parallel", "parallel")))(*args)


def mm_tn(a, g, name):
    L, M = a.shape
    N = g.shape[1]
    tn = _col_tile(N) if N > 1024 else N
    if M * tn * 4 > 8 * 1024 * 1024:
        tn = 512
    tl = 512 if L % 512 == 0 else TR

    def body(a_ref, g_ref, o_ref):
        @pl.when(pl.program_id(1) == 0)
        def _():
            o_ref[...] = jnp.zeros((M, tn), F32)

        o_ref[...] += lax.dot_general(a_ref[...], g_ref[...], (((0,), (0,)), ((), ())), preferred_element_type=F32)

    return pl.pallas_call(
        body, name=name, grid=(N // tn, L // tl), out_shape=_sds((M, N), F32),
        in_specs=[pl.BlockSpec((tl, M), lambda j, l: (l, 0)), pl.BlockSpec((tl, tn), lambda j, l: (l, j))],
        out_specs=pl.BlockSpec((M, tn), lambda j, l: (0, j)),
        compiler_params=_cparams(("parallel", "arbitrary")))(a, g)


def _row_spec(width=D):
    return pl.BlockSpec((TR, width), lambda i: (i, 0))


def _row_col_spec(width, col):
    return pl.BlockSpec((TR, width), lambda i: (i, col))


def norm_mod(x, mod, w, shift_row, name, res=None, gate_row=None):
    L = x.shape[0]
    has_res = res is not None

    def body(*refs):
        if has_res:
            x_ref, res_ref, mod_ref, w_ref, h_ref, xo_ref = refs
            xin = x_ref[...] + mod_ref[gate_row:gate_row + 1, :] * res_ref[...]
            xo_ref[...] = xin
        else:
            x_ref, mod_ref, w_ref, h_ref = refs
            xin = x_ref[...]
        r = lax.rsqrt(jnp.mean(xin * xin, axis=-1, keepdims=True) + 1e-6)
        h = (xin * r * w_ref[...]) * (1.0 + mod_ref[shift_row + 1:shift_row + 2, :]) + mod_ref[shift_row:shift_row + 1, :]
        h_ref[...] = h.astype(BF16)

    ins = [x] + ([res] if has_res else []) + [mod, w]
    in_specs = [_row_spec()] + ([_row_spec()] if has_res else []) + [_const_spec((8, D)), _const_spec((1, D))]
    out_shape = [_sds((L, D), BF16)] + ([_sds((L, D), F32)] if has_res else [])
    out_specs = [_row_spec()] + ([_row_spec()] if has_res else [])
    out = pl.pallas_call(body, name=name, grid=(L // TR,), out_shape=out_shape, in_specs=in_specs,
                         out_specs=out_specs, compiler_params=_cparams(("parallel",)))(*ins)
    return out if has_res else out[0]


def ln_silu(uc, lnw, lnb, name):
    L = uc.shape[0]

    def body(u_ref, w_ref, b_ref, o_ref):
        u = u_ref[...]
        mu = jnp.mean(u, axis=-1, keepdims=True)
        var = jnp.mean(jnp.square(u - mu), axis=-1, keepdims=True)
        v = (u - mu) * lax.rsqrt(var + 1e-5) * w_ref[...] + b_ref[...]
        o_ref[...] = _silu(v).astype(BF16)

    return pl.pallas_call(body, name=name, grid=(L // TR,), out_shape=_sds((L, D_CONF), BF16),
                          in_specs=[_row_spec(), _const_spec((1, D)), _const_spec((1, D))], out_specs=_row_spec(),
                          compiler_params=_cparams(("parallel",)))(uc, lnw, lnb)


def ln_silu_bwd(du_all, uc, lnw, lnb, name):
    L = uc.shape[0]

    def body(du_ref, u_ref, w_ref, b_ref, o_ref, acc_ref):
        @pl.when(pl.program_id(0) == 0)
        def _():
            acc_ref[...] = jnp.zeros((8, D), F32)

        u = u_ref[...]
        mu = jnp.mean(u, axis=-1, keepdims=True)
        rl = lax.rsqrt(jnp.mean(jnp.square(u - mu), axis=-1, keepdims=True) + 1e-5)
        n = (u - mu) * rl
        v = n * w_ref[...] + b_ref[...]
        dv = du_ref[...] * _dsilu(v)
        acc_ref[0:1, :] += _colsum(dv * n)
        acc_ref[1:2, :] += _colsum(dv)
        dn = dv * w_ref[...]
        o_ref[...] = rl * (dn - jnp.mean(dn, axis=-1, keepdims=True) - n * jnp.mean(dn * n, axis=-1, keepdims=True))

    return pl.pallas_call(body, name=name, grid=(L // TR,), out_shape=[_sds((L, D), F32), _sds((8, D), F32)],
                          in_specs=[_row_col_spec(D, 1), _row_spec(), _const_spec((1, D)), _const_spec((1, D))],
                          out_specs=[_row_spec(), _const_spec((8, D))],
                          compiler_params=_cparams(("arbitrary",)))(du_all, uc, lnw, lnb)


def final_loss(ff, x1, mod, fw, target, name):
    L = ff.shape[0]

    def body(ff_ref, x1_ref, mod_ref, fw_ref, t_ref, dx_ref, dff_ref, acc_ref):
        @pl.when(pl.program_id(0) == 0)
        def _():
            acc_ref[...] = jnp.zeros((8, D), F32)

        ff_v = ff_ref[...]
        g2 = mod_ref[5:6, :]
        x2 = x1_ref[...] + g2 * ff_v
        r = lax.rsqrt(jnp.mean(x2 * x2, axis=-1, keepdims=True) + 1e-6)
        n = x2 * r
        err = n * fw_ref[...] - t_ref[...]
        dy = err * (1.0 / D)
        dn = dy * fw_ref[...]
        dx2 = r * (dn - n * jnp.mean(dn * n, axis=-1, keepdims=True))
        acc_ref[0:1, :] += _colsum(dy * n)
        acc_ref[1:2, :] += _colsum(dx2 * ff_v)
        acc_ref[2:3, :] += _colsum(err * err)
        dx_ref[...] = dx2
        dff_ref[...] = (dx2 * g2).astype(BF16)

    return pl.pallas_call(
        body, name=name, grid=(L // TR,), out_shape=[_sds((L, D), F32), _sds((L, D), BF16), _sds((8, D), F32)],
        in_specs=[_row_spec(), _row_spec(), _const_spec((8, D)), _const_spec((1, D)), _row_spec()],
        out_specs=[_row_spec(), _row_spec(), _const_spec((8, D))],
        compiler_params=_cparams(("arbitrary",)))(ff, x1, mod, fw, target)


def norm_mod_bwd(dh, xin, dres, mod, w, shift_row, name, mix=None, gate_row=None):
    L = dh.shape[0]
    has_mix = mix is not None

    def body(*refs):
        if has_mix:
            dh_ref, x_ref, dres_ref, mod_ref, w_ref, mix_ref, dx_ref, dmix_ref, acc_ref = refs
        else:
            dh_ref, x_ref, dres_ref, mod_ref, w_ref, dx_ref, acc_ref = refs

        @pl.when(pl.program_id(0) == 0)
        def _():
            acc_ref[...] = jnp.zeros((8, D), F32)

        dh_v = dh_ref[...]
        x = x_ref[...]
        r = lax.rsqrt(jnp.mean(x * x, axis=-1, keepdims=True) + 1e-6)
        n = x * r
        nw = n * w_ref[...]
        sc1 = 1.0 + mod_ref[shift_row + 1:shift_row + 2, :]
        acc_ref[0:1, :] += _colsum(dh_v)
        acc_ref[1:2, :] += _colsum(dh_v * nw)
        dnw = dh_v * sc1
        acc_ref[2:3, :] += _colsum(dnw * n)
        dn = dnw * w_ref[...]
        dx = r * (dn - n * jnp.mean(dn * n, axis=-1, keepdims=True)) + dres_ref[...]
        dx_ref[...] = dx
        if has_mix:
            acc_ref[3:4, :] += _colsum(dx * mix_ref[...])
            dmix_ref[...] = (dx * mod_ref[gate_row:gate_row + 1, :]).astype(BF16)

    ins = [dh, xin, dres, mod, w] + ([mix] if has_mix else [])
    in_specs = [_row_spec(), _row_spec(), _row_spec(), _const_spec((8, D)), _const_spec((1, D))] + ([_row_spec()] if has_mix else [])
    out_shape = [_sds((L, D), F32)] + ([_sds((L, D), BF16)] if has_mix else []) + [_sds((8, D), F32)]
    out_specs = [_row_spec()] + ([_row_spec()] if has_mix else []) + [_const_spec((8, D))]
    return pl.pallas_call(body, name=name, grid=(L // TR,), out_shape=out_shape, in_specs=in_specs,
                          out_specs=out_specs, compiler_params=_cparams(("arbitrary",)))(*ins)


def _halo(k):
    return 8 if k <= 9 else 32


def _prev_spec(h, col0):
    return pl.BlockSpec((h, CB), lambda j, i: (jnp.maximum(i * (TC // h) - 1, 0), j + col0))


def _next_spec(h, col0, n_tiles):
    return pl.BlockSpec((h, CB), lambda j, i: (jnp.minimum(i + 1, n_tiles - 1) * (TC // h), j + col0))


def _tile_spec(col0):
    return pl.BlockSpec((TC, CB), lambda j, i: (i, j + col0))


def _w_spec(kp, col0):
    return pl.BlockSpec((kp, CB), lambda j, i: (0, j + col0))


SUBLANES = 8


def _shifted_windows(v, taps, rows):
    for r in range(SUBLANES):
        group = [(o, k) for o, k in taps if o % SUBLANES == r]
        if not group:
            continue
        s = v if r == 0 else pltpu.roll(v, v.shape[0] - r, 0)
        for o, k in group:
            yield k, s[o - r:o - r + rows, :]


def _causal_taps(ext_ref, w_ref, k_taps, first, rows):
    acc = None
    for k, win in _shifted_windows(ext_ref[...], [(first - (k_taps - 1) + k, k) for k in range(k_taps)], rows):
        t = w_ref[k:k + 1, :] * win
        acc = t if acc is None else acc + t
    return acc


def _anticausal_taps(d_ref, w_ref, k_taps, rows):
    acc = None
    for k, win in _shifted_windows(d_ref[...], [(k_taps - 1 - k, k) for k in range(k_taps)], rows):
        t = w_ref[k:k + 1, :] * win
        acc = t if acc is None else acc + t
    return acc


def _acc_conv_wgrad(dw_ref, d_tile, ext_ref, k_taps, first):
    for k, win in _shifted_windows(ext_ref[...], [(first - (k_taps - 1) + k, k) for k in range(k_taps)], TC):
        dw_ref[k:k + 1, :] += _colsum(d_tile * win)
    dw_ref[k_taps:k_taps + 1, :] += _colsum(d_tile)


def conv_silu_fwd(x, col0, width, w, b, name):
    L = x.shape[0]
    k_taps = w.shape[0]
    h = _halo(k_taps)

    def body(xp_ref, x_ref, w_ref, b_ref, o_ref, ext_ref):
        i = pl.program_id(1)
        ext_ref[0:h, :] = jnp.where(i > 0, xp_ref[...], 0.0)
        ext_ref[h:h + TC, :] = x_ref[...]
        o_ref[...] = _silu(_causal_taps(ext_ref, w_ref, k_taps, h, TC) + b_ref[...])

    return pl.pallas_call(
        body, name=name, grid=(width // CB, L // TC), out_shape=_sds((L, width), F32),
        in_specs=[_prev_spec(h, col0), _tile_spec(col0), _w_spec(k_taps, 0), pl.BlockSpec((1, CB), lambda j, i: (0, j))],
        out_specs=_tile_spec(0), scratch_shapes=[pltpu.VMEM((h + TC, CB), F32)],
        compiler_params=_cparams(("parallel", "parallel")))(x, x, w, b)


def conv_silu_bwd(x, col0, width, w, b, dpost, name):
    L = x.shape[0]
    k_taps = w.shape[0]
    h = _halo(k_taps)
    nt = L // TC

    def body(xp_ref, x_ref, xn_ref, d_ref, dn_ref, w_ref, b_ref, dx_ref, dw_ref, ext_ref, dpre_ref):
        i = pl.program_id(1)

        @pl.when(i == 0)
        def _():
            dw_ref[...] = jnp.zeros((8, CB), F32)

        ext_ref[0:h, :] = jnp.where(i > 0, xp_ref[...], 0.0)
        ext_ref[h:h + TC, :] = x_ref[...]
        ext_ref[h + TC:h + TC + h, :] = xn_ref[...]
        pre = _causal_taps(ext_ref, w_ref, k_taps, h, TC + h) + b_ref[...]
        dpre_ref[0:TC, :] = d_ref[...] * _dsilu(pre[0:TC, :])
        dpre_ref[TC:TC + h, :] = jnp.where(i < nt - 1, dn_ref[...], 0.0) * _dsilu(pre[TC:TC + h, :])
        dx_ref[...] = _anticausal_taps(dpre_ref, w_ref, k_taps, TC).astype(BF16)
        _acc_conv_wgrad(dw_ref, dpre_ref[0:TC, :], ext_ref, k_taps, h)

    return pl.pallas_call(
        body, name=name, grid=(width // CB, nt),
        out_shape=[_sds((L, width), BF16), _sds((8, width), F32)],
        in_specs=[_prev_spec(h, col0), _tile_spec(col0), _next_spec(h, col0, nt), _tile_spec(0), _next_spec(h, 0, nt),
                  _w_spec(k_taps, 0), pl.BlockSpec((1, CB), lambda j, i: (0, j))],
        out_specs=[_tile_spec(0), _w_spec(8, 0)],
        scratch_shapes=[pltpu.VMEM((h + TC + h, CB), F32), pltpu.VMEM((TC + h, CB), F32)],
        compiler_params=_cparams(("parallel", "arbitrary")))(x, x, x, dpost, dpost, w, b)


def conf_conv_fwd(proj, col_a, col_g, w, b, name):
    L = proj.shape[0]
    k_taps = w.shape[0]
    h = _halo(k_taps)

    def body(ap_ref, a_ref, gp_ref, g_ref, w_ref, b_ref, o_ref, ext_ref):
        i = pl.program_id(1)
        ext_ref[0:h, :] = jnp.where(i > 0, ap_ref[...] * _sigmoid(gp_ref[...]), 0.0)
        ext_ref[h:h + TC, :] = a_ref[...] * _sigmoid(g_ref[...])
        o_ref[...] = _causal_taps(ext_ref, w_ref, k_taps, h, TC) + b_ref[...]

    return pl.pallas_call(
        body, name=name, grid=(D_CONF // CB, L // TC), out_shape=_sds((L, D_CONF), F32),
        in_specs=[_prev_spec(h, col_a), _tile_spec(col_a), _prev_spec(h, col_g), _tile_spec(col_g), _w_spec(k_taps, 0),
                  pl.BlockSpec((1, CB), lambda j, i: (0, j))],
        out_specs=_tile_spec(0), scratch_shapes=[pltpu.VMEM((h + TC, CB), F32)],
        compiler_params=_cparams(("parallel", "parallel")))(proj, proj, proj, proj, w, b)


def conf_conv_bwd(proj, col_a, col_g, w, duc, name):
    L = proj.shape[0]
    k_taps = w.shape[0]
    h = _halo(k_taps)
    nt = L // TC

    def body(ap_ref, a_ref, gp_ref, g_ref, d_ref, dn_ref, w_ref, da_ref, dg_ref, dw_ref, ext_ref, dext_ref):
        i = pl.program_id(1)

        @pl.when(i == 0)
        def _():
            dw_ref[...] = jnp.zeros((32, CB), F32)

        a = a_ref[...]
        s = _sigmoid(g_ref[...])
        ext_ref[0:h, :] = jnp.where(i > 0, ap_ref[...] * _sigmoid(gp_ref[...]), 0.0)
        ext_ref[h:h + TC, :] = a * s
        dext_ref[0:TC, :] = d_ref[...]
        dext_ref[TC:TC + h, :] = jnp.where(i < nt - 1, dn_ref[...], 0.0)
        du0 = _anticausal_taps(dext_ref, w_ref, k_taps, TC)
        da_ref[...] = (du0 * s).astype(BF16)
        dg_ref[...] = (du0 * a * s * (1.0 - s)).astype(BF16)
        _acc_conv_wgrad(dw_ref, d_ref[...], ext_ref, k_taps, h)

    return pl.pallas_call(
        body, name=name, grid=(D_CONF // CB, nt),
        out_shape=[_sds((L, D_CONF), BF16), _sds((L, D_CONF), BF16), _sds((32, D_CONF), F32)],
        in_specs=[_prev_spec(h, col_a), _tile_spec(col_a), _prev_spec(h, col_g), _tile_spec(col_g), _tile_spec(0),
                  _next_spec(h, 0, nt), _w_spec(k_taps, 0)],
        out_specs=[_tile_spec(0), _tile_spec(0), _w_spec(32, 0)],
        scratch_shapes=[pltpu.VMEM((h + TC, CB), F32), pltpu.VMEM((TC + h, CB), F32)],
        compiler_params=_cparams(("parallel", "arbitrary")))(proj, proj, proj, proj, duc, duc, w)


def ffn_conv_fwd(up, w, b, name):
    L = up.shape[0]
    k_taps = w.shape[0]
    h = _halo(k_taps)
    cv = D_FF // CB

    def body(gp_ref, g_ref, vp_ref, v_ref, wg_ref, wv_ref, bg_ref, bv_ref, o_ref, eg_ref, ev_ref):
        i = pl.program_id(1)
        eg_ref[0:h, :] = jnp.where(i > 0, gp_ref[...], 0.0)
        eg_ref[h:h + TC, :] = g_ref[...]
        ev_ref[0:h, :] = jnp.where(i > 0, vp_ref[...], 0.0)
        ev_ref[h:h + TC, :] = v_ref[...]
        pg = _causal_taps(eg_ref, wg_ref, k_taps, h, TC) + bg_ref[...]
        pv = _causal_taps(ev_ref, wv_ref, k_taps, h, TC) + bv_ref[...]
        o_ref[...] = (_silu(pg) * pv).astype(BF16)

    bspec = lambda c0: pl.BlockSpec((1, CB), lambda j, i: (0, j + c0))
    return pl.pallas_call(
        body, name=name, grid=(cv, L // TC), out_shape=_sds((L, D_FF), BF16),
        in_specs=[_prev_spec(h, 0), _tile_spec(0), _prev_spec(h, cv), _tile_spec(cv), _w_spec(k_taps, 0), _w_spec(k_taps, cv),
                  bspec(0), bspec(cv)],
        out_specs=_tile_spec(0), scratch_shapes=[pltpu.VMEM((h + TC, CB), F32), pltpu.VMEM((h + TC, CB), F32)],
        compiler_params=_cparams(("parallel", "parallel")))(up, up, up, up, w, w, b, b)


def ffn_conv_bwd(up, w, b, dact, name):
    L = up.shape[0]
    k_taps = w.shape[0]
    h = _halo(k_taps)
    nt = L // TC
    cv = D_FF // CB

    def body(gp_ref, g_ref, gn_ref, vp_ref, v_ref, vn_ref, d_ref, dn_ref, wg_ref, wv_ref, bg_ref, bv_ref,
             dg_ref, dv_ref, dwg_ref, dwv_ref, eg_ref, ev_ref, pg_ref, pv_ref):
        i = pl.program_id(1)

        @pl.when(i == 0)
        def _():
            dwg_ref[...] = jnp.zeros((8, CB), F32)
            dwv_ref[...] = jnp.zeros((8, CB), F32)

        for e_ref, p_ref, c_ref, n_ref in ((eg_ref, gp_ref, g_ref, gn_ref), (ev_ref, vp_ref, v_ref, vn_ref)):
            e_ref[0:h, :] = jnp.where(i > 0, p_ref[...], 0.0)
            e_ref[h:h + TC, :] = c_ref[...]
            e_ref[h + TC:h + TC + h, :] = n_ref[...]
        pg = _causal_taps(eg_ref, wg_ref, k_taps, h, TC + h) + bg_ref[...]
        pv = _causal_taps(ev_ref, wv_ref, k_taps, h, TC + h) + bv_ref[...]
        dact_t = d_ref[...]
        dact_n = jnp.where(i < nt - 1, dn_ref[...], 0.0)
        pg_ref[0:TC, :] = dact_t * pv[0:TC, :] * _dsilu(pg[0:TC, :])
        pg_ref[TC:TC + h, :] = dact_n * pv[TC:TC + h, :] * _dsilu(pg[TC:TC + h, :])
        pv_ref[0:TC, :] = dact_t * _silu(pg[0:TC, :])
        pv_ref[TC:TC + h, :] = dact_n * _silu(pg[TC:TC + h, :])
        dg_ref[...] = _anticausal_taps(pg_ref, wg_ref, k_taps, TC).astype(BF16)
        dv_ref[...] = _anticausal_taps(pv_ref, wv_ref, k_taps, TC).astype(BF16)
        _acc_conv_wgrad(dwg_ref, pg_ref[0:TC, :], eg_ref, k_taps, h)
        _acc_conv_wgrad(dwv_ref, pv_ref[0:TC, :], ev_ref, k_taps, h)

    bspec = lambda c0: pl.BlockSpec((1, CB), lambda j, i: (0, j + c0))
    ext = pltpu.VMEM((h + TC + h, CB), F32)
    dpre = pltpu.VMEM((TC + h, CB), F32)
    return pl.pallas_call(
        body, name=name, grid=(cv, nt),
        out_shape=[_sds((L, D_FF), BF16), _sds((L, D_FF), BF16), _sds((8, D_FF), F32), _sds((8, D_FF), F32)],
        in_specs=[_prev_spec(h, 0), _tile_spec(0), _next_spec(h, 0, nt), _prev_spec(h, cv), _tile_spec(cv), _next_spec(h, cv, nt),
                  _tile_spec(0), _next_spec(h, 0, nt), _w_spec(k_taps, 0), _w_spec(k_taps, cv), bspec(0), bspec(cv)],
        out_specs=[_tile_spec(0), _tile_spec(0), _w_spec(8, 0), _w_spec(8, 0)],
        scratch_shapes=[ext, ext, dpre, dpre],
        compiler_params=_cparams(("parallel", "arbitrary")))(up, up, up, up, up, up, dact, dact, w, w, b, b)


def _ssd_common(xbc_ref, dt_ref, dtb_ref, alog_ref, cs_ref):
    xs = xbc_ref[:, 0:D_SSD]
    sp_in = dt_ref[...] + dtb_ref[...]
    dtf = _softplus(sp_in)
    a_f = -jnp.exp(alog_ref[...])
    a_dt = dtf * a_f
    row = lax.broadcasted_iota(jnp.int32, (Q, Q), 0)
    col = lax.broadcasted_iota(jnp.int32, (Q, Q), 1)
    causal = row >= col
    cs = _dot_exact(causal.astype(F32), a_dt, 3, "a")
    cs_ref[...] = cs
    cs_last = cs_ref[Q - 1:Q, :]
    return xs, sp_in, dtf, a_f, cs, cs_last, causal


def _head_decay(cs_j, cst_ref, e, causal):
    lane = lax.broadcasted_iota(jnp.int32, (Q, LANE), 1)
    rolled = pltpu.roll(cs_j, HEAD, 1)
    own = (lane < HEAD) if e == 0 else (lane >= HEAD)
    col_b = jnp.where(own, cs_j, rolled)
    col_b = jnp.concatenate([col_b] * (Q // LANE), axis=1)
    row_b = cst_ref[e * HEAD:e * HEAD + 1, :]
    return jnp.where(causal, jnp.exp(jnp.minimum(col_b - row_b, 0.0)), 0.0)


def ssd_fwd(xbc, proj, dtb_f, alog_f, dsk_f, snw, name):
    L = xbc.shape[0]
    nc = L // Q

    def body(xbc_ref, z_ref, dt_ref, dtb_ref, alog_ref, dsk_ref, snw_ref, y_ref, yn_ref, sp_ref, s_ref, cs_ref, cst_ref, yd_ref):
        @pl.when(pl.program_id(0) == 0)
        def _():
            s_ref[...] = jnp.zeros((N_STATE, D_SSD), F32)

        xs, _, dtf, a_f, cs, cs_last, causal = _ssd_common(xbc_ref, dt_ref, dtb_ref, alog_ref, cs_ref)
        e_cs = jnp.exp(cs)
        xdt = xs * dtf
        zst = jnp.exp(cs_last - cs) * xdt
        sp_ref[0] = s_ref[...]
        lane = lax.broadcasted_iota(jnp.int32, (Q, LANE), 1)
        for g in range(2):
            gl = slice(g * 512, g * 512 + 512)
            b_g = xbc_ref[:, D_SSD + g * N_STATE:D_SSD + (g + 1) * N_STATE]
            c_g = xbc_ref[:, D_SSD + 2 * N_STATE + g * N_STATE:D_SSD + 2 * N_STATE + (g + 1) * N_STATE]
            s_prev = s_ref[:, gl]
            cb = _dot_nt(c_g, b_g)
            yd_ref[:, gl] = e_cs[:, gl] * _dot(c_g, s_prev)
            for j in range(4):
                tl = slice(g * 512 + j * LANE, g * 512 + (j + 1) * LANE)
                cs_j = cs[:, tl]
                cst_ref[...] = cs_j.T
                x_j = xdt[:, tl]
                o0 = _dot(cb * _head_decay(cs_j, cst_ref, 0, causal), x_j)
                o1 = _dot(cb * _head_decay(cs_j, cst_ref, 1, causal), x_j)
                yd_ref[:, tl] += jnp.where(lane < HEAD, o0, o1)
            s_ref[:, gl] = jnp.exp(cs_last[:, gl]) * s_prev + _dot_tn(b_g, zst[:, gl])
        y = yd_ref[...] + xs * dsk_ref[...]
        y_ref[...] = y
        yz = y * _silu(z_ref[...])
        r = lax.rsqrt(jnp.mean(yz * yz, axis=-1, keepdims=True) + 1e-6)
        yn_ref[...] = (yz * r * snw_ref[...]).astype(BF16)

    chunk = lambda w, c: pl.BlockSpec((Q, w), lambda i: (i, c))
    return pl.pallas_call(
        body, name=name, grid=(nc,),
        out_shape=[_sds((L, D_SSD), F32), _sds((L, D_SSD), BF16), _sds((nc, N_STATE, D_SSD), F32)],
        in_specs=[chunk(D_XBC, 0), chunk(D, 0), chunk(D, 1)] + [_const_spec((1, D))] * 4,
        out_specs=[chunk(D, 0), chunk(D, 0), pl.BlockSpec((1, N_STATE, D_SSD), lambda i: (i, 0, 0))],
        scratch_shapes=[pltpu.VMEM((N_STATE, D_SSD), F32), pltpu.VMEM((Q, D_SSD), F32), pltpu.VMEM((LANE, Q), F32),
                        pltpu.VMEM((Q, D_SSD), F32)],
        compiler_params=_cparams(("arbitrary",)))(xbc, proj, proj, dtb_f, alog_f, dsk_f, snw)


def ssd_bwd(dmixin, y, xbc, proj, s_prev_all, dtb_f, alog_f, dsk_f, snw, name):
    L = xbc.shape[0]
    nc = L // Q

    def body(dyn_ref, y_ref, xbc_ref, z_ref, dt_ref, sp_ref, dtb_ref, alog_ref, dsk_ref, snw_ref,
             dz_ref, ddt_ref, dxbc_ref, acc_ref, acc16_ref, ds_ref, cs_ref, cst_ref, dcs_ref, dx_ref):
        step = pl.program_id(0)

        @pl.when(step == 0)
        def _():
            ds_ref[...] = jnp.zeros((N_STATE, D_SSD), F32)
            acc_ref[...] = jnp.zeros((8, D), F32)

        z = z_ref[...]
        y = y_ref[...]
        sz = _sigmoid(z)
        siluz = z * sz
        yz = y * siluz
        r = lax.rsqrt(jnp.mean(yz * yz, axis=-1, keepdims=True) + 1e-6)
        n = yz * r
        dyn = dyn_ref[...]
        acc_ref[0:1, :] += _colsum(dyn * n)
        dn = dyn * snw_ref[...]
        dyz = r * (dn - n * jnp.mean(dn * n, axis=-1, keepdims=True))
        dy = dyz * siluz
        dz_ref[...] = (dyz * y * (sz * (1.0 + z * (1.0 - sz)))).astype(BF16)

        xs, sp_in, dtf, a_f, cs, cs_last, causal = _ssd_common(xbc_ref, dt_ref, dtb_ref, alog_ref, cs_ref)
        acc_ref[3:4, :] += _colsum(dy * xs)
        e_cs = jnp.exp(cs)
        xdt = xs * dtf
        dst = jnp.exp(cs_last - cs)
        zst = dst * xdt
        e_last = jnp.exp(cs_last)
        lane = lax.broadcasted_iota(jnp.int32, (Q, LANE), 1)
        ones = jnp.ones((Q, LANE), F32)
        dcs_last_parts = []
        for g in range(2):
            gl = slice(g * 512, g * 512 + 512)
            b_g = xbc_ref[:, D_SSD + g * N_STATE:D_SSD + (g + 1) * N_STATE]
            c_g = xbc_ref[:, D_SSD + 2 * N_STATE + g * N_STATE:D_SSD + 2 * N_STATE + (g + 1) * N_STATE]
            s_prev = sp_ref[0, :, gl]
            ds_g = ds_ref[:, gl]
            dy_g = dy[:, gl]
            cb = _dot_nt(c_g, b_g)
            y_off = e_cs[:, gl] * _dot(c_g, s_prev)
            edy = e_cs[:, gl] * dy_g
            d_c = _dot_nt(edy, s_prev)
            d_z = _dot(b_g, ds_g)
            d_b = _dot_nt(zst[:, gl], ds_g)
            t_g = d_z * zst[:, gl]
            dcs_ref[:, gl] = dy_g * y_off - t_g
            dx_ref[:, gl] = d_z * dst[:, gl]
            dcs_last_parts.append(_colsum(t_g) + _colsum(ds_g * s_prev) * e_last[:, gl])
            ds_ref[:, gl] = e_last[:, gl] * ds_g + _dot_tn(c_g, edy)
            dcb = jnp.zeros((Q, Q), F32)
            for j in range(4):
                tl = slice(g * 512 + j * LANE, g * 512 + (j + 1) * LANE)
                cs_j = cs[:, tl]
                cst_ref[...] = cs_j.T
                x_j = xdt[:, tl]
                dy_j = dy[:, tl]
                dx_j = jnp.zeros((Q, LANE), F32)
                dcs_j = jnp.zeros((Q, LANE), F32)
                for e in range(2):
                    own = (lane < HEAD) if e == 0 else (lane >= HEAD)
                    w_h = _head_decay(cs_j, cst_ref, e, causal)
                    g_h = cb * w_h
                    dy_m = jnp.where(own, dy_j, 0.0)
                    d_g = _dot_nt(dy_m, x_j)
                    dx_j = dx_j + _dot_tn(g_h, dy_m)
                    dcb = dcb + d_g * w_h
                    p_h = d_g * g_h
                    row_sums = _dot_exact(p_h, ones, 2, "b")
                    col_sums = _dot_exact(p_h, ones, 2, "b", (((0,), (0,)), ((), ())))
                    dcs_j = dcs_j + jnp.where(own, row_sums - col_sums, 0.0)
                dcs_ref[:, tl] += dcs_j * (1.0 / HEAD)
                dx_ref[:, tl] += dx_j
            d_c = d_c + _dot(dcb, b_g)
            d_b = d_b + _dot_tn(dcb, c_g)
            dxbc_ref[:, D_SSD + g * N_STATE:D_SSD + (g + 1) * N_STATE] = d_b
            dxbc_ref[:, D_SSD + 2 * N_STATE + g * N_STATE:D_SSD + 2 * N_STATE + (g + 1) * N_STATE] = d_c
        dcs_last = jnp.concatenate(dcs_last_parts, axis=1)
        anticausal = lax.broadcasted_iota(jnp.int32, (Q, Q), 0) <= lax.broadcasted_iota(jnp.int32, (Q, Q), 1)
        d_adt = _dot_exact(anticausal.astype(F32), dcs_ref[...], 3, "a") + dcs_last
        dx = dx_ref[...]
        acc_ref[2:3, :] += _colsum(d_adt * dtf) * a_f
        d_dtf = d_adt * a_f + dx * xs
        dxbc_ref[:, 0:D_SSD] = dx * dtf + dy * dsk_ref[...]
        d_raw = d_dtf * _sigmoid(sp_in)
        acc_ref[1:2, :] += _colsum(d_raw)
        head_of_lane = lax.broadcasted_iota(jnp.int32, (D_SSD, LANE), 0) // HEAD
        fold = (head_of_lane == lax.broadcasted_iota(jnp.int32, (D_SSD, LANE), 1)).astype(F32)
        ddt_ref[...] = _dot_exact(d_raw, fold, 2, "b").astype(BF16)

        @pl.when(step == nc - 1)
        def _():
            acc16_ref[...] = _dot_exact(acc_ref[...], fold, 3, "b")

    rchunk = lambda w, c: pl.BlockSpec((Q, w), lambda i: (nc - 1 - i, c))
    return pl.pallas_call(
        body, name=name, grid=(nc,),
        out_shape=[_sds((L, D_SSD), BF16), _sds((L, LANE), BF16), _sds((L, D_XBC), F32), _sds((8, D), F32), _sds((8, LANE), F32)],
        in_specs=[rchunk(D, 0), rchunk(D, 0), rchunk(D_XBC, 0), rchunk(D, 0), rchunk(D, 1),
                  pl.BlockSpec((1, N_STATE, D_SSD), lambda i: (nc - 1 - i, 0, 0))] + [_const_spec((1, D))] * 4,
        out_specs=[rchunk(D, 0), rchunk(LANE, 0), rchunk(D_XBC, 0), _const_spec((8, D)), _const_spec((8, LANE))],
        scratch_shapes=[pltpu.VMEM((N_STATE, D_SSD), F32), pltpu.VMEM((Q, D_SSD), F32), pltpu.VMEM((LANE, Q), F32),
                        pltpu.VMEM((Q, D_SSD), F32), pltpu.VMEM((Q, D_SSD), F32)],
        compiler_params=_cparams(("arbitrary",)))(dmixin, y, xbc, proj, proj, s_prev_all, dtb_f, alog_f, dsk_f, snw)


def _adamw_math(w, g, m, v):
    m_n = ADAM_B1 * m + (1.0 - ADAM_B1) * g
    v_n = ADAM_B2 * v + (1.0 - ADAM_B2) * jnp.square(g)
    c1 = 1.0 - ADAM_B1 ** ADAM_STEP
    c2 = 1.0 - ADAM_B2 ** ADAM_STEP
    return -ADAM_LR * ((m_n / c1) / (jnp.sqrt(v_n / c2) + ADAM_EPS) + ADAM_WD * w), m_n, v_n


def _sum_slots(p_ref):
    acc = p_ref[0].astype(F32)
    for s in range(1, N_DEV):
        acc = acc + p_ref[s].astype(F32)
    return acc


def adamw_slots(w, slots, m, v, name):
    rows, cols = w.shape
    tr = next(t for t in (256, 128, 64, 32, 16) if rows % t == 0)

    def body(w_ref, s_ref, m_ref, v_ref, g_ref, d_ref, mo_ref, vo_ref):
        g_v = _sum_slots(s_ref)
        g_ref[...] = g_v
        d_ref[...], mo_ref[...], vo_ref[...] = _adamw_math(w_ref[...], g_v, m_ref[...], v_ref[...])

    spec = pl.BlockSpec((tr, cols), lambda i: (i, 0))
    return pl.pallas_call(body, name=name, grid=(rows // tr,), out_shape=[_sds((rows, cols), F32)] * 4,
                          in_specs=[spec, pl.BlockSpec((N_DEV, tr, cols), lambda i: (0, i, 0)), spec, spec], out_specs=[spec] * 4,
                          compiler_params=_cparams(("parallel",)))(w, slots, m, v)


def adamw_many(ws, gs, ms, vs, name):
    n = len(ws)

    def body(*refs):
        for p in range(n):
            d_v, m_v, v_v = _adamw_math(refs[p][...], refs[n + p][...], refs[2 * n + p][...], refs[3 * n + p][...])
            refs[4 * n + p][...] = d_v
            refs[5 * n + p][...] = m_v
            refs[6 * n + p][...] = v_v

    vm = pl.BlockSpec(memory_space=pltpu.VMEM)
    out = pl.pallas_call(body, name=name, out_shape=[_sds(w.shape, F32) for w in ws] * 3, in_specs=[vm] * (4 * n),
                         out_specs=[vm] * (3 * n), compiler_params=_cparams())(*ws, *gs, *ms, *vs)
    return out[:n], out[n:2 * n], out[2 * n:]


def _pack_layout(shapes):
    row, layout = 0, []
    for rows, cols in shapes:
        chunks = []
        for c0 in range(0, cols, D):
            chunks.append((row, c0, min(D, cols - c0)))
            row += rows
        layout.append(chunks)
    return row, layout


def pack_rows(arrays, name):
    total, layout = _pack_layout([a.shape for a in arrays])
    n = len(arrays)

    def body(*refs):
        o_ref = refs[n]
        o_ref[...] = jnp.zeros((total, D), F32)
        for p in range(n):
            rows = arrays[p].shape[0]
            for r0, c0, w in layout[p]:
                o_ref[r0:r0 + rows, 0:w] = refs[p][:, c0:c0 + w]

    vm = pl.BlockSpec(memory_space=pltpu.VMEM)
    return pl.pallas_call(body, name=name, out_shape=_sds((total, D), F32), in_specs=[vm] * n, out_specs=vm,
                          compiler_params=_cparams())(*arrays)


def unpack_rows(packed, shapes):
    _, layout = _pack_layout(shapes)
    out = []
    for (rows, _), chunks in zip(shapes, layout):
        parts = [packed[..., r0:r0 + rows, 0:w] for r0, _, w in chunks]
        out.append(parts[0] if len(parts) == 1 else jnp.concatenate(parts, axis=-1))
    return out


def sum_slots_many(parts, name):
    n = len(parts)

    def body(*refs):
        for p in range(n):
            refs[n + p][...] = _sum_slots(refs[p])

    vm = pl.BlockSpec(memory_space=pltpu.VMEM)
    return pl.pallas_call(body, name=name, out_shape=[_sds(p.shape[1:], F32) for p in parts], in_specs=[vm] * n,
                          out_specs=[vm] * n, compiler_params=_cparams())(*parts)


def ada_mod(c_all, ada_w_shard, ada_b_cols, name):
    def body(c_ref, w_ref, b_ref, o_ref, ca_ref):
        ca = _silu(c_ref[...])
        ca_ref[...] = ca
        o_ref[...] = _dot(ca, w_ref[...]) + b_ref[...]

    vm = pl.BlockSpec(memory_space=pltpu.VMEM)
    return pl.pallas_call(body, name=name, out_shape=[_sds((N_DEV, ada_w_shard.shape[1]), F32), _sds((N_DEV, D), F32)],
                          in_specs=[vm, vm, vm], out_specs=[vm, vm], compiler_params=_cparams())(c_all, ada_w_shard, ada_b_cols)


def ada_wgrad(c_act_all, dmod_cols, name):
    def body(c_ref, d_ref, o_ref):
        o_ref[...] = _dot_tn_hi(c_ref[...], d_ref[...])

    vm = pl.BlockSpec(memory_space=pltpu.VMEM)
    return pl.pallas_call(body, name=name, out_shape=_sds((D, dmod_cols.shape[1]), F32), in_specs=[vm, vm], out_specs=vm,
                          compiler_params=_cparams())(c_act_all, dmod_cols)


def exchange(srcs, name, gather):
    n = len(srcs)
    shapes = [tuple(s.shape) if gather else tuple(s.shape[1:]) for s in srcs]

    def body(*refs):
        src_refs, out_refs = refs[:n], refs[n:2 * n]
        send_sems, recv_sems, local_sems = refs[2 * n:]
        x, y, c = lax.axis_index("x"), lax.axis_index("y"), lax.axis_index("c")
        me = 4 * x + 2 * y + c

        def peer(k):
            bx, by, bc = (k >> 2) & 1, (k >> 1) & 1, k & 1
            px, py, pc = (x + bx) % 2, (y + by) % 2, (c + bc) % 2
            return (px, py, pc), 4 * px + 2 * py + pc

        def copy(a, k, landing):
            dev, idx = peer(k)
            return pltpu.make_async_remote_copy(
                src_ref=src_refs[a] if gather else src_refs[a].at[idx], dst_ref=out_refs[a].at[idx if landing else me],
                send_sem=send_sems.at[a, k - 1], recv_sem=recv_sems.at[a, k - 1],
                device_id=dev, device_id_type=pl.DeviceIdType.MESH)

        mine = [pltpu.make_async_copy(src_refs[a] if gather else src_refs[a].at[me], out_refs[a].at[me], local_sems.at[a])
                for a in range(n)]
        for cp in mine:
            cp.start()
        sends = [copy(a, k, False) for a in range(n) for k in range(1, N_DEV)]
        for cp in sends:
            cp.start()
        for a in range(n):
            for k in range(1, N_DEV):
                copy(a, k, True).wait_recv()
        for cp in sends:
            cp.wait_send()
        for cp in mine:
            cp.wait()

    hbm = pl.BlockSpec(memory_space=pl.ANY)
    return pl.pallas_call(
        body, name=name, out_shape=[_sds((N_DEV,) + shp, s.dtype) for shp, s in zip(shapes, srcs)], in_specs=[hbm] * n,
        out_specs=[hbm] * n,
        scratch_shapes=[pltpu.SemaphoreType.DMA((n, N_DEV - 1)), pltpu.SemaphoreType.DMA((n, N_DEV - 1)),
                        pltpu.SemaphoreType.DMA((n,))],
        compiler_params=pltpu.CompilerParams(has_side_effects=True))(*srcs)


def gather_two_level(src, name):
    def body(src_ref, out_ref, send_sems, recv_sems, local_sem):
        x, y, c = lax.axis_index("x"), lax.axis_index("y"), lax.axis_index("c")
        me, sibling = (x, y, c), (x, y, 1 - c)
        chips = [(1 - x, y), (x, 1 - y), (1 - x, 1 - y)]

        def slot(px, py, pc):
            return out_ref.at[4 * px + 2 * py + pc]

        def copy(k, block, to, src=None):
            return pltpu.make_async_remote_copy(
                src_ref=slot(*block) if src is None else src, dst_ref=slot(*block), send_sem=send_sems.at[k],
                recv_sem=recv_sems.at[k], device_id=to, device_id_type=pl.DeviceIdType.MESH)

        mine = pltpu.make_async_copy(src_ref, slot(*me), local_sem)
        mine.start()
        first = [copy(0, me, sibling, src=src_ref)]
        first += [copy(1 + j, me, (*chip, c), src=src_ref) for j, chip in enumerate(chips)]
        for cp in first:
            cp.start()
        passed = [copy(4 + j, (*chip, c), sibling) for j, chip in enumerate(chips)]
        for j, chip in enumerate(chips):
            copy(1 + j, (*chip, c), me).wait_recv()
            passed[j].start()
        copy(0, sibling, me).wait_recv()
        for j, chip in enumerate(chips):
            copy(4 + j, (*chip, 1 - c), me).wait_recv()
        for cp in first + passed:
            cp.wait_send()
        mine.wait()

    hbm = pl.BlockSpec(memory_space=pl.ANY)
    return pl.pallas_call(
        body, name=name, out_shape=_sds((N_DEV,) + tuple(src.shape), src.dtype), in_specs=[hbm], out_specs=hbm,
        scratch_shapes=[pltpu.SemaphoreType.DMA((N_DEV - 1,)), pltpu.SemaphoreType.DMA((N_DEV - 1,)), pltpu.SemaphoreType.DMA],
        compiler_params=pltpu.CompilerParams(has_side_effects=True))(src)


def _peer(k):
    x, y, c = lax.axis_index("x"), lax.axis_index("y"), lax.axis_index("c")
    px, py, pc = (x + ((k >> 2) & 1)) % 2, (y + ((k >> 1) & 1)) % 2, (c + (k & 1)) % 2
    return (px, py, pc), 4 * px + 2 * py + pc


def _my_slot():
    return 4 * lax.axis_index("x") + 2 * lax.axis_index("y") + lax.axis_index("c")


_HBM = pl.BlockSpec(memory_space=pltpu.HBM)
_SEM = pl.BlockSpec(memory_space=pltpu.SEMAPHORE)
_EFFECT = pltpu.SideEffectType.DATAFLOW_SIDE_EFFECTING


def exchange_start(srcs, name, gather):
    n = len(srcs)
    shapes = [tuple(s.shape) if gather else tuple(s.shape[1:]) for s in srcs]
    lands = [lax.empty((N_DEV,) + shp, s.dtype) for shp, s in zip(shapes, srcs)]

    def body(*refs):
        src_refs, land_refs = refs[:n], refs[n:2 * n]
        sems = refs[2 * n:4 * n]
        token = refs[-1]
        me = _my_slot()
        for a in range(n):
            for k in range(1, N_DEV):
                dev, idx = _peer(k)
                pltpu.make_async_remote_copy(
                    src_ref=src_refs[a] if gather else src_refs[a].at[idx], dst_ref=land_refs[a].at[me],
                    send_sem=sems[2 * a].at[k - 1], recv_sem=sems[2 * a + 1].at[k - 1],
                    device_id=dev, device_id_type=pl.DeviceIdType.MESH).start()
        token[...] = jnp.zeros_like(token)

    out_shape = ([pltpu.SemaphoreType.DMA((N_DEV - 1,))] * (2 * n) + [pltpu.HBM(s.shape, s.dtype) for s in srcs]
                 + [pltpu.HBM(l.shape, l.dtype) for l in lands] + [_sds((8, LANE), F32)])
    out = pl.pallas_call(
        body, name=name, out_shape=out_shape, in_specs=[_HBM] * (2 * n),
        out_specs=[_SEM] * (2 * n) + [_HBM] * (2 * n) + [pl.BlockSpec(memory_space=pltpu.VMEM)],
        input_output_aliases={i: 2 * n + i for i in range(2 * n)},
        compiler_params=pltpu.CompilerParams(has_side_effects=_EFFECT))(
            *[pltpu.with_memory_space_constraint(s, pltpu.HBM) for s in srcs],
            *[pltpu.with_memory_space_constraint(l, pltpu.HBM) for l in lands])
    parts = [(out[2 * a], out[2 * a + 1], out[2 * n + a], out[3 * n + a]) for a in range(n)]
    return parts, out[-1]


def exchange_wait(parts, after, name, gather):
    n = len(parts)

    def body(*refs):
        src_refs, land_refs = refs[:n], refs[n:2 * n]
        sems = refs[2 * n:4 * n]
        for a in range(n):
            for k in range(1, N_DEV):
                dev, idx = _peer(k)
                copy = pltpu.make_async_remote_copy(
                    src_ref=src_refs[a] if gather else src_refs[a].at[idx], dst_ref=land_refs[a].at[idx],
                    send_sem=sems[2 * a].at[k - 1], recv_sem=sems[2 * a + 1].at[k - 1],
                    device_id=dev, device_id_type=pl.DeviceIdType.MESH)
                copy.wait_send()
                copy.wait_recv()

    srcs = [p[2] for p in parts]
    lands = [p[3] for p in parts]
    sems = [s for p in parts for s in p[:2]]
    out = pl.pallas_call(
        body, name=name, out_shape=[pltpu.HBM(a.shape, a.dtype) for a in srcs + lands],
        in_specs=[_HBM] * (2 * n) + [_SEM] * (2 * n) + [pl.BlockSpec(memory_space=pl.ANY)], out_specs=[_HBM] * (2 * n),
        input_output_aliases={i: i for i in range(2 * n)},
        compiler_params=pltpu.CompilerParams(has_side_effects=_EFFECT))(*srcs, *lands, *sems, after)
    return out[n:]


def _cols_to_slabs(g):
    r, c = g.shape
    return g.reshape(r, N_DEV, c // N_DEV).transpose(1, 0, 2)


def _slabs_to_cols(s):
    _, r, cs = s.shape
    return s.transpose(1, 0, 2).reshape(r, N_DEV * cs)


def _rep_heads(v):
    return jnp.repeat(v.reshape(N_HEADS), HEAD).reshape(1, D_SSD)


def local_fwd_bwd(x, target, mod, get_w, put_grad, small):
    n1w, n2w, fnw = small["norm1_w"], small["norm2_w"], small["final_norm_w"]
    dtb_f, alog_f, dsk_f = _rep_heads(small["dt_bias"]), _rep_heads(small["a_log"]), _rep_heads(small["d_skip"])
    snw = small["ssd_norm_w"]

    def after(v, token):
        return v + token[0:1, 0:1]

    h1 = norm_mod(x, mod, n1w, 0, "norm1")
    w_in = get_w("w_in", h1)
    proj = mm_nn([(h1, w_in["w_in_p"], 0)], "in_proj")
    xbc = conv_silu_fwd(proj, 4096 // CB, D_XBC, small["ssd_conv_w"], small["ssd_conv_b"], "ssd_conv")
    y, ysn, s_prev = ssd_fwd(xbc, proj, dtb_f, alog_f, dsk_f, snw, "ssd_scan")
    uc = conf_conv_fwd(proj, 2048 // CB, 3072 // CB, small["conf_conv_w"], small["conf_conv_b"], "conf_conv")
    u = ln_silu(uc, small["conf_ln_w"], small["conf_ln_b"], "conf_ln")
    w_out = get_w("w_out", u)
    mix = mm_nn([(ysn, w_out, 0), (u, w_out, 1)], "out_proj")
    h2, x1 = norm_mod(x, mod, n2w, 3, "norm2", res=mix, gate_row=2)
    w_up = get_w("w_up", h2)
    up = mm_nn([(h2, w_up, 0)], "up_proj")
    act = ffn_conv_fwd(up, small["ffn_conv_w"], small["ffn_conv_b"], "ffn_conv")
    w_down = get_w("w_down", act)
    ff = mm_nn([(act, w_down, 0)], "down_proj")
    dx2, dff, acc_f = final_loss(ff, x1, mod, fnw, target, "final_loss")

    token = put_grad("w_down", mm_tn(act, dff, "wgrad_down"))
    dact = mm_nt([(dff, w_down, 0)], "dact")
    dupg, dupv, dwg, dwv = ffn_conv_bwd(up, small["ffn_conv_w"], after(small["ffn_conv_b"], token), dact, "ffn_conv_bwd")
    token = put_grad("w_up", jnp.concatenate([mm_tn(h2, dupg, "wgrad_up_gate"), mm_tn(h2, dupv, "wgrad_up_val")], axis=1))
    dh2 = mm_nt([(dupg, w_up, 0), (dupv, w_up, 1)], "dh2")
    dx1, dmix, acc_2 = norm_mod_bwd(dh2, x1, dx2, mod, after(n2w, token), 3, "norm2_bwd", mix=mix, gate_row=2)

    token = put_grad("w_out", jnp.concatenate([mm_tn(ysn, dmix, "wgrad_out_ssd"), mm_tn(u, dmix, "wgrad_out_conf")], axis=0))
    dmixin = mm_nt([(dmix, w_out, 0)], "dmixin")
    duc, acc_ln = ln_silu_bwd(dmixin, uc, after(small["conf_ln_w"], token), small["conf_ln_b"], "conf_ln_bwd")
    dcfa, dcfg, dw_cc = conf_conv_bwd(proj, 2048 // CB, 3072 // CB, small["conf_conv_w"], duc, "conf_conv_bwd")
    dz, ddt, dxbc_post, acc_s, acc_s16 = ssd_bwd(dmixin, y, xbc, proj, s_prev, dtb_f, alog_f, dsk_f, snw, "ssd_scan_bwd")
    dxbc, dw_sc = conv_silu_bwd(proj, 4096 // CB, D_XBC, small["ssd_conv_w"], small["ssd_conv_b"], dxbc_post, "ssd_conv_bwd")
    token = put_grad("w_in", jnp.concatenate(
        [mm_tn(h1, dz, "wgrad_in_z"), mm_tn(h1, dxbc, "wgrad_in_xbc"), mm_tn(h1, ddt, "wgrad_in_dt")[:, :N_HEADS],
         mm_tn(h1, dcfa, "wgrad_in_cfa"), mm_tn(h1, dcfg, "wgrad_in_cfg")], axis=1))
    dh1 = mm_nt([(dz, w_in["w_z"], 0), (ddt, w_in["w_dt16"], 0), (dcfa, w_in["w_cfa"], 0), (dcfg, w_in["w_cfg"], 0),
                 (dxbc, w_in["w_xbc"], 0)], "dh1")
    grad_x, acc_1 = norm_mod_bwd(dh1, x, dx1, mod, after(n1w, token), 0, "norm1_bwd")

    small_accs = dict(acc_1=acc_1, acc_2=acc_2, acc_f=acc_f, acc_ln=acc_ln, acc_s=acc_s, acc_s16=acc_s16, dw_sc=dw_sc,
                      dw_cc=dw_cc, dwg=dwg, dwv=dwv)
    return grad_x, small_accs


def kernel(x, c, ada_w, ada_b, norm1_w, w_in, ssd_conv_w, ssd_conv_b, dt_bias, a_log, d_skip, ssd_norm_w, conf_conv_w, conf_conv_b, conf_ln_w, conf_ln_b, w_out, norm2_w, w_up, ffn_conv_w, ffn_conv_b, w_down, final_norm_w, loss_target, m_ada_w, m_ada_b, m_norm1_w, m_w_in, m_ssd_conv_w, m_ssd_conv_b, m_dt_bias, m_a_log, m_d_skip, m_ssd_norm_w, m_conf_conv_w, m_conf_conv_b, m_conf_ln_w, m_conf_ln_b, m_w_out, m_norm2_w, m_w_up, m_ffn_conv_w, m_ffn_conv_b, m_w_down, m_final_norm_w, v_ada_w, v_ada_b, v_norm1_w, v_w_in, v_ssd_conv_w, v_ssd_conv_b, v_dt_bias, v_a_log, v_d_skip, v_ssd_norm_w, v_conf_conv_w, v_conf_conv_b, v_conf_ln_w, v_conf_ln_b, v_w_out, v_norm2_w, v_w_up, v_ffn_conv_w, v_ffn_conv_b, v_w_down, v_final_norm_w):
    me = 4 * lax.axis_index("x") + 2 * lax.axis_index("y") + lax.axis_index("c")
    weights = dict(ada_w=ada_w, ada_b=ada_b, norm1_w=norm1_w, w_in=w_in, ssd_conv_w=ssd_conv_w, ssd_conv_b=ssd_conv_b,
                   dt_bias=dt_bias, a_log=a_log, d_skip=d_skip, ssd_norm_w=ssd_norm_w, conf_conv_w=conf_conv_w,
                   conf_conv_b=conf_conv_b, conf_ln_w=conf_ln_w, conf_ln_b=conf_ln_b, w_out=w_out, norm2_w=norm2_w, w_up=w_up,
                   ffn_conv_w=ffn_conv_w, ffn_conv_b=ffn_conv_b, w_down=w_down, final_norm_w=final_norm_w)
    moms_m = dict(ada_w=m_ada_w, ada_b=m_ada_b, norm1_w=m_norm1_w, w_in=m_w_in, ssd_conv_w=m_ssd_conv_w, ssd_conv_b=m_ssd_conv_b,
                  dt_bias=m_dt_bias, a_log=m_a_log, d_skip=m_d_skip, ssd_norm_w=m_ssd_norm_w, conf_conv_w=m_conf_conv_w,
                  conf_conv_b=m_conf_conv_b, conf_ln_w=m_conf_ln_w, conf_ln_b=m_conf_ln_b, w_out=m_w_out, norm2_w=m_norm2_w,
                  w_up=m_w_up, ffn_conv_w=m_ffn_conv_w, ffn_conv_b=m_ffn_conv_b, w_down=m_w_down, final_norm_w=m_final_norm_w)
    moms_v = dict(ada_w=v_ada_w, ada_b=v_ada_b, norm1_w=v_norm1_w, w_in=v_w_in, ssd_conv_w=v_ssd_conv_w, ssd_conv_b=v_ssd_conv_b,
                  dt_bias=v_dt_bias, a_log=v_a_log, d_skip=v_d_skip, ssd_norm_w=v_ssd_norm_w, conf_conv_w=v_conf_conv_w,
                  conf_conv_b=v_conf_conv_b, conf_ln_w=v_conf_ln_w, conf_ln_b=v_conf_ln_b, w_out=v_w_out, norm2_w=v_norm2_w,
                  w_up=v_w_up, ffn_conv_w=v_ffn_conv_w, ffn_conv_b=v_ffn_conv_b, w_down=v_w_down, final_norm_w=v_final_norm_w)
    names = list(weights)

    def to2d(a):
        return a[0] if a.ndim == 3 else a.reshape(1, -1)

    big = ("w_in", "w_out", "w_up", "w_down")

    c_all, scw_all, ccw_all, fcw_all = exchange([c.reshape(8, LANE), ssd_conv_w[0], conf_conv_w[0], ffn_conv_w[0]],
                                                "gather_small", gather=True)
    c_all = c_all.reshape(N_DEV, D)

    ada_cols = ada_w.shape[2]
    ada_b_cols = lax.dynamic_slice(ada_b, (0, me * ada_cols), (1, ada_cols))
    mod_cols, c_act_all = ada_mod(c_all, ada_w[0], ada_b_cols, "ada_mod")
    mod_parts, = exchange([jnp.pad(mod_cols, ((0, 0), (0, D - ada_cols))).reshape(N_DEV, 8, LANE)], "scatter_mod", gather=False)
    mod = mod_parts.reshape(N_DEV, D)[:, :ada_cols].reshape(6, D)
    mod = jnp.pad(mod, ((0, 2), (0, 0)))

    shards, mod = lax.optimization_barrier(([weights[n][0].astype(BF16) for n in big], mod))
    w_in_slabs = gather_two_level(shards[0], "gather_w_in")
    later, w_in_slabs = lax.optimization_barrier((shards[1:], w_in_slabs))
    gather_parts, token = exchange_start(later, "gather_weights_start", gather=True)
    mod = mod + token[0:1, 0:1]

    small = {n: to2d(weights[n]) for n in names if n not in ("ada_w",) + big}
    small["ssd_conv_w"] = _slabs_to_cols(scw_all)
    small["conf_conv_w"] = _slabs_to_cols(ccw_all)
    small["ffn_conv_w"] = _slabs_to_cols(fcw_all)

    def with_own(landed, own):
        return lax.dynamic_update_slice(landed, own[None], (me,) + (0,) * own.ndim)

    def get_w(n, after):
        if n == "w_in":
            slabs = w_in_slabs
        else:
            a = big.index(n)
            landed, = exchange_wait([gather_parts[a - 1]], after, "gather_" + n + "_wait", gather=True)
            slabs = with_own(landed, shards[a])
        if n == "w_out":
            return slabs.reshape(2 * D, D)
        if n == "w_down":
            return slabs.reshape(D_FF, D)
        full = _slabs_to_cols(slabs)
        if n == "w_up":
            return full
        w_z, w_xbc, w_dt, w_cfa, w_cfg = full[:, :1024], full[:, 1024:2560], full[:, 2560:2576], full[:, 2576:3600], full[:, 3600:]
        return dict(w_in_p=jnp.concatenate([w_z, jnp.repeat(w_dt, HEAD, axis=1), w_cfa, w_cfg, w_xbc], axis=1), w_z=w_z,
                    w_xbc=w_xbc, w_dt16=jnp.pad(w_dt, ((0, 0), (0, LANE - N_HEADS))), w_cfa=w_cfa, w_cfg=w_cfg)

    scatter_parts, sent = {}, {}

    def put_grad(n, g):
        slabs = _cols_to_slabs(g) if n in ("w_in", "w_up") else g.reshape(N_DEV, g.shape[0] // N_DEV, g.shape[1])
        sent[n] = slabs.astype(BF16)
        (scatter_parts[n],), token = exchange_start([sent[n]], "scatter_" + n + "_start", gather=False)
        return token

    grad_x, accs = local_fwd_bwd(x[0], loss_target[0], mod, get_w, put_grad, small)
    loss = lax.psum(0.5 / D * jnp.sum(accs["acc_f"][2:3]), ("x", "y", "c"))

    grads, delta, new_m, new_v = {}, {}, {}, {}

    def finish(ns, after, name):
        landed = exchange_wait([scatter_parts[n] for n in ns], after, name, gather=False)
        for n, slots in zip(ns, landed):
            slots = with_own(slots, lax.dynamic_index_in_dim(sent[n], me, 0, keepdims=False))
            grads[n], delta[n], new_m[n], new_v[n] = adamw_slots(weights[n][0], slots, moms_m[n][0], moms_v[n][0], "adamw_" + n)

    finish(big[1:], grad_x, "scatter_grads_wait")

    order = ("acc_1", "acc_2", "acc_f", "acc_ln", "acc_s", "acc_s16", "dw_sc", "dw_cc", "dwg", "dwv")
    shapes = [accs[k].shape for k in order]
    acc_list, _ = lax.optimization_barrier(([accs[k] for k in order], [new_v[n] for n in big[1:]]))
    packed_all, = exchange([pack_rows(acc_list, "pack_small_grads")], "gather_small_grads", gather=True)
    packed_red, = sum_slots_many([packed_all], "sum_small_grads")
    gathered = dict(zip(order, unpack_rows(packed_all, shapes)))
    red = dict(zip(order, unpack_rows(packed_red, shapes)))

    def mod_rows(a1, a2, af):
        return jnp.concatenate([a1[..., 0:2, :], a2[..., 3:4, :], a2[..., 0:2, :], af[..., 1:2, :]], axis=-2)

    dmod_all = mod_rows(gathered["acc_1"], gathered["acc_2"], gathered["acc_f"]).reshape(N_DEV, 6 * D)
    grads["ada_w"] = ada_wgrad(c_act_all, lax.dynamic_slice(dmod_all, (0, me * ada_cols), (N_DEV, ada_cols)), "ada_wgrad")

    def my_cols(full, k_taps):
        cols = full.shape[1] // N_DEV
        return lax.dynamic_slice(full, (0, me * cols), (k_taps, cols))

    fcw = jnp.concatenate([red["dwg"], red["dwv"]], axis=1)
    grads.update(
        ada_b=mod_rows(red["acc_1"], red["acc_2"], red["acc_f"]).reshape(1, 6 * D), norm1_w=red["acc_1"][2:3],
        ssd_conv_w=my_cols(red["dw_sc"], K_SSD), ssd_conv_b=red["dw_sc"][K_SSD:K_SSD + 1],
        dt_bias=red["acc_s16"][1:2, :N_HEADS], a_log=red["acc_s16"][2:3, :N_HEADS], d_skip=red["acc_s16"][3:4, :N_HEADS],
        ssd_norm_w=red["acc_s"][0:1], conf_conv_w=my_cols(red["dw_cc"], K_CONF), conf_conv_b=red["dw_cc"][K_CONF:K_CONF + 1],
        conf_ln_w=red["acc_ln"][0:1], conf_ln_b=red["acc_ln"][1:2], norm2_w=red["acc_2"][2:3],
        ffn_conv_w=my_cols(fcw, K_FFN), ffn_conv_b=fcw[K_FFN:K_FFN + 1], final_norm_w=red["acc_f"][0:1])

    rest = [n for n in names if n not in big]
    d_l, m_l, v_l = adamw_many([to2d(weights[n]) for n in rest], [grads[n] for n in rest], [to2d(moms_m[n]) for n in rest],
                               [to2d(moms_v[n]) for n in rest], "adamw_small")
    for n, dd, mm, vv in zip(rest, d_l, m_l, v_l):
        delta[n], new_m[n], new_v[n] = dd, mm, vv
    finish(big[:1], d_l[0], "scatter_w_in_wait")
    shape_of = lambda d_: {n: d_[n].reshape(weights[n].shape) for n in names}
    grads, delta, new_m, new_v = shape_of(grads), shape_of(delta), shape_of(new_m), shape_of(new_v)
    return (loss, grad_x[None], *[grads[n] for n in names], *[delta[n] for n in names], *[new_m[n] for n in names],
            *[new_v[n] for n in names])
```

```python
import functools

import jax
import jax.numpy as jnp
from jax import lax
from jax.experimental import pallas as pl
from jax.experimental.pallas import tpu as pltpu

F32 = jnp.float32
BF16 = jnp.bfloat16
HI = lax.Precision.HIGHEST

N_DEV = 8
D = 1024
D_SSD = 1024
HEAD = 64
N_HEADS = 16
N_STATE = 128
D_XBC = 1536
D_CONF = 1024
D_FF = 2816
K_SSD, K_CONF, K_FFN = 4, 31, 3
D_INP = 5632
LANE = 128
TR = 256
TM = 512
Q = 256
CB = 256
TC = 1024
VMEM_LIMIT = 56 * 1024 * 1024

ADAM_LR, ADAM_B1, ADAM_B2, ADAM_EPS, ADAM_WD, ADAM_STEP = 0.001, 0.9, 0.999, 1e-08, 0.01, 10


def _cparams(sem=None):
    return pltpu.CompilerParams(vmem_limit_bytes=VMEM_LIMIT, dimension_semantics=sem)


def _sds(shape, dtype):
    return jax.ShapeDtypeStruct(shape, dtype)


def _sigmoid(x):
    return 1.0 / (1.0 + jnp.exp(-x))


def _silu(x):
    return x * _sigmoid(x)


def _dsilu(x):
    s = _sigmoid(x)
    return s * (1.0 + x * (1.0 - s))


def _softplus(x):
    return jnp.maximum(x, 0.0) + jnp.log(1.0 + jnp.exp(-jnp.abs(x)))


def _dot(a, b):
    return jnp.dot(a.astype(BF16), b.astype(BF16), preferred_element_type=F32)


def _dot_nt(a, b):
    return lax.dot_general(a.astype(BF16), b.astype(BF16), (((1,), (1,)), ((), ())), preferred_element_type=F32)


def _dot_tn(a, b):
    return lax.dot_general(a.astype(BF16), b.astype(BF16), (((0,), (0,)), ((), ())), preferred_element_type=F32)


def _dot_hi(a, b):
    return jnp.dot(a, b, precision=HI, preferred_element_type=F32)


def _bf16_terms(a, terms):
    parts, rem = [], a
    for t in range(terms):
        p = rem.astype(BF16)
        parts.append(p)
        if t + 1 < terms:
            rem = rem - p.astype(F32)
    return parts


def _dot_exact(a, b, terms, exact, dims=(((1,), (0,)), ((), ()))):
    if exact == "a":
        a_b = a.astype(BF16)
        outs = [lax.dot_general(a_b, p, dims, preferred_element_type=F32) for p in _bf16_terms(b, terms)]
    else:
        b_b = b.astype(BF16)
        outs = [lax.dot_general(p, b_b, dims, preferred_element_type=F32) for p in _bf16_terms(a, terms)]
    acc = outs[-1]
    for o in reversed(outs[:-1]):
        acc = acc + o
    return acc


def _dot_tn_hi(a, b):
    return lax.dot_general(a, b, (((0,), (0,)), ((), ())), precision=HI, preferred_element_type=F32)


def _colsum(x):
    return jnp.sum(x, axis=0, keepdims=True)


def _const_spec(shape):
    return pl.BlockSpec(shape, lambda *_: (0,) * len(shape))


def _col_tile(n):
    for t in (1408, 1024, 768, 512, 256, 128):
        if n % t == 0 and t <= n:
            return t
    return n


def mm_nn(pairs, name):
    L = pairs[0][0].shape[0]
    N = pairs[0][1].shape[1]
    tn = _col_tile(N)
    n = len(pairs)

    def body(*refs):
        acc = None
        for p in range(n):
            t = jnp.dot(refs[2 * p][...], refs[2 * p + 1][...], preferred_element_type=F32)
            acc = t if acc is None else acc + t
        refs[-1][...] = acc

    in_specs, args = [], []
    for a, w, rb in pairs:
        in_specs += [pl.BlockSpec((TM, a.shape[1]), lambda j, i: (i, 0)),
                     pl.BlockSpec((a.shape[1], tn), functools.partial(lambda j, i, rb: (rb, j), rb=rb))]
        args += [a, w]
    return pl.pallas_call(
        body, name=name, grid=(N // tn, L // TM), out_shape=_sds((L, N), F32), in_specs=in_specs,
        out_specs=pl.BlockSpec((TM, tn), lambda j, i: (i, j)),
        compiler_params=_cparams(("parallel", "parallel")))(*args)


def mm_nt(pairs, name):
    L = pairs[0][0].shape[0]
    K = pairs[0][1].shape[0]
    tk = _col_tile(K)
    n = len(pairs)

    def body(*refs):
        o_ref = refs[-1]
        acc = None
        for p in range(n):
            t = lax.dot_general(refs[2 * p][...], refs[2 * p + 1][...], (((1,), (1,)), ((), ())),
                                preferred_element_type=F32)
            acc = t if acc is None else acc + t
        o_ref[...] = acc

    in_specs, args = [], []
    for a, w, cb in pairs:
        in_specs += [pl.BlockSpec((TM, a.shape[1]), lambda j, i: (i, 0)),
                     pl.BlockSpec((tk, a.shape[1]), functools.partial(lambda j, i, cb: (j, cb), cb=cb))]
        args += [a, w]
    return pl.pallas_call(
        body, name=name, grid=(K // tk, L // TM), out_shape=_sds((L, K), F32), in_specs=in_specs,
        out_specs=pl.BlockSpec((TM, tk), lambda j, i: (i, j)),
        compiler_params=_cparams(("parallel", "parallel")))(*args)


def mm_tn(a, g, name):
    L, M = a.shape
    N = g.shape[1]
    tn = _col_tile(N) if N > 1024 else N
    if M * tn * 4 > 8 * 1024 * 1024:
        tn = 512
    tl = 512 if L % 512 == 0 else TR

    def body(a_ref, g_ref, o_ref):
        @pl.when(pl.program_id(1) == 0)
        def _():
            o_ref[...] = jnp.zeros((M, tn), F32)

        o_ref[...] += lax.dot_general(a_ref[...], g_ref[...], (((0,), (0,)), ((), ())), preferred_element_type=F32)

    return pl.pallas_call(
        body, name=name, grid=(N // tn, L // tl), out_shape=_sds((M, N), F32),
        in_specs=[pl.BlockSpec((tl, M), lambda j, l: (l, 0)), pl.BlockSpec((tl, tn), lambda j, l: (l, j))],
        out_specs=pl.BlockSpec((M, tn), lambda j, l: (0, j)),
        compiler_params=_cparams(("parallel", "arbitrary")))(a, g)


def _row_spec(width=D):
    return pl.BlockSpec((TR, width), lambda i: (i, 0))


def _row_col_spec(width, col):
    return pl.BlockSpec((TR, width), lambda i: (i, col))


def norm_mod(x, mod, w, shift_row, name, res=None, gate_row=None):
    L = x.shape[0]
    has_res = res is not None

    def body(*refs):
        if has_res:
            x_ref, res_ref, mod_ref, w_ref, h_ref, xo_ref = refs
            xin = x_ref[...] + mod_ref[gate_row:gate_row + 1, :] * res_ref[...]
            xo_ref[...] = xin
        else:
            x_ref, mod_ref, w_ref, h_ref = refs
            xin = x_ref[...]
        r = lax.rsqrt(jnp.mean(xin * xin, axis=-1, keepdims=True) + 1e-6)
        h = (xin * r * w_ref[...]) * (1.0 + mod_ref[shift_row + 1:shift_row + 2, :]) + mod_ref[shift_row:shift_row + 1, :]
        h_ref[...] = h.astype(BF16)

    ins = [x] + ([res] if has_res else []) + [mod, w]
    in_specs = [_row_spec()] + ([_row_spec()] if has_res else []) + [_const_spec((8, D)), _const_spec((1, D))]
    out_shape = [_sds((L, D), BF16)] + ([_sds((L, D), F32)] if has_res else [])
    out_specs = [_row_spec()] + ([_row_spec()] if has_res else [])
    out = pl.pallas_call(body, name=name, grid=(L // TR,), out_shape=out_shape, in_specs=in_specs,
                         out_specs=out_specs, compiler_params=_cparams(("parallel",)))(*ins)
    return out if has_res else out[0]


def ln_silu(uc, lnw, lnb, name):
    L = uc.shape[0]

    def body(u_ref, w_ref, b_ref, o_ref):
        u = u_ref[...]
        mu = jnp.mean(u, axis=-1, keepdims=True)
        var = jnp.mean(jnp.square(u - mu), axis=-1, keepdims=True)
        v = (u - mu) * lax.rsqrt(var + 1e-5) * w_ref[...] + b_ref[...]
        o_ref[...] = _silu(v).astype(BF16)

    return pl.pallas_call(body, name=name, grid=(L // TR,), out_shape=_sds((L, D_CONF), BF16),
                          in_specs=[_row_spec(), _const_spec((1, D)), _const_spec((1, D))], out_specs=_row_spec(),
                          compiler_params=_cparams(("parallel",)))(uc, lnw, lnb)


def ln_silu_bwd(du_all, uc, lnw, lnb, name):
    L = uc.shape[0]

    def body(du_ref, u_ref, w_ref, b_ref, o_ref, acc_ref):
        @pl.when(pl.program_id(0) == 0)
        def _():
            acc_ref[...] = jnp.zeros((8, D), F32)

        u = u_ref[...]
        mu = jnp.mean(u, axis=-1, keepdims=True)
        rl = lax.rsqrt(jnp.mean(jnp.square(u - mu), axis=-1, keepdims=True) + 1e-5)
        n = (u - mu) * rl
        v = n * w_ref[...] + b_ref[...]
        dv = du_ref[...] * _dsilu(v)
        acc_ref[0:1, :] += _colsum(dv * n)
        acc_ref[1:2, :] += _colsum(dv)
        dn = dv * w_ref[...]
        o_ref[...] = rl * (dn - jnp.mean(dn, axis=-1, keepdims=True) - n * jnp.mean(dn * n, axis=-1, keepdims=True))

    return pl.pallas_call(body, name=name, grid=(L // TR,), out_shape=[_sds((L, D), F32), _sds((8, D), F32)],
                          in_specs=[_row_col_spec(D, 1), _row_spec(), _const_spec((1, D)), _const_spec((1, D))],
                          out_specs=[_row_spec(), _const_spec((8, D))],
                          compiler_params=_cparams(("arbitrary",)))(du_all, uc, lnw, lnb)


def final_loss(ff, x1, mod, fw, target, name):
    L = ff.shape[0]

    def body(ff_ref, x1_ref, mod_ref, fw_ref, t_ref, dx_ref, dff_ref, acc_ref):
        @pl.when(pl.program_id(0) == 0)
        def _():
            acc_ref[...] = jnp.zeros((8, D), F32)

        ff_v = ff_ref[...]
        g2 = mod_ref[5:6, :]
        x2 = x1_ref[...] + g2 * ff_v
        r = lax.rsqrt(jnp.mean(x2 * x2, axis=-1, keepdims=True) + 1e-6)
        n = x2 * r
        err = n * fw_ref[...] - t_ref[...]
        dy = err * (1.0 / D)
        dn = dy * fw_ref[...]
        dx2 = r * (dn - n * jnp.mean(dn * n, axis=-1, keepdims=True))
        acc_ref[0:1, :] += _colsum(dy * n)
        acc_ref[1:2, :] += _colsum(dx2 * ff_v)
        acc_ref[2:3, :] += _colsum(err * err)
        dx_ref[...] = dx2
        dff_ref[...] = (dx2 * g2).astype(BF16)

    return pl.pallas_call(
        body, name=name, grid=(L // TR,), out_shape=[_sds((L, D), F32), _sds((L, D), BF16), _sds((8, D), F32)],
        in_specs=[_row_spec(), _row_spec(), _const_spec((8, D)), _const_spec((1, D)), _row_spec()],
        out_specs=[_row_spec(), _row_spec(), _const_spec((8, D))],
        compiler_params=_cparams(("arbitrary",)))(ff, x1, mod, fw, target)


def norm_mod_bwd(dh, xin, dres, mod, w, shift_row, name, mix=None, gate_row=None):
    L = dh.shape[0]
    has_mix = mix is not None

    def body(*refs):
        if has_mix:
            dh_ref, x_ref, dres_ref, mod_ref, w_ref, mix_ref, dx_ref, dmix_ref, acc_ref = refs
        else:
            dh_ref, x_ref, dres_ref, mod_ref, w_ref, dx_ref, acc_ref = refs

        @pl.when(pl.program_id(0) == 0)
        def _():
            acc_ref[...] = jnp.zeros((8, D), F32)

        dh_v = dh_ref[...]
        x = x_ref[...]
        r = lax.rsqrt(jnp.mean(x * x, axis=-1, keepdims=True) + 1e-6)
        n = x * r
        nw = n * w_ref[...]
        sc1 = 1.0 + mod_ref[shift_row + 1:shift_row + 2, :]
        acc_ref[0:1, :] += _colsum(dh_v)
        acc_ref[1:2, :] += _colsum(dh_v * nw)
        dnw = dh_v * sc1
        acc_ref[2:3, :] += _colsum(dnw * n)
        dn = dnw * w_ref[...]
        dx = r * (dn - n * jnp.mean(dn * n, axis=-1, keepdims=True)) + dres_ref[...]
        dx_ref[...] = dx
        if has_mix:
            acc_ref[3:4, :] += _colsum(dx * mix_ref[...])
            dmix_ref[...] = (dx * mod_ref[gate_row:gate_row + 1, :]).astype(BF16)

    ins = [dh, xin, dres, mod, w] + ([mix] if has_mix else [])
    in_specs = [_row_spec(), _row_spec(), _row_spec(), _const_spec((8, D)), _const_spec((1, D))] + ([_row_spec()] if has_mix else [])
    out_shape = [_sds((L, D), F32)] + ([_sds((L, D), BF16)] if has_mix else []) + [_sds((8, D), F32)]
    out_specs = [_row_spec()] + ([_row_spec()] if has_mix else []) + [_const_spec((8, D))]
    return pl.pallas_call(body, name=name, grid=(L // TR,), out_shape=out_shape, in_specs=in_specs,
                          out_specs=out_specs, compiler_params=_cparams(("arbitrary",)))(*ins)


def _halo(k):
    return 8 if k <= 9 else 32


def _prev_spec(h, col0):
    return pl.BlockSpec((h, CB), lambda j, i: (jnp.maximum(i * (TC // h) - 1, 0), j + col0))


def _next_spec(h, col0, n_tiles):
    return pl.BlockSpec((h, CB), lambda j, i: (jnp.minimum(i + 1, n_tiles - 1) * (TC // h), j + col0))


def _tile_spec(col0):
    return pl.BlockSpec((TC, CB), lambda j, i: (i, j + col0))


def _w_spec(kp, col0):
    return pl.BlockSpec((kp, CB), lambda j, i: (0, j + col0))


SUBLANES = 8


def _shifted_windows(v, taps, rows):
    for r in range(SUBLANES):
        group = [(o, k) for o, k in taps if o % SUBLANES == r]
        if not group:
            continue
        s = v if r == 0 else pltpu.roll(v, v.shape[0] - r, 0)
        for o, k in group:
            yield k, s[o - r:o - r + rows, :]


def _causal_taps(ext_ref, w_ref, k_taps, first, rows):
    acc = None
    for k, win in _shifted_windows(ext_ref[...], [(first - (k_taps - 1) + k, k) for k in range(k_taps)], rows):
        t = w_ref[k:k + 1, :] * win
        acc = t if acc is None else acc + t
    return acc


def _anticausal_taps(d_ref, w_ref, k_taps, rows):
    acc = None
    for k, win in _shifted_windows(d_ref[...], [(k_taps - 1 - k, k) for k in range(k_taps)], rows):
        t = w_ref[k:k + 1, :] * win
        acc = t if acc is None else acc + t
    return acc


def _acc_conv_wgrad(dw_ref, d_tile, ext_ref, k_taps, first):
    for k, win in _shifted_windows(ext_ref[...], [(first - (k_taps - 1) + k, k) for k in range(k_taps)], TC):
        dw_ref[k:k + 1, :] += _colsum(d_tile * win)
    dw_ref[k_taps:k_taps + 1, :] += _colsum(d_tile)


def conv_silu_fwd(x, col0, width, w, b, name):
    L = x.shape[0]
    k_taps = w.shape[0]
    h = _halo(k_taps)

    def body(xp_ref, x_ref, w_ref, b_ref, o_ref, ext_ref):
        i = pl.program_id(1)
        ext_ref[0:h, :] = jnp.where(i > 0, xp_ref[...], 0.0)
        ext_ref[h:h + TC, :] = x_ref[...]
        o_ref[...] = _silu(_causal_taps(ext_ref, w_ref, k_taps, h, TC) + b_ref[...])

    return pl.pallas_call(
        body, name=name, grid=(width // CB, L // TC), out_shape=_sds((L, width), F32),
        in_specs=[_prev_spec(h, col0), _tile_spec(col0), _w_spec(k_taps, 0), pl.BlockSpec((1, CB), lambda j, i: (0, j))],
        out_specs=_tile_spec(0), scratch_shapes=[pltpu.VMEM((h + TC, CB), F32)],
        compiler_params=_cparams(("parallel", "parallel")))(x, x, w, b)


def conv_silu_bwd(x, col0, width, w, b, dpost, name):
    L = x.shape[0]
    k_taps = w.shape[0]
    h = _halo(k_taps)
    nt = L // TC

    def body(xp_ref, x_ref, xn_ref, d_ref, dn_ref, w_ref, b_ref, dx_ref, dw_ref, ext_ref, dpre_ref):
        i = pl.program_id(1)

        @pl.when(i == 0)
        def _():
            dw_ref[...] = jnp.zeros((8, CB), F32)

        ext_ref[0:h, :] = jnp.where(i > 0, xp_ref[...], 0.0)
        ext_ref[h:h + TC, :] = x_ref[...]
        ext_ref[h + TC:h + TC + h, :] = xn_ref[...]
        pre = _causal_taps(ext_ref, w_ref, k_taps, h, TC + h) + b_ref[...]
        dpre_ref[0:TC, :] = d_ref[...] * _dsilu(pre[0:TC, :])
        dpre_ref[TC:TC + h, :] = jnp.where(i < nt - 1, dn_ref[...], 0.0) * _dsilu(pre[TC:TC + h, :])
        dx_ref[...] = _anticausal_taps(dpre_ref, w_ref, k_taps, TC).astype(BF16)
        _acc_conv_wgrad(dw_ref, dpre_ref[0:TC, :], ext_ref, k_taps, h)

    return pl.pallas_call(
        body, name=name, grid=(width // CB, nt),
        out_shape=[_sds((L, width), BF16), _sds((8, width), F32)],
        in_specs=[_prev_spec(h, col0), _tile_spec(col0), _next_spec(h, col0, nt), _tile_spec(0), _next_spec(h, 0, nt),
                  _w_spec(k_taps, 0), pl.BlockSpec((1, CB), lambda j, i: (0, j))],
        out_specs=[_tile_spec(0), _w_spec(8, 0)],
        scratch_shapes=[pltpu.VMEM((h + TC + h, CB), F32), pltpu.VMEM((TC + h, CB), F32)],
        compiler_params=_cparams(("parallel", "arbitrary")))(x, x, x, dpost, dpost, w, b)


def conf_conv_fwd(proj, col_a, col_g, w, b, name):
    L = proj.shape[0]
    k_taps = w.shape[0]
    h = _halo(k_taps)

    def body(ap_ref, a_ref, gp_ref, g_ref, w_ref, b_ref, o_ref, ext_ref):
        i = pl.program_id(1)
        ext_ref[0:h, :] = jnp.where(i > 0, ap_ref[...] * _sigmoid(gp_ref[...]), 0.0)
        ext_ref[h:h + TC, :] = a_ref[...] * _sigmoid(g_ref[...])
        o_ref[...] = _causal_taps(ext_ref, w_ref, k_taps, h, TC) + b_ref[...]

    return pl.pallas_call(
        body, name=name, grid=(D_CONF // CB, L // TC), out_shape=_sds((L, D_CONF), F32),
        in_specs=[_prev_spec(h, col_a), _tile_spec(col_a), _prev_spec(h, col_g), _tile_spec(col_g), _w_spec(k_taps, 0),
                  pl.BlockSpec((1, CB), lambda j, i: (0, j))],
        out_specs=_tile_spec(0), scratch_shapes=[pltpu.VMEM((h + TC, CB), F32)],
        compiler_params=_cparams(("parallel", "parallel")))(proj, proj, proj, proj, w, b)


def conf_conv_bwd(proj, col_a, col_g, w, duc, name):
    L = proj.shape[0]
    k_taps = w.shape[0]
    h = _halo(k_taps)
    nt = L // TC

    def body(ap_ref, a_ref, gp_ref, g_ref, d_ref, dn_ref, w_ref, da_ref, dg_ref, dw_ref, ext_ref, dext_ref):
        i = pl.program_id(1)

        @pl.when(i == 0)
        def _():
            dw_ref[...] = jnp.zeros((32, CB), F32)

        a = a_ref[...]
        s = _sigmoid(g_ref[...])
        ext_ref[0:h, :] = jnp.where(i > 0, ap_ref[...] * _sigmoid(gp_ref[...]), 0.0)
        ext_ref[h:h + TC, :] = a * s
        dext_ref[0:TC, :] = d_ref[...]
        dext_ref[TC:TC + h, :] = jnp.where(i < nt - 1, dn_ref[...], 0.0)
        du0 = _anticausal_taps(dext_ref, w_ref, k_taps, TC)
        da_ref[...] = (du0 * s).astype(BF16)
        dg_ref[...] = (du0 * a * s * (1.0 - s)).astype(BF16)
        _acc_conv_wgrad(dw_ref, d_ref[...], ext_ref, k_taps, h)

    return pl.pallas_call(
        body, name=name, grid=(D_CONF // CB, nt),
        out_shape=[_sds((L, D_CONF), BF16), _sds((L, D_CONF), BF16), _sds((32, D_CONF), F32)],
        in_specs=[_prev_spec(h, col_a), _tile_spec(col_a), _prev_spec(h, col_g), _tile_spec(col_g), _tile_spec(0),
                  _next_spec(h, 0, nt), _w_spec(k_taps, 0)],
        out_specs=[_tile_spec(0), _tile_spec(0), _w_spec(32, 0)],
        scratch_shapes=[pltpu.VMEM((h + TC, CB), F32), pltpu.VMEM((TC + h, CB), F32)],
        compiler_params=_cparams(("parallel", "arbitrary")))(proj, proj, proj, proj, duc, duc, w)


def ffn_conv_fwd(up, w, b, name):
    L = up.shape[0]
    k_taps = w.shape[0]
    h = _halo(k_taps)
    cv = D_FF // CB

    def body(gp_ref, g_ref, vp_ref, v_ref, wg_ref, wv_ref, bg_ref, bv_ref, o_ref, eg_ref, ev_ref):
        i = pl.program_id(1)
        eg_ref[0:h, :] = jnp.where(i > 0, gp_ref[...], 0.0)
        eg_ref[h:h + TC, :] = g_ref[...]
        ev_ref[0:h, :] = jnp.where(i > 0, vp_ref[...], 0.0)
        ev_ref[h:h + TC, :] = v_ref[...]
        pg = _causal_taps(eg_ref, wg_ref, k_taps, h, TC) + bg_ref[...]
        pv = _causal_taps(ev_ref, wv_ref, k_taps, h, TC) + bv_ref[...]
        o_ref[...] = (_silu(pg) * pv).astype(BF16)

    bspec = lambda c0: pl.BlockSpec((1, CB), lambda j, i: (0, j + c0))
    return pl.pallas_call(
        body, name=name, grid=(cv, L // TC), out_shape=_sds((L, D_FF), BF16),
        in_specs=[_prev_spec(h, 0), _tile_spec(0), _prev_spec(h, cv), _tile_spec(cv), _w_spec(k_taps, 0), _w_spec(k_taps, cv),
                  bspec(0), bspec(cv)],
        out_specs=_tile_spec(0), scratch_shapes=[pltpu.VMEM((h + TC, CB), F32), pltpu.VMEM((h + TC, CB), F32)],
        compiler_params=_cparams(("parallel", "parallel")))(up, up, up, up, w, w, b, b)


def ffn_conv_bwd(up, w, b, dact, name):
    L = up.shape[0]
    k_taps = w.shape[0]
    h = _halo(k_taps)
    nt = L // TC
    cv = D_FF // CB

    def body(gp_ref, g_ref, gn_ref, vp_ref, v_ref, vn_ref, d_ref, dn_ref, wg_ref, wv_ref, bg_ref, bv_ref,
             dg_ref, dv_ref, dwg_ref, dwv_ref, eg_ref, ev_ref, pg_ref, pv_ref):
        i = pl.program_id(1)

        @pl.when(i == 0)
        def _():
            dwg_ref[...] = jnp.zeros((8, CB), F32)
            dwv_ref[...] = jnp.zeros((8, CB), F32)

        for e_ref, p_ref, c_ref, n_ref in ((eg_ref, gp_ref, g_ref, gn_ref), (ev_ref, vp_ref, v_ref, vn_ref)):
            e_ref[0:h, :] = jnp.where(i > 0, p_ref[...], 0.0)
            e_ref[h:h + TC, :] = c_ref[...]
            e_ref[h + TC:h + TC + h, :] = n_ref[...]
        pg = _causal_taps(eg_ref, wg_ref, k_taps, h, TC + h) + bg_ref[...]
        pv = _causal_taps(ev_ref, wv_ref, k_taps, h, TC + h) + bv_ref[...]
        dact_t = d_ref[...]
        dact_n = jnp.where(i < nt - 1, dn_ref[...], 0.0)
        pg_ref[0:TC, :] = dact_t * pv[0:TC, :] * _dsilu(pg[0:TC, :])
        pg_ref[TC:TC + h, :] = dact_n * pv[TC:TC + h, :] * _dsilu(pg[TC:TC + h, :])
        pv_ref[0:TC, :] = dact_t * _silu(pg[0:TC, :])
        pv_ref[TC:TC + h, :] = dact_n * _silu(pg[TC:TC + h, :])
        dg_ref[...] = _anticausal_taps(pg_ref, wg_ref, k_taps, TC).astype(BF16)
        dv_ref[...] = _anticausal_taps(pv_ref, wv_ref, k_taps, TC).astype(BF16)
        _acc_conv_wgrad(dwg_ref, pg_ref[0:TC, :], eg_ref, k_taps, h)
        _acc_conv_wgrad(dwv_ref, pv_ref[0:TC, :], ev_ref, k_taps, h)

    bspec = lambda c0: pl.BlockSpec((1, CB), lambda j, i: (0, j + c0))
    ext = pltpu.VMEM((h + TC + h, CB), F32)
    dpre = pltpu.VMEM((TC + h, CB), F32)
    return pl.pallas_call(
        body, name=name, grid=(cv, nt),
        out_shape=[_sds((L, D_FF), BF16), _sds((L, D_FF), BF16), _sds((8, D_FF), F32), _sds((8, D_FF), F32)],
        in_specs=[_prev_spec(h, 0), _tile_spec(0), _next_spec(h, 0, nt), _prev_spec(h, cv), _tile_spec(cv), _next_spec(h, cv, nt),
                  _tile_spec(0), _next_spec(h, 0, nt), _w_spec(k_taps, 0), _w_spec(k_taps, cv), bspec(0), bspec(cv)],
        out_specs=[_tile_spec(0), _tile_spec(0), _w_spec(8, 0), _w_spec(8, 0)],
        scratch_shapes=[ext, ext, dpre, dpre],
        compiler_params=_cparams(("parallel", "arbitrary")))(up, up, up, up, up, up, dact, dact, w, w, b, b)


def _ssd_common(xbc_ref, dt_ref, dtb_ref, alog_ref, cs_ref):
    xs = xbc_ref[:, 0:D_SSD]
    sp_in = dt_ref[...] + dtb_ref[...]
    dtf = _softplus(sp_in)
    a_f = -jnp.exp(alog_ref[...])
    a_dt = dtf * a_f
    row = lax.broadcasted_iota(jnp.int32, (Q, Q), 0)
    col = lax.broadcasted_iota(jnp.int32, (Q, Q), 1)
    causal = row >= col
    cs = _dot_exact(causal.astype(F32), a_dt, 3, "a")
    cs_ref[...] = cs
    cs_last = cs_ref[Q - 1:Q, :]
    return xs, sp_in, dtf, a_f, cs, cs_last, causal


def _head_decay(cs_j, cst_ref, e, causal):
    lane = lax.broadcasted_iota(jnp.int32, (Q, LANE), 1)
    rolled = pltpu.roll(cs_j, HEAD, 1)
    own = (lane < HEAD) if e == 0 else (lane >= HEAD)
    col_b = jnp.where(own, cs_j, rolled)
    col_b = jnp.concatenate([col_b] * (Q // LANE), axis=1)
    row_b = cst_ref[e * HEAD:e * HEAD + 1, :]
    return jnp.where(causal, jnp.exp(jnp.minimum(col_b - row_b, 0.0)), 0.0)


def ssd_fwd(xbc, proj, dtb_f, alog_f, dsk_f, snw, name):
    L = xbc.shape[0]
    nc = L // Q

    def body(xbc_ref, z_ref, dt_ref, dtb_ref, alog_ref, dsk_ref, snw_ref, y_ref, yn_ref, sp_ref, s_ref, cs_ref, cst_ref, yd_ref):
        @pl.when(pl.program_id(0) == 0)
        def _():
            s_ref[...] = jnp.zeros((N_STATE, D_SSD), F32)

        xs, _, dtf, a_f, cs, cs_last, causal = _ssd_common(xbc_ref, dt_ref, dtb_ref, alog_ref, cs_ref)
        e_cs = jnp.exp(cs)
        xdt = xs * dtf
        zst = jnp.exp(cs_last - cs) * xdt
        sp_ref[0] = s_ref[...]
        lane = lax.broadcasted_iota(jnp.int32, (Q, LANE), 1)
        for g in range(2):
            gl = slice(g * 512, g * 512 + 512)
            b_g = xbc_ref[:, D_SSD + g * N_STATE:D_SSD + (g + 1) * N_STATE]
            c_g = xbc_ref[:, D_SSD + 2 * N_STATE + g * N_STATE:D_SSD + 2 * N_STATE + (g + 1) * N_STATE]
            s_prev = s_ref[:, gl]
            cb = _dot_nt(c_g, b_g)
            yd_ref[:, gl] = e_cs[:, gl] * _dot(c_g, s_prev)
            for j in range(4):
                tl = slice(g * 512 + j * LANE, g * 512 + (j + 1) * LANE)
                cs_j = cs[:, tl]
                cst_ref[...] = cs_j.T
                x_j = xdt[:, tl]
                o0 = _dot(cb * _head_decay(cs_j, cst_ref, 0, causal), x_j)
                o1 = _dot(cb * _head_decay(cs_j, cst_ref, 1, causal), x_j)
                yd_ref[:, tl] += jnp.where(lane < HEAD, o0, o1)
            s_ref[:, gl] = jnp.exp(cs_last[:, gl]) * s_prev + _dot_tn(b_g, zst[:, gl])
        y = yd_ref[...] + xs * dsk_ref[...]
        y_ref[...] = y
        yz = y * _silu(z_ref[...])
        r = lax.rsqrt(jnp.mean(yz * yz, axis=-1, keepdims=True) + 1e-6)
        yn_ref[...] = (yz * r * snw_ref[...]).astype(BF16)

    chunk = lambda w, c: pl.BlockSpec((Q, w), lambda i: (i, c))
    return pl.pallas_call(
        body, name=name, grid=(nc,),
        out_shape=[_sds((L, D_SSD), F32), _sds((L, D_SSD), BF16), _sds((nc, N_STATE, D_SSD), F32)],
        in_specs=[chunk(D_XBC, 0), chunk(D, 0), chunk(D, 1)] + [_const_spec((1, D))] * 4,
        out_specs=[chunk(D, 0), chunk(D, 0), pl.BlockSpec((1, N_STATE, D_SSD), lambda i: (i, 0, 0))],
        scratch_shapes=[pltpu.VMEM((N_STATE, D_SSD), F32), pltpu.VMEM((Q, D_SSD), F32), pltpu.VMEM((LANE, Q), F32),
                        pltpu.VMEM((Q, D_SSD), F32)],
        compiler_params=_cparams(("arbitrary",)))(xbc, proj, proj, dtb_f, alog_f, dsk_f, snw)


def ssd_bwd(dmixin, y, xbc, proj, s_prev_all, dtb_f, alog_f, dsk_f, snw, name):
    L = xbc.shape[0]
    nc = L // Q

    def body(dyn_ref, y_ref, xbc_ref, z_ref, dt_ref, sp_ref, dtb_ref, alog_ref, dsk_ref, snw_ref,
             dz_ref, ddt_ref, dxbc_ref, acc_ref, acc16_ref, ds_ref, cs_ref, cst_ref, dcs_ref, dx_ref):
        step = pl.program_id(0)

        @pl.when(step == 0)
        def _():
            ds_ref[...] = jnp.zeros((N_STATE, D_SSD), F32)
            acc_ref[...] = jnp.zeros((8, D), F32)

        z = z_ref[...]
        y = y_ref[...]
        sz = _sigmoid(z)
        siluz = z * sz
        yz = y * siluz
        r = lax.rsqrt(jnp.mean(yz * yz, axis=-1, keepdims=True) + 1e-6)
        n = yz * r
        dyn = dyn_ref[...]
        acc_ref[0:1, :] += _colsum(dyn * n)
        dn = dyn * snw_ref[...]
        dyz = r * (dn - n * jnp.mean(dn * n, axis=-1, keepdims=True))
        dy = dyz * siluz
        dz_ref[...] = (dyz * y * (sz * (1.0 + z * (1.0 - sz)))).astype(BF16)

        xs, sp_in, dtf, a_f, cs, cs_last, causal = _ssd_common(xbc_ref, dt_ref, dtb_ref, alog_ref, cs_ref)
        acc_ref[3:4, :] += _colsum(dy * xs)
        e_cs = jnp.exp(cs)
        xdt = xs * dtf
        dst = jnp.exp(cs_last - cs)
        zst = dst * xdt
        e_last = jnp.exp(cs_last)
        lane = lax.broadcasted_iota(jnp.int32, (Q, LANE), 1)
        ones = jnp.ones((Q, LANE), F32)
        dcs_last_parts = []
        for g in range(2):
            gl = slice(g * 512, g * 512 + 512)
            b_g = xbc_ref[:, D_SSD + g * N_STATE:D_SSD + (g + 1) * N_STATE]
            c_g = xbc_ref[:, D_SSD + 2 * N_STATE + g * N_STATE:D_SSD + 2 * N_STATE + (g + 1) * N_STATE]
            s_prev = sp_ref[0, :, gl]
            ds_g = ds_ref[:, gl]
            dy_g = dy[:, gl]
            cb = _dot_nt(c_g, b_g)
            y_off = e_cs[:, gl] * _dot(c_g, s_prev)
            edy = e_cs[:, gl] * dy_g
            d_c = _dot_nt(edy, s_prev)
            d_z = _dot(b_g, ds_g)
            d_b = _dot_nt(zst[:, gl], ds_g)
            t_g = d_z * zst[:, gl]
            dcs_ref[:, gl] = dy_g * y_off - t_g
            dx_ref[:, gl] = d_z * dst[:, gl]
            dcs_last_parts.append(_colsum(t_g) + _colsum(ds_g * s_prev) * e_last[:, gl])
            ds_ref[:, gl] = e_last[:, gl] * ds_g + _dot_tn(c_g, edy)
            dcb = jnp.zeros((Q, Q), F32)
            for j in range(4):
                tl = slice(g * 512 + j * LANE, g * 512 + (j + 1) * LANE)
                cs_j = cs[:, tl]
                cst_ref[...] = cs_j.T
                x_j = xdt[:, tl]
                dy_j = dy[:, tl]
                dx_j = jnp.zeros((Q, LANE), F32)
                dcs_j = jnp.zeros((Q, LANE), F32)
                for e in range(2):
                    own = (lane < HEAD) if e == 0 else (lane >= HEAD)
                    w_h = _head_decay(cs_j, cst_ref, e, causal)
                    g_h = cb * w_h
                    dy_m = jnp.where(own, dy_j, 0.0)
                    d_g = _dot_nt(dy_m, x_j)
                    dx_j = dx_j + _dot_tn(g_h, dy_m)
                    dcb = dcb + d_g * w_h
                    p_h = d_g * g_h
                    row_sums = _dot_exact(p_h, ones, 2, "b")
                    col_sums = _dot_exact(p_h, ones, 2, "b", (((0,), (0,)), ((), ())))
                    dcs_j = dcs_j + jnp.where(own, row_sums - col_sums, 0.0)
                dcs_ref[:, tl] += dcs_j * (1.0 / HEAD)
                dx_ref[:, tl] += dx_j
            d_c = d_c + _dot(dcb, b_g)
            d_b = d_b + _dot_tn(dcb, c_g)
            dxbc_ref[:, D_SSD + g * N_STATE:D_SSD + (g + 1) * N_STATE] = d_b
            dxbc_ref[:, D_SSD + 2 * N_STATE + g * N_STATE:D_SSD + 2 * N_STATE + (g + 1) * N_STATE] = d_c
        dcs_last = jnp.concatenate(dcs_last_parts, axis=1)
        anticausal = lax.broadcasted_iota(jnp.int32, (Q, Q), 0) <= lax.broadcasted_iota(jnp.int32, (Q, Q), 1)
        d_adt = _dot_exact(anticausal.astype(F32), dcs_ref[...], 3, "a") + dcs_last
        dx = dx_ref[...]
        acc_ref[2:3, :] += _colsum(d_adt * dtf) * a_f
        d_dtf = d_adt * a_f + dx * xs
        dxbc_ref[:, 0:D_SSD] = dx * dtf + dy * dsk_ref[...]
        d_raw = d_dtf * _sigmoid(sp_in)
        acc_ref[1:2, :] += _colsum(d_raw)
        head_of_lane = lax.broadcasted_iota(jnp.int32, (D_SSD, LANE), 0) // HEAD
        fold = (head_of_lane == lax.broadcasted_iota(jnp.int32, (D_SSD, LANE), 1)).astype(F32)
        ddt_ref[...] = _dot_exact(d_raw, fold, 2, "b").astype(BF16)

        @pl.when(step == nc - 1)
        def _():
            acc16_ref[...] = _dot_exact(acc_ref[...], fold, 3, "b")

    rchunk = lambda w, c: pl.BlockSpec((Q, w), lambda i: (nc - 1 - i, c))
    return pl.pallas_call(
        body, name=name, grid=(nc,),
        out_shape=[_sds((L, D_SSD), BF16), _sds((L, LANE), BF16), _sds((L, D_XBC), F32), _sds((8, D), F32), _sds((8, LANE), F32)],
        in_specs=[rchunk(D, 0), rchunk(D, 0), rchunk(D_XBC, 0), rchunk(D, 0), rchunk(D, 1),
                  pl.BlockSpec((1, N_STATE, D_SSD), lambda i: (nc - 1 - i, 0, 0))] + [_const_spec((1, D))] * 4,
        out_specs=[rchunk(D, 0), rchunk(LANE, 0), rchunk(D_XBC, 0), _const_spec((8, D)), _const_spec((8, LANE))],
        scratch_shapes=[pltpu.VMEM((N_STATE, D_SSD), F32), pltpu.VMEM((Q, D_SSD), F32), pltpu.VMEM((LANE, Q), F32),
                        pltpu.VMEM((Q, D_SSD), F32), pltpu.VMEM((Q, D_SSD), F32)],
        compiler_params=_cparams(("arbitrary",)))(dmixin, y, xbc, proj, proj, s_prev_all, dtb_f, alog_f, dsk_f, snw)


def _adamw_math(w, g, m, v):
    m_n = ADAM_B1 * m + (1.0 - ADAM_B1) * g
    v_n = ADAM_B2 * v + (1.0 - ADAM_B2) * jnp.square(g)
    c1 = 1.0 - ADAM_B1 ** ADAM_STEP
    c2 = 1.0 - ADAM_B2 ** ADAM_STEP
    return -ADAM_LR * ((m_n / c1) / (jnp.sqrt(v_n / c2) + ADAM_EPS) + ADAM_WD * w), m_n, v_n


def _sum_slots(p_ref):
    acc = p_ref[0].astype(F32)
    for s in range(1, N_DEV):
        acc = acc + p_ref[s].astype(F32)
    return acc


def adamw_slots(w, slots, m, v, name):
    rows, cols = w.shape
    tc = 256

    def body(w_ref, s_ref, m_ref, v_ref, g_ref, d_ref, mo_ref, vo_ref):
        g_v = _sum_slots(s_ref)
        g_ref[...] = g_v
        d_ref[...], mo_ref[...], vo_ref[...] = _adamw_math(w_ref[...], g_v, m_ref[...], v_ref[...])

    spec = pl.BlockSpec((rows, tc), lambda i: (0, i))
    return pl.pallas_call(body, name=name, grid=(cols // tc,), out_shape=[_sds((rows, cols), F32)] * 4,
                          in_specs=[spec, pl.BlockSpec((N_DEV, rows, tc), lambda i: (0, 0, i)), spec, spec], out_specs=[spec] * 4,
                          compiler_params=_cparams(("parallel",)))(w, slots, m, v)


def adamw_many(ws, gs, ms, vs, name):
    n = len(ws)

    def body(*refs):
        for p in range(n):
            d_v, m_v, v_v = _adamw_math(refs[p][...], refs[n + p][...], refs[2 * n + p][...], refs[3 * n + p][...])
            refs[4 * n + p][...] = d_v
            refs[5 * n + p][...] = m_v
            refs[6 * n + p][...] = v_v

    vm = pl.BlockSpec(memory_space=pltpu.VMEM)
    out = pl.pallas_call(body, name=name, out_shape=[_sds(w.shape, F32) for w in ws] * 3, in_specs=[vm] * (4 * n),
                         out_specs=[vm] * (3 * n), compiler_params=_cparams())(*ws, *gs, *ms, *vs)
    return out[:n], out[n:2 * n], out[2 * n:]


def _pack_layout(shapes):
    row, layout = 0, []
    for rows, cols in shapes:
        chunks = []
        for c0 in range(0, cols, D):
            chunks.append((row, c0, min(D, cols - c0)))
            row += rows
        layout.append(chunks)
    return row, layout


def pack_rows(arrays, name):
    total, layout = _pack_layout([a.shape for a in arrays])
    n = len(arrays)

    def body(*refs):
        o_ref = refs[n]
        o_ref[...] = jnp.zeros((total, D), F32)
        for p in range(n):
            rows = arrays[p].shape[0]
            for r0, c0, w in layout[p]:
                o_ref[r0:r0 + rows, 0:w] = refs[p][:, c0:c0 + w]

    vm = pl.BlockSpec(memory_space=pltpu.VMEM)
    return pl.pallas_call(body, name=name, out_shape=_sds((total, D), F32), in_specs=[vm] * n, out_specs=vm,
                          compiler_params=_cparams())(*arrays)


def unpack_rows(packed, shapes):
    _, layout = _pack_layout(shapes)
    out = []
    for (rows, _), chunks in zip(shapes, layout):
        parts = [packed[..., r0:r0 + rows, 0:w] for r0, _, w in chunks]
        out.append(parts[0] if len(parts) == 1 else jnp.concatenate(parts, axis=-1))
    return out


def sum_slots_many(parts, name):
    n = len(parts)

    def body(*refs):
        for p in range(n):
            refs[n + p][...] = _sum_slots(refs[p])

    vm = pl.BlockSpec(memory_space=pltpu.VMEM)
    return pl.pallas_call(body, name=name, out_shape=[_sds(p.shape[1:], F32) for p in parts], in_specs=[vm] * n,
                          out_specs=[vm] * n, compiler_params=_cparams())(*parts)


def ada_mod(c_all, ada_w_shard, ada_b_cols, name):
    def body(c_ref, w_ref, b_ref, o_ref, ca_ref):
        ca = _silu(c_ref[...])
        ca_ref[...] = ca
        o_ref[...] = _dot(ca, w_ref[...]) + b_ref[...]

    vm = pl.BlockSpec(memory_space=pltpu.VMEM)
    return pl.pallas_call(body, name=name, out_shape=[_sds((N_DEV, ada_w_shard.shape[1]), F32), _sds((N_DEV, D), F32)],
                          in_specs=[vm, vm, vm], out_specs=[vm, vm], compiler_params=_cparams())(c_all, ada_w_shard, ada_b_cols)


def ada_wgrad(c_act_all, dmod_cols, name):
    def body(c_ref, d_ref, o_ref):
        o_ref[...] = _dot_tn_hi(c_ref[...], d_ref[...])

    vm = pl.BlockSpec(memory_space=pltpu.VMEM)
    return pl.pallas_call(body, name=name, out_shape=_sds((D, dmod_cols.shape[1]), F32), in_specs=[vm, vm], out_specs=vm,
                          compiler_params=_cparams())(c_act_all, dmod_cols)


def exchange(srcs, name, gather):
    n = len(srcs)
    shapes = [tuple(s.shape) if gather else tuple(s.shape[1:]) for s in srcs]

    def body(*refs):
        src_refs, out_refs = refs[:n], refs[n:2 * n]
        send_sems, recv_sems, local_sems = refs[2 * n:]
        x, y, c = lax.axis_index("x"), lax.axis_index("y"), lax.axis_index("c")
        me = 4 * x + 2 * y + c

        def peer(k):
            bx, by, bc = (k >> 2) & 1, (k >> 1) & 1, k & 1
            px, py, pc = (x + bx) % 2, (y + by) % 2, (c + bc) % 2
            return (px, py, pc), 4 * px + 2 * py + pc

        def copy(a, k, landing):
            dev, idx = peer(k)
            return pltpu.make_async_remote_copy(
                src_ref=src_refs[a] if gather else src_refs[a].at[idx], dst_ref=out_refs[a].at[idx if landing else me],
                send_sem=send_sems.at[a, k - 1], recv_sem=recv_sems.at[a, k - 1],
                device_id=dev, device_id_type=pl.DeviceIdType.MESH)

        mine = [pltpu.make_async_copy(src_refs[a] if gather else src_refs[a].at[me], out_refs[a].at[me], local_sems.at[a])
                for a in range(n)]
        for cp in mine:
            cp.start()
        sends = [copy(a, k, False) for a in range(n) for k in range(1, N_DEV)]
        for cp in sends:
            cp.start()
        for a in range(n):
            for k in range(1, N_DEV):
                copy(a, k, True).wait_recv()
        for cp in sends:
            cp.wait_send()
        for cp in mine:
            cp.wait()

    hbm = pl.BlockSpec(memory_space=pl.ANY)
    return pl.pallas_call(
        body, name=name, out_shape=[_sds((N_DEV,) + shp, s.dtype) for shp, s in zip(shapes, srcs)], in_specs=[hbm] * n,
        out_specs=[hbm] * n,
        scratch_shapes=[pltpu.SemaphoreType.DMA((n, N_DEV - 1)), pltpu.SemaphoreType.DMA((n, N_DEV - 1)),
                        pltpu.SemaphoreType.DMA((n,))],
        compiler_params=pltpu.CompilerParams(has_side_effects=True))(*srcs)


def gather_two_level(src, name):
    def body(src_ref, out_ref, send_sems, recv_sems, local_sem):
        x, y, c = lax.axis_index("x"), lax.axis_index("y"), lax.axis_index("c")
        me, sibling = (x, y, c), (x, y, 1 - c)
        chips = [(1 - x, y), (x, 1 - y), (1 - x, 1 - y)]

        def slot(px, py, pc):
            return out_ref.at[4 * px + 2 * py + pc]

        def copy(k, block, to, src=None):
            return pltpu.make_async_remote_copy(
                src_ref=slot(*block) if src is None else src, dst_ref=slot(*block), send_sem=send_sems.at[k],
                recv_sem=recv_sems.at[k], device_id=to, device_id_type=pl.DeviceIdType.MESH)

        mine = pltpu.make_async_copy(src_ref, slot(*me), local_sem)
        mine.start()
        first = [copy(0, me, sibling, src=src_ref)]
        first += [copy(1 + j, me, (*chip, c), src=src_ref) for j, chip in enumerate(chips)]
        for cp in first:
            cp.start()
        passed = [copy(4 + j, (*chip, c), sibling) for j, chip in enumerate(chips)]
        for j, chip in enumerate(chips):
            copy(1 + j, (*chip, c), me).wait_recv()
            passed[j].start()
        copy(0, sibling, me).wait_recv()
        for j, chip in enumerate(chips):
            copy(4 + j, (*chip, 1 - c), me).wait_recv()
        for cp in first + passed:
            cp.wait_send()
        mine.wait()

    hbm = pl.BlockSpec(memory_space=pl.ANY)
    return pl.pallas_call(
        body, name=name, out_shape=_sds((N_DEV,) + tuple(src.shape), src.dtype), in_specs=[hbm], out_specs=hbm,
        scratch_shapes=[pltpu.SemaphoreType.DMA((N_DEV - 1,)), pltpu.SemaphoreType.DMA((N_DEV - 1,)), pltpu.SemaphoreType.DMA],
        compiler_params=pltpu.CompilerParams(has_side_effects=True))(src)


def _peer(k):
    x, y, c = lax.axis_index("x"), lax.axis_index("y"), lax.axis_index("c")
    px, py, pc = (x + ((k >> 2) & 1)) % 2, (y + ((k >> 1) & 1)) % 2, (c + (k & 1)) % 2
    return (px, py, pc), 4 * px + 2 * py + pc


def _my_slot():
    return 4 * lax.axis_index("x") + 2 * lax.axis_index("y") + lax.axis_index("c")


_HBM = pl.BlockSpec(memory_space=pltpu.HBM)
_SEM = pl.BlockSpec(memory_space=pltpu.SEMAPHORE)
_EFFECT = pltpu.SideEffectType.DATAFLOW_SIDE_EFFECTING


def exchange_start(srcs, name, gather):
    n = len(srcs)
    shapes = [tuple(s.shape) if gather else tuple(s.shape[1:]) for s in srcs]
    lands = [lax.empty((N_DEV,) + shp, s.dtype) for shp, s in zip(shapes, srcs)]

    def body(*refs):
        src_refs, land_refs = refs[:n], refs[n:2 * n]
        sems = refs[2 * n:4 * n]
        token = refs[-1]
        me = _my_slot()
        for a in range(n):
            for k in range(1, N_DEV):
                dev, idx = _peer(k)
                pltpu.make_async_remote_copy(
                    src_ref=src_refs[a] if gather else src_refs[a].at[idx], dst_ref=land_refs[a].at[me],
                    send_sem=sems[2 * a].at[k - 1], recv_sem=sems[2 * a + 1].at[k - 1],
                    device_id=dev, device_id_type=pl.DeviceIdType.MESH).start()
        token[...] = jnp.zeros_like(token)

    out_shape = ([pltpu.SemaphoreType.DMA((N_DEV - 1,))] * (2 * n) + [pltpu.HBM(s.shape, s.dtype) for s in srcs]
                 + [pltpu.HBM(l.shape, l.dtype) for l in lands] + [_sds((8, LANE), F32)])
    out = pl.pallas_call(
        body, name=name, out_shape=out_shape, in_specs=[_HBM] * (2 * n),
        out_specs=[_SEM] * (2 * n) + [_HBM] * (2 * n) + [pl.BlockSpec(memory_space=pltpu.VMEM)],
        input_output_aliases={i: 2 * n + i for i in range(2 * n)},
        compiler_params=pltpu.CompilerParams(has_side_effects=_EFFECT))(
            *[pltpu.with_memory_space_constraint(s, pltpu.HBM) for s in srcs],
            *[pltpu.with_memory_space_constraint(l, pltpu.HBM) for l in lands])
    parts = [(out[2 * a], out[2 * a + 1], out[2 * n + a], out[3 * n + a]) for a in range(n)]
    return parts, out[-1]


def exchange_wait(parts, after, name, gather):
    n = len(parts)

    def body(*refs):
        src_refs, land_refs = refs[:n], refs[n:2 * n]
        sems = refs[2 * n:4 * n]
        for a in range(n):
            for k in range(1, N_DEV):
                dev, idx = _peer(k)
                copy = pltpu.make_async_remote_copy(
                    src_ref=src_refs[a] if gather else src_refs[a].at[idx], dst_ref=land_refs[a].at[idx],
                    send_sem=sems[2 * a].at[k - 1], recv_sem=sems[2 * a + 1].at[k - 1],
                    device_id=dev, device_id_type=pl.DeviceIdType.MESH)
                copy.wait_send()
                copy.wait_recv()

    srcs = [p[2] for p in parts]
    lands = [p[3] for p in parts]
    sems = [s for p in parts for s in p[:2]]
    out = pl.pallas_call(
        body, name=name, out_shape=[pltpu.HBM(a.shape, a.dtype) for a in srcs + lands],
        in_specs=[_HBM] * (2 * n) + [_SEM] * (2 * n) + [pl.BlockSpec(memory_space=pl.ANY)], out_specs=[_HBM] * (2 * n),
        input_output_aliases={i: i for i in range(2 * n)},
        compiler_params=pltpu.CompilerParams(has_side_effects=_EFFECT))(*srcs, *lands, *sems, after)
    return out[n:]


def _cols_to_slabs(g):
    r, c = g.shape
    return g.reshape(r, N_DEV, c // N_DEV).transpose(1, 0, 2)


def _slabs_to_cols(s):
    _, r, cs = s.shape
    return s.transpose(1, 0, 2).reshape(r, N_DEV * cs)


def _rep_heads(v):
    return jnp.repeat(v.reshape(N_HEADS), HEAD).reshape(1, D_SSD)


def local_fwd_bwd(x, target, mod, get_w, put_grad, small):
    n1w, n2w, fnw = small["norm1_w"], small["norm2_w"], small["final_norm_w"]
    dtb_f, alog_f, dsk_f = _rep_heads(small["dt_bias"]), _rep_heads(small["a_log"]), _rep_heads(small["d_skip"])
    snw = small["ssd_norm_w"]

    def after(v, token):
        return v + token[0:1, 0:1]

    h1 = norm_mod(x, mod, n1w, 0, "norm1")
    w_in = get_w("w_in", h1)
    proj = mm_nt([(h1, w_in["w_in_p"], 0)], "in_proj")
    xbc = conv_silu_fwd(proj, 4096 // CB, D_XBC, small["ssd_conv_w"], small["ssd_conv_b"], "ssd_conv")
    y, ysn, s_prev = ssd_fwd(xbc, proj, dtb_f, alog_f, dsk_f, snw, "ssd_scan")
    uc = conf_conv_fwd(proj, 2048 // CB, 3072 // CB, small["conf_conv_w"], small["conf_conv_b"], "conf_conv")
    u = ln_silu(uc, small["conf_ln_w"], small["conf_ln_b"], "conf_ln")
    w_out = get_w("w_out", u)
    mix = mm_nn([(ysn, w_out, 0), (u, w_out, 1)], "out_proj")
    h2, x1 = norm_mod(x, mod, n2w, 3, "norm2", res=mix, gate_row=2)
    w_up_t = get_w("w_up", h2)
    up = mm_nt([(h2, w_up_t, 0)], "up_proj")
    act = ffn_conv_fwd(up, small["ffn_conv_w"], small["ffn_conv_b"], "ffn_conv")
    w_down = get_w("w_down", act)
    ff = mm_nn([(act, w_down, 0)], "down_proj")
    dx2, dff, acc_f = final_loss(ff, x1, mod, fnw, target, "final_loss")

    token = put_grad("w_down", mm_tn(act, dff, "wgrad_down"))
    dact = mm_nt([(dff, w_down, 0)], "dact")
    dupg, dupv, dwg, dwv = ffn_conv_bwd(up, small["ffn_conv_w"], after(small["ffn_conv_b"], token), dact, "ffn_conv_bwd")
    token = put_grad("w_up", jnp.concatenate([mm_tn(dupg, h2, "wgrad_up_gate"), mm_tn(dupv, h2, "wgrad_up_val")], axis=0))
    dh2 = mm_nn([(dupg, w_up_t, 0), (dupv, w_up_t, 1)], "dh2")
    dx1, dmix, acc_2 = norm_mod_bwd(dh2, x1, dx2, mod, after(n2w, token), 3, "norm2_bwd", mix=mix, gate_row=2)

    token = put_grad("w_out", jnp.concatenate([mm_tn(ysn, dmix, "wgrad_out_ssd"), mm_tn(u, dmix, "wgrad_out_conf")], axis=0))
    dmixin = mm_nt([(dmix, w_out, 0)], "dmixin")
    duc, acc_ln = ln_silu_bwd(dmixin, uc, after(small["conf_ln_w"], token), small["conf_ln_b"], "conf_ln_bwd")
    dcfa, dcfg, dw_cc = conf_conv_bwd(proj, 2048 // CB, 3072 // CB, small["conf_conv_w"], duc, "conf_conv_bwd")
    dz, ddt, dxbc_post, acc_s, acc_s16 = ssd_bwd(dmixin, y, xbc, proj, s_prev, dtb_f, alog_f, dsk_f, snw, "ssd_scan_bwd")
    dxbc, dw_sc = conv_silu_bwd(proj, 4096 // CB, D_XBC, small["ssd_conv_w"], small["ssd_conv_b"], dxbc_post, "ssd_conv_bwd")
    token = put_grad("w_in", jnp.concatenate(
        [mm_tn(dz, h1, "wgrad_in_z"), mm_tn(dxbc, h1, "wgrad_in_xbc"), mm_tn(ddt, h1, "wgrad_in_dt")[:N_HEADS],
         mm_tn(dcfa, h1, "wgrad_in_cfa"), mm_tn(dcfg, h1, "wgrad_in_cfg")], axis=0))
    dh1 = mm_nn([(dz, w_in["w_z"], 0), (ddt, w_in["w_dt16"], 0), (dcfa, w_in["w_cfa"], 0), (dcfg, w_in["w_cfg"], 0),
                 (dxbc, w_in["w_xbc"], 0)], "dh1")
    grad_x, acc_1 = norm_mod_bwd(dh1, x, dx1, mod, after(n1w, token), 0, "norm1_bwd")

    small_accs = dict(acc_1=acc_1, acc_2=acc_2, acc_f=acc_f, acc_ln=acc_ln, acc_s=acc_s, acc_s16=acc_s16, dw_sc=dw_sc,
                      dw_cc=dw_cc, dwg=dwg, dwv=dwv)
    return grad_x, small_accs


def kernel(x, c, ada_w, ada_b, norm1_w, w_in, ssd_conv_w, ssd_conv_b, dt_bias, a_log, d_skip, ssd_norm_w, conf_conv_w, conf_conv_b, conf_ln_w, conf_ln_b, w_out, norm2_w, w_up, ffn_conv_w, ffn_conv_b, w_down, final_norm_w, loss_target, m_ada_w, m_ada_b, m_norm1_w, m_w_in, m_ssd_conv_w, m_ssd_conv_b, m_dt_bias, m_a_log, m_d_skip, m_ssd_norm_w, m_conf_conv_w, m_conf_conv_b, m_conf_ln_w, m_conf_ln_b, m_w_out, m_norm2_w, m_w_up, m_ffn_conv_w, m_ffn_conv_b, m_w_down, m_final_norm_w, v_ada_w, v_ada_b, v_norm1_w, v_w_in, v_ssd_conv_w, v_ssd_conv_b, v_dt_bias, v_a_log, v_d_skip, v_ssd_norm_w, v_conf_conv_w, v_conf_conv_b, v_conf_ln_w, v_conf_ln_b, v_w_out, v_norm2_w, v_w_up, v_ffn_conv_w, v_ffn_conv_b, v_w_down, v_final_norm_w):
    me = 4 * lax.axis_index("x") + 2 * lax.axis_index("y") + lax.axis_index("c")
    weights = dict(ada_w=ada_w, ada_b=ada_b, norm1_w=norm1_w, w_in=w_in, ssd_conv_w=ssd_conv_w, ssd_conv_b=ssd_conv_b,
                   dt_bias=dt_bias, a_log=a_log, d_skip=d_skip, ssd_norm_w=ssd_norm_w, conf_conv_w=conf_conv_w,
                   conf_conv_b=conf_conv_b, conf_ln_w=conf_ln_w, conf_ln_b=conf_ln_b, w_out=w_out, norm2_w=norm2_w, w_up=w_up,
                   ffn_conv_w=ffn_conv_w, ffn_conv_b=ffn_conv_b, w_down=w_down, final_norm_w=final_norm_w)
    moms_m = dict(ada_w=m_ada_w, ada_b=m_ada_b, norm1_w=m_norm1_w, w_in=m_w_in, ssd_conv_w=m_ssd_conv_w, ssd_conv_b=m_ssd_conv_b,
                  dt_bias=m_dt_bias, a_log=m_a_log, d_skip=m_d_skip, ssd_norm_w=m_ssd_norm_w, conf_conv_w=m_conf_conv_w,
                  conf_conv_b=m_conf_conv_b, conf_ln_w=m_conf_ln_w, conf_ln_b=m_conf_ln_b, w_out=m_w_out, norm2_w=m_norm2_w,
                  w_up=m_w_up, ffn_conv_w=m_ffn_conv_w, ffn_conv_b=m_ffn_conv_b, w_down=m_w_down, final_norm_w=m_final_norm_w)
    moms_v = dict(ada_w=v_ada_w, ada_b=v_ada_b, norm1_w=v_norm1_w, w_in=v_w_in, ssd_conv_w=v_ssd_conv_w, ssd_conv_b=v_ssd_conv_b,
                  dt_bias=v_dt_bias, a_log=v_a_log, d_skip=v_d_skip, ssd_norm_w=v_ssd_norm_w, conf_conv_w=v_conf_conv_w,
                  conf_conv_b=v_conf_conv_b, conf_ln_w=v_conf_ln_w, conf_ln_b=v_conf_ln_b, w_out=v_w_out, norm2_w=v_norm2_w,
                  w_up=v_w_up, ffn_conv_w=v_ffn_conv_w, ffn_conv_b=v_ffn_conv_b, w_down=v_w_down, final_norm_w=v_final_norm_w)
    names = list(weights)

    def to2d(a):
        return a[0] if a.ndim == 3 else a.reshape(1, -1)

    big = ("w_in", "w_out", "w_up", "w_down")

    c_all, scw_all, ccw_all, fcw_all = exchange([c.reshape(8, LANE), ssd_conv_w[0], conf_conv_w[0], ffn_conv_w[0]],
                                                "gather_small", gather=True)
    c_all = c_all.reshape(N_DEV, D)

    ada_cols = ada_w.shape[2]
    ada_b_cols = lax.dynamic_slice(ada_b, (0, me * ada_cols), (1, ada_cols))
    mod_cols, c_act_all = ada_mod(c_all, ada_w[0], ada_b_cols, "ada_mod")
    mod_parts, = exchange([jnp.pad(mod_cols, ((0, 0), (0, D - ada_cols))).reshape(N_DEV, 8, LANE)], "scatter_mod", gather=False)
    mod = mod_parts.reshape(N_DEV, D)[:, :ada_cols].reshape(6, D)
    mod = jnp.pad(mod, ((0, 2), (0, 0)))

    def rows_of(a):
        return jnp.swapaxes(a, 1, 2)[0] if a.shape[2] != D else a[0]

    shards, mod = lax.optimization_barrier(([rows_of(weights[n]).astype(BF16) for n in big], mod))
    w_in_slabs = gather_two_level(shards[0], "gather_w_in")
    later, w_in_slabs = lax.optimization_barrier((shards[1:], w_in_slabs))
    gather_parts, token = exchange_start(later, "gather_weights_start", gather=True)
    mod = mod + token[0:1, 0:1]

    small = {n: to2d(weights[n]) for n in names if n not in ("ada_w",) + big}
    small["ssd_conv_w"] = _slabs_to_cols(scw_all)
    small["conf_conv_w"] = _slabs_to_cols(ccw_all)
    small["ffn_conv_w"] = _slabs_to_cols(fcw_all)

    def with_own(landed, own):
        return lax.dynamic_update_slice(landed, own[None], (me,) + (0,) * own.ndim)

    def get_w(n, after):
        if n == "w_in":
            slabs = w_in_slabs
        else:
            a = big.index(n)
            landed, = exchange_wait([gather_parts[a - 1]], after, "gather_" + n + "_wait", gather=True)
            slabs = with_own(landed, shards[a])
        full = slabs.reshape(N_DEV * slabs.shape[1], D)
        if n != "w_in":
            return full
        w_z, w_xbc, w_dt, w_cfa, w_cfg = full[:1024], full[1024:2560], full[2560:2576], full[2576:3600], full[3600:]
        return dict(w_in_p=jnp.concatenate([w_z, jnp.repeat(w_dt, HEAD, axis=0), w_cfa, w_cfg, w_xbc], axis=0), w_z=w_z,
                    w_xbc=w_xbc, w_dt16=jnp.pad(w_dt, ((0, LANE - N_HEADS), (0, 0))), w_cfa=w_cfa, w_cfg=w_cfg)

    scatter_parts, sent = {}, {}

    def put_grad(n, g):
        sent[n] = g.reshape(N_DEV, g.shape[0] // N_DEV, g.shape[1]).astype(BF16)
        (scatter_parts[n],), token = exchange_start([sent[n]], "scatter_" + n + "_start", gather=False)
        return token

    grad_x, accs = local_fwd_bwd(x[0], loss_target[0], mod, get_w, put_grad, small)
    loss = lax.psum(0.5 / D * jnp.sum(accs["acc_f"][2:3]), ("x", "y", "c"))

    grads, delta, new_m, new_v = {}, {}, {}, {}

    def finish(ns, after, name):
        landed = exchange_wait([scatter_parts[n] for n in ns], after, name, gather=False)
        for n, slots in zip(ns, landed):
            slots = with_own(slots, lax.dynamic_index_in_dim(sent[n], me, 0, keepdims=False))
            out = adamw_slots(rows_of(weights[n]), slots, rows_of(moms_m[n]), rows_of(moms_v[n]), "adamw_" + n)
            if weights[n].shape[2] != D:
                out = [jnp.swapaxes(o, 0, 1) for o in out]
            grads[n], delta[n], new_m[n], new_v[n] = out

    finish(big[1:], grad_x, "scatter_grads_wait")

    order = ("acc_1", "acc_2", "acc_f", "acc_ln", "acc_s", "acc_s16", "dw_sc", "dw_cc", "dwg", "dwv")
    shapes = [accs[k].shape for k in order]
    acc_list, _ = lax.optimization_barrier(([accs[k] for k in order], [new_v[n] for n in big[1:]]))
    packed_all, = exchange([pack_rows(acc_list, "pack_small_grads")], "gather_small_grads", gather=True)
    packed_red, = sum_slots_many([packed_all], "sum_small_grads")
    gathered = dict(zip(order, unpack_rows(packed_all, shapes)))
    red = dict(zip(order, unpack_rows(packed_red, shapes)))

    def mod_rows(a1, a2, af):
        return jnp.concatenate([a1[..., 0:2, :], a2[..., 3:4, :], a2[..., 0:2, :], af[..., 1:2, :]], axis=-2)

    dmod_all = mod_rows(gathered["acc_1"], gathered["acc_2"], gathered["acc_f"]).reshape(N_DEV, 6 * D)
    grads["ada_w"] = ada_wgrad(c_act_all, lax.dynamic_slice(dmod_all, (0, me * ada_cols), (N_DEV, ada_cols)), "ada_wgrad")

    def my_cols(full, k_taps):
        cols = full.shape[1] // N_DEV
        return lax.dynamic_slice(full, (0, me * cols), (k_taps, cols))

    fcw = jnp.concatenate([red["dwg"], red["dwv"]], axis=1)
    grads.update(
        ada_b=mod_rows(red["acc_1"], red["acc_2"], red["acc_f"]).reshape(1, 6 * D), norm1_w=red["acc_1"][2:3],
        ssd_conv_w=my_cols(red["dw_sc"], K_SSD), ssd_conv_b=red["dw_sc"][K_SSD:K_SSD + 1],
        dt_bias=red["acc_s16"][1:2, :N_HEADS], a_log=red["acc_s16"][2:3, :N_HEADS], d_skip=red["acc_s16"][3:4, :N_HEADS],
        ssd_norm_w=red["acc_s"][0:1], conf_conv_w=my_cols(red["dw_cc"], K_CONF), conf_conv_b=red["dw_cc"][K_CONF:K_CONF + 1],
        conf_ln_w=red["acc_ln"][0:1], conf_ln_b=red["acc_ln"][1:2], norm2_w=red["acc_2"][2:3],
        ffn_conv_w=my_cols(fcw, K_FFN), ffn_conv_b=fcw[K_FFN:K_FFN + 1], final_norm_w=red["acc_f"][0:1])

    rest = [n for n in names if n not in big]
    d_l, m_l, v_l = adamw_many([to2d(weights[n]) for n in rest], [grads[n] for n in rest], [to2d(moms_m[n]) for n in rest],
                               [to2d(moms_v[n]) for n in rest], "adamw_small")
    for n, dd, mm, vv in zip(rest, d_l, m_l, v_l):
        delta[n], new_m[n], new_v[n] = dd, mm, vv
    finish(big[:1], d_l[0], "scatter_w_in_wait")
    shape_of = lambda d_: {n: d_[n].reshape(weights[n].shape) for n in names}
    grads, delta, new_m, new_v = shape_of(grads), shape_of(delta), shape_of(new_m), shape_of(new_v)
    return (loss, grad_x[None], *[grads[n] for n in names], *[delta[n] for n in names], *[new_m[n] for n in names],
            *[new_v[n] for n in names])
```

```python
import functools

import jax
import jax.numpy as jnp
from jax import lax
from jax.experimental import pallas as pl
from jax.experimental.pallas import tpu as pltpu

F32 = jnp.float32
BF16 = jnp.bfloat16
HI = lax.Precision.HIGHEST

N_DEV = 8
D = 1024
D_SSD = 1024
HEAD = 64
N_HEADS = 16
N_STATE = 128
D_XBC = 1536
D_CONF = 1024
D_FF = 2816
K_SSD, K_CONF, K_FFN = 4, 31, 3
D_INP = 5632
LANE = 128
TR = 256
TM = 512
Q = 256
CB = 256
TC = 1024
VMEM_LIMIT = 56 * 1024 * 1024

ADAM_LR, ADAM_B1, ADAM_B2, ADAM_EPS, ADAM_WD, ADAM_STEP = 0.001, 0.9, 0.999, 1e-08, 0.01, 10


def _cparams(sem=None):
    return pltpu.CompilerParams(vmem_limit_bytes=VMEM_LIMIT, dimension_semantics=sem)


def _sds(shape, dtype):
    return jax.ShapeDtypeStruct(shape, dtype)


def _sigmoid(x):
    return 1.0 / (1.0 + jnp.exp(-x))


def _silu(x):
    return x * _sigmoid(x)


def _dsilu(x):
    s = _sigmoid(x)
    return s * (1.0 + x * (1.0 - s))


def _softplus(x):
    return jnp.maximum(x, 0.0) + jnp.log(1.0 + jnp.exp(-jnp.abs(x)))


def _dot(a, b):
    return jnp.dot(a.astype(BF16), b.astype(BF16), preferred_element_type=F32)


def _dot_nt(a, b):
    return lax.dot_general(a.astype(BF16), b.astype(BF16), (((1,), (1,)), ((), ())), preferred_element_type=F32)


def _dot_tn(a, b):
    return lax.dot_general(a.astype(BF16), b.astype(BF16), (((0,), (0,)), ((), ())), preferred_element_type=F32)


def _bf16_terms(a, terms):
    parts, rem = [], a
    for t in range(terms):
        p = rem.astype(BF16)
        parts.append(p)
        if t + 1 < terms:
            rem = rem - p.astype(F32)
    return parts


def _dot_exact(a, b, terms, exact, dims=(((1,), (0,)), ((), ()))):
    if exact == "a":
        a_b = a.astype(BF16)
        outs = [lax.dot_general(a_b, p, dims, preferred_element_type=F32) for p in _bf16_terms(b, terms)]
    else:
        b_b = b.astype(BF16)
        outs = [lax.dot_general(p, b_b, dims, preferred_element_type=F32) for p in _bf16_terms(a, terms)]
    acc = outs[-1]
    for o in reversed(outs[:-1]):
        acc = acc + o
    return acc


def _dot_tn_hi(a, b):
    return lax.dot_general(a, b, (((0,), (0,)), ((), ())), precision=HI, preferred_element_type=F32)


def _colsum(x):
    return jnp.sum(x, axis=0, keepdims=True)


def _const_spec(shape):
    return pl.BlockSpec(shape, lambda *_: (0,) * len(shape))


def _col_tile(n):
    for t in (1408, 1024, 768, 512, 256, 128):
        if n % t == 0 and t <= n:
            return t
    return n


def mm_nt(pairs, name, rows=None):
    L = pairs[0][0].shape[0]
    K = pairs[0][1].shape[0] if rows is None else rows
    tk = _col_tile(K)
    n = len(pairs)

    def body(*refs):
        o_ref = refs[-1]
        acc = None
        for p in range(n):
            t = lax.dot_general(refs[2 * p][...], refs[2 * p + 1][...], (((1,), (1,)), ((), ())),
                                preferred_element_type=F32)
            acc = t if acc is None else acc + t
        o_ref[...] = acc

    in_specs, args = [], []
    for a, w, cb in pairs:
        in_specs += [pl.BlockSpec((TM, a.shape[1]), lambda j, i: (i, 0)),
                     pl.BlockSpec((tk, a.shape[1]), functools.partial(lambda j, i, cb: (j, cb), cb=cb))]
        args += [a, w]
    return pl.pallas_call(
        body, name=name, grid=(K // tk, L // TM), out_shape=_sds((L, K), F32), in_specs=in_specs,
        out_specs=pl.BlockSpec((TM, tk), lambda j, i: (i, j)),
        compiler_params=_cparams(("parallel", "parallel")))(*args)


def mm_tn(a, g, name):
    L, M = a.shape
    N = g.shape[1]
    tn = _col_tile(N) if N > 1024 else N
    if M * tn * 4 > 8 * 1024 * 1024:
        tn = 512
    tl = 512 if L % 512 == 0 else TR
    nl = L // tl

    def body(a_ref, g_ref, o_ref, acc_ref):
        @pl.when(pl.program_id(1) == 0)
        def _():
            acc_ref[...] = jnp.zeros((M, tn), F32)

        acc_ref[...] += lax.dot_general(a_ref[...], g_ref[...], (((0,), (0,)), ((), ())), preferred_element_type=F32)

        @pl.when(pl.program_id(1) == nl - 1)
        def _():
            o_ref[...] = acc_ref[...].astype(BF16)

    return pl.pallas_call(
        body, name=name, grid=(N // tn, nl), out_shape=_sds((M, N), BF16),
        in_specs=[pl.BlockSpec((tl, M), lambda j, l: (l, 0)), pl.BlockSpec((tl, tn), lambda j, l: (l, j))],
        out_specs=pl.BlockSpec((M, tn), lambda j, l: (0, j)), scratch_shapes=[pltpu.VMEM((M, tn), F32)],
        compiler_params=_cparams(("parallel", "arbitrary")))(a, g)


def _row_spec(width=D):
    return pl.BlockSpec((TR, width), lambda i: (i, 0))


def _row_col_spec(width, col):
    return pl.BlockSpec((TR, width), lambda i: (i, col))


def norm_mod(x, mod, w, shift_row, name):
    L = x.shape[0]

    def body(x_ref, mod_ref, w_ref, h_ref):
        xin = x_ref[...]
        r = lax.rsqrt(jnp.mean(xin * xin, axis=-1, keepdims=True) + 1e-6)
        h = (xin * r * w_ref[...]) * (1.0 + mod_ref[shift_row + 1:shift_row + 2, :]) + mod_ref[shift_row:shift_row + 1, :]
        h_ref[...] = h.astype(BF16)

    return pl.pallas_call(body, name=name, grid=(L // TR,), out_shape=_sds((L, D), BF16),
                          in_specs=[_row_spec(), _const_spec((8, D)), _const_spec((1, D))], out_specs=_row_spec(),
                          compiler_params=_cparams(("parallel",)))(x, mod, w)


def mixer_out(ysn, uc, lnw, lnb, w_out, x, mod, n2w, name):
    L = x.shape[0]

    def body(ysn_ref, uc_ref, lnw_ref, lnb_ref, wo_ref, x_ref, mod_ref, n2w_ref, mix_ref, u_ref, x1_ref, h2_ref):
        uc_v = uc_ref[...]
        mu = jnp.mean(uc_v, axis=-1, keepdims=True)
        var = jnp.mean(jnp.square(uc_v - mu), axis=-1, keepdims=True)
        u = _silu((uc_v - mu) * lax.rsqrt(var + 1e-5) * lnw_ref[...] + lnb_ref[...]).astype(BF16)
        u_ref[...] = u
        mix = (jnp.dot(ysn_ref[...], wo_ref[0:D_SSD, :], preferred_element_type=F32)
               + jnp.dot(u, wo_ref[D_SSD:D_SSD + D_CONF, :], preferred_element_type=F32))
        mix_ref[...] = mix
        x1 = x_ref[...] + mod_ref[2:3, :] * mix
        x1_ref[...] = x1
        r = lax.rsqrt(jnp.mean(x1 * x1, axis=-1, keepdims=True) + 1e-6)
        h2_ref[...] = ((x1 * r * n2w_ref[...]) * (1.0 + mod_ref[4:5, :]) + mod_ref[3:4, :]).astype(BF16)

    return pl.pallas_call(
        body, name=name, grid=(L // TR,),
        out_shape=[_sds((L, D), F32), _sds((L, D_CONF), BF16), _sds((L, D), F32), _sds((L, D), BF16)],
        in_specs=[_row_spec(), _row_spec(), _const_spec((1, D)), _const_spec((1, D)), _const_spec((D_SSD + D_CONF, D)), _row_spec(),
                  _const_spec((8, D)), _const_spec((1, D))],
        out_specs=[_row_spec()] * 4, compiler_params=_cparams(("parallel",)))(ysn, uc, lnw, lnb, w_out, x, mod, n2w)


def ln_silu_bwd(du_all, uc, lnw, lnb, name):
    L = uc.shape[0]

    def body(du_ref, u_ref, w_ref, b_ref, o_ref, acc_ref):
        @pl.when(pl.program_id(0) == 0)
        def _():
            acc_ref[...] = jnp.zeros((8, D), F32)

        u = u_ref[...]
        mu = jnp.mean(u, axis=-1, keepdims=True)
        rl = lax.rsqrt(jnp.mean(jnp.square(u - mu), axis=-1, keepdims=True) + 1e-5)
        n = (u - mu) * rl
        v = n * w_ref[...] + b_ref[...]
        dv = du_ref[...] * _dsilu(v)
        acc_ref[0:1, :] += _colsum(dv * n)
        acc_ref[1:2, :] += _colsum(dv)
        dn = dv * w_ref[...]
        o_ref[...] = rl * (dn - jnp.mean(dn, axis=-1, keepdims=True) - n * jnp.mean(dn * n, axis=-1, keepdims=True))

    return pl.pallas_call(body, name=name, grid=(L // TR,), out_shape=[_sds((L, D), F32), _sds((8, D), F32)],
                          in_specs=[_row_col_spec(D, 1), _row_spec(), _const_spec((1, D)), _const_spec((1, D))],
                          out_specs=[_row_spec(), _const_spec((8, D))],
                          compiler_params=_cparams(("arbitrary",)))(du_all, uc, lnw, lnb)


def final_loss(act, w_down, x1, mod, fw, target, name):
    L = act.shape[0]

    def body(act_ref, wd_ref, x1_ref, mod_ref, fw_ref, t_ref, dx_ref, dff_ref, dact_ref, acc_ref):
        @pl.when(pl.program_id(0) == 0)
        def _():
            acc_ref[...] = jnp.zeros((8, D), F32)

        ff_v = jnp.dot(act_ref[...], wd_ref[...], preferred_element_type=F32)
        g2 = mod_ref[5:6, :]
        x2 = x1_ref[...] + g2 * ff_v
        r = lax.rsqrt(jnp.mean(x2 * x2, axis=-1, keepdims=True) + 1e-6)
        n = x2 * r
        err = n * fw_ref[...] - t_ref[...]
        dy = err * (1.0 / D)
        dn = dy * fw_ref[...]
        dx2 = r * (dn - n * jnp.mean(dn * n, axis=-1, keepdims=True))
        acc_ref[0:1, :] += _colsum(dy * n)
        acc_ref[1:2, :] += _colsum(dx2 * ff_v)
        acc_ref[2:3, :] += _colsum(err * err)
        dx_ref[...] = dx2
        dff = (dx2 * g2).astype(BF16)
        dff_ref[...] = dff
        dact_ref[...] = lax.dot_general(dff, wd_ref[...], (((1,), (1,)), ((), ())), preferred_element_type=F32)

    return pl.pallas_call(
        body, name=name, grid=(L // TR,),
        out_shape=[_sds((L, D), F32), _sds((L, D), BF16), _sds((L, D_FF), F32), _sds((8, D), F32)],
        in_specs=[_row_spec(D_FF), _const_spec((D_FF, D)), _row_spec(), _const_spec((8, D)), _const_spec((1, D)), _row_spec()],
        out_specs=[_row_spec(), _row_spec(), _row_spec(D_FF), _const_spec((8, D))],
        compiler_params=_cparams(("arbitrary",)))(act, w_down, x1, mod, fw, target)


def norm_mod_bwd(dh_pairs, xin, dres, mod, w, shift_row, name, mix=None, gate_row=None):
    L = xin.shape[0]
    has_mix = mix is not None
    n_pairs = len(dh_pairs)

    def body(*refs):
        pair_refs, refs = refs[:2 * n_pairs], refs[2 * n_pairs:]
        if has_mix:
            x_ref, dres_ref, mod_ref, w_ref, mix_ref, dx_ref, dmix_ref, acc_ref = refs
        else:
            x_ref, dres_ref, mod_ref, w_ref, dx_ref, acc_ref = refs

        @pl.when(pl.program_id(0) == 0)
        def _():
            acc_ref[...] = jnp.zeros((8, D), F32)

        dh_v = None
        for p in range(n_pairs):
            t = jnp.dot(pair_refs[2 * p][...], pair_refs[2 * p + 1][...], preferred_element_type=F32)
            dh_v = t if dh_v is None else dh_v + t
        x = x_ref[...]
        r = lax.rsqrt(jnp.mean(x * x, axis=-1, keepdims=True) + 1e-6)
        n = x * r
        nw = n * w_ref[...]
        sc1 = 1.0 + mod_ref[shift_row + 1:shift_row + 2, :]
        acc_ref[0:1, :] += _colsum(dh_v)
        acc_ref[1:2, :] += _colsum(dh_v * nw)
        dnw = dh_v * sc1
        acc_ref[2:3, :] += _colsum(dnw * n)
        dn = dnw * w_ref[...]
        dx = r * (dn - n * jnp.mean(dn * n, axis=-1, keepdims=True)) + dres_ref[...]
        dx_ref[...] = dx
        if has_mix:
            acc_ref[3:4, :] += _colsum(dx * mix_ref[...])
            dmix_ref[...] = (dx * mod_ref[gate_row:gate_row + 1, :]).astype(BF16)

    ins, in_specs = [], []
    for a, wt, rb in dh_pairs:
        ins += [a, wt]
        in_specs += [_row_spec(a.shape[1]), pl.BlockSpec((a.shape[1], D), functools.partial(lambda i, rb: (rb, 0), rb=rb))]
    ins += [xin, dres, mod, w] + ([mix] if has_mix else [])
    in_specs += [_row_spec(), _row_spec(), _const_spec((8, D)), _const_spec((1, D))] + ([_row_spec()] if has_mix else [])
    out_shape = [_sds((L, D), F32)] + ([_sds((L, D), BF16)] if has_mix else []) + [_sds((8, D), F32)]
    out_specs = [_row_spec()] + ([_row_spec()] if has_mix else []) + [_const_spec((8, D))]
    return pl.pallas_call(body, name=name, grid=(L // TR,), out_shape=out_shape, in_specs=in_specs,
                          out_specs=out_specs, compiler_params=_cparams(("arbitrary",)))(*ins)


def _halo(k):
    return 8 if k <= 9 else 32


def _prev_spec(h, col0):
    return pl.BlockSpec((h, CB), lambda j, i: (jnp.maximum(i * (TC // h) - 1, 0), j + col0))


def _next_spec(h, col0, n_tiles):
    return pl.BlockSpec((h, CB), lambda j, i: (jnp.minimum(i + 1, n_tiles - 1) * (TC // h), j + col0))


def _tile_spec(col0):
    return pl.BlockSpec((TC, CB), lambda j, i: (i, j + col0))


def _w_spec(kp, col0):
    return pl.BlockSpec((kp, CB), lambda j, i: (0, j + col0))


SUBLANES = 8


def _shifted_windows(v, taps, rows):
    for r in range(SUBLANES):
        group = [(o, k) for o, k in taps if o % SUBLANES == r]
        if not group:
            continue
        s = v if r == 0 else pltpu.roll(v, v.shape[0] - r, 0)
        for o, k in group:
            yield k, s[o - r:o - r + rows, :]


def _causal_taps(ext_ref, w_ref, k_taps, first, rows):
    acc = None
    for k, win in _shifted_windows(ext_ref[...], [(first - (k_taps - 1) + k, k) for k in range(k_taps)], rows):
        t = w_ref[k:k + 1, :] * win
        acc = t if acc is None else acc + t
    return acc


def _anticausal_taps(d_ref, w_ref, k_taps, rows):
    acc = None
    for k, win in _shifted_windows(d_ref[...], [(k_taps - 1 - k, k) for k in range(k_taps)], rows):
        t = w_ref[k:k + 1, :] * win
        acc = t if acc is None else acc + t
    return acc


def _acc_conv_wgrad(dw_ref, d_tile, ext_ref, k_taps, first):
    for k, win in _shifted_windows(ext_ref[...], [(first - (k_taps - 1) + k, k) for k in range(k_taps)], TC):
        dw_ref[k:k + 1, :] += _colsum(d_tile * win)
    dw_ref[k_taps:k_taps + 1, :] += _colsum(d_tile)


def conv_silu_fwd(x, col0, width, w, b, name):
    L = x.shape[0]
    k_taps = w.shape[0]
    h = _halo(k_taps)

    def body(xp_ref, x_ref, w_ref, b_ref, o_ref, ext_ref):
        i = pl.program_id(1)
        ext_ref[0:h, :] = jnp.where(i > 0, xp_ref[...], 0.0)
        ext_ref[h:h + TC, :] = x_ref[...]
        o_ref[...] = _silu(_causal_taps(ext_ref, w_ref, k_taps, h, TC) + b_ref[...])

    return pl.pallas_call(
        body, name=name, grid=(width // CB, L // TC), out_shape=_sds((L, width), F32),
        in_specs=[_prev_spec(h, col0), _tile_spec(col0), _w_spec(k_taps, 0), pl.BlockSpec((1, CB), lambda j, i: (0, j))],
        out_specs=_tile_spec(0), scratch_shapes=[pltpu.VMEM((h + TC, CB), F32)],
        compiler_params=_cparams(("parallel", "parallel")))(x, x, w, b)


def conv_silu_bwd(x, col0, width, w, b, dpost, name):
    L = x.shape[0]
    k_taps = w.shape[0]
    h = _halo(k_taps)
    nt = L // TC

    def body(xp_ref, x_ref, xn_ref, d_ref, dn_ref, w_ref, b_ref, dx_ref, dw_ref, ext_ref, dpre_ref):
        i = pl.program_id(1)

        @pl.when(i == 0)
        def _():
            dw_ref[...] = jnp.zeros((8, CB), F32)

        ext_ref[0:h, :] = jnp.where(i > 0, xp_ref[...], 0.0)
        ext_ref[h:h + TC, :] = x_ref[...]
        ext_ref[h + TC:h + TC + h, :] = xn_ref[...]
        pre = _causal_taps(ext_ref, w_ref, k_taps, h, TC + h) + b_ref[...]
        dpre_ref[0:TC, :] = d_ref[...] * _dsilu(pre[0:TC, :])
        dpre_ref[TC:TC + h, :] = jnp.where(i < nt - 1, dn_ref[...], 0.0) * _dsilu(pre[TC:TC + h, :])
        dx_ref[...] = _anticausal_taps(dpre_ref, w_ref, k_taps, TC).astype(BF16)
        _acc_conv_wgrad(dw_ref, dpre_ref[0:TC, :], ext_ref, k_taps, h)

    return pl.pallas_call(
        body, name=name, grid=(width // CB, nt),
        out_shape=[_sds((L, width), BF16), _sds((8, width), F32)],
        in_specs=[_prev_spec(h, col0), _tile_spec(col0), _next_spec(h, col0, nt), _tile_spec(0), _next_spec(h, 0, nt),
                  _w_spec(k_taps, 0), pl.BlockSpec((1, CB), lambda j, i: (0, j))],
        out_specs=[_tile_spec(0), _w_spec(8, 0)],
        scratch_shapes=[pltpu.VMEM((h + TC + h, CB), F32), pltpu.VMEM((TC + h, CB), F32)],
        compiler_params=_cparams(("parallel", "arbitrary")))(x, x, x, dpost, dpost, w, b)


def conf_conv_fwd(proj, col_a, col_g, w, b, name):
    L = proj.shape[0]
    k_taps = w.shape[0]
    h = _halo(k_taps)

    def body(ap_ref, a_ref, gp_ref, g_ref, w_ref, b_ref, o_ref, ext_ref):
        i = pl.program_id(1)
        ext_ref[0:h, :] = jnp.where(i > 0, ap_ref[...] * _sigmoid(gp_ref[...]), 0.0)
        ext_ref[h:h + TC, :] = a_ref[...] * _sigmoid(g_ref[...])
        o_ref[...] = _causal_taps(ext_ref, w_ref, k_taps, h, TC) + b_ref[...]

    return pl.pallas_call(
        body, name=name, grid=(D_CONF // CB, L // TC), out_shape=_sds((L, D_CONF), F32),
        in_specs=[_prev_spec(h, col_a), _tile_spec(col_a), _prev_spec(h, col_g), _tile_spec(col_g), _w_spec(k_taps, 0),
                  pl.BlockSpec((1, CB), lambda j, i: (0, j))],
        out_specs=_tile_spec(0), scratch_shapes=[pltpu.VMEM((h + TC, CB), F32)],
        compiler_params=_cparams(("parallel", "parallel")))(proj, proj, proj, proj, w, b)


def conf_conv_bwd(proj, col_a, col_g, w, duc, name):
    L = proj.shape[0]
    k_taps = w.shape[0]
    h = _halo(k_taps)
    nt = L // TC

    def body(ap_ref, a_ref, gp_ref, g_ref, d_ref, dn_ref, w_ref, da_ref, dg_ref, dw_ref, ext_ref, dext_ref):
        i = pl.program_id(1)

        @pl.when(i == 0)
        def _():
            dw_ref[...] = jnp.zeros((32, CB), F32)

        a = a_ref[...]
        s = _sigmoid(g_ref[...])
        ext_ref[0:h, :] = jnp.where(i > 0, ap_ref[...] * _sigmoid(gp_ref[...]), 0.0)
        ext_ref[h:h + TC, :] = a * s
        dext_ref[0:TC, :] = d_ref[...]
        dext_ref[TC:TC + h, :] = jnp.where(i < nt - 1, dn_ref[...], 0.0)
        du0 = _anticausal_taps(dext_ref, w_ref, k_taps, TC)
        da_ref[...] = (du0 * s).astype(BF16)
        dg_ref[...] = (du0 * a * s * (1.0 - s)).astype(BF16)
        _acc_conv_wgrad(dw_ref, d_ref[...], ext_ref, k_taps, h)

    return pl.pallas_call(
        body, name=name, grid=(D_CONF // CB, nt),
        out_shape=[_sds((L, D_CONF), BF16), _sds((L, D_CONF), BF16), _sds((32, D_CONF), F32)],
        in_specs=[_prev_spec(h, col_a), _tile_spec(col_a), _prev_spec(h, col_g), _tile_spec(col_g), _tile_spec(0),
                  _next_spec(h, 0, nt), _w_spec(k_taps, 0)],
        out_specs=[_tile_spec(0), _tile_spec(0), _w_spec(32, 0)],
        scratch_shapes=[pltpu.VMEM((h + TC, CB), F32), pltpu.VMEM((TC + h, CB), F32)],
        compiler_params=_cparams(("parallel", "arbitrary")))(proj, proj, proj, proj, duc, duc, w)


def ffn_conv_fwd(up, w, b, name):
    L = up.shape[0]
    k_taps = w.shape[0]
    h = _halo(k_taps)
    cv = D_FF // CB

    def body(gp_ref, g_ref, vp_ref, v_ref, wg_ref, wv_ref, bg_ref, bv_ref, o_ref, eg_ref, ev_ref):
        i = pl.program_id(1)
        eg_ref[0:h, :] = jnp.where(i > 0, gp_ref[...], 0.0)
        eg_ref[h:h + TC, :] = g_ref[...]
        ev_ref[0:h, :] = jnp.where(i > 0, vp_ref[...], 0.0)
        ev_ref[h:h + TC, :] = v_ref[...]
        pg = _causal_taps(eg_ref, wg_ref, k_taps, h, TC) + bg_ref[...]
        pv = _causal_taps(ev_ref, wv_ref, k_taps, h, TC) + bv_ref[...]
        o_ref[...] = (_silu(pg) * pv).astype(BF16)

    bspec = lambda c0: pl.BlockSpec((1, CB), lambda j, i: (0, j + c0))
    return pl.pallas_call(
        body, name=name, grid=(cv, L // TC), out_shape=_sds((L, D_FF), BF16),
        in_specs=[_prev_spec(h, 0), _tile_spec(0), _prev_spec(h, cv), _tile_spec(cv), _w_spec(k_taps, 0), _w_spec(k_taps, cv),
                  bspec(0), bspec(cv)],
        out_specs=_tile_spec(0), scratch_shapes=[pltpu.VMEM((h + TC, CB), F32), pltpu.VMEM((h + TC, CB), F32)],
        compiler_params=_cparams(("parallel", "parallel")))(up, up, up, up, w, w, b, b)


def ffn_conv_bwd(up, w, b, dact, name):
    L = up.shape[0]
    k_taps = w.shape[0]
    h = _halo(k_taps)
    nt = L // TC
    cv = D_FF // CB

    def body(gp_ref, g_ref, gn_ref, vp_ref, v_ref, vn_ref, d_ref, dn_ref, wg_ref, wv_ref, bg_ref, bv_ref,
             dg_ref, dv_ref, dwg_ref, dwv_ref, eg_ref, ev_ref, pg_ref, pv_ref):
        i = pl.program_id(1)

        @pl.when(i == 0)
        def _():
            dwg_ref[...] = jnp.zeros((8, CB), F32)
            dwv_ref[...] = jnp.zeros((8, CB), F32)

        for e_ref, p_ref, c_ref, n_ref in ((eg_ref, gp_ref, g_ref, gn_ref), (ev_ref, vp_ref, v_ref, vn_ref)):
            e_ref[0:h, :] = jnp.where(i > 0, p_ref[...], 0.0)
            e_ref[h:h + TC, :] = c_ref[...]
            e_ref[h + TC:h + TC + h, :] = n_ref[...]
        pg = _causal_taps(eg_ref, wg_ref, k_taps, h, TC + h) + bg_ref[...]
        pv = _causal_taps(ev_ref, wv_ref, k_taps, h, TC + h) + bv_ref[...]
        dact_t = d_ref[...]
        dact_n = jnp.where(i < nt - 1, dn_ref[...], 0.0)
        pg_ref[0:TC, :] = dact_t * pv[0:TC, :] * _dsilu(pg[0:TC, :])
        pg_ref[TC:TC + h, :] = dact_n * pv[TC:TC + h, :] * _dsilu(pg[TC:TC + h, :])
        pv_ref[0:TC, :] = dact_t * _silu(pg[0:TC, :])
        pv_ref[TC:TC + h, :] = dact_n * _silu(pg[TC:TC + h, :])
        dg_ref[...] = _anticausal_taps(pg_ref, wg_ref, k_taps, TC).astype(BF16)
        dv_ref[...] = _anticausal_taps(pv_ref, wv_ref, k_taps, TC).astype(BF16)
        _acc_conv_wgrad(dwg_ref, pg_ref[0:TC, :], eg_ref, k_taps, h)
        _acc_conv_wgrad(dwv_ref, pv_ref[0:TC, :], ev_ref, k_taps, h)

    bspec = lambda c0: pl.BlockSpec((1, CB), lambda j, i: (0, j + c0))
    ext = pltpu.VMEM((h + TC + h, CB), F32)
    dpre = pltpu.VMEM((TC + h, CB), F32)
    return pl.pallas_call(
        body, name=name, grid=(cv, nt),
        out_shape=[_sds((L, D_FF), BF16), _sds((L, D_FF), BF16), _sds((8, D_FF), F32), _sds((8, D_FF), F32)],
        in_specs=[_prev_spec(h, 0), _tile_spec(0), _next_spec(h, 0, nt), _prev_spec(h, cv), _tile_spec(cv), _next_spec(h, cv, nt),
                  _tile_spec(0), _next_spec(h, 0, nt), _w_spec(k_taps, 0), _w_spec(k_taps, cv), bspec(0), bspec(cv)],
        out_specs=[_tile_spec(0), _tile_spec(0), _w_spec(8, 0), _w_spec(8, 0)],
        scratch_shapes=[ext, ext, dpre, dpre],
        compiler_params=_cparams(("parallel", "arbitrary")))(up, up, up, up, up, up, dact, dact, w, w, b, b)


def _ssd_common(xbc_ref, dt_ref, dtb_ref, alog_ref, cs_ref):
    xs = xbc_ref[:, 0:D_SSD]
    sp_in = dt_ref[...] + dtb_ref[...]
    dtf = _softplus(sp_in)
    a_f = -jnp.exp(alog_ref[...])
    a_dt = dtf * a_f
    row = lax.broadcasted_iota(jnp.int32, (Q, Q), 0)
    col = lax.broadcasted_iota(jnp.int32, (Q, Q), 1)
    causal = row >= col
    cs = _dot_exact(causal.astype(F32), a_dt, 3, "a")
    cs_ref[...] = cs
    cs_last = cs_ref[Q - 1:Q, :]
    return xs, sp_in, dtf, a_f, cs, cs_last, causal


def _head_decay(cs_j, cst_ref, e, causal):
    lane = lax.broadcasted_iota(jnp.int32, (Q, LANE), 1)
    rolled = pltpu.roll(cs_j, HEAD, 1)
    own = (lane < HEAD) if e == 0 else (lane >= HEAD)
    col_b = jnp.where(own, cs_j, rolled)
    col_b = jnp.concatenate([col_b] * (Q // LANE), axis=1)
    row_b = cst_ref[e * HEAD:e * HEAD + 1, :]
    return jnp.where(causal, jnp.exp(jnp.minimum(col_b - row_b, 0.0)), 0.0)


def ssd_fwd(xbc, z_src, dt_src, dtb_f, alog_f, dsk_f, snw, name):
    L = xbc.shape[0]
    nc = L // Q

    def body(xbc_ref, z_ref, dt_ref, dtb_ref, alog_ref, dsk_ref, snw_ref, y_ref, yn_ref, sp_ref, s_ref, cs_ref, cst_ref, yd_ref):
        @pl.when(pl.program_id(0) == 0)
        def _():
            s_ref[...] = jnp.zeros((N_STATE, D_SSD), F32)

        xs, _, dtf, a_f, cs, cs_last, causal = _ssd_common(xbc_ref, dt_ref, dtb_ref, alog_ref, cs_ref)
        e_cs = jnp.exp(cs)
        xdt = xs * dtf
        zst = jnp.exp(cs_last - cs) * xdt
        sp_ref[0] = s_ref[...]
        lane = lax.broadcasted_iota(jnp.int32, (Q, LANE), 1)
        for g in range(2):
            gl = slice(g * 512, g * 512 + 512)
            b_g = xbc_ref[:, D_SSD + g * N_STATE:D_SSD + (g + 1) * N_STATE]
            c_g = xbc_ref[:, D_SSD + 2 * N_STATE + g * N_STATE:D_SSD + 2 * N_STATE + (g + 1) * N_STATE]
            s_prev = s_ref[:, gl]
            cb = _dot_nt(c_g, b_g)
            yd_ref[:, gl] = e_cs[:, gl] * _dot(c_g, s_prev)
            for j in range(4):
                tl = slice(g * 512 + j * LANE, g * 512 + (j + 1) * LANE)
                cs_j = cs[:, tl]
                cst_ref[...] = cs_j.T
                x_j = xdt[:, tl]
                o0 = _dot(cb * _head_decay(cs_j, cst_ref, 0, causal), x_j)
                o1 = _dot(cb * _head_decay(cs_j, cst_ref, 1, causal), x_j)
                yd_ref[:, tl] += jnp.where(lane < HEAD, o0, o1)
            s_ref[:, gl] = jnp.exp(cs_last[:, gl]) * s_prev + _dot_tn(b_g, zst[:, gl])
        y = yd_ref[...] + xs * dsk_ref[...]
        y_ref[...] = y
        yz = y * _silu(z_ref[...])
        r = lax.rsqrt(jnp.mean(yz * yz, axis=-1, keepdims=True) + 1e-6)
        yn_ref[...] = (yz * r * snw_ref[...]).astype(BF16)

    chunk = lambda w, c: pl.BlockSpec((Q, w), lambda i: (i, c))
    return pl.pallas_call(
        body, name=name, grid=(nc,),
        out_shape=[_sds((L, D_SSD), F32), _sds((L, D_SSD), BF16), _sds((nc, N_STATE, D_SSD), F32)],
        in_specs=[chunk(D_XBC, 0), chunk(D, 0), chunk(D, 0)] + [_const_spec((1, D))] * 4,
        out_specs=[chunk(D, 0), chunk(D, 0), pl.BlockSpec((1, N_STATE, D_SSD), lambda i: (i, 0, 0))],
        scratch_shapes=[pltpu.VMEM((N_STATE, D_SSD), F32), pltpu.VMEM((Q, D_SSD), F32), pltpu.VMEM((LANE, Q), F32),
                        pltpu.VMEM((Q, D_SSD), F32)],
        compiler_params=_cparams(("arbitrary",)))(xbc, z_src, dt_src, dtb_f, alog_f, dsk_f, snw)


def ssd_bwd(dmixin, y, xbc, z_src, dt_src, s_prev_all, dtb_f, alog_f, dsk_f, snw, name):
    L = xbc.shape[0]
    nc = L // Q

    def body(dyn_ref, y_ref, xbc_ref, z_ref, dt_ref, sp_ref, dtb_ref, alog_ref, dsk_ref, snw_ref,
             dz_ref, ddt_ref, dxbc_ref, acc_ref, acc16_ref, ds_ref, cs_ref, cst_ref, dcs_ref, dx_ref):
        step = pl.program_id(0)

        @pl.when(step == 0)
        def _():
            ds_ref[...] = jnp.zeros((N_STATE, D_SSD), F32)
            acc_ref[...] = jnp.zeros((8, D), F32)

        z = z_ref[...]
        y = y_ref[...]
        sz = _sigmoid(z)
        siluz = z * sz
        yz = y * siluz
        r = lax.rsqrt(jnp.mean(yz * yz, axis=-1, keepdims=True) + 1e-6)
        n = yz * r
        dyn = dyn_ref[...]
        acc_ref[0:1, :] += _colsum(dyn * n)
        dn = dyn * snw_ref[...]
        dyz = r * (dn - n * jnp.mean(dn * n, axis=-1, keepdims=True))
        dy = dyz * siluz
        dz_ref[...] = (dyz * y * (sz * (1.0 + z * (1.0 - sz)))).astype(BF16)

        xs, sp_in, dtf, a_f, cs, cs_last, causal = _ssd_common(xbc_ref, dt_ref, dtb_ref, alog_ref, cs_ref)
        acc_ref[3:4, :] += _colsum(dy * xs)
        e_cs = jnp.exp(cs)
        xdt = xs * dtf
        dst = jnp.exp(cs_last - cs)
        zst = dst * xdt
        e_last = jnp.exp(cs_last)
        lane = lax.broadcasted_iota(jnp.int32, (Q, LANE), 1)
        ones = jnp.ones((Q, LANE), F32)
        dcs_last_parts = []
        for g in range(2):
            gl = slice(g * 512, g * 512 + 512)
            b_g = xbc_ref[:, D_SSD + g * N_STATE:D_SSD + (g + 1) * N_STATE]
            c_g = xbc_ref[:, D_SSD + 2 * N_STATE + g * N_STATE:D_SSD + 2 * N_STATE + (g + 1) * N_STATE]
            s_prev = sp_ref[0, :, gl]
            ds_g = ds_ref[:, gl]
            dy_g = dy[:, gl]
            cb = _dot_nt(c_g, b_g)
            y_off = e_cs[:, gl] * _dot(c_g, s_prev)
            edy = e_cs[:, gl] * dy_g
            d_c = _dot_nt(edy, s_prev)
            d_z = _dot(b_g, ds_g)
            d_b = _dot_nt(zst[:, gl], ds_g)
            t_g = d_z * zst[:, gl]
            dcs_ref[:, gl] = dy_g * y_off - t_g
            dx_ref[:, gl] = d_z * dst[:, gl]
            dcs_last_parts.append(_colsum(t_g) + _colsum(ds_g * s_prev) * e_last[:, gl])
            ds_ref[:, gl] = e_last[:, gl] * ds_g + _dot_tn(c_g, edy)
            dcb = jnp.zeros((Q, Q), F32)
            for j in range(4):
                tl = slice(g * 512 + j * LANE, g * 512 + (j + 1) * LANE)
                cs_j = cs[:, tl]
                cst_ref[...] = cs_j.T
                x_j = xdt[:, tl]
                dy_j = dy[:, tl]
                dx_j = jnp.zeros((Q, LANE), F32)
                dcs_j = jnp.zeros((Q, LANE), F32)
                for e in range(2):
                    own = (lane < HEAD) if e == 0 else (lane >= HEAD)
                    w_h = _head_decay(cs_j, cst_ref, e, causal)
                    g_h = cb * w_h
                    dy_m = jnp.where(own, dy_j, 0.0)
                    d_g = _dot_nt(dy_m, x_j)
                    dx_j = dx_j + _dot_tn(g_h, dy_m)
                    dcb = dcb + d_g * w_h
                    p_h = d_g * g_h
                    row_sums = _dot_exact(p_h, ones, 2, "b")
                    col_sums = _dot_exact(p_h, ones, 2, "b", (((0,), (0,)), ((), ())))
                    dcs_j = dcs_j + jnp.where(own, row_sums - col_sums, 0.0)
                dcs_ref[:, tl] += dcs_j * (1.0 / HEAD)
                dx_ref[:, tl] += dx_j
            d_c = d_c + _dot(dcb, b_g)
            d_b = d_b + _dot_tn(dcb, c_g)
            dxbc_ref[:, D_SSD + g * N_STATE:D_SSD + (g + 1) * N_STATE] = d_b
            dxbc_ref[:, D_SSD + 2 * N_STATE + g * N_STATE:D_SSD + 2 * N_STATE + (g + 1) * N_STATE] = d_c
        dcs_last = jnp.concatenate(dcs_last_parts, axis=1)
        anticausal = lax.broadcasted_iota(jnp.int32, (Q, Q), 0) <= lax.broadcasted_iota(jnp.int32, (Q, Q), 1)
        d_adt = _dot_exact(anticausal.astype(F32), dcs_ref[...], 3, "a") + dcs_last
        dx = dx_ref[...]
        acc_ref[2:3, :] += _colsum(d_adt * dtf) * a_f
        d_dtf = d_adt * a_f + dx * xs
        dxbc_ref[:, 0:D_SSD] = dx * dtf + dy * dsk_ref[...]
        d_raw = d_dtf * _sigmoid(sp_in)
        acc_ref[1:2, :] += _colsum(d_raw)
        head_of_lane = lax.broadcasted_iota(jnp.int32, (D_SSD, LANE), 0) // HEAD
        fold = (head_of_lane == lax.broadcasted_iota(jnp.int32, (D_SSD, LANE), 1)).astype(F32)
        ddt_ref[...] = _dot_exact(d_raw, fold, 2, "b").astype(BF16)

        @pl.when(step == nc - 1)
        def _():
            acc16_ref[...] = _dot_exact(acc_ref[...], fold, 3, "b")

    rchunk = lambda w, c: pl.BlockSpec((Q, w), lambda i: (nc - 1 - i, c))
    return pl.pallas_call(
        body, name=name, grid=(nc,),
        out_shape=[_sds((L, D_SSD), BF16), _sds((L, LANE), BF16), _sds((L, D_XBC), F32), _sds((8, D), F32), _sds((8, LANE), F32)],
        in_specs=[rchunk(D, 0), rchunk(D, 0), rchunk(D_XBC, 0), rchunk(D, 0), rchunk(D, 0),
                  pl.BlockSpec((1, N_STATE, D_SSD), lambda i: (nc - 1 - i, 0, 0))] + [_const_spec((1, D))] * 4,
        out_specs=[rchunk(D, 0), rchunk(LANE, 0), rchunk(D_XBC, 0), _const_spec((8, D)), _const_spec((8, LANE))],
        scratch_shapes=[pltpu.VMEM((N_STATE, D_SSD), F32), pltpu.VMEM((Q, D_SSD), F32), pltpu.VMEM((LANE, Q), F32),
                        pltpu.VMEM((Q, D_SSD), F32), pltpu.VMEM((Q, D_SSD), F32)],
        compiler_params=_cparams(("arbitrary",)))(dmixin, y, xbc, z_src, dt_src, s_prev_all, dtb_f, alog_f, dsk_f, snw)


def _adamw_math(w, g, m, v):
    m_n = ADAM_B1 * m + (1.0 - ADAM_B1) * g
    v_n = ADAM_B2 * v + (1.0 - ADAM_B2) * jnp.square(g)
    c1 = 1.0 - ADAM_B1 ** ADAM_STEP
    c2 = 1.0 - ADAM_B2 ** ADAM_STEP
    return -ADAM_LR * ((m_n / c1) / (jnp.sqrt(v_n / c2) + ADAM_EPS) + ADAM_WD * w), m_n, v_n


def _sum_slots(p_ref):
    acc = p_ref[0].astype(F32)
    for s in range(1, N_DEV):
        acc = acc + p_ref[s].astype(F32)
    return acc


def adamw_slots(w, slots, m, v, name):
    rows, cols = w.shape
    tc = 256

    def body(w_ref, s_ref, m_ref, v_ref, g_ref, d_ref, mo_ref, vo_ref):
        g_v = _sum_slots(s_ref)
        g_ref[...] = g_v
        d_ref[...], mo_ref[...], vo_ref[...] = _adamw_math(w_ref[...], g_v, m_ref[...], v_ref[...])

    spec = pl.BlockSpec((rows, tc), lambda i: (0, i))
    return pl.pallas_call(body, name=name, grid=(cols // tc,), out_shape=[_sds((rows, cols), F32)] * 4,
                          in_specs=[spec, pl.BlockSpec((N_DEV, rows, tc), lambda i: (0, 0, i)), spec, spec], out_specs=[spec] * 4,
                          compiler_params=_cparams(("parallel",)))(w, slots, m, v)


def adamw_many(ws, gs, ms, vs, name):
    n = len(ws)

    def body(*refs):
        for p in range(n):
            d_v, m_v, v_v = _adamw_math(refs[p][...], refs[n + p][...], refs[2 * n + p][...], refs[3 * n + p][...])
            refs[4 * n + p][...] = d_v
            refs[5 * n + p][...] = m_v
            refs[6 * n + p][...] = v_v

    vm = pl.BlockSpec(memory_space=pltpu.VMEM)
    out = pl.pallas_call(body, name=name, out_shape=[_sds(w.shape, F32) for w in ws] * 3, in_specs=[vm] * (4 * n),
                         out_specs=[vm] * (3 * n), compiler_params=_cparams())(*ws, *gs, *ms, *vs)
    return out[:n], out[n:2 * n], out[2 * n:]


def _pack_layout(shapes):
    row, layout = 0, []
    for rows, cols in shapes:
        chunks = []
        for c0 in range(0, cols, D):
            chunks.append((row, c0, min(D, cols - c0)))
            row += rows
        layout.append(chunks)
    return row, layout


def pack_rows(arrays, name):
    total, layout = _pack_layout([a.shape for a in arrays])
    n = len(arrays)

    def body(*refs):
        o_ref = refs[n]
        o_ref[...] = jnp.zeros((total, D), F32)
        for p in range(n):
            rows = arrays[p].shape[0]
            for r0, c0, w in layout[p]:
                o_ref[r0:r0 + rows, 0:w] = refs[p][:, c0:c0 + w]

    vm = pl.BlockSpec(memory_space=pltpu.VMEM)
    return pl.pallas_call(body, name=name, out_shape=_sds((total, D), F32), in_specs=[vm] * n, out_specs=vm,
                          compiler_params=_cparams())(*arrays)


def unpack_rows(packed, shapes):
    _, layout = _pack_layout(shapes)
    out = []
    for (rows, _), chunks in zip(shapes, layout):
        parts = [packed[..., r0:r0 + rows, 0:w] for r0, _, w in chunks]
        out.append(parts[0] if len(parts) == 1 else jnp.concatenate(parts, axis=-1))
    return out


def sum_slots_many(parts, name):
    n = len(parts)

    def body(*refs):
        for p in range(n):
            refs[n + p][...] = _sum_slots(refs[p])

    vm = pl.BlockSpec(memory_space=pltpu.VMEM)
    return pl.pallas_call(body, name=name, out_shape=[_sds(p.shape[1:], F32) for p in parts], in_specs=[vm] * n,
                          out_specs=[vm] * n, compiler_params=_cparams())(*parts)


def ada_mod(c_all, ada_w_shard, ada_b_cols, name):
    def body(c_ref, w_ref, b_ref, o_ref, ca_ref):
        ca = _silu(c_ref[...])
        ca_ref[...] = ca
        o_ref[...] = _dot(ca, w_ref[...]) + b_ref[...]

    vm = pl.BlockSpec(memory_space=pltpu.VMEM)
    return pl.pallas_call(body, name=name, out_shape=[_sds((N_DEV, ada_w_shard.shape[1]), F32), _sds((N_DEV, D), F32)],
                          in_specs=[vm, vm, vm], out_specs=[vm, vm], compiler_params=_cparams())(c_all, ada_w_shard, ada_b_cols)


def ada_wgrad(c_act_all, dmod_cols, name):
    def body(c_ref, d_ref, o_ref):
        o_ref[...] = _dot_tn_hi(c_ref[...], d_ref[...])

    vm = pl.BlockSpec(memory_space=pltpu.VMEM)
    return pl.pallas_call(body, name=name, out_shape=_sds((D, dmod_cols.shape[1]), F32), in_specs=[vm, vm], out_specs=vm,
                          compiler_params=_cparams())(c_act_all, dmod_cols)


def exchange(srcs, name, gather):
    n = len(srcs)
    shapes = [tuple(s.shape) if gather else tuple(s.shape[1:]) for s in srcs]

    def body(*refs):
        src_refs, out_refs = refs[:n], refs[n:2 * n]
        send_sems, recv_sems, local_sems = refs[2 * n:]
        x, y, c = lax.axis_index("x"), lax.axis_index("y"), lax.axis_index("c")
        me = 4 * x + 2 * y + c

        def peer(k):
            bx, by, bc = (k >> 2) & 1, (k >> 1) & 1, k & 1
            px, py, pc = (x + bx) % 2, (y + by) % 2, (c + bc) % 2
            return (px, py, pc), 4 * px + 2 * py + pc

        def copy(a, k, landing):
            dev, idx = peer(k)
            return pltpu.make_async_remote_copy(
                src_ref=src_refs[a] if gather else src_refs[a].at[idx], dst_ref=out_refs[a].at[idx if landing else me],
                send_sem=send_sems.at[a, k - 1], recv_sem=recv_sems.at[a, k - 1],
                device_id=dev, device_id_type=pl.DeviceIdType.MESH)

        mine = [pltpu.make_async_copy(src_refs[a] if gather else src_refs[a].at[me], out_refs[a].at[me], local_sems.at[a])
                for a in range(n)]
        for cp in mine:
            cp.start()
        sends = [copy(a, k, False) for a in range(n) for k in range(1, N_DEV)]
        for cp in sends:
            cp.start()
        for a in range(n):
            for k in range(1, N_DEV):
                copy(a, k, True).wait_recv()
        for cp in sends:
            cp.wait_send()
        for cp in mine:
            cp.wait()

    hbm = pl.BlockSpec(memory_space=pl.ANY)
    return pl.pallas_call(
        body, name=name, out_shape=[_sds((N_DEV,) + shp, s.dtype) for shp, s in zip(shapes, srcs)], in_specs=[hbm] * n,
        out_specs=[hbm] * n,
        scratch_shapes=[pltpu.SemaphoreType.DMA((n, N_DEV - 1)), pltpu.SemaphoreType.DMA((n, N_DEV - 1)),
                        pltpu.SemaphoreType.DMA((n,))],
        compiler_params=pltpu.CompilerParams(has_side_effects=True))(*srcs)


def gather_two_level(src, name):
    def body(src_ref, out_ref, send_sems, recv_sems, local_sem):
        x, y, c = lax.axis_index("x"), lax.axis_index("y"), lax.axis_index("c")
        me, sibling = (x, y, c), (x, y, 1 - c)
        chips = [(1 - x, y), (x, 1 - y), (1 - x, 1 - y)]

        def slot(px, py, pc):
            return out_ref.at[4 * px + 2 * py + pc]

        def copy(k, block, to, src=None):
            return pltpu.make_async_remote_copy(
                src_ref=slot(*block) if src is None else src, dst_ref=slot(*block), send_sem=send_sems.at[k],
                recv_sem=recv_sems.at[k], device_id=to, device_id_type=pl.DeviceIdType.MESH)

        mine = pltpu.make_async_copy(src_ref, slot(*me), local_sem)
        mine.start()
        first = [copy(0, me, sibling, src=src_ref)]
        first += [copy(1 + j, me, (*chip, c), src=src_ref) for j, chip in enumerate(chips)]
        for cp in first:
            cp.start()
        passed = [copy(4 + j, (*chip, c), sibling) for j, chip in enumerate(chips)]
        for j, chip in enumerate(chips):
            copy(1 + j, (*chip, c), me).wait_recv()
            passed[j].start()
        copy(0, sibling, me).wait_recv()
        for j, chip in enumerate(chips):
            copy(4 + j, (*chip, 1 - c), me).wait_recv()
        for cp in first + passed:
            cp.wait_send()
        mine.wait()

    hbm = pl.BlockSpec(memory_space=pl.ANY)
    return pl.pallas_call(
        body, name=name, out_shape=_sds((N_DEV,) + tuple(src.shape), src.dtype), in_specs=[hbm], out_specs=hbm,
        scratch_shapes=[pltpu.SemaphoreType.DMA((N_DEV - 1,)), pltpu.SemaphoreType.DMA((N_DEV - 1,)), pltpu.SemaphoreType.DMA],
        compiler_params=pltpu.CompilerParams(has_side_effects=True))(src)


def _peer(k):
    x, y, c = lax.axis_index("x"), lax.axis_index("y"), lax.axis_index("c")
    px, py, pc = (x + ((k >> 2) & 1)) % 2, (y + ((k >> 1) & 1)) % 2, (c + (k & 1)) % 2
    return (px, py, pc), 4 * px + 2 * py + pc


def _my_slot():
    return 4 * lax.axis_index("x") + 2 * lax.axis_index("y") + lax.axis_index("c")


_HBM = pl.BlockSpec(memory_space=pltpu.HBM)
_SEM = pl.BlockSpec(memory_space=pltpu.SEMAPHORE)
_EFFECT = pltpu.SideEffectType.DATAFLOW_SIDE_EFFECTING


def exchange_start(srcs, name, gather):
    n = len(srcs)
    shapes = [tuple(s.shape) if gather else tuple(s.shape[1:]) for s in srcs]
    lands = [lax.empty((N_DEV,) + shp, s.dtype) for shp, s in zip(shapes, srcs)]

    def body(*refs):
        src_refs, land_refs = refs[:n], refs[n:2 * n]
        sems = refs[2 * n:4 * n]
        token = refs[-1]
        me = _my_slot()
        for a in range(n):
            for k in range(1, N_DEV):
                dev, idx = _peer(k)
                pltpu.make_async_remote_copy(
                    src_ref=src_refs[a] if gather else src_refs[a].at[idx], dst_ref=land_refs[a].at[me],
                    send_sem=sems[2 * a].at[k - 1], recv_sem=sems[2 * a + 1].at[k - 1],
                    device_id=dev, device_id_type=pl.DeviceIdType.MESH).start()
        token[...] = jnp.zeros_like(token)

    out_shape = ([pltpu.SemaphoreType.DMA((N_DEV - 1,))] * (2 * n) + [pltpu.HBM(s.shape, s.dtype) for s in srcs]
                 + [pltpu.HBM(l.shape, l.dtype) for l in lands] + [_sds((8, LANE), F32)])
    out = pl.pallas_call(
        body, name=name, out_shape=out_shape, in_specs=[_HBM] * (2 * n),
        out_specs=[_SEM] * (2 * n) + [_HBM] * (2 * n) + [pl.BlockSpec(memory_space=pltpu.VMEM)],
        input_output_aliases={i: 2 * n + i for i in range(2 * n)},
        compiler_params=pltpu.CompilerParams(has_side_effects=_EFFECT))(
            *[pltpu.with_memory_space_constraint(s, pltpu.HBM) for s in srcs],
            *[pltpu.with_memory_space_constraint(l, pltpu.HBM) for l in lands])
    parts = [(out[2 * a], out[2 * a + 1], out[2 * n + a], out[3 * n + a]) for a in range(n)]
    return parts, out[-1]


def exchange_wait(parts, after, name, gather):
    n = len(parts)

    def body(*refs):
        src_refs, land_refs = refs[:n], refs[n:2 * n]
        sems = refs[2 * n:4 * n]
        for a in range(n):
            for k in range(1, N_DEV):
                dev, idx = _peer(k)
                copy = pltpu.make_async_remote_copy(
                    src_ref=src_refs[a] if gather else src_refs[a].at[idx], dst_ref=land_refs[a].at[idx],
                    send_sem=sems[2 * a].at[k - 1], recv_sem=sems[2 * a + 1].at[k - 1],
                    device_id=dev, device_id_type=pl.DeviceIdType.MESH)
                copy.wait_send()
                copy.wait_recv()

    srcs = [p[2] for p in parts]
    lands = [p[3] for p in parts]
    sems = [s for p in parts for s in p[:2]]
    out = pl.pallas_call(
        body, name=name, out_shape=[pltpu.HBM(a.shape, a.dtype) for a in srcs + lands],
        in_specs=[_HBM] * (2 * n) + [_SEM] * (2 * n) + [pl.BlockSpec(memory_space=pl.ANY)], out_specs=[_HBM] * (2 * n),
        input_output_aliases={i: i for i in range(2 * n)},
        compiler_params=pltpu.CompilerParams(has_side_effects=_EFFECT))(*srcs, *lands, *sems, after)
    return out[n:]


def _slabs_to_cols(s):
    _, r, cs = s.shape
    return s.transpose(1, 0, 2).reshape(r, N_DEV * cs)


def _rep_heads(v):
    return jnp.repeat(v.reshape(N_HEADS), HEAD).reshape(1, D_SSD)


def local_fwd_bwd(x, target, mod, get_w, put_grad, small):
    n1w, n2w, fnw = small["norm1_w"], small["norm2_w"], small["final_norm_w"]
    dtb_f, alog_f, dsk_f = _rep_heads(small["dt_bias"]), _rep_heads(small["a_log"]), _rep_heads(small["d_skip"])
    snw = small["ssd_norm_w"]

    def after(v, token):
        return v + token[0:1, 0:1]

    h1 = norm_mod(x, mod, n1w, 0, "norm1")
    w_in = get_w("w_in", h1)
    proj_zx = mm_nt([(h1, w_in["w_full"], 0)], "in_proj_zx", rows=D_SSD + D_XBC)
    proj_dt = mm_nt([(h1, w_in["w_dt_rep"], 0)], "in_proj_dt")
    proj_cf = mm_nt([(h1, w_in["w_cf"], 0)], "in_proj_conf")
    xbc = conv_silu_fwd(proj_zx, D_SSD // CB, D_XBC, small["ssd_conv_w"], small["ssd_conv_b"], "ssd_conv")
    y, ysn, s_prev = ssd_fwd(xbc, proj_zx, proj_dt, dtb_f, alog_f, dsk_f, snw, "ssd_scan")
    uc = conf_conv_fwd(proj_cf, 0, D_CONF // CB, small["conf_conv_w"], small["conf_conv_b"], "conf_conv")
    w_out = get_w("w_out", uc)
    mix, u, x1, h2 = mixer_out(ysn, uc, small["conf_ln_w"], small["conf_ln_b"], w_out, x, mod, n2w, "out_proj_norm2")
    w_up_t = get_w("w_up", h2)
    up = mm_nt([(h2, w_up_t, 0)], "up_proj")
    act = ffn_conv_fwd(up, small["ffn_conv_w"], small["ffn_conv_b"], "ffn_conv")
    w_down = get_w("w_down", act)
    dx2, dff, dact, acc_f = final_loss(act, w_down, x1, mod, fnw, target, "down_proj_loss")

    token = put_grad("w_down", mm_tn(act, dff, "wgrad_down"))
    dupg, dupv, dwg, dwv = ffn_conv_bwd(up, small["ffn_conv_w"], after(small["ffn_conv_b"], token), dact, "ffn_conv_bwd")
    token = put_grad("w_up", jnp.concatenate([mm_tn(dupg, h2, "wgrad_up_gate"), mm_tn(dupv, h2, "wgrad_up_val")], axis=0))
    dx1, dmix, acc_2 = norm_mod_bwd([(dupg, w_up_t, 0), (dupv, w_up_t, 1)], x1, dx2, mod, after(n2w, token), 3, "norm2_bwd",
                                    mix=mix, gate_row=2)

    token = put_grad("w_out", jnp.concatenate([mm_tn(ysn, dmix, "wgrad_out_ssd"), mm_tn(u, dmix, "wgrad_out_conf")], axis=0))
    dmixin = mm_nt([(dmix, w_out, 0)], "dmixin")
    duc, acc_ln = ln_silu_bwd(dmixin, uc, after(small["conf_ln_w"], token), small["conf_ln_b"], "conf_ln_bwd")
    dcfa, dcfg, dw_cc = conf_conv_bwd(proj_cf, 0, D_CONF // CB, small["conf_conv_w"], duc, "conf_conv_bwd")
    dz, ddt, dxbc_post, acc_s, acc_s16 = ssd_bwd(dmixin, y, xbc, proj_zx, proj_dt, s_prev, dtb_f, alog_f, dsk_f, snw,
                                                 "ssd_scan_bwd")
    dxbc, dw_sc = conv_silu_bwd(proj_zx, D_SSD // CB, D_XBC, small["ssd_conv_w"], small["ssd_conv_b"], dxbc_post, "ssd_conv_bwd")
    token = put_grad("w_in", jnp.concatenate(
        [mm_tn(dz, h1, "wgrad_in_z"), mm_tn(dxbc, h1, "wgrad_in_xbc"), mm_tn(ddt, h1, "wgrad_in_dt")[:N_HEADS],
         mm_tn(dcfa, h1, "wgrad_in_cfa"), mm_tn(dcfg, h1, "wgrad_in_cfg")], axis=0))
    dh1_pairs = [(dz, w_in["w_full"], 0), (ddt, w_in["w_dt16"], 0), (dcfa, w_in["w_cf"], 0), (dcfg, w_in["w_cf"], 1),
                 (dxbc, w_in["w_xbc"], 0)]
    grad_x, acc_1 = norm_mod_bwd(dh1_pairs, x, dx1, mod, after(n1w, token), 0, "norm1_bwd")

    small_accs = dict(acc_1=acc_1, acc_2=acc_2, acc_f=acc_f, acc_ln=acc_ln, acc_s=acc_s, acc_s16=acc_s16, dw_sc=dw_sc,
                      dw_cc=dw_cc, dwg=dwg, dwv=dwv)
    return grad_x, small_accs


def kernel(x, c, ada_w, ada_b, norm1_w, w_in, ssd_conv_w, ssd_conv_b, dt_bias, a_log, d_skip, ssd_norm_w, conf_conv_w, conf_conv_b, conf_ln_w, conf_ln_b, w_out, norm2_w, w_up, ffn_conv_w, ffn_conv_b, w_down, final_norm_w, loss_target, m_ada_w, m_ada_b, m_norm1_w, m_w_in, m_ssd_conv_w, m_ssd_conv_b, m_dt_bias, m_a_log, m_d_skip, m_ssd_norm_w, m_conf_conv_w, m_conf_conv_b, m_conf_ln_w, m_conf_ln_b, m_w_out, m_norm2_w, m_w_up, m_ffn_conv_w, m_ffn_conv_b, m_w_down, m_final_norm_w, v_ada_w, v_ada_b, v_norm1_w, v_w_in, v_ssd_conv_w, v_ssd_conv_b, v_dt_bias, v_a_log, v_d_skip, v_ssd_norm_w, v_conf_conv_w, v_conf_conv_b, v_conf_ln_w, v_conf_ln_b, v_w_out, v_norm2_w, v_w_up, v_ffn_conv_w, v_ffn_conv_b, v_w_down, v_final_norm_w):
    me = 4 * lax.axis_index("x") + 2 * lax.axis_index("y") + lax.axis_index("c")
    weights = dict(ada_w=ada_w, ada_b=ada_b, norm1_w=norm1_w, w_in=w_in, ssd_conv_w=ssd_conv_w, ssd_conv_b=ssd_conv_b,
                   dt_bias=dt_bias, a_log=a_log, d_skip=d_skip, ssd_norm_w=ssd_norm_w, conf_conv_w=conf_conv_w,
                   conf_conv_b=conf_conv_b, conf_ln_w=conf_ln_w, conf_ln_b=conf_ln_b, w_out=w_out, norm2_w=norm2_w, w_up=w_up,
                   ffn_conv_w=ffn_conv_w, ffn_conv_b=ffn_conv_b, w_down=w_down, final_norm_w=final_norm_w)
    moms_m = dict(ada_w=m_ada_w, ada_b=m_ada_b, norm1_w=m_norm1_w, w_in=m_w_in, ssd_conv_w=m_ssd_conv_w, ssd_conv_b=m_ssd_conv_b,
                  dt_bias=m_dt_bias, a_log=m_a_log, d_skip=m_d_skip, ssd_norm_w=m_ssd_norm_w, conf_conv_w=m_conf_conv_w,
                  conf_conv_b=m_conf_conv_b, conf_ln_w=m_conf_ln_w, conf_ln_b=m_conf_ln_b, w_out=m_w_out, norm2_w=m_norm2_w,
                  w_up=m_w_up, ffn_conv_w=m_ffn_conv_w, ffn_conv_b=m_ffn_conv_b, w_down=m_w_down, final_norm_w=m_final_norm_w)
    moms_v = dict(ada_w=v_ada_w, ada_b=v_ada_b, norm1_w=v_norm1_w, w_in=v_w_in, ssd_conv_w=v_ssd_conv_w, ssd_conv_b=v_ssd_conv_b,
                  dt_bias=v_dt_bias, a_log=v_a_log, d_skip=v_d_skip, ssd_norm_w=v_ssd_norm_w, conf_conv_w=v_conf_conv_w,
                  conf_conv_b=v_conf_conv_b, conf_ln_w=v_conf_ln_w, conf_ln_b=v_conf_ln_b, w_out=v_w_out, norm2_w=v_norm2_w,
                  w_up=v_w_up, ffn_conv_w=v_ffn_conv_w, ffn_conv_b=v_ffn_conv_b, w_down=v_w_down, final_norm_w=v_final_norm_w)
    names = list(weights)

    def to2d(a):
        return a[0] if a.ndim == 3 else a.reshape(1, -1)

    big = ("w_in", "w_out", "w_up", "w_down")

    c_all, scw_all, ccw_all, fcw_all = exchange([c.reshape(8, LANE), ssd_conv_w[0], conf_conv_w[0], ffn_conv_w[0]],
                                                "gather_small", gather=True)
    c_all = c_all.reshape(N_DEV, D)

    ada_cols = ada_w.shape[2]
    ada_b_cols = lax.dynamic_slice(ada_b, (0, me * ada_cols), (1, ada_cols))
    mod_cols, c_act_all = ada_mod(c_all, ada_w[0], ada_b_cols, "ada_mod")
    mod_parts, = exchange([jnp.pad(mod_cols, ((0, 0), (0, D - ada_cols))).reshape(N_DEV, 8, LANE)], "scatter_mod", gather=False)
    mod = mod_parts.reshape(N_DEV, D)[:, :ada_cols].reshape(6, D)
    mod = jnp.pad(mod, ((0, 2), (0, 0)))

    def rows_of(a):
        return jnp.swapaxes(a, 1, 2)[0] if a.shape[2] != D else a[0]

    shards, mod = lax.optimization_barrier(([rows_of(weights[n]).astype(BF16) for n in big], mod))
    w_in_slabs = gather_two_level(shards[0], "gather_w_in")
    later, w_in_slabs = lax.optimization_barrier((shards[1:], w_in_slabs))
    gather_parts, token = exchange_start(later, "gather_weights_start", gather=True)
    mod = mod + token[0:1, 0:1]

    small = {n: to2d(weights[n]) for n in names if n not in ("ada_w",) + big}
    small["ssd_conv_w"] = _slabs_to_cols(scw_all)
    small["conf_conv_w"] = _slabs_to_cols(ccw_all)
    small["ffn_conv_w"] = _slabs_to_cols(fcw_all)

    def with_own(landed, own):
        return lax.dynamic_update_slice(landed, own[None], (me,) + (0,) * own.ndim)

    def get_w(n, after):
        if n == "w_in":
            slabs = w_in_slabs
        else:
            a = big.index(n)
            landed, = exchange_wait([gather_parts[a - 1]], after, "gather_" + n + "_wait", gather=True)
            slabs = with_own(landed, shards[a])
        full = slabs.reshape(N_DEV * slabs.shape[1], D)
        if n != "w_in":
            return full
        w_dt = full[D_SSD + D_XBC:D_SSD + D_XBC + N_HEADS]
        return dict(w_full=full, w_xbc=full[D_SSD:D_SSD + D_XBC], w_cf=full[D_SSD + D_XBC + N_HEADS:],
                    w_dt_rep=jnp.repeat(w_dt, HEAD, axis=0), w_dt16=jnp.pad(w_dt, ((0, LANE - N_HEADS), (0, 0))))

    scatter_parts, sent = {}, {}

    def put_grad(n, g):
        sent[n] = g.reshape(N_DEV, g.shape[0] // N_DEV, g.shape[1]).astype(BF16)
        (scatter_parts[n],), token = exchange_start([sent[n]], "scatter_" + n + "_start", gather=False)
        return token

    grad_x, accs = local_fwd_bwd(x[0], loss_target[0], mod, get_w, put_grad, small)
    loss = lax.psum(0.5 / D * jnp.sum(accs["acc_f"][2:3]), ("x", "y", "c"))

    grads, delta, new_m, new_v = {}, {}, {}, {}

    def finish(ns, after, name):
        landed = exchange_wait([scatter_parts[n] for n in ns], after, name, gather=False)
        for n, slots in zip(ns, landed):
            slots = with_own(slots, lax.dynamic_index_in_dim(sent[n], me, 0, keepdims=False))
            out = adamw_slots(rows_of(weights[n]), slots, rows_of(moms_m[n]), rows_of(moms_v[n]), "adamw_" + n)
            if weights[n].shape[2] != D:
                out = [jnp.swapaxes(o, 0, 1) for o in out]
            grads[n], delta[n], new_m[n], new_v[n] = out

    finish(big[1:], grad_x, "scatter_grads_wait")

    order = ("acc_1", "acc_2", "acc_f", "acc_ln", "acc_s", "acc_s16", "dw_sc", "dw_cc", "dwg", "dwv")
    shapes = [accs[k].shape for k in order]
    acc_list, _ = lax.optimization_barrier(([accs[k] for k in order], [new_v[n] for n in big[1:]]))
    packed_all, = exchange([pack_rows(acc_list, "pack_small_grads")], "gather_small_grads", gather=True)
    packed_red, = sum_slots_many([packed_all], "sum_small_grads")
    gathered = dict(zip(order, unpack_rows(packed_all, shapes)))
    red = dict(zip(order, unpack_rows(packed_red, shapes)))

    def mod_rows(a1, a2, af):
        return jnp.concatenate([a1[..., 0:2, :], a2[..., 3:4, :], a2[..., 0:2, :], af[..., 1:2, :]], axis=-2)

    dmod_all = mod_rows(gathered["acc_1"], gathered["acc_2"], gathered["acc_f"]).reshape(N_DEV, 6 * D)
    grads["ada_w"] = ada_wgrad(c_act_all, lax.dynamic_slice(dmod_all, (0, me * ada_cols), (N_DEV, ada_cols)), "ada_wgrad")

    def my_cols(full, k_taps):
        cols = full.shape[1] // N_DEV
        return lax.dynamic_slice(full, (0, me * cols), (k_taps, cols))

    fcw = jnp.concatenate([red["dwg"], red["dwv"]], axis=1)
    grads.update(
        ada_b=mod_rows(red["acc_1"], red["acc_2"], red["acc_f"]).reshape(1, 6 * D), norm1_w=red["acc_1"][2:3],
        ssd_conv_w=my_cols(red["dw_sc"], K_SSD), ssd_conv_b=red["dw_sc"][K_SSD:K_SSD + 1],
        dt_bias=red["acc_s16"][1:2, :N_HEADS], a_log=red["acc_s16"][2:3, :N_HEADS], d_skip=red["acc_s16"][3:4, :N_HEADS],
        ssd_norm_w=red["acc_s"][0:1], conf_conv_w=my_cols(red["dw_cc"], K_CONF), conf_conv_b=red["dw_cc"][K_CONF:K_CONF + 1],
        conf_ln_w=red["acc_ln"][0:1], conf_ln_b=red["acc_ln"][1:2], norm2_w=red["acc_2"][2:3],
        ffn_conv_w=my_cols(fcw, K_FFN), ffn_conv_b=fcw[K_FFN:K_FFN + 1], final_norm_w=red["acc_f"][0:1])

    rest = [n for n in names if n not in big]
    d_l, m_l, v_l = adamw_many([to2d(weights[n]) for n in rest], [grads[n] for n in rest], [to2d(moms_m[n]) for n in rest],
                               [to2d(moms_v[n]) for n in rest], "adamw_small")
    for n, dd, mm, vv in zip(rest, d_l, m_l, v_l):
        delta[n], new_m[n], new_v[n] = dd, mm, vv
    finish(big[:1], d_l[0], "scatter_w_in_wait")
    shape_of = lambda d_: {n: d_[n].reshape(weights[n].shape) for n in names}
    grads, delta, new_m, new_v = shape_of(grads), shape_of(delta), shape_of(new_m), shape_of(new_v)
    return (loss, grad_x[None], *[grads[n] for n in names], *[delta[n] for n in names], *[new_m[n] for n in names],
            *[new_v[n] for n in names])
```

```python
import functools

import jax
import jax.numpy as jnp
from jax import lax
from jax.experimental import pallas as pl
from jax.experimental.pallas import tpu as pltpu

F32 = jnp.float32
BF16 = jnp.bfloat16
HI = lax.Precision.HIGHEST

N_DEV = 8
D = 1024
D_SSD = 1024
HEAD = 64
N_HEADS = 16
N_STATE = 128
D_XBC = 1536
D_CONF = 1024
D_FF = 2816
K_SSD, K_CONF, K_FFN = 4, 31, 3
D_INP = 5632
LANE = 128
TR = 256
TM = 512
Q = 256
CB = 256
TC = 1024
VMEM_LIMIT = 56 * 1024 * 1024

ADAM_LR, ADAM_B1, ADAM_B2, ADAM_EPS, ADAM_WD, ADAM_STEP = 0.001, 0.9, 0.999, 1e-08, 0.01, 10


def _cparams(sem=None):
    return pltpu.CompilerParams(vmem_limit_bytes=VMEM_LIMIT, dimension_semantics=sem)


def _sds(shape, dtype):
    return jax.ShapeDtypeStruct(shape, dtype)


def _sigmoid(x):
    return 1.0 / (1.0 + jnp.exp(-x))


def _silu(x):
    return x * _sigmoid(x)


def _dsilu(x):
    s = _sigmoid(x)
    return s * (1.0 + x * (1.0 - s))


def _softplus(x):
    return jnp.maximum(x, 0.0) + jnp.log(1.0 + jnp.exp(-jnp.abs(x)))


def _dot(a, b):
    return jnp.dot(a.astype(BF16), b.astype(BF16), preferred_element_type=F32)


def _dot_nt(a, b):
    return lax.dot_general(a.astype(BF16), b.astype(BF16), (((1,), (1,)), ((), ())), preferred_element_type=F32)


def _dot_tn(a, b):
    return lax.dot_general(a.astype(BF16), b.astype(BF16), (((0,), (0,)), ((), ())), preferred_element_type=F32)


def _bf16_terms(a, terms):
    parts, rem = [], a
    for t in range(terms):
        p = rem.astype(BF16)
        parts.append(p)
        if t + 1 < terms:
            rem = rem - p.astype(F32)
    return parts


def _dot_exact(a, b, terms, exact, dims=(((1,), (0,)), ((), ()))):
    if exact == "a":
        a_b = a.astype(BF16)
        outs = [lax.dot_general(a_b, p, dims, preferred_element_type=F32) for p in _bf16_terms(b, terms)]
    else:
        b_b = b.astype(BF16)
        outs = [lax.dot_general(p, b_b, dims, preferred_element_type=F32) for p in _bf16_terms(a, terms)]
    acc = outs[-1]
    for o in reversed(outs[:-1]):
        acc = acc + o
    return acc


def _dot_tn_hi(a, b):
    return lax.dot_general(a, b, (((0,), (0,)), ((), ())), precision=HI, preferred_element_type=F32)


def _colsum(x):
    return jnp.sum(x, axis=0, keepdims=True)


def _const_spec(shape):
    return pl.BlockSpec(shape, lambda *_: (0,) * len(shape))


def _col_tile(n):
    for t in (1408, 1024, 768, 512, 256, 128):
        if n % t == 0 and t <= n:
            return t
    return n


def mm_nt(pairs, name, rows=None):
    L = pairs[0][0].shape[0]
    K = pairs[0][1].shape[0] if rows is None else rows
    tk = _col_tile(K)
    n = len(pairs)

    def body(*refs):
        o_ref = refs[-1]
        acc = None
        for p in range(n):
            t = lax.dot_general(refs[2 * p][...], refs[2 * p + 1][...], (((1,), (1,)), ((), ())),
                                preferred_element_type=F32)
            acc = t if acc is None else acc + t
        o_ref[...] = acc

    in_specs, args = [], []
    for a, w, cb in pairs:
        in_specs += [pl.BlockSpec((TM, a.shape[1]), lambda j, i: (i, 0)),
                     pl.BlockSpec((tk, a.shape[1]), functools.partial(lambda j, i, cb: (j, cb), cb=cb))]
        args += [a, w]
    return pl.pallas_call(
        body, name=name, grid=(K // tk, L // TM), out_shape=_sds((L, K), F32), in_specs=in_specs,
        out_specs=pl.BlockSpec((TM, tk), lambda j, i: (i, j)),
        compiler_params=_cparams(("parallel", "parallel")))(*args)


def mm_tn(a, g, name):
    L, M = a.shape
    N = g.shape[1]
    tn = _col_tile(N) if N > 1024 else N
    if M * tn * 4 > 8 * 1024 * 1024:
        tn = 512
    tl = 512 if L % 512 == 0 else TR
    nl = L // tl

    def body(a_ref, g_ref, o_ref, acc_ref):
        @pl.when(pl.program_id(1) == 0)
        def _():
            acc_ref[...] = jnp.zeros((M, tn), F32)

        acc_ref[...] += lax.dot_general(a_ref[...], g_ref[...], (((0,), (0,)), ((), ())), preferred_element_type=F32)

        @pl.when(pl.program_id(1) == nl - 1)
        def _():
            o_ref[...] = acc_ref[...].astype(BF16)

    return pl.pallas_call(
        body, name=name, grid=(N // tn, nl), out_shape=_sds((M, N), BF16),
        in_specs=[pl.BlockSpec((tl, M), lambda j, l: (l, 0)), pl.BlockSpec((tl, tn), lambda j, l: (l, j))],
        out_specs=pl.BlockSpec((M, tn), lambda j, l: (0, j)), scratch_shapes=[pltpu.VMEM((M, tn), F32)],
        compiler_params=_cparams(("parallel", "arbitrary")))(a, g)


def _row_spec(width=D):
    return pl.BlockSpec((TR, width), lambda i: (i, 0))


def _row_col_spec(width, col):
    return pl.BlockSpec((TR, width), lambda i: (i, col))


def norm_mod(x, mod, w, shift_row, name):
    L = x.shape[0]

    def body(x_ref, mod_ref, w_ref, h_ref):
        xin = x_ref[...]
        r = lax.rsqrt(jnp.mean(xin * xin, axis=-1, keepdims=True) + 1e-6)
        h = (xin * r * w_ref[...]) * (1.0 + mod_ref[shift_row + 1:shift_row + 2, :]) + mod_ref[shift_row:shift_row + 1, :]
        h_ref[...] = h.astype(BF16)

    return pl.pallas_call(body, name=name, grid=(L // TR,), out_shape=_sds((L, D), BF16),
                          in_specs=[_row_spec(), _const_spec((8, D)), _const_spec((1, D))], out_specs=_row_spec(),
                          compiler_params=_cparams(("parallel",)))(x, mod, w)


def mixer_out(ysn, uc, lnw, lnb, w_out, x, mod, n2w, name):
    L = x.shape[0]

    def body(ysn_ref, uc_ref, lnw_ref, lnb_ref, wo_ref, x_ref, mod_ref, n2w_ref, mix_ref, u_ref, x1_ref, h2_ref):
        uc_v = uc_ref[...]
        mu = jnp.mean(uc_v, axis=-1, keepdims=True)
        var = jnp.mean(jnp.square(uc_v - mu), axis=-1, keepdims=True)
        u = _silu((uc_v - mu) * lax.rsqrt(var + 1e-5) * lnw_ref[...] + lnb_ref[...]).astype(BF16)
        u_ref[...] = u
        mix = (jnp.dot(ysn_ref[...], wo_ref[0:D_SSD, :], preferred_element_type=F32)
               + jnp.dot(u, wo_ref[D_SSD:D_SSD + D_CONF, :], preferred_element_type=F32))
        mix_ref[...] = mix
        x1 = x_ref[...] + mod_ref[2:3, :] * mix
        x1_ref[...] = x1
        r = lax.rsqrt(jnp.mean(x1 * x1, axis=-1, keepdims=True) + 1e-6)
        h2_ref[...] = ((x1 * r * n2w_ref[...]) * (1.0 + mod_ref[4:5, :]) + mod_ref[3:4, :]).astype(BF16)

    return pl.pallas_call(
        body, name=name, grid=(L // TR,),
        out_shape=[_sds((L, D), F32), _sds((L, D_CONF), BF16), _sds((L, D), F32), _sds((L, D), BF16)],
        in_specs=[_row_spec(), _row_spec(), _const_spec((1, D)), _const_spec((1, D)), _const_spec((D_SSD + D_CONF, D)), _row_spec(),
                  _const_spec((8, D)), _const_spec((1, D))],
        out_specs=[_row_spec()] * 4, compiler_params=_cparams(("parallel",)))(ysn, uc, lnw, lnb, w_out, x, mod, n2w)


def ln_silu_bwd(du_all, uc, lnw, lnb, name):
    L = uc.shape[0]

    def body(du_ref, u_ref, w_ref, b_ref, o_ref, acc_ref):
        @pl.when(pl.program_id(0) == 0)
        def _():
            acc_ref[...] = jnp.zeros((8, D), F32)

        u = u_ref[...]
        mu = jnp.mean(u, axis=-1, keepdims=True)
        rl = lax.rsqrt(jnp.mean(jnp.square(u - mu), axis=-1, keepdims=True) + 1e-5)
        n = (u - mu) * rl
        v = n * w_ref[...] + b_ref[...]
        dv = du_ref[...] * _dsilu(v)
        acc_ref[0:1, :] += _colsum(dv * n)
        acc_ref[1:2, :] += _colsum(dv)
        dn = dv * w_ref[...]
        o_ref[...] = rl * (dn - jnp.mean(dn, axis=-1, keepdims=True) - n * jnp.mean(dn * n, axis=-1, keepdims=True))

    return pl.pallas_call(body, name=name, grid=(L // TR,), out_shape=[_sds((L, D), F32), _sds((8, D), F32)],
                          in_specs=[_row_col_spec(D, 1), _row_spec(), _const_spec((1, D)), _const_spec((1, D))],
                          out_specs=[_row_spec(), _const_spec((8, D))],
                          compiler_params=_cparams(("arbitrary",)))(du_all, uc, lnw, lnb)


def final_loss(act, w_down, x1, mod, fw, target, name):
    L = act.shape[0]

    def body(act_ref, wd_ref, x1_ref, mod_ref, fw_ref, t_ref, dx_ref, dff_ref, dact_ref, acc_ref):
        @pl.when(pl.program_id(0) == 0)
        def _():
            acc_ref[...] = jnp.zeros((8, D), F32)

        ff_v = jnp.dot(act_ref[...], wd_ref[...], preferred_element_type=F32)
        g2 = mod_ref[5:6, :]
        x2 = x1_ref[...] + g2 * ff_v
        r = lax.rsqrt(jnp.mean(x2 * x2, axis=-1, keepdims=True) + 1e-6)
        n = x2 * r
        err = n * fw_ref[...] - t_ref[...]
        dy = err * (1.0 / D)
        dn = dy * fw_ref[...]
        dx2 = r * (dn - n * jnp.mean(dn * n, axis=-1, keepdims=True))
        acc_ref[0:1, :] += _colsum(dy * n)
        acc_ref[1:2, :] += _colsum(dx2 * ff_v)
        acc_ref[2:3, :] += _colsum(err * err)
        dx_ref[...] = dx2
        dff = (dx2 * g2).astype(BF16)
        dff_ref[...] = dff
        dact_ref[...] = lax.dot_general(dff, wd_ref[...], (((1,), (1,)), ((), ())), preferred_element_type=F32)

    return pl.pallas_call(
        body, name=name, grid=(L // TR,),
        out_shape=[_sds((L, D), F32), _sds((L, D), BF16), _sds((L, D_FF), F32), _sds((8, D), F32)],
        in_specs=[_row_spec(D_FF), _const_spec((D_FF, D)), _row_spec(), _const_spec((8, D)), _const_spec((1, D)), _row_spec()],
        out_specs=[_row_spec(), _row_spec(), _row_spec(D_FF), _const_spec((8, D))],
        compiler_params=_cparams(("arbitrary",)))(act, w_down, x1, mod, fw, target)


def norm_mod_bwd(dh_pairs, xin, dres, mod, w, shift_row, name, mix=None, gate_row=None):
    L = xin.shape[0]
    has_mix = mix is not None
    n_pairs = len(dh_pairs)

    def body(*refs):
        pair_refs, refs = refs[:2 * n_pairs], refs[2 * n_pairs:]
        if has_mix:
            x_ref, dres_ref, mod_ref, w_ref, mix_ref, dx_ref, dmix_ref, acc_ref = refs
        else:
            x_ref, dres_ref, mod_ref, w_ref, dx_ref, acc_ref = refs

        @pl.when(pl.program_id(0) == 0)
        def _():
            acc_ref[...] = jnp.zeros((8, D), F32)

        dh_v = None
        for p in range(n_pairs):
            t = jnp.dot(pair_refs[2 * p][...], pair_refs[2 * p + 1][...], preferred_element_type=F32)
            dh_v = t if dh_v is None else dh_v + t
        x = x_ref[...]
        r = lax.rsqrt(jnp.mean(x * x, axis=-1, keepdims=True) + 1e-6)
        n = x * r
        nw = n * w_ref[...]
        sc1 = 1.0 + mod_ref[shift_row + 1:shift_row + 2, :]
        acc_ref[0:1, :] += _colsum(dh_v)
        acc_ref[1:2, :] += _colsum(dh_v * nw)
        dnw = dh_v * sc1
        acc_ref[2:3, :] += _colsum(dnw * n)
        dn = dnw * w_ref[...]
        dx = r * (dn - n * jnp.mean(dn * n, axis=-1, keepdims=True)) + dres_ref[...]
        dx_ref[...] = dx
        if has_mix:
            acc_ref[3:4, :] += _colsum(dx * mix_ref[...])
            dmix_ref[...] = (dx * mod_ref[gate_row:gate_row + 1, :]).astype(BF16)

    ins, in_specs = [], []
    for a, wt, rb in dh_pairs:
        ins += [a, wt]
        in_specs += [_row_spec(a.shape[1]), pl.BlockSpec((a.shape[1], D), functools.partial(lambda i, rb: (rb, 0), rb=rb))]
    ins += [xin, dres, mod, w] + ([mix] if has_mix else [])
    in_specs += [_row_spec(), _row_spec(), _const_spec((8, D)), _const_spec((1, D))] + ([_row_spec()] if has_mix else [])
    out_shape = [_sds((L, D), F32)] + ([_sds((L, D), BF16)] if has_mix else []) + [_sds((8, D), F32)]
    out_specs = [_row_spec()] + ([_row_spec()] if has_mix else []) + [_const_spec((8, D))]
    return pl.pallas_call(body, name=name, grid=(L // TR,), out_shape=out_shape, in_specs=in_specs,
                          out_specs=out_specs, compiler_params=_cparams(("arbitrary",)))(*ins)


def _halo(k):
    return 8 if k <= 9 else 32


def _prev_spec(h, col0):
    return pl.BlockSpec((h, CB), lambda j, i: (jnp.maximum(i * (TC // h) - 1, 0), j + col0))


def _next_spec(h, col0, n_tiles):
    return pl.BlockSpec((h, CB), lambda j, i: (jnp.minimum(i + 1, n_tiles - 1) * (TC // h), j + col0))


def _tile_spec(col0):
    return pl.BlockSpec((TC, CB), lambda j, i: (i, j + col0))


def _w_spec(kp, col0):
    return pl.BlockSpec((kp, CB), lambda j, i: (0, j + col0))


SUBLANES = 8


def _shifted_windows(v, taps, rows):
    for r in range(SUBLANES):
        group = [(o, k) for o, k in taps if o % SUBLANES == r]
        if not group:
            continue
        s = v if r == 0 else pltpu.roll(v, v.shape[0] - r, 0)
        for o, k in group:
            yield k, s[o - r:o - r + rows, :]


def _causal_taps(ext_ref, w_ref, k_taps, first, rows):
    acc = None
    for k, win in _shifted_windows(ext_ref[...], [(first - (k_taps - 1) + k, k) for k in range(k_taps)], rows):
        t = w_ref[k:k + 1, :] * win
        acc = t if acc is None else acc + t
    return acc


def _anticausal_taps(d_ref, w_ref, k_taps, rows):
    acc = None
    for k, win in _shifted_windows(d_ref[...], [(k_taps - 1 - k, k) for k in range(k_taps)], rows):
        t = w_ref[k:k + 1, :] * win
        acc = t if acc is None else acc + t
    return acc


def _acc_conv_wgrad(dw_ref, d_tile, ext_ref, k_taps, first):
    for k, win in _shifted_windows(ext_ref[...], [(first - (k_taps - 1) + k, k) for k in range(k_taps)], TC):
        dw_ref[k:k + 1, :] += _colsum(d_tile * win)
    dw_ref[k_taps:k_taps + 1, :] += _colsum(d_tile)


def conv_silu_fwd(x, col0, width, w, b, name):
    L = x.shape[0]
    k_taps = w.shape[0]
    h = _halo(k_taps)

    def body(xp_ref, x_ref, w_ref, b_ref, o_ref, ext_ref):
        i = pl.program_id(1)
        ext_ref[0:h, :] = jnp.where(i > 0, xp_ref[...], 0.0)
        ext_ref[h:h + TC, :] = x_ref[...]
        o_ref[...] = _silu(_causal_taps(ext_ref, w_ref, k_taps, h, TC) + b_ref[...])

    return pl.pallas_call(
        body, name=name, grid=(width // CB, L // TC), out_shape=_sds((L, width), F32),
        in_specs=[_prev_spec(h, col0), _tile_spec(col0), _w_spec(k_taps, 0), pl.BlockSpec((1, CB), lambda j, i: (0, j))],
        out_specs=_tile_spec(0), scratch_shapes=[pltpu.VMEM((h + TC, CB), F32)],
        compiler_params=_cparams(("parallel", "parallel")))(x, x, w, b)


def conv_silu_bwd(x, col0, width, w, b, dpost, name):
    L = x.shape[0]
    k_taps = w.shape[0]
    h = _halo(k_taps)
    nt = L // TC

    def body(xp_ref, x_ref, xn_ref, d_ref, dn_ref, w_ref, b_ref, dx_ref, dw_ref, ext_ref, dpre_ref):
        i = pl.program_id(1)

        @pl.when(i == 0)
        def _():
            dw_ref[...] = jnp.zeros((8, CB), F32)

        ext_ref[0:h, :] = jnp.where(i > 0, xp_ref[...], 0.0)
        ext_ref[h:h + TC, :] = x_ref[...]
        ext_ref[h + TC:h + TC + h, :] = xn_ref[...]
        pre = _causal_taps(ext_ref, w_ref, k_taps, h, TC + h) + b_ref[...]
        dpre_ref[0:TC, :] = d_ref[...] * _dsilu(pre[0:TC, :])
        dpre_ref[TC:TC + h, :] = jnp.where(i < nt - 1, dn_ref[...], 0.0) * _dsilu(pre[TC:TC + h, :])
        dx_ref[...] = _anticausal_taps(dpre_ref, w_ref, k_taps, TC).astype(BF16)
        _acc_conv_wgrad(dw_ref, dpre_ref[0:TC, :], ext_ref, k_taps, h)

    return pl.pallas_call(
        body, name=name, grid=(width // CB, nt),
        out_shape=[_sds((L, width), BF16), _sds((8, width), F32)],
        in_specs=[_prev_spec(h, col0), _tile_spec(col0), _next_spec(h, col0, nt), _tile_spec(0), _next_spec(h, 0, nt),
                  _w_spec(k_taps, 0), pl.BlockSpec((1, CB), lambda j, i: (0, j))],
        out_specs=[_tile_spec(0), _w_spec(8, 0)],
        scratch_shapes=[pltpu.VMEM((h + TC + h, CB), F32), pltpu.VMEM((TC + h, CB), F32)],
        compiler_params=_cparams(("parallel", "arbitrary")))(x, x, x, dpost, dpost, w, b)


def conf_conv_fwd(proj, col_a, col_g, w, b, name):
    L = proj.shape[0]
    k_taps = w.shape[0]
    h = _halo(k_taps)

    def body(ap_ref, a_ref, gp_ref, g_ref, w_ref, b_ref, o_ref, ext_ref):
        i = pl.program_id(1)
        ext_ref[0:h, :] = jnp.where(i > 0, ap_ref[...] * _sigmoid(gp_ref[...]), 0.0)
        ext_ref[h:h + TC, :] = a_ref[...] * _sigmoid(g_ref[...])
        o_ref[...] = _causal_taps(ext_ref, w_ref, k_taps, h, TC) + b_ref[...]

    return pl.pallas_call(
        body, name=name, grid=(D_CONF // CB, L // TC), out_shape=_sds((L, D_CONF), F32),
        in_specs=[_prev_spec(h, col_a), _tile_spec(col_a), _prev_spec(h, col_g), _tile_spec(col_g), _w_spec(k_taps, 0),
                  pl.BlockSpec((1, CB), lambda j, i: (0, j))],
        out_specs=_tile_spec(0), scratch_shapes=[pltpu.VMEM((h + TC, CB), F32)],
        compiler_params=_cparams(("parallel", "parallel")))(proj, proj, proj, proj, w, b)


def conf_conv_bwd(proj, col_a, col_g, w, duc, name):
    L = proj.shape[0]
    k_taps = w.shape[0]
    h = _halo(k_taps)
    nt = L // TC

    def body(ap_ref, a_ref, gp_ref, g_ref, d_ref, dn_ref, w_ref, da_ref, dg_ref, dw_ref, ext_ref, dext_ref):
        i = pl.program_id(1)

        @pl.when(i == 0)
        def _():
            dw_ref[...] = jnp.zeros((32, CB), F32)

        a = a_ref[...]
        s = _sigmoid(g_ref[...])
        ext_ref[0:h, :] = jnp.where(i > 0, ap_ref[...] * _sigmoid(gp_ref[...]), 0.0)
        ext_ref[h:h + TC, :] = a * s
        dext_ref[0:TC, :] = d_ref[...]
        dext_ref[TC:TC + h, :] = jnp.where(i < nt - 1, dn_ref[...], 0.0)
        du0 = _anticausal_taps(dext_ref, w_ref, k_taps, TC)
        da_ref[...] = (du0 * s).astype(BF16)
        dg_ref[...] = (du0 * a * s * (1.0 - s)).astype(BF16)
        _acc_conv_wgrad(dw_ref, d_ref[...], ext_ref, k_taps, h)

    return pl.pallas_call(
        body, name=name, grid=(D_CONF // CB, nt),
        out_shape=[_sds((L, D_CONF), BF16), _sds((L, D_CONF), BF16), _sds((32, D_CONF), F32)],
        in_specs=[_prev_spec(h, col_a), _tile_spec(col_a), _prev_spec(h, col_g), _tile_spec(col_g), _tile_spec(0),
                  _next_spec(h, 0, nt), _w_spec(k_taps, 0)],
        out_specs=[_tile_spec(0), _tile_spec(0), _w_spec(32, 0)],
        scratch_shapes=[pltpu.VMEM((h + TC, CB), F32), pltpu.VMEM((TC + h, CB), F32)],
        compiler_params=_cparams(("parallel", "arbitrary")))(proj, proj, proj, proj, duc, duc, w)


def ffn_conv_fwd(up, w, b, name):
    L = up.shape[0]
    k_taps = w.shape[0]
    h = _halo(k_taps)
    cv = D_FF // CB

    def body(gp_ref, g_ref, vp_ref, v_ref, wg_ref, wv_ref, bg_ref, bv_ref, o_ref, eg_ref, ev_ref):
        i = pl.program_id(1)
        eg_ref[0:h, :] = jnp.where(i > 0, gp_ref[...], 0.0)
        eg_ref[h:h + TC, :] = g_ref[...]
        ev_ref[0:h, :] = jnp.where(i > 0, vp_ref[...], 0.0)
        ev_ref[h:h + TC, :] = v_ref[...]
        pg = _causal_taps(eg_ref, wg_ref, k_taps, h, TC) + bg_ref[...]
        pv = _causal_taps(ev_ref, wv_ref, k_taps, h, TC) + bv_ref[...]
        o_ref[...] = (_silu(pg) * pv).astype(BF16)

    bspec = lambda c0: pl.BlockSpec((1, CB), lambda j, i: (0, j + c0))
    return pl.pallas_call(
        body, name=name, grid=(cv, L // TC), out_shape=_sds((L, D_FF), BF16),
        in_specs=[_prev_spec(h, 0), _tile_spec(0), _prev_spec(h, cv), _tile_spec(cv), _w_spec(k_taps, 0), _w_spec(k_taps, cv),
                  bspec(0), bspec(cv)],
        out_specs=_tile_spec(0), scratch_shapes=[pltpu.VMEM((h + TC, CB), F32), pltpu.VMEM((h + TC, CB), F32)],
        compiler_params=_cparams(("parallel", "parallel")))(up, up, up, up, w, w, b, b)


def ffn_conv_bwd(up, w, b, dact, name):
    L = up.shape[0]
    k_taps = w.shape[0]
    h = _halo(k_taps)
    nt = L // TC
    cv = D_FF // CB

    def body(gp_ref, g_ref, gn_ref, vp_ref, v_ref, vn_ref, d_ref, dn_ref, wg_ref, wv_ref, bg_ref, bv_ref,
             dg_ref, dv_ref, dwg_ref, dwv_ref, eg_ref, ev_ref, pg_ref, pv_ref):
        i = pl.program_id(1)

        @pl.when(i == 0)
        def _():
            dwg_ref[...] = jnp.zeros((8, CB), F32)
            dwv_ref[...] = jnp.zeros((8, CB), F32)

        for e_ref, p_ref, c_ref, n_ref in ((eg_ref, gp_ref, g_ref, gn_ref), (ev_ref, vp_ref, v_ref, vn_ref)):
            e_ref[0:h, :] = jnp.where(i > 0, p_ref[...], 0.0)
            e_ref[h:h + TC, :] = c_ref[...]
            e_ref[h + TC:h + TC + h, :] = n_ref[...]
        pg = _causal_taps(eg_ref, wg_ref, k_taps, h, TC + h) + bg_ref[...]
        pv = _causal_taps(ev_ref, wv_ref, k_taps, h, TC + h) + bv_ref[...]
        dact_t = d_ref[...]
        dact_n = jnp.where(i < nt - 1, dn_ref[...], 0.0)
        pg_ref[0:TC, :] = dact_t * pv[0:TC, :] * _dsilu(pg[0:TC, :])
        pg_ref[TC:TC + h, :] = dact_n * pv[TC:TC + h, :] * _dsilu(pg[TC:TC + h, :])
        pv_ref[0:TC, :] = dact_t * _silu(pg[0:TC, :])
        pv_ref[TC:TC + h, :] = dact_n * _silu(pg[TC:TC + h, :])
        dg_ref[...] = _anticausal_taps(pg_ref, wg_ref, k_taps, TC).astype(BF16)
        dv_ref[...] = _anticausal_taps(pv_ref, wv_ref, k_taps, TC).astype(BF16)
        _acc_conv_wgrad(dwg_ref, pg_ref[0:TC, :], eg_ref, k_taps, h)
        _acc_conv_wgrad(dwv_ref, pv_ref[0:TC, :], ev_ref, k_taps, h)

    bspec = lambda c0: pl.BlockSpec((1, CB), lambda j, i: (0, j + c0))
    ext = pltpu.VMEM((h + TC + h, CB), F32)
    dpre = pltpu.VMEM((TC + h, CB), F32)
    return pl.pallas_call(
        body, name=name, grid=(cv, nt),
        out_shape=[_sds((L, D_FF), BF16), _sds((L, D_FF), BF16), _sds((8, D_FF), F32), _sds((8, D_FF), F32)],
        in_specs=[_prev_spec(h, 0), _tile_spec(0), _next_spec(h, 0, nt), _prev_spec(h, cv), _tile_spec(cv), _next_spec(h, cv, nt),
                  _tile_spec(0), _next_spec(h, 0, nt), _w_spec(k_taps, 0), _w_spec(k_taps, cv), bspec(0), bspec(cv)],
        out_specs=[_tile_spec(0), _tile_spec(0), _w_spec(8, 0), _w_spec(8, 0)],
        scratch_shapes=[ext, ext, dpre, dpre],
        compiler_params=_cparams(("parallel", "arbitrary")))(up, up, up, up, up, up, dact, dact, w, w, b, b)


def _ssd_common(xbc_ref, dt_ref, dtb_ref, alog_ref, cs_ref):
    xs = xbc_ref[:, 0:D_SSD]
    sp_in = dt_ref[...] + dtb_ref[...]
    dtf = _softplus(sp_in)
    a_f = -jnp.exp(alog_ref[...])
    a_dt = dtf * a_f
    row = lax.broadcasted_iota(jnp.int32, (Q, Q), 0)
    col = lax.broadcasted_iota(jnp.int32, (Q, Q), 1)
    causal = row >= col
    cs = _dot_exact(causal.astype(F32), a_dt, 3, "a")
    cs_ref[...] = cs
    cs_last = cs_ref[Q - 1:Q, :]
    return xs, sp_in, dtf, a_f, cs, cs_last, causal


def _head_decay(cs_j, cst_ref, e, causal):
    lane = lax.broadcasted_iota(jnp.int32, (Q, LANE), 1)
    rolled = pltpu.roll(cs_j, HEAD, 1)
    own = (lane < HEAD) if e == 0 else (lane >= HEAD)
    col_b = jnp.where(own, cs_j, rolled)
    col_b = jnp.concatenate([col_b] * (Q // LANE), axis=1)
    row_b = cst_ref[e * HEAD:e * HEAD + 1, :]
    return jnp.where(causal, jnp.exp(jnp.minimum(col_b - row_b, 0.0)), 0.0)


def ssd_fwd(xbc, z_src, dt_src, dtb_f, alog_f, dsk_f, snw, name):
    L = xbc.shape[0]
    nc = L // Q

    def body(xbc_ref, z_ref, dt_ref, dtb_ref, alog_ref, dsk_ref, snw_ref, y_ref, yn_ref, sp_ref, s_ref, cs_ref, cst_ref, yd_ref):
        @pl.when(pl.program_id(0) == 0)
        def _():
            s_ref[...] = jnp.zeros((N_STATE, D_SSD), F32)

        xs, _, dtf, a_f, cs, cs_last, causal = _ssd_common(xbc_ref, dt_ref, dtb_ref, alog_ref, cs_ref)
        e_cs = jnp.exp(cs)
        xdt = xs * dtf
        zst = jnp.exp(cs_last - cs) * xdt
        sp_ref[0] = s_ref[...]
        lane = lax.broadcasted_iota(jnp.int32, (Q, LANE), 1)
        for g in range(2):
            gl = slice(g * 512, g * 512 + 512)
            b_g = xbc_ref[:, D_SSD + g * N_STATE:D_SSD + (g + 1) * N_STATE]
            c_g = xbc_ref[:, D_SSD + 2 * N_STATE + g * N_STATE:D_SSD + 2 * N_STATE + (g + 1) * N_STATE]
            s_prev = s_ref[:, gl]
            cb = _dot_nt(c_g, b_g)
            yd_ref[:, gl] = e_cs[:, gl] * _dot(c_g, s_prev)
            for j in range(4):
                tl = slice(g * 512 + j * LANE, g * 512 + (j + 1) * LANE)
                cs_j = cs[:, tl]
                cst_ref[...] = cs_j.T
                x_j = xdt[:, tl]
                o0 = _dot(cb * _head_decay(cs_j, cst_ref, 0, causal), x_j)
                o1 = _dot(cb * _head_decay(cs_j, cst_ref, 1, causal), x_j)
                yd_ref[:, tl] += jnp.where(lane < HEAD, o0, o1)
            s_ref[:, gl] = jnp.exp(cs_last[:, gl]) * s_prev + _dot_tn(b_g, zst[:, gl])
        y = yd_ref[...] + xs * dsk_ref[...]
        y_ref[...] = y
        yz = y * _silu(z_ref[...])
        r = lax.rsqrt(jnp.mean(yz * yz, axis=-1, keepdims=True) + 1e-6)
        yn_ref[...] = (yz * r * snw_ref[...]).astype(BF16)

    chunk = lambda w, c: pl.BlockSpec((Q, w), lambda i: (i, c))
    return pl.pallas_call(
        body, name=name, grid=(nc,),
        out_shape=[_sds((L, D_SSD), F32), _sds((L, D_SSD), BF16), _sds((nc, N_STATE, D_SSD), F32)],
        in_specs=[chunk(D_XBC, 0), chunk(D, 0), chunk(D, 0)] + [_const_spec((1, D))] * 4,
        out_specs=[chunk(D, 0), chunk(D, 0), pl.BlockSpec((1, N_STATE, D_SSD), lambda i: (i, 0, 0))],
        scratch_shapes=[pltpu.VMEM((N_STATE, D_SSD), F32), pltpu.VMEM((Q, D_SSD), F32), pltpu.VMEM((LANE, Q), F32),
                        pltpu.VMEM((Q, D_SSD), F32)],
        compiler_params=_cparams(("arbitrary",)))(xbc, z_src, dt_src, dtb_f, alog_f, dsk_f, snw)


def ssd_bwd(dmixin, y, xbc, z_src, dt_src, s_prev_all, dtb_f, alog_f, dsk_f, snw, name):
    L = xbc.shape[0]
    nc = L // Q

    def body(dyn_ref, y_ref, xbc_ref, z_ref, dt_ref, sp_ref, dtb_ref, alog_ref, dsk_ref, snw_ref,
             dz_ref, ddt_ref, dxbc_ref, acc_ref, acc16_ref, ds_ref, cs_ref, cst_ref, dcs_ref, dx_ref):
        step = pl.program_id(0)

        @pl.when(step == 0)
        def _():
            ds_ref[...] = jnp.zeros((N_STATE, D_SSD), F32)
            acc_ref[...] = jnp.zeros((8, D), F32)

        z = z_ref[...]
        y = y_ref[...]
        sz = _sigmoid(z)
        siluz = z * sz
        yz = y * siluz
        r = lax.rsqrt(jnp.mean(yz * yz, axis=-1, keepdims=True) + 1e-6)
        n = yz * r
        dyn = dyn_ref[...]
        acc_ref[0:1, :] += _colsum(dyn * n)
        dn = dyn * snw_ref[...]
        dyz = r * (dn - n * jnp.mean(dn * n, axis=-1, keepdims=True))
        dy = dyz * siluz
        dz_ref[...] = (dyz * y * (sz * (1.0 + z * (1.0 - sz)))).astype(BF16)

        xs, sp_in, dtf, a_f, cs, cs_last, causal = _ssd_common(xbc_ref, dt_ref, dtb_ref, alog_ref, cs_ref)
        acc_ref[3:4, :] += _colsum(dy * xs)
        e_cs = jnp.exp(cs)
        xdt = xs * dtf
        dst = jnp.exp(cs_last - cs)
        zst = dst * xdt
        e_last = jnp.exp(cs_last)
        lane = lax.broadcasted_iota(jnp.int32, (Q, LANE), 1)
        ones = jnp.ones((Q, LANE), F32)
        dcs_last_parts = []
        for g in range(2):
            gl = slice(g * 512, g * 512 + 512)
            b_g = xbc_ref[:, D_SSD + g * N_STATE:D_SSD + (g + 1) * N_STATE]
            c_g = xbc_ref[:, D_SSD + 2 * N_STATE + g * N_STATE:D_SSD + 2 * N_STATE + (g + 1) * N_STATE]
            s_prev = sp_ref[0, :, gl]
            ds_g = ds_ref[:, gl]
            dy_g = dy[:, gl]
            cb = _dot_nt(c_g, b_g)
            y_off = e_cs[:, gl] * _dot(c_g, s_prev)
            edy = e_cs[:, gl] * dy_g
            d_c = _dot_nt(edy, s_prev)
            d_z = _dot(b_g, ds_g)
            d_b = _dot_nt(zst[:, gl], ds_g)
            t_g = d_z * zst[:, gl]
            dcs_ref[:, gl] = dy_g * y_off - t_g
            dx_ref[:, gl] = d_z * dst[:, gl]
            dcs_last_parts.append(_colsum(t_g) + _colsum(ds_g * s_prev) * e_last[:, gl])
            ds_ref[:, gl] = e_last[:, gl] * ds_g + _dot_tn(c_g, edy)
            dcb = jnp.zeros((Q, Q), F32)
            for j in range(4):
                tl = slice(g * 512 + j * LANE, g * 512 + (j + 1) * LANE)
                cs_j = cs[:, tl]
                cst_ref[...] = cs_j.T
                x_j = xdt[:, tl]
                dy_j = dy[:, tl]
                dx_j = jnp.zeros((Q, LANE), F32)
                dcs_j = jnp.zeros((Q, LANE), F32)
                for e in range(2):
                    own = (lane < HEAD) if e == 0 else (lane >= HEAD)
                    w_h = _head_decay(cs_j, cst_ref, e, causal)
                    g_h = cb * w_h
                    dy_m = jnp.where(own, dy_j, 0.0)
                    d_g = _dot_nt(dy_m, x_j)
                    dx_j = dx_j + _dot_tn(g_h, dy_m)
                    dcb = dcb + d_g * w_h
                    p_h = d_g * g_h
                    row_sums = _dot_exact(p_h, ones, 2, "b")
                    col_sums = _dot_exact(p_h, ones, 2, "b", (((0,), (0,)), ((), ())))
                    dcs_j = dcs_j + jnp.where(own, row_sums - col_sums, 0.0)
                dcs_ref[:, tl] += dcs_j * (1.0 / HEAD)
                dx_ref[:, tl] += dx_j
            d_c = d_c + _dot(dcb, b_g)
            d_b = d_b + _dot_tn(dcb, c_g)
            dxbc_ref[:, D_SSD + g * N_STATE:D_SSD + (g + 1) * N_STATE] = d_b
            dxbc_ref[:, D_SSD + 2 * N_STATE + g * N_STATE:D_SSD + 2 * N_STATE + (g + 1) * N_STATE] = d_c
        dcs_last = jnp.concatenate(dcs_last_parts, axis=1)
        anticausal = lax.broadcasted_iota(jnp.int32, (Q, Q), 0) <= lax.broadcasted_iota(jnp.int32, (Q, Q), 1)
        d_adt = _dot_exact(anticausal.astype(F32), dcs_ref[...], 3, "a") + dcs_last
        dx = dx_ref[...]
        acc_ref[2:3, :] += _colsum(d_adt * dtf) * a_f
        d_dtf = d_adt * a_f + dx * xs
        dxbc_ref[:, 0:D_SSD] = dx * dtf + dy * dsk_ref[...]
        d_raw = d_dtf * _sigmoid(sp_in)
        acc_ref[1:2, :] += _colsum(d_raw)
        head_of_lane = lax.broadcasted_iota(jnp.int32, (D_SSD, LANE), 0) // HEAD
        fold = (head_of_lane == lax.broadcasted_iota(jnp.int32, (D_SSD, LANE), 1)).astype(F32)
        ddt_ref[...] = _dot_exact(d_raw, fold, 2, "b").astype(BF16)

        @pl.when(step == nc - 1)
        def _():
            acc16_ref[...] = _dot_exact(acc_ref[...], fold, 3, "b")

    rchunk = lambda w, c: pl.BlockSpec((Q, w), lambda i: (nc - 1 - i, c))
    return pl.pallas_call(
        body, name=name, grid=(nc,),
        out_shape=[_sds((L, D_SSD), BF16), _sds((L, LANE), BF16), _sds((L, D_XBC), F32), _sds((8, D), F32), _sds((8, LANE), F32)],
        in_specs=[rchunk(D, 0), rchunk(D, 0), rchunk(D_XBC, 0), rchunk(D, 0), rchunk(D, 0),
                  pl.BlockSpec((1, N_STATE, D_SSD), lambda i: (nc - 1 - i, 0, 0))] + [_const_spec((1, D))] * 4,
        out_specs=[rchunk(D, 0), rchunk(LANE, 0), rchunk(D_XBC, 0), _const_spec((8, D)), _const_spec((8, LANE))],
        scratch_shapes=[pltpu.VMEM((N_STATE, D_SSD), F32), pltpu.VMEM((Q, D_SSD), F32), pltpu.VMEM((LANE, Q), F32),
                        pltpu.VMEM((Q, D_SSD), F32), pltpu.VMEM((Q, D_SSD), F32)],
        compiler_params=_cparams(("arbitrary",)))(dmixin, y, xbc, z_src, dt_src, s_prev_all, dtb_f, alog_f, dsk_f, snw)


def _adamw_math(w, g, m, v):
    m_n = ADAM_B1 * m + (1.0 - ADAM_B1) * g
    v_n = ADAM_B2 * v + (1.0 - ADAM_B2) * jnp.square(g)
    c1 = 1.0 - ADAM_B1 ** ADAM_STEP
    c2 = 1.0 - ADAM_B2 ** ADAM_STEP
    return -ADAM_LR * ((m_n / c1) / (jnp.sqrt(v_n / c2) + ADAM_EPS) + ADAM_WD * w), m_n, v_n


def _sum_slots(p_ref):
    acc = p_ref[0].astype(F32)
    for s in range(1, N_DEV):
        acc = acc + p_ref[s].astype(F32)
    return acc


def adamw_slots(w, slots, m, v, name, alt=None, alt_from=N_DEV):
    rows, cols = w.shape
    tc = 256

    def body(*refs):
        if alt is None:
            w_ref, s_ref, m_ref, v_ref, g_ref, d_ref, mo_ref, vo_ref = refs
            g_v = _sum_slots(s_ref)
        else:
            w_ref, s_ref, alt_ref, m_ref, v_ref, g_ref, d_ref, mo_ref, vo_ref = refs
            use_alt = 4 * lax.axis_index("x") + 2 * lax.axis_index("y") + lax.axis_index("c") >= alt_from
            g_v = None
            for s in range(N_DEV):
                t = jnp.where(use_alt, alt_ref[s], s_ref[s]).astype(F32)
                g_v = t if g_v is None else g_v + t
        g_ref[...] = g_v
        d_ref[...], mo_ref[...], vo_ref[...] = _adamw_math(w_ref[...], g_v, m_ref[...], v_ref[...])

    spec = pl.BlockSpec((rows, tc), lambda i: (0, i))
    slab_spec = pl.BlockSpec((N_DEV, rows, tc), lambda i: (0, 0, i))
    ins = [w, slots] + ([alt] if alt is not None else []) + [m, v]
    in_specs = [spec, slab_spec] + ([slab_spec] if alt is not None else []) + [spec, spec]
    return pl.pallas_call(body, name=name, grid=(cols // tc,), out_shape=[_sds((rows, cols), F32)] * 4, in_specs=in_specs,
                          out_specs=[spec] * 4, compiler_params=_cparams(("parallel",)))(*ins)


def adamw_many(ws, gs, ms, vs, name):
    n = len(ws)

    def body(*refs):
        for p in range(n):
            d_v, m_v, v_v = _adamw_math(refs[p][...], refs[n + p][...], refs[2 * n + p][...], refs[3 * n + p][...])
            refs[4 * n + p][...] = d_v
            refs[5 * n + p][...] = m_v
            refs[6 * n + p][...] = v_v

    vm = pl.BlockSpec(memory_space=pltpu.VMEM)
    out = pl.pallas_call(body, name=name, out_shape=[_sds(w.shape, F32) for w in ws] * 3, in_specs=[vm] * (4 * n),
                         out_specs=[vm] * (3 * n), compiler_params=_cparams())(*ws, *gs, *ms, *vs)
    return out[:n], out[n:2 * n], out[2 * n:]


def _pack_layout(shapes):
    row, layout = 0, []
    for rows, cols in shapes:
        chunks = []
        for c0 in range(0, cols, D):
            chunks.append((row, c0, min(D, cols - c0)))
            row += rows
        layout.append(chunks)
    return row, layout


def pack_rows(arrays, name):
    total, layout = _pack_layout([a.shape for a in arrays])
    n = len(arrays)

    def body(*refs):
        o_ref = refs[n]
        o_ref[...] = jnp.zeros((total, D), F32)
        for p in range(n):
            rows = arrays[p].shape[0]
            for r0, c0, w in layout[p]:
                o_ref[r0:r0 + rows, 0:w] = refs[p][:, c0:c0 + w]

    vm = pl.BlockSpec(memory_space=pltpu.VMEM)
    return pl.pallas_call(body, name=name, out_shape=_sds((total, D), F32), in_specs=[vm] * n, out_specs=vm,
                          compiler_params=_cparams())(*arrays)


def unpack_rows(packed, shapes):
    _, layout = _pack_layout(shapes)
    out = []
    for (rows, _), chunks in zip(shapes, layout):
        parts = [packed[..., r0:r0 + rows, 0:w] for r0, _, w in chunks]
        out.append(parts[0] if len(parts) == 1 else jnp.concatenate(parts, axis=-1))
    return out


def sum_slots_many(parts, name):
    n = len(parts)

    def body(*refs):
        for p in range(n):
            refs[n + p][...] = _sum_slots(refs[p])

    vm = pl.BlockSpec(memory_space=pltpu.VMEM)
    return pl.pallas_call(body, name=name, out_shape=[_sds(p.shape[1:], F32) for p in parts], in_specs=[vm] * n,
                          out_specs=[vm] * n, compiler_params=_cparams())(*parts)


def ada_mod(c_all, ada_w_shard, ada_b_cols, name):
    def body(c_ref, w_ref, b_ref, o_ref, ca_ref):
        ca = _silu(c_ref[...])
        ca_ref[...] = ca
        o_ref[...] = _dot(ca, w_ref[...]) + b_ref[...]

    vm = pl.BlockSpec(memory_space=pltpu.VMEM)
    return pl.pallas_call(body, name=name, out_shape=[_sds((N_DEV, ada_w_shard.shape[1]), F32), _sds((N_DEV, D), F32)],
                          in_specs=[vm, vm, vm], out_specs=[vm, vm], compiler_params=_cparams())(c_all, ada_w_shard, ada_b_cols)


def ada_wgrad(c_act_all, dmod_cols, name):
    def body(c_ref, d_ref, o_ref):
        o_ref[...] = _dot_tn_hi(c_ref[...], d_ref[...])

    vm = pl.BlockSpec(memory_space=pltpu.VMEM)
    return pl.pallas_call(body, name=name, out_shape=_sds((D, dmod_cols.shape[1]), F32), in_specs=[vm, vm], out_specs=vm,
                          compiler_params=_cparams())(c_act_all, dmod_cols)


def exchange(srcs, name, gather):
    n = len(srcs)
    shapes = [tuple(s.shape) if gather else tuple(s.shape[1:]) for s in srcs]

    def body(*refs):
        src_refs, out_refs = refs[:n], refs[n:2 * n]
        send_sems, recv_sems, local_sems = refs[2 * n:]
        x, y, c = lax.axis_index("x"), lax.axis_index("y"), lax.axis_index("c")
        me = 4 * x + 2 * y + c

        def peer(k):
            bx, by, bc = (k >> 2) & 1, (k >> 1) & 1, k & 1
            px, py, pc = (x + bx) % 2, (y + by) % 2, (c + bc) % 2
            return (px, py, pc), 4 * px + 2 * py + pc

        def copy(a, k, landing):
            dev, idx = peer(k)
            return pltpu.make_async_remote_copy(
                src_ref=src_refs[a] if gather else src_refs[a].at[idx], dst_ref=out_refs[a].at[idx if landing else me],
                send_sem=send_sems.at[a, k - 1], recv_sem=recv_sems.at[a, k - 1],
                device_id=dev, device_id_type=pl.DeviceIdType.MESH)

        mine = [pltpu.make_async_copy(src_refs[a] if gather else src_refs[a].at[me], out_refs[a].at[me], local_sems.at[a])
                for a in range(n)]
        for cp in mine:
            cp.start()
        sends = [copy(a, k, False) for a in range(n) for k in range(1, N_DEV)]
        for cp in sends:
            cp.start()
        for a in range(n):
            for k in range(1, N_DEV):
                copy(a, k, True).wait_recv()
        for cp in sends:
            cp.wait_send()
        for cp in mine:
            cp.wait()

    hbm = pl.BlockSpec(memory_space=pl.ANY)
    return pl.pallas_call(
        body, name=name, out_shape=[_sds((N_DEV,) + shp, s.dtype) for shp, s in zip(shapes, srcs)], in_specs=[hbm] * n,
        out_specs=[hbm] * n,
        scratch_shapes=[pltpu.SemaphoreType.DMA((n, N_DEV - 1)), pltpu.SemaphoreType.DMA((n, N_DEV - 1)),
                        pltpu.SemaphoreType.DMA((n,))],
        compiler_params=pltpu.CompilerParams(has_side_effects=True))(*srcs)


def gather_two_level(src, name):
    def body(src_ref, out_ref, send_sems, recv_sems, local_sem):
        x, y, c = lax.axis_index("x"), lax.axis_index("y"), lax.axis_index("c")
        me, sibling = (x, y, c), (x, y, 1 - c)
        chips = [(1 - x, y), (x, 1 - y), (1 - x, 1 - y)]

        def slot(px, py, pc):
            return out_ref.at[4 * px + 2 * py + pc]

        def copy(k, block, to, src=None):
            return pltpu.make_async_remote_copy(
                src_ref=slot(*block) if src is None else src, dst_ref=slot(*block), send_sem=send_sems.at[k],
                recv_sem=recv_sems.at[k], device_id=to, device_id_type=pl.DeviceIdType.MESH)

        mine = pltpu.make_async_copy(src_ref, slot(*me), local_sem)
        mine.start()
        first = [copy(0, me, sibling, src=src_ref)]
        first += [copy(1 + j, me, (*chip, c), src=src_ref) for j, chip in enumerate(chips)]
        for cp in first:
            cp.start()
        passed = [copy(4 + j, (*chip, c), sibling) for j, chip in enumerate(chips)]
        for j, chip in enumerate(chips):
            copy(1 + j, (*chip, c), me).wait_recv()
            passed[j].start()
        copy(0, sibling, me).wait_recv()
        for j, chip in enumerate(chips):
            copy(4 + j, (*chip, 1 - c), me).wait_recv()
        for cp in first + passed:
            cp.wait_send()
        mine.wait()

    hbm = pl.BlockSpec(memory_space=pl.ANY)
    return pl.pallas_call(
        body, name=name, out_shape=_sds((N_DEV,) + tuple(src.shape), src.dtype), in_specs=[hbm], out_specs=hbm,
        scratch_shapes=[pltpu.SemaphoreType.DMA((N_DEV - 1,)), pltpu.SemaphoreType.DMA((N_DEV - 1,)), pltpu.SemaphoreType.DMA],
        compiler_params=pltpu.CompilerParams(has_side_effects=True))(src)


def _peer(k):
    x, y, c = lax.axis_index("x"), lax.axis_index("y"), lax.axis_index("c")
    px, py, pc = (x + ((k >> 2) & 1)) % 2, (y + ((k >> 1) & 1)) % 2, (c + (k & 1)) % 2
    return (px, py, pc), 4 * px + 2 * py + pc


def _my_slot():
    return 4 * lax.axis_index("x") + 2 * lax.axis_index("y") + lax.axis_index("c")


_HBM = pl.BlockSpec(memory_space=pltpu.HBM)
_SEM = pl.BlockSpec(memory_space=pltpu.SEMAPHORE)
_EFFECT = pltpu.SideEffectType.DATAFLOW_SIDE_EFFECTING


def _when_in(slot, slots, fn):
    if tuple(slots) == (0, N_DEV):
        fn()
    else:
        pl.when((slot >= slots[0]) & (slot < slots[1]))(fn)


def exchange_start(srcs, name, gather, slots=(0, N_DEV)):
    n = len(srcs)
    lo, hi = slots
    shapes = [tuple(s.shape) if gather else tuple(s.shape[1:]) for s in srcs]
    lands = [lax.empty((N_DEV,) + shp, s.dtype) for shp, s in zip(shapes, srcs)]

    def body(*refs):
        src_refs, land_refs = refs[:n], refs[n:2 * n]
        sems = refs[2 * n:4 * n]
        token = refs[-1]
        me = _my_slot()
        for a in range(n):
            for k in range(1, N_DEV):
                dev, idx = _peer(k)

                def start(a=a, k=k, dev=dev, idx=idx):
                    pltpu.make_async_remote_copy(
                        src_ref=src_refs[a] if gather else src_refs[a].at[jnp.clip(idx - lo, 0, src_refs[a].shape[0] - 1)],
                        dst_ref=land_refs[a].at[me],
                        send_sem=sems[2 * a].at[k - 1], recv_sem=sems[2 * a + 1].at[k - 1],
                        device_id=dev, device_id_type=pl.DeviceIdType.MESH).start()

                _when_in(idx, slots, start)
        token[...] = jnp.zeros_like(token)

    out_shape = ([pltpu.SemaphoreType.DMA((N_DEV - 1,))] * (2 * n) + [pltpu.HBM(s.shape, s.dtype) for s in srcs]
                 + [pltpu.HBM(l.shape, l.dtype) for l in lands] + [_sds((8, LANE), F32)])
    out = pl.pallas_call(
        body, name=name, out_shape=out_shape, in_specs=[_HBM] * (2 * n),
        out_specs=[_SEM] * (2 * n) + [_HBM] * (2 * n) + [pl.BlockSpec(memory_space=pltpu.VMEM)],
        input_output_aliases={i: 2 * n + i for i in range(2 * n)},
        compiler_params=pltpu.CompilerParams(has_side_effects=_EFFECT))(
            *[pltpu.with_memory_space_constraint(s, pltpu.HBM) for s in srcs],
            *[pltpu.with_memory_space_constraint(l, pltpu.HBM) for l in lands])
    parts = [(out[2 * a], out[2 * a + 1], out[2 * n + a], out[3 * n + a]) for a in range(n)]
    return parts, out[-1]


def exchange_wait(parts, after, name, gather, slots=(0, N_DEV)):
    n = len(parts)
    lo = slots[0]

    def body(*refs):
        src_refs, land_refs = refs[:n], refs[n:2 * n]
        sems = refs[2 * n:4 * n]
        me = _my_slot()
        for a in range(n):
            for k in range(1, N_DEV):
                dev, idx = _peer(k)
                copy = pltpu.make_async_remote_copy(
                    src_ref=src_refs[a] if gather else src_refs[a].at[jnp.clip(idx - lo, 0, src_refs[a].shape[0] - 1)],
                    dst_ref=land_refs[a].at[idx], send_sem=sems[2 * a].at[k - 1], recv_sem=sems[2 * a + 1].at[k - 1],
                    device_id=dev, device_id_type=pl.DeviceIdType.MESH)
                _when_in(idx, slots, copy.wait_send)
                _when_in(me, slots, copy.wait_recv)

    srcs = [p[2] for p in parts]
    lands = [p[3] for p in parts]
    sems = [s for p in parts for s in p[:2]]
    out = pl.pallas_call(
        body, name=name, out_shape=[pltpu.HBM(a.shape, a.dtype) for a in srcs + lands],
        in_specs=[_HBM] * (2 * n) + [_SEM] * (2 * n) + [pl.BlockSpec(memory_space=pl.ANY)], out_specs=[_HBM] * (2 * n),
        input_output_aliases={i: i for i in range(2 * n)},
        compiler_params=pltpu.CompilerParams(has_side_effects=_EFFECT))(*srcs, *lands, *sems, after)
    return out[n:]


def _slabs_to_cols(s):
    _, r, cs = s.shape
    return s.transpose(1, 0, 2).reshape(r, N_DEV * cs)


def _rep_heads(v):
    return jnp.repeat(v.reshape(N_HEADS), HEAD).reshape(1, D_SSD)


def local_fwd_bwd(x, target, mod, get_w, put_grad, small):
    n1w, n2w, fnw = small["norm1_w"], small["norm2_w"], small["final_norm_w"]
    dtb_f, alog_f, dsk_f = _rep_heads(small["dt_bias"]), _rep_heads(small["a_log"]), _rep_heads(small["d_skip"])
    snw = small["ssd_norm_w"]

    def after(v, token):
        return v + token[0:1, 0:1]

    h1 = norm_mod(x, mod, n1w, 0, "norm1")
    w_in = get_w("w_in", h1)
    proj_zx = mm_nt([(h1, w_in["w_full"], 0)], "in_proj_zx", rows=D_SSD + D_XBC)
    proj_dt = mm_nt([(h1, w_in["w_dt_rep"], 0)], "in_proj_dt")
    proj_cf = mm_nt([(h1, w_in["w_cf"], 0)], "in_proj_conf")
    xbc = conv_silu_fwd(proj_zx, D_SSD // CB, D_XBC, small["ssd_conv_w"], small["ssd_conv_b"], "ssd_conv")
    y, ysn, s_prev = ssd_fwd(xbc, proj_zx, proj_dt, dtb_f, alog_f, dsk_f, snw, "ssd_scan")
    uc = conf_conv_fwd(proj_cf, 0, D_CONF // CB, small["conf_conv_w"], small["conf_conv_b"], "conf_conv")
    w_out = get_w("w_out", uc)
    mix, u, x1, h2 = mixer_out(ysn, uc, small["conf_ln_w"], small["conf_ln_b"], w_out, x, mod, n2w, "out_proj_norm2")
    w_up_t = get_w("w_up", h2)
    up = mm_nt([(h2, w_up_t, 0)], "up_proj")
    act = ffn_conv_fwd(up, small["ffn_conv_w"], small["ffn_conv_b"], "ffn_conv")
    w_down = get_w("w_down", act)
    dx2, dff, dact, acc_f = final_loss(act, w_down, x1, mod, fnw, target, "down_proj_loss")

    token = put_grad("w_down", mm_tn(act, dff, "wgrad_down"))
    dupg, dupv, dwg, dwv = ffn_conv_bwd(up, small["ffn_conv_w"], after(small["ffn_conv_b"], token), dact, "ffn_conv_bwd")
    token = put_grad("w_up", jnp.concatenate([mm_tn(dupg, h2, "wgrad_up_gate"), mm_tn(dupv, h2, "wgrad_up_val")], axis=0))
    dx1, dmix, acc_2 = norm_mod_bwd([(dupg, w_up_t, 0), (dupv, w_up_t, 1)], x1, dx2, mod, after(n2w, token), 3, "norm2_bwd",
                                    mix=mix, gate_row=2)

    token = put_grad("w_out", jnp.concatenate([mm_tn(ysn, dmix, "wgrad_out_ssd"), mm_tn(u, dmix, "wgrad_out_conf")], axis=0))
    dmixin = mm_nt([(dmix, w_out, 0)], "dmixin")
    duc, acc_ln = ln_silu_bwd(dmixin, uc, after(small["conf_ln_w"], token), small["conf_ln_b"], "conf_ln_bwd")
    dcfa, dcfg, dw_cc = conf_conv_bwd(proj_cf, 0, D_CONF // CB, small["conf_conv_w"], duc, "conf_conv_bwd")
    g_cfa, g_cfg = mm_tn(dcfa, h1, "wgrad_in_cfa"), mm_tn(dcfg, h1, "wgrad_in_cfg")
    token = put_grad("w_in_conf", (g_cfa, g_cfg))
    dz, ddt, dxbc_post, acc_s, acc_s16 = ssd_bwd(dmixin, y, xbc, proj_zx, proj_dt, s_prev, dtb_f, alog_f, dsk_f,
                                                 after(snw, token), "ssd_scan_bwd")
    dxbc, dw_sc = conv_silu_bwd(proj_zx, D_SSD // CB, D_XBC, small["ssd_conv_w"], small["ssd_conv_b"], dxbc_post, "ssd_conv_bwd")
    token = put_grad("w_in_rest", (mm_tn(dz, h1, "wgrad_in_z"), mm_tn(dxbc, h1, "wgrad_in_xbc"),
                                   mm_tn(ddt, h1, "wgrad_in_dt")[:N_HEADS], g_cfa))
    dh1_pairs = [(dz, w_in["w_full"], 0), (ddt, w_in["w_dt16"], 0), (dcfa, w_in["w_cf"], 0), (dcfg, w_in["w_cf"], 1),
                 (dxbc, w_in["w_xbc"], 0)]
    grad_x, acc_1 = norm_mod_bwd(dh1_pairs, x, dx1, mod, after(n1w, token), 0, "norm1_bwd")

    small_accs = dict(acc_1=acc_1, acc_2=acc_2, acc_f=acc_f, acc_ln=acc_ln, acc_s=acc_s, acc_s16=acc_s16, dw_sc=dw_sc,
                      dw_cc=dw_cc, dwg=dwg, dwv=dwv)
    return grad_x, small_accs


def kernel(x, c, ada_w, ada_b, norm1_w, w_in, ssd_conv_w, ssd_conv_b, dt_bias, a_log, d_skip, ssd_norm_w, conf_conv_w, conf_conv_b, conf_ln_w, conf_ln_b, w_out, norm2_w, w_up, ffn_conv_w, ffn_conv_b, w_down, final_norm_w, loss_target, m_ada_w, m_ada_b, m_norm1_w, m_w_in, m_ssd_conv_w, m_ssd_conv_b, m_dt_bias, m_a_log, m_d_skip, m_ssd_norm_w, m_conf_conv_w, m_conf_conv_b, m_conf_ln_w, m_conf_ln_b, m_w_out, m_norm2_w, m_w_up, m_ffn_conv_w, m_ffn_conv_b, m_w_down, m_final_norm_w, v_ada_w, v_ada_b, v_norm1_w, v_w_in, v_ssd_conv_w, v_ssd_conv_b, v_dt_bias, v_a_log, v_d_skip, v_ssd_norm_w, v_conf_conv_w, v_conf_conv_b, v_conf_ln_w, v_conf_ln_b, v_w_out, v_norm2_w, v_w_up, v_ffn_conv_w, v_ffn_conv_b, v_w_down, v_final_norm_w):
    me = 4 * lax.axis_index("x") + 2 * lax.axis_index("y") + lax.axis_index("c")
    weights = dict(ada_w=ada_w, ada_b=ada_b, norm1_w=norm1_w, w_in=w_in, ssd_conv_w=ssd_conv_w, ssd_conv_b=ssd_conv_b,
                   dt_bias=dt_bias, a_log=a_log, d_skip=d_skip, ssd_norm_w=ssd_norm_w, conf_conv_w=conf_conv_w,
                   conf_conv_b=conf_conv_b, conf_ln_w=conf_ln_w, conf_ln_b=conf_ln_b, w_out=w_out, norm2_w=norm2_w, w_up=w_up,
                   ffn_conv_w=ffn_conv_w, ffn_conv_b=ffn_conv_b, w_down=w_down, final_norm_w=final_norm_w)
    moms_m = dict(ada_w=m_ada_w, ada_b=m_ada_b, norm1_w=m_norm1_w, w_in=m_w_in, ssd_conv_w=m_ssd_conv_w, ssd_conv_b=m_ssd_conv_b,
                  dt_bias=m_dt_bias, a_log=m_a_log, d_skip=m_d_skip, ssd_norm_w=m_ssd_norm_w, conf_conv_w=m_conf_conv_w,
                  conf_conv_b=m_conf_conv_b, conf_ln_w=m_conf_ln_w, conf_ln_b=m_conf_ln_b, w_out=m_w_out, norm2_w=m_norm2_w,
                  w_up=m_w_up, ffn_conv_w=m_ffn_conv_w, ffn_conv_b=m_ffn_conv_b, w_down=m_w_down, final_norm_w=m_final_norm_w)
    moms_v = dict(ada_w=v_ada_w, ada_b=v_ada_b, norm1_w=v_norm1_w, w_in=v_w_in, ssd_conv_w=v_ssd_conv_w, ssd_conv_b=v_ssd_conv_b,
                  dt_bias=v_dt_bias, a_log=v_a_log, d_skip=v_d_skip, ssd_norm_w=v_ssd_norm_w, conf_conv_w=v_conf_conv_w,
                  conf_conv_b=v_conf_conv_b, conf_ln_w=v_conf_ln_w, conf_ln_b=v_conf_ln_b, w_out=v_w_out, norm2_w=v_norm2_w,
                  w_up=v_w_up, ffn_conv_w=v_ffn_conv_w, ffn_conv_b=v_ffn_conv_b, w_down=v_w_down, final_norm_w=v_final_norm_w)
    names = list(weights)

    def to2d(a):
        return a[0] if a.ndim == 3 else a.reshape(1, -1)

    big = ("w_in", "w_out", "w_up", "w_down")

    c_all, scw_all, ccw_all, fcw_all = exchange([c.reshape(8, LANE), ssd_conv_w[0], conf_conv_w[0], ffn_conv_w[0]],
                                                "gather_small", gather=True)
    c_all = c_all.reshape(N_DEV, D)

    ada_cols = ada_w.shape[2]
    ada_b_cols = lax.dynamic_slice(ada_b, (0, me * ada_cols), (1, ada_cols))
    mod_cols, c_act_all = ada_mod(c_all, ada_w[0], ada_b_cols, "ada_mod")
    mod_parts, = exchange([jnp.pad(mod_cols, ((0, 0), (0, D - ada_cols))).reshape(N_DEV, 8, LANE)], "scatter_mod", gather=False)
    mod = mod_parts.reshape(N_DEV, D)[:, :ada_cols].reshape(6, D)
    mod = jnp.pad(mod, ((0, 2), (0, 0)))

    def rows_of(a):
        return jnp.swapaxes(a, 1, 2)[0] if a.shape[2] != D else a[0]

    shards, mod = lax.optimization_barrier(([rows_of(weights[n]).astype(BF16) for n in big], mod))
    w_in_slabs = gather_two_level(shards[0], "gather_w_in")
    later, w_in_slabs = lax.optimization_barrier((shards[1:], w_in_slabs))
    gather_parts, token = exchange_start(later, "gather_weights_start", gather=True)
    mod = mod + token[0:1, 0:1]

    small = {n: to2d(weights[n]) for n in names if n not in ("ada_w",) + big}
    small["ssd_conv_w"] = _slabs_to_cols(scw_all)
    small["conf_conv_w"] = _slabs_to_cols(ccw_all)
    small["ffn_conv_w"] = _slabs_to_cols(fcw_all)

    def with_own(landed, own):
        return lax.dynamic_update_slice(landed, own[None], (me,) + (0,) * own.ndim)

    def get_w(n, after):
        if n == "w_in":
            slabs = w_in_slabs
        else:
            a = big.index(n)
            landed, = exchange_wait([gather_parts[a - 1]], after, "gather_" + n + "_wait", gather=True)
            slabs = with_own(landed, shards[a])
        full = slabs.reshape(N_DEV * slabs.shape[1], D)
        if n != "w_in":
            return full
        w_dt = full[D_SSD + D_XBC:D_SSD + D_XBC + N_HEADS]
        return dict(w_full=full, w_xbc=full[D_SSD:D_SSD + D_XBC], w_cf=full[D_SSD + D_XBC + N_HEADS:],
                    w_dt_rep=jnp.repeat(w_dt, HEAD, axis=0), w_dt16=jnp.pad(w_dt, ((0, LANE - N_HEADS), (0, 0))))

    scatter_parts, sent = {}, {}

    split_dev = 5
    split_row = split_dev * (w_in.shape[2]) - (D_SSD + D_XBC + N_HEADS)
    ranges = {"w_in_conf": (split_dev, N_DEV), "w_in_rest": (0, split_dev)}

    def put_grad(n, g):
        if n == "w_in_conf":
            g = jnp.concatenate([g[0][split_row:], g[1]], axis=0)
        elif n == "w_in_rest":
            g = jnp.concatenate([g[0], g[1], g[2], g[3][:split_row]], axis=0)
        lo, hi = ranges.get(n, (0, N_DEV))
        sent[n] = g.reshape(hi - lo, g.shape[0] // (hi - lo), g.shape[1]).astype(BF16)
        (scatter_parts[n],), token = exchange_start([sent[n]], "scatter_" + n + "_start", gather=False, slots=(lo, hi))
        return token

    grad_x, accs = local_fwd_bwd(x[0], loss_target[0], mod, get_w, put_grad, small)
    loss = lax.psum(0.5 / D * jnp.sum(accs["acc_f"][2:3]), ("x", "y", "c"))

    grads, delta, new_m, new_v = {}, {}, {}, {}

    def landed_slabs(part_names, after, name):
        out = []
        for pn in part_names:
            lo, hi = ranges.get(pn, (0, N_DEV))
            slots, = exchange_wait([scatter_parts[pn]], after, name + "_" + pn, gather=False, slots=(lo, hi))
            own = lax.dynamic_index_in_dim(sent[pn], jnp.clip(me - lo, 0, hi - lo - 1), 0, keepdims=False)
            out.append(with_own(slots, own))
        return out

    def finish(n, slots, alt=None):
        out = adamw_slots(rows_of(weights[n]), slots, rows_of(moms_m[n]), rows_of(moms_v[n]), "adamw_" + n, alt=alt,
                          alt_from=split_dev)
        if weights[n].shape[2] != D:
            out = [jnp.swapaxes(o, 0, 1) for o in out]
        grads[n], delta[n], new_m[n], new_v[n] = out

    for n, slots in zip(big[1:], landed_slabs(big[1:], grad_x, "scatter_wait")):
        finish(n, slots)

    order = ("acc_1", "acc_2", "acc_f", "acc_ln", "acc_s", "acc_s16", "dw_sc", "dw_cc", "dwg", "dwv")
    shapes = [accs[k].shape for k in order]
    acc_list, _ = lax.optimization_barrier(([accs[k] for k in order], [new_v[n] for n in big[1:]]))
    packed_all, = exchange([pack_rows(acc_list, "pack_small_grads")], "gather_small_grads", gather=True)
    packed_red, = sum_slots_many([packed_all], "sum_small_grads")
    gathered = dict(zip(order, unpack_rows(packed_all, shapes)))
    red = dict(zip(order, unpack_rows(packed_red, shapes)))

    def mod_rows(a1, a2, af):
        return jnp.concatenate([a1[..., 0:2, :], a2[..., 3:4, :], a2[..., 0:2, :], af[..., 1:2, :]], axis=-2)

    dmod_all = mod_rows(gathered["acc_1"], gathered["acc_2"], gathered["acc_f"]).reshape(N_DEV, 6 * D)
    grads["ada_w"] = ada_wgrad(c_act_all, lax.dynamic_slice(dmod_all, (0, me * ada_cols), (N_DEV, ada_cols)), "ada_wgrad")

    def my_cols(full, k_taps):
        cols = full.shape[1] // N_DEV
        return lax.dynamic_slice(full, (0, me * cols), (k_taps, cols))

    fcw = jnp.concatenate([red["dwg"], red["dwv"]], axis=1)
    grads.update(
        ada_b=mod_rows(red["acc_1"], red["acc_2"], red["acc_f"]).reshape(1, 6 * D), norm1_w=red["acc_1"][2:3],
        ssd_conv_w=my_cols(red["dw_sc"], K_SSD), ssd_conv_b=red["dw_sc"][K_SSD:K_SSD + 1],
        dt_bias=red["acc_s16"][1:2, :N_HEADS], a_log=red["acc_s16"][2:3, :N_HEADS], d_skip=red["acc_s16"][3:4, :N_HEADS],
        ssd_norm_w=red["acc_s"][0:1], conf_conv_w=my_cols(red["dw_cc"], K_CONF), conf_conv_b=red["dw_cc"][K_CONF:K_CONF + 1],
        conf_ln_w=red["acc_ln"][0:1], conf_ln_b=red["acc_ln"][1:2], norm2_w=red["acc_2"][2:3],
        ffn_conv_w=my_cols(fcw, K_FFN), ffn_conv_b=fcw[K_FFN:K_FFN + 1], final_norm_w=red["acc_f"][0:1])

    rest = [n for n in names if n not in big]
    d_l, m_l, v_l = adamw_many([to2d(weights[n]) for n in rest], [grads[n] for n in rest], [to2d(moms_m[n]) for n in rest],
                               [to2d(moms_v[n]) for n in rest], "adamw_small")
    for n, dd, mm, vv in zip(rest, d_l, m_l, v_l):
        delta[n], new_m[n], new_v[n] = dd, mm, vv
    rest_slabs, conf_slabs = landed_slabs(("w_in_rest", "w_in_conf"), d_l[0], "scatter_wait")
    finish("w_in", rest_slabs, alt=conf_slabs)
    shape_of = lambda d_: {n: d_[n].reshape(weights[n].shape) for n in names}
    grads, delta, new_m, new_v = shape_of(grads), shape_of(delta), shape_of(new_m), shape_of(new_v)
    return (loss, grad_x[None], *[grads[n] for n in names], *[delta[n] for n in names], *[new_m[n] for n in names],
            *[new_v[n] for n in names])
```

```python
import functools

import jax
import jax.numpy as jnp
from jax import lax
from jax.experimental import pallas as pl
from jax.experimental.pallas import tpu as pltpu

F32 = jnp.float32
BF16 = jnp.bfloat16
HI = lax.Precision.HIGHEST

N_DEV = 8
D = 1024
D_SSD = 1024
HEAD = 64
N_HEADS = 16
N_STATE = 128
D_XBC = 1536
D_CONF = 1024
D_FF = 2816
K_SSD, K_CONF, K_FFN = 4, 31, 3
D_INP = 5632
LANE = 128
TR = 256
TM = 512
Q = 256
CB = 256
TC = 1024
VMEM_LIMIT = 56 * 1024 * 1024

ADAM_LR, ADAM_B1, ADAM_B2, ADAM_EPS, ADAM_WD, ADAM_STEP = 0.001, 0.9, 0.999, 1e-08, 0.01, 10


def _cparams(sem=None):
    return pltpu.CompilerParams(vmem_limit_bytes=VMEM_LIMIT, dimension_semantics=sem)


def _sds(shape, dtype):
    return jax.ShapeDtypeStruct(shape, dtype)


def _sigmoid(x):
    return 1.0 / (1.0 + jnp.exp(-x))


def _silu(x):
    return x * _sigmoid(x)


def _dsilu(x):
    s = _sigmoid(x)
    return s * (1.0 + x * (1.0 - s))


def _softplus(x):
    return jnp.maximum(x, 0.0) + jnp.log(1.0 + jnp.exp(-jnp.abs(x)))


def _dot(a, b):
    return jnp.dot(a.astype(BF16), b.astype(BF16), preferred_element_type=F32)


def _dot_nt(a, b):
    return lax.dot_general(a.astype(BF16), b.astype(BF16), (((1,), (1,)), ((), ())), preferred_element_type=F32)


def _dot_tn(a, b):
    return lax.dot_general(a.astype(BF16), b.astype(BF16), (((0,), (0,)), ((), ())), preferred_element_type=F32)


def _bf16_terms(a, terms):
    parts, rem = [], a
    for t in range(terms):
        p = rem.astype(BF16)
        parts.append(p)
        if t + 1 < terms:
            rem = rem - p.astype(F32)
    return parts


def _dot_exact(a, b, terms, exact, dims=(((1,), (0,)), ((), ()))):
    if exact == "a":
        a_b = a.astype(BF16)
        outs = [lax.dot_general(a_b, p, dims, preferred_element_type=F32) for p in _bf16_terms(b, terms)]
    else:
        b_b = b.astype(BF16)
        outs = [lax.dot_general(p, b_b, dims, preferred_element_type=F32) for p in _bf16_terms(a, terms)]
    acc = outs[-1]
    for o in reversed(outs[:-1]):
        acc = acc + o
    return acc


def _dot_tn_hi(a, b):
    return lax.dot_general(a, b, (((0,), (0,)), ((), ())), precision=HI, preferred_element_type=F32)


def _colsum(x):
    return jnp.sum(x, axis=0, keepdims=True)


def _const_spec(shape):
    return pl.BlockSpec(shape, lambda *_: (0,) * len(shape))


def _col_tile(n):
    for t in (1408, 1024, 768, 512, 256, 128):
        if n % t == 0 and t <= n:
            return t
    return n


def mm_nt(pairs, name):
    L = pairs[0][0].shape[0]
    K = pairs[0][1].shape[0]
    tk = _col_tile(K)
    n = len(pairs)

    def body(*refs):
        o_ref = refs[-1]
        acc = None
        for p in range(n):
            t = lax.dot_general(refs[2 * p][...], refs[2 * p + 1][...], (((1,), (1,)), ((), ())),
                                preferred_element_type=F32)
            acc = t if acc is None else acc + t
        o_ref[...] = acc

    in_specs, args = [], []
    for a, w, cb in pairs:
        in_specs += [pl.BlockSpec((TM, a.shape[1]), lambda j, i: (i, 0)),
                     pl.BlockSpec((tk, a.shape[1]), functools.partial(lambda j, i, cb: (j, cb), cb=cb))]
        args += [a, w]
    return pl.pallas_call(
        body, name=name, grid=(K // tk, L // TM), out_shape=_sds((L, K), F32), in_specs=in_specs,
        out_specs=pl.BlockSpec((TM, tk), lambda j, i: (i, j)),
        compiler_params=_cparams(("parallel", "parallel")))(*args)


def mm_tn(a, g, name):
    L, M = a.shape
    N = g.shape[1]
    tn = _col_tile(N) if N > 1024 else N
    if M * tn * 4 > 12 * 1024 * 1024:
        tn = 512
    tl = 512 if L % 512 == 0 else TR
    nl = L // tl

    def body(a_ref, g_ref, o_ref, acc_ref):
        @pl.when(pl.program_id(1) == 0)
        def _():
            acc_ref[...] = jnp.zeros((M, tn), F32)

        acc_ref[...] += lax.dot_general(a_ref[...], g_ref[...], (((0,), (0,)), ((), ())), preferred_element_type=F32)

        @pl.when(pl.program_id(1) == nl - 1)
        def _():
            o_ref[...] = acc_ref[...].astype(BF16)

    return pl.pallas_call(
        body, name=name, grid=(N // tn, nl), out_shape=_sds((M, N), BF16),
        in_specs=[pl.BlockSpec((tl, M), lambda j, l: (l, 0)), pl.BlockSpec((tl, tn), lambda j, l: (l, j))],
        out_specs=pl.BlockSpec((M, tn), lambda j, l: (0, j)), scratch_shapes=[pltpu.VMEM((M, tn), F32)],
        compiler_params=_cparams(("parallel", "arbitrary")))(a, g)


def _row_spec(width=D):
    return pl.BlockSpec((TR, width), lambda i: (i, 0))


def in_proj(x, mod, n1w, w_full, w_dt_rep, w_cf, name):
    L = x.shape[0]
    n_zx = D_SSD + D_XBC

    def body(x_ref, mod_ref, w_ref, wzx_ref, wdt_ref, wcf_ref, h_ref, zx_ref, dt_ref, cf_ref):
        xin = x_ref[...]
        r = lax.rsqrt(jnp.mean(xin * xin, axis=-1, keepdims=True) + 1e-6)
        h = ((xin * r * w_ref[...]) * (1.0 + mod_ref[1:2, :]) + mod_ref[0:1, :]).astype(BF16)
        h_ref[...] = h
        nt = (((1,), (1,)), ((), ()))
        zx_ref[...] = lax.dot_general(h, wzx_ref[...], nt, preferred_element_type=F32)
        dt_ref[...] = lax.dot_general(h, wdt_ref[...], nt, preferred_element_type=F32)
        cf_ref[...] = lax.dot_general(h, wcf_ref[...], nt, preferred_element_type=F32)

    row = lambda w: pl.BlockSpec((TM, w), lambda i: (i, 0))
    return pl.pallas_call(
        body, name=name, grid=(L // TM,),
        out_shape=[_sds((L, D), BF16), _sds((L, n_zx), F32), _sds((L, D_SSD), F32), _sds((L, 2 * D_CONF), F32)],
        in_specs=[row(D), _const_spec((8, D)), _const_spec((1, D)), _const_spec((n_zx, D)), _const_spec((D_SSD, D)),
                  _const_spec((2 * D_CONF, D))],
        out_specs=[row(D), row(n_zx), row(D_SSD), row(2 * D_CONF)],
        compiler_params=_cparams(("parallel",)))(x, mod, n1w, w_full, w_dt_rep, w_cf)


def mixer_out(ysn, uc, lnw, lnb, w_out, x, mod, n2w, name):
    L = x.shape[0]

    def body(ysn_ref, uc_ref, lnw_ref, lnb_ref, wo_ref, x_ref, mod_ref, n2w_ref, mix_ref, u_ref, x1_ref, h2_ref):
        uc_v = uc_ref[...]
        mu = jnp.mean(uc_v, axis=-1, keepdims=True)
        var = jnp.mean(jnp.square(uc_v - mu), axis=-1, keepdims=True)
        u = _silu((uc_v - mu) * lax.rsqrt(var + 1e-5) * lnw_ref[...] + lnb_ref[...]).astype(BF16)
        u_ref[...] = u
        mix = (jnp.dot(ysn_ref[...], wo_ref[0:D_SSD, :], preferred_element_type=F32)
               + jnp.dot(u, wo_ref[D_SSD:D_SSD + D_CONF, :], preferred_element_type=F32))
        mix_ref[...] = mix
        x1 = x_ref[...] + mod_ref[2:3, :] * mix
        x1_ref[...] = x1
        r = lax.rsqrt(jnp.mean(x1 * x1, axis=-1, keepdims=True) + 1e-6)
        h2_ref[...] = ((x1 * r * n2w_ref[...]) * (1.0 + mod_ref[4:5, :]) + mod_ref[3:4, :]).astype(BF16)

    return pl.pallas_call(
        body, name=name, grid=(L // TR,),
        out_shape=[_sds((L, D), F32), _sds((L, D_CONF), BF16), _sds((L, D), F32), _sds((L, D), BF16)],
        in_specs=[_row_spec(), _row_spec(), _const_spec((1, D)), _const_spec((1, D)), _const_spec((D_SSD + D_CONF, D)), _row_spec(),
                  _const_spec((8, D)), _const_spec((1, D))],
        out_specs=[_row_spec()] * 4, compiler_params=_cparams(("parallel",)))(ysn, uc, lnw, lnb, w_out, x, mod, n2w)


def mixer_out_bwd(dmix, w_out, uc, lnw, lnb, name):
    L = uc.shape[0]

    def body(dm_ref, wo_ref, u_ref, w_ref, b_ref, dy_ref, o_ref, acc_ref):
        @pl.when(pl.program_id(0) == 0)
        def _():
            acc_ref[...] = jnp.zeros((8, D), F32)

        nt = (((1,), (1,)), ((), ()))
        dm = dm_ref[...]
        dy_ref[...] = lax.dot_general(dm, wo_ref[0:D_SSD, :], nt, preferred_element_type=F32)
        du = lax.dot_general(dm, wo_ref[D_SSD:D_SSD + D_CONF, :], nt, preferred_element_type=F32)
        u = u_ref[...]
        mu = jnp.mean(u, axis=-1, keepdims=True)
        rl = lax.rsqrt(jnp.mean(jnp.square(u - mu), axis=-1, keepdims=True) + 1e-5)
        n = (u - mu) * rl
        v = n * w_ref[...] + b_ref[...]
        dv = du * _dsilu(v)
        acc_ref[0:1, :] += _colsum(dv * n)
        acc_ref[1:2, :] += _colsum(dv)
        dn = dv * w_ref[...]
        o_ref[...] = rl * (dn - jnp.mean(dn, axis=-1, keepdims=True) - n * jnp.mean(dn * n, axis=-1, keepdims=True))

    return pl.pallas_call(body, name=name, grid=(L // TR,), out_shape=[_sds((L, D), F32), _sds((L, D), F32), _sds((8, D), F32)],
                          in_specs=[_row_spec(), _const_spec((D_SSD + D_CONF, D)), _row_spec(), _const_spec((1, D)),
                                    _const_spec((1, D))],
                          out_specs=[_row_spec(), _row_spec(), _const_spec((8, D))],
                          compiler_params=_cparams(("arbitrary",)))(dmix, w_out, uc, lnw, lnb)


def final_loss(act, w_down, x1, mod, fw, target, name):
    L = act.shape[0]

    def body(act_ref, wd_ref, x1_ref, mod_ref, fw_ref, t_ref, dx_ref, dff_ref, dact_ref, acc_ref):
        @pl.when(pl.program_id(0) == 0)
        def _():
            acc_ref[...] = jnp.zeros((8, D), F32)

        ff_v = jnp.dot(act_ref[...], wd_ref[...], preferred_element_type=F32)
        g2 = mod_ref[5:6, :]
        x2 = x1_ref[...] + g2 * ff_v
        r = lax.rsqrt(jnp.mean(x2 * x2, axis=-1, keepdims=True) + 1e-6)
        n = x2 * r
        err = n * fw_ref[...] - t_ref[...]
        dy = err * (1.0 / D)
        dn = dy * fw_ref[...]
        dx2 = r * (dn - n * jnp.mean(dn * n, axis=-1, keepdims=True))
        acc_ref[0:1, :] += _colsum(dy * n)
        acc_ref[1:2, :] += _colsum(dx2 * ff_v)
        acc_ref[2:3, :] += _colsum(err * err)
        dx_ref[...] = dx2
        dff = (dx2 * g2).astype(BF16)
        dff_ref[...] = dff
        dact_ref[...] = lax.dot_general(dff, wd_ref[...], (((1,), (1,)), ((), ())), preferred_element_type=F32)

    return pl.pallas_call(
        body, name=name, grid=(L // TR,),
        out_shape=[_sds((L, D), F32), _sds((L, D), BF16), _sds((L, D_FF), F32), _sds((8, D), F32)],
        in_specs=[_row_spec(D_FF), _const_spec((D_FF, D)), _row_spec(), _const_spec((8, D)), _const_spec((1, D)), _row_spec()],
        out_specs=[_row_spec(), _row_spec(), _row_spec(D_FF), _const_spec((8, D))],
        compiler_params=_cparams(("arbitrary",)))(act, w_down, x1, mod, fw, target)


def norm_mod_bwd(dh_pairs, xin, dres, mod, w, shift_row, name, mix=None, gate_row=None):
    L = xin.shape[0]
    has_mix = mix is not None
    n_pairs = len(dh_pairs)

    def body(*refs):
        pair_refs, refs = refs[:2 * n_pairs], refs[2 * n_pairs:]
        if has_mix:
            x_ref, dres_ref, mod_ref, w_ref, mix_ref, dx_ref, dmix_ref, acc_ref = refs
        else:
            x_ref, dres_ref, mod_ref, w_ref, dx_ref, acc_ref = refs

        @pl.when(pl.program_id(0) == 0)
        def _():
            acc_ref[...] = jnp.zeros((8, D), F32)

        dh_v = None
        for p in range(n_pairs):
            t = jnp.dot(pair_refs[2 * p][...], pair_refs[2 * p + 1][...], preferred_element_type=F32)
            dh_v = t if dh_v is None else dh_v + t
        x = x_ref[...]
        r = lax.rsqrt(jnp.mean(x * x, axis=-1, keepdims=True) + 1e-6)
        n = x * r
        nw = n * w_ref[...]
        sc1 = 1.0 + mod_ref[shift_row + 1:shift_row + 2, :]
        acc_ref[0:1, :] += _colsum(dh_v)
        acc_ref[1:2, :] += _colsum(dh_v * nw)
        dnw = dh_v * sc1
        acc_ref[2:3, :] += _colsum(dnw * n)
        dn = dnw * w_ref[...]
        dx = r * (dn - n * jnp.mean(dn * n, axis=-1, keepdims=True)) + dres_ref[...]
        dx_ref[...] = dx
        if has_mix:
            acc_ref[3:4, :] += _colsum(dx * mix_ref[...])
            dmix_ref[...] = (dx * mod_ref[gate_row:gate_row + 1, :]).astype(BF16)

    ins, in_specs = [], []
    for a, wt, rb in dh_pairs:
        ins += [a, wt]
        in_specs += [_row_spec(a.shape[1]), pl.BlockSpec((a.shape[1], D), functools.partial(lambda i, rb: (rb, 0), rb=rb))]
    ins += [xin, dres, mod, w] + ([mix] if has_mix else [])
    in_specs += [_row_spec(), _row_spec(), _const_spec((8, D)), _const_spec((1, D))] + ([_row_spec()] if has_mix else [])
    out_shape = [_sds((L, D), F32)] + ([_sds((L, D), BF16)] if has_mix else []) + [_sds((8, D), F32)]
    out_specs = [_row_spec()] + ([_row_spec()] if has_mix else []) + [_const_spec((8, D))]
    return pl.pallas_call(body, name=name, grid=(L // TR,), out_shape=out_shape, in_specs=in_specs,
                          out_specs=out_specs, compiler_params=_cparams(("arbitrary",)))(*ins)


def _halo(k):
    return 8 if k <= 9 else 32


def _prev_spec(h, col0):
    return pl.BlockSpec((h, CB), lambda j, i: (jnp.maximum(i * (TC // h) - 1, 0), j + col0))


def _next_spec(h, col0, n_tiles):
    return pl.BlockSpec((h, CB), lambda j, i: (jnp.minimum(i + 1, n_tiles - 1) * (TC // h), j + col0))


def _tile_spec(col0):
    return pl.BlockSpec((TC, CB), lambda j, i: (i, j + col0))


def _w_spec(kp, col0):
    return pl.BlockSpec((kp, CB), lambda j, i: (0, j + col0))


SUBLANES = 8


def _shifted_windows(v, taps, rows):
    for r in range(SUBLANES):
        group = [(o, k) for o, k in taps if o % SUBLANES == r]
        if not group:
            continue
        s = v if r == 0 else pltpu.roll(v, v.shape[0] - r, 0)
        for o, k in group:
            yield k, s[o - r:o - r + rows, :]


def _causal_taps(ext_ref, w_ref, k_taps, first, rows):
    acc = None
    for k, win in _shifted_windows(ext_ref[...], [(first - (k_taps - 1) + k, k) for k in range(k_taps)], rows):
        t = w_ref[k:k + 1, :] * win
        acc = t if acc is None else acc + t
    return acc


def _anticausal_taps(d_ref, w_ref, k_taps, rows):
    acc = None
    for k, win in _shifted_windows(d_ref[...], [(k_taps - 1 - k, k) for k in range(k_taps)], rows):
        t = w_ref[k:k + 1, :] * win
        acc = t if acc is None else acc + t
    return acc


def _acc_conv_wgrad(dw_ref, d_tile, ext_ref, k_taps, first):
    for k, win in _shifted_windows(ext_ref[...], [(first - (k_taps - 1) + k, k) for k in range(k_taps)], TC):
        dw_ref[k:k + 1, :] += _colsum(d_tile * win)
    dw_ref[k_taps:k_taps + 1, :] += _colsum(d_tile)


def conv_silu_fwd(x, col0, width, w, b, name):
    L = x.shape[0]
    k_taps = w.shape[0]
    h = _halo(k_taps)

    def body(xp_ref, x_ref, w_ref, b_ref, o_ref, ext_ref):
        i = pl.program_id(1)
        ext_ref[0:h, :] = jnp.where(i > 0, xp_ref[...], 0.0)
        ext_ref[h:h + TC, :] = x_ref[...]
        o_ref[...] = _silu(_causal_taps(ext_ref, w_ref, k_taps, h, TC) + b_ref[...])

    return pl.pallas_call(
        body, name=name, grid=(width // CB, L // TC), out_shape=_sds((L, width), F32),
        in_specs=[_prev_spec(h, col0), _tile_spec(col0), _w_spec(k_taps, 0), pl.BlockSpec((1, CB), lambda j, i: (0, j))],
        out_specs=_tile_spec(0), scratch_shapes=[pltpu.VMEM((h + TC, CB), F32)],
        compiler_params=_cparams(("parallel", "parallel")))(x, x, w, b)


def conv_silu_bwd(x, col0, width, w, b, dpost, name):
    L = x.shape[0]
    k_taps = w.shape[0]
    h = _halo(k_taps)
    nt = L // TC

    def body(xp_ref, x_ref, xn_ref, d_ref, dn_ref, w_ref, b_ref, dx_ref, dw_ref, ext_ref, dpre_ref):
        i = pl.program_id(1)

        @pl.when(i == 0)
        def _():
            dw_ref[...] = jnp.zeros((8, CB), F32)

        ext_ref[0:h, :] = jnp.where(i > 0, xp_ref[...], 0.0)
        ext_ref[h:h + TC, :] = x_ref[...]
        ext_ref[h + TC:h + TC + h, :] = xn_ref[...]
        pre = _causal_taps(ext_ref, w_ref, k_taps, h, TC + h) + b_ref[...]
        dpre_ref[0:TC, :] = d_ref[...] * _dsilu(pre[0:TC, :])
        dpre_ref[TC:TC + h, :] = jnp.where(i < nt - 1, dn_ref[...], 0.0) * _dsilu(pre[TC:TC + h, :])
        dx_ref[...] = _anticausal_taps(dpre_ref, w_ref, k_taps, TC).astype(BF16)
        _acc_conv_wgrad(dw_ref, dpre_ref[0:TC, :], ext_ref, k_taps, h)

    return pl.pallas_call(
        body, name=name, grid=(width // CB, nt),
        out_shape=[_sds((L, width), BF16), _sds((8, width), F32)],
        in_specs=[_prev_spec(h, col0), _tile_spec(col0), _next_spec(h, col0, nt), _tile_spec(0), _next_spec(h, 0, nt),
                  _w_spec(k_taps, 0), pl.BlockSpec((1, CB), lambda j, i: (0, j))],
        out_specs=[_tile_spec(0), _w_spec(8, 0)],
        scratch_shapes=[pltpu.VMEM((h + TC + h, CB), F32), pltpu.VMEM((TC + h, CB), F32)],
        compiler_params=_cparams(("parallel", "arbitrary")))(x, x, x, dpost, dpost, w, b)


def conf_conv_fwd(proj, col_a, col_g, w, b, name):
    L = proj.shape[0]
    k_taps = w.shape[0]
    h = _halo(k_taps)

    def body(ap_ref, a_ref, gp_ref, g_ref, w_ref, b_ref, o_ref, ext_ref):
        i = pl.program_id(1)
        ext_ref[0:h, :] = jnp.where(i > 0, ap_ref[...] * _sigmoid(gp_ref[...]), 0.0)
        ext_ref[h:h + TC, :] = a_ref[...] * _sigmoid(g_ref[...])
        o_ref[...] = _causal_taps(ext_ref, w_ref, k_taps, h, TC) + b_ref[...]

    return pl.pallas_call(
        body, name=name, grid=(D_CONF // CB, L // TC), out_shape=_sds((L, D_CONF), F32),
        in_specs=[_prev_spec(h, col_a), _tile_spec(col_a), _prev_spec(h, col_g), _tile_spec(col_g), _w_spec(k_taps, 0),
                  pl.BlockSpec((1, CB), lambda j, i: (0, j))],
        out_specs=_tile_spec(0), scratch_shapes=[pltpu.VMEM((h + TC, CB), F32)],
        compiler_params=_cparams(("parallel", "parallel")))(proj, proj, proj, proj, w, b)


def conf_conv_bwd(proj, col_a, col_g, w, duc, name):
    L = proj.shape[0]
    k_taps = w.shape[0]
    h = _halo(k_taps)
    nt = L // TC

    def body(ap_ref, a_ref, gp_ref, g_ref, d_ref, dn_ref, w_ref, da_ref, dg_ref, dw_ref, ext_ref, dext_ref):
        i = pl.program_id(1)

        @pl.when(i == 0)
        def _():
            dw_ref[...] = jnp.zeros((32, CB), F32)

        a = a_ref[...]
        s = _sigmoid(g_ref[...])
        ext_ref[0:h, :] = jnp.where(i > 0, ap_ref[...] * _sigmoid(gp_ref[...]), 0.0)
        ext_ref[h:h + TC, :] = a * s
        dext_ref[0:TC, :] = d_ref[...]
        dext_ref[TC:TC + h, :] = jnp.where(i < nt - 1, dn_ref[...], 0.0)
        du0 = _anticausal_taps(dext_ref, w_ref, k_taps, TC)
        da_ref[...] = (du0 * s).astype(BF16)
        dg_ref[...] = (du0 * a * s * (1.0 - s)).astype(BF16)
        _acc_conv_wgrad(dw_ref, d_ref[...], ext_ref, k_taps, h)

    return pl.pallas_call(
        body, name=name, grid=(D_CONF // CB, nt),
        out_shape=[_sds((L, D_CONF), BF16), _sds((L, D_CONF), BF16), _sds((32, D_CONF), F32)],
        in_specs=[_prev_spec(h, col_a), _tile_spec(col_a), _prev_spec(h, col_g), _tile_spec(col_g), _tile_spec(0),
                  _next_spec(h, 0, nt), _w_spec(k_taps, 0)],
        out_specs=[_tile_spec(0), _tile_spec(0), _w_spec(32, 0)],
        scratch_shapes=[pltpu.VMEM((h + TC, CB), F32), pltpu.VMEM((TC + h, CB), F32)],
        compiler_params=_cparams(("parallel", "arbitrary")))(proj, proj, proj, proj, duc, duc, w)


def ffn_conv_fwd(up, w, b, name):
    L = up.shape[0]
    k_taps = w.shape[0]
    h = _halo(k_taps)
    cv = D_FF // CB

    def body(gp_ref, g_ref, vp_ref, v_ref, wg_ref, wv_ref, bg_ref, bv_ref, o_ref, eg_ref, ev_ref):
        i = pl.program_id(1)
        eg_ref[0:h, :] = jnp.where(i > 0, gp_ref[...], 0.0)
        eg_ref[h:h + TC, :] = g_ref[...]
        ev_ref[0:h, :] = jnp.where(i > 0, vp_ref[...], 0.0)
        ev_ref[h:h + TC, :] = v_ref[...]
        pg = _causal_taps(eg_ref, wg_ref, k_taps, h, TC) + bg_ref[...]
        pv = _causal_taps(ev_ref, wv_ref, k_taps, h, TC) + bv_ref[...]
        o_ref[...] = (_silu(pg) * pv).astype(BF16)

    bspec = lambda c0: pl.BlockSpec((1, CB), lambda j, i: (0, j + c0))
    return pl.pallas_call(
        body, name=name, grid=(cv, L // TC), out_shape=_sds((L, D_FF), BF16),
        in_specs=[_prev_spec(h, 0), _tile_spec(0), _prev_spec(h, cv), _tile_spec(cv), _w_spec(k_taps, 0), _w_spec(k_taps, cv),
                  bspec(0), bspec(cv)],
        out_specs=_tile_spec(0), scratch_shapes=[pltpu.VMEM((h + TC, CB), F32), pltpu.VMEM((h + TC, CB), F32)],
        compiler_params=_cparams(("parallel", "parallel")))(up, up, up, up, w, w, b, b)


def ffn_conv_bwd(up, w, b, dact, name):
    L = up.shape[0]
    k_taps = w.shape[0]
    h = _halo(k_taps)
    nt = L // TC
    cv = D_FF // CB

    def body(gp_ref, g_ref, gn_ref, vp_ref, v_ref, vn_ref, d_ref, dn_ref, wg_ref, wv_ref, bg_ref, bv_ref,
             dg_ref, dv_ref, dwg_ref, dwv_ref, eg_ref, ev_ref, pg_ref, pv_ref):
        i = pl.program_id(1)

        @pl.when(i == 0)
        def _():
            dwg_ref[...] = jnp.zeros((8, CB), F32)
            dwv_ref[...] = jnp.zeros((8, CB), F32)

        for e_ref, p_ref, c_ref, n_ref in ((eg_ref, gp_ref, g_ref, gn_ref), (ev_ref, vp_ref, v_ref, vn_ref)):
            e_ref[0:h, :] = jnp.where(i > 0, p_ref[...], 0.0)
            e_ref[h:h + TC, :] = c_ref[...]
            e_ref[h + TC:h + TC + h, :] = n_ref[...]
        pg = _causal_taps(eg_ref, wg_ref, k_taps, h, TC + h) + bg_ref[...]
        pv = _causal_taps(ev_ref, wv_ref, k_taps, h, TC + h) + bv_ref[...]
        dact_t = d_ref[...]
        dact_n = jnp.where(i < nt - 1, dn_ref[...], 0.0)
        pg_ref[0:TC, :] = dact_t * pv[0:TC, :] * _dsilu(pg[0:TC, :])
        pg_ref[TC:TC + h, :] = dact_n * pv[TC:TC + h, :] * _dsilu(pg[TC:TC + h, :])
        pv_ref[0:TC, :] = dact_t * _silu(pg[0:TC, :])
        pv_ref[TC:TC + h, :] = dact_n * _silu(pg[TC:TC + h, :])
        dg_ref[...] = _anticausal_taps(pg_ref, wg_ref, k_taps, TC).astype(BF16)
        dv_ref[...] = _anticausal_taps(pv_ref, wv_ref, k_taps, TC).astype(BF16)
        _acc_conv_wgrad(dwg_ref, pg_ref[0:TC, :], eg_ref, k_taps, h)
        _acc_conv_wgrad(dwv_ref, pv_ref[0:TC, :], ev_ref, k_taps, h)

    bspec = lambda c0: pl.BlockSpec((1, CB), lambda j, i: (0, j + c0))
    ext = pltpu.VMEM((h + TC + h, CB), F32)
    dpre = pltpu.VMEM((TC + h, CB), F32)
    return pl.pallas_call(
        body, name=name, grid=(cv, nt),
        out_shape=[_sds((L, D_FF), BF16), _sds((L, D_FF), BF16), _sds((8, D_FF), F32), _sds((8, D_FF), F32)],
        in_specs=[_prev_spec(h, 0), _tile_spec(0), _next_spec(h, 0, nt), _prev_spec(h, cv), _tile_spec(cv), _next_spec(h, cv, nt),
                  _tile_spec(0), _next_spec(h, 0, nt), _w_spec(k_taps, 0), _w_spec(k_taps, cv), bspec(0), bspec(cv)],
        out_specs=[_tile_spec(0), _tile_spec(0), _w_spec(8, 0), _w_spec(8, 0)],
        scratch_shapes=[ext, ext, dpre, dpre],
        compiler_params=_cparams(("parallel", "arbitrary")))(up, up, up, up, up, up, dact, dact, w, w, b, b)


def _ssd_common(xbc_ref, dt_ref, dtb_ref, alog_ref, cs_ref):
    xs = xbc_ref[:, 0:D_SSD]
    sp_in = dt_ref[...] + dtb_ref[...]
    dtf = _softplus(sp_in)
    a_f = -jnp.exp(alog_ref[...])
    a_dt = dtf * a_f
    row = lax.broadcasted_iota(jnp.int32, (Q, Q), 0)
    col = lax.broadcasted_iota(jnp.int32, (Q, Q), 1)
    causal = row >= col
    cs = _dot_exact(causal.astype(F32), a_dt, 3, "a")
    cs_ref[...] = cs
    cs_last = cs_ref[Q - 1:Q, :]
    return xs, sp_in, dtf, a_f, cs, cs_last, causal


def _head_decay(cs_j, cst_ref, e, causal):
    lane = lax.broadcasted_iota(jnp.int32, (Q, LANE), 1)
    rolled = pltpu.roll(cs_j, HEAD, 1)
    own = (lane < HEAD) if e == 0 else (lane >= HEAD)
    col_b = jnp.where(own, cs_j, rolled)
    col_b = jnp.concatenate([col_b] * (Q // LANE), axis=1)
    row_b = cst_ref[e * HEAD:e * HEAD + 1, :]
    return jnp.where(causal, jnp.exp(jnp.minimum(col_b - row_b, 0.0)), 0.0)


def ssd_fwd(xbc, z_src, dt_src, dtb_f, alog_f, dsk_f, snw, name):
    L = xbc.shape[0]
    nc = L // Q

    def body(xbc_ref, z_ref, dt_ref, dtb_ref, alog_ref, dsk_ref, snw_ref, y_ref, yn_ref, sp_ref, s_ref, cs_ref, cst_ref, yd_ref):
        @pl.when(pl.program_id(0) == 0)
        def _():
            s_ref[...] = jnp.zeros((N_STATE, D_SSD), F32)

        xs, _, dtf, a_f, cs, cs_last, causal = _ssd_common(xbc_ref, dt_ref, dtb_ref, alog_ref, cs_ref)
        e_cs = jnp.exp(cs)
        xdt = xs * dtf
        zst = jnp.exp(cs_last - cs) * xdt
        sp_ref[0] = s_ref[...]
        lane = lax.broadcasted_iota(jnp.int32, (Q, LANE), 1)
        for g in range(2):
            gl = slice(g * 512, g * 512 + 512)
            b_g = xbc_ref[:, D_SSD + g * N_STATE:D_SSD + (g + 1) * N_STATE]
            c_g = xbc_ref[:, D_SSD + 2 * N_STATE + g * N_STATE:D_SSD + 2 * N_STATE + (g + 1) * N_STATE]
            s_prev = s_ref[:, gl]
            cb = _dot_nt(c_g, b_g)
            yd_ref[:, gl] = e_cs[:, gl] * _dot(c_g, s_prev)
            for j in range(4):
                tl = slice(g * 512 + j * LANE, g * 512 + (j + 1) * LANE)
                cs_j = cs[:, tl]
                cst_ref[...] = cs_j.T
                x_j = xdt[:, tl]
                o0 = _dot(cb * _head_decay(cs_j, cst_ref, 0, causal), x_j)
                o1 = _dot(cb * _head_decay(cs_j, cst_ref, 1, causal), x_j)
                yd_ref[:, tl] += jnp.where(lane < HEAD, o0, o1)
            s_ref[:, gl] = jnp.exp(cs_last[:, gl]) * s_prev + _dot_tn(b_g, zst[:, gl])
        y = yd_ref[...] + xs * dsk_ref[...]
        y_ref[...] = y
        yz = y * _silu(z_ref[...])
        r = lax.rsqrt(jnp.mean(yz * yz, axis=-1, keepdims=True) + 1e-6)
        yn_ref[...] = (yz * r * snw_ref[...]).astype(BF16)

    chunk = lambda w, c: pl.BlockSpec((Q, w), lambda i: (i, c))
    return pl.pallas_call(
        body, name=name, grid=(nc,),
        out_shape=[_sds((L, D_SSD), F32), _sds((L, D_SSD), BF16), _sds((nc, N_STATE, D_SSD), F32)],
        in_specs=[chunk(D_XBC, 0), chunk(D, 0), chunk(D, 0)] + [_const_spec((1, D))] * 4,
        out_specs=[chunk(D, 0), chunk(D, 0), pl.BlockSpec((1, N_STATE, D_SSD), lambda i: (i, 0, 0))],
        scratch_shapes=[pltpu.VMEM((N_STATE, D_SSD), F32), pltpu.VMEM((Q, D_SSD), F32), pltpu.VMEM((LANE, Q), F32),
                        pltpu.VMEM((Q, D_SSD), F32)],
        compiler_params=_cparams(("arbitrary",)))(xbc, z_src, dt_src, dtb_f, alog_f, dsk_f, snw)


def ssd_bwd(dysn, y, xbc, z_src, dt_src, s_prev_all, dtb_f, alog_f, dsk_f, snw, name):
    L = xbc.shape[0]
    nc = L // Q

    def body(dyn_ref, y_ref, xbc_ref, z_ref, dt_ref, sp_ref, dtb_ref, alog_ref, dsk_ref, snw_ref,
             dz_ref, ddt_ref, dxbc_ref, acc_ref, acc16_ref, ds_ref, cs_ref, cst_ref, dcs_ref, dx_ref):
        step = pl.program_id(0)

        @pl.when(step == 0)
        def _():
            ds_ref[...] = jnp.zeros((N_STATE, D_SSD), F32)
            acc_ref[...] = jnp.zeros((8, D), F32)

        z = z_ref[...]
        y = y_ref[...]
        sz = _sigmoid(z)
        siluz = z * sz
        yz = y * siluz
        r = lax.rsqrt(jnp.mean(yz * yz, axis=-1, keepdims=True) + 1e-6)
        n = yz * r
        dyn = dyn_ref[...]
        acc_ref[0:1, :] += _colsum(dyn * n)
        dn = dyn * snw_ref[...]
        dyz = r * (dn - n * jnp.mean(dn * n, axis=-1, keepdims=True))
        dy = dyz * siluz
        dz_ref[...] = (dyz * y * (sz * (1.0 + z * (1.0 - sz)))).astype(BF16)

        xs, sp_in, dtf, a_f, cs, cs_last, causal = _ssd_common(xbc_ref, dt_ref, dtb_ref, alog_ref, cs_ref)
        acc_ref[3:4, :] += _colsum(dy * xs)
        e_cs = jnp.exp(cs)
        xdt = xs * dtf
        dst = jnp.exp(cs_last - cs)
        zst = dst * xdt
        e_last = jnp.exp(cs_last)
        lane = lax.broadcasted_iota(jnp.int32, (Q, LANE), 1)
        ones = jnp.ones((Q, LANE), F32)
        dcs_last_parts = []
        for g in range(2):
            gl = slice(g * 512, g * 512 + 512)
            b_g = xbc_ref[:, D_SSD + g * N_STATE:D_SSD + (g + 1) * N_STATE]
            c_g = xbc_ref[:, D_SSD + 2 * N_STATE + g * N_STATE:D_SSD + 2 * N_STATE + (g + 1) * N_STATE]
            s_prev = sp_ref[0, :, gl]
            ds_g = ds_ref[:, gl]
            dy_g = dy[:, gl]
            cb = _dot_nt(c_g, b_g)
            y_off = e_cs[:, gl] * _dot(c_g, s_prev)
            edy = e_cs[:, gl] * dy_g
            d_c = _dot_nt(edy, s_prev)
            d_z = _dot(b_g, ds_g)
            d_b = _dot_nt(zst[:, gl], ds_g)
            t_g = d_z * zst[:, gl]
            dcs_ref[:, gl] = dy_g * y_off - t_g
            dx_ref[:, gl] = d_z * dst[:, gl]
            dcs_last_parts.append(_colsum(t_g) + _colsum(ds_g * s_prev) * e_last[:, gl])
            ds_ref[:, gl] = e_last[:, gl] * ds_g + _dot_tn(c_g, edy)
            dcb = jnp.zeros((Q, Q), F32)
            for j in range(4):
                tl = slice(g * 512 + j * LANE, g * 512 + (j + 1) * LANE)
                cs_j = cs[:, tl]
                cst_ref[...] = cs_j.T
                x_j = xdt[:, tl]
                dy_j = dy[:, tl]
                dx_j = jnp.zeros((Q, LANE), F32)
                dcs_j = jnp.zeros((Q, LANE), F32)
                for e in range(2):
                    own = (lane < HEAD) if e == 0 else (lane >= HEAD)
                    w_h = _head_decay(cs_j, cst_ref, e, causal)
                    g_h = cb * w_h
                    dy_m = jnp.where(own, dy_j, 0.0)
                    d_g = _dot_nt(dy_m, x_j)
                    dx_j = dx_j + _dot_tn(g_h, dy_m)
                    dcb = dcb + d_g * w_h
                    p_h = d_g * g_h
                    row_sums = _dot_exact(p_h, ones, 2, "b")
                    col_sums = _dot_exact(p_h, ones, 2, "b", (((0,), (0,)), ((), ())))
                    dcs_j = dcs_j + jnp.where(own, row_sums - col_sums, 0.0)
                dcs_ref[:, tl] += dcs_j * (1.0 / HEAD)
                dx_ref[:, tl] += dx_j
            d_c = d_c + _dot(dcb, b_g)
            d_b = d_b + _dot_tn(dcb, c_g)
            dxbc_ref[:, D_SSD + g * N_STATE:D_SSD + (g + 1) * N_STATE] = d_b
            dxbc_ref[:, D_SSD + 2 * N_STATE + g * N_STATE:D_SSD + 2 * N_STATE + (g + 1) * N_STATE] = d_c
        dcs_last = jnp.concatenate(dcs_last_parts, axis=1)
        anticausal = lax.broadcasted_iota(jnp.int32, (Q, Q), 0) <= lax.broadcasted_iota(jnp.int32, (Q, Q), 1)
        d_adt = _dot_exact(anticausal.astype(F32), dcs_ref[...], 3, "a") + dcs_last
        dx = dx_ref[...]
        acc_ref[2:3, :] += _colsum(d_adt * dtf) * a_f
        d_dtf = d_adt * a_f + dx * xs
        dxbc_ref[:, 0:D_SSD] = dx * dtf + dy * dsk_ref[...]
        d_raw = d_dtf * _sigmoid(sp_in)
        acc_ref[1:2, :] += _colsum(d_raw)
        head_of_lane = lax.broadcasted_iota(jnp.int32, (D_SSD, LANE), 0) // HEAD
        fold = (head_of_lane == lax.broadcasted_iota(jnp.int32, (D_SSD, LANE), 1)).astype(F32)
        ddt_ref[...] = _dot_exact(d_raw, fold, 2, "b").astype(BF16)

        @pl.when(step == nc - 1)
        def _():
            acc16_ref[...] = _dot_exact(acc_ref[...], fold, 3, "b")

    rchunk = lambda w, c: pl.BlockSpec((Q, w), lambda i: (nc - 1 - i, c))
    return pl.pallas_call(
        body, name=name, grid=(nc,),
        out_shape=[_sds((L, D_SSD), BF16), _sds((L, LANE), BF16), _sds((L, D_XBC), F32), _sds((8, D), F32), _sds((8, LANE), F32)],
        in_specs=[rchunk(D, 0), rchunk(D, 0), rchunk(D_XBC, 0), rchunk(D, 0), rchunk(D, 0),
                  pl.BlockSpec((1, N_STATE, D_SSD), lambda i: (nc - 1 - i, 0, 0))] + [_const_spec((1, D))] * 4,
        out_specs=[rchunk(D, 0), rchunk(LANE, 0), rchunk(D_XBC, 0), _const_spec((8, D)), _const_spec((8, LANE))],
        scratch_shapes=[pltpu.VMEM((N_STATE, D_SSD), F32), pltpu.VMEM((Q, D_SSD), F32), pltpu.VMEM((LANE, Q), F32),
                        pltpu.VMEM((Q, D_SSD), F32), pltpu.VMEM((Q, D_SSD), F32)],
        compiler_params=_cparams(("arbitrary",)))(dysn, y, xbc, z_src, dt_src, s_prev_all, dtb_f, alog_f, dsk_f, snw)


def _adamw_math(w, g, m, v):
    m_n = ADAM_B1 * m + (1.0 - ADAM_B1) * g
    v_n = ADAM_B2 * v + (1.0 - ADAM_B2) * jnp.square(g)
    c1 = 1.0 - ADAM_B1 ** ADAM_STEP
    c2 = 1.0 - ADAM_B2 ** ADAM_STEP
    return -ADAM_LR * ((m_n / c1) / (jnp.sqrt(v_n / c2) + ADAM_EPS) + ADAM_WD * w), m_n, v_n


def _sum_slots(p_ref):
    acc = p_ref[0].astype(F32)
    for s in range(1, N_DEV):
        acc = acc + p_ref[s].astype(F32)
    return acc


def adamw_slots(w, slots, m, v, name):
    rows, cols = w.shape
    tc = 256

    def body(w_ref, s_ref, m_ref, v_ref, g_ref, d_ref, mo_ref, vo_ref):
        g_v = _sum_slots(s_ref)
        g_ref[...] = g_v
        d_ref[...], mo_ref[...], vo_ref[...] = _adamw_math(w_ref[...], g_v, m_ref[...], v_ref[...])

    spec = pl.BlockSpec((rows, tc), lambda i: (0, i))
    return pl.pallas_call(body, name=name, grid=(cols // tc,), out_shape=[_sds((rows, cols), F32)] * 4,
                          in_specs=[spec, pl.BlockSpec((N_DEV, rows, tc), lambda i: (0, 0, i)), spec, spec], out_specs=[spec] * 4,
                          compiler_params=_cparams(("parallel",)))(w, slots, m, v)


def adamw_many(ws, gs, ms, vs, name):
    n = len(ws)

    def body(*refs):
        for p in range(n):
            d_v, m_v, v_v = _adamw_math(refs[p][...], refs[n + p][...], refs[2 * n + p][...], refs[3 * n + p][...])
            refs[4 * n + p][...] = d_v
            refs[5 * n + p][...] = m_v
            refs[6 * n + p][...] = v_v

    vm = pl.BlockSpec(memory_space=pltpu.VMEM)
    out = pl.pallas_call(body, name=name, out_shape=[_sds(w.shape, F32) for w in ws] * 3, in_specs=[vm] * (4 * n),
                         out_specs=[vm] * (3 * n), compiler_params=_cparams())(*ws, *gs, *ms, *vs)
    return out[:n], out[n:2 * n], out[2 * n:]


def _pack_layout(shapes):
    row, layout = 0, []
    for rows, cols in shapes:
        chunks = []
        for c0 in range(0, cols, D):
            chunks.append((row, c0, min(D, cols - c0)))
            row += rows
        layout.append(chunks)
    return row, layout


def pack_rows(arrays, name):
    total, layout = _pack_layout([a.shape for a in arrays])
    n = len(arrays)

    def body(*refs):
        o_ref = refs[n]
        o_ref[...] = jnp.zeros((total, D), F32)
        for p in range(n):
            rows = arrays[p].shape[0]
            for r0, c0, w in layout[p]:
                o_ref[r0:r0 + rows, 0:w] = refs[p][:, c0:c0 + w]

    vm = pl.BlockSpec(memory_space=pltpu.VMEM)
    return pl.pallas_call(body, name=name, out_shape=_sds((total, D), F32), in_specs=[vm] * n, out_specs=vm,
                          compiler_params=_cparams())(*arrays)


def unpack_rows(packed, shapes):
    _, layout = _pack_layout(shapes)
    out = []
    for (rows, _), chunks in zip(shapes, layout):
        parts = [packed[..., r0:r0 + rows, 0:w] for r0, _, w in chunks]
        out.append(parts[0] if len(parts) == 1 else jnp.concatenate(parts, axis=-1))
    return out


def sum_slots_many(parts, name):
    n = len(parts)

    def body(*refs):
        for p in range(n):
            refs[n + p][...] = _sum_slots(refs[p])

    vm = pl.BlockSpec(memory_space=pltpu.VMEM)
    return pl.pallas_call(body, name=name, out_shape=[_sds(p.shape[1:], F32) for p in parts], in_specs=[vm] * n,
                          out_specs=[vm] * n, compiler_params=_cparams())(*parts)


def ada_mod(c_all, ada_w_shard, ada_b_cols, name):
    def body(c_ref, w_ref, b_ref, o_ref, ca_ref):
        ca = _silu(c_ref[...])
        ca_ref[...] = ca
        o_ref[...] = _dot(ca, w_ref[...]) + b_ref[...]

    vm = pl.BlockSpec(memory_space=pltpu.VMEM)
    return pl.pallas_call(body, name=name, out_shape=[_sds((N_DEV, ada_w_shard.shape[1]), F32), _sds((N_DEV, D), F32)],
                          in_specs=[vm, vm, vm], out_specs=[vm, vm], compiler_params=_cparams())(c_all, ada_w_shard, ada_b_cols)


def ada_wgrad(c_act_all, dmod_cols, name):
    def body(c_ref, d_ref, o_ref):
        o_ref[...] = _dot_tn_hi(c_ref[...], d_ref[...])

    vm = pl.BlockSpec(memory_space=pltpu.VMEM)
    return pl.pallas_call(body, name=name, out_shape=_sds((D, dmod_cols.shape[1]), F32), in_specs=[vm, vm], out_specs=vm,
                          compiler_params=_cparams())(c_act_all, dmod_cols)


def exchange(srcs, name, gather):
    n = len(srcs)
    shapes = [tuple(s.shape) if gather else tuple(s.shape[1:]) for s in srcs]

    def body(*refs):
        src_refs, out_refs = refs[:n], refs[n:2 * n]
        send_sems, recv_sems, local_sems = refs[2 * n:]
        x, y, c = lax.axis_index("x"), lax.axis_index("y"), lax.axis_index("c")
        me = 4 * x + 2 * y + c

        def peer(k):
            bx, by, bc = (k >> 2) & 1, (k >> 1) & 1, k & 1
            px, py, pc = (x + bx) % 2, (y + by) % 2, (c + bc) % 2
            return (px, py, pc), 4 * px + 2 * py + pc

        def copy(a, k, landing):
            dev, idx = peer(k)
            return pltpu.make_async_remote_copy(
                src_ref=src_refs[a] if gather else src_refs[a].at[idx], dst_ref=out_refs[a].at[idx if landing else me],
                send_sem=send_sems.at[a, k - 1], recv_sem=recv_sems.at[a, k - 1],
                device_id=dev, device_id_type=pl.DeviceIdType.MESH)

        mine = [pltpu.make_async_copy(src_refs[a] if gather else src_refs[a].at[me], out_refs[a].at[me], local_sems.at[a])
                for a in range(n)]
        for cp in mine:
            cp.start()
        sends = [copy(a, k, False) for a in range(n) for k in range(1, N_DEV)]
        for cp in sends:
            cp.start()
        for a in range(n):
            for k in range(1, N_DEV):
                copy(a, k, True).wait_recv()
        for cp in sends:
            cp.wait_send()
        for cp in mine:
            cp.wait()

    hbm = pl.BlockSpec(memory_space=pl.ANY)
    return pl.pallas_call(
        body, name=name, out_shape=[_sds((N_DEV,) + shp, s.dtype) for shp, s in zip(shapes, srcs)], in_specs=[hbm] * n,
        out_specs=[hbm] * n,
        scratch_shapes=[pltpu.SemaphoreType.DMA((n, N_DEV - 1)), pltpu.SemaphoreType.DMA((n, N_DEV - 1)),
                        pltpu.SemaphoreType.DMA((n,))],
        compiler_params=pltpu.CompilerParams(has_side_effects=True))(*srcs)


def gather_two_level(src, name):
    def body(src_ref, out_ref, send_sems, recv_sems, local_sem):
        x, y, c = lax.axis_index("x"), lax.axis_index("y"), lax.axis_index("c")
        me, sibling = (x, y, c), (x, y, 1 - c)
        chips = [(1 - x, y), (x, 1 - y), (1 - x, 1 - y)]

        def slot(px, py, pc):
            return out_ref.at[4 * px + 2 * py + pc]

        def copy(k, block, to, src=None):
            return pltpu.make_async_remote_copy(
                src_ref=slot(*block) if src is None else src, dst_ref=slot(*block), send_sem=send_sems.at[k],
                recv_sem=recv_sems.at[k], device_id=to, device_id_type=pl.DeviceIdType.MESH)

        mine = pltpu.make_async_copy(src_ref, slot(*me), local_sem)
        mine.start()
        first = [copy(0, me, sibling, src=src_ref)]
        first += [copy(1 + j, me, (*chip, c), src=src_ref) for j, chip in enumerate(chips)]
        for cp in first:
            cp.start()
        passed = [copy(4 + j, (*chip, c), sibling) for j, chip in enumerate(chips)]
        for j, chip in enumerate(chips):
            copy(1 + j, (*chip, c), me).wait_recv()
            passed[j].start()
        copy(0, sibling, me).wait_recv()
        for j, chip in enumerate(chips):
            copy(4 + j, (*chip, 1 - c), me).wait_recv()
        for cp in first + passed:
            cp.wait_send()
        mine.wait()

    hbm = pl.BlockSpec(memory_space=pl.ANY)
    return pl.pallas_call(
        body, name=name, out_shape=_sds((N_DEV,) + tuple(src.shape), src.dtype), in_specs=[hbm], out_specs=hbm,
        scratch_shapes=[pltpu.SemaphoreType.DMA((N_DEV - 1,)), pltpu.SemaphoreType.DMA((N_DEV - 1,)), pltpu.SemaphoreType.DMA],
        compiler_params=pltpu.CompilerParams(has_side_effects=True))(src)


def _peer(k):
    x, y, c = lax.axis_index("x"), lax.axis_index("y"), lax.axis_index("c")
    px, py, pc = (x + ((k >> 2) & 1)) % 2, (y + ((k >> 1) & 1)) % 2, (c + (k & 1)) % 2
    return (px, py, pc), 4 * px + 2 * py + pc


def _my_slot():
    return 4 * lax.axis_index("x") + 2 * lax.axis_index("y") + lax.axis_index("c")


_HBM = pl.BlockSpec(memory_space=pltpu.HBM)
_SEM = pl.BlockSpec(memory_space=pltpu.SEMAPHORE)
_EFFECT = pltpu.SideEffectType.DATAFLOW_SIDE_EFFECTING


def exchange_start(srcs, name, gather):
    n = len(srcs)
    shapes = [tuple(s.shape) if gather else tuple(s.shape[1:]) for s in srcs]
    lands = [lax.empty((N_DEV,) + shp, s.dtype) for shp, s in zip(shapes, srcs)]

    def body(*refs):
        src_refs, land_refs = refs[:n], refs[n:2 * n]
        sems = refs[2 * n:4 * n]
        token = refs[-1]
        me = _my_slot()
        for a in range(n):
            for k in range(1, N_DEV):
                dev, idx = _peer(k)
                pltpu.make_async_remote_copy(
                    src_ref=src_refs[a] if gather else src_refs[a].at[idx], dst_ref=land_refs[a].at[me],
                    send_sem=sems[2 * a].at[k - 1], recv_sem=sems[2 * a + 1].at[k - 1],
                    device_id=dev, device_id_type=pl.DeviceIdType.MESH).start()
        token[...] = jnp.zeros_like(token)

    out_shape = ([pltpu.SemaphoreType.DMA((N_DEV - 1,))] * (2 * n) + [pltpu.HBM(s.shape, s.dtype) for s in srcs]
                 + [pltpu.HBM(l.shape, l.dtype) for l in lands] + [_sds((8, LANE), F32)])
    out = pl.pallas_call(
        body, name=name, out_shape=out_shape, in_specs=[_HBM] * (2 * n),
        out_specs=[_SEM] * (2 * n) + [_HBM] * (2 * n) + [pl.BlockSpec(memory_space=pltpu.VMEM)],
        input_output_aliases={i: 2 * n + i for i in range(2 * n)},
        compiler_params=pltpu.CompilerParams(has_side_effects=_EFFECT))(
            *[pltpu.with_memory_space_constraint(s, pltpu.HBM) for s in srcs],
            *[pltpu.with_memory_space_constraint(l, pltpu.HBM) for l in lands])
    parts = [(out[2 * a], out[2 * a + 1], out[2 * n + a], out[3 * n + a]) for a in range(n)]
    return parts, out[-1]


def exchange_wait(parts, after, name, gather):
    n = len(parts)

    def body(*refs):
        src_refs, land_refs = refs[:n], refs[n:2 * n]
        sems = refs[2 * n:4 * n]
        for a in range(n):
            for k in range(1, N_DEV):
                dev, idx = _peer(k)
                copy = pltpu.make_async_remote_copy(
                    src_ref=src_refs[a] if gather else src_refs[a].at[idx], dst_ref=land_refs[a].at[idx],
                    send_sem=sems[2 * a].at[k - 1], recv_sem=sems[2 * a + 1].at[k - 1],
                    device_id=dev, device_id_type=pl.DeviceIdType.MESH)
                copy.wait_send()
                copy.wait_recv()

    srcs = [p[2] for p in parts]
    lands = [p[3] for p in parts]
    sems = [s for p in parts for s in p[:2]]
    out = pl.pallas_call(
        body, name=name, out_shape=[pltpu.HBM(a.shape, a.dtype) for a in srcs + lands],
        in_specs=[_HBM] * (2 * n) + [_SEM] * (2 * n) + [pl.BlockSpec(memory_space=pl.ANY)], out_specs=[_HBM] * (2 * n),
        input_output_aliases={i: i for i in range(2 * n)},
        compiler_params=pltpu.CompilerParams(has_side_effects=_EFFECT))(*srcs, *lands, *sems, after)
    return out[n:]


def _slabs_to_cols(s):
    _, r, cs = s.shape
    return s.transpose(1, 0, 2).reshape(r, N_DEV * cs)


def _rep_heads(v):
    return jnp.repeat(v.reshape(N_HEADS), HEAD).reshape(1, D_SSD)


def local_fwd_bwd(x, target, mod, get_w, put_grad, small):
    n1w, n2w, fnw = small["norm1_w"], small["norm2_w"], small["final_norm_w"]
    dtb_f, alog_f, dsk_f = _rep_heads(small["dt_bias"]), _rep_heads(small["a_log"]), _rep_heads(small["d_skip"])
    snw = small["ssd_norm_w"]

    def after(v, token):
        return v + token[0:1, 0:1]

    w_in = get_w("w_in", mod)
    h1, proj_zx, proj_dt, proj_cf = in_proj(x, mod, n1w, w_in["w_full"], w_in["w_dt_rep"], w_in["w_cf"], "norm1_in_proj")
    xbc = conv_silu_fwd(proj_zx, D_SSD // CB, D_XBC, small["ssd_conv_w"], small["ssd_conv_b"], "ssd_conv")
    y, ysn, s_prev = ssd_fwd(xbc, proj_zx, proj_dt, dtb_f, alog_f, dsk_f, snw, "ssd_scan")
    uc = conf_conv_fwd(proj_cf, 0, D_CONF // CB, small["conf_conv_w"], small["conf_conv_b"], "conf_conv")
    w_out = get_w("w_out", uc)
    mix, u, x1, h2 = mixer_out(ysn, uc, small["conf_ln_w"], small["conf_ln_b"], w_out, x, mod, n2w, "out_proj_norm2")
    w_up_t = get_w("w_up", h2)
    up = mm_nt([(h2, w_up_t, 0)], "up_proj")
    act = ffn_conv_fwd(up, small["ffn_conv_w"], small["ffn_conv_b"], "ffn_conv")
    w_down = get_w("w_down", act)
    dx2, dff, dact, acc_f = final_loss(act, w_down, x1, mod, fnw, target, "down_proj_loss")

    token = put_grad("w_down", mm_tn(act, dff, "wgrad_down"))
    dupg, dupv, dwg, dwv = ffn_conv_bwd(up, small["ffn_conv_w"], after(small["ffn_conv_b"], token), dact, "ffn_conv_bwd")
    token = put_grad("w_up", jnp.concatenate([mm_tn(dupg, h2, "wgrad_up_gate"), mm_tn(dupv, h2, "wgrad_up_val")], axis=0))
    dx1, dmix, acc_2 = norm_mod_bwd([(dupg, w_up_t, 0), (dupv, w_up_t, 1)], x1, dx2, mod, after(n2w, token), 3, "norm2_bwd",
                                    mix=mix, gate_row=2)

    token = put_grad("w_out", jnp.concatenate([mm_tn(ysn, dmix, "wgrad_out_ssd"), mm_tn(u, dmix, "wgrad_out_conf")], axis=0))
    dysn, duc, acc_ln = mixer_out_bwd(dmix, w_out, uc, after(small["conf_ln_w"], token), small["conf_ln_b"], "out_proj_bwd")
    dcfa, dcfg, dw_cc = conf_conv_bwd(proj_cf, 0, D_CONF // CB, small["conf_conv_w"], duc, "conf_conv_bwd")
    dz, ddt, dxbc_post, acc_s, acc_s16 = ssd_bwd(dysn, y, xbc, proj_zx, proj_dt, s_prev, dtb_f, alog_f, dsk_f, snw,
                                                 "ssd_scan_bwd")
    dxbc, dw_sc = conv_silu_bwd(proj_zx, D_SSD // CB, D_XBC, small["ssd_conv_w"], small["ssd_conv_b"], dxbc_post, "ssd_conv_bwd")
    token = put_grad("w_in", jnp.concatenate(
        [mm_tn(dz, h1, "wgrad_in_z"), mm_tn(dxbc, h1, "wgrad_in_xbc"), mm_tn(ddt, h1, "wgrad_in_dt")[:N_HEADS],
         mm_tn(dcfa, h1, "wgrad_in_cfa"), mm_tn(dcfg, h1, "wgrad_in_cfg")], axis=0))
    dh1_pairs = [(dz, w_in["w_full"], 0), (ddt, w_in["w_dt16"], 0), (dcfa, w_in["w_cf"], 0), (dcfg, w_in["w_cf"], 1),
                 (dxbc, w_in["w_xbc"], 0)]
    grad_x, acc_1 = norm_mod_bwd(dh1_pairs, x, dx1, mod, after(n1w, token), 0, "norm1_bwd")

    small_accs = dict(acc_1=acc_1, acc_2=acc_2, acc_f=acc_f, acc_ln=acc_ln, acc_s=acc_s, acc_s16=acc_s16, dw_sc=dw_sc,
                      dw_cc=dw_cc, dwg=dwg, dwv=dwv)
    return grad_x, small_accs


def kernel(x, c, ada_w, ada_b, norm1_w, w_in, ssd_conv_w, ssd_conv_b, dt_bias, a_log, d_skip, ssd_norm_w, conf_conv_w, conf_conv_b, conf_ln_w, conf_ln_b, w_out, norm2_w, w_up, ffn_conv_w, ffn_conv_b, w_down, final_norm_w, loss_target, m_ada_w, m_ada_b, m_norm1_w, m_w_in, m_ssd_conv_w, m_ssd_conv_b, m_dt_bias, m_a_log, m_d_skip, m_ssd_norm_w, m_conf_conv_w, m_conf_conv_b, m_conf_ln_w, m_conf_ln_b, m_w_out, m_norm2_w, m_w_up, m_ffn_conv_w, m_ffn_conv_b, m_w_down, m_final_norm_w, v_ada_w, v_ada_b, v_norm1_w, v_w_in, v_ssd_conv_w, v_ssd_conv_b, v_dt_bias, v_a_log, v_d_skip, v_ssd_norm_w, v_conf_conv_w, v_conf_conv_b, v_conf_ln_w, v_conf_ln_b, v_w_out, v_norm2_w, v_w_up, v_ffn_conv_w, v_ffn_conv_b, v_w_down, v_final_norm_w):
    me = 4 * lax.axis_index("x") + 2 * lax.axis_index("y") + lax.axis_index("c")
    weights = dict(ada_w=ada_w, ada_b=ada_b, norm1_w=norm1_w, w_in=w_in, ssd_conv_w=ssd_conv_w, ssd_conv_b=ssd_conv_b,
                   dt_bias=dt_bias, a_log=a_log, d_skip=d_skip, ssd_norm_w=ssd_norm_w, conf_conv_w=conf_conv_w,
                   conf_conv_b=conf_conv_b, conf_ln_w=conf_ln_w, conf_ln_b=conf_ln_b, w_out=w_out, norm2_w=norm2_w, w_up=w_up,
                   ffn_conv_w=ffn_conv_w, ffn_conv_b=ffn_conv_b, w_down=w_down, final_norm_w=final_norm_w)
    moms_m = dict(ada_w=m_ada_w, ada_b=m_ada_b, norm1_w=m_norm1_w, w_in=m_w_in, ssd_conv_w=m_ssd_conv_w, ssd_conv_b=m_ssd_conv_b,
                  dt_bias=m_dt_bias, a_log=m_a_log, d_skip=m_d_skip, ssd_norm_w=m_ssd_norm_w, conf_conv_w=m_conf_conv_w,
                  conf_conv_b=m_conf_conv_b, conf_ln_w=m_conf_ln_w, conf_ln_b=m_conf_ln_b, w_out=m_w_out, norm2_w=m_norm2_w,
                  w_up=m_w_up, ffn_conv_w=m_ffn_conv_w, ffn_conv_b=m_ffn_conv_b, w_down=m_w_down, final_norm_w=m_final_norm_w)
    moms_v = dict(ada_w=v_ada_w, ada_b=v_ada_b, norm1_w=v_norm1_w, w_in=v_w_in, ssd_conv_w=v_ssd_conv_w, ssd_conv_b=v_ssd_conv_b,
                  dt_bias=v_dt_bias, a_log=v_a_log, d_skip=v_d_skip, ssd_norm_w=v_ssd_norm_w, conf_conv_w=v_conf_conv_w,
                  conf_conv_b=v_conf_conv_b, conf_ln_w=v_conf_ln_w, conf_ln_b=v_conf_ln_b, w_out=v_w_out, norm2_w=v_norm2_w,
                  w_up=v_w_up, ffn_conv_w=v_ffn_conv_w, ffn_conv_b=v_ffn_conv_b, w_down=v_w_down, final_norm_w=v_final_norm_w)
    names = list(weights)

    def to2d(a):
        return a[0] if a.ndim == 3 else a.reshape(1, -1)

    big = ("w_in", "w_out", "w_up", "w_down")

    c_all, scw_all, ccw_all, fcw_all = exchange([c.reshape(8, LANE), ssd_conv_w[0], conf_conv_w[0], ffn_conv_w[0]],
                                                "gather_small", gather=True)
    c_all = c_all.reshape(N_DEV, D)

    ada_cols = ada_w.shape[2]
    ada_b_cols = lax.dynamic_slice(ada_b, (0, me * ada_cols), (1, ada_cols))
    mod_cols, c_act_all = ada_mod(c_all, ada_w[0], ada_b_cols, "ada_mod")
    mod_parts, = exchange([jnp.pad(mod_cols, ((0, 0), (0, D - ada_cols))).reshape(N_DEV, 8, LANE)], "scatter_mod", gather=False)
    mod = mod_parts.reshape(N_DEV, D)[:, :ada_cols].reshape(6, D)
    mod = jnp.pad(mod, ((0, 2), (0, 0)))

    def rows_of(a):
        return jnp.swapaxes(a, 1, 2)[0] if a.shape[2] != D else a[0]

    shards, mod = lax.optimization_barrier(([rows_of(weights[n]).astype(BF16) for n in big], mod))
    w_in_slabs = gather_two_level(shards[0], "gather_w_in")
    later, w_in_slabs = lax.optimization_barrier((shards[1:], w_in_slabs))
    gather_parts, token = exchange_start(later, "gather_weights_start", gather=True)
    mod = mod + token[0:1, 0:1]

    small = {n: to2d(weights[n]) for n in names if n not in ("ada_w",) + big}
    small["ssd_conv_w"] = _slabs_to_cols(scw_all)
    small["conf_conv_w"] = _slabs_to_cols(ccw_all)
    small["ffn_conv_w"] = _slabs_to_cols(fcw_all)

    def with_own(landed, own):
        return lax.dynamic_update_slice(landed, own[None], (me,) + (0,) * own.ndim)

    def get_w(n, after):
        if n == "w_in":
            slabs = w_in_slabs
        else:
            a = big.index(n)
            landed, = exchange_wait([gather_parts[a - 1]], after, "gather_" + n + "_wait", gather=True)
            slabs = with_own(landed, shards[a])
        full = slabs.reshape(N_DEV * slabs.shape[1], D)
        if n != "w_in":
            return full
        w_dt = full[D_SSD + D_XBC:D_SSD + D_XBC + N_HEADS]
        return dict(w_full=full, w_xbc=full[D_SSD:D_SSD + D_XBC], w_cf=full[D_SSD + D_XBC + N_HEADS:],
                    w_dt_rep=jnp.repeat(w_dt, HEAD, axis=0), w_dt16=jnp.pad(w_dt, ((0, LANE - N_HEADS), (0, 0))))

    scatter_parts, sent = {}, {}

    def put_grad(n, g):
        sent[n] = g.reshape(N_DEV, g.shape[0] // N_DEV, g.shape[1]).astype(BF16)
        (scatter_parts[n],), token = exchange_start([sent[n]], "scatter_" + n + "_start", gather=False)
        return token

    grad_x, accs = local_fwd_bwd(x[0], loss_target[0], mod, get_w, put_grad, small)
    loss = lax.psum(0.5 / D * jnp.sum(accs["acc_f"][2:3]), ("x", "y", "c"))

    grads, delta, new_m, new_v = {}, {}, {}, {}

    def finish(ns, after, name):
        landed = exchange_wait([scatter_parts[n] for n in ns], after, name, gather=False)
        for n, slots in zip(ns, landed):
            slots = with_own(slots, lax.dynamic_index_in_dim(sent[n], me, 0, keepdims=False))
            out = adamw_slots(rows_of(weights[n]), slots, rows_of(moms_m[n]), rows_of(moms_v[n]), "adamw_" + n)
            if weights[n].shape[2] != D:
                out = [jnp.swapaxes(o, 0, 1) for o in out]
            grads[n], delta[n], new_m[n], new_v[n] = out

    finish(big[1:], grad_x, "scatter_grads_wait")

    order = ("acc_1", "acc_2", "acc_f", "acc_ln", "acc_s", "acc_s16", "dw_sc", "dw_cc", "dwg", "dwv")
    shapes = [accs[k].shape for k in order]
    acc_list, _ = lax.optimization_barrier(([accs[k] for k in order], [new_v[n] for n in big[1:]]))
    packed_all, = exchange([pack_rows(acc_list, "pack_small_grads")], "gather_small_grads", gather=True)
    packed_red, = sum_slots_many([packed_all], "sum_small_grads")
    gathered = dict(zip(order, unpack_rows(packed_all, shapes)))
    red = dict(zip(order, unpack_rows(packed_red, shapes)))

    def mod_rows(a1, a2, af):
        return jnp.concatenate([a1[..., 0:2, :], a2[..., 3:4, :], a2[..., 0:2, :], af[..., 1:2, :]], axis=-2)

    dmod_all = mod_rows(gathered["acc_1"], gathered["acc_2"], gathered["acc_f"]).reshape(N_DEV, 6 * D)
    grads["ada_w"] = ada_wgrad(c_act_all, lax.dynamic_slice(dmod_all, (0, me * ada_cols), (N_DEV, ada_cols)), "ada_wgrad")

    def my_cols(full, k_taps):
        cols = full.shape[1] // N_DEV
        return lax.dynamic_slice(full, (0, me * cols), (k_taps, cols))

    fcw = jnp.concatenate([red["dwg"], red["dwv"]], axis=1)
    grads.update(
        ada_b=mod_rows(red["acc_1"], red["acc_2"], red["acc_f"]).reshape(1, 6 * D), norm1_w=red["acc_1"][2:3],
        ssd_conv_w=my_cols(red["dw_sc"], K_SSD), ssd_conv_b=red["dw_sc"][K_SSD:K_SSD + 1],
        dt_bias=red["acc_s16"][1:2, :N_HEADS], a_log=red["acc_s16"][2:3, :N_HEADS], d_skip=red["acc_s16"][3:4, :N_HEADS],
        ssd_norm_w=red["acc_s"][0:1], conf_conv_w=my_cols(red["dw_cc"], K_CONF), conf_conv_b=red["dw_cc"][K_CONF:K_CONF + 1],
        conf_ln_w=red["acc_ln"][0:1], conf_ln_b=red["acc_ln"][1:2], norm2_w=red["acc_2"][2:3],
        ffn_conv_w=my_cols(fcw, K_FFN), ffn_conv_b=fcw[K_FFN:K_FFN + 1], final_norm_w=red["acc_f"][0:1])

    rest = [n for n in names if n not in big]
    d_l, m_l, v_l = adamw_many([to2d(weights[n]) for n in rest], [grads[n] for n in rest], [to2d(moms_m[n]) for n in rest],
                               [to2d(moms_v[n]) for n in rest], "adamw_small")
    for n, dd, mm, vv in zip(rest, d_l, m_l, v_l):
        delta[n], new_m[n], new_v[n] = dd, mm, vv
    finish(big[:1], d_l[0], "scatter_w_in_wait")
    shape_of = lambda d_: {n: d_[n].reshape(weights[n].shape) for n in names}
    grads, delta, new_m, new_v = shape_of(grads), shape_of(delta), shape_of(new_m), shape_of(new_v)
    return (loss, grad_x[None], *[grads[n] for n in names], *[delta[n] for n in names], *[new_m[n] for n in names],
            *[new_v[n] for n in names])
```

```python
import functools

import jax
import jax.numpy as jnp
from jax import lax
from jax.experimental import pallas as pl
from jax.experimental.pallas import tpu as pltpu

F32 = jnp.float32
BF16 = jnp.bfloat16
HI = lax.Precision.HIGHEST

N_DEV = 8
D = 1024
D_SSD = 1024
HEAD = 64
N_HEADS = 16
N_STATE = 128
D_XBC = 1536
D_CONF = 1024
D_FF = 2816
K_SSD, K_CONF, K_FFN = 4, 31, 3
D_INP = 5632
LANE = 128
TR = 256
TM = 512
Q = 256
CB = 256
TC = 1024
VMEM_LIMIT = 56 * 1024 * 1024

ADAM_LR, ADAM_B1, ADAM_B2, ADAM_EPS, ADAM_WD, ADAM_STEP = 0.001, 0.9, 0.999, 1e-08, 0.01, 10


def _cparams(sem=None):
    return pltpu.CompilerParams(vmem_limit_bytes=VMEM_LIMIT, dimension_semantics=sem)


def _sds(shape, dtype):
    return jax.ShapeDtypeStruct(shape, dtype)


def _sigmoid(x):
    return 1.0 / (1.0 + jnp.exp(-x))


def _silu(x):
    return x * _sigmoid(x)


def _dsilu(x):
    s = _sigmoid(x)
    return s * (1.0 + x * (1.0 - s))


def _softplus(x):
    return jnp.maximum(x, 0.0) + jnp.log(1.0 + jnp.exp(-jnp.abs(x)))


def _dot(a, b):
    return jnp.dot(a.astype(BF16), b.astype(BF16), preferred_element_type=F32)


def _dot_nt(a, b):
    return lax.dot_general(a.astype(BF16), b.astype(BF16), (((1,), (1,)), ((), ())), preferred_element_type=F32)


def _dot_tn(a, b):
    return lax.dot_general(a.astype(BF16), b.astype(BF16), (((0,), (0,)), ((), ())), preferred_element_type=F32)


def _bf16_terms(a, terms):
    parts, rem = [], a
    for t in range(terms):
        p = rem.astype(BF16)
        parts.append(p)
        if t + 1 < terms:
            rem = rem - p.astype(F32)
    return parts


def _dot_exact(a, b, terms, exact, dims=(((1,), (0,)), ((), ()))):
    if exact == "a":
        a_b = a.astype(BF16)
        outs = [lax.dot_general(a_b, p, dims, preferred_element_type=F32) for p in _bf16_terms(b, terms)]
    else:
        b_b = b.astype(BF16)
        outs = [lax.dot_general(p, b_b, dims, preferred_element_type=F32) for p in _bf16_terms(a, terms)]
    acc = outs[-1]
    for o in reversed(outs[:-1]):
        acc = acc + o
    return acc


def _dot_tn_hi(a, b):
    return lax.dot_general(a, b, (((0,), (0,)), ((), ())), precision=HI, preferred_element_type=F32)


def _colsum(x):
    return jnp.sum(x, axis=0, keepdims=True)


def _const_spec(shape):
    return pl.BlockSpec(shape, lambda *_: (0,) * len(shape))


def _col_tile(n):
    for t in (1408, 1024, 768, 512, 256, 128):
        if n % t == 0 and t <= n:
            return t
    return n


def mm_nt(pairs, name):
    L = pairs[0][0].shape[0]
    K = pairs[0][1].shape[0]
    tk = _col_tile(K)
    n = len(pairs)

    def body(*refs):
        o_ref = refs[-1]
        acc = None
        for p in range(n):
            t = lax.dot_general(refs[2 * p][...], refs[2 * p + 1][...], (((1,), (1,)), ((), ())),
                                preferred_element_type=F32)
            acc = t if acc is None else acc + t
        o_ref[...] = acc

    in_specs, args = [], []
    for a, w, cb in pairs:
        in_specs += [pl.BlockSpec((TM, a.shape[1]), lambda j, i: (i, 0)),
                     pl.BlockSpec((tk, a.shape[1]), functools.partial(lambda j, i, cb: (j, cb), cb=cb))]
        args += [a, w]
    return pl.pallas_call(
        body, name=name, grid=(K // tk, L // TM), out_shape=_sds((L, K), F32), in_specs=in_specs,
        out_specs=pl.BlockSpec((TM, tk), lambda j, i: (i, j)),
        compiler_params=_cparams(("parallel", "parallel")))(*args)


def mm_tn(a, g, name):
    L, M = a.shape
    N = g.shape[1]
    tn = _col_tile(N) if N > 1024 else N
    if M * tn * 4 > 12 * 1024 * 1024:
        tn = 512
    tl = 512 if L % 512 == 0 else TR
    nl = L // tl

    def body(a_ref, g_ref, o_ref, acc_ref):
        @pl.when(pl.program_id(1) == 0)
        def _():
            acc_ref[...] = jnp.zeros((M, tn), F32)

        acc_ref[...] += lax.dot_general(a_ref[...], g_ref[...], (((0,), (0,)), ((), ())), preferred_element_type=F32)

        @pl.when(pl.program_id(1) == nl - 1)
        def _():
            o_ref[...] = acc_ref[...].astype(BF16)

    return pl.pallas_call(
        body, name=name, grid=(N // tn, nl), out_shape=_sds((M, N), BF16),
        in_specs=[pl.BlockSpec((tl, M), lambda j, l: (l, 0)), pl.BlockSpec((tl, tn), lambda j, l: (l, j))],
        out_specs=pl.BlockSpec((M, tn), lambda j, l: (0, j)), scratch_shapes=[pltpu.VMEM((M, tn), F32)],
        compiler_params=_cparams(("parallel", "arbitrary")))(a, g)


def _row_spec(width=D):
    return pl.BlockSpec((TR, width), lambda i: (i, 0))


def in_proj(x, mod, n1w, w_full, w_dt_rep, w_cf, name):
    L = x.shape[0]
    n_zx = D_SSD + D_XBC

    def body(x_ref, mod_ref, w_ref, wzx_ref, wdt_ref, wcf_ref, h_ref, zx_ref, dt_ref, cf_ref):
        xin = x_ref[...]
        r = lax.rsqrt(jnp.mean(xin * xin, axis=-1, keepdims=True) + 1e-6)
        h = ((xin * r * w_ref[...]) * (1.0 + mod_ref[1:2, :]) + mod_ref[0:1, :]).astype(BF16)
        h_ref[...] = h
        nt = (((1,), (1,)), ((), ()))
        zx_ref[...] = lax.dot_general(h, wzx_ref[...], nt, preferred_element_type=F32)
        dt_ref[...] = lax.dot_general(h, wdt_ref[...], nt, preferred_element_type=F32)
        cf_ref[...] = lax.dot_general(h, wcf_ref[...], nt, preferred_element_type=F32)

    row = lambda w: pl.BlockSpec((TM, w), lambda i: (i, 0))
    return pl.pallas_call(
        body, name=name, grid=(L // TM,),
        out_shape=[_sds((L, D), BF16), _sds((L, n_zx), F32), _sds((L, D_SSD), F32), _sds((L, 2 * D_CONF), F32)],
        in_specs=[row(D), _const_spec((8, D)), _const_spec((1, D)), _const_spec((n_zx, D)), _const_spec((D_SSD, D)),
                  _const_spec((2 * D_CONF, D))],
        out_specs=[row(D), row(n_zx), row(D_SSD), row(2 * D_CONF)],
        compiler_params=_cparams(("parallel",)))(x, mod, n1w, w_full, w_dt_rep, w_cf)


def mixer_out(ysn, uc, lnw, lnb, w_out, x, mod, n2w, name):
    L = x.shape[0]

    def body(ysn_ref, uc_ref, lnw_ref, lnb_ref, wo_ref, x_ref, mod_ref, n2w_ref, mix_ref, u_ref, x1_ref, h2_ref):
        uc_v = uc_ref[...]
        mu = jnp.mean(uc_v, axis=-1, keepdims=True)
        var = jnp.mean(jnp.square(uc_v - mu), axis=-1, keepdims=True)
        u = _silu((uc_v - mu) * lax.rsqrt(var + 1e-5) * lnw_ref[...] + lnb_ref[...]).astype(BF16)
        u_ref[...] = u
        mix = (jnp.dot(ysn_ref[...], wo_ref[0:D_SSD, :], preferred_element_type=F32)
               + jnp.dot(u, wo_ref[D_SSD:D_SSD + D_CONF, :], preferred_element_type=F32))
        mix_ref[...] = mix
        x1 = x_ref[...] + mod_ref[2:3, :] * mix
        x1_ref[...] = x1
        r = lax.rsqrt(jnp.mean(x1 * x1, axis=-1, keepdims=True) + 1e-6)
        h2_ref[...] = ((x1 * r * n2w_ref[...]) * (1.0 + mod_ref[4:5, :]) + mod_ref[3:4, :]).astype(BF16)

    return pl.pallas_call(
        body, name=name, grid=(L // TR,),
        out_shape=[_sds((L, D), F32), _sds((L, D_CONF), BF16), _sds((L, D), F32), _sds((L, D), BF16)],
        in_specs=[_row_spec(), _row_spec(), _const_spec((1, D)), _const_spec((1, D)), _const_spec((D_SSD + D_CONF, D)), _row_spec(),
                  _const_spec((8, D)), _const_spec((1, D))],
        out_specs=[_row_spec()] * 4, compiler_params=_cparams(("parallel",)))(ysn, uc, lnw, lnb, w_out, x, mod, n2w)


def mixer_out_bwd(dmix, w_out, uc, lnw, lnb, name):
    L = uc.shape[0]

    def body(dm_ref, wo_ref, u_ref, w_ref, b_ref, dy_ref, o_ref, acc_ref):
        @pl.when(pl.program_id(0) == 0)
        def _():
            acc_ref[...] = jnp.zeros((8, D), F32)

        nt = (((1,), (1,)), ((), ()))
        dm = dm_ref[...]
        dy_ref[...] = lax.dot_general(dm, wo_ref[0:D_SSD, :], nt, preferred_element_type=F32)
        du = lax.dot_general(dm, wo_ref[D_SSD:D_SSD + D_CONF, :], nt, preferred_element_type=F32)
        u = u_ref[...]
        mu = jnp.mean(u, axis=-1, keepdims=True)
        rl = lax.rsqrt(jnp.mean(jnp.square(u - mu), axis=-1, keepdims=True) + 1e-5)
        n = (u - mu) * rl
        v = n * w_ref[...] + b_ref[...]
        dv = du * _dsilu(v)
        acc_ref[0:1, :] += _colsum(dv * n)
        acc_ref[1:2, :] += _colsum(dv)
        dn = dv * w_ref[...]
        o_ref[...] = rl * (dn - jnp.mean(dn, axis=-1, keepdims=True) - n * jnp.mean(dn * n, axis=-1, keepdims=True))

    return pl.pallas_call(body, name=name, grid=(L // TR,), out_shape=[_sds((L, D), F32), _sds((L, D), F32), _sds((8, D), F32)],
                          in_specs=[_row_spec(), _const_spec((D_SSD + D_CONF, D)), _row_spec(), _const_spec((1, D)),
                                    _const_spec((1, D))],
                          out_specs=[_row_spec(), _row_spec(), _const_spec((8, D))],
                          compiler_params=_cparams(("arbitrary",)))(dmix, w_out, uc, lnw, lnb)


def final_loss(act, w_down, x1, mod, fw, target, name):
    L = act.shape[0]

    def body(act_ref, wd_ref, x1_ref, mod_ref, fw_ref, t_ref, dx_ref, dff_ref, dact_ref, acc_ref):
        @pl.when(pl.program_id(0) == 0)
        def _():
            acc_ref[...] = jnp.zeros((8, D), F32)

        ff_v = jnp.dot(act_ref[...], wd_ref[...], preferred_element_type=F32)
        g2 = mod_ref[5:6, :]
        x2 = x1_ref[...] + g2 * ff_v
        r = lax.rsqrt(jnp.mean(x2 * x2, axis=-1, keepdims=True) + 1e-6)
        n = x2 * r
        err = n * fw_ref[...] - t_ref[...]
        dy = err * (1.0 / D)
        dn = dy * fw_ref[...]
        dx2 = r * (dn - n * jnp.mean(dn * n, axis=-1, keepdims=True))
        acc_ref[0:1, :] += _colsum(dy * n)
        acc_ref[1:2, :] += _colsum(dx2 * ff_v)
        acc_ref[2:3, :] += _colsum(err * err)
        dx_ref[...] = dx2
        dff = (dx2 * g2).astype(BF16)
        dff_ref[...] = dff
        dact_ref[...] = lax.dot_general(dff, wd_ref[...], (((1,), (1,)), ((), ())), preferred_element_type=F32)

    return pl.pallas_call(
        body, name=name, grid=(L // TR,),
        out_shape=[_sds((L, D), F32), _sds((L, D), BF16), _sds((L, D_FF), F32), _sds((8, D), F32)],
        in_specs=[_row_spec(D_FF), _const_spec((D_FF, D)), _row_spec(), _const_spec((8, D)), _const_spec((1, D)), _row_spec()],
        out_specs=[_row_spec(), _row_spec(), _row_spec(D_FF), _const_spec((8, D))],
        compiler_params=_cparams(("arbitrary",)))(act, w_down, x1, mod, fw, target)


def norm_mod_bwd(dh_pairs, xin, dres, mod, w, shift_row, name, mix=None, gate_row=None):
    L = xin.shape[0]
    has_mix = mix is not None
    n_pairs = len(dh_pairs)

    def body(*refs):
        pair_refs, refs = refs[:2 * n_pairs], refs[2 * n_pairs:]
        if has_mix:
            x_ref, dres_ref, mod_ref, w_ref, mix_ref, dx_ref, dmix_ref, acc_ref = refs
        else:
            x_ref, dres_ref, mod_ref, w_ref, dx_ref, acc_ref = refs

        @pl.when(pl.program_id(0) == 0)
        def _():
            acc_ref[...] = jnp.zeros((8, D), F32)

        dh_v = None
        for p in range(n_pairs):
            t = jnp.dot(pair_refs[2 * p][...], pair_refs[2 * p + 1][...], preferred_element_type=F32)
            dh_v = t if dh_v is None else dh_v + t
        x = x_ref[...]
        r = lax.rsqrt(jnp.mean(x * x, axis=-1, keepdims=True) + 1e-6)
        n = x * r
        nw = n * w_ref[...]
        sc1 = 1.0 + mod_ref[shift_row + 1:shift_row + 2, :]
        acc_ref[0:1, :] += _colsum(dh_v)
        acc_ref[1:2, :] += _colsum(dh_v * nw)
        dnw = dh_v * sc1
        acc_ref[2:3, :] += _colsum(dnw * n)
        dn = dnw * w_ref[...]
        dx = r * (dn - n * jnp.mean(dn * n, axis=-1, keepdims=True)) + dres_ref[...]
        dx_ref[...] = dx
        if has_mix:
            acc_ref[3:4, :] += _colsum(dx * mix_ref[...])
            dmix_ref[...] = (dx * mod_ref[gate_row:gate_row + 1, :]).astype(BF16)

    ins, in_specs = [], []
    for a, wt, rb in dh_pairs:
        ins += [a, wt]
        in_specs += [_row_spec(a.shape[1]), pl.BlockSpec((a.shape[1], D), functools.partial(lambda i, rb: (rb, 0), rb=rb))]
    ins += [xin, dres, mod, w] + ([mix] if has_mix else [])
    in_specs += [_row_spec(), _row_spec(), _const_spec((8, D)), _const_spec((1, D))] + ([_row_spec()] if has_mix else [])
    out_shape = [_sds((L, D), F32)] + ([_sds((L, D), BF16)] if has_mix else []) + [_sds((8, D), F32)]
    out_specs = [_row_spec()] + ([_row_spec()] if has_mix else []) + [_const_spec((8, D))]
    return pl.pallas_call(body, name=name, grid=(L // TR,), out_shape=out_shape, in_specs=in_specs,
                          out_specs=out_specs, compiler_params=_cparams(("arbitrary",)))(*ins)


def _halo(k):
    return 8 if k <= 9 else 32


def _prev_spec(h, col0):
    return pl.BlockSpec((h, CB), lambda j, i: (jnp.maximum(i * (TC // h) - 1, 0), j + col0))


def _next_spec(h, col0, n_tiles):
    return pl.BlockSpec((h, CB), lambda j, i: (jnp.minimum(i + 1, n_tiles - 1) * (TC // h), j + col0))


def _tile_spec(col0):
    return pl.BlockSpec((TC, CB), lambda j, i: (i, j + col0))


def _w_spec(kp, col0):
    return pl.BlockSpec((kp, CB), lambda j, i: (0, j + col0))


SUBLANES = 8


def _shifted_windows(v, taps, rows):
    for r in range(SUBLANES):
        group = [(o, k) for o, k in taps if o % SUBLANES == r]
        if not group:
            continue
        s = v if r == 0 else pltpu.roll(v, v.shape[0] - r, 0)
        for o, k in group:
            yield k, s[o - r:o - r + rows, :]


def _causal_taps(ext_ref, w_ref, k_taps, first, rows):
    acc = None
    for k, win in _shifted_windows(ext_ref[...], [(first - (k_taps - 1) + k, k) for k in range(k_taps)], rows):
        t = w_ref[k:k + 1, :] * win
        acc = t if acc is None else acc + t
    return acc


def _anticausal_taps(d_ref, w_ref, k_taps, rows):
    acc = None
    for k, win in _shifted_windows(d_ref[...], [(k_taps - 1 - k, k) for k in range(k_taps)], rows):
        t = w_ref[k:k + 1, :] * win
        acc = t if acc is None else acc + t
    return acc


def _acc_conv_wgrad(dw_ref, d_tile, ext_ref, k_taps, first):
    for k, win in _shifted_windows(ext_ref[...], [(first - (k_taps - 1) + k, k) for k in range(k_taps)], TC):
        dw_ref[k:k + 1, :] += _colsum(d_tile * win)
    dw_ref[k_taps:k_taps + 1, :] += _colsum(d_tile)


def conv_silu_fwd(x, col0, width, w, b, name):
    L = x.shape[0]
    k_taps = w.shape[0]
    h = _halo(k_taps)

    def body(xp_ref, x_ref, w_ref, b_ref, o_ref, ext_ref):
        i = pl.program_id(1)
        ext_ref[0:h, :] = jnp.where(i > 0, xp_ref[...], 0.0)
        ext_ref[h:h + TC, :] = x_ref[...]
        o_ref[...] = _silu(_causal_taps(ext_ref, w_ref, k_taps, h, TC) + b_ref[...])

    return pl.pallas_call(
        body, name=name, grid=(width // CB, L // TC), out_shape=_sds((L, width), F32),
        in_specs=[_prev_spec(h, col0), _tile_spec(col0), _w_spec(k_taps, 0), pl.BlockSpec((1, CB), lambda j, i: (0, j))],
        out_specs=_tile_spec(0), scratch_shapes=[pltpu.VMEM((h + TC, CB), F32)],
        compiler_params=_cparams(("parallel", "parallel")))(x, x, w, b)


def conv_silu_bwd(x, col0, width, w, b, dpost, name):
    L = x.shape[0]
    k_taps = w.shape[0]
    h = _halo(k_taps)
    nt = L // TC

    def body(xp_ref, x_ref, xn_ref, d_ref, dn_ref, w_ref, b_ref, dx_ref, dw_ref, ext_ref, dpre_ref):
        i = pl.program_id(1)

        @pl.when(i == 0)
        def _():
            dw_ref[...] = jnp.zeros((8, CB), F32)

        ext_ref[0:h, :] = jnp.where(i > 0, xp_ref[...], 0.0)
        ext_ref[h:h + TC, :] = x_ref[...]
        ext_ref[h + TC:h + TC + h, :] = xn_ref[...]
        pre = _causal_taps(ext_ref, w_ref, k_taps, h, TC + h) + b_ref[...]
        dpre_ref[0:TC, :] = d_ref[...] * _dsilu(pre[0:TC, :])
        dpre_ref[TC:TC + h, :] = jnp.where(i < nt - 1, dn_ref[...], 0.0) * _dsilu(pre[TC:TC + h, :])
        dx_ref[...] = _anticausal_taps(dpre_ref, w_ref, k_taps, TC).astype(BF16)
        _acc_conv_wgrad(dw_ref, dpre_ref[0:TC, :], ext_ref, k_taps, h)

    return pl.pallas_call(
        body, name=name, grid=(width // CB, nt),
        out_shape=[_sds((L, width), BF16), _sds((8, width), F32)],
        in_specs=[_prev_spec(h, col0), _tile_spec(col0), _next_spec(h, col0, nt), _tile_spec(0), _next_spec(h, 0, nt),
                  _w_spec(k_taps, 0), pl.BlockSpec((1, CB), lambda j, i: (0, j))],
        out_specs=[_tile_spec(0), _w_spec(8, 0)],
        scratch_shapes=[pltpu.VMEM((h + TC + h, CB), F32), pltpu.VMEM((TC + h, CB), F32)],
        compiler_params=_cparams(("parallel", "arbitrary")))(x, x, x, dpost, dpost, w, b)


def conf_conv_fwd(proj, col_a, col_g, w, b, name):
    L = proj.shape[0]
    k_taps = w.shape[0]
    h = _halo(k_taps)

    def body(ap_ref, a_ref, gp_ref, g_ref, w_ref, b_ref, o_ref, ext_ref):
        i = pl.program_id(1)
        ext_ref[0:h, :] = jnp.where(i > 0, ap_ref[...] * _sigmoid(gp_ref[...]), 0.0)
        ext_ref[h:h + TC, :] = a_ref[...] * _sigmoid(g_ref[...])
        o_ref[...] = _causal_taps(ext_ref, w_ref, k_taps, h, TC) + b_ref[...]

    return pl.pallas_call(
        body, name=name, grid=(D_CONF // CB, L // TC), out_shape=_sds((L, D_CONF), F32),
        in_specs=[_prev_spec(h, col_a), _tile_spec(col_a), _prev_spec(h, col_g), _tile_spec(col_g), _w_spec(k_taps, 0),
                  pl.BlockSpec((1, CB), lambda j, i: (0, j))],
        out_specs=_tile_spec(0), scratch_shapes=[pltpu.VMEM((h + TC, CB), F32)],
        compiler_params=_cparams(("parallel", "parallel")))(proj, proj, proj, proj, w, b)


def conf_conv_bwd(proj, col_a, col_g, w, duc, name):
    L = proj.shape[0]
    k_taps = w.shape[0]
    h = _halo(k_taps)
    nt = L // TC

    def body(ap_ref, a_ref, gp_ref, g_ref, d_ref, dn_ref, w_ref, da_ref, dg_ref, dw_ref, ext_ref, dext_ref):
        i = pl.program_id(1)

        @pl.when(i == 0)
        def _():
            dw_ref[...] = jnp.zeros((32, CB), F32)

        a = a_ref[...]
        s = _sigmoid(g_ref[...])
        ext_ref[0:h, :] = jnp.where(i > 0, ap_ref[...] * _sigmoid(gp_ref[...]), 0.0)
        ext_ref[h:h + TC, :] = a * s
        dext_ref[0:TC, :] = d_ref[...]
        dext_ref[TC:TC + h, :] = jnp.where(i < nt - 1, dn_ref[...], 0.0)
        du0 = _anticausal_taps(dext_ref, w_ref, k_taps, TC)
        da_ref[...] = (du0 * s).astype(BF16)
        dg_ref[...] = (du0 * a * s * (1.0 - s)).astype(BF16)
        _acc_conv_wgrad(dw_ref, d_ref[...], ext_ref, k_taps, h)

    return pl.pallas_call(
        body, name=name, grid=(D_CONF // CB, nt),
        out_shape=[_sds((L, D_CONF), BF16), _sds((L, D_CONF), BF16), _sds((32, D_CONF), F32)],
        in_specs=[_prev_spec(h, col_a), _tile_spec(col_a), _prev_spec(h, col_g), _tile_spec(col_g), _tile_spec(0),
                  _next_spec(h, 0, nt), _w_spec(k_taps, 0)],
        out_specs=[_tile_spec(0), _tile_spec(0), _w_spec(32, 0)],
        scratch_shapes=[pltpu.VMEM((h + TC, CB), F32), pltpu.VMEM((TC + h, CB), F32)],
        compiler_params=_cparams(("parallel", "arbitrary")))(proj, proj, proj, proj, duc, duc, w)


def ffn_conv_fwd(up, w, b, name):
    L = up.shape[0]
    k_taps = w.shape[0]
    h = _halo(k_taps)
    cv = D_FF // CB

    def body(gp_ref, g_ref, vp_ref, v_ref, wg_ref, wv_ref, bg_ref, bv_ref, o_ref, eg_ref, ev_ref):
        i = pl.program_id(1)
        eg_ref[0:h, :] = jnp.where(i > 0, gp_ref[...], 0.0)
        eg_ref[h:h + TC, :] = g_ref[...]
        ev_ref[0:h, :] = jnp.where(i > 0, vp_ref[...], 0.0)
        ev_ref[h:h + TC, :] = v_ref[...]
        pg = _causal_taps(eg_ref, wg_ref, k_taps, h, TC) + bg_ref[...]
        pv = _causal_taps(ev_ref, wv_ref, k_taps, h, TC) + bv_ref[...]
        o_ref[...] = (_silu(pg) * pv).astype(BF16)

    bspec = lambda c0: pl.BlockSpec((1, CB), lambda j, i: (0, j + c0))
    return pl.pallas_call(
        body, name=name, grid=(cv, L // TC), out_shape=_sds((L, D_FF), BF16),
        in_specs=[_prev_spec(h, 0), _tile_spec(0), _prev_spec(h, cv), _tile_spec(cv), _w_spec(k_taps, 0), _w_spec(k_taps, cv),
                  bspec(0), bspec(cv)],
        out_specs=_tile_spec(0), scratch_shapes=[pltpu.VMEM((h + TC, CB), F32), pltpu.VMEM((h + TC, CB), F32)],
        compiler_params=_cparams(("parallel", "parallel")))(up, up, up, up, w, w, b, b)


def ffn_conv_bwd(up, w, b, dact, name):
    L = up.shape[0]
    k_taps = w.shape[0]
    h = _halo(k_taps)
    nt = L // TC
    cv = D_FF // CB

    def body(gp_ref, g_ref, gn_ref, vp_ref, v_ref, vn_ref, d_ref, dn_ref, wg_ref, wv_ref, bg_ref, bv_ref,
             dg_ref, dv_ref, dwg_ref, dwv_ref, eg_ref, ev_ref, pg_ref, pv_ref):
        i = pl.program_id(1)

        @pl.when(i == 0)
        def _():
            dwg_ref[...] = jnp.zeros((8, CB), F32)
            dwv_ref[...] = jnp.zeros((8, CB), F32)

        for e_ref, p_ref, c_ref, n_ref in ((eg_ref, gp_ref, g_ref, gn_ref), (ev_ref, vp_ref, v_ref, vn_ref)):
            e_ref[0:h, :] = jnp.where(i > 0, p_ref[...], 0.0)
            e_ref[h:h + TC, :] = c_ref[...]
            e_ref[h + TC:h + TC + h, :] = n_ref[...]
        pg = _causal_taps(eg_ref, wg_ref, k_taps, h, TC + h) + bg_ref[...]
        pv = _causal_taps(ev_ref, wv_ref, k_taps, h, TC + h) + bv_ref[...]
        dact_t = d_ref[...]
        dact_n = jnp.where(i < nt - 1, dn_ref[...], 0.0)
        pg_ref[0:TC, :] = dact_t * pv[0:TC, :] * _dsilu(pg[0:TC, :])
        pg_ref[TC:TC + h, :] = dact_n * pv[TC:TC + h, :] * _dsilu(pg[TC:TC + h, :])
        pv_ref[0:TC, :] = dact_t * _silu(pg[0:TC, :])
        pv_ref[TC:TC + h, :] = dact_n * _silu(pg[TC:TC + h, :])
        dg_ref[...] = _anticausal_taps(pg_ref, wg_ref, k_taps, TC).astype(BF16)
        dv_ref[...] = _anticausal_taps(pv_ref, wv_ref, k_taps, TC).astype(BF16)
        _acc_conv_wgrad(dwg_ref, pg_ref[0:TC, :], eg_ref, k_taps, h)
        _acc_conv_wgrad(dwv_ref, pv_ref[0:TC, :], ev_ref, k_taps, h)

    bspec = lambda c0: pl.BlockSpec((1, CB), lambda j, i: (0, j + c0))
    ext = pltpu.VMEM((h + TC + h, CB), F32)
    dpre = pltpu.VMEM((TC + h, CB), F32)
    return pl.pallas_call(
        body, name=name, grid=(cv, nt),
        out_shape=[_sds((L, D_FF), BF16), _sds((L, D_FF), BF16), _sds((8, D_FF), F32), _sds((8, D_FF), F32)],
        in_specs=[_prev_spec(h, 0), _tile_spec(0), _next_spec(h, 0, nt), _prev_spec(h, cv), _tile_spec(cv), _next_spec(h, cv, nt),
                  _tile_spec(0), _next_spec(h, 0, nt), _w_spec(k_taps, 0), _w_spec(k_taps, cv), bspec(0), bspec(cv)],
        out_specs=[_tile_spec(0), _tile_spec(0), _w_spec(8, 0), _w_spec(8, 0)],
        scratch_shapes=[ext, ext, dpre, dpre],
        compiler_params=_cparams(("parallel", "arbitrary")))(up, up, up, up, up, up, dact, dact, w, w, b, b)


def _ssd_common(xbc_ref, dt_ref, dtb_ref, alog_ref, cs_ref):
    xs = xbc_ref[:, 0:D_SSD]
    sp_in = dt_ref[...] + dtb_ref[...]
    dtf = _softplus(sp_in)
    a_f = -jnp.exp(alog_ref[...])
    a_dt = dtf * a_f
    row = lax.broadcasted_iota(jnp.int32, (Q, Q), 0)
    col = lax.broadcasted_iota(jnp.int32, (Q, Q), 1)
    causal = row >= col
    cs = _dot_exact(causal.astype(F32), a_dt, 3, "a")
    cs_ref[...] = cs
    cs_last = cs_ref[Q - 1:Q, :]
    return xs, sp_in, dtf, a_f, cs, cs_last, causal


def _head_decay(cs_j, cst_ref, e, causal):
    lane = lax.broadcasted_iota(jnp.int32, (Q, LANE), 1)
    rolled = pltpu.roll(cs_j, HEAD, 1)
    own = (lane < HEAD) if e == 0 else (lane >= HEAD)
    col_b = jnp.where(own, cs_j, rolled)
    col_b = jnp.concatenate([col_b] * (Q // LANE), axis=1)
    row_b = cst_ref[e * HEAD:e * HEAD + 1, :]
    return jnp.where(causal, jnp.exp(jnp.minimum(col_b - row_b, 0.0)), 0.0)


def ssd_fwd(xbc, z_src, dt_src, dtb_f, alog_f, dsk_f, snw, name):
    L = xbc.shape[0]
    nc = L // Q

    def body(xbc_ref, z_ref, dt_ref, dtb_ref, alog_ref, dsk_ref, snw_ref, y_ref, yn_ref, sp_ref, s_ref, cs_ref, cst_ref, yd_ref):
        @pl.when(pl.program_id(0) == 0)
        def _():
            s_ref[...] = jnp.zeros((N_STATE, D_SSD), F32)

        xs, _, dtf, a_f, cs, cs_last, causal = _ssd_common(xbc_ref, dt_ref, dtb_ref, alog_ref, cs_ref)
        e_cs = jnp.exp(cs)
        xdt = xs * dtf
        zst = jnp.exp(cs_last - cs) * xdt
        sp_ref[0] = s_ref[...]
        lane = lax.broadcasted_iota(jnp.int32, (Q, LANE), 1)
        for g in range(2):
            gl = slice(g * 512, g * 512 + 512)
            b_g = xbc_ref[:, D_SSD + g * N_STATE:D_SSD + (g + 1) * N_STATE]
            c_g = xbc_ref[:, D_SSD + 2 * N_STATE + g * N_STATE:D_SSD + 2 * N_STATE + (g + 1) * N_STATE]
            s_prev = s_ref[:, gl]
            cb = _dot_nt(c_g, b_g)
            yd_ref[:, gl] = e_cs[:, gl] * _dot(c_g, s_prev)
            for j in range(4):
                tl = slice(g * 512 + j * LANE, g * 512 + (j + 1) * LANE)
                cs_j = cs[:, tl]
                cst_ref[...] = cs_j.T
                x_j = xdt[:, tl]
                o0 = _dot(cb * _head_decay(cs_j, cst_ref, 0, causal), x_j)
                o1 = _dot(cb * _head_decay(cs_j, cst_ref, 1, causal), x_j)
                yd_ref[:, tl] += jnp.where(lane < HEAD, o0, o1)
            s_ref[:, gl] = jnp.exp(cs_last[:, gl]) * s_prev + _dot_tn(b_g, zst[:, gl])
        y = yd_ref[...] + xs * dsk_ref[...]
        y_ref[...] = y
        yz = y * _silu(z_ref[...])
        r = lax.rsqrt(jnp.mean(yz * yz, axis=-1, keepdims=True) + 1e-6)
        yn_ref[...] = (yz * r * snw_ref[...]).astype(BF16)

    chunk = lambda w, c: pl.BlockSpec((Q, w), lambda i: (i, c))
    return pl.pallas_call(
        body, name=name, grid=(nc,),
        out_shape=[_sds((L, D_SSD), F32), _sds((L, D_SSD), BF16), _sds((nc, N_STATE, D_SSD), F32)],
        in_specs=[chunk(D_XBC, 0), chunk(D, 0), chunk(D, 0)] + [_const_spec((1, D))] * 4,
        out_specs=[chunk(D, 0), chunk(D, 0), pl.BlockSpec((1, N_STATE, D_SSD), lambda i: (i, 0, 0))],
        scratch_shapes=[pltpu.VMEM((N_STATE, D_SSD), F32), pltpu.VMEM((Q, D_SSD), F32), pltpu.VMEM((LANE, Q), F32),
                        pltpu.VMEM((Q, D_SSD), F32)],
        compiler_params=_cparams(("arbitrary",)))(xbc, z_src, dt_src, dtb_f, alog_f, dsk_f, snw)


def ssd_bwd(dysn, y, xbc, z_src, dt_src, s_prev_all, dtb_f, alog_f, dsk_f, snw, name):
    L = xbc.shape[0]
    nc = L // Q

    def body(dyn_ref, y_ref, xbc_ref, z_ref, dt_ref, sp_ref, dtb_ref, alog_ref, dsk_ref, snw_ref,
             dz_ref, ddt_ref, dxbc_ref, acc_ref, acc16_ref, ds_ref, cs_ref, cst_ref, dcs_ref, dx_ref):
        step = pl.program_id(0)

        @pl.when(step == 0)
        def _():
            ds_ref[...] = jnp.zeros((N_STATE, D_SSD), F32)
            acc_ref[...] = jnp.zeros((8, D), F32)

        z = z_ref[...]
        y = y_ref[...]
        sz = _sigmoid(z)
        siluz = z * sz
        yz = y * siluz
        r = lax.rsqrt(jnp.mean(yz * yz, axis=-1, keepdims=True) + 1e-6)
        n = yz * r
        dyn = dyn_ref[...]
        acc_ref[0:1, :] += _colsum(dyn * n)
        dn = dyn * snw_ref[...]
        dyz = r * (dn - n * jnp.mean(dn * n, axis=-1, keepdims=True))
        dy = dyz * siluz
        dz_ref[...] = (dyz * y * (sz * (1.0 + z * (1.0 - sz)))).astype(BF16)

        xs, sp_in, dtf, a_f, cs, cs_last, causal = _ssd_common(xbc_ref, dt_ref, dtb_ref, alog_ref, cs_ref)
        acc_ref[3:4, :] += _colsum(dy * xs)
        e_cs = jnp.exp(cs)
        xdt = xs * dtf
        dst = jnp.exp(cs_last - cs)
        zst = dst * xdt
        e_last = jnp.exp(cs_last)
        lane = lax.broadcasted_iota(jnp.int32, (Q, LANE), 1)
        ones = jnp.ones((Q, LANE), F32)
        dcs_last_parts = []
        for g in range(2):
            gl = slice(g * 512, g * 512 + 512)
            b_g = xbc_ref[:, D_SSD + g * N_STATE:D_SSD + (g + 1) * N_STATE]
            c_g = xbc_ref[:, D_SSD + 2 * N_STATE + g * N_STATE:D_SSD + 2 * N_STATE + (g + 1) * N_STATE]
            s_prev = sp_ref[0, :, gl]
            ds_g = ds_ref[:, gl]
            dy_g = dy[:, gl]
            cb = _dot_nt(c_g, b_g)
            y_off = e_cs[:, gl] * _dot(c_g, s_prev)
            edy = e_cs[:, gl] * dy_g
            d_c = _dot_nt(edy, s_prev)
            d_z = _dot(b_g, ds_g)
            d_b = _dot_nt(zst[:, gl], ds_g)
            t_g = d_z * zst[:, gl]
            dcs_ref[:, gl] = dy_g * y_off - t_g
            dx_ref[:, gl] = d_z * dst[:, gl]
            dcs_last_parts.append(_colsum(t_g) + _colsum(ds_g * s_prev) * e_last[:, gl])
            ds_ref[:, gl] = e_last[:, gl] * ds_g + _dot_tn(c_g, edy)
            dcb = jnp.zeros((Q, Q), F32)
            for j in range(4):
                tl = slice(g * 512 + j * LANE, g * 512 + (j + 1) * LANE)
                cs_j = cs[:, tl]
                cst_ref[...] = cs_j.T
                x_j = xdt[:, tl]
                dy_j = dy[:, tl]
                dx_j = jnp.zeros((Q, LANE), F32)
                dcs_j = jnp.zeros((Q, LANE), F32)
                for e in range(2):
                    own = (lane < HEAD) if e == 0 else (lane >= HEAD)
                    w_h = _head_decay(cs_j, cst_ref, e, causal)
                    g_h = cb * w_h
                    dy_m = jnp.where(own, dy_j, 0.0)
                    d_g = _dot_nt(dy_m, x_j)
                    dx_j = dx_j + _dot_tn(g_h, dy_m)
                    dcb = dcb + d_g * w_h
                    p_h = d_g * g_h
                    row_sums = _dot_exact(p_h, ones, 2, "b")
                    col_sums = _dot_exact(p_h, ones, 2, "b", (((0,), (0,)), ((), ())))
                    dcs_j = dcs_j + jnp.where(own, row_sums - col_sums, 0.0)
                dcs_ref[:, tl] += dcs_j * (1.0 / HEAD)
                dx_ref[:, tl] += dx_j
            d_c = d_c + _dot(dcb, b_g)
            d_b = d_b + _dot_tn(dcb, c_g)
            dxbc_ref[:, D_SSD + g * N_STATE:D_SSD + (g + 1) * N_STATE] = d_b
            dxbc_ref[:, D_SSD + 2 * N_STATE + g * N_STATE:D_SSD + 2 * N_STATE + (g + 1) * N_STATE] = d_c
        dcs_last = jnp.concatenate(dcs_last_parts, axis=1)
        anticausal = lax.broadcasted_iota(jnp.int32, (Q, Q), 0) <= lax.broadcasted_iota(jnp.int32, (Q, Q), 1)
        d_adt = _dot_exact(anticausal.astype(F32), dcs_ref[...], 3, "a") + dcs_last
        dx = dx_ref[...]
        acc_ref[2:3, :] += _colsum(d_adt * dtf) * a_f
        d_dtf = d_adt * a_f + dx * xs
        dxbc_ref[:, 0:D_SSD] = dx * dtf + dy * dsk_ref[...]
        d_raw = d_dtf * _sigmoid(sp_in)
        acc_ref[1:2, :] += _colsum(d_raw)
        head_of_lane = lax.broadcasted_iota(jnp.int32, (D_SSD, LANE), 0) // HEAD
        fold = (head_of_lane == lax.broadcasted_iota(jnp.int32, (D_SSD, LANE), 1)).astype(F32)
        ddt_ref[...] = _dot_exact(d_raw, fold, 2, "b").astype(BF16)

        @pl.when(step == nc - 1)
        def _():
            acc16_ref[...] = _dot_exact(acc_ref[...], fold, 3, "b")

    rchunk = lambda w, c: pl.BlockSpec((Q, w), lambda i: (nc - 1 - i, c))
    return pl.pallas_call(
        body, name=name, grid=(nc,),
        out_shape=[_sds((L, D_SSD), BF16), _sds((L, LANE), BF16), _sds((L, D_XBC), F32), _sds((8, D), F32), _sds((8, LANE), F32)],
        in_specs=[rchunk(D, 0), rchunk(D, 0), rchunk(D_XBC, 0), rchunk(D, 0), rchunk(D, 0),
                  pl.BlockSpec((1, N_STATE, D_SSD), lambda i: (nc - 1 - i, 0, 0))] + [_const_spec((1, D))] * 4,
        out_specs=[rchunk(D, 0), rchunk(LANE, 0), rchunk(D_XBC, 0), _const_spec((8, D)), _const_spec((8, LANE))],
        scratch_shapes=[pltpu.VMEM((N_STATE, D_SSD), F32), pltpu.VMEM((Q, D_SSD), F32), pltpu.VMEM((LANE, Q), F32),
                        pltpu.VMEM((Q, D_SSD), F32), pltpu.VMEM((Q, D_SSD), F32)],
        compiler_params=_cparams(("arbitrary",)))(dysn, y, xbc, z_src, dt_src, s_prev_all, dtb_f, alog_f, dsk_f, snw)


def _adamw_math(w, g, m, v):
    m_n = ADAM_B1 * m + (1.0 - ADAM_B1) * g
    v_n = ADAM_B2 * v + (1.0 - ADAM_B2) * jnp.square(g)
    c1 = 1.0 - ADAM_B1 ** ADAM_STEP
    c2 = 1.0 - ADAM_B2 ** ADAM_STEP
    return -ADAM_LR * ((m_n / c1) / (jnp.sqrt(v_n / c2) + ADAM_EPS) + ADAM_WD * w), m_n, v_n


def _sum_slots(p_ref):
    acc = p_ref[0].astype(F32)
    for s in range(1, p_ref.shape[0]):
        acc = acc + p_ref[s].astype(F32)
    return acc


def adamw_slots(w, slots, m, v, name):
    rows, cols = w.shape
    tc = 256

    def body(w_ref, s_ref, m_ref, v_ref, g_ref, d_ref, mo_ref, vo_ref):
        g_v = _sum_slots(s_ref)
        g_ref[...] = g_v
        d_ref[...], mo_ref[...], vo_ref[...] = _adamw_math(w_ref[...], g_v, m_ref[...], v_ref[...])

    spec = pl.BlockSpec((rows, tc), lambda i: (0, i))
    return pl.pallas_call(body, name=name, grid=(cols // tc,), out_shape=[_sds((rows, cols), F32)] * 4,
                          in_specs=[spec, pl.BlockSpec((slots.shape[0], rows, tc), lambda i: (0, 0, i)), spec, spec], out_specs=[spec] * 4,
                          compiler_params=_cparams(("parallel",)))(w, slots, m, v)


def adamw_many(ws, gs, ms, vs, name):
    n = len(ws)

    def body(*refs):
        for p in range(n):
            d_v, m_v, v_v = _adamw_math(refs[p][...], refs[n + p][...], refs[2 * n + p][...], refs[3 * n + p][...])
            refs[4 * n + p][...] = d_v
            refs[5 * n + p][...] = m_v
            refs[6 * n + p][...] = v_v

    vm = pl.BlockSpec(memory_space=pltpu.VMEM)
    out = pl.pallas_call(body, name=name, out_shape=[_sds(w.shape, F32) for w in ws] * 3, in_specs=[vm] * (4 * n),
                         out_specs=[vm] * (3 * n), compiler_params=_cparams())(*ws, *gs, *ms, *vs)
    return out[:n], out[n:2 * n], out[2 * n:]


def _pack_layout(shapes):
    row, layout = 0, []
    for rows, cols in shapes:
        chunks = []
        for c0 in range(0, cols, D):
            chunks.append((row, c0, min(D, cols - c0)))
            row += rows
        layout.append(chunks)
    return row, layout


def pack_rows(arrays, name):
    total, layout = _pack_layout([a.shape for a in arrays])
    n = len(arrays)

    def body(*refs):
        o_ref = refs[n]
        o_ref[...] = jnp.zeros((total, D), F32)
        for p in range(n):
            rows = arrays[p].shape[0]
            for r0, c0, w in layout[p]:
                o_ref[r0:r0 + rows, 0:w] = refs[p][:, c0:c0 + w]

    vm = pl.BlockSpec(memory_space=pltpu.VMEM)
    return pl.pallas_call(body, name=name, out_shape=_sds((total, D), F32), in_specs=[vm] * n, out_specs=vm,
                          compiler_params=_cparams())(*arrays)


def unpack_rows(packed, shapes):
    _, layout = _pack_layout(shapes)
    out = []
    for (rows, _), chunks in zip(shapes, layout):
        parts = [packed[..., r0:r0 + rows, 0:w] for r0, _, w in chunks]
        out.append(parts[0] if len(parts) == 1 else jnp.concatenate(parts, axis=-1))
    return out


def sum_slots_many(parts, name):
    n = len(parts)

    def body(*refs):
        for p in range(n):
            refs[n + p][...] = _sum_slots(refs[p])

    vm = pl.BlockSpec(memory_space=pltpu.VMEM)
    return pl.pallas_call(body, name=name, out_shape=[_sds(p.shape[1:], F32) for p in parts], in_specs=[vm] * n,
                          out_specs=[vm] * n, compiler_params=_cparams())(*parts)


def ada_mod(c_all, ada_w_shard, ada_b_cols, name):
    def body(c_ref, w_ref, b_ref, o_ref, ca_ref):
        ca = _silu(c_ref[...])
        ca_ref[...] = ca
        o_ref[...] = _dot(ca, w_ref[...]) + b_ref[...]

    vm = pl.BlockSpec(memory_space=pltpu.VMEM)
    return pl.pallas_call(body, name=name, out_shape=[_sds((N_DEV, ada_w_shard.shape[1]), F32), _sds((N_DEV, D), F32)],
                          in_specs=[vm, vm, vm], out_specs=[vm, vm], compiler_params=_cparams())(c_all, ada_w_shard, ada_b_cols)


def ada_wgrad(c_act_all, dmod_cols, name):
    def body(c_ref, d_ref, o_ref):
        o_ref[...] = _dot_tn_hi(c_ref[...], d_ref[...])

    vm = pl.BlockSpec(memory_space=pltpu.VMEM)
    return pl.pallas_call(body, name=name, out_shape=_sds((D, dmod_cols.shape[1]), F32), in_specs=[vm, vm], out_specs=vm,
                          compiler_params=_cparams())(c_act_all, dmod_cols)


def exchange(srcs, name, gather):
    n = len(srcs)
    shapes = [tuple(s.shape) if gather else tuple(s.shape[1:]) for s in srcs]

    def body(*refs):
        src_refs, out_refs = refs[:n], refs[n:2 * n]
        send_sems, recv_sems, local_sems = refs[2 * n:]
        x, y, c = lax.axis_index("x"), lax.axis_index("y"), lax.axis_index("c")
        me = 4 * x + 2 * y + c

        def peer(k):
            bx, by, bc = (k >> 2) & 1, (k >> 1) & 1, k & 1
            px, py, pc = (x + bx) % 2, (y + by) % 2, (c + bc) % 2
            return (px, py, pc), 4 * px + 2 * py + pc

        def copy(a, k, landing):
            dev, idx = peer(k)
            return pltpu.make_async_remote_copy(
                src_ref=src_refs[a] if gather else src_refs[a].at[idx], dst_ref=out_refs[a].at[idx if landing else me],
                send_sem=send_sems.at[a, k - 1], recv_sem=recv_sems.at[a, k - 1],
                device_id=dev, device_id_type=pl.DeviceIdType.MESH)

        mine = [pltpu.make_async_copy(src_refs[a] if gather else src_refs[a].at[me], out_refs[a].at[me], local_sems.at[a])
                for a in range(n)]
        for cp in mine:
            cp.start()
        sends = [copy(a, k, False) for a in range(n) for k in range(1, N_DEV)]
        for cp in sends:
            cp.start()
        for a in range(n):
            for k in range(1, N_DEV):
                copy(a, k, True).wait_recv()
        for cp in sends:
            cp.wait_send()
        for cp in mine:
            cp.wait()

    hbm = pl.BlockSpec(memory_space=pl.ANY)
    return pl.pallas_call(
        body, name=name, out_shape=[_sds((N_DEV,) + shp, s.dtype) for shp, s in zip(shapes, srcs)], in_specs=[hbm] * n,
        out_specs=[hbm] * n,
        scratch_shapes=[pltpu.SemaphoreType.DMA((n, N_DEV - 1)), pltpu.SemaphoreType.DMA((n, N_DEV - 1)),
                        pltpu.SemaphoreType.DMA((n,))],
        compiler_params=pltpu.CompilerParams(has_side_effects=True))(*srcs)


def gather_two_level(src, name):
    def body(src_ref, out_ref, send_sems, recv_sems, local_sem):
        x, y, c = lax.axis_index("x"), lax.axis_index("y"), lax.axis_index("c")
        me, sibling = (x, y, c), (x, y, 1 - c)
        chips = [(1 - x, y), (x, 1 - y), (1 - x, 1 - y)]

        def slot(px, py, pc):
            return out_ref.at[4 * px + 2 * py + pc]

        def copy(k, block, to, src=None):
            return pltpu.make_async_remote_copy(
                src_ref=slot(*block) if src is None else src, dst_ref=slot(*block), send_sem=send_sems.at[k],
                recv_sem=recv_sems.at[k], device_id=to, device_id_type=pl.DeviceIdType.MESH)

        mine = pltpu.make_async_copy(src_ref, slot(*me), local_sem)
        mine.start()
        first = [copy(0, me, sibling, src=src_ref)]
        first += [copy(1 + j, me, (*chip, c), src=src_ref) for j, chip in enumerate(chips)]
        for cp in first:
            cp.start()
        passed = [copy(4 + j, (*chip, c), sibling) for j, chip in enumerate(chips)]
        for j, chip in enumerate(chips):
            copy(1 + j, (*chip, c), me).wait_recv()
            passed[j].start()
        copy(0, sibling, me).wait_recv()
        for j, chip in enumerate(chips):
            copy(4 + j, (*chip, 1 - c), me).wait_recv()
        for cp in first + passed:
            cp.wait_send()
        mine.wait()

    hbm = pl.BlockSpec(memory_space=pl.ANY)
    return pl.pallas_call(
        body, name=name, out_shape=_sds((N_DEV,) + tuple(src.shape), src.dtype), in_specs=[hbm], out_specs=hbm,
        scratch_shapes=[pltpu.SemaphoreType.DMA((N_DEV - 1,)), pltpu.SemaphoreType.DMA((N_DEV - 1,)), pltpu.SemaphoreType.DMA],
        compiler_params=pltpu.CompilerParams(has_side_effects=True))(src)


N_CHIP = 4


def sibling_swap(slabs, name):
    shape = tuple(slabs.shape[1:])

    def body(src_ref, out_ref, send_sems, recv_sems):
        x, y, c = lax.axis_index("x"), lax.axis_index("y"), lax.axis_index("c")
        copies = [pltpu.make_async_remote_copy(
            src_ref=src_ref.at[2 * j + (1 - c)], dst_ref=out_ref.at[j], send_sem=send_sems.at[j], recv_sem=recv_sems.at[j],
            device_id=(x, y, 1 - c), device_id_type=pl.DeviceIdType.MESH) for j in range(N_CHIP)]
        for cp in copies:
            cp.start()
        for cp in copies:
            cp.wait_recv()
        for cp in copies:
            cp.wait_send()

    hbm = pl.BlockSpec(memory_space=pl.ANY)
    return pl.pallas_call(
        body, name=name, out_shape=_sds((N_CHIP,) + shape, slabs.dtype), in_specs=[hbm], out_specs=hbm,
        scratch_shapes=[pltpu.SemaphoreType.DMA((N_CHIP,)), pltpu.SemaphoreType.DMA((N_CHIP,))],
        compiler_params=pltpu.CompilerParams(has_side_effects=True))(slabs)


def pair_sum(slabs, from_sibling, name):
    _, rows, cols = slabs.shape
    tc = 256

    def body(mine_ref, sib_ref, o_ref):
        c = lax.axis_index("c")
        mine = jnp.where(c == 0, mine_ref[0, 0], mine_ref[0, 1])
        o_ref[0] = (mine.astype(F32) + sib_ref[0].astype(F32)).astype(BF16)

    return pl.pallas_call(
        body, name=name, grid=(N_CHIP, cols // tc), out_shape=_sds((N_CHIP, rows, cols), BF16),
        in_specs=[pl.BlockSpec((1, 2, rows, tc), lambda j, i: (j, 0, 0, i)), pl.BlockSpec((1, rows, tc), lambda j, i: (j, 0, i))],
        out_specs=pl.BlockSpec((1, rows, tc), lambda j, i: (j, 0, i)),
        compiler_params=_cparams(("parallel", "parallel")))(slabs.reshape(N_CHIP, 2, rows, cols), from_sibling)


def _peer(k):
    x, y, c = lax.axis_index("x"), lax.axis_index("y"), lax.axis_index("c")
    px, py, pc = (x + ((k >> 2) & 1)) % 2, (y + ((k >> 1) & 1)) % 2, (c + (k & 1)) % 2
    return (px, py, pc), 4 * px + 2 * py + pc


def _my_slot():
    return 4 * lax.axis_index("x") + 2 * lax.axis_index("y") + lax.axis_index("c")


_HBM = pl.BlockSpec(memory_space=pltpu.HBM)
_SEM = pl.BlockSpec(memory_space=pltpu.SEMAPHORE)
_EFFECT = pltpu.SideEffectType.DATAFLOW_SIDE_EFFECTING


def _peer_set(same_core):
    return ((2, 4, 6), 1) if same_core else (tuple(range(1, N_DEV)), 0)


def exchange_start(srcs, name, gather, same_core=False):
    n = len(srcs)
    ks, shift = _peer_set(same_core)
    shapes = [tuple(s.shape) if gather else tuple(s.shape[1:]) for s in srcs]
    lands = [lax.empty((N_DEV >> shift,) + shp, s.dtype) for shp, s in zip(shapes, srcs)]

    def body(*refs):
        src_refs, land_refs = refs[:n], refs[n:2 * n]
        sems = refs[2 * n:4 * n]
        token = refs[-1]
        me = _my_slot()
        for a in range(n):
            for i, k in enumerate(ks):
                dev, idx = _peer(k)
                pltpu.make_async_remote_copy(
                    src_ref=src_refs[a] if gather else src_refs[a].at[idx >> shift], dst_ref=land_refs[a].at[me >> shift],
                    send_sem=sems[2 * a].at[i], recv_sem=sems[2 * a + 1].at[i],
                    device_id=dev, device_id_type=pl.DeviceIdType.MESH).start()
        token[...] = jnp.zeros_like(token)

    out_shape = ([pltpu.SemaphoreType.DMA((len(ks),))] * (2 * n) + [pltpu.HBM(s.shape, s.dtype) for s in srcs]
                 + [pltpu.HBM(l.shape, l.dtype) for l in lands] + [_sds((8, LANE), F32)])
    out = pl.pallas_call(
        body, name=name, out_shape=out_shape, in_specs=[_HBM] * (2 * n),
        out_specs=[_SEM] * (2 * n) + [_HBM] * (2 * n) + [pl.BlockSpec(memory_space=pltpu.VMEM)],
        input_output_aliases={i: 2 * n + i for i in range(2 * n)},
        compiler_params=pltpu.CompilerParams(has_side_effects=_EFFECT))(
            *[pltpu.with_memory_space_constraint(s, pltpu.HBM) for s in srcs],
            *[pltpu.with_memory_space_constraint(l, pltpu.HBM) for l in lands])
    parts = [(out[2 * a], out[2 * a + 1], out[2 * n + a], out[3 * n + a]) for a in range(n)]
    return parts, out[-1]


def exchange_wait(parts, after, name, gather, same_core=False):
    n = len(parts)
    ks, shift = _peer_set(same_core)

    def body(*refs):
        src_refs, land_refs = refs[:n], refs[n:2 * n]
        sems = refs[2 * n:4 * n]
        for a in range(n):
            for i, k in enumerate(ks):
                dev, idx = _peer(k)
                copy = pltpu.make_async_remote_copy(
                    src_ref=src_refs[a] if gather else src_refs[a].at[idx >> shift], dst_ref=land_refs[a].at[idx >> shift],
                    send_sem=sems[2 * a].at[i], recv_sem=sems[2 * a + 1].at[i],
                    device_id=dev, device_id_type=pl.DeviceIdType.MESH)
                copy.wait_send()
                copy.wait_recv()

    srcs = [p[2] for p in parts]
    lands = [p[3] for p in parts]
    sems = [s for p in parts for s in p[:2]]
    out = pl.pallas_call(
        body, name=name, out_shape=[pltpu.HBM(a.shape, a.dtype) for a in srcs + lands],
        in_specs=[_HBM] * (2 * n) + [_SEM] * (2 * n) + [pl.BlockSpec(memory_space=pl.ANY)], out_specs=[_HBM] * (2 * n),
        input_output_aliases={i: i for i in range(2 * n)},
        compiler_params=pltpu.CompilerParams(has_side_effects=_EFFECT))(*srcs, *lands, *sems, after)
    return out[n:]


def _slabs_to_cols(s):
    _, r, cs = s.shape
    return s.transpose(1, 0, 2).reshape(r, N_DEV * cs)


def _rep_heads(v):
    return jnp.repeat(v.reshape(N_HEADS), HEAD).reshape(1, D_SSD)


def local_fwd_bwd(x, target, mod, get_w, put_grad, small):
    n1w, n2w, fnw = small["norm1_w"], small["norm2_w"], small["final_norm_w"]
    dtb_f, alog_f, dsk_f = _rep_heads(small["dt_bias"]), _rep_heads(small["a_log"]), _rep_heads(small["d_skip"])
    snw = small["ssd_norm_w"]

    def after(v, token):
        return v + token[0:1, 0:1]

    w_in = get_w("w_in", mod)
    h1, proj_zx, proj_dt, proj_cf = in_proj(x, mod, n1w, w_in["w_full"], w_in["w_dt_rep"], w_in["w_cf"], "norm1_in_proj")
    xbc = conv_silu_fwd(proj_zx, D_SSD // CB, D_XBC, small["ssd_conv_w"], small["ssd_conv_b"], "ssd_conv")
    y, ysn, s_prev = ssd_fwd(xbc, proj_zx, proj_dt, dtb_f, alog_f, dsk_f, snw, "ssd_scan")
    uc = conf_conv_fwd(proj_cf, 0, D_CONF // CB, small["conf_conv_w"], small["conf_conv_b"], "conf_conv")
    w_out = get_w("w_out", uc)
    mix, u, x1, h2 = mixer_out(ysn, uc, small["conf_ln_w"], small["conf_ln_b"], w_out, x, mod, n2w, "out_proj_norm2")
    w_up_t = get_w("w_up", h2)
    up = mm_nt([(h2, w_up_t, 0)], "up_proj")
    act = ffn_conv_fwd(up, small["ffn_conv_w"], small["ffn_conv_b"], "ffn_conv")
    w_down = get_w("w_down", act)
    dx2, dff, dact, acc_f = final_loss(act, w_down, x1, mod, fnw, target, "down_proj_loss")

    token = put_grad("w_down", mm_tn(act, dff, "wgrad_down"))
    dupg, dupv, dwg, dwv = ffn_conv_bwd(up, small["ffn_conv_w"], after(small["ffn_conv_b"], token), dact, "ffn_conv_bwd")
    token = put_grad("w_up", jnp.concatenate([mm_tn(dupg, h2, "wgrad_up_gate"), mm_tn(dupv, h2, "wgrad_up_val")], axis=0))
    dx1, dmix, acc_2 = norm_mod_bwd([(dupg, w_up_t, 0), (dupv, w_up_t, 1)], x1, dx2, mod, after(n2w, token), 3, "norm2_bwd",
                                    mix=mix, gate_row=2)

    token = put_grad("w_out", jnp.concatenate([mm_tn(ysn, dmix, "wgrad_out_ssd"), mm_tn(u, dmix, "wgrad_out_conf")], axis=0))
    dysn, duc, acc_ln = mixer_out_bwd(dmix, w_out, uc, after(small["conf_ln_w"], token), small["conf_ln_b"], "out_proj_bwd")
    dcfa, dcfg, dw_cc = conf_conv_bwd(proj_cf, 0, D_CONF // CB, small["conf_conv_w"], duc, "conf_conv_bwd")
    dz, ddt, dxbc_post, acc_s, acc_s16 = ssd_bwd(dysn, y, xbc, proj_zx, proj_dt, s_prev, dtb_f, alog_f, dsk_f, snw,
                                                 "ssd_scan_bwd")
    dxbc, dw_sc = conv_silu_bwd(proj_zx, D_SSD // CB, D_XBC, small["ssd_conv_w"], small["ssd_conv_b"], dxbc_post, "ssd_conv_bwd")
    token = put_grad("w_in", jnp.concatenate(
        [mm_tn(dz, h1, "wgrad_in_z"), mm_tn(dxbc, h1, "wgrad_in_xbc"), mm_tn(ddt, h1, "wgrad_in_dt")[:N_HEADS],
         mm_tn(dcfa, h1, "wgrad_in_cfa"), mm_tn(dcfg, h1, "wgrad_in_cfg")], axis=0))
    dh1_pairs = [(dz, w_in["w_full"], 0), (ddt, w_in["w_dt16"], 0), (dcfa, w_in["w_cf"], 0), (dcfg, w_in["w_cf"], 1),
                 (dxbc, w_in["w_xbc"], 0)]
    grad_x, acc_1 = norm_mod_bwd(dh1_pairs, x, dx1, mod, after(n1w, token), 0, "norm1_bwd")

    small_accs = dict(acc_1=acc_1, acc_2=acc_2, acc_f=acc_f, acc_ln=acc_ln, acc_s=acc_s, acc_s16=acc_s16, dw_sc=dw_sc,
                      dw_cc=dw_cc, dwg=dwg, dwv=dwv)
    return grad_x, small_accs


def kernel(x, c, ada_w, ada_b, norm1_w, w_in, ssd_conv_w, ssd_conv_b, dt_bias, a_log, d_skip, ssd_norm_w, conf_conv_w, conf_conv_b, conf_ln_w, conf_ln_b, w_out, norm2_w, w_up, ffn_conv_w, ffn_conv_b, w_down, final_norm_w, loss_target, m_ada_w, m_ada_b, m_norm1_w, m_w_in, m_ssd_conv_w, m_ssd_conv_b, m_dt_bias, m_a_log, m_d_skip, m_ssd_norm_w, m_conf_conv_w, m_conf_conv_b, m_conf_ln_w, m_conf_ln_b, m_w_out, m_norm2_w, m_w_up, m_ffn_conv_w, m_ffn_conv_b, m_w_down, m_final_norm_w, v_ada_w, v_ada_b, v_norm1_w, v_w_in, v_ssd_conv_w, v_ssd_conv_b, v_dt_bias, v_a_log, v_d_skip, v_ssd_norm_w, v_conf_conv_w, v_conf_conv_b, v_conf_ln_w, v_conf_ln_b, v_w_out, v_norm2_w, v_w_up, v_ffn_conv_w, v_ffn_conv_b, v_w_down, v_final_norm_w):
    me = 4 * lax.axis_index("x") + 2 * lax.axis_index("y") + lax.axis_index("c")
    weights = dict(ada_w=ada_w, ada_b=ada_b, norm1_w=norm1_w, w_in=w_in, ssd_conv_w=ssd_conv_w, ssd_conv_b=ssd_conv_b,
                   dt_bias=dt_bias, a_log=a_log, d_skip=d_skip, ssd_norm_w=ssd_norm_w, conf_conv_w=conf_conv_w,
                   conf_conv_b=conf_conv_b, conf_ln_w=conf_ln_w, conf_ln_b=conf_ln_b, w_out=w_out, norm2_w=norm2_w, w_up=w_up,
                   ffn_conv_w=ffn_conv_w, ffn_conv_b=ffn_conv_b, w_down=w_down, final_norm_w=final_norm_w)
    moms_m = dict(ada_w=m_ada_w, ada_b=m_ada_b, norm1_w=m_norm1_w, w_in=m_w_in, ssd_conv_w=m_ssd_conv_w, ssd_conv_b=m_ssd_conv_b,
                  dt_bias=m_dt_bias, a_log=m_a_log, d_skip=m_d_skip, ssd_norm_w=m_ssd_norm_w, conf_conv_w=m_conf_conv_w,
                  conf_conv_b=m_conf_conv_b, conf_ln_w=m_conf_ln_w, conf_ln_b=m_conf_ln_b, w_out=m_w_out, norm2_w=m_norm2_w,
                  w_up=m_w_up, ffn_conv_w=m_ffn_conv_w, ffn_conv_b=m_ffn_conv_b, w_down=m_w_down, final_norm_w=m_final_norm_w)
    moms_v = dict(ada_w=v_ada_w, ada_b=v_ada_b, norm1_w=v_norm1_w, w_in=v_w_in, ssd_conv_w=v_ssd_conv_w, ssd_conv_b=v_ssd_conv_b,
                  dt_bias=v_dt_bias, a_log=v_a_log, d_skip=v_d_skip, ssd_norm_w=v_ssd_norm_w, conf_conv_w=v_conf_conv_w,
                  conf_conv_b=v_conf_conv_b, conf_ln_w=v_conf_ln_w, conf_ln_b=v_conf_ln_b, w_out=v_w_out, norm2_w=v_norm2_w,
                  w_up=v_w_up, ffn_conv_w=v_ffn_conv_w, ffn_conv_b=v_ffn_conv_b, w_down=v_w_down, final_norm_w=v_final_norm_w)
    names = list(weights)

    def to2d(a):
        return a[0] if a.ndim == 3 else a.reshape(1, -1)

    big = ("w_in", "w_out", "w_up", "w_down")

    c_all, scw_all, ccw_all, fcw_all = exchange([c.reshape(8, LANE), ssd_conv_w[0], conf_conv_w[0], ffn_conv_w[0]],
                                                "gather_small", gather=True)
    c_all = c_all.reshape(N_DEV, D)

    ada_cols = ada_w.shape[2]
    ada_b_cols = lax.dynamic_slice(ada_b, (0, me * ada_cols), (1, ada_cols))
    mod_cols, c_act_all = ada_mod(c_all, ada_w[0], ada_b_cols, "ada_mod")
    mod_parts, = exchange([jnp.pad(mod_cols, ((0, 0), (0, D - ada_cols))).reshape(N_DEV, 8, LANE)], "scatter_mod", gather=False)
    mod = mod_parts.reshape(N_DEV, D)[:, :ada_cols].reshape(6, D)
    mod = jnp.pad(mod, ((0, 2), (0, 0)))

    def rows_of(a):
        return jnp.swapaxes(a, 1, 2)[0] if a.shape[2] != D else a[0]

    shards, mod = lax.optimization_barrier(([rows_of(weights[n]).astype(BF16) for n in big], mod))
    w_in_slabs = gather_two_level(shards[0], "gather_w_in")
    later, w_in_slabs = lax.optimization_barrier((shards[1:], w_in_slabs))
    gather_parts, token = exchange_start(later, "gather_weights_start", gather=True)
    mod = mod + token[0:1, 0:1]

    small = {n: to2d(weights[n]) for n in names if n not in ("ada_w",) + big}
    small["ssd_conv_w"] = _slabs_to_cols(scw_all)
    small["conf_conv_w"] = _slabs_to_cols(ccw_all)
    small["ffn_conv_w"] = _slabs_to_cols(fcw_all)

    def with_own(landed, own):
        return lax.dynamic_update_slice(landed, own[None], (me,) + (0,) * own.ndim)

    def get_w(n, after):
        if n == "w_in":
            slabs = w_in_slabs
        else:
            a = big.index(n)
            landed, = exchange_wait([gather_parts[a - 1]], after, "gather_" + n + "_wait", gather=True)
            slabs = with_own(landed, shards[a])
        full = slabs.reshape(N_DEV * slabs.shape[1], D)
        if n != "w_in":
            return full
        w_dt = full[D_SSD + D_XBC:D_SSD + D_XBC + N_HEADS]
        return dict(w_full=full, w_xbc=full[D_SSD:D_SSD + D_XBC], w_cf=full[D_SSD + D_XBC + N_HEADS:],
                    w_dt_rep=jnp.repeat(w_dt, HEAD, axis=0), w_dt16=jnp.pad(w_dt, ((0, LANE - N_HEADS), (0, 0))))

    scatter_parts, sent = {}, {}

    def put_grad(n, g):
        slabs = g.reshape(N_DEV, g.shape[0] // N_DEV, g.shape[1]).astype(BF16)
        two_level = n == "w_in"
        sent[n] = pair_sum(slabs, sibling_swap(slabs, "scatter_w_in_sibling"), "scatter_w_in_pair_sum") if two_level else slabs
        (scatter_parts[n],), token = exchange_start([sent[n]], "scatter_" + n + "_start", gather=False, same_core=two_level)
        return token

    grad_x, accs = local_fwd_bwd(x[0], loss_target[0], mod, get_w, put_grad, small)
    loss = lax.psum(0.5 / D * jnp.sum(accs["acc_f"][2:3]), ("x", "y", "c"))

    grads, delta, new_m, new_v = {}, {}, {}, {}

    def finish(ns, after, name):
        two_level = ns == big[:1]
        landed = exchange_wait([scatter_parts[n] for n in ns], after, name, gather=False, same_core=two_level)
        for n, slots in zip(ns, landed):
            own_slot = me // 2 if two_level else me
            slots = lax.dynamic_update_slice(slots, lax.dynamic_index_in_dim(sent[n], own_slot, 0, keepdims=True),
                                             (own_slot,) + (0,) * (slots.ndim - 1))
            out = adamw_slots(rows_of(weights[n]), slots, rows_of(moms_m[n]), rows_of(moms_v[n]), "adamw_" + n)
            if weights[n].shape[2] != D:
                out = [jnp.swapaxes(o, 0, 1) for o in out]
            grads[n], delta[n], new_m[n], new_v[n] = out

    finish(big[1:], grad_x, "scatter_grads_wait")

    order = ("acc_1", "acc_2", "acc_f", "acc_ln", "acc_s", "acc_s16", "dw_sc", "dw_cc", "dwg", "dwv")
    shapes = [accs[k].shape for k in order]
    acc_list, _ = lax.optimization_barrier(([accs[k] for k in order], [new_v[n] for n in big[1:]]))
    packed_all, = exchange([pack_rows(acc_list, "pack_small_grads")], "gather_small_grads", gather=True)
    packed_red, = sum_slots_many([packed_all], "sum_small_grads")
    gathered = dict(zip(order, unpack_rows(packed_all, shapes)))
    red = dict(zip(order, unpack_rows(packed_red, shapes)))

    def mod_rows(a1, a2, af):
        return jnp.concatenate([a1[..., 0:2, :], a2[..., 3:4, :], a2[..., 0:2, :], af[..., 1:2, :]], axis=-2)

    dmod_all = mod_rows(gathered["acc_1"], gathered["acc_2"], gathered["acc_f"]).reshape(N_DEV, 6 * D)
    grads["ada_w"] = ada_wgrad(c_act_all, lax.dynamic_slice(dmod_all, (0, me * ada_cols), (N_DEV, ada_cols)), "ada_wgrad")

    def my_cols(full, k_taps):
        cols = full.shape[1] // N_DEV
        return lax.dynamic_slice(full, (0, me * cols), (k_taps, cols))

    fcw = jnp.concatenate([red["dwg"], red["dwv"]], axis=1)
    grads.update(
        ada_b=mod_rows(red["acc_1"], red["acc_2"], red["acc_f"]).reshape(1, 6 * D), norm1_w=red["acc_1"][2:3],
        ssd_conv_w=my_cols(red["dw_sc"], K_SSD), ssd_conv_b=red["dw_sc"][K_SSD:K_SSD + 1],
        dt_bias=red["acc_s16"][1:2, :N_HEADS], a_log=red["acc_s16"][2:3, :N_HEADS], d_skip=red["acc_s16"][3:4, :N_HEADS],
        ssd_norm_w=red["acc_s"][0:1], conf_conv_w=my_cols(red["dw_cc"], K_CONF), conf_conv_b=red["dw_cc"][K_CONF:K_CONF + 1],
        conf_ln_w=red["acc_ln"][0:1], conf_ln_b=red["acc_ln"][1:2], norm2_w=red["acc_2"][2:3],
        ffn_conv_w=my_cols(fcw, K_FFN), ffn_conv_b=fcw[K_FFN:K_FFN + 1], final_norm_w=red["acc_f"][0:1])

    rest = [n for n in names if n not in big]
    d_l, m_l, v_l = adamw_many([to2d(weights[n]) for n in rest], [grads[n] for n in rest], [to2d(moms_m[n]) for n in rest],
                               [to2d(moms_v[n]) for n in rest], "adamw_small")
    for n, dd, mm, vv in zip(rest, d_l, m_l, v_l):
        delta[n], new_m[n], new_v[n] = dd, mm, vv
    finish(big[:1], d_l[0], "scatter_w_in_wait")
    shape_of = lambda d_: {n: d_[n].reshape(weights[n].shape) for n in names}
    grads, delta, new_m, new_v = shape_of(grads), shape_of(delta), shape_of(new_m), shape_of(new_v)
    return (loss, grad_x[None], *[grads[n] for n in names], *[delta[n] for n in names], *[new_m[n] for n in names],
            *[new_v[n] for n in names])
```

```python
import functools

import jax
import jax.numpy as jnp
from jax import lax
from jax.experimental import pallas as pl
from jax.experimental.pallas import tpu as pltpu

F32 = jnp.float32
BF16 = jnp.bfloat16
HI = lax.Precision.HIGHEST

N_DEV = 8
D = 1024
D_SSD = 1024
HEAD = 64
N_HEADS = 16
N_STATE = 128
D_XBC = 1536
D_CONF = 1024
D_FF = 2816
K_SSD, K_CONF, K_FFN = 4, 31, 3
D_INP = 5632
LANE = 128
TR = 256
TM = 512
Q = 256
CB = 256
TC = 1024
VMEM_LIMIT = 56 * 1024 * 1024

ADAM_LR, ADAM_B1, ADAM_B2, ADAM_EPS, ADAM_WD, ADAM_STEP = 0.001, 0.9, 0.999, 1e-08, 0.01, 10


def _cparams(sem=None):
    return pltpu.CompilerParams(vmem_limit_bytes=VMEM_LIMIT, dimension_semantics=sem)


def _sds(shape, dtype):
    return jax.ShapeDtypeStruct(shape, dtype)


def _sigmoid(x):
    return 1.0 / (1.0 + jnp.exp(-x))


def _silu(x):
    return x * _sigmoid(x)


def _dsilu(x):
    s = _sigmoid(x)
    return s * (1.0 + x * (1.0 - s))


def _softplus(x):
    return jnp.maximum(x, 0.0) + jnp.log(1.0 + jnp.exp(-jnp.abs(x)))


def _dot(a, b):
    return jnp.dot(a.astype(BF16), b.astype(BF16), preferred_element_type=F32)


def _dot_nt(a, b):
    return lax.dot_general(a.astype(BF16), b.astype(BF16), (((1,), (1,)), ((), ())), preferred_element_type=F32)


def _dot_tn(a, b):
    return lax.dot_general(a.astype(BF16), b.astype(BF16), (((0,), (0,)), ((), ())), preferred_element_type=F32)


def _bf16_terms(a, terms):
    parts, rem = [], a
    for t in range(terms):
        p = rem.astype(BF16)
        parts.append(p)
        if t + 1 < terms:
            rem = rem - p.astype(F32)
    return parts


def _dot_exact(a, b, terms, exact, dims=(((1,), (0,)), ((), ()))):
    if exact == "a":
        a_b = a.astype(BF16)
        outs = [lax.dot_general(a_b, p, dims, preferred_element_type=F32) for p in _bf16_terms(b, terms)]
    else:
        b_b = b.astype(BF16)
        outs = [lax.dot_general(p, b_b, dims, preferred_element_type=F32) for p in _bf16_terms(a, terms)]
    acc = outs[-1]
    for o in reversed(outs[:-1]):
        acc = acc + o
    return acc


def _dot_tn_hi(a, b):
    return lax.dot_general(a, b, (((0,), (0,)), ((), ())), precision=HI, preferred_element_type=F32)


def _colsum(x):
    return jnp.sum(x, axis=0, keepdims=True)


def _const_spec(shape):
    return pl.BlockSpec(shape, lambda *_: (0,) * len(shape))


def _col_tile(n):
    for t in (1408, 1024, 768, 512, 256, 128):
        if n % t == 0 and t <= n:
            return t
    return n


def mm_nt(pairs, name):
    L = pairs[0][0].shape[0]
    K = pairs[0][1].shape[0]
    tk = _col_tile(K)
    n = len(pairs)

    def body(*refs):
        o_ref = refs[-1]
        acc = None
        for p in range(n):
            t = lax.dot_general(refs[2 * p][...], refs[2 * p + 1][...], (((1,), (1,)), ((), ())),
                                preferred_element_type=F32)
            acc = t if acc is None else acc + t
        o_ref[...] = acc

    in_specs, args = [], []
    for a, w, cb in pairs:
        in_specs += [pl.BlockSpec((TM, a.shape[1]), lambda j, i: (i, 0)),
                     pl.BlockSpec((tk, a.shape[1]), functools.partial(lambda j, i, cb: (j, cb), cb=cb))]
        args += [a, w]
    return pl.pallas_call(
        body, name=name, grid=(K // tk, L // TM), out_shape=_sds((L, K), F32), in_specs=in_specs,
        out_specs=pl.BlockSpec((TM, tk), lambda j, i: (i, j)),
        compiler_params=_cparams(("parallel", "parallel")))(*args)


def mm_tn(a, g, name):
    L, M = a.shape
    N = g.shape[1]
    tn = _col_tile(N) if N > 1024 else N
    if M * tn * 4 > 12 * 1024 * 1024:
        tn = 512
    tl = 512 if L % 512 == 0 else TR
    nl = L // tl

    def body(a_ref, g_ref, o_ref, acc_ref):
        @pl.when(pl.program_id(1) == 0)
        def _():
            acc_ref[...] = jnp.zeros((M, tn), F32)

        acc_ref[...] += lax.dot_general(a_ref[...], g_ref[...], (((0,), (0,)), ((), ())), preferred_element_type=F32)

        @pl.when(pl.program_id(1) == nl - 1)
        def _():
            o_ref[...] = acc_ref[...].astype(BF16)

    return pl.pallas_call(
        body, name=name, grid=(N // tn, nl), out_shape=_sds((M, N), BF16),
        in_specs=[pl.BlockSpec((tl, M), lambda j, l: (l, 0)), pl.BlockSpec((tl, tn), lambda j, l: (l, j))],
        out_specs=pl.BlockSpec((M, tn), lambda j, l: (0, j)), scratch_shapes=[pltpu.VMEM((M, tn), F32)],
        compiler_params=_cparams(("parallel", "arbitrary")))(a, g)


def _row_spec(width=D):
    return pl.BlockSpec((TR, width), lambda i: (i, 0))


def in_proj(x, mod, n1w, w_full, w_dt_rep, w_cf, name):
    L = x.shape[0]
    n_zx = D_SSD + D_XBC

    def body(x_ref, mod_ref, w_ref, wzx_ref, wdt_ref, wcf_ref, h_ref, zx_ref, dt_ref, cf_ref):
        xin = x_ref[...]
        r = lax.rsqrt(jnp.mean(xin * xin, axis=-1, keepdims=True) + 1e-6)
        h = ((xin * r * w_ref[...]) * (1.0 + mod_ref[1:2, :]) + mod_ref[0:1, :]).astype(BF16)
        h_ref[...] = h
        nt = (((1,), (1,)), ((), ()))
        zx_ref[...] = lax.dot_general(h, wzx_ref[...], nt, preferred_element_type=F32)
        dt_ref[...] = lax.dot_general(h, wdt_ref[...], nt, preferred_element_type=F32)
        cf_ref[...] = lax.dot_general(h, wcf_ref[...], nt, preferred_element_type=F32)

    row = lambda w: pl.BlockSpec((TM, w), lambda i: (i, 0))
    return pl.pallas_call(
        body, name=name, grid=(L // TM,),
        out_shape=[_sds((L, D), BF16), _sds((L, n_zx), F32), _sds((L, D_SSD), F32), _sds((L, 2 * D_CONF), F32)],
        in_specs=[row(D), _const_spec((8, D)), _const_spec((1, D)), _const_spec((n_zx, D)), _const_spec((D_SSD, D)),
                  _const_spec((2 * D_CONF, D))],
        out_specs=[row(D), row(n_zx), row(D_SSD), row(2 * D_CONF)],
        compiler_params=_cparams(("parallel",)))(x, mod, n1w, w_full, w_dt_rep, w_cf)


def mixer_out(ysn, uc, lnw, lnb, w_out, x, mod, n2w, name):
    L = x.shape[0]

    def body(ysn_ref, uc_ref, lnw_ref, lnb_ref, wo_ref, x_ref, mod_ref, n2w_ref, mix_ref, u_ref, x1_ref, h2_ref):
        uc_v = uc_ref[...]
        mu = jnp.mean(uc_v, axis=-1, keepdims=True)
        var = jnp.mean(jnp.square(uc_v - mu), axis=-1, keepdims=True)
        u = _silu((uc_v - mu) * lax.rsqrt(var + 1e-5) * lnw_ref[...] + lnb_ref[...]).astype(BF16)
        u_ref[...] = u
        mix = (jnp.dot(ysn_ref[...], wo_ref[0:D_SSD, :], preferred_element_type=F32)
               + jnp.dot(u, wo_ref[D_SSD:D_SSD + D_CONF, :], preferred_element_type=F32))
        mix_ref[...] = mix
        x1 = x_ref[...] + mod_ref[2:3, :] * mix
        x1_ref[...] = x1
        r = lax.rsqrt(jnp.mean(x1 * x1, axis=-1, keepdims=True) + 1e-6)
        h2_ref[...] = ((x1 * r * n2w_ref[...]) * (1.0 + mod_ref[4:5, :]) + mod_ref[3:4, :]).astype(BF16)

    return pl.pallas_call(
        body, name=name, grid=(L // TR,),
        out_shape=[_sds((L, D), F32), _sds((L, D_CONF), BF16), _sds((L, D), F32), _sds((L, D), BF16)],
        in_specs=[_row_spec(), _row_spec(), _const_spec((1, D)), _const_spec((1, D)), _const_spec((D_SSD + D_CONF, D)), _row_spec(),
                  _const_spec((8, D)), _const_spec((1, D))],
        out_specs=[_row_spec()] * 4, compiler_params=_cparams(("parallel",)))(ysn, uc, lnw, lnb, w_out, x, mod, n2w)


def mixer_out_bwd(dmix, w_out, uc, lnw, lnb, name):
    L = uc.shape[0]

    def body(dm_ref, wo_ref, u_ref, w_ref, b_ref, dy_ref, o_ref, acc_ref):
        @pl.when(pl.program_id(0) == 0)
        def _():
            acc_ref[...] = jnp.zeros((8, D), F32)

        nt = (((1,), (1,)), ((), ()))
        dm = dm_ref[...]
        dy_ref[...] = lax.dot_general(dm, wo_ref[0:D_SSD, :], nt, preferred_element_type=F32)
        du = lax.dot_general(dm, wo_ref[D_SSD:D_SSD + D_CONF, :], nt, preferred_element_type=F32)
        u = u_ref[...]
        mu = jnp.mean(u, axis=-1, keepdims=True)
        rl = lax.rsqrt(jnp.mean(jnp.square(u - mu), axis=-1, keepdims=True) + 1e-5)
        n = (u - mu) * rl
        v = n * w_ref[...] + b_ref[...]
        dv = du * _dsilu(v)
        acc_ref[0:1, :] += _colsum(dv * n)
        acc_ref[1:2, :] += _colsum(dv)
        dn = dv * w_ref[...]
        o_ref[...] = rl * (dn - jnp.mean(dn, axis=-1, keepdims=True) - n * jnp.mean(dn * n, axis=-1, keepdims=True))

    return pl.pallas_call(body, name=name, grid=(L // TR,), out_shape=[_sds((L, D), F32), _sds((L, D), F32), _sds((8, D), F32)],
                          in_specs=[_row_spec(), _const_spec((D_SSD + D_CONF, D)), _row_spec(), _const_spec((1, D)),
                                    _const_spec((1, D))],
                          out_specs=[_row_spec(), _row_spec(), _const_spec((8, D))],
                          compiler_params=_cparams(("arbitrary",)))(dmix, w_out, uc, lnw, lnb)


def final_loss(act, w_down, x1, mod, fw, target, name):
    L = act.shape[0]

    def body(act_ref, wd_ref, x1_ref, mod_ref, fw_ref, t_ref, dx_ref, dff_ref, dact_ref, acc_ref):
        @pl.when(pl.program_id(0) == 0)
        def _():
            acc_ref[...] = jnp.zeros((8, D), F32)

        ff_v = jnp.dot(act_ref[...], wd_ref[...], preferred_element_type=F32)
        g2 = mod_ref[5:6, :]
        x2 = x1_ref[...] + g2 * ff_v
        r = lax.rsqrt(jnp.mean(x2 * x2, axis=-1, keepdims=True) + 1e-6)
        n = x2 * r
        err = n * fw_ref[...] - t_ref[...]
        dy = err * (1.0 / D)
        dn = dy * fw_ref[...]
        dx2 = r * (dn - n * jnp.mean(dn * n, axis=-1, keepdims=True))
        acc_ref[0:1, :] += _colsum(dy * n)
        acc_ref[1:2, :] += _colsum(dx2 * ff_v)
        acc_ref[2:3, :] += _colsum(err * err)
        dx_ref[...] = dx2
        dff = (dx2 * g2).astype(BF16)
        dff_ref[...] = dff
        dact_ref[...] = lax.dot_general(dff, wd_ref[...], (((1,), (1,)), ((), ())), preferred_element_type=F32)

    return pl.pallas_call(
        body, name=name, grid=(L // TR,),
        out_shape=[_sds((L, D), F32), _sds((L, D), BF16), _sds((L, D_FF), F32), _sds((8, D), F32)],
        in_specs=[_row_spec(D_FF), _const_spec((D_FF, D)), _row_spec(), _const_spec((8, D)), _const_spec((1, D)), _row_spec()],
        out_specs=[_row_spec(), _row_spec(), _row_spec(D_FF), _const_spec((8, D))],
        compiler_params=_cparams(("arbitrary",)))(act, w_down, x1, mod, fw, target)


def norm_mod_bwd(dh_pairs, xin, dres, mod, w, shift_row, name, mix=None, gate_row=None):
    L = xin.shape[0]
    has_mix = mix is not None
    n_pairs = len(dh_pairs)

    def body(*refs):
        pair_refs, refs = refs[:2 * n_pairs], refs[2 * n_pairs:]
        if has_mix:
            x_ref, dres_ref, mod_ref, w_ref, mix_ref, dx_ref, dmix_ref, acc_ref = refs
        else:
            x_ref, dres_ref, mod_ref, w_ref, dx_ref, acc_ref = refs

        @pl.when(pl.program_id(0) == 0)
        def _():
            acc_ref[...] = jnp.zeros((8, D), F32)

        dh_v = None
        for p in range(n_pairs):
            t = jnp.dot(pair_refs[2 * p][...], pair_refs[2 * p + 1][...], preferred_element_type=F32)
            dh_v = t if dh_v is None else dh_v + t
        x = x_ref[...]
        r = lax.rsqrt(jnp.mean(x * x, axis=-1, keepdims=True) + 1e-6)
        n = x * r
        nw = n * w_ref[...]
        sc1 = 1.0 + mod_ref[shift_row + 1:shift_row + 2, :]
        acc_ref[0:1, :] += _colsum(dh_v)
        acc_ref[1:2, :] += _colsum(dh_v * nw)
        dnw = dh_v * sc1
        acc_ref[2:3, :] += _colsum(dnw * n)
        dn = dnw * w_ref[...]
        dx = r * (dn - n * jnp.mean(dn * n, axis=-1, keepdims=True)) + dres_ref[...]
        dx_ref[...] = dx
        if has_mix:
            acc_ref[3:4, :] += _colsum(dx * mix_ref[...])
            dmix_ref[...] = (dx * mod_ref[gate_row:gate_row + 1, :]).astype(BF16)

    ins, in_specs = [], []
    for a, wt, rb in dh_pairs:
        ins += [a, wt]
        in_specs += [_row_spec(a.shape[1]), pl.BlockSpec((a.shape[1], D), functools.partial(lambda i, rb: (rb, 0), rb=rb))]
    ins += [xin, dres, mod, w] + ([mix] if has_mix else [])
    in_specs += [_row_spec(), _row_spec(), _const_spec((8, D)), _const_spec((1, D))] + ([_row_spec()] if has_mix else [])
    out_shape = [_sds((L, D), F32)] + ([_sds((L, D), BF16)] if has_mix else []) + [_sds((8, D), F32)]
    out_specs = [_row_spec()] + ([_row_spec()] if has_mix else []) + [_const_spec((8, D))]
    return pl.pallas_call(body, name=name, grid=(L // TR,), out_shape=out_shape, in_specs=in_specs,
                          out_specs=out_specs, compiler_params=_cparams(("arbitrary",)))(*ins)


def _halo(k):
    return 8 if k <= 9 else 32


def _prev_spec(h, col0):
    return pl.BlockSpec((h, CB), lambda j, i: (jnp.maximum(i * (TC // h) - 1, 0), j + col0))


def _next_spec(h, col0, n_tiles):
    return pl.BlockSpec((h, CB), lambda j, i: (jnp.minimum(i + 1, n_tiles - 1) * (TC // h), j + col0))


def _tile_spec(col0):
    return pl.BlockSpec((TC, CB), lambda j, i: (i, j + col0))


def _w_spec(kp, col0):
    return pl.BlockSpec((kp, CB), lambda j, i: (0, j + col0))


SUBLANES = 8


def _shifted_windows(v, taps, rows):
    for r in range(SUBLANES):
        group = [(o, k) for o, k in taps if o % SUBLANES == r]
        if not group:
            continue
        s = v if r == 0 else pltpu.roll(v, v.shape[0] - r, 0)
        for o, k in group:
            yield k, s[o - r:o - r + rows, :]


def _causal_taps(ext_ref, w_ref, k_taps, first, rows):
    acc = None
    for k, win in _shifted_windows(ext_ref[...], [(first - (k_taps - 1) + k, k) for k in range(k_taps)], rows):
        t = w_ref[k:k + 1, :] * win
        acc = t if acc is None else acc + t
    return acc


def _anticausal_taps(d_ref, w_ref, k_taps, rows):
    acc = None
    for k, win in _shifted_windows(d_ref[...], [(k_taps - 1 - k, k) for k in range(k_taps)], rows):
        t = w_ref[k:k + 1, :] * win
        acc = t if acc is None else acc + t
    return acc


def _acc_conv_wgrad(dw_ref, d_tile, ext_ref, k_taps, first):
    for k, win in _shifted_windows(ext_ref[...], [(first - (k_taps - 1) + k, k) for k in range(k_taps)], TC):
        dw_ref[k:k + 1, :] += _colsum(d_tile * win)
    dw_ref[k_taps:k_taps + 1, :] += _colsum(d_tile)


def conv_silu_fwd(x, col0, width, w, b, name):
    L = x.shape[0]
    k_taps = w.shape[0]
    h = _halo(k_taps)

    def body(xp_ref, x_ref, w_ref, b_ref, o_ref, ext_ref):
        i = pl.program_id(1)
        ext_ref[0:h, :] = jnp.where(i > 0, xp_ref[...], 0.0)
        ext_ref[h:h + TC, :] = x_ref[...]
        o_ref[...] = _silu(_causal_taps(ext_ref, w_ref, k_taps, h, TC) + b_ref[...])

    return pl.pallas_call(
        body, name=name, grid=(width // CB, L // TC), out_shape=_sds((L, width), F32),
        in_specs=[_prev_spec(h, col0), _tile_spec(col0), _w_spec(k_taps, 0), pl.BlockSpec((1, CB), lambda j, i: (0, j))],
        out_specs=_tile_spec(0), scratch_shapes=[pltpu.VMEM((h + TC, CB), F32)],
        compiler_params=_cparams(("parallel", "parallel")))(x, x, w, b)


def conv_silu_bwd(x, col0, width, w, b, dpost, name):
    L = x.shape[0]
    k_taps = w.shape[0]
    h = _halo(k_taps)
    nt = L // TC

    def body(xp_ref, x_ref, xn_ref, d_ref, dn_ref, w_ref, b_ref, dx_ref, dw_ref, ext_ref, dpre_ref):
        i = pl.program_id(1)

        @pl.when(i == 0)
        def _():
            dw_ref[...] = jnp.zeros((8, CB), F32)

        ext_ref[0:h, :] = jnp.where(i > 0, xp_ref[...], 0.0)
        ext_ref[h:h + TC, :] = x_ref[...]
        ext_ref[h + TC:h + TC + h, :] = xn_ref[...]
        pre = _causal_taps(ext_ref, w_ref, k_taps, h, TC + h) + b_ref[...]
        dpre_ref[0:TC, :] = d_ref[...] * _dsilu(pre[0:TC, :])
        dpre_ref[TC:TC + h, :] = jnp.where(i < nt - 1, dn_ref[...], 0.0) * _dsilu(pre[TC:TC + h, :])
        dx_ref[...] = _anticausal_taps(dpre_ref, w_ref, k_taps, TC).astype(BF16)
        _acc_conv_wgrad(dw_ref, dpre_ref[0:TC, :], ext_ref, k_taps, h)

    return pl.pallas_call(
        body, name=name, grid=(width // CB, nt),
        out_shape=[_sds((L, width), BF16), _sds((8, width), F32)],
        in_specs=[_prev_spec(h, col0), _tile_spec(col0), _next_spec(h, col0, nt), _tile_spec(0), _next_spec(h, 0, nt),
                  _w_spec(k_taps, 0), pl.BlockSpec((1, CB), lambda j, i: (0, j))],
        out_specs=[_tile_spec(0), _w_spec(8, 0)],
        scratch_shapes=[pltpu.VMEM((h + TC + h, CB), F32), pltpu.VMEM((TC + h, CB), F32)],
        compiler_params=_cparams(("parallel", "arbitrary")))(x, x, x, dpost, dpost, w, b)


def conf_conv_fwd(proj, col_a, col_g, w, b, name):
    L = proj.shape[0]
    k_taps = w.shape[0]
    h = _halo(k_taps)

    def body(ap_ref, a_ref, gp_ref, g_ref, w_ref, b_ref, o_ref, ext_ref):
        i = pl.program_id(1)
        ext_ref[0:h, :] = jnp.where(i > 0, ap_ref[...] * _sigmoid(gp_ref[...]), 0.0)
        ext_ref[h:h + TC, :] = a_ref[...] * _sigmoid(g_ref[...])
        o_ref[...] = _causal_taps(ext_ref, w_ref, k_taps, h, TC) + b_ref[...]

    return pl.pallas_call(
        body, name=name, grid=(D_CONF // CB, L // TC), out_shape=_sds((L, D_CONF), F32),
        in_specs=[_prev_spec(h, col_a), _tile_spec(col_a), _prev_spec(h, col_g), _tile_spec(col_g), _w_spec(k_taps, 0),
                  pl.BlockSpec((1, CB), lambda j, i: (0, j))],
        out_specs=_tile_spec(0), scratch_shapes=[pltpu.VMEM((h + TC, CB), F32)],
        compiler_params=_cparams(("parallel", "parallel")))(proj, proj, proj, proj, w, b)


def conf_conv_bwd(proj, col_a, col_g, w, duc, name):
    L = proj.shape[0]
    k_taps = w.shape[0]
    h = _halo(k_taps)
    nt = L // TC

    def body(ap_ref, a_ref, gp_ref, g_ref, d_ref, dn_ref, w_ref, da_ref, dg_ref, dw_ref, ext_ref, dext_ref):
        i = pl.program_id(1)

        @pl.when(i == 0)
        def _():
            dw_ref[...] = jnp.zeros((32, CB), F32)

        a = a_ref[...]
        s = _sigmoid(g_ref[...])
        ext_ref[0:h, :] = jnp.where(i > 0, ap_ref[...] * _sigmoid(gp_ref[...]), 0.0)
        ext_ref[h:h + TC, :] = a * s
        dext_ref[0:TC, :] = d_ref[...]
        dext_ref[TC:TC + h, :] = jnp.where(i < nt - 1, dn_ref[...], 0.0)
        du0 = _anticausal_taps(dext_ref, w_ref, k_taps, TC)
        da_ref[...] = (du0 * s).astype(BF16)
        dg_ref[...] = (du0 * a * s * (1.0 - s)).astype(BF16)
        _acc_conv_wgrad(dw_ref, d_ref[...], ext_ref, k_taps, h)

    return pl.pallas_call(
        body, name=name, grid=(D_CONF // CB, nt),
        out_shape=[_sds((L, D_CONF), BF16), _sds((L, D_CONF), BF16), _sds((32, D_CONF), F32)],
        in_specs=[_prev_spec(h, col_a), _tile_spec(col_a), _prev_spec(h, col_g), _tile_spec(col_g), _tile_spec(0),
                  _next_spec(h, 0, nt), _w_spec(k_taps, 0)],
        out_specs=[_tile_spec(0), _tile_spec(0), _w_spec(32, 0)],
        scratch_shapes=[pltpu.VMEM((h + TC, CB), F32), pltpu.VMEM((TC + h, CB), F32)],
        compiler_params=_cparams(("parallel", "arbitrary")))(proj, proj, proj, proj, duc, duc, w)


def ffn_conv_fwd(up, w, b, name):
    L = up.shape[0]
    k_taps = w.shape[0]
    h = _halo(k_taps)
    cv = D_FF // CB

    def body(gp_ref, g_ref, vp_ref, v_ref, wg_ref, wv_ref, bg_ref, bv_ref, o_ref, eg_ref, ev_ref):
        i = pl.program_id(1)
        eg_ref[0:h, :] = jnp.where(i > 0, gp_ref[...], 0.0)
        eg_ref[h:h + TC, :] = g_ref[...]
        ev_ref[0:h, :] = jnp.where(i > 0, vp_ref[...], 0.0)
        ev_ref[h:h + TC, :] = v_ref[...]
        pg = _causal_taps(eg_ref, wg_ref, k_taps, h, TC) + bg_ref[...]
        pv = _causal_taps(ev_ref, wv_ref, k_taps, h, TC) + bv_ref[...]
        o_ref[...] = (_silu(pg) * pv).astype(BF16)

    bspec = lambda c0: pl.BlockSpec((1, CB), lambda j, i: (0, j + c0))
    return pl.pallas_call(
        body, name=name, grid=(cv, L // TC), out_shape=_sds((L, D_FF), BF16),
        in_specs=[_prev_spec(h, 0), _tile_spec(0), _prev_spec(h, cv), _tile_spec(cv), _w_spec(k_taps, 0), _w_spec(k_taps, cv),
                  bspec(0), bspec(cv)],
        out_specs=_tile_spec(0), scratch_shapes=[pltpu.VMEM((h + TC, CB), F32), pltpu.VMEM((h + TC, CB), F32)],
        compiler_params=_cparams(("parallel", "parallel")))(up, up, up, up, w, w, b, b)


def ffn_conv_bwd(up, w, b, dact, name):
    L = up.shape[0]
    k_taps = w.shape[0]
    h = _halo(k_taps)
    nt = L // TC
    cv = D_FF // CB

    def body(gp_ref, g_ref, gn_ref, vp_ref, v_ref, vn_ref, d_ref, dn_ref, wg_ref, wv_ref, bg_ref, bv_ref,
             dg_ref, dv_ref, dwg_ref, dwv_ref, eg_ref, ev_ref, pg_ref, pv_ref):
        i = pl.program_id(1)

        @pl.when(i == 0)
        def _():
            dwg_ref[...] = jnp.zeros((8, CB), F32)
            dwv_ref[...] = jnp.zeros((8, CB), F32)

        for e_ref, p_ref, c_ref, n_ref in ((eg_ref, gp_ref, g_ref, gn_ref), (ev_ref, vp_ref, v_ref, vn_ref)):
            e_ref[0:h, :] = jnp.where(i > 0, p_ref[...], 0.0)
            e_ref[h:h + TC, :] = c_ref[...]
            e_ref[h + TC:h + TC + h, :] = n_ref[...]
        pg = _causal_taps(eg_ref, wg_ref, k_taps, h, TC + h) + bg_ref[...]
        pv = _causal_taps(ev_ref, wv_ref, k_taps, h, TC + h) + bv_ref[...]
        dact_t = d_ref[...]
        dact_n = jnp.where(i < nt - 1, dn_ref[...], 0.0)
        pg_ref[0:TC, :] = dact_t * pv[0:TC, :] * _dsilu(pg[0:TC, :])
        pg_ref[TC:TC + h, :] = dact_n * pv[TC:TC + h, :] * _dsilu(pg[TC:TC + h, :])
        pv_ref[0:TC, :] = dact_t * _silu(pg[0:TC, :])
        pv_ref[TC:TC + h, :] = dact_n * _silu(pg[TC:TC + h, :])
        dg_ref[...] = _anticausal_taps(pg_ref, wg_ref, k_taps, TC).astype(BF16)
        dv_ref[...] = _anticausal_taps(pv_ref, wv_ref, k_taps, TC).astype(BF16)
        _acc_conv_wgrad(dwg_ref, pg_ref[0:TC, :], eg_ref, k_taps, h)
        _acc_conv_wgrad(dwv_ref, pv_ref[0:TC, :], ev_ref, k_taps, h)

    bspec = lambda c0: pl.BlockSpec((1, CB), lambda j, i: (0, j + c0))
    ext = pltpu.VMEM((h + TC + h, CB), F32)
    dpre = pltpu.VMEM((TC + h, CB), F32)
    return pl.pallas_call(
        body, name=name, grid=(cv, nt),
        out_shape=[_sds((L, D_FF), BF16), _sds((L, D_FF), BF16), _sds((8, D_FF), F32), _sds((8, D_FF), F32)],
        in_specs=[_prev_spec(h, 0), _tile_spec(0), _next_spec(h, 0, nt), _prev_spec(h, cv), _tile_spec(cv), _next_spec(h, cv, nt),
                  _tile_spec(0), _next_spec(h, 0, nt), _w_spec(k_taps, 0), _w_spec(k_taps, cv), bspec(0), bspec(cv)],
        out_specs=[_tile_spec(0), _tile_spec(0), _w_spec(8, 0), _w_spec(8, 0)],
        scratch_shapes=[ext, ext, dpre, dpre],
        compiler_params=_cparams(("parallel", "arbitrary")))(up, up, up, up, up, up, dact, dact, w, w, b, b)


def _ssd_common(xbc_ref, dt_ref, dtb_ref, alog_ref, cs_ref):
    xs = xbc_ref[:, 0:D_SSD]
    sp_in = dt_ref[...] + dtb_ref[...]
    dtf = _softplus(sp_in)
    a_f = -jnp.exp(alog_ref[...])
    a_dt = dtf * a_f
    row = lax.broadcasted_iota(jnp.int32, (Q, Q), 0)
    col = lax.broadcasted_iota(jnp.int32, (Q, Q), 1)
    causal = row >= col
    cs = _dot_exact(causal.astype(F32), a_dt, 3, "a")
    cs_ref[...] = cs
    cs_last = cs_ref[Q - 1:Q, :]
    return xs, sp_in, dtf, a_f, cs, cs_last, causal


def _head_decay(cs_j, cst_ref, e, causal):
    lane = lax.broadcasted_iota(jnp.int32, (Q, LANE), 1)
    rolled = pltpu.roll(cs_j, HEAD, 1)
    own = (lane < HEAD) if e == 0 else (lane >= HEAD)
    col_b = jnp.where(own, cs_j, rolled)
    col_b = jnp.concatenate([col_b] * (Q // LANE), axis=1)
    row_b = cst_ref[e * HEAD:e * HEAD + 1, :]
    return jnp.where(causal, jnp.exp(jnp.minimum(col_b - row_b, 0.0)), 0.0)


def ssd_fwd(xbc, z_src, dt_src, dtb_f, alog_f, dsk_f, snw, name):
    L = xbc.shape[0]
    nc = L // Q

    def body(xbc_ref, z_ref, dt_ref, dtb_ref, alog_ref, dsk_ref, snw_ref, y_ref, yn_ref, sp_ref, s_ref, cs_ref, cst_ref, yd_ref):
        @pl.when(pl.program_id(0) == 0)
        def _():
            s_ref[...] = jnp.zeros((N_STATE, D_SSD), F32)

        xs, _, dtf, a_f, cs, cs_last, causal = _ssd_common(xbc_ref, dt_ref, dtb_ref, alog_ref, cs_ref)
        e_cs = jnp.exp(cs)
        xdt = xs * dtf
        zst = jnp.exp(cs_last - cs) * xdt
        sp_ref[0] = s_ref[...]
        lane = lax.broadcasted_iota(jnp.int32, (Q, LANE), 1)
        for g in range(2):
            gl = slice(g * 512, g * 512 + 512)
            b_g = xbc_ref[:, D_SSD + g * N_STATE:D_SSD + (g + 1) * N_STATE]
            c_g = xbc_ref[:, D_SSD + 2 * N_STATE + g * N_STATE:D_SSD + 2 * N_STATE + (g + 1) * N_STATE]
            s_prev = s_ref[:, gl]
            cb = _dot_nt(c_g, b_g)
            yd_ref[:, gl] = e_cs[:, gl] * _dot(c_g, s_prev)
            for j in range(4):
                tl = slice(g * 512 + j * LANE, g * 512 + (j + 1) * LANE)
                cs_j = cs[:, tl]
                cst_ref[...] = cs_j.T
                x_j = xdt[:, tl]
                o0 = _dot(cb * _head_decay(cs_j, cst_ref, 0, causal), x_j)
                o1 = _dot(cb * _head_decay(cs_j, cst_ref, 1, causal), x_j)
                yd_ref[:, tl] += jnp.where(lane < HEAD, o0, o1)
            s_ref[:, gl] = jnp.exp(cs_last[:, gl]) * s_prev + _dot_tn(b_g, zst[:, gl])
        y = yd_ref[...] + xs * dsk_ref[...]
        y_ref[...] = y
        yz = y * _silu(z_ref[...])
        r = lax.rsqrt(jnp.mean(yz * yz, axis=-1, keepdims=True) + 1e-6)
        yn_ref[...] = (yz * r * snw_ref[...]).astype(BF16)

    chunk = lambda w, c: pl.BlockSpec((Q, w), lambda i: (i, c))
    return pl.pallas_call(
        body, name=name, grid=(nc,),
        out_shape=[_sds((L, D_SSD), F32), _sds((L, D_SSD), BF16), _sds((nc, N_STATE, D_SSD), F32)],
        in_specs=[chunk(D_XBC, 0), chunk(D, 0), chunk(D, 0)] + [_const_spec((1, D))] * 4,
        out_specs=[chunk(D, 0), chunk(D, 0), pl.BlockSpec((1, N_STATE, D_SSD), lambda i: (i, 0, 0))],
        scratch_shapes=[pltpu.VMEM((N_STATE, D_SSD), F32), pltpu.VMEM((Q, D_SSD), F32), pltpu.VMEM((LANE, Q), F32),
                        pltpu.VMEM((Q, D_SSD), F32)],
        compiler_params=_cparams(("arbitrary",)))(xbc, z_src, dt_src, dtb_f, alog_f, dsk_f, snw)


def ssd_bwd(dysn, y, xbc, z_src, dt_src, s_prev_all, dtb_f, alog_f, dsk_f, snw, name):
    L = xbc.shape[0]
    nc = L // Q

    def body(dyn_ref, y_ref, xbc_ref, z_ref, dt_ref, sp_ref, dtb_ref, alog_ref, dsk_ref, snw_ref,
             dz_ref, ddt_ref, dxbc_ref, acc_ref, acc16_ref, ds_ref, cs_ref, cst_ref, dcs_ref, dx_ref):
        step = pl.program_id(0)

        @pl.when(step == 0)
        def _():
            ds_ref[...] = jnp.zeros((N_STATE, D_SSD), F32)
            acc_ref[...] = jnp.zeros((8, D), F32)

        z = z_ref[...]
        y = y_ref[...]
        sz = _sigmoid(z)
        siluz = z * sz
        yz = y * siluz
        r = lax.rsqrt(jnp.mean(yz * yz, axis=-1, keepdims=True) + 1e-6)
        n = yz * r
        dyn = dyn_ref[...]
        acc_ref[0:1, :] += _colsum(dyn * n)
        dn = dyn * snw_ref[...]
        dyz = r * (dn - n * jnp.mean(dn * n, axis=-1, keepdims=True))
        dy = dyz * siluz
        dz_ref[...] = (dyz * y * (sz * (1.0 + z * (1.0 - sz)))).astype(BF16)

        xs, sp_in, dtf, a_f, cs, cs_last, causal = _ssd_common(xbc_ref, dt_ref, dtb_ref, alog_ref, cs_ref)
        acc_ref[3:4, :] += _colsum(dy * xs)
        e_cs = jnp.exp(cs)
        xdt = xs * dtf
        dst = jnp.exp(cs_last - cs)
        zst = dst * xdt
        e_last = jnp.exp(cs_last)
        lane = lax.broadcasted_iota(jnp.int32, (Q, LANE), 1)
        ones = jnp.ones((Q, LANE), F32)
        dcs_last_parts = []
        for g in range(2):
            gl = slice(g * 512, g * 512 + 512)
            b_g = xbc_ref[:, D_SSD + g * N_STATE:D_SSD + (g + 1) * N_STATE]
            c_g = xbc_ref[:, D_SSD + 2 * N_STATE + g * N_STATE:D_SSD + 2 * N_STATE + (g + 1) * N_STATE]
            s_prev = sp_ref[0, :, gl]
            ds_g = ds_ref[:, gl]
            dy_g = dy[:, gl]
            cb = _dot_nt(c_g, b_g)
            y_off = e_cs[:, gl] * _dot(c_g, s_prev)
            edy = e_cs[:, gl] * dy_g
            d_c = _dot_nt(edy, s_prev)
            d_z = _dot(b_g, ds_g)
            d_b = _dot_nt(zst[:, gl], ds_g)
            t_g = d_z * zst[:, gl]
            dcs_ref[:, gl] = dy_g * y_off - t_g
            dx_ref[:, gl] = d_z * dst[:, gl]
            dcs_last_parts.append(_colsum(t_g) + _colsum(ds_g * s_prev) * e_last[:, gl])
            ds_ref[:, gl] = e_last[:, gl] * ds_g + _dot_tn(c_g, edy)
            dcb = jnp.zeros((Q, Q), F32)
            for j in range(4):
                tl = slice(g * 512 + j * LANE, g * 512 + (j + 1) * LANE)
                cs_j = cs[:, tl]
                cst_ref[...] = cs_j.T
                x_j = xdt[:, tl]
                dy_j = dy[:, tl]
                dx_j = jnp.zeros((Q, LANE), F32)
                dcs_j = jnp.zeros((Q, LANE), F32)
                for e in range(2):
                    own = (lane < HEAD) if e == 0 else (lane >= HEAD)
                    w_h = _head_decay(cs_j, cst_ref, e, causal)
                    g_h = cb * w_h
                    dy_m = jnp.where(own, dy_j, 0.0)
                    d_g = _dot_nt(dy_m, x_j)
                    dx_j = dx_j + _dot_tn(g_h, dy_m)
                    dcb = dcb + d_g * w_h
                    p_h = d_g * g_h
                    row_sums = _dot_exact(p_h, ones, 2, "b")
                    col_sums = _dot_exact(p_h, ones, 2, "b", (((0,), (0,)), ((), ())))
                    dcs_j = dcs_j + jnp.where(own, row_sums - col_sums, 0.0)
                dcs_ref[:, tl] += dcs_j * (1.0 / HEAD)
                dx_ref[:, tl] += dx_j
            d_c = d_c + _dot(dcb, b_g)
            d_b = d_b + _dot_tn(dcb, c_g)
            dxbc_ref[:, D_SSD + g * N_STATE:D_SSD + (g + 1) * N_STATE] = d_b
            dxbc_ref[:, D_SSD + 2 * N_STATE + g * N_STATE:D_SSD + 2 * N_STATE + (g + 1) * N_STATE] = d_c
        dcs_last = jnp.concatenate(dcs_last_parts, axis=1)
        anticausal = lax.broadcasted_iota(jnp.int32, (Q, Q), 0) <= lax.broadcasted_iota(jnp.int32, (Q, Q), 1)
        d_adt = _dot_exact(anticausal.astype(F32), dcs_ref[...], 3, "a") + dcs_last
        dx = dx_ref[...]
        acc_ref[2:3, :] += _colsum(d_adt * dtf) * a_f
        d_dtf = d_adt * a_f + dx * xs
        dxbc_ref[:, 0:D_SSD] = dx * dtf + dy * dsk_ref[...]
        d_raw = d_dtf * _sigmoid(sp_in)
        acc_ref[1:2, :] += _colsum(d_raw)
        head_of_lane = lax.broadcasted_iota(jnp.int32, (D_SSD, LANE), 0) // HEAD
        fold = (head_of_lane == lax.broadcasted_iota(jnp.int32, (D_SSD, LANE), 1)).astype(F32)
        ddt_ref[...] = _dot_exact(d_raw, fold, 2, "b").astype(BF16)

        @pl.when(step == nc - 1)
        def _():
            acc16_ref[...] = _dot_exact(acc_ref[...], fold, 3, "b")

    rchunk = lambda w, c: pl.BlockSpec((Q, w), lambda i: (nc - 1 - i, c))
    return pl.pallas_call(
        body, name=name, grid=(nc,),
        out_shape=[_sds((L, D_SSD), BF16), _sds((L, LANE), BF16), _sds((L, D_XBC), F32), _sds((8, D), F32), _sds((8, LANE), F32)],
        in_specs=[rchunk(D, 0), rchunk(D, 0), rchunk(D_XBC, 0), rchunk(D, 0), rchunk(D, 0),
                  pl.BlockSpec((1, N_STATE, D_SSD), lambda i: (nc - 1 - i, 0, 0))] + [_const_spec((1, D))] * 4,
        out_specs=[rchunk(D, 0), rchunk(LANE, 0), rchunk(D_XBC, 0), _const_spec((8, D)), _const_spec((8, LANE))],
        scratch_shapes=[pltpu.VMEM((N_STATE, D_SSD), F32), pltpu.VMEM((Q, D_SSD), F32), pltpu.VMEM((LANE, Q), F32),
                        pltpu.VMEM((Q, D_SSD), F32), pltpu.VMEM((Q, D_SSD), F32)],
        compiler_params=_cparams(("arbitrary",)))(dysn, y, xbc, z_src, dt_src, s_prev_all, dtb_f, alog_f, dsk_f, snw)


def _adamw_math(w, g, m, v):
    m_n = ADAM_B1 * m + (1.0 - ADAM_B1) * g
    v_n = ADAM_B2 * v + (1.0 - ADAM_B2) * jnp.square(g)
    c1 = 1.0 - ADAM_B1 ** ADAM_STEP
    c2 = 1.0 - ADAM_B2 ** ADAM_STEP
    return -ADAM_LR * ((m_n / c1) / (jnp.sqrt(v_n / c2) + ADAM_EPS) + ADAM_WD * w), m_n, v_n


def _sum_slots(p_ref):
    acc = p_ref[0].astype(F32)
    for s in range(1, p_ref.shape[0]):
        acc = acc + p_ref[s].astype(F32)
    return acc


def adamw_slots(w, slots, m, v, name):
    rows, cols = w.shape
    tc = 256

    def body(w_ref, s_ref, m_ref, v_ref, g_ref, d_ref, mo_ref, vo_ref):
        g_v = _sum_slots(s_ref)
        g_ref[...] = g_v
        d_ref[...], mo_ref[...], vo_ref[...] = _adamw_math(w_ref[...], g_v, m_ref[...], v_ref[...])

    spec = pl.BlockSpec((rows, tc), lambda i: (0, i))
    return pl.pallas_call(body, name=name, grid=(cols // tc,), out_shape=[_sds((rows, cols), F32)] * 4,
                          in_specs=[spec, pl.BlockSpec((slots.shape[0], rows, tc), lambda i: (0, 0, i)), spec, spec], out_specs=[spec] * 4,
                          compiler_params=_cparams(("parallel",)))(w, slots, m, v)


def adamw_many(ws, gs, ms, vs, name):
    n = len(ws)

    def body(*refs):
        for p in range(n):
            d_v, m_v, v_v = _adamw_math(refs[p][...], refs[n + p][...], refs[2 * n + p][...], refs[3 * n + p][...])
            refs[4 * n + p][...] = d_v
            refs[5 * n + p][...] = m_v
            refs[6 * n + p][...] = v_v

    vm = pl.BlockSpec(memory_space=pltpu.VMEM)
    out = pl.pallas_call(body, name=name, out_shape=[_sds(w.shape, F32) for w in ws] * 3, in_specs=[vm] * (4 * n),
                         out_specs=[vm] * (3 * n), compiler_params=_cparams())(*ws, *gs, *ms, *vs)
    return out[:n], out[n:2 * n], out[2 * n:]


def _pack_layout(shapes):
    row, layout = 0, []
    for rows, cols in shapes:
        chunks = []
        for c0 in range(0, cols, D):
            chunks.append((row, c0, min(D, cols - c0)))
            row += rows
        layout.append(chunks)
    return row, layout


def pack_rows(entries, name):
    arrays = [e[0] for e in entries]
    used, layout = _pack_layout([(e[2], e[0].shape[1]) for e in entries])
    total = -(-used // SUBLANES) * SUBLANES
    n = len(arrays)

    def body(*refs):
        o_ref = refs[n]
        o_ref[...] = jnp.zeros((total, D), F32)
        for p in range(n):
            _, first, rows = entries[p]
            for r0, c0, w in layout[p]:
                o_ref[r0:r0 + rows, 0:w] = refs[p][first:first + rows, c0:c0 + w]

    vm = pl.BlockSpec(memory_space=pltpu.VMEM)
    return pl.pallas_call(body, name=name, out_shape=_sds((total, D), F32), in_specs=[vm] * n, out_specs=vm,
                          compiler_params=_cparams())(*arrays)


def unpack_rows(packed, shapes):
    _, layout = _pack_layout(shapes)
    out = []
    for (rows, _), chunks in zip(shapes, layout):
        parts = [packed[..., r0:r0 + rows, 0:w] for r0, _, w in chunks]
        out.append(parts[0] if len(parts) == 1 else jnp.concatenate(parts, axis=-1))
    return out


def sum_slots_many(parts, name):
    n = len(parts)

    def body(*refs):
        for p in range(n):
            refs[n + p][...] = _sum_slots(refs[p])

    vm = pl.BlockSpec(memory_space=pltpu.VMEM)
    return pl.pallas_call(body, name=name, out_shape=[_sds(p.shape[1:], F32) for p in parts], in_specs=[vm] * n,
                          out_specs=[vm] * n, compiler_params=_cparams())(*parts)


def ada_mod(c_all, ada_w_shard, ada_b_cols, name):
    def body(c_ref, w_ref, b_ref, o_ref, ca_ref):
        ca = _silu(c_ref[...])
        ca_ref[...] = ca
        o_ref[...] = _dot(ca, w_ref[...]) + b_ref[...]

    vm = pl.BlockSpec(memory_space=pltpu.VMEM)
    return pl.pallas_call(body, name=name, out_shape=[_sds((N_DEV, ada_w_shard.shape[1]), F32), _sds((N_DEV, D), F32)],
                          in_specs=[vm, vm, vm], out_specs=[vm, vm], compiler_params=_cparams())(c_all, ada_w_shard, ada_b_cols)


def ada_wgrad(c_act_all, dmod_cols, name):
    def body(c_ref, d_ref, o_ref):
        o_ref[...] = _dot_tn_hi(c_ref[...], d_ref[...])

    vm = pl.BlockSpec(memory_space=pltpu.VMEM)
    return pl.pallas_call(body, name=name, out_shape=_sds((D, dmod_cols.shape[1]), F32), in_specs=[vm, vm], out_specs=vm,
                          compiler_params=_cparams())(c_act_all, dmod_cols)


def exchange(srcs, name, gather):
    n = len(srcs)
    gathers = [gather] * n if isinstance(gather, bool) else list(gather)
    shapes = [tuple(s.shape) if g else tuple(s.shape[1:]) for s, g in zip(srcs, gathers)]

    def body(*refs):
        src_refs, out_refs = refs[:n], refs[n:2 * n]
        send_sems, recv_sems, local_sems = refs[2 * n:]
        x, y, c = lax.axis_index("x"), lax.axis_index("y"), lax.axis_index("c")
        me = 4 * x + 2 * y + c

        def peer(k):
            bx, by, bc = (k >> 2) & 1, (k >> 1) & 1, k & 1
            px, py, pc = (x + bx) % 2, (y + by) % 2, (c + bc) % 2
            return (px, py, pc), 4 * px + 2 * py + pc

        def copy(a, k, landing):
            dev, idx = peer(k)
            return pltpu.make_async_remote_copy(
                src_ref=src_refs[a] if gathers[a] else src_refs[a].at[idx], dst_ref=out_refs[a].at[idx if landing else me],
                send_sem=send_sems.at[a, k - 1], recv_sem=recv_sems.at[a, k - 1],
                device_id=dev, device_id_type=pl.DeviceIdType.MESH)

        mine = [pltpu.make_async_copy(src_refs[a] if gathers[a] else src_refs[a].at[me], out_refs[a].at[me], local_sems.at[a])
                for a in range(n)]
        for cp in mine:
            cp.start()
        sends = [copy(a, k, False) for a in range(n) for k in range(1, N_DEV)]
        for cp in sends:
            cp.start()
        for a in range(n):
            for k in range(1, N_DEV):
                copy(a, k, True).wait_recv()
        for cp in sends:
            cp.wait_send()
        for cp in mine:
            cp.wait()

    hbm = pl.BlockSpec(memory_space=pl.ANY)
    return pl.pallas_call(
        body, name=name, out_shape=[_sds((N_DEV,) + shp, s.dtype) for shp, s in zip(shapes, srcs)], in_specs=[hbm] * n,
        out_specs=[hbm] * n,
        scratch_shapes=[pltpu.SemaphoreType.DMA((n, N_DEV - 1)), pltpu.SemaphoreType.DMA((n, N_DEV - 1)),
                        pltpu.SemaphoreType.DMA((n,))],
        compiler_params=pltpu.CompilerParams(has_side_effects=True))(*srcs)


def gather_two_level(src, name):
    def body(src_ref, out_ref, send_sems, recv_sems, local_sem):
        x, y, c = lax.axis_index("x"), lax.axis_index("y"), lax.axis_index("c")
        me, sibling = (x, y, c), (x, y, 1 - c)
        chips = [(1 - x, y), (x, 1 - y), (1 - x, 1 - y)]

        def slot(px, py, pc):
            return out_ref.at[4 * px + 2 * py + pc]

        def copy(k, block, to, src=None):
            return pltpu.make_async_remote_copy(
                src_ref=slot(*block) if src is None else src, dst_ref=slot(*block), send_sem=send_sems.at[k],
                recv_sem=recv_sems.at[k], device_id=to, device_id_type=pl.DeviceIdType.MESH)

        mine = pltpu.make_async_copy(src_ref, slot(*me), local_sem)
        mine.start()
        first = [copy(0, me, sibling, src=src_ref)]
        first += [copy(1 + j, me, (*chip, c), src=src_ref) for j, chip in enumerate(chips)]
        for cp in first:
            cp.start()
        passed = [copy(4 + j, (*chip, c), sibling) for j, chip in enumerate(chips)]
        for j, chip in enumerate(chips):
            copy(1 + j, (*chip, c), me).wait_recv()
            passed[j].start()
        copy(0, sibling, me).wait_recv()
        for j, chip in enumerate(chips):
            copy(4 + j, (*chip, 1 - c), me).wait_recv()
        for cp in first + passed:
            cp.wait_send()
        mine.wait()

    hbm = pl.BlockSpec(memory_space=pl.ANY)
    return pl.pallas_call(
        body, name=name, out_shape=_sds((N_DEV,) + tuple(src.shape), src.dtype), in_specs=[hbm], out_specs=hbm,
        scratch_shapes=[pltpu.SemaphoreType.DMA((N_DEV - 1,)), pltpu.SemaphoreType.DMA((N_DEV - 1,)), pltpu.SemaphoreType.DMA],
        compiler_params=pltpu.CompilerParams(has_side_effects=True))(src)


def _peer(k):
    x, y, c = lax.axis_index("x"), lax.axis_index("y"), lax.axis_index("c")
    px, py, pc = (x + ((k >> 2) & 1)) % 2, (y + ((k >> 1) & 1)) % 2, (c + (k & 1)) % 2
    return (px, py, pc), 4 * px + 2 * py + pc


def _my_slot():
    return 4 * lax.axis_index("x") + 2 * lax.axis_index("y") + lax.axis_index("c")


_HBM = pl.BlockSpec(memory_space=pltpu.HBM)
_SEM = pl.BlockSpec(memory_space=pltpu.SEMAPHORE)
_EFFECT = pltpu.SideEffectType.DATAFLOW_SIDE_EFFECTING


def exchange_start(srcs, name, gather):
    n = len(srcs)
    shapes = [tuple(s.shape) if gather else tuple(s.shape[1:]) for s in srcs]
    lands = [lax.empty((N_DEV,) + shp, s.dtype) for shp, s in zip(shapes, srcs)]

    def body(*refs):
        src_refs, land_refs = refs[:n], refs[n:2 * n]
        sems = refs[2 * n:4 * n]
        token = refs[-1]
        me = _my_slot()
        for a in range(n):
            for k in range(1, N_DEV):
                dev, idx = _peer(k)
                pltpu.make_async_remote_copy(
                    src_ref=src_refs[a] if gather else src_refs[a].at[idx], dst_ref=land_refs[a].at[me],
                    send_sem=sems[2 * a].at[k - 1], recv_sem=sems[2 * a + 1].at[k - 1],
                    device_id=dev, device_id_type=pl.DeviceIdType.MESH).start()
        token[...] = jnp.zeros_like(token)

    out_shape = ([pltpu.SemaphoreType.DMA((N_DEV - 1,))] * (2 * n) + [pltpu.HBM(s.shape, s.dtype) for s in srcs]
                 + [pltpu.HBM(l.shape, l.dtype) for l in lands] + [_sds((8, LANE), F32)])
    out = pl.pallas_call(
        body, name=name, out_shape=out_shape, in_specs=[_HBM] * (2 * n),
        out_specs=[_SEM] * (2 * n) + [_HBM] * (2 * n) + [pl.BlockSpec(memory_space=pltpu.VMEM)],
        input_output_aliases={i: 2 * n + i for i in range(2 * n)},
        compiler_params=pltpu.CompilerParams(has_side_effects=_EFFECT))(
            *[pltpu.with_memory_space_constraint(s, pltpu.HBM) for s in srcs],
            *[pltpu.with_memory_space_constraint(l, pltpu.HBM) for l in lands])
    parts = [(out[2 * a], out[2 * a + 1], out[2 * n + a], out[3 * n + a]) for a in range(n)]
    return parts, out[-1]


def exchange_wait(parts, after, name, gather):
    n = len(parts)

    def body(*refs):
        src_refs, land_refs = refs[:n], refs[n:2 * n]
        sems = refs[2 * n:4 * n]
        for a in range(n):
            for k in range(1, N_DEV):
                dev, idx = _peer(k)
                copy = pltpu.make_async_remote_copy(
                    src_ref=src_refs[a] if gather else src_refs[a].at[idx], dst_ref=land_refs[a].at[idx],
                    send_sem=sems[2 * a].at[k - 1], recv_sem=sems[2 * a + 1].at[k - 1],
                    device_id=dev, device_id_type=pl.DeviceIdType.MESH)
                copy.wait_send()
                copy.wait_recv()

    srcs = [p[2] for p in parts]
    lands = [p[3] for p in parts]
    sems = [s for p in parts for s in p[:2]]
    out = pl.pallas_call(
        body, name=name, out_shape=[pltpu.HBM(a.shape, a.dtype) for a in srcs + lands],
        in_specs=[_HBM] * (2 * n) + [_SEM] * (2 * n) + [pl.BlockSpec(memory_space=pl.ANY)], out_specs=[_HBM] * (2 * n),
        input_output_aliases={i: i for i in range(2 * n)},
        compiler_params=pltpu.CompilerParams(has_side_effects=_EFFECT))(*srcs, *lands, *sems, after)
    return out[n:]


def _cols_to_slabs(g):
    r, c = g.shape
    return g.reshape(r, N_DEV, c // N_DEV).transpose(1, 0, 2)


def _slabs_to_cols(s):
    _, r, cs = s.shape
    return s.transpose(1, 0, 2).reshape(r, N_DEV * cs)


def _rep_heads(v):
    return jnp.repeat(v.reshape(N_HEADS), HEAD).reshape(1, D_SSD)


def local_fwd_bwd(x, target, mod, get_w, put_grad, small):
    n1w, n2w, fnw = small["norm1_w"], small["norm2_w"], small["final_norm_w"]
    dtb_f, alog_f, dsk_f = _rep_heads(small["dt_bias"]), _rep_heads(small["a_log"]), _rep_heads(small["d_skip"])
    snw = small["ssd_norm_w"]

    def after(v, token):
        return v + token[0:1, 0:1]

    w_in = get_w("w_in", mod)
    h1, proj_zx, proj_dt, proj_cf = in_proj(x, mod, n1w, w_in["w_full"], w_in["w_dt_rep"], w_in["w_cf"], "norm1_in_proj")
    xbc = conv_silu_fwd(proj_zx, D_SSD // CB, D_XBC, small["ssd_conv_w"], small["ssd_conv_b"], "ssd_conv")
    y, ysn, s_prev = ssd_fwd(xbc, proj_zx, proj_dt, dtb_f, alog_f, dsk_f, snw, "ssd_scan")
    uc = conf_conv_fwd(proj_cf, 0, D_CONF // CB, small["conf_conv_w"], small["conf_conv_b"], "conf_conv")
    w_out = get_w("w_out", uc)
    mix, u, x1, h2 = mixer_out(ysn, uc, small["conf_ln_w"], small["conf_ln_b"], w_out, x, mod, n2w, "out_proj_norm2")
    w_up_t = get_w("w_up", h2)
    up = mm_nt([(h2, w_up_t, 0)], "up_proj")
    act = ffn_conv_fwd(up, small["ffn_conv_w"], small["ffn_conv_b"], "ffn_conv")
    w_down = get_w("w_down", act)
    dx2, dff, dact, acc_f = final_loss(act, w_down, x1, mod, fnw, target, "down_proj_loss")

    token = put_grad("w_down", mm_tn(act, dff, "wgrad_down"))
    dupg, dupv, dwg, dwv = ffn_conv_bwd(up, small["ffn_conv_w"], after(small["ffn_conv_b"], token), dact, "ffn_conv_bwd")
    token = put_grad("w_up", jnp.concatenate([mm_tn(dupg, h2, "wgrad_up_gate"), mm_tn(dupv, h2, "wgrad_up_val")], axis=0))
    dx1, dmix, acc_2 = norm_mod_bwd([(dupg, w_up_t, 0), (dupv, w_up_t, 1)], x1, dx2, mod, after(n2w, token), 3, "norm2_bwd",
                                    mix=mix, gate_row=2)

    token = put_grad("w_out", jnp.concatenate([mm_tn(ysn, dmix, "wgrad_out_ssd"), mm_tn(u, dmix, "wgrad_out_conf")], axis=0))
    dysn, duc, acc_ln = mixer_out_bwd(dmix, w_out, uc, after(small["conf_ln_w"], token), small["conf_ln_b"], "out_proj_bwd")
    dcfa, dcfg, dw_cc = conf_conv_bwd(proj_cf, 0, D_CONF // CB, small["conf_conv_w"], duc, "conf_conv_bwd")
    dz, ddt, dxbc_post, acc_s, acc_s16 = ssd_bwd(dysn, y, xbc, proj_zx, proj_dt, s_prev, dtb_f, alog_f, dsk_f, snw,
                                                 "ssd_scan_bwd")
    dxbc, dw_sc = conv_silu_bwd(proj_zx, D_SSD // CB, D_XBC, small["ssd_conv_w"], small["ssd_conv_b"], dxbc_post, "ssd_conv_bwd")
    token = put_grad("w_in", jnp.concatenate(
        [mm_tn(dz, h1, "wgrad_in_z"), mm_tn(dxbc, h1, "wgrad_in_xbc"), mm_tn(ddt, h1, "wgrad_in_dt")[:N_HEADS],
         mm_tn(dcfa, h1, "wgrad_in_cfa"), mm_tn(dcfg, h1, "wgrad_in_cfg")], axis=0))
    dh1_pairs = [(dz, w_in["w_full"], 0), (ddt, w_in["w_dt16"], 0), (dcfa, w_in["w_cf"], 0), (dcfg, w_in["w_cf"], 1),
                 (dxbc, w_in["w_xbc"], 0)]
    grad_x, acc_1 = norm_mod_bwd(dh1_pairs, x, dx1, mod, after(n1w, token), 0, "norm1_bwd")

    small_accs = dict(acc_1=acc_1, acc_2=acc_2, acc_f=acc_f, acc_ln=acc_ln, acc_s=acc_s, acc_s16=acc_s16, dw_sc=dw_sc,
                      dw_cc=dw_cc, dwg=dwg, dwv=dwv)
    return grad_x, small_accs


def kernel(x, c, ada_w, ada_b, norm1_w, w_in, ssd_conv_w, ssd_conv_b, dt_bias, a_log, d_skip, ssd_norm_w, conf_conv_w, conf_conv_b, conf_ln_w, conf_ln_b, w_out, norm2_w, w_up, ffn_conv_w, ffn_conv_b, w_down, final_norm_w, loss_target, m_ada_w, m_ada_b, m_norm1_w, m_w_in, m_ssd_conv_w, m_ssd_conv_b, m_dt_bias, m_a_log, m_d_skip, m_ssd_norm_w, m_conf_conv_w, m_conf_conv_b, m_conf_ln_w, m_conf_ln_b, m_w_out, m_norm2_w, m_w_up, m_ffn_conv_w, m_ffn_conv_b, m_w_down, m_final_norm_w, v_ada_w, v_ada_b, v_norm1_w, v_w_in, v_ssd_conv_w, v_ssd_conv_b, v_dt_bias, v_a_log, v_d_skip, v_ssd_norm_w, v_conf_conv_w, v_conf_conv_b, v_conf_ln_w, v_conf_ln_b, v_w_out, v_norm2_w, v_w_up, v_ffn_conv_w, v_ffn_conv_b, v_w_down, v_final_norm_w):
    me = 4 * lax.axis_index("x") + 2 * lax.axis_index("y") + lax.axis_index("c")
    weights = dict(ada_w=ada_w, ada_b=ada_b, norm1_w=norm1_w, w_in=w_in, ssd_conv_w=ssd_conv_w, ssd_conv_b=ssd_conv_b,
                   dt_bias=dt_bias, a_log=a_log, d_skip=d_skip, ssd_norm_w=ssd_norm_w, conf_conv_w=conf_conv_w,
                   conf_conv_b=conf_conv_b, conf_ln_w=conf_ln_w, conf_ln_b=conf_ln_b, w_out=w_out, norm2_w=norm2_w, w_up=w_up,
                   ffn_conv_w=ffn_conv_w, ffn_conv_b=ffn_conv_b, w_down=w_down, final_norm_w=final_norm_w)
    moms_m = dict(ada_w=m_ada_w, ada_b=m_ada_b, norm1_w=m_norm1_w, w_in=m_w_in, ssd_conv_w=m_ssd_conv_w, ssd_conv_b=m_ssd_conv_b,
                  dt_bias=m_dt_bias, a_log=m_a_log, d_skip=m_d_skip, ssd_norm_w=m_ssd_norm_w, conf_conv_w=m_conf_conv_w,
                  conf_conv_b=m_conf_conv_b, conf_ln_w=m_conf_ln_w, conf_ln_b=m_conf_ln_b, w_out=m_w_out, norm2_w=m_norm2_w,
                  w_up=m_w_up, ffn_conv_w=m_ffn_conv_w, ffn_conv_b=m_ffn_conv_b, w_down=m_w_down, final_norm_w=m_final_norm_w)
    moms_v = dict(ada_w=v_ada_w, ada_b=v_ada_b, norm1_w=v_norm1_w, w_in=v_w_in, ssd_conv_w=v_ssd_conv_w, ssd_conv_b=v_ssd_conv_b,
                  dt_bias=v_dt_bias, a_log=v_a_log, d_skip=v_d_skip, ssd_norm_w=v_ssd_norm_w, conf_conv_w=v_conf_conv_w,
                  conf_conv_b=v_conf_conv_b, conf_ln_w=v_conf_ln_w, conf_ln_b=v_conf_ln_b, w_out=v_w_out, norm2_w=v_norm2_w,
                  w_up=v_w_up, ffn_conv_w=v_ffn_conv_w, ffn_conv_b=v_ffn_conv_b, w_down=v_w_down, final_norm_w=v_final_norm_w)
    names = list(weights)

    def to2d(a):
        return a[0] if a.ndim == 3 else a.reshape(1, -1)

    big = ("w_in", "w_out", "w_up", "w_down")

    c_all, scw_all, ccw_all, fcw_all = exchange([c.reshape(8, LANE), ssd_conv_w[0], conf_conv_w[0], ffn_conv_w[0]],
                                                "gather_small", gather=True)
    c_all = c_all.reshape(N_DEV, D)

    ada_cols = ada_w.shape[2]
    ada_b_cols = lax.dynamic_slice(ada_b, (0, me * ada_cols), (1, ada_cols))
    mod_cols, c_act_all = ada_mod(c_all, ada_w[0], ada_b_cols, "ada_mod")
    mod_parts, = exchange([jnp.pad(mod_cols, ((0, 0), (0, D - ada_cols))).reshape(N_DEV, 8, LANE)], "scatter_mod", gather=False)
    mod = mod_parts.reshape(N_DEV, D)[:, :ada_cols].reshape(6, D)
    mod = jnp.pad(mod, ((0, 2), (0, 0)))

    def rows_of(a):
        return jnp.swapaxes(a, 1, 2)[0] if a.shape[2] != D else a[0]

    shards, mod = lax.optimization_barrier(([rows_of(weights[n]).astype(BF16) for n in big], mod))
    w_in_slabs = gather_two_level(shards[0], "gather_w_in")
    later, w_in_slabs = lax.optimization_barrier((shards[1:], w_in_slabs))
    gather_parts, token = exchange_start(later, "gather_weights_start", gather=True)
    mod = mod + token[0:1, 0:1]

    small = {n: to2d(weights[n]) for n in names if n not in ("ada_w",) + big}
    small["ssd_conv_w"] = _slabs_to_cols(scw_all)
    small["conf_conv_w"] = _slabs_to_cols(ccw_all)
    small["ffn_conv_w"] = _slabs_to_cols(fcw_all)

    def with_own(landed, own):
        return lax.dynamic_update_slice(landed, own[None], (me,) + (0,) * own.ndim)

    def get_w(n, after):
        if n == "w_in":
            slabs = w_in_slabs
        else:
            a = big.index(n)
            landed, = exchange_wait([gather_parts[a - 1]], after, "gather_" + n + "_wait", gather=True)
            slabs = with_own(landed, shards[a])
        full = slabs.reshape(N_DEV * slabs.shape[1], D)
        if n != "w_in":
            return full
        w_dt = full[D_SSD + D_XBC:D_SSD + D_XBC + N_HEADS]
        return dict(w_full=full, w_xbc=full[D_SSD:D_SSD + D_XBC], w_cf=full[D_SSD + D_XBC + N_HEADS:],
                    w_dt_rep=jnp.repeat(w_dt, HEAD, axis=0), w_dt16=jnp.pad(w_dt, ((0, LANE - N_HEADS), (0, 0))))

    scatter_parts, sent = {}, {}

    def put_grad(n, g):
        sent[n] = g.reshape(N_DEV, g.shape[0] // N_DEV, g.shape[1]).astype(BF16)
        (scatter_parts[n],), token = exchange_start([sent[n]], "scatter_" + n + "_start", gather=False)
        return token

    grad_x, accs = local_fwd_bwd(x[0], loss_target[0], mod, get_w, put_grad, small)
    loss = lax.psum(0.5 / D * jnp.sum(accs["acc_f"][2:3]), ("x", "y", "c"))

    grads, delta, new_m, new_v = {}, {}, {}, {}

    def finish(ns, after, name):
        landed = exchange_wait([scatter_parts[n] for n in ns], after, name, gather=False)
        for n, slots in zip(ns, landed):
            slots = with_own(slots, lax.dynamic_index_in_dim(sent[n], me, 0, keepdims=False))
            out = adamw_slots(rows_of(weights[n]), slots, rows_of(moms_m[n]), rows_of(moms_v[n]), "adamw_" + n)
            if weights[n].shape[2] != D:
                out = [jnp.swapaxes(o, 0, 1) for o in out]
            grads[n], delta[n], new_m[n], new_v[n] = out

    finish(big[1:], grad_x, "scatter_grads_wait")

    accs = dict(zip(accs, lax.optimization_barrier((list(accs.values()), [new_v[n] for n in big[1:]]))[0]))
    rep = (("acc_1", 0, 3), ("acc_2", 0, 4), ("acc_f", 0, 2), ("acc_ln", 0, 2), ("acc_s", 0, 1), ("acc_s16", 1, 3),
           ("dw_sc", K_SSD, 1), ("dw_cc", K_CONF, 1), ("dwg", K_FFN, 1), ("dwv", K_FFN, 1))
    shapes = [(rows, accs[k].shape[1]) for k, _, rows in rep]
    conv_slabs = [_cols_to_slabs(accs["dw_sc"][:K_SSD]), _cols_to_slabs(accs["dw_cc"][:K_CONF]),
                  _cols_to_slabs(jnp.concatenate([accs["dwg"][:K_FFN], accs["dwv"][:K_FFN]], axis=1))]
    packed = pack_rows([(accs[k], first, rows) for k, first, rows in rep], "pack_small_grads")
    landed = exchange([packed] + conv_slabs, "exchange_small_grads", gather=[True, False, False, False])
    packed_red, g_scw, g_ccw, g_fcw = sum_slots_many(landed, "sum_small_grads")
    a1_all, a2_all, af_all = unpack_rows(landed[0], shapes)[:3]
    r1, r2, rf, rln, rs, r16, rscb, rccb, rfbg, rfbv = unpack_rows(packed_red, shapes)

    def mod_rows(a1, a2, af):
        return jnp.concatenate([a1[..., 0:2, :], a2[..., 3:4, :], a2[..., 0:2, :], af[..., 1:2, :]], axis=-2)

    dmod_all = mod_rows(a1_all, a2_all, af_all).reshape(N_DEV, 6 * D)
    grads["ada_w"] = ada_wgrad(c_act_all, lax.dynamic_slice(dmod_all, (0, me * ada_cols), (N_DEV, ada_cols)), "ada_wgrad")
    grads.update(
        ada_b=mod_rows(r1, r2, rf).reshape(1, 6 * D), norm1_w=r1[2:3], ssd_conv_w=g_scw, ssd_conv_b=rscb,
        dt_bias=r16[0:1, :N_HEADS], a_log=r16[1:2, :N_HEADS], d_skip=r16[2:3, :N_HEADS], ssd_norm_w=rs,
        conf_conv_w=g_ccw, conf_conv_b=rccb, conf_ln_w=rln[0:1], conf_ln_b=rln[1:2], norm2_w=r2[2:3],
        ffn_conv_w=g_fcw, ffn_conv_b=jnp.concatenate([rfbg, rfbv], axis=1), final_norm_w=rf[0:1])

    rest = [n for n in names if n not in big]
    d_l, m_l, v_l = adamw_many([to2d(weights[n]) for n in rest], [grads[n] for n in rest], [to2d(moms_m[n]) for n in rest],
                               [to2d(moms_v[n]) for n in rest], "adamw_small")
    for n, dd, mm, vv in zip(rest, d_l, m_l, v_l):
        delta[n], new_m[n], new_v[n] = dd, mm, vv
    finish(big[:1], d_l[0], "scatter_w_in_wait")
    shape_of = lambda d_: {n: d_[n].reshape(weights[n].shape) for n in names}
    grads, delta, new_m, new_v = shape_of(grads), shape_of(delta), shape_of(new_m), shape_of(new_v)
    return (loss, grad_x[None], *[grads[n] for n in names], *[delta[n] for n in names], *[new_m[n] for n in names],
            *[new_v[n] for n in names])
```

```python
import functools

import jax
import jax.numpy as jnp
from jax import lax
from jax.experimental import pallas as pl
from jax.experimental.pallas import tpu as pltpu

F32 = jnp.float32
BF16 = jnp.bfloat16
HI = lax.Precision.HIGHEST

N_DEV = 8
D = 1024
D_SSD = 1024
HEAD = 64
N_HEADS = 16
N_STATE = 128
D_XBC = 1536
D_CONF = 1024
D_FF = 2816
K_SSD, K_CONF, K_FFN = 4, 31, 3
D_INP = 5632
LANE = 128
TR = 256
TM = 512
Q = 256
CB = 256
TC = 1024
VMEM_LIMIT = 56 * 1024 * 1024

ADAM_LR, ADAM_B1, ADAM_B2, ADAM_EPS, ADAM_WD, ADAM_STEP = 0.001, 0.9, 0.999, 1e-08, 0.01, 10


def _cparams(sem=None):
    return pltpu.CompilerParams(vmem_limit_bytes=VMEM_LIMIT, dimension_semantics=sem)


def _sds(shape, dtype):
    return jax.ShapeDtypeStruct(shape, dtype)


def _sigmoid(x):
    return 1.0 / (1.0 + jnp.exp(-x))


def _silu(x):
    return x * _sigmoid(x)


def _dsilu(x):
    s = _sigmoid(x)
    return s * (1.0 + x * (1.0 - s))


def _softplus(x):
    return jnp.maximum(x, 0.0) + jnp.log(1.0 + jnp.exp(-jnp.abs(x)))


def _dot(a, b):
    return jnp.dot(a.astype(BF16), b.astype(BF16), preferred_element_type=F32)


def _dot_nt(a, b):
    return lax.dot_general(a.astype(BF16), b.astype(BF16), (((1,), (1,)), ((), ())), preferred_element_type=F32)


def _dot_tn(a, b):
    return lax.dot_general(a.astype(BF16), b.astype(BF16), (((0,), (0,)), ((), ())), preferred_element_type=F32)


def _bf16_terms(a, terms):
    parts, rem = [], a
    for t in range(terms):
        p = rem.astype(BF16)
        parts.append(p)
        if t + 1 < terms:
            rem = rem - p.astype(F32)
    return parts


def _dot_exact(a, b, terms, exact, dims=(((1,), (0,)), ((), ()))):
    if exact == "a":
        a_b = a.astype(BF16)
        outs = [lax.dot_general(a_b, p, dims, preferred_element_type=F32) for p in _bf16_terms(b, terms)]
    else:
        b_b = b.astype(BF16)
        outs = [lax.dot_general(p, b_b, dims, preferred_element_type=F32) for p in _bf16_terms(a, terms)]
    acc = outs[-1]
    for o in reversed(outs[:-1]):
        acc = acc + o
    return acc


def _dot_tn_hi(a, b):
    return lax.dot_general(a, b, (((0,), (0,)), ((), ())), precision=HI, preferred_element_type=F32)


def _colsum(x):
    return jnp.sum(x, axis=0, keepdims=True)


def _const_spec(shape):
    return pl.BlockSpec(shape, lambda *_: (0,) * len(shape))


def _col_tile(n):
    for t in (1408, 1024, 768, 512, 256, 128):
        if n % t == 0 and t <= n:
            return t
    return n


def mm_nt(pairs, name):
    L = pairs[0][0].shape[0]
    K = pairs[0][1].shape[0]
    tk = _col_tile(K)
    n = len(pairs)

    def body(*refs):
        o_ref = refs[-1]
        acc = None
        for p in range(n):
            t = lax.dot_general(refs[2 * p][...], refs[2 * p + 1][...], (((1,), (1,)), ((), ())),
                                preferred_element_type=F32)
            acc = t if acc is None else acc + t
        o_ref[...] = acc

    in_specs, args = [], []
    for a, w, cb in pairs:
        in_specs += [pl.BlockSpec((TM, a.shape[1]), lambda j, i: (i, 0)),
                     pl.BlockSpec((tk, a.shape[1]), functools.partial(lambda j, i, cb: (j, cb), cb=cb))]
        args += [a, w]
    return pl.pallas_call(
        body, name=name, grid=(K // tk, L // TM), out_shape=_sds((L, K), F32), in_specs=in_specs,
        out_specs=pl.BlockSpec((TM, tk), lambda j, i: (i, j)),
        compiler_params=_cparams(("parallel", "parallel")))(*args)


def mm_tn(a, g, name):
    L, M = a.shape
    N = g.shape[1]
    tn = _col_tile(N) if N > 1024 else N
    if M * tn * 4 > 12 * 1024 * 1024:
        tn = 512
    tl = 512 if L % 512 == 0 else TR
    nl = L // tl

    def body(a_ref, g_ref, o_ref, acc_ref):
        @pl.when(pl.program_id(1) == 0)
        def _():
            acc_ref[...] = jnp.zeros((M, tn), F32)

        acc_ref[...] += lax.dot_general(a_ref[...], g_ref[...], (((0,), (0,)), ((), ())), preferred_element_type=F32)

        @pl.when(pl.program_id(1) == nl - 1)
        def _():
            o_ref[...] = acc_ref[...].astype(BF16)

    return pl.pallas_call(
        body, name=name, grid=(N // tn, nl), out_shape=_sds((M, N), BF16),
        in_specs=[pl.BlockSpec((tl, M), lambda j, l: (l, 0)), pl.BlockSpec((tl, tn), lambda j, l: (l, j))],
        out_specs=pl.BlockSpec((M, tn), lambda j, l: (0, j)), scratch_shapes=[pltpu.VMEM((M, tn), F32)],
        compiler_params=_cparams(("parallel", "arbitrary")))(a, g)


def mm_tn_stack(a_list, g, name):
    L, M = a_list[0].shape
    N = g.shape[1]
    n = len(a_list)
    tl = 512 if L % 512 == 0 else TR
    nl = L // tl

    def body(*refs):
        a_refs, g_ref, o_ref, acc_ref = refs[:n], refs[n], refs[n + 1], refs[n + 2]
        j, l = pl.program_id(0), pl.program_id(1)

        @pl.when(l == 0)
        def _():
            acc_ref[...] = jnp.zeros((M, N), F32)

        for p in range(n):
            @pl.when(j == p)
            def _(p=p):
                acc_ref[...] += lax.dot_general(a_refs[p][...], g_ref[...], (((0,), (0,)), ((), ())), preferred_element_type=F32)

        @pl.when(l == nl - 1)
        def _():
            o_ref[...] = acc_ref[...].astype(BF16)

    a_specs = [pl.BlockSpec((tl, M), functools.partial(lambda j, l, p: (jnp.where(j == p, l, 0), 0), p=p)) for p in range(n)]
    return pl.pallas_call(
        body, name=name, grid=(n, nl), out_shape=_sds((n * M, N), BF16),
        in_specs=a_specs + [pl.BlockSpec((tl, N), lambda j, l: (l, 0))],
        out_specs=pl.BlockSpec((M, N), lambda j, l: (j, 0)), scratch_shapes=[pltpu.VMEM((M, N), F32)],
        compiler_params=_cparams(("arbitrary", "arbitrary")))(*a_list, g)


def mm_tn_concat(pieces, g, name):
    L = g.shape[0]
    N = g.shape[1]
    n = len(pieces)
    offsets = [sum(r for _, r in pieces[:p]) for p in range(n + 1)]
    tl = 512 if L % 512 == 0 else TR
    nl = L // tl

    def body(*refs):
        a_refs, g_ref, o_ref, acc_ref = refs[:n], refs[n], refs[n + 1], refs[n + 2]
        l = pl.program_id(0)

        @pl.when(l == 0)
        def _():
            acc_ref[...] = jnp.zeros((offsets[-1], N), F32)

        g_v = g_ref[...]
        for p in range(n):
            t = lax.dot_general(a_refs[p][...], g_v, (((0,), (0,)), ((), ())), preferred_element_type=F32)
            acc_ref[offsets[p]:offsets[p + 1], :] += t[:pieces[p][1], :]

        @pl.when(l == nl - 1)
        def _():
            o_ref[...] = acc_ref[...].astype(BF16)

    return pl.pallas_call(
        body, name=name, grid=(nl,), out_shape=_sds((offsets[-1], N), BF16),
        in_specs=[pl.BlockSpec((tl, a.shape[1]), lambda l: (l, 0)) for a, _ in pieces] + [pl.BlockSpec((tl, N), lambda l: (l, 0))],
        out_specs=_const_spec((offsets[-1], N)), scratch_shapes=[pltpu.VMEM((offsets[-1], N), F32)],
        compiler_params=_cparams(("arbitrary",)))(*[a for a, _ in pieces], g)


def _row_spec(width=D):
    return pl.BlockSpec((TR, width), lambda i: (i, 0))


def in_proj(x, mod, n1w, w_full, w_dt_rep, w_cf, name):
    L = x.shape[0]
    n_zx = D_SSD + D_XBC

    def body(x_ref, mod_ref, w_ref, wzx_ref, wdt_ref, wcf_ref, h_ref, zx_ref, dt_ref, cf_ref):
        xin = x_ref[...]
        r = lax.rsqrt(jnp.mean(xin * xin, axis=-1, keepdims=True) + 1e-6)
        h = ((xin * r * w_ref[...]) * (1.0 + mod_ref[1:2, :]) + mod_ref[0:1, :]).astype(BF16)
        h_ref[...] = h
        nt = (((1,), (1,)), ((), ()))
        zx_ref[...] = lax.dot_general(h, wzx_ref[...], nt, preferred_element_type=F32)
        dt_ref[...] = lax.dot_general(h, wdt_ref[...], nt, preferred_element_type=F32)
        cf_ref[...] = lax.dot_general(h, wcf_ref[...], nt, preferred_element_type=F32)

    row = lambda w: pl.BlockSpec((TM, w), lambda i: (i, 0))
    return pl.pallas_call(
        body, name=name, grid=(L // TM,),
        out_shape=[_sds((L, D), BF16), _sds((L, n_zx), F32), _sds((L, D_SSD), F32), _sds((L, 2 * D_CONF), F32)],
        in_specs=[row(D), _const_spec((8, D)), _const_spec((1, D)), _const_spec((n_zx, D)), _const_spec((D_SSD, D)),
                  _const_spec((2 * D_CONF, D))],
        out_specs=[row(D), row(n_zx), row(D_SSD), row(2 * D_CONF)],
        compiler_params=_cparams(("parallel",)))(x, mod, n1w, w_full, w_dt_rep, w_cf)


def mixer_out(ysn, uc, lnw, lnb, w_out, x, mod, n2w, name):
    L = x.shape[0]

    def body(ysn_ref, uc_ref, lnw_ref, lnb_ref, wo_ref, x_ref, mod_ref, n2w_ref, mix_ref, u_ref, x1_ref, h2_ref):
        uc_v = uc_ref[...]
        mu = jnp.mean(uc_v, axis=-1, keepdims=True)
        var = jnp.mean(jnp.square(uc_v - mu), axis=-1, keepdims=True)
        u = _silu((uc_v - mu) * lax.rsqrt(var + 1e-5) * lnw_ref[...] + lnb_ref[...]).astype(BF16)
        u_ref[...] = u
        mix = (jnp.dot(ysn_ref[...], wo_ref[0:D_SSD, :], preferred_element_type=F32)
               + jnp.dot(u, wo_ref[D_SSD:D_SSD + D_CONF, :], preferred_element_type=F32))
        mix_ref[...] = mix
        x1 = x_ref[...] + mod_ref[2:3, :] * mix
        x1_ref[...] = x1
        r = lax.rsqrt(jnp.mean(x1 * x1, axis=-1, keepdims=True) + 1e-6)
        h2_ref[...] = ((x1 * r * n2w_ref[...]) * (1.0 + mod_ref[4:5, :]) + mod_ref[3:4, :]).astype(BF16)

    return pl.pallas_call(
        body, name=name, grid=(L // TR,),
        out_shape=[_sds((L, D), F32), _sds((L, D_CONF), BF16), _sds((L, D), F32), _sds((L, D), BF16)],
        in_specs=[_row_spec(), _row_spec(), _const_spec((1, D)), _const_spec((1, D)), _const_spec((D_SSD + D_CONF, D)), _row_spec(),
                  _const_spec((8, D)), _const_spec((1, D))],
        out_specs=[_row_spec()] * 4, compiler_params=_cparams(("parallel",)))(ysn, uc, lnw, lnb, w_out, x, mod, n2w)


def mixer_out_bwd(dmix, w_out, uc, lnw, lnb, name):
    L = uc.shape[0]

    def body(dm_ref, wo_ref, u_ref, w_ref, b_ref, dy_ref, o_ref, acc_ref):
        @pl.when(pl.program_id(0) == 0)
        def _():
            acc_ref[...] = jnp.zeros((8, D), F32)

        nt = (((1,), (1,)), ((), ()))
        dm = dm_ref[...]
        dy_ref[...] = lax.dot_general(dm, wo_ref[0:D_SSD, :], nt, preferred_element_type=F32)
        du = lax.dot_general(dm, wo_ref[D_SSD:D_SSD + D_CONF, :], nt, preferred_element_type=F32)
        u = u_ref[...]
        mu = jnp.mean(u, axis=-1, keepdims=True)
        rl = lax.rsqrt(jnp.mean(jnp.square(u - mu), axis=-1, keepdims=True) + 1e-5)
        n = (u - mu) * rl
        v = n * w_ref[...] + b_ref[...]
        dv = du * _dsilu(v)
        acc_ref[0:1, :] += _colsum(dv * n)
        acc_ref[1:2, :] += _colsum(dv)
        dn = dv * w_ref[...]
        o_ref[...] = rl * (dn - jnp.mean(dn, axis=-1, keepdims=True) - n * jnp.mean(dn * n, axis=-1, keepdims=True))

    return pl.pallas_call(body, name=name, grid=(L // TR,), out_shape=[_sds((L, D), F32), _sds((L, D), F32), _sds((8, D), F32)],
                          in_specs=[_row_spec(), _const_spec((D_SSD + D_CONF, D)), _row_spec(), _const_spec((1, D)),
                                    _const_spec((1, D))],
                          out_specs=[_row_spec(), _row_spec(), _const_spec((8, D))],
                          compiler_params=_cparams(("arbitrary",)))(dmix, w_out, uc, lnw, lnb)


def final_loss(act, w_down, x1, mod, fw, target, name):
    L = act.shape[0]

    def body(act_ref, wd_ref, x1_ref, mod_ref, fw_ref, t_ref, dx_ref, dff_ref, dact_ref, acc_ref):
        @pl.when(pl.program_id(0) == 0)
        def _():
            acc_ref[...] = jnp.zeros((8, D), F32)

        ff_v = jnp.dot(act_ref[...], wd_ref[...], preferred_element_type=F32)
        g2 = mod_ref[5:6, :]
        x2 = x1_ref[...] + g2 * ff_v
        r = lax.rsqrt(jnp.mean(x2 * x2, axis=-1, keepdims=True) + 1e-6)
        n = x2 * r
        err = n * fw_ref[...] - t_ref[...]
        dy = err * (1.0 / D)
        dn = dy * fw_ref[...]
        dx2 = r * (dn - n * jnp.mean(dn * n, axis=-1, keepdims=True))
        acc_ref[0:1, :] += _colsum(dy * n)
        acc_ref[1:2, :] += _colsum(dx2 * ff_v)
        acc_ref[2:3, :] += _colsum(err * err)
        dx_ref[...] = dx2
        dff = (dx2 * g2).astype(BF16)
        dff_ref[...] = dff
        dact_ref[...] = lax.dot_general(dff, wd_ref[...], (((1,), (1,)), ((), ())), preferred_element_type=F32)

    return pl.pallas_call(
        body, name=name, grid=(L // TR,),
        out_shape=[_sds((L, D), F32), _sds((L, D), BF16), _sds((L, D_FF), F32), _sds((8, D), F32)],
        in_specs=[_row_spec(D_FF), _const_spec((D_FF, D)), _row_spec(), _const_spec((8, D)), _const_spec((1, D)), _row_spec()],
        out_specs=[_row_spec(), _row_spec(), _row_spec(D_FF), _const_spec((8, D))],
        compiler_params=_cparams(("arbitrary",)))(act, w_down, x1, mod, fw, target)


def norm_mod_bwd(dh_pairs, xin, dres, mod, w, shift_row, name, mix=None, gate_row=None):
    L = xin.shape[0]
    has_mix = mix is not None
    n_pairs = len(dh_pairs)

    def body(*refs):
        pair_refs, refs = refs[:2 * n_pairs], refs[2 * n_pairs:]
        if has_mix:
            x_ref, dres_ref, mod_ref, w_ref, mix_ref, dx_ref, dmix_ref, acc_ref = refs
        else:
            x_ref, dres_ref, mod_ref, w_ref, dx_ref, acc_ref = refs

        @pl.when(pl.program_id(0) == 0)
        def _():
            acc_ref[...] = jnp.zeros((8, D), F32)

        dh_v = None
        for p in range(n_pairs):
            t = jnp.dot(pair_refs[2 * p][...], pair_refs[2 * p + 1][...], preferred_element_type=F32)
            dh_v = t if dh_v is None else dh_v + t
        x = x_ref[...]
        r = lax.rsqrt(jnp.mean(x * x, axis=-1, keepdims=True) + 1e-6)
        n = x * r
        nw = n * w_ref[...]
        sc1 = 1.0 + mod_ref[shift_row + 1:shift_row + 2, :]
        acc_ref[0:1, :] += _colsum(dh_v)
        acc_ref[1:2, :] += _colsum(dh_v * nw)
        dnw = dh_v * sc1
        acc_ref[2:3, :] += _colsum(dnw * n)
        dn = dnw * w_ref[...]
        dx = r * (dn - n * jnp.mean(dn * n, axis=-1, keepdims=True)) + dres_ref[...]
        dx_ref[...] = dx
        if has_mix:
            acc_ref[3:4, :] += _colsum(dx * mix_ref[...])
            dmix_ref[...] = (dx * mod_ref[gate_row:gate_row + 1, :]).astype(BF16)

    ins, in_specs = [], []
    for a, wt, rb in dh_pairs:
        ins += [a, wt]
        in_specs += [_row_spec(a.shape[1]), pl.BlockSpec((a.shape[1], D), functools.partial(lambda i, rb: (rb, 0), rb=rb))]
    ins += [xin, dres, mod, w] + ([mix] if has_mix else [])
    in_specs += [_row_spec(), _row_spec(), _const_spec((8, D)), _const_spec((1, D))] + ([_row_spec()] if has_mix else [])
    out_shape = [_sds((L, D), F32)] + ([_sds((L, D), BF16)] if has_mix else []) + [_sds((8, D), F32)]
    out_specs = [_row_spec()] + ([_row_spec()] if has_mix else []) + [_const_spec((8, D))]
    return pl.pallas_call(body, name=name, grid=(L // TR,), out_shape=out_shape, in_specs=in_specs,
                          out_specs=out_specs, compiler_params=_cparams(("arbitrary",)))(*ins)


def _halo(k):
    return 8 if k <= 9 else 32


def _prev_spec(h, col0):
    return pl.BlockSpec((h, CB), lambda j, i: (jnp.maximum(i * (TC // h) - 1, 0), j + col0))


def _next_spec(h, col0, n_tiles):
    return pl.BlockSpec((h, CB), lambda j, i: (jnp.minimum(i + 1, n_tiles - 1) * (TC // h), j + col0))


def _tile_spec(col0):
    return pl.BlockSpec((TC, CB), lambda j, i: (i, j + col0))


def _w_spec(kp, col0):
    return pl.BlockSpec((kp, CB), lambda j, i: (0, j + col0))


SUBLANES = 8


def _shifted_windows(v, taps, rows):
    for r in range(SUBLANES):
        group = [(o, k) for o, k in taps if o % SUBLANES == r]
        if not group:
            continue
        s = v if r == 0 else pltpu.roll(v, v.shape[0] - r, 0)
        for o, k in group:
            yield k, s[o - r:o - r + rows, :]


def _causal_taps(ext_ref, w_ref, k_taps, first, rows):
    acc = None
    for k, win in _shifted_windows(ext_ref[...], [(first - (k_taps - 1) + k, k) for k in range(k_taps)], rows):
        t = w_ref[k:k + 1, :] * win
        acc = t if acc is None else acc + t
    return acc


def _anticausal_taps(d_ref, w_ref, k_taps, rows):
    acc = None
    for k, win in _shifted_windows(d_ref[...], [(k_taps - 1 - k, k) for k in range(k_taps)], rows):
        t = w_ref[k:k + 1, :] * win
        acc = t if acc is None else acc + t
    return acc


def _acc_conv_wgrad(dw_ref, d_tile, ext_ref, k_taps, first):
    for k, win in _shifted_windows(ext_ref[...], [(first - (k_taps - 1) + k, k) for k in range(k_taps)], TC):
        dw_ref[k:k + 1, :] += _colsum(d_tile * win)
    dw_ref[k_taps:k_taps + 1, :] += _colsum(d_tile)


def conv_silu_fwd(x, col0, width, w, b, name):
    L = x.shape[0]
    k_taps = w.shape[0]
    h = _halo(k_taps)

    def body(xp_ref, x_ref, w_ref, b_ref, o_ref, ext_ref):
        i = pl.program_id(1)
        ext_ref[0:h, :] = jnp.where(i > 0, xp_ref[...], 0.0)
        ext_ref[h:h + TC, :] = x_ref[...]
        o_ref[...] = _silu(_causal_taps(ext_ref, w_ref, k_taps, h, TC) + b_ref[...])

    return pl.pallas_call(
        body, name=name, grid=(width // CB, L // TC), out_shape=_sds((L, width), F32),
        in_specs=[_prev_spec(h, col0), _tile_spec(col0), _w_spec(k_taps, 0), pl.BlockSpec((1, CB), lambda j, i: (0, j))],
        out_specs=_tile_spec(0), scratch_shapes=[pltpu.VMEM((h + TC, CB), F32)],
        compiler_params=_cparams(("parallel", "parallel")))(x, x, w, b)


def conv_silu_bwd(x, col0, width, w, b, dpost, name):
    L = x.shape[0]
    k_taps = w.shape[0]
    h = _halo(k_taps)
    nt = L // TC

    def body(xp_ref, x_ref, xn_ref, d_ref, dn_ref, w_ref, b_ref, dx_ref, dw_ref, ext_ref, dpre_ref):
        i = pl.program_id(1)

        @pl.when(i == 0)
        def _():
            dw_ref[...] = jnp.zeros((8, CB), F32)

        ext_ref[0:h, :] = jnp.where(i > 0, xp_ref[...], 0.0)
        ext_ref[h:h + TC, :] = x_ref[...]
        ext_ref[h + TC:h + TC + h, :] = xn_ref[...]
        pre = _causal_taps(ext_ref, w_ref, k_taps, h, TC + h) + b_ref[...]
        dpre_ref[0:TC, :] = d_ref[...] * _dsilu(pre[0:TC, :])
        dpre_ref[TC:TC + h, :] = jnp.where(i < nt - 1, dn_ref[...], 0.0) * _dsilu(pre[TC:TC + h, :])
        dx_ref[...] = _anticausal_taps(dpre_ref, w_ref, k_taps, TC).astype(BF16)
        _acc_conv_wgrad(dw_ref, dpre_ref[0:TC, :], ext_ref, k_taps, h)

    return pl.pallas_call(
        body, name=name, grid=(width // CB, nt),
        out_shape=[_sds((L, width), BF16), _sds((8, width), F32)],
        in_specs=[_prev_spec(h, col0), _tile_spec(col0), _next_spec(h, col0, nt), _tile_spec(0), _next_spec(h, 0, nt),
                  _w_spec(k_taps, 0), pl.BlockSpec((1, CB), lambda j, i: (0, j))],
        out_specs=[_tile_spec(0), _w_spec(8, 0)],
        scratch_shapes=[pltpu.VMEM((h + TC + h, CB), F32), pltpu.VMEM((TC + h, CB), F32)],
        compiler_params=_cparams(("parallel", "arbitrary")))(x, x, x, dpost, dpost, w, b)


def conf_conv_fwd(proj, col_a, col_g, w, b, name):
    L = proj.shape[0]
    k_taps = w.shape[0]
    h = _halo(k_taps)

    def body(ap_ref, a_ref, gp_ref, g_ref, w_ref, b_ref, o_ref, ext_ref):
        i = pl.program_id(1)
        ext_ref[0:h, :] = jnp.where(i > 0, ap_ref[...] * _sigmoid(gp_ref[...]), 0.0)
        ext_ref[h:h + TC, :] = a_ref[...] * _sigmoid(g_ref[...])
        o_ref[...] = _causal_taps(ext_ref, w_ref, k_taps, h, TC) + b_ref[...]

    return pl.pallas_call(
        body, name=name, grid=(D_CONF // CB, L // TC), out_shape=_sds((L, D_CONF), F32),
        in_specs=[_prev_spec(h, col_a), _tile_spec(col_a), _prev_spec(h, col_g), _tile_spec(col_g), _w_spec(k_taps, 0),
                  pl.BlockSpec((1, CB), lambda j, i: (0, j))],
        out_specs=_tile_spec(0), scratch_shapes=[pltpu.VMEM((h + TC, CB), F32)],
        compiler_params=_cparams(("parallel", "parallel")))(proj, proj, proj, proj, w, b)


def conf_conv_bwd(proj, col_a, col_g, w, duc, name):
    L = proj.shape[0]
    k_taps = w.shape[0]
    h = _halo(k_taps)
    nt = L // TC

    def body(ap_ref, a_ref, gp_ref, g_ref, d_ref, dn_ref, w_ref, da_ref, dg_ref, dw_ref, ext_ref, dext_ref):
        i = pl.program_id(1)

        @pl.when(i == 0)
        def _():
            dw_ref[...] = jnp.zeros((32, CB), F32)

        a = a_ref[...]
        s = _sigmoid(g_ref[...])
        ext_ref[0:h, :] = jnp.where(i > 0, ap_ref[...] * _sigmoid(gp_ref[...]), 0.0)
        ext_ref[h:h + TC, :] = a * s
        dext_ref[0:TC, :] = d_ref[...]
        dext_ref[TC:TC + h, :] = jnp.where(i < nt - 1, dn_ref[...], 0.0)
        du0 = _anticausal_taps(dext_ref, w_ref, k_taps, TC)
        da_ref[...] = (du0 * s).astype(BF16)
        dg_ref[...] = (du0 * a * s * (1.0 - s)).astype(BF16)
        _acc_conv_wgrad(dw_ref, d_ref[...], ext_ref, k_taps, h)

    return pl.pallas_call(
        body, name=name, grid=(D_CONF // CB, nt),
        out_shape=[_sds((L, D_CONF), BF16), _sds((L, D_CONF), BF16), _sds((32, D_CONF), F32)],
        in_specs=[_prev_spec(h, col_a), _tile_spec(col_a), _prev_spec(h, col_g), _tile_spec(col_g), _tile_spec(0),
                  _next_spec(h, 0, nt), _w_spec(k_taps, 0)],
        out_specs=[_tile_spec(0), _tile_spec(0), _w_spec(32, 0)],
        scratch_shapes=[pltpu.VMEM((h + TC, CB), F32), pltpu.VMEM((TC + h, CB), F32)],
        compiler_params=_cparams(("parallel", "arbitrary")))(proj, proj, proj, proj, duc, duc, w)


def ffn_conv_fwd(up, w, b, name):
    L = up.shape[0]
    k_taps = w.shape[0]
    h = _halo(k_taps)
    cv = D_FF // CB

    def body(gp_ref, g_ref, vp_ref, v_ref, wg_ref, wv_ref, bg_ref, bv_ref, o_ref, eg_ref, ev_ref):
        i = pl.program_id(1)
        eg_ref[0:h, :] = jnp.where(i > 0, gp_ref[...], 0.0)
        eg_ref[h:h + TC, :] = g_ref[...]
        ev_ref[0:h, :] = jnp.where(i > 0, vp_ref[...], 0.0)
        ev_ref[h:h + TC, :] = v_ref[...]
        pg = _causal_taps(eg_ref, wg_ref, k_taps, h, TC) + bg_ref[...]
        pv = _causal_taps(ev_ref, wv_ref, k_taps, h, TC) + bv_ref[...]
        o_ref[...] = (_silu(pg) * pv).astype(BF16)

    bspec = lambda c0: pl.BlockSpec((1, CB), lambda j, i: (0, j + c0))
    return pl.pallas_call(
        body, name=name, grid=(cv, L // TC), out_shape=_sds((L, D_FF), BF16),
        in_specs=[_prev_spec(h, 0), _tile_spec(0), _prev_spec(h, cv), _tile_spec(cv), _w_spec(k_taps, 0), _w_spec(k_taps, cv),
                  bspec(0), bspec(cv)],
        out_specs=_tile_spec(0), scratch_shapes=[pltpu.VMEM((h + TC, CB), F32), pltpu.VMEM((h + TC, CB), F32)],
        compiler_params=_cparams(("parallel", "parallel")))(up, up, up, up, w, w, b, b)


def ffn_conv_bwd(up, w, b, dact, name):
    L = up.shape[0]
    k_taps = w.shape[0]
    h = _halo(k_taps)
    nt = L // TC
    cv = D_FF // CB

    def body(gp_ref, g_ref, gn_ref, vp_ref, v_ref, vn_ref, d_ref, dn_ref, wg_ref, wv_ref, bg_ref, bv_ref,
             dg_ref, dv_ref, dwg_ref, dwv_ref, eg_ref, ev_ref, pg_ref, pv_ref):
        i = pl.program_id(1)

        @pl.when(i == 0)
        def _():
            dwg_ref[...] = jnp.zeros((8, CB), F32)
            dwv_ref[...] = jnp.zeros((8, CB), F32)

        for e_ref, p_ref, c_ref, n_ref in ((eg_ref, gp_ref, g_ref, gn_ref), (ev_ref, vp_ref, v_ref, vn_ref)):
            e_ref[0:h, :] = jnp.where(i > 0, p_ref[...], 0.0)
            e_ref[h:h + TC, :] = c_ref[...]
            e_ref[h + TC:h + TC + h, :] = n_ref[...]
        pg = _causal_taps(eg_ref, wg_ref, k_taps, h, TC + h) + bg_ref[...]
        pv = _causal_taps(ev_ref, wv_ref, k_taps, h, TC + h) + bv_ref[...]
        dact_t = d_ref[...]
        dact_n = jnp.where(i < nt - 1, dn_ref[...], 0.0)
        pg_ref[0:TC, :] = dact_t * pv[0:TC, :] * _dsilu(pg[0:TC, :])
        pg_ref[TC:TC + h, :] = dact_n * pv[TC:TC + h, :] * _dsilu(pg[TC:TC + h, :])
        pv_ref[0:TC, :] = dact_t * _silu(pg[0:TC, :])
        pv_ref[TC:TC + h, :] = dact_n * _silu(pg[TC:TC + h, :])
        dg_ref[...] = _anticausal_taps(pg_ref, wg_ref, k_taps, TC).astype(BF16)
        dv_ref[...] = _anticausal_taps(pv_ref, wv_ref, k_taps, TC).astype(BF16)
        _acc_conv_wgrad(dwg_ref, pg_ref[0:TC, :], eg_ref, k_taps, h)
        _acc_conv_wgrad(dwv_ref, pv_ref[0:TC, :], ev_ref, k_taps, h)

    bspec = lambda c0: pl.BlockSpec((1, CB), lambda j, i: (0, j + c0))
    ext = pltpu.VMEM((h + TC + h, CB), F32)
    dpre = pltpu.VMEM((TC + h, CB), F32)
    return pl.pallas_call(
        body, name=name, grid=(cv, nt),
        out_shape=[_sds((L, D_FF), BF16), _sds((L, D_FF), BF16), _sds((8, D_FF), F32), _sds((8, D_FF), F32)],
        in_specs=[_prev_spec(h, 0), _tile_spec(0), _next_spec(h, 0, nt), _prev_spec(h, cv), _tile_spec(cv), _next_spec(h, cv, nt),
                  _tile_spec(0), _next_spec(h, 0, nt), _w_spec(k_taps, 0), _w_spec(k_taps, cv), bspec(0), bspec(cv)],
        out_specs=[_tile_spec(0), _tile_spec(0), _w_spec(8, 0), _w_spec(8, 0)],
        scratch_shapes=[ext, ext, dpre, dpre],
        compiler_params=_cparams(("parallel", "arbitrary")))(up, up, up, up, up, up, dact, dact, w, w, b, b)


def _ssd_common(xbc_ref, dt_ref, dtb_ref, alog_ref, cs_ref):
    xs = xbc_ref[:, 0:D_SSD]
    sp_in = dt_ref[...] + dtb_ref[...]
    dtf = _softplus(sp_in)
    a_f = -jnp.exp(alog_ref[...])
    a_dt = dtf * a_f
    row = lax.broadcasted_iota(jnp.int32, (Q, Q), 0)
    col = lax.broadcasted_iota(jnp.int32, (Q, Q), 1)
    causal = row >= col
    cs = _dot_exact(causal.astype(F32), a_dt, 3, "a")
    cs_ref[...] = cs
    cs_last = cs_ref[Q - 1:Q, :]
    return xs, sp_in, dtf, a_f, cs, cs_last, causal


def _head_decay(cs_j, cst_ref, e, causal):
    lane = lax.broadcasted_iota(jnp.int32, (Q, LANE), 1)
    rolled = pltpu.roll(cs_j, HEAD, 1)
    own = (lane < HEAD) if e == 0 else (lane >= HEAD)
    col_b = jnp.where(own, cs_j, rolled)
    col_b = jnp.concatenate([col_b] * (Q // LANE), axis=1)
    row_b = cst_ref[e * HEAD:e * HEAD + 1, :]
    return jnp.where(causal, jnp.exp(jnp.minimum(col_b - row_b, 0.0)), 0.0)


def ssd_fwd(xbc, z_src, dt_src, dtb_f, alog_f, dsk_f, snw, name):
    L = xbc.shape[0]
    nc = L // Q

    def body(xbc_ref, z_ref, dt_ref, dtb_ref, alog_ref, dsk_ref, snw_ref, y_ref, yn_ref, sp_ref, s_ref, cs_ref, cst_ref, yd_ref):
        @pl.when(pl.program_id(0) == 0)
        def _():
            s_ref[...] = jnp.zeros((N_STATE, D_SSD), F32)

        xs, _, dtf, a_f, cs, cs_last, causal = _ssd_common(xbc_ref, dt_ref, dtb_ref, alog_ref, cs_ref)
        e_cs = jnp.exp(cs)
        xdt = xs * dtf
        zst = jnp.exp(cs_last - cs) * xdt
        sp_ref[0] = s_ref[...]
        lane = lax.broadcasted_iota(jnp.int32, (Q, LANE), 1)
        for g in range(2):
            gl = slice(g * 512, g * 512 + 512)
            b_g = xbc_ref[:, D_SSD + g * N_STATE:D_SSD + (g + 1) * N_STATE]
            c_g = xbc_ref[:, D_SSD + 2 * N_STATE + g * N_STATE:D_SSD + 2 * N_STATE + (g + 1) * N_STATE]
            s_prev = s_ref[:, gl]
            cb = _dot_nt(c_g, b_g)
            yd_ref[:, gl] = e_cs[:, gl] * _dot(c_g, s_prev)
            for j in range(4):
                tl = slice(g * 512 + j * LANE, g * 512 + (j + 1) * LANE)
                cs_j = cs[:, tl]
                cst_ref[...] = cs_j.T
                x_j = xdt[:, tl]
                o0 = _dot(cb * _head_decay(cs_j, cst_ref, 0, causal), x_j)
                o1 = _dot(cb * _head_decay(cs_j, cst_ref, 1, causal), x_j)
                yd_ref[:, tl] += jnp.where(lane < HEAD, o0, o1)
            s_ref[:, gl] = jnp.exp(cs_last[:, gl]) * s_prev + _dot_tn(b_g, zst[:, gl])
        y = yd_ref[...] + xs * dsk_ref[...]
        y_ref[...] = y
        yz = y * _silu(z_ref[...])
        r = lax.rsqrt(jnp.mean(yz * yz, axis=-1, keepdims=True) + 1e-6)
        yn_ref[...] = (yz * r * snw_ref[...]).astype(BF16)

    chunk = lambda w, c: pl.BlockSpec((Q, w), lambda i: (i, c))
    return pl.pallas_call(
        body, name=name, grid=(nc,),
        out_shape=[_sds((L, D_SSD), F32), _sds((L, D_SSD), BF16), _sds((nc, N_STATE, D_SSD), F32)],
        in_specs=[chunk(D_XBC, 0), chunk(D, 0), chunk(D, 0)] + [_const_spec((1, D))] * 4,
        out_specs=[chunk(D, 0), chunk(D, 0), pl.BlockSpec((1, N_STATE, D_SSD), lambda i: (i, 0, 0))],
        scratch_shapes=[pltpu.VMEM((N_STATE, D_SSD), F32), pltpu.VMEM((Q, D_SSD), F32), pltpu.VMEM((LANE, Q), F32),
                        pltpu.VMEM((Q, D_SSD), F32)],
        compiler_params=_cparams(("arbitrary",)))(xbc, z_src, dt_src, dtb_f, alog_f, dsk_f, snw)


def ssd_bwd(dysn, y, xbc, z_src, dt_src, s_prev_all, dtb_f, alog_f, dsk_f, snw, name):
    L = xbc.shape[0]
    nc = L // Q

    def body(dyn_ref, y_ref, xbc_ref, z_ref, dt_ref, sp_ref, dtb_ref, alog_ref, dsk_ref, snw_ref,
             dz_ref, ddt_ref, dxbc_ref, acc_ref, acc16_ref, ds_ref, cs_ref, cst_ref, dcs_ref, dx_ref):
        step = pl.program_id(0)

        @pl.when(step == 0)
        def _():
            ds_ref[...] = jnp.zeros((N_STATE, D_SSD), F32)
            acc_ref[...] = jnp.zeros((8, D), F32)

        z = z_ref[...]
        y = y_ref[...]
        sz = _sigmoid(z)
        siluz = z * sz
        yz = y * siluz
        r = lax.rsqrt(jnp.mean(yz * yz, axis=-1, keepdims=True) + 1e-6)
        n = yz * r
        dyn = dyn_ref[...]
        acc_ref[0:1, :] += _colsum(dyn * n)
        dn = dyn * snw_ref[...]
        dyz = r * (dn - n * jnp.mean(dn * n, axis=-1, keepdims=True))
        dy = dyz * siluz
        dz_ref[...] = (dyz * y * (sz * (1.0 + z * (1.0 - sz)))).astype(BF16)

        xs, sp_in, dtf, a_f, cs, cs_last, causal = _ssd_common(xbc_ref, dt_ref, dtb_ref, alog_ref, cs_ref)
        acc_ref[3:4, :] += _colsum(dy * xs)
        e_cs = jnp.exp(cs)
        xdt = xs * dtf
        dst = jnp.exp(cs_last - cs)
        zst = dst * xdt
        e_last = jnp.exp(cs_last)
        lane = lax.broadcasted_iota(jnp.int32, (Q, LANE), 1)
        ones = jnp.ones((Q, LANE), F32)
        dcs_last_parts = []
        for g in range(2):
            gl = slice(g * 512, g * 512 + 512)
            b_g = xbc_ref[:, D_SSD + g * N_STATE:D_SSD + (g + 1) * N_STATE]
            c_g = xbc_ref[:, D_SSD + 2 * N_STATE + g * N_STATE:D_SSD + 2 * N_STATE + (g + 1) * N_STATE]
            s_prev = sp_ref[0, :, gl]
            ds_g = ds_ref[:, gl]
            dy_g = dy[:, gl]
            cb = _dot_nt(c_g, b_g)
            y_off = e_cs[:, gl] * _dot(c_g, s_prev)
            edy = e_cs[:, gl] * dy_g
            d_c = _dot_nt(edy, s_prev)
            d_z = _dot(b_g, ds_g)
            d_b = _dot_nt(zst[:, gl], ds_g)
            t_g = d_z * zst[:, gl]
            dcs_ref[:, gl] = dy_g * y_off - t_g
            dx_ref[:, gl] = d_z * dst[:, gl]
            dcs_last_parts.append(_colsum(t_g) + _colsum(ds_g * s_prev) * e_last[:, gl])
            ds_ref[:, gl] = e_last[:, gl] * ds_g + _dot_tn(c_g, edy)
            dcb = jnp.zeros((Q, Q), F32)
            for j in range(4):
                tl = slice(g * 512 + j * LANE, g * 512 + (j + 1) * LANE)
                cs_j = cs[:, tl]
                cst_ref[...] = cs_j.T
                x_j = xdt[:, tl]
                dy_j = dy[:, tl]
                dx_j = jnp.zeros((Q, LANE), F32)
                dcs_j = jnp.zeros((Q, LANE), F32)
                for e in range(2):
                    own = (lane < HEAD) if e == 0 else (lane >= HEAD)
                    w_h = _head_decay(cs_j, cst_ref, e, causal)
                    g_h = cb * w_h
                    dy_m = jnp.where(own, dy_j, 0.0)
                    d_g = _dot_nt(dy_m, x_j)
                    dx_j = dx_j + _dot_tn(g_h, dy_m)
                    dcb = dcb + d_g * w_h
                    p_h = d_g * g_h
                    row_sums = _dot_exact(p_h, ones, 2, "b")
                    col_sums = _dot_exact(p_h, ones, 2, "b", (((0,), (0,)), ((), ())))
                    dcs_j = dcs_j + jnp.where(own, row_sums - col_sums, 0.0)
                dcs_ref[:, tl] += dcs_j * (1.0 / HEAD)
                dx_ref[:, tl] += dx_j
            d_c = d_c + _dot(dcb, b_g)
            d_b = d_b + _dot_tn(dcb, c_g)
            dxbc_ref[:, D_SSD + g * N_STATE:D_SSD + (g + 1) * N_STATE] = d_b
            dxbc_ref[:, D_SSD + 2 * N_STATE + g * N_STATE:D_SSD + 2 * N_STATE + (g + 1) * N_STATE] = d_c
        dcs_last = jnp.concatenate(dcs_last_parts, axis=1)
        anticausal = lax.broadcasted_iota(jnp.int32, (Q, Q), 0) <= lax.broadcasted_iota(jnp.int32, (Q, Q), 1)
        d_adt = _dot_exact(anticausal.astype(F32), dcs_ref[...], 3, "a") + dcs_last
        dx = dx_ref[...]
        acc_ref[2:3, :] += _colsum(d_adt * dtf) * a_f
        d_dtf = d_adt * a_f + dx * xs
        dxbc_ref[:, 0:D_SSD] = dx * dtf + dy * dsk_ref[...]
        d_raw = d_dtf * _sigmoid(sp_in)
        acc_ref[1:2, :] += _colsum(d_raw)
        head_of_lane = lax.broadcasted_iota(jnp.int32, (D_SSD, LANE), 0) // HEAD
        fold = (head_of_lane == lax.broadcasted_iota(jnp.int32, (D_SSD, LANE), 1)).astype(F32)
        ddt_ref[...] = _dot_exact(d_raw, fold, 2, "b").astype(BF16)

        @pl.when(step == nc - 1)
        def _():
            acc16_ref[...] = _dot_exact(acc_ref[...], fold, 3, "b")

    rchunk = lambda w, c: pl.BlockSpec((Q, w), lambda i: (nc - 1 - i, c))
    return pl.pallas_call(
        body, name=name, grid=(nc,),
        out_shape=[_sds((L, D_SSD), BF16), _sds((L, LANE), BF16), _sds((L, D_XBC), F32), _sds((8, D), F32), _sds((8, LANE), F32)],
        in_specs=[rchunk(D, 0), rchunk(D, 0), rchunk(D_XBC, 0), rchunk(D, 0), rchunk(D, 0),
                  pl.BlockSpec((1, N_STATE, D_SSD), lambda i: (nc - 1 - i, 0, 0))] + [_const_spec((1, D))] * 4,
        out_specs=[rchunk(D, 0), rchunk(LANE, 0), rchunk(D_XBC, 0), _const_spec((8, D)), _const_spec((8, LANE))],
        scratch_shapes=[pltpu.VMEM((N_STATE, D_SSD), F32), pltpu.VMEM((Q, D_SSD), F32), pltpu.VMEM((LANE, Q), F32),
                        pltpu.VMEM((Q, D_SSD), F32), pltpu.VMEM((Q, D_SSD), F32)],
        compiler_params=_cparams(("arbitrary",)))(dysn, y, xbc, z_src, dt_src, s_prev_all, dtb_f, alog_f, dsk_f, snw)


def _adamw_math(w, g, m, v):
    m_n = ADAM_B1 * m + (1.0 - ADAM_B1) * g
    v_n = ADAM_B2 * v + (1.0 - ADAM_B2) * jnp.square(g)
    c1 = 1.0 - ADAM_B1 ** ADAM_STEP
    c2 = 1.0 - ADAM_B2 ** ADAM_STEP
    return -ADAM_LR * ((m_n / c1) / (jnp.sqrt(v_n / c2) + ADAM_EPS) + ADAM_WD * w), m_n, v_n


def _sum_slots(p_ref):
    acc = p_ref[0].astype(F32)
    for s in range(1, p_ref.shape[0]):
        acc = acc + p_ref[s].astype(F32)
    return acc


def adamw_slots(w, slots, m, v, name):
    rows, cols = w.shape
    tc = 256

    def body(w_ref, s_ref, m_ref, v_ref, g_ref, d_ref, mo_ref, vo_ref):
        g_v = _sum_slots(s_ref)
        g_ref[...] = g_v
        d_ref[...], mo_ref[...], vo_ref[...] = _adamw_math(w_ref[...], g_v, m_ref[...], v_ref[...])

    spec = pl.BlockSpec((rows, tc), lambda i: (0, i))
    return pl.pallas_call(body, name=name, grid=(cols // tc,), out_shape=[_sds((rows, cols), F32)] * 4,
                          in_specs=[spec, pl.BlockSpec((slots.shape[0], rows, tc), lambda i: (0, 0, i)), spec, spec], out_specs=[spec] * 4,
                          compiler_params=_cparams(("parallel",)))(w, slots, m, v)


def adamw_many(ws, gs, ms, vs, name):
    n = len(ws)

    def body(*refs):
        for p in range(n):
            d_v, m_v, v_v = _adamw_math(refs[p][...], refs[n + p][...], refs[2 * n + p][...], refs[3 * n + p][...])
            refs[4 * n + p][...] = d_v
            refs[5 * n + p][...] = m_v
            refs[6 * n + p][...] = v_v

    vm = pl.BlockSpec(memory_space=pltpu.VMEM)
    out = pl.pallas_call(body, name=name, out_shape=[_sds(w.shape, F32) for w in ws] * 3, in_specs=[vm] * (4 * n),
                         out_specs=[vm] * (3 * n), compiler_params=_cparams())(*ws, *gs, *ms, *vs)
    return out[:n], out[n:2 * n], out[2 * n:]


def _pack_layout(shapes):
    row, layout = 0, []
    for rows, cols in shapes:
        chunks = []
        for c0 in range(0, cols, D):
            chunks.append((row, c0, min(D, cols - c0)))
            row += rows
        layout.append(chunks)
    return row, layout


def pack_rows(entries, name):
    arrays = [e[0] for e in entries]
    used, layout = _pack_layout([(e[2], e[0].shape[1]) for e in entries])
    total = -(-used // SUBLANES) * SUBLANES
    n = len(arrays)

    def body(*refs):
        o_ref = refs[n]
        o_ref[...] = jnp.zeros((total, D), F32)
        for p in range(n):
            _, first, rows = entries[p]
            for r0, c0, w in layout[p]:
                o_ref[r0:r0 + rows, 0:w] = refs[p][first:first + rows, c0:c0 + w]

    vm = pl.BlockSpec(memory_space=pltpu.VMEM)
    return pl.pallas_call(body, name=name, out_shape=_sds((total, D), F32), in_specs=[vm] * n, out_specs=vm,
                          compiler_params=_cparams())(*arrays)


def unpack_rows(packed, shapes):
    _, layout = _pack_layout(shapes)
    out = []
    for (rows, _), chunks in zip(shapes, layout):
        parts = [packed[..., r0:r0 + rows, 0:w] for r0, _, w in chunks]
        out.append(parts[0] if len(parts) == 1 else jnp.concatenate(parts, axis=-1))
    return out


def sum_slots_many(parts, name):
    n = len(parts)

    def body(*refs):
        for p in range(n):
            refs[n + p][...] = _sum_slots(refs[p])

    vm = pl.BlockSpec(memory_space=pltpu.VMEM)
    return pl.pallas_call(body, name=name, out_shape=[_sds(p.shape[1:], F32) for p in parts], in_specs=[vm] * n,
                          out_specs=[vm] * n, compiler_params=_cparams())(*parts)


def ada_mod(c_all, ada_w_shard, ada_b_cols, name):
    def body(c_ref, w_ref, b_ref, o_ref, ca_ref):
        ca = _silu(c_ref[...])
        ca_ref[...] = ca
        o_ref[...] = _dot(ca, w_ref[...]) + b_ref[...]

    vm = pl.BlockSpec(memory_space=pltpu.VMEM)
    return pl.pallas_call(body, name=name, out_shape=[_sds((N_DEV, ada_w_shard.shape[1]), F32), _sds((N_DEV, D), F32)],
                          in_specs=[vm, vm, vm], out_specs=[vm, vm], compiler_params=_cparams())(c_all, ada_w_shard, ada_b_cols)


def ada_wgrad(c_act_all, dmod_cols, name):
    def body(c_ref, d_ref, o_ref):
        o_ref[...] = _dot_tn_hi(c_ref[...], d_ref[...])

    vm = pl.BlockSpec(memory_space=pltpu.VMEM)
    return pl.pallas_call(body, name=name, out_shape=_sds((D, dmod_cols.shape[1]), F32), in_specs=[vm, vm], out_specs=vm,
                          compiler_params=_cparams())(c_act_all, dmod_cols)


def exchange(srcs, name, gather):
    n = len(srcs)
    gathers = [gather] * n if isinstance(gather, bool) else list(gather)
    shapes = [tuple(s.shape) if g else tuple(s.shape[1:]) for s, g in zip(srcs, gathers)]

    def body(*refs):
        src_refs, out_refs = refs[:n], refs[n:2 * n]
        send_sems, recv_sems, local_sems = refs[2 * n:]
        x, y, c = lax.axis_index("x"), lax.axis_index("y"), lax.axis_index("c")
        me = 4 * x + 2 * y + c

        def peer(k):
            bx, by, bc = (k >> 2) & 1, (k >> 1) & 1, k & 1
            px, py, pc = (x + bx) % 2, (y + by) % 2, (c + bc) % 2
            return (px, py, pc), 4 * px + 2 * py + pc

        def copy(a, k, landing):
            dev, idx = peer(k)
            return pltpu.make_async_remote_copy(
                src_ref=src_refs[a] if gathers[a] else src_refs[a].at[idx], dst_ref=out_refs[a].at[idx if landing else me],
                send_sem=send_sems.at[a, k - 1], recv_sem=recv_sems.at[a, k - 1],
                device_id=dev, device_id_type=pl.DeviceIdType.MESH)

        mine = [pltpu.make_async_copy(src_refs[a] if gathers[a] else src_refs[a].at[me], out_refs[a].at[me], local_sems.at[a])
                for a in range(n)]
        for cp in mine:
            cp.start()
        sends = [copy(a, k, False) for a in range(n) for k in range(1, N_DEV)]
        for cp in sends:
            cp.start()
        for a in range(n):
            for k in range(1, N_DEV):
                copy(a, k, True).wait_recv()
        for cp in sends:
            cp.wait_send()
        for cp in mine:
            cp.wait()

    hbm = pl.BlockSpec(memory_space=pl.ANY)
    return pl.pallas_call(
        body, name=name, out_shape=[_sds((N_DEV,) + shp, s.dtype) for shp, s in zip(shapes, srcs)], in_specs=[hbm] * n,
        out_specs=[hbm] * n,
        scratch_shapes=[pltpu.SemaphoreType.DMA((n, N_DEV - 1)), pltpu.SemaphoreType.DMA((n, N_DEV - 1)),
                        pltpu.SemaphoreType.DMA((n,))],
        compiler_params=pltpu.CompilerParams(has_side_effects=True))(*srcs)


def gather_two_level(srcs, name):
    n = len(srcs)

    def body(*refs):
        src_refs, out_refs = refs[:n], refs[n:2 * n]
        send_sems, recv_sems, local_sems = refs[2 * n:]
        x, y, c = lax.axis_index("x"), lax.axis_index("y"), lax.axis_index("c")
        me, sibling = (x, y, c), (x, y, 1 - c)
        chips = [(1 - x, y), (x, 1 - y), (1 - x, 1 - y)]

        def slot(a, px, py, pc):
            return out_refs[a].at[4 * px + 2 * py + pc]

        def copy(a, k, block, to, src=None):
            return pltpu.make_async_remote_copy(
                src_ref=slot(a, *block) if src is None else src, dst_ref=slot(a, *block), send_sem=send_sems.at[a, k],
                recv_sem=recv_sems.at[a, k], device_id=to, device_id_type=pl.DeviceIdType.MESH)

        mine = [pltpu.make_async_copy(src_refs[a], slot(a, *me), local_sems.at[a]) for a in range(n)]
        for cp in mine:
            cp.start()
        first = []
        for a in range(n):
            first += [copy(a, 0, me, sibling, src=src_refs[a])]
            first += [copy(a, 1 + j, me, (*chip, c), src=src_refs[a]) for j, chip in enumerate(chips)]
        for cp in first:
            cp.start()
        passed = []
        for a in range(n):
            for j, chip in enumerate(chips):
                copy(a, 1 + j, (*chip, c), me).wait_recv()
                passed.append(copy(a, 4 + j, (*chip, c), sibling))
                passed[-1].start()
        for a in range(n):
            copy(a, 0, sibling, me).wait_recv()
            for j, chip in enumerate(chips):
                copy(a, 4 + j, (*chip, 1 - c), me).wait_recv()
        for cp in first + passed:
            cp.wait_send()
        for cp in mine:
            cp.wait()

    hbm = pl.BlockSpec(memory_space=pl.ANY)
    return pl.pallas_call(
        body, name=name, out_shape=[_sds((N_DEV,) + tuple(s.shape), s.dtype) for s in srcs], in_specs=[hbm] * n,
        out_specs=[hbm] * n,
        scratch_shapes=[pltpu.SemaphoreType.DMA((n, N_DEV - 1)), pltpu.SemaphoreType.DMA((n, N_DEV - 1)),
                        pltpu.SemaphoreType.DMA((n,))],
        compiler_params=pltpu.CompilerParams(has_side_effects=True))(*srcs)


def _peer(k):
    x, y, c = lax.axis_index("x"), lax.axis_index("y"), lax.axis_index("c")
    px, py, pc = (x + ((k >> 2) & 1)) % 2, (y + ((k >> 1) & 1)) % 2, (c + (k & 1)) % 2
    return (px, py, pc), 4 * px + 2 * py + pc


def _my_slot():
    return 4 * lax.axis_index("x") + 2 * lax.axis_index("y") + lax.axis_index("c")


_HBM = pl.BlockSpec(memory_space=pltpu.HBM)
_SEM = pl.BlockSpec(memory_space=pltpu.SEMAPHORE)
_EFFECT = pltpu.SideEffectType.DATAFLOW_SIDE_EFFECTING


def exchange_start(srcs, name, gather):
    n = len(srcs)
    shapes = [tuple(s.shape) if gather else tuple(s.shape[1:]) for s in srcs]
    lands = [lax.empty((N_DEV,) + shp, s.dtype) for shp, s in zip(shapes, srcs)]

    def body(*refs):
        src_refs, land_refs = refs[:n], refs[n:2 * n]
        sems = refs[2 * n:4 * n]
        token = refs[-1]
        me = _my_slot()
        for a in range(n):
            for k in range(1, N_DEV):
                dev, idx = _peer(k)
                pltpu.make_async_remote_copy(
                    src_ref=src_refs[a] if gather else src_refs[a].at[idx], dst_ref=land_refs[a].at[me],
                    send_sem=sems[2 * a].at[k - 1], recv_sem=sems[2 * a + 1].at[k - 1],
                    device_id=dev, device_id_type=pl.DeviceIdType.MESH).start()
        token[...] = jnp.zeros_like(token)

    out_shape = ([pltpu.SemaphoreType.DMA((N_DEV - 1,))] * (2 * n) + [pltpu.HBM(s.shape, s.dtype) for s in srcs]
                 + [pltpu.HBM(l.shape, l.dtype) for l in lands] + [_sds((8, LANE), F32)])
    out = pl.pallas_call(
        body, name=name, out_shape=out_shape, in_specs=[_HBM] * (2 * n),
        out_specs=[_SEM] * (2 * n) + [_HBM] * (2 * n) + [pl.BlockSpec(memory_space=pltpu.VMEM)],
        input_output_aliases={i: 2 * n + i for i in range(2 * n)},
        compiler_params=pltpu.CompilerParams(has_side_effects=_EFFECT))(
            *[pltpu.with_memory_space_constraint(s, pltpu.HBM) for s in srcs],
            *[pltpu.with_memory_space_constraint(l, pltpu.HBM) for l in lands])
    parts = [(out[2 * a], out[2 * a + 1], out[2 * n + a], out[3 * n + a]) for a in range(n)]
    return parts, out[-1]


def exchange_wait(parts, after, name, gather):
    n = len(parts)

    def body(*refs):
        src_refs, land_refs = refs[:n], refs[n:2 * n]
        sems = refs[2 * n:4 * n]
        for a in range(n):
            for k in range(1, N_DEV):
                dev, idx = _peer(k)
                copy = pltpu.make_async_remote_copy(
                    src_ref=src_refs[a] if gather else src_refs[a].at[idx], dst_ref=land_refs[a].at[idx],
                    send_sem=sems[2 * a].at[k - 1], recv_sem=sems[2 * a + 1].at[k - 1],
                    device_id=dev, device_id_type=pl.DeviceIdType.MESH)
                copy.wait_send()
                copy.wait_recv()

    srcs = [p[2] for p in parts]
    lands = [p[3] for p in parts]
    sems = [s for p in parts for s in p[:2]]
    out = pl.pallas_call(
        body, name=name, out_shape=[pltpu.HBM(a.shape, a.dtype) for a in srcs + lands],
        in_specs=[_HBM] * (2 * n) + [_SEM] * (2 * n) + [pl.BlockSpec(memory_space=pl.ANY)], out_specs=[_HBM] * (2 * n),
        input_output_aliases={i: i for i in range(2 * n)},
        compiler_params=pltpu.CompilerParams(has_side_effects=_EFFECT))(*srcs, *lands, *sems, after)
    return out[n:]


def _cols_to_slabs(g):
    r, c = g.shape
    return g.reshape(r, N_DEV, c // N_DEV).transpose(1, 0, 2)


def _slabs_to_cols(s):
    _, r, cs = s.shape
    return s.transpose(1, 0, 2).reshape(r, N_DEV * cs)


def _rep_heads(v):
    return jnp.repeat(v.reshape(N_HEADS), HEAD).reshape(1, D_SSD)


def local_fwd_bwd(x, target, mod, get_w, put_grad, small):
    n1w, n2w, fnw = small["norm1_w"], small["norm2_w"], small["final_norm_w"]
    dtb_f, alog_f, dsk_f = _rep_heads(small["dt_bias"]), _rep_heads(small["a_log"]), _rep_heads(small["d_skip"])
    snw = small["ssd_norm_w"]

    def after(v, token):
        return v + token[0:1, 0:1]

    w_in = get_w("w_in", mod)
    h1, proj_zx, proj_dt, proj_cf = in_proj(x, mod, n1w, w_in["w_full"], w_in["w_dt_rep"], w_in["w_cf"], "norm1_in_proj")
    xbc = conv_silu_fwd(proj_zx, D_SSD // CB, D_XBC, small["ssd_conv_w"], small["ssd_conv_b"], "ssd_conv")
    y, ysn, s_prev = ssd_fwd(xbc, proj_zx, proj_dt, dtb_f, alog_f, dsk_f, snw, "ssd_scan")
    uc = conf_conv_fwd(proj_cf, 0, D_CONF // CB, small["conf_conv_w"], small["conf_conv_b"], "conf_conv")
    w_out = get_w("w_out", uc)
    mix, u, x1, h2 = mixer_out(ysn, uc, small["conf_ln_w"], small["conf_ln_b"], w_out, x, mod, n2w, "out_proj_norm2")
    w_up_t = get_w("w_up", h2)
    up = mm_nt([(h2, w_up_t, 0)], "up_proj")
    act = ffn_conv_fwd(up, small["ffn_conv_w"], small["ffn_conv_b"], "ffn_conv")
    w_down = get_w("w_down", act)
    dx2, dff, dact, acc_f = final_loss(act, w_down, x1, mod, fnw, target, "down_proj_loss")

    token = put_grad("w_down", mm_tn(act, dff, "wgrad_down"))
    dupg, dupv, dwg, dwv = ffn_conv_bwd(up, small["ffn_conv_w"], after(small["ffn_conv_b"], token), dact, "ffn_conv_bwd")
    token = put_grad("w_up", mm_tn_stack([dupg, dupv], h2, "wgrad_up"))
    dx1, dmix, acc_2 = norm_mod_bwd([(dupg, w_up_t, 0), (dupv, w_up_t, 1)], x1, dx2, mod, after(n2w, token), 3, "norm2_bwd",
                                    mix=mix, gate_row=2)

    token = put_grad("w_out", mm_tn_stack([ysn, u], dmix, "wgrad_out"))
    dysn, duc, acc_ln = mixer_out_bwd(dmix, w_out, uc, after(small["conf_ln_w"], token), small["conf_ln_b"], "out_proj_bwd")
    dcfa, dcfg, dw_cc = conf_conv_bwd(proj_cf, 0, D_CONF // CB, small["conf_conv_w"], duc, "conf_conv_bwd")
    dz, ddt, dxbc_post, acc_s, acc_s16 = ssd_bwd(dysn, y, xbc, proj_zx, proj_dt, s_prev, dtb_f, alog_f, dsk_f, snw,
                                                 "ssd_scan_bwd")
    dxbc, dw_sc = conv_silu_bwd(proj_zx, D_SSD // CB, D_XBC, small["ssd_conv_w"], small["ssd_conv_b"], dxbc_post, "ssd_conv_bwd")
    token = put_grad("w_in", mm_tn_concat(
        [(dz, D_SSD), (dxbc, D_XBC), (ddt, N_HEADS), (dcfa, D_CONF), (dcfg, D_CONF)], h1, "wgrad_in"))
    dh1_pairs = [(dz, w_in["w_full"], 0), (ddt, w_in["w_dt16"], 0), (dcfa, w_in["w_cf"], 0), (dcfg, w_in["w_cf"], 1),
                 (dxbc, w_in["w_xbc"], 0)]
    grad_x, acc_1 = norm_mod_bwd(dh1_pairs, x, dx1, mod, after(n1w, token), 0, "norm1_bwd")

    small_accs = dict(acc_1=acc_1, acc_2=acc_2, acc_f=acc_f, acc_ln=acc_ln, acc_s=acc_s, acc_s16=acc_s16, dw_sc=dw_sc,
                      dw_cc=dw_cc, dwg=dwg, dwv=dwv)
    return grad_x, small_accs


def kernel(x, c, ada_w, ada_b, norm1_w, w_in, ssd_conv_w, ssd_conv_b, dt_bias, a_log, d_skip, ssd_norm_w, conf_conv_w, conf_conv_b, conf_ln_w, conf_ln_b, w_out, norm2_w, w_up, ffn_conv_w, ffn_conv_b, w_down, final_norm_w, loss_target, m_ada_w, m_ada_b, m_norm1_w, m_w_in, m_ssd_conv_w, m_ssd_conv_b, m_dt_bias, m_a_log, m_d_skip, m_ssd_norm_w, m_conf_conv_w, m_conf_conv_b, m_conf_ln_w, m_conf_ln_b, m_w_out, m_norm2_w, m_w_up, m_ffn_conv_w, m_ffn_conv_b, m_w_down, m_final_norm_w, v_ada_w, v_ada_b, v_norm1_w, v_w_in, v_ssd_conv_w, v_ssd_conv_b, v_dt_bias, v_a_log, v_d_skip, v_ssd_norm_w, v_conf_conv_w, v_conf_conv_b, v_conf_ln_w, v_conf_ln_b, v_w_out, v_norm2_w, v_w_up, v_ffn_conv_w, v_ffn_conv_b, v_w_down, v_final_norm_w):
    me = 4 * lax.axis_index("x") + 2 * lax.axis_index("y") + lax.axis_index("c")
    weights = dict(ada_w=ada_w, ada_b=ada_b, norm1_w=norm1_w, w_in=w_in, ssd_conv_w=ssd_conv_w, ssd_conv_b=ssd_conv_b,
                   dt_bias=dt_bias, a_log=a_log, d_skip=d_skip, ssd_norm_w=ssd_norm_w, conf_conv_w=conf_conv_w,
                   conf_conv_b=conf_conv_b, conf_ln_w=conf_ln_w, conf_ln_b=conf_ln_b, w_out=w_out, norm2_w=norm2_w, w_up=w_up,
                   ffn_conv_w=ffn_conv_w, ffn_conv_b=ffn_conv_b, w_down=w_down, final_norm_w=final_norm_w)
    moms_m = dict(ada_w=m_ada_w, ada_b=m_ada_b, norm1_w=m_norm1_w, w_in=m_w_in, ssd_conv_w=m_ssd_conv_w, ssd_conv_b=m_ssd_conv_b,
                  dt_bias=m_dt_bias, a_log=m_a_log, d_skip=m_d_skip, ssd_norm_w=m_ssd_norm_w, conf_conv_w=m_conf_conv_w,
                  conf_conv_b=m_conf_conv_b, conf_ln_w=m_conf_ln_w, conf_ln_b=m_conf_ln_b, w_out=m_w_out, norm2_w=m_norm2_w,
                  w_up=m_w_up, ffn_conv_w=m_ffn_conv_w, ffn_conv_b=m_ffn_conv_b, w_down=m_w_down, final_norm_w=m_final_norm_w)
    moms_v = dict(ada_w=v_ada_w, ada_b=v_ada_b, norm1_w=v_norm1_w, w_in=v_w_in, ssd_conv_w=v_ssd_conv_w, ssd_conv_b=v_ssd_conv_b,
                  dt_bias=v_dt_bias, a_log=v_a_log, d_skip=v_d_skip, ssd_norm_w=v_ssd_norm_w, conf_conv_w=v_conf_conv_w,
                  conf_conv_b=v_conf_conv_b, conf_ln_w=v_conf_ln_w, conf_ln_b=v_conf_ln_b, w_out=v_w_out, norm2_w=v_norm2_w,
                  w_up=v_w_up, ffn_conv_w=v_ffn_conv_w, ffn_conv_b=v_ffn_conv_b, w_down=v_w_down, final_norm_w=v_final_norm_w)
    names = list(weights)

    def to2d(a):
        return a[0] if a.ndim == 3 else a.reshape(1, -1)

    big = ("w_in", "w_out", "w_up", "w_down")

    def rows_of(a):
        return jnp.swapaxes(a, 1, 2)[0] if a.shape[2] != D else a[0]

    shards = [rows_of(weights[n]).astype(BF16) for n in big]

    c_all, scw_all, ccw_all, fcw_all, w_in_slabs = gather_two_level(
        [c.reshape(8, LANE), ssd_conv_w[0], conf_conv_w[0], ffn_conv_w[0], shards[0]], "gather_first")
    c_all = c_all.reshape(N_DEV, D)

    ada_cols = ada_w.shape[2]
    ada_b_cols = lax.dynamic_slice(ada_b, (0, me * ada_cols), (1, ada_cols))
    mod_cols, c_act_all = ada_mod(c_all, ada_w[0], ada_b_cols, "ada_mod")
    mod_parts, = exchange([jnp.pad(mod_cols, ((0, 0), (0, D - ada_cols))).reshape(N_DEV, 8, LANE)], "scatter_mod", gather=False)
    mod = mod_parts.reshape(N_DEV, D)[:, :ada_cols].reshape(6, D)
    mod = jnp.pad(mod, ((0, 2), (0, 0)))

    later, mod = lax.optimization_barrier((shards[1:], mod))
    gather_parts, token = exchange_start(later, "gather_weights_start", gather=True)
    mod = mod + token[0:1, 0:1]

    small = {n: to2d(weights[n]) for n in names if n not in ("ada_w",) + big}
    small["ssd_conv_w"] = _slabs_to_cols(scw_all)
    small["conf_conv_w"] = _slabs_to_cols(ccw_all)
    small["ffn_conv_w"] = _slabs_to_cols(fcw_all)

    def with_own(landed, own):
        return lax.dynamic_update_slice(landed, own[None], (me,) + (0,) * own.ndim)

    def get_w(n, after):
        if n == "w_in":
            slabs = w_in_slabs
        else:
            a = big.index(n)
            landed, = exchange_wait([gather_parts[a - 1]], after, "gather_" + n + "_wait", gather=True)
            slabs = with_own(landed, shards[a])
        full = slabs.reshape(N_DEV * slabs.shape[1], D)
        if n != "w_in":
            return full
        w_dt = full[D_SSD + D_XBC:D_SSD + D_XBC + N_HEADS]
        return dict(w_full=full, w_xbc=full[D_SSD:D_SSD + D_XBC], w_cf=full[D_SSD + D_XBC + N_HEADS:],
                    w_dt_rep=jnp.repeat(w_dt, HEAD, axis=0), w_dt16=jnp.pad(w_dt, ((0, LANE - N_HEADS), (0, 0))))

    scatter_parts, sent = {}, {}

    def put_grad(n, g):
        sent[n] = g.reshape(N_DEV, g.shape[0] // N_DEV, g.shape[1]).astype(BF16)
        (scatter_parts[n],), token = exchange_start([sent[n]], "scatter_" + n + "_start", gather=False)
        return token

    grad_x, accs = local_fwd_bwd(x[0], loss_target[0], mod, get_w, put_grad, small)
    loss = lax.psum(0.5 / D * jnp.sum(accs["acc_f"][2:3]), ("x", "y", "c"))

    grads, delta, new_m, new_v = {}, {}, {}, {}

    def finish(ns, after, name):
        landed = exchange_wait([scatter_parts[n] for n in ns], after, name, gather=False)
        for n, slots in zip(ns, landed):
            slots = with_own(slots, lax.dynamic_index_in_dim(sent[n], me, 0, keepdims=False))
            out = adamw_slots(rows_of(weights[n]), slots, rows_of(moms_m[n]), rows_of(moms_v[n]), "adamw_" + n)
            if weights[n].shape[2] != D:
                out = [jnp.swapaxes(o, 0, 1) for o in out]
            grads[n], delta[n], new_m[n], new_v[n] = out

    finish(big[1:], grad_x, "scatter_grads_wait")

    accs = dict(zip(accs, lax.optimization_barrier((list(accs.values()), [new_v[n] for n in big[1:]]))[0]))
    rep = (("acc_1", 0, 3), ("acc_2", 0, 4), ("acc_f", 0, 2), ("acc_ln", 0, 2), ("acc_s", 0, 1), ("acc_s16", 1, 3),
           ("dw_sc", K_SSD, 1), ("dw_cc", K_CONF, 1), ("dwg", K_FFN, 1), ("dwv", K_FFN, 1))
    shapes = [(rows, accs[k].shape[1]) for k, _, rows in rep]
    conv_slabs = [_cols_to_slabs(accs["dw_sc"][:K_SSD]), _cols_to_slabs(accs["dw_cc"][:K_CONF]),
                  _cols_to_slabs(jnp.concatenate([accs["dwg"][:K_FFN], accs["dwv"][:K_FFN]], axis=1))]
    packed = pack_rows([(accs[k], first, rows) for k, first, rows in rep], "pack_small_grads")
    landed = exchange([packed] + conv_slabs, "exchange_small_grads", gather=[True, False, False, False])
    packed_red, g_scw, g_ccw, g_fcw = sum_slots_many(landed, "sum_small_grads")
    a1_all, a2_all, af_all = unpack_rows(landed[0], shapes)[:3]
    r1, r2, rf, rln, rs, r16, rscb, rccb, rfbg, rfbv = unpack_rows(packed_red, shapes)

    def mod_rows(a1, a2, af):
        return jnp.concatenate([a1[..., 0:2, :], a2[..., 3:4, :], a2[..., 0:2, :], af[..., 1:2, :]], axis=-2)

    dmod_all = mod_rows(a1_all, a2_all, af_all).reshape(N_DEV, 6 * D)
    grads["ada_w"] = ada_wgrad(c_act_all, lax.dynamic_slice(dmod_all, (0, me * ada_cols), (N_DEV, ada_cols)), "ada_wgrad")
    grads.update(
        ada_b=mod_rows(r1, r2, rf).reshape(1, 6 * D), norm1_w=r1[2:3], ssd_conv_w=g_scw, ssd_conv_b=rscb,
        dt_bias=r16[0:1, :N_HEADS], a_log=r16[1:2, :N_HEADS], d_skip=r16[2:3, :N_HEADS], ssd_norm_w=rs,
        conf_conv_w=g_ccw, conf_conv_b=rccb, conf_ln_w=rln[0:1], conf_ln_b=rln[1:2], norm2_w=r2[2:3],
        ffn_conv_w=g_fcw, ffn_conv_b=jnp.concatenate([rfbg, rfbv], axis=1), final_norm_w=rf[0:1])

    rest = [n for n in names if n not in big]
    d_l, m_l, v_l = adamw_many([to2d(weights[n]) for n in rest], [grads[n] for n in rest], [to2d(moms_m[n]) for n in rest],
                               [to2d(moms_v[n]) for n in rest], "adamw_small")
    for n, dd, mm, vv in zip(rest, d_l, m_l, v_l):
        delta[n], new_m[n], new_v[n] = dd, mm, vv
    finish(big[:1], d_l[0], "scatter_w_in_wait")
    shape_of = lambda d_: {n: d_[n].reshape(weights[n].shape) for n in names}
    grads, delta, new_m, new_v = shape_of(grads), shape_of(delta), shape_of(new_m), shape_of(new_v)
    return (loss, grad_x[None], *[grads[n] for n in names], *[delta[n] for n in names], *[new_m[n] for n in names],
            *[new_v[n] for n in names])
```

```python
import functools

import jax
import jax.numpy as jnp
from jax import lax
from jax.experimental import pallas as pl
from jax.experimental.pallas import tpu as pltpu

F32 = jnp.float32
BF16 = jnp.bfloat16
HI = lax.Precision.HIGHEST

N_DEV = 8
D = 1024
D_SSD = 1024
HEAD = 64
N_HEADS = 16
N_STATE = 128
D_XBC = 1536
D_CONF = 1024
D_FF = 2816
K_SSD, K_CONF, K_FFN = 4, 31, 3
D_INP = 5632
LANE = 128
TR = 256
TM = 512
Q = 256
CB = 256
TC = 1024
VMEM_LIMIT = 56 * 1024 * 1024

ADAM_LR, ADAM_B1, ADAM_B2, ADAM_EPS, ADAM_WD, ADAM_STEP = 0.001, 0.9, 0.999, 1e-08, 0.01, 10


def _cparams(sem=None):
    return pltpu.CompilerParams(vmem_limit_bytes=VMEM_LIMIT, dimension_semantics=sem)


def _sds(shape, dtype):
    return jax.ShapeDtypeStruct(shape, dtype)


def _sigmoid(x):
    return 1.0 / (1.0 + jnp.exp(-x))


def _silu(x):
    return x * _sigmoid(x)


def _dsilu(x):
    s = _sigmoid(x)
    return s * (1.0 + x * (1.0 - s))


def _softplus(x):
    return jnp.maximum(x, 0.0) + jnp.log(1.0 + jnp.exp(-jnp.abs(x)))


def _dot(a, b):
    return jnp.dot(a.astype(BF16), b.astype(BF16), preferred_element_type=F32)


def _dot_nt(a, b):
    return lax.dot_general(a.astype(BF16), b.astype(BF16), (((1,), (1,)), ((), ())), preferred_element_type=F32)


def _dot_tn(a, b):
    return lax.dot_general(a.astype(BF16), b.astype(BF16), (((0,), (0,)), ((), ())), preferred_element_type=F32)


def _bf16_terms(a, terms):
    parts, rem = [], a
    for t in range(terms):
        p = rem.astype(BF16)
        parts.append(p)
        if t + 1 < terms:
            rem = rem - p.astype(F32)
    return parts


def _dot_exact(a, b, terms, exact, dims=(((1,), (0,)), ((), ()))):
    if exact == "a":
        a_b = a.astype(BF16)
        outs = [lax.dot_general(a_b, p, dims, preferred_element_type=F32) for p in _bf16_terms(b, terms)]
    else:
        b_b = b.astype(BF16)
        outs = [lax.dot_general(p, b_b, dims, preferred_element_type=F32) for p in _bf16_terms(a, terms)]
    acc = outs[-1]
    for o in reversed(outs[:-1]):
        acc = acc + o
    return acc


def _dot_tn_hi(a, b):
    return lax.dot_general(a, b, (((0,), (0,)), ((), ())), precision=HI, preferred_element_type=F32)


def _colsum(x):
    return jnp.sum(x, axis=0, keepdims=True)


def _const_spec(shape):
    return pl.BlockSpec(shape, lambda *_: (0,) * len(shape))


def _col_tile(n):
    for t in (1408, 1024, 768, 512, 256, 128):
        if n % t == 0 and t <= n:
            return t
    return n


def mm_nt(pairs, name):
    L = pairs[0][0].shape[0]
    K = pairs[0][1].shape[0]
    tk = _col_tile(K)
    n = len(pairs)

    def body(*refs):
        o_ref = refs[-1]
        acc = None
        for p in range(n):
            t = lax.dot_general(refs[2 * p][...], refs[2 * p + 1][...], (((1,), (1,)), ((), ())),
                                preferred_element_type=F32)
            acc = t if acc is None else acc + t
        o_ref[...] = acc

    in_specs, args = [], []
    for a, w, cb in pairs:
        in_specs += [pl.BlockSpec((TM, a.shape[1]), lambda j, i: (i, 0)),
                     pl.BlockSpec((tk, a.shape[1]), functools.partial(lambda j, i, cb: (j, cb), cb=cb))]
        args += [a, w]
    return pl.pallas_call(
        body, name=name, grid=(K // tk, L // TM), out_shape=_sds((L, K), F32), in_specs=in_specs,
        out_specs=pl.BlockSpec((TM, tk), lambda j, i: (i, j)),
        compiler_params=_cparams(("parallel", "parallel")))(*args)


def mm_tn(a, g, name):
    L, M = a.shape
    N = g.shape[1]
    tn = _col_tile(N) if N > 1024 else N
    if M * tn * 4 > 12 * 1024 * 1024:
        tn = 512
    tl = 512 if L % 512 == 0 else TR
    nl = L // tl

    def body(a_ref, g_ref, o_ref, acc_ref):
        @pl.when(pl.program_id(1) == 0)
        def _():
            acc_ref[...] = jnp.zeros((M, tn), F32)

        acc_ref[...] += lax.dot_general(a_ref[...], g_ref[...], (((0,), (0,)), ((), ())), preferred_element_type=F32)

        @pl.when(pl.program_id(1) == nl - 1)
        def _():
            o_ref[...] = acc_ref[...].astype(BF16)

    return pl.pallas_call(
        body, name=name, grid=(N // tn, nl), out_shape=_sds((M, N), BF16),
        in_specs=[pl.BlockSpec((tl, M), lambda j, l: (l, 0)), pl.BlockSpec((tl, tn), lambda j, l: (l, j))],
        out_specs=pl.BlockSpec((M, tn), lambda j, l: (0, j)), scratch_shapes=[pltpu.VMEM((M, tn), F32)],
        compiler_params=_cparams(("parallel", "arbitrary")))(a, g)


def mm_tn_stack(a_list, g, name):
    L, M = a_list[0].shape
    N = g.shape[1]
    n = len(a_list)
    tl = 512 if L % 512 == 0 else TR
    nl = L // tl

    def body(*refs):
        a_refs, g_ref, o_ref, acc_ref = refs[:n], refs[n], refs[n + 1], refs[n + 2]
        j, l = pl.program_id(0), pl.program_id(1)

        @pl.when(l == 0)
        def _():
            acc_ref[...] = jnp.zeros((M, N), F32)

        for p in range(n):
            @pl.when(j == p)
            def _(p=p):
                acc_ref[...] += lax.dot_general(a_refs[p][...], g_ref[...], (((0,), (0,)), ((), ())), preferred_element_type=F32)

        @pl.when(l == nl - 1)
        def _():
            o_ref[...] = acc_ref[...].astype(BF16)

    a_specs = [pl.BlockSpec((tl, M), functools.partial(lambda j, l, p: (jnp.where(j == p, l, 0), 0), p=p)) for p in range(n)]
    return pl.pallas_call(
        body, name=name, grid=(n, nl), out_shape=_sds((n * M, N), BF16),
        in_specs=a_specs + [pl.BlockSpec((tl, N), lambda j, l: (l, 0))],
        out_specs=pl.BlockSpec((M, N), lambda j, l: (j, 0)), scratch_shapes=[pltpu.VMEM((M, N), F32)],
        compiler_params=_cparams(("arbitrary", "arbitrary")))(*a_list, g)


def mm_tn_concat(pieces, g, name):
    L = g.shape[0]
    N = g.shape[1]
    n = len(pieces)
    offsets = [sum(r for _, r in pieces[:p]) for p in range(n + 1)]
    slab = offsets[-1] // N_DEV
    tl = 512 if L % 512 == 0 else TR
    nl = L // tl

    def body(*refs):
        a_refs, g_ref, o_ref, acc_ref = refs[:n], refs[n], refs[n + 1], refs[n + 2]
        l = pl.program_id(0)

        @pl.when(l == 0)
        def _():
            acc_ref[...] = jnp.zeros((offsets[-1], N), F32)

        g_v = g_ref[...]
        for p in range(n):
            t = lax.dot_general(a_refs[p][...], g_v, (((0,), (0,)), ((), ())), preferred_element_type=F32)
            acc_ref[offsets[p]:offsets[p + 1], :] += t[:pieces[p][1], :]

        @pl.when(l == nl - 1)
        def _():
            for s in range(N_DEV):
                o_ref[s] = acc_ref[s * slab:(s + 1) * slab, :].astype(BF16)

    return pl.pallas_call(
        body, name=name, grid=(nl,), out_shape=_sds((N_DEV, slab, N), BF16),
        in_specs=[pl.BlockSpec((tl, a.shape[1]), lambda l: (l, 0)) for a, _ in pieces] + [pl.BlockSpec((tl, N), lambda l: (l, 0))],
        out_specs=_const_spec((N_DEV, slab, N)), scratch_shapes=[pltpu.VMEM((offsets[-1], N), F32)],
        compiler_params=_cparams(("arbitrary",)))(*[a for a, _ in pieces], g)


def _row_spec(width=D):
    return pl.BlockSpec((TR, width), lambda i: (i, 0))


def in_proj(x, mod, n1w, w_full, w_dt_rep, w_cf, name):
    L = x.shape[0]
    n_zx = D_SSD + D_XBC

    def body(x_ref, mod_ref, w_ref, wzx_ref, wdt_ref, wcf_ref, h_ref, zx_ref, dt_ref, cf_ref):
        xin = x_ref[...]
        r = lax.rsqrt(jnp.mean(xin * xin, axis=-1, keepdims=True) + 1e-6)
        h = ((xin * r * w_ref[...]) * (1.0 + mod_ref[1:2, :]) + mod_ref[0:1, :]).astype(BF16)
        h_ref[...] = h
        nt = (((1,), (1,)), ((), ()))
        zx_ref[...] = lax.dot_general(h, wzx_ref[...], nt, preferred_element_type=F32)
        dt_ref[...] = lax.dot_general(h, wdt_ref[...], nt, preferred_element_type=F32)
        cf_ref[...] = lax.dot_general(h, wcf_ref[...], nt, preferred_element_type=F32)

    row = lambda w: pl.BlockSpec((TM, w), lambda i: (i, 0))
    return pl.pallas_call(
        body, name=name, grid=(L // TM,),
        out_shape=[_sds((L, D), BF16), _sds((L, n_zx), F32), _sds((L, D_SSD), F32), _sds((L, 2 * D_CONF), F32)],
        in_specs=[row(D), _const_spec((8, D)), _const_spec((1, D)), _const_spec((n_zx, D)), _const_spec((D_SSD, D)),
                  _const_spec((2 * D_CONF, D))],
        out_specs=[row(D), row(n_zx), row(D_SSD), row(2 * D_CONF)],
        compiler_params=_cparams(("parallel",)))(x, mod, n1w, w_full, w_dt_rep, w_cf)


def mixer_out(ysn, uc, lnw, lnb, w_out, x, mod, n2w, name):
    L = x.shape[0]

    def body(ysn_ref, uc_ref, lnw_ref, lnb_ref, wo_ref, x_ref, mod_ref, n2w_ref, mix_ref, u_ref, x1_ref, h2_ref):
        uc_v = uc_ref[...]
        mu = jnp.mean(uc_v, axis=-1, keepdims=True)
        var = jnp.mean(jnp.square(uc_v - mu), axis=-1, keepdims=True)
        u = _silu((uc_v - mu) * lax.rsqrt(var + 1e-5) * lnw_ref[...] + lnb_ref[...]).astype(BF16)
        u_ref[...] = u
        mix = (jnp.dot(ysn_ref[...], wo_ref[0:D_SSD, :], preferred_element_type=F32)
               + jnp.dot(u, wo_ref[D_SSD:D_SSD + D_CONF, :], preferred_element_type=F32))
        mix_ref[...] = mix
        x1 = x_ref[...] + mod_ref[2:3, :] * mix
        x1_ref[...] = x1
        r = lax.rsqrt(jnp.mean(x1 * x1, axis=-1, keepdims=True) + 1e-6)
        h2_ref[...] = ((x1 * r * n2w_ref[...]) * (1.0 + mod_ref[4:5, :]) + mod_ref[3:4, :]).astype(BF16)

    return pl.pallas_call(
        body, name=name, grid=(L // TR,),
        out_shape=[_sds((L, D), F32), _sds((L, D_CONF), BF16), _sds((L, D), F32), _sds((L, D), BF16)],
        in_specs=[_row_spec(), _row_spec(), _const_spec((1, D)), _const_spec((1, D)), _const_spec((D_SSD + D_CONF, D)), _row_spec(),
                  _const_spec((8, D)), _const_spec((1, D))],
        out_specs=[_row_spec()] * 4, compiler_params=_cparams(("parallel",)))(ysn, uc, lnw, lnb, w_out, x, mod, n2w)


def mixer_out_bwd(dmix, w_out, uc, lnw, lnb, name):
    L = uc.shape[0]

    def body(dm_ref, wo_ref, u_ref, w_ref, b_ref, dy_ref, o_ref, acc_ref):
        @pl.when(pl.program_id(0) == 0)
        def _():
            acc_ref[...] = jnp.zeros((8, D), F32)

        nt = (((1,), (1,)), ((), ()))
        dm = dm_ref[...]
        dy_ref[...] = lax.dot_general(dm, wo_ref[0:D_SSD, :], nt, preferred_element_type=F32)
        du = lax.dot_general(dm, wo_ref[D_SSD:D_SSD + D_CONF, :], nt, preferred_element_type=F32)
        u = u_ref[...]
        mu = jnp.mean(u, axis=-1, keepdims=True)
        rl = lax.rsqrt(jnp.mean(jnp.square(u - mu), axis=-1, keepdims=True) + 1e-5)
        n = (u - mu) * rl
        v = n * w_ref[...] + b_ref[...]
        dv = du * _dsilu(v)
        acc_ref[0:1, :] += _colsum(dv * n)
        acc_ref[1:2, :] += _colsum(dv)
        dn = dv * w_ref[...]
        o_ref[...] = rl * (dn - jnp.mean(dn, axis=-1, keepdims=True) - n * jnp.mean(dn * n, axis=-1, keepdims=True))

    return pl.pallas_call(body, name=name, grid=(L // TR,), out_shape=[_sds((L, D), F32), _sds((L, D), F32), _sds((8, D), F32)],
                          in_specs=[_row_spec(), _const_spec((D_SSD + D_CONF, D)), _row_spec(), _const_spec((1, D)),
                                    _const_spec((1, D))],
                          out_specs=[_row_spec(), _row_spec(), _const_spec((8, D))],
                          compiler_params=_cparams(("arbitrary",)))(dmix, w_out, uc, lnw, lnb)


def final_loss(act, w_down, x1, mod, fw, target, name):
    L = act.shape[0]

    def body(act_ref, wd_ref, x1_ref, mod_ref, fw_ref, t_ref, dx_ref, dff_ref, dact_ref, acc_ref):
        @pl.when(pl.program_id(0) == 0)
        def _():
            acc_ref[...] = jnp.zeros((8, D), F32)

        ff_v = jnp.dot(act_ref[...], wd_ref[...], preferred_element_type=F32)
        g2 = mod_ref[5:6, :]
        x2 = x1_ref[...] + g2 * ff_v
        r = lax.rsqrt(jnp.mean(x2 * x2, axis=-1, keepdims=True) + 1e-6)
        n = x2 * r
        err = n * fw_ref[...] - t_ref[...]
        dy = err * (1.0 / D)
        dn = dy * fw_ref[...]
        dx2 = r * (dn - n * jnp.mean(dn * n, axis=-1, keepdims=True))
        acc_ref[0:1, :] += _colsum(dy * n)
        acc_ref[1:2, :] += _colsum(dx2 * ff_v)
        acc_ref[2:3, :] += _colsum(err * err)
        dx_ref[...] = dx2
        dff = (dx2 * g2).astype(BF16)
        dff_ref[...] = dff
        dact_ref[...] = lax.dot_general(dff, wd_ref[...], (((1,), (1,)), ((), ())), preferred_element_type=F32)

    return pl.pallas_call(
        body, name=name, grid=(L // TR,),
        out_shape=[_sds((L, D), F32), _sds((L, D), BF16), _sds((L, D_FF), F32), _sds((8, D), F32)],
        in_specs=[_row_spec(D_FF), _const_spec((D_FF, D)), _row_spec(), _const_spec((8, D)), _const_spec((1, D)), _row_spec()],
        out_specs=[_row_spec(), _row_spec(), _row_spec(D_FF), _const_spec((8, D))],
        compiler_params=_cparams(("arbitrary",)))(act, w_down, x1, mod, fw, target)


def norm_mod_bwd(dh_pairs, xin, dres, mod, w, shift_row, name, mix=None, gate_row=None):
    L = xin.shape[0]
    has_mix = mix is not None
    n_pairs = len(dh_pairs)

    def body(*refs):
        pair_refs, refs = refs[:2 * n_pairs], refs[2 * n_pairs:]
        if has_mix:
            x_ref, dres_ref, mod_ref, w_ref, mix_ref, dx_ref, dmix_ref, acc_ref = refs
        else:
            x_ref, dres_ref, mod_ref, w_ref, dx_ref, acc_ref = refs

        @pl.when(pl.program_id(0) == 0)
        def _():
            acc_ref[...] = jnp.zeros((8, D), F32)

        dh_v = None
        for p in range(n_pairs):
            t = jnp.dot(pair_refs[2 * p][...], pair_refs[2 * p + 1][...], preferred_element_type=F32)
            dh_v = t if dh_v is None else dh_v + t
        x = x_ref[...]
        r = lax.rsqrt(jnp.mean(x * x, axis=-1, keepdims=True) + 1e-6)
        n = x * r
        nw = n * w_ref[...]
        sc1 = 1.0 + mod_ref[shift_row + 1:shift_row + 2, :]
        acc_ref[0:1, :] += _colsum(dh_v)
        acc_ref[1:2, :] += _colsum(dh_v * nw)
        dnw = dh_v * sc1
        acc_ref[2:3, :] += _colsum(dnw * n)
        dn = dnw * w_ref[...]
        dx = r * (dn - n * jnp.mean(dn * n, axis=-1, keepdims=True)) + dres_ref[...]
        dx_ref[...] = dx
        if has_mix:
            acc_ref[3:4, :] += _colsum(dx * mix_ref[...])
            dmix_ref[...] = (dx * mod_ref[gate_row:gate_row + 1, :]).astype(BF16)

    ins, in_specs = [], []
    for a, wt, rb in dh_pairs:
        ins += [a, wt]
        in_specs += [_row_spec(a.shape[1]), pl.BlockSpec((a.shape[1], D), functools.partial(lambda i, rb: (rb, 0), rb=rb))]
    ins += [xin, dres, mod, w] + ([mix] if has_mix else [])
    in_specs += [_row_spec(), _row_spec(), _const_spec((8, D)), _const_spec((1, D))] + ([_row_spec()] if has_mix else [])
    out_shape = [_sds((L, D), F32)] + ([_sds((L, D), BF16)] if has_mix else []) + [_sds((8, D), F32)]
    out_specs = [_row_spec()] + ([_row_spec()] if has_mix else []) + [_const_spec((8, D))]
    return pl.pallas_call(body, name=name, grid=(L // TR,), out_shape=out_shape, in_specs=in_specs,
                          out_specs=out_specs, compiler_params=_cparams(("arbitrary",)))(*ins)


def _halo(k):
    return 8 if k <= 9 else 32


def _prev_spec(h, col0):
    return pl.BlockSpec((h, CB), lambda j, i: (jnp.maximum(i * (TC // h) - 1, 0), j + col0))


def _next_spec(h, col0, n_tiles):
    return pl.BlockSpec((h, CB), lambda j, i: (jnp.minimum(i + 1, n_tiles - 1) * (TC // h), j + col0))


def _tile_spec(col0):
    return pl.BlockSpec((TC, CB), lambda j, i: (i, j + col0))


def _w_spec(kp, col0):
    return pl.BlockSpec((kp, CB), lambda j, i: (0, j + col0))


SUBLANES = 8


def _shifted_windows(v, taps, rows):
    for r in range(SUBLANES):
        group = [(o, k) for o, k in taps if o % SUBLANES == r]
        if not group:
            continue
        s = v if r == 0 else pltpu.roll(v, v.shape[0] - r, 0)
        for o, k in group:
            yield k, s[o - r:o - r + rows, :]


def _causal_taps(ext_ref, w_ref, k_taps, first, rows):
    acc = None
    for k, win in _shifted_windows(ext_ref[...], [(first - (k_taps - 1) + k, k) for k in range(k_taps)], rows):
        t = w_ref[k:k + 1, :] * win
        acc = t if acc is None else acc + t
    return acc


def _anticausal_taps(d_ref, w_ref, k_taps, rows):
    acc = None
    for k, win in _shifted_windows(d_ref[...], [(k_taps - 1 - k, k) for k in range(k_taps)], rows):
        t = w_ref[k:k + 1, :] * win
        acc = t if acc is None else acc + t
    return acc


def _acc_conv_wgrad(dw_ref, d_tile, ext_ref, k_taps, first):
    for k, win in _shifted_windows(ext_ref[...], [(first - (k_taps - 1) + k, k) for k in range(k_taps)], TC):
        dw_ref[k:k + 1, :] += _colsum(d_tile * win)
    dw_ref[k_taps:k_taps + 1, :] += _colsum(d_tile)


def conv_silu_fwd(x, col0, width, w, b, name):
    L = x.shape[0]
    k_taps = w.shape[0]
    h = _halo(k_taps)

    def body(xp_ref, x_ref, w_ref, b_ref, o_ref, ext_ref):
        i = pl.program_id(1)
        ext_ref[0:h, :] = jnp.where(i > 0, xp_ref[...], 0.0)
        ext_ref[h:h + TC, :] = x_ref[...]
        o_ref[...] = _silu(_causal_taps(ext_ref, w_ref, k_taps, h, TC) + b_ref[...])

    return pl.pallas_call(
        body, name=name, grid=(width // CB, L // TC), out_shape=_sds((L, width), F32),
        in_specs=[_prev_spec(h, col0), _tile_spec(col0), _w_spec(k_taps, 0), pl.BlockSpec((1, CB), lambda j, i: (0, j))],
        out_specs=_tile_spec(0), scratch_shapes=[pltpu.VMEM((h + TC, CB), F32)],
        compiler_params=_cparams(("parallel", "parallel")))(x, x, w, b)


def conv_silu_bwd(x, col0, width, w, b, dpost, name):
    L = x.shape[0]
    k_taps = w.shape[0]
    h = _halo(k_taps)
    nt = L // TC

    def body(xp_ref, x_ref, xn_ref, d_ref, dn_ref, w_ref, b_ref, dx_ref, dw_ref, ext_ref, dpre_ref):
        i = pl.program_id(1)

        @pl.when(i == 0)
        def _():
            dw_ref[...] = jnp.zeros((8, CB), F32)

        ext_ref[0:h, :] = jnp.where(i > 0, xp_ref[...], 0.0)
        ext_ref[h:h + TC, :] = x_ref[...]
        ext_ref[h + TC:h + TC + h, :] = xn_ref[...]
        pre = _causal_taps(ext_ref, w_ref, k_taps, h, TC + h) + b_ref[...]
        dpre_ref[0:TC, :] = d_ref[...] * _dsilu(pre[0:TC, :])
        dpre_ref[TC:TC + h, :] = jnp.where(i < nt - 1, dn_ref[...], 0.0) * _dsilu(pre[TC:TC + h, :])
        dx_ref[...] = _anticausal_taps(dpre_ref, w_ref, k_taps, TC).astype(BF16)
        _acc_conv_wgrad(dw_ref, dpre_ref[0:TC, :], ext_ref, k_taps, h)

    return pl.pallas_call(
        body, name=name, grid=(width // CB, nt),
        out_shape=[_sds((L, width), BF16), _sds((8, width), F32)],
        in_specs=[_prev_spec(h, col0), _tile_spec(col0), _next_spec(h, col0, nt), _tile_spec(0), _next_spec(h, 0, nt),
                  _w_spec(k_taps, 0), pl.BlockSpec((1, CB), lambda j, i: (0, j))],
        out_specs=[_tile_spec(0), _w_spec(8, 0)],
        scratch_shapes=[pltpu.VMEM((h + TC + h, CB), F32), pltpu.VMEM((TC + h, CB), F32)],
        compiler_params=_cparams(("parallel", "arbitrary")))(x, x, x, dpost, dpost, w, b)


def conf_conv_fwd(proj, col_a, col_g, w, b, name):
    L = proj.shape[0]
    k_taps = w.shape[0]
    h = _halo(k_taps)

    def body(ap_ref, a_ref, gp_ref, g_ref, w_ref, b_ref, o_ref, ext_ref):
        i = pl.program_id(1)
        ext_ref[0:h, :] = jnp.where(i > 0, ap_ref[...] * _sigmoid(gp_ref[...]), 0.0)
        ext_ref[h:h + TC, :] = a_ref[...] * _sigmoid(g_ref[...])
        o_ref[...] = _causal_taps(ext_ref, w_ref, k_taps, h, TC) + b_ref[...]

    return pl.pallas_call(
        body, name=name, grid=(D_CONF // CB, L // TC), out_shape=_sds((L, D_CONF), F32),
        in_specs=[_prev_spec(h, col_a), _tile_spec(col_a), _prev_spec(h, col_g), _tile_spec(col_g), _w_spec(k_taps, 0),
                  pl.BlockSpec((1, CB), lambda j, i: (0, j))],
        out_specs=_tile_spec(0), scratch_shapes=[pltpu.VMEM((h + TC, CB), F32)],
        compiler_params=_cparams(("parallel", "parallel")))(proj, proj, proj, proj, w, b)


def conf_conv_bwd(proj, col_a, col_g, w, duc, name):
    L = proj.shape[0]
    k_taps = w.shape[0]
    h = _halo(k_taps)
    nt = L // TC

    def body(ap_ref, a_ref, gp_ref, g_ref, d_ref, dn_ref, w_ref, da_ref, dg_ref, dw_ref, ext_ref, dext_ref):
        i = pl.program_id(1)

        @pl.when(i == 0)
        def _():
            dw_ref[...] = jnp.zeros((32, CB), F32)

        a = a_ref[...]
        s = _sigmoid(g_ref[...])
        ext_ref[0:h, :] = jnp.where(i > 0, ap_ref[...] * _sigmoid(gp_ref[...]), 0.0)
        ext_ref[h:h + TC, :] = a * s
        dext_ref[0:TC, :] = d_ref[...]
        dext_ref[TC:TC + h, :] = jnp.where(i < nt - 1, dn_ref[...], 0.0)
        du0 = _anticausal_taps(dext_ref, w_ref, k_taps, TC)
        da_ref[...] = (du0 * s).astype(BF16)
        dg_ref[...] = (du0 * a * s * (1.0 - s)).astype(BF16)
        _acc_conv_wgrad(dw_ref, d_ref[...], ext_ref, k_taps, h)

    return pl.pallas_call(
        body, name=name, grid=(D_CONF // CB, nt),
        out_shape=[_sds((L, D_CONF), BF16), _sds((L, D_CONF), BF16), _sds((32, D_CONF), F32)],
        in_specs=[_prev_spec(h, col_a), _tile_spec(col_a), _prev_spec(h, col_g), _tile_spec(col_g), _tile_spec(0),
                  _next_spec(h, 0, nt), _w_spec(k_taps, 0)],
        out_specs=[_tile_spec(0), _tile_spec(0), _w_spec(32, 0)],
        scratch_shapes=[pltpu.VMEM((h + TC, CB), F32), pltpu.VMEM((TC + h, CB), F32)],
        compiler_params=_cparams(("parallel", "arbitrary")))(proj, proj, proj, proj, duc, duc, w)


def ffn_conv_fwd(up, w, b, name):
    L = up.shape[0]
    k_taps = w.shape[0]
    h = _halo(k_taps)
    cv = D_FF // CB

    def body(gp_ref, g_ref, vp_ref, v_ref, wg_ref, wv_ref, bg_ref, bv_ref, o_ref, eg_ref, ev_ref):
        i = pl.program_id(1)
        eg_ref[0:h, :] = jnp.where(i > 0, gp_ref[...], 0.0)
        eg_ref[h:h + TC, :] = g_ref[...]
        ev_ref[0:h, :] = jnp.where(i > 0, vp_ref[...], 0.0)
        ev_ref[h:h + TC, :] = v_ref[...]
        pg = _causal_taps(eg_ref, wg_ref, k_taps, h, TC) + bg_ref[...]
        pv = _causal_taps(ev_ref, wv_ref, k_taps, h, TC) + bv_ref[...]
        o_ref[...] = (_silu(pg) * pv).astype(BF16)

    bspec = lambda c0: pl.BlockSpec((1, CB), lambda j, i: (0, j + c0))
    return pl.pallas_call(
        body, name=name, grid=(cv, L // TC), out_shape=_sds((L, D_FF), BF16),
        in_specs=[_prev_spec(h, 0), _tile_spec(0), _prev_spec(h, cv), _tile_spec(cv), _w_spec(k_taps, 0), _w_spec(k_taps, cv),
                  bspec(0), bspec(cv)],
        out_specs=_tile_spec(0), scratch_shapes=[pltpu.VMEM((h + TC, CB), F32), pltpu.VMEM((h + TC, CB), F32)],
        compiler_params=_cparams(("parallel", "parallel")))(up, up, up, up, w, w, b, b)


def ffn_conv_bwd(up, w, b, dact, name):
    L = up.shape[0]
    k_taps = w.shape[0]
    h = _halo(k_taps)
    nt = L // TC
    cv = D_FF // CB

    def body(gp_ref, g_ref, gn_ref, vp_ref, v_ref, vn_ref, d_ref, dn_ref, wg_ref, wv_ref, bg_ref, bv_ref,
             dg_ref, dv_ref, dwg_ref, dwv_ref, eg_ref, ev_ref, pg_ref, pv_ref):
        i = pl.program_id(1)

        @pl.when(i == 0)
        def _():
            dwg_ref[...] = jnp.zeros((8, CB), F32)
            dwv_ref[...] = jnp.zeros((8, CB), F32)

        for e_ref, p_ref, c_ref, n_ref in ((eg_ref, gp_ref, g_ref, gn_ref), (ev_ref, vp_ref, v_ref, vn_ref)):
            e_ref[0:h, :] = jnp.where(i > 0, p_ref[...], 0.0)
            e_ref[h:h + TC, :] = c_ref[...]
            e_ref[h + TC:h + TC + h, :] = n_ref[...]
        pg = _causal_taps(eg_ref, wg_ref, k_taps, h, TC + h) + bg_ref[...]
        pv = _causal_taps(ev_ref, wv_ref, k_taps, h, TC + h) + bv_ref[...]
        dact_t = d_ref[...]
        dact_n = jnp.where(i < nt - 1, dn_ref[...], 0.0)
        pg_ref[0:TC, :] = dact_t * pv[0:TC, :] * _dsilu(pg[0:TC, :])
        pg_ref[TC:TC + h, :] = dact_n * pv[TC:TC + h, :] * _dsilu(pg[TC:TC + h, :])
        pv_ref[0:TC, :] = dact_t * _silu(pg[0:TC, :])
        pv_ref[TC:TC + h, :] = dact_n * _silu(pg[TC:TC + h, :])
        dg_ref[...] = _anticausal_taps(pg_ref, wg_ref, k_taps, TC).astype(BF16)
        dv_ref[...] = _anticausal_taps(pv_ref, wv_ref, k_taps, TC).astype(BF16)
        _acc_conv_wgrad(dwg_ref, pg_ref[0:TC, :], eg_ref, k_taps, h)
        _acc_conv_wgrad(dwv_ref, pv_ref[0:TC, :], ev_ref, k_taps, h)

    bspec = lambda c0: pl.BlockSpec((1, CB), lambda j, i: (0, j + c0))
    ext = pltpu.VMEM((h + TC + h, CB), F32)
    dpre = pltpu.VMEM((TC + h, CB), F32)
    return pl.pallas_call(
        body, name=name, grid=(cv, nt),
        out_shape=[_sds((L, D_FF), BF16), _sds((L, D_FF), BF16), _sds((8, D_FF), F32), _sds((8, D_FF), F32)],
        in_specs=[_prev_spec(h, 0), _tile_spec(0), _next_spec(h, 0, nt), _prev_spec(h, cv), _tile_spec(cv), _next_spec(h, cv, nt),
                  _tile_spec(0), _next_spec(h, 0, nt), _w_spec(k_taps, 0), _w_spec(k_taps, cv), bspec(0), bspec(cv)],
        out_specs=[_tile_spec(0), _tile_spec(0), _w_spec(8, 0), _w_spec(8, 0)],
        scratch_shapes=[ext, ext, dpre, dpre],
        compiler_params=_cparams(("parallel", "arbitrary")))(up, up, up, up, up, up, dact, dact, w, w, b, b)


def _ssd_common(xbc_ref, dt_ref, dtb_ref, alog_ref, cs_ref):
    xs = xbc_ref[:, 0:D_SSD]
    sp_in = dt_ref[...] + dtb_ref[...]
    dtf = _softplus(sp_in)
    a_f = -jnp.exp(alog_ref[...])
    a_dt = dtf * a_f
    row = lax.broadcasted_iota(jnp.int32, (Q, Q), 0)
    col = lax.broadcasted_iota(jnp.int32, (Q, Q), 1)
    causal = row >= col
    cs = _dot_exact(causal.astype(F32), a_dt, 3, "a")
    cs_ref[...] = cs
    cs_last = cs_ref[Q - 1:Q, :]
    return xs, sp_in, dtf, a_f, cs, cs_last, causal


def _head_decay(cs_j, cst_ref, e, causal):
    lane = lax.broadcasted_iota(jnp.int32, (Q, LANE), 1)
    rolled = pltpu.roll(cs_j, HEAD, 1)
    own = (lane < HEAD) if e == 0 else (lane >= HEAD)
    col_b = jnp.where(own, cs_j, rolled)
    col_b = jnp.concatenate([col_b] * (Q // LANE), axis=1)
    row_b = cst_ref[e * HEAD:e * HEAD + 1, :]
    return jnp.where(causal, jnp.exp(jnp.minimum(col_b - row_b, 0.0)), 0.0)


def ssd_fwd(xbc, z_src, dt_src, dtb_f, alog_f, dsk_f, snw, name):
    L = xbc.shape[0]
    nc = L // Q

    def body(xbc_ref, z_ref, dt_ref, dtb_ref, alog_ref, dsk_ref, snw_ref, y_ref, yn_ref, sp_ref, s_ref, cs_ref, cst_ref, yd_ref):
        @pl.when(pl.program_id(0) == 0)
        def _():
            s_ref[...] = jnp.zeros((N_STATE, D_SSD), F32)

        xs, _, dtf, a_f, cs, cs_last, causal = _ssd_common(xbc_ref, dt_ref, dtb_ref, alog_ref, cs_ref)
        e_cs = jnp.exp(cs)
        xdt = xs * dtf
        zst = jnp.exp(cs_last - cs) * xdt
        sp_ref[0] = s_ref[...]
        lane = lax.broadcasted_iota(jnp.int32, (Q, LANE), 1)
        for g in range(2):
            gl = slice(g * 512, g * 512 + 512)
            b_g = xbc_ref[:, D_SSD + g * N_STATE:D_SSD + (g + 1) * N_STATE]
            c_g = xbc_ref[:, D_SSD + 2 * N_STATE + g * N_STATE:D_SSD + 2 * N_STATE + (g + 1) * N_STATE]
            s_prev = s_ref[:, gl]
            cb = _dot_nt(c_g, b_g)
            yd_ref[:, gl] = e_cs[:, gl] * _dot(c_g, s_prev)
            for j in range(4):
                tl = slice(g * 512 + j * LANE, g * 512 + (j + 1) * LANE)
                cs_j = cs[:, tl]
                cst_ref[...] = cs_j.T
                x_j = xdt[:, tl]
                o0 = _dot(cb * _head_decay(cs_j, cst_ref, 0, causal), x_j)
                o1 = _dot(cb * _head_decay(cs_j, cst_ref, 1, causal), x_j)
                yd_ref[:, tl] += jnp.where(lane < HEAD, o0, o1)
            s_ref[:, gl] = jnp.exp(cs_last[:, gl]) * s_prev + _dot_tn(b_g, zst[:, gl])
        y = yd_ref[...] + xs * dsk_ref[...]
        y_ref[...] = y
        yz = y * _silu(z_ref[...])
        r = lax.rsqrt(jnp.mean(yz * yz, axis=-1, keepdims=True) + 1e-6)
        yn_ref[...] = (yz * r * snw_ref[...]).astype(BF16)

    chunk = lambda w, c: pl.BlockSpec((Q, w), lambda i: (i, c))
    return pl.pallas_call(
        body, name=name, grid=(nc,),
        out_shape=[_sds((L, D_SSD), F32), _sds((L, D_SSD), BF16), _sds((nc, N_STATE, D_SSD), F32)],
        in_specs=[chunk(D_XBC, 0), chunk(D, 0), chunk(D, 0)] + [_const_spec((1, D))] * 4,
        out_specs=[chunk(D, 0), chunk(D, 0), pl.BlockSpec((1, N_STATE, D_SSD), lambda i: (i, 0, 0))],
        scratch_shapes=[pltpu.VMEM((N_STATE, D_SSD), F32), pltpu.VMEM((Q, D_SSD), F32), pltpu.VMEM((LANE, Q), F32),
                        pltpu.VMEM((Q, D_SSD), F32)],
        compiler_params=_cparams(("arbitrary",)))(xbc, z_src, dt_src, dtb_f, alog_f, dsk_f, snw)


def ssd_bwd(dysn, y, xbc, z_src, dt_src, s_prev_all, dtb_f, alog_f, dsk_f, snw, name):
    L = xbc.shape[0]
    nc = L // Q

    def body(dyn_ref, y_ref, xbc_ref, z_ref, dt_ref, sp_ref, dtb_ref, alog_ref, dsk_ref, snw_ref,
             dz_ref, ddt_ref, dxbc_ref, acc_ref, acc16_ref, ds_ref, cs_ref, cst_ref, dcs_ref, dx_ref):
        step = pl.program_id(0)

        @pl.when(step == 0)
        def _():
            ds_ref[...] = jnp.zeros((N_STATE, D_SSD), F32)
            acc_ref[...] = jnp.zeros((8, D), F32)

        z = z_ref[...]
        y = y_ref[...]
        sz = _sigmoid(z)
        siluz = z * sz
        yz = y * siluz
        r = lax.rsqrt(jnp.mean(yz * yz, axis=-1, keepdims=True) + 1e-6)
        n = yz * r
        dyn = dyn_ref[...]
        acc_ref[0:1, :] += _colsum(dyn * n)
        dn = dyn * snw_ref[...]
        dyz = r * (dn - n * jnp.mean(dn * n, axis=-1, keepdims=True))
        dy = dyz * siluz
        dz_ref[...] = (dyz * y * (sz * (1.0 + z * (1.0 - sz)))).astype(BF16)

        xs, sp_in, dtf, a_f, cs, cs_last, causal = _ssd_common(xbc_ref, dt_ref, dtb_ref, alog_ref, cs_ref)
        acc_ref[3:4, :] += _colsum(dy * xs)
        e_cs = jnp.exp(cs)
        xdt = xs * dtf
        dst = jnp.exp(cs_last - cs)
        zst = dst * xdt
        e_last = jnp.exp(cs_last)
        lane = lax.broadcasted_iota(jnp.int32, (Q, LANE), 1)
        ones = jnp.ones((Q, LANE), F32)
        dcs_last_parts = []
        for g in range(2):
            gl = slice(g * 512, g * 512 + 512)
            b_g = xbc_ref[:, D_SSD + g * N_STATE:D_SSD + (g + 1) * N_STATE]
            c_g = xbc_ref[:, D_SSD + 2 * N_STATE + g * N_STATE:D_SSD + 2 * N_STATE + (g + 1) * N_STATE]
            s_prev = sp_ref[0, :, gl]
            ds_g = ds_ref[:, gl]
            dy_g = dy[:, gl]
            cb = _dot_nt(c_g, b_g)
            y_off = e_cs[:, gl] * _dot(c_g, s_prev)
            edy = e_cs[:, gl] * dy_g
            d_c = _dot_nt(edy, s_prev)
            d_z = _dot(b_g, ds_g)
            d_b = _dot_nt(zst[:, gl], ds_g)
            t_g = d_z * zst[:, gl]
            dcs_ref[:, gl] = dy_g * y_off - t_g
            dx_ref[:, gl] = d_z * dst[:, gl]
            dcs_last_parts.append(_colsum(t_g) + _colsum(ds_g * s_prev) * e_last[:, gl])
            ds_ref[:, gl] = e_last[:, gl] * ds_g + _dot_tn(c_g, edy)
            dcb = jnp.zeros((Q, Q), F32)
            for j in range(4):
                tl = slice(g * 512 + j * LANE, g * 512 + (j + 1) * LANE)
                cs_j = cs[:, tl]
                cst_ref[...] = cs_j.T
                x_j = xdt[:, tl]
                dy_j = dy[:, tl]
                dx_j = jnp.zeros((Q, LANE), F32)
                dcs_j = jnp.zeros((Q, LANE), F32)
                for e in range(2):
                    own = (lane < HEAD) if e == 0 else (lane >= HEAD)
                    w_h = _head_decay(cs_j, cst_ref, e, causal)
                    g_h = cb * w_h
                    dy_m = jnp.where(own, dy_j, 0.0)
                    d_g = _dot_nt(dy_m, x_j)
                    dx_j = dx_j + _dot_tn(g_h, dy_m)
                    dcb = dcb + d_g * w_h
                    p_h = d_g * g_h
                    row_sums = _dot_exact(p_h, ones, 2, "b")
                    col_sums = _dot_exact(p_h, ones, 2, "b", (((0,), (0,)), ((), ())))
                    dcs_j = dcs_j + jnp.where(own, row_sums - col_sums, 0.0)
                dcs_ref[:, tl] += dcs_j * (1.0 / HEAD)
                dx_ref[:, tl] += dx_j
            d_c = d_c + _dot(dcb, b_g)
            d_b = d_b + _dot_tn(dcb, c_g)
            dxbc_ref[:, D_SSD + g * N_STATE:D_SSD + (g + 1) * N_STATE] = d_b
            dxbc_ref[:, D_SSD + 2 * N_STATE + g * N_STATE:D_SSD + 2 * N_STATE + (g + 1) * N_STATE] = d_c
        dcs_last = jnp.concatenate(dcs_last_parts, axis=1)
        anticausal = lax.broadcasted_iota(jnp.int32, (Q, Q), 0) <= lax.broadcasted_iota(jnp.int32, (Q, Q), 1)
        d_adt = _dot_exact(anticausal.astype(F32), dcs_ref[...], 3, "a") + dcs_last
        dx = dx_ref[...]
        acc_ref[2:3, :] += _colsum(d_adt * dtf) * a_f
        d_dtf = d_adt * a_f + dx * xs
        dxbc_ref[:, 0:D_SSD] = dx * dtf + dy * dsk_ref[...]
        d_raw = d_dtf * _sigmoid(sp_in)
        acc_ref[1:2, :] += _colsum(d_raw)
        head_of_lane = lax.broadcasted_iota(jnp.int32, (D_SSD, LANE), 0) // HEAD
        fold = (head_of_lane == lax.broadcasted_iota(jnp.int32, (D_SSD, LANE), 1)).astype(F32)
        ddt_ref[...] = _dot_exact(d_raw, fold, 2, "b").astype(BF16)

        @pl.when(step == nc - 1)
        def _():
            acc16_ref[...] = _dot_exact(acc_ref[...], fold, 3, "b")

    rchunk = lambda w, c: pl.BlockSpec((Q, w), lambda i: (nc - 1 - i, c))
    return pl.pallas_call(
        body, name=name, grid=(nc,),
        out_shape=[_sds((L, D_SSD), BF16), _sds((L, LANE), BF16), _sds((L, D_XBC), F32), _sds((8, D), F32), _sds((8, LANE), F32)],
        in_specs=[rchunk(D, 0), rchunk(D, 0), rchunk(D_XBC, 0), rchunk(D, 0), rchunk(D, 0),
                  pl.BlockSpec((1, N_STATE, D_SSD), lambda i: (nc - 1 - i, 0, 0))] + [_const_spec((1, D))] * 4,
        out_specs=[rchunk(D, 0), rchunk(LANE, 0), rchunk(D_XBC, 0), _const_spec((8, D)), _const_spec((8, LANE))],
        scratch_shapes=[pltpu.VMEM((N_STATE, D_SSD), F32), pltpu.VMEM((Q, D_SSD), F32), pltpu.VMEM((LANE, Q), F32),
                        pltpu.VMEM((Q, D_SSD), F32), pltpu.VMEM((Q, D_SSD), F32)],
        compiler_params=_cparams(("arbitrary",)))(dysn, y, xbc, z_src, dt_src, s_prev_all, dtb_f, alog_f, dsk_f, snw)


def _adamw_math(w, g, m, v):
    m_n = ADAM_B1 * m + (1.0 - ADAM_B1) * g
    v_n = ADAM_B2 * v + (1.0 - ADAM_B2) * jnp.square(g)
    c1 = 1.0 - ADAM_B1 ** ADAM_STEP
    c2 = 1.0 - ADAM_B2 ** ADAM_STEP
    return -ADAM_LR * ((m_n / c1) / (jnp.sqrt(v_n / c2) + ADAM_EPS) + ADAM_WD * w), m_n, v_n


def _sum_slots(p_ref):
    acc = p_ref[0].astype(F32)
    for s in range(1, p_ref.shape[0]):
        acc = acc + p_ref[s].astype(F32)
    return acc


def adamw_slots(w, slots, m, v, name):
    rows, cols = w.shape
    tc = 256

    def body(w_ref, s_ref, m_ref, v_ref, g_ref, d_ref, mo_ref, vo_ref):
        g_v = _sum_slots(s_ref)
        g_ref[...] = g_v
        d_ref[...], mo_ref[...], vo_ref[...] = _adamw_math(w_ref[...], g_v, m_ref[...], v_ref[...])

    spec = pl.BlockSpec((rows, tc), lambda i: (0, i))
    return pl.pallas_call(body, name=name, grid=(cols // tc,), out_shape=[_sds((rows, cols), F32)] * 4,
                          in_specs=[spec, pl.BlockSpec((slots.shape[0], rows, tc), lambda i: (0, 0, i)), spec, spec], out_specs=[spec] * 4,
                          compiler_params=_cparams(("parallel",)))(w, slots, m, v)


def adamw_many(ws, gs, ms, vs, name):
    n = len(ws)

    def body(*refs):
        for p in range(n):
            d_v, m_v, v_v = _adamw_math(refs[p][...], refs[n + p][...], refs[2 * n + p][...], refs[3 * n + p][...])
            refs[4 * n + p][...] = d_v
            refs[5 * n + p][...] = m_v
            refs[6 * n + p][...] = v_v

    vm = pl.BlockSpec(memory_space=pltpu.VMEM)
    out = pl.pallas_call(body, name=name, out_shape=[_sds(w.shape, F32) for w in ws] * 3, in_specs=[vm] * (4 * n),
                         out_specs=[vm] * (3 * n), compiler_params=_cparams())(*ws, *gs, *ms, *vs)
    return out[:n], out[n:2 * n], out[2 * n:]


def _pack_layout(shapes):
    row, layout = 0, []
    for rows, cols in shapes:
        chunks = []
        for c0 in range(0, cols, D):
            chunks.append((row, c0, min(D, cols - c0)))
            row += rows
        layout.append(chunks)
    return row, layout


def pack_rows(entries, name):
    arrays = [e[0] for e in entries]
    used, layout = _pack_layout([(e[2], e[0].shape[1]) for e in entries])
    total = -(-used // SUBLANES) * SUBLANES
    n = len(arrays)

    def body(*refs):
        o_ref = refs[n]
        o_ref[...] = jnp.zeros((total, D), F32)
        for p in range(n):
            _, first, rows = entries[p]
            for r0, c0, w in layout[p]:
                o_ref[r0:r0 + rows, 0:w] = refs[p][first:first + rows, c0:c0 + w]

    vm = pl.BlockSpec(memory_space=pltpu.VMEM)
    return pl.pallas_call(body, name=name, out_shape=_sds((total, D), F32), in_specs=[vm] * n, out_specs=vm,
                          compiler_params=_cparams())(*arrays)


def unpack_rows(packed, shapes):
    _, layout = _pack_layout(shapes)
    out = []
    for (rows, _), chunks in zip(shapes, layout):
        parts = [packed[..., r0:r0 + rows, 0:w] for r0, _, w in chunks]
        out.append(parts[0] if len(parts) == 1 else jnp.concatenate(parts, axis=-1))
    return out


def sum_slots_many(parts, name):
    n = len(parts)

    def body(*refs):
        for p in range(n):
            refs[n + p][...] = _sum_slots(refs[p])

    vm = pl.BlockSpec(memory_space=pltpu.VMEM)
    return pl.pallas_call(body, name=name, out_shape=[_sds(p.shape[1:], F32) for p in parts], in_specs=[vm] * n,
                          out_specs=[vm] * n, compiler_params=_cparams())(*parts)


def ada_mod(c_all, ada_w_shard, ada_b_cols, name):
    def body(c_ref, w_ref, b_ref, o_ref, ca_ref):
        ca = _silu(c_ref[...])
        ca_ref[...] = ca
        o_ref[...] = _dot(ca, w_ref[...]) + b_ref[...]

    vm = pl.BlockSpec(memory_space=pltpu.VMEM)
    return pl.pallas_call(body, name=name, out_shape=[_sds((N_DEV, ada_w_shard.shape[1]), F32), _sds((N_DEV, D), F32)],
                          in_specs=[vm, vm, vm], out_specs=[vm, vm], compiler_params=_cparams())(c_all, ada_w_shard, ada_b_cols)


def ada_wgrad(c_act_all, dmod_cols, name):
    def body(c_ref, d_ref, o_ref):
        o_ref[...] = _dot_tn_hi(c_ref[...], d_ref[...])

    vm = pl.BlockSpec(memory_space=pltpu.VMEM)
    return pl.pallas_call(body, name=name, out_shape=_sds((D, dmod_cols.shape[1]), F32), in_specs=[vm, vm], out_specs=vm,
                          compiler_params=_cparams())(c_act_all, dmod_cols)


def exchange(srcs, name, gather):
    n = len(srcs)
    gathers = [gather] * n if isinstance(gather, bool) else list(gather)
    shapes = [tuple(s.shape) if g else tuple(s.shape[1:]) for s, g in zip(srcs, gathers)]

    def body(*refs):
        src_refs, out_refs = refs[:n], refs[n:2 * n]
        send_sems, recv_sems, local_sems = refs[2 * n:]
        x, y, c = lax.axis_index("x"), lax.axis_index("y"), lax.axis_index("c")
        me = 4 * x + 2 * y + c

        def peer(k):
            bx, by, bc = (k >> 2) & 1, (k >> 1) & 1, k & 1
            px, py, pc = (x + bx) % 2, (y + by) % 2, (c + bc) % 2
            return (px, py, pc), 4 * px + 2 * py + pc

        def copy(a, k, landing):
            dev, idx = peer(k)
            return pltpu.make_async_remote_copy(
                src_ref=src_refs[a] if gathers[a] else src_refs[a].at[idx], dst_ref=out_refs[a].at[idx if landing else me],
                send_sem=send_sems.at[a, k - 1], recv_sem=recv_sems.at[a, k - 1],
                device_id=dev, device_id_type=pl.DeviceIdType.MESH)

        mine = [pltpu.make_async_copy(src_refs[a] if gathers[a] else src_refs[a].at[me], out_refs[a].at[me], local_sems.at[a])
                for a in range(n)]
        for cp in mine:
            cp.start()
        sends = [copy(a, k, False) for a in range(n) for k in range(1, N_DEV)]
        for cp in sends:
            cp.start()
        for a in range(n):
            for k in range(1, N_DEV):
                copy(a, k, True).wait_recv()
        for cp in sends:
            cp.wait_send()
        for cp in mine:
            cp.wait()

    hbm = pl.BlockSpec(memory_space=pl.ANY)
    return pl.pallas_call(
        body, name=name, out_shape=[_sds((N_DEV,) + shp, s.dtype) for shp, s in zip(shapes, srcs)], in_specs=[hbm] * n,
        out_specs=[hbm] * n,
        scratch_shapes=[pltpu.SemaphoreType.DMA((n, N_DEV - 1)), pltpu.SemaphoreType.DMA((n, N_DEV - 1)),
                        pltpu.SemaphoreType.DMA((n,))],
        compiler_params=pltpu.CompilerParams(has_side_effects=True))(*srcs)


def gather_two_level(srcs, name):
    n = len(srcs)

    def body(*refs):
        src_refs, out_refs = refs[:n], refs[n:2 * n]
        send_sems, recv_sems, local_sems = refs[2 * n:]
        x, y, c = lax.axis_index("x"), lax.axis_index("y"), lax.axis_index("c")
        me, sibling = (x, y, c), (x, y, 1 - c)
        chips = [(1 - x, y), (x, 1 - y), (1 - x, 1 - y)]

        def slot(a, px, py, pc):
            return out_refs[a].at[4 * px + 2 * py + pc]

        def copy(a, k, block, to, src=None):
            return pltpu.make_async_remote_copy(
                src_ref=slot(a, *block) if src is None else src, dst_ref=slot(a, *block), send_sem=send_sems.at[a, k],
                recv_sem=recv_sems.at[a, k], device_id=to, device_id_type=pl.DeviceIdType.MESH)

        mine = [pltpu.make_async_copy(src_refs[a], slot(a, *me), local_sems.at[a]) for a in range(n)]
        for cp in mine:
            cp.start()
        first = []
        for a in range(n):
            first += [copy(a, 0, me, sibling, src=src_refs[a])]
            first += [copy(a, 1 + j, me, (*chip, c), src=src_refs[a]) for j, chip in enumerate(chips)]
        for cp in first:
            cp.start()
        passed = []
        for a in range(n):
            for j, chip in enumerate(chips):
                copy(a, 1 + j, (*chip, c), me).wait_recv()
                passed.append(copy(a, 4 + j, (*chip, c), sibling))
                passed[-1].start()
        for a in range(n):
            copy(a, 0, sibling, me).wait_recv()
            for j, chip in enumerate(chips):
                copy(a, 4 + j, (*chip, 1 - c), me).wait_recv()
        for cp in first + passed:
            cp.wait_send()
        for cp in mine:
            cp.wait()

    hbm = pl.BlockSpec(memory_space=pl.ANY)
    return pl.pallas_call(
        body, name=name, out_shape=[_sds((N_DEV,) + tuple(s.shape), s.dtype) for s in srcs], in_specs=[hbm] * n,
        out_specs=[hbm] * n,
        scratch_shapes=[pltpu.SemaphoreType.DMA((n, N_DEV - 1)), pltpu.SemaphoreType.DMA((n, N_DEV - 1)),
                        pltpu.SemaphoreType.DMA((n,))],
        compiler_params=pltpu.CompilerParams(has_side_effects=True))(*srcs)


def _peer(k):
    x, y, c = lax.axis_index("x"), lax.axis_index("y"), lax.axis_index("c")
    px, py, pc = (x + ((k >> 2) & 1)) % 2, (y + ((k >> 1) & 1)) % 2, (c + (k & 1)) % 2
    return (px, py, pc), 4 * px + 2 * py + pc


def _my_slot():
    return 4 * lax.axis_index("x") + 2 * lax.axis_index("y") + lax.axis_index("c")


_HBM = pl.BlockSpec(memory_space=pltpu.HBM)
_SEM = pl.BlockSpec(memory_space=pltpu.SEMAPHORE)
_EFFECT = pltpu.SideEffectType.DATAFLOW_SIDE_EFFECTING


def exchange_start(srcs, name, gather):
    n = len(srcs)
    shapes = [tuple(s.shape) if gather else tuple(s.shape[1:]) for s in srcs]
    lands = [lax.empty((N_DEV,) + shp, s.dtype) for shp, s in zip(shapes, srcs)]

    def body(*refs):
        src_refs, land_refs = refs[:n], refs[n:2 * n]
        sems = refs[2 * n:4 * n]
        token = refs[-1]
        me = _my_slot()
        for a in range(n):
            for k in range(1, N_DEV):
                dev, idx = _peer(k)
                pltpu.make_async_remote_copy(
                    src_ref=src_refs[a] if gather else src_refs[a].at[idx], dst_ref=land_refs[a].at[me],
                    send_sem=sems[2 * a].at[k - 1], recv_sem=sems[2 * a + 1].at[k - 1],
                    device_id=dev, device_id_type=pl.DeviceIdType.MESH).start()
        token[...] = jnp.zeros_like(token)

    out_shape = ([pltpu.SemaphoreType.DMA((N_DEV - 1,))] * (2 * n) + [pltpu.HBM(s.shape, s.dtype) for s in srcs]
                 + [pltpu.HBM(l.shape, l.dtype) for l in lands] + [_sds((8, LANE), F32)])
    out = pl.pallas_call(
        body, name=name, out_shape=out_shape, in_specs=[_HBM] * (2 * n),
        out_specs=[_SEM] * (2 * n) + [_HBM] * (2 * n) + [pl.BlockSpec(memory_space=pltpu.VMEM)],
        input_output_aliases={i: 2 * n + i for i in range(2 * n)},
        compiler_params=pltpu.CompilerParams(has_side_effects=_EFFECT))(
            *[pltpu.with_memory_space_constraint(s, pltpu.HBM) for s in srcs],
            *[pltpu.with_memory_space_constraint(l, pltpu.HBM) for l in lands])
    parts = [(out[2 * a], out[2 * a + 1], out[2 * n + a], out[3 * n + a]) for a in range(n)]
    return parts, out[-1]


def exchange_wait(parts, after, name, gather):
    n = len(parts)

    def body(*refs):
        src_refs, land_refs = refs[:n], refs[n:2 * n]
        sems = refs[2 * n:4 * n]
        for a in range(n):
            for k in range(1, N_DEV):
                dev, idx = _peer(k)
                copy = pltpu.make_async_remote_copy(
                    src_ref=src_refs[a] if gather else src_refs[a].at[idx], dst_ref=land_refs[a].at[idx],
                    send_sem=sems[2 * a].at[k - 1], recv_sem=sems[2 * a + 1].at[k - 1],
                    device_id=dev, device_id_type=pl.DeviceIdType.MESH)
                copy.wait_send()
                copy.wait_recv()

    srcs = [p[2] for p in parts]
    lands = [p[3] for p in parts]
    sems = [s for p in parts for s in p[:2]]
    out = pl.pallas_call(
        body, name=name, out_shape=[pltpu.HBM(a.shape, a.dtype) for a in srcs + lands],
        in_specs=[_HBM] * (2 * n) + [_SEM] * (2 * n) + [pl.BlockSpec(memory_space=pl.ANY)], out_specs=[_HBM] * (2 * n),
        input_output_aliases={i: i for i in range(2 * n)},
        compiler_params=pltpu.CompilerParams(has_side_effects=_EFFECT))(*srcs, *lands, *sems, after)
    return list(zip(out[:n], out[n:]))


def _cols_to_slabs(g):
    r, c = g.shape
    return g.reshape(r, N_DEV, c // N_DEV).transpose(1, 0, 2)


def _slabs_to_cols(s):
    _, r, cs = s.shape
    return s.transpose(1, 0, 2).reshape(r, N_DEV * cs)


def _rep_heads(v):
    return jnp.repeat(v.reshape(N_HEADS), HEAD).reshape(1, D_SSD)


def local_fwd_bwd(x, target, mod, get_w, put_grad, small):
    n1w, n2w, fnw = small["norm1_w"], small["norm2_w"], small["final_norm_w"]
    dtb_f, alog_f, dsk_f = _rep_heads(small["dt_bias"]), _rep_heads(small["a_log"]), _rep_heads(small["d_skip"])
    snw = small["ssd_norm_w"]

    def after(v, token):
        return v + token[0:1, 0:1]

    w_in = get_w("w_in", mod)
    h1, proj_zx, proj_dt, proj_cf = in_proj(x, mod, n1w, w_in["w_full"], w_in["w_dt_rep"], w_in["w_cf"], "norm1_in_proj")
    xbc = conv_silu_fwd(proj_zx, D_SSD // CB, D_XBC, small["ssd_conv_w"], small["ssd_conv_b"], "ssd_conv")
    y, ysn, s_prev = ssd_fwd(xbc, proj_zx, proj_dt, dtb_f, alog_f, dsk_f, snw, "ssd_scan")
    uc = conf_conv_fwd(proj_cf, 0, D_CONF // CB, small["conf_conv_w"], small["conf_conv_b"], "conf_conv")
    w_out = get_w("w_out", uc)
    mix, u, x1, h2 = mixer_out(ysn, uc, small["conf_ln_w"], small["conf_ln_b"], w_out, x, mod, n2w, "out_proj_norm2")
    w_up_t = get_w("w_up", h2)
    up = mm_nt([(h2, w_up_t, 0)], "up_proj")
    act = ffn_conv_fwd(up, small["ffn_conv_w"], small["ffn_conv_b"], "ffn_conv")
    w_down = get_w("w_down", act)
    dx2, dff, dact, acc_f = final_loss(act, w_down, x1, mod, fnw, target, "down_proj_loss")

    token = put_grad("w_down", mm_tn(act, dff, "wgrad_down"))
    dupg, dupv, dwg, dwv = ffn_conv_bwd(up, small["ffn_conv_w"], after(small["ffn_conv_b"], token), dact, "ffn_conv_bwd")
    token = put_grad("w_up", mm_tn_stack([dupg, dupv], h2, "wgrad_up"))
    dx1, dmix, acc_2 = norm_mod_bwd([(dupg, w_up_t, 0), (dupv, w_up_t, 1)], x1, dx2, mod, after(n2w, token), 3, "norm2_bwd",
                                    mix=mix, gate_row=2)

    token = put_grad("w_out", mm_tn_stack([ysn, u], dmix, "wgrad_out"))
    dysn, duc, acc_ln = mixer_out_bwd(dmix, w_out, uc, after(small["conf_ln_w"], token), small["conf_ln_b"], "out_proj_bwd")
    dcfa, dcfg, dw_cc = conf_conv_bwd(proj_cf, 0, D_CONF // CB, small["conf_conv_w"], duc, "conf_conv_bwd")
    dz, ddt, dxbc_post, acc_s, acc_s16 = ssd_bwd(dysn, y, xbc, proj_zx, proj_dt, s_prev, dtb_f, alog_f, dsk_f, snw,
                                                 "ssd_scan_bwd")
    dxbc, dw_sc = conv_silu_bwd(proj_zx, D_SSD // CB, D_XBC, small["ssd_conv_w"], small["ssd_conv_b"], dxbc_post, "ssd_conv_bwd")
    token = put_grad("w_in", mm_tn_concat(
        [(dz, D_SSD), (dxbc, D_XBC), (ddt, N_HEADS), (dcfa, D_CONF), (dcfg, D_CONF)], h1, "wgrad_in"))
    dh1_pairs = [(dz, w_in["w_full"], 0), (ddt, w_in["w_dt16"], 0), (dcfa, w_in["w_cf"], 0), (dcfg, w_in["w_cf"], 1),
                 (dxbc, w_in["w_xbc"], 0)]
    grad_x, acc_1 = norm_mod_bwd(dh1_pairs, x, dx1, mod, after(n1w, token), 0, "norm1_bwd")

    small_accs = dict(acc_1=acc_1, acc_2=acc_2, acc_f=acc_f, acc_ln=acc_ln, acc_s=acc_s, acc_s16=acc_s16, dw_sc=dw_sc,
                      dw_cc=dw_cc, dwg=dwg, dwv=dwv)
    return grad_x, small_accs


def kernel(x, c, ada_w, ada_b, norm1_w, w_in, ssd_conv_w, ssd_conv_b, dt_bias, a_log, d_skip, ssd_norm_w, conf_conv_w, conf_conv_b, conf_ln_w, conf_ln_b, w_out, norm2_w, w_up, ffn_conv_w, ffn_conv_b, w_down, final_norm_w, loss_target, m_ada_w, m_ada_b, m_norm1_w, m_w_in, m_ssd_conv_w, m_ssd_conv_b, m_dt_bias, m_a_log, m_d_skip, m_ssd_norm_w, m_conf_conv_w, m_conf_conv_b, m_conf_ln_w, m_conf_ln_b, m_w_out, m_norm2_w, m_w_up, m_ffn_conv_w, m_ffn_conv_b, m_w_down, m_final_norm_w, v_ada_w, v_ada_b, v_norm1_w, v_w_in, v_ssd_conv_w, v_ssd_conv_b, v_dt_bias, v_a_log, v_d_skip, v_ssd_norm_w, v_conf_conv_w, v_conf_conv_b, v_conf_ln_w, v_conf_ln_b, v_w_out, v_norm2_w, v_w_up, v_ffn_conv_w, v_ffn_conv_b, v_w_down, v_final_norm_w):
    me = 4 * lax.axis_index("x") + 2 * lax.axis_index("y") + lax.axis_index("c")
    weights = dict(ada_w=ada_w, ada_b=ada_b, norm1_w=norm1_w, w_in=w_in, ssd_conv_w=ssd_conv_w, ssd_conv_b=ssd_conv_b,
                   dt_bias=dt_bias, a_log=a_log, d_skip=d_skip, ssd_norm_w=ssd_norm_w, conf_conv_w=conf_conv_w,
                   conf_conv_b=conf_conv_b, conf_ln_w=conf_ln_w, conf_ln_b=conf_ln_b, w_out=w_out, norm2_w=norm2_w, w_up=w_up,
                   ffn_conv_w=ffn_conv_w, ffn_conv_b=ffn_conv_b, w_down=w_down, final_norm_w=final_norm_w)
    moms_m = dict(ada_w=m_ada_w, ada_b=m_ada_b, norm1_w=m_norm1_w, w_in=m_w_in, ssd_conv_w=m_ssd_conv_w, ssd_conv_b=m_ssd_conv_b,
                  dt_bias=m_dt_bias, a_log=m_a_log, d_skip=m_d_skip, ssd_norm_w=m_ssd_norm_w, conf_conv_w=m_conf_conv_w,
                  conf_conv_b=m_conf_conv_b, conf_ln_w=m_conf_ln_w, conf_ln_b=m_conf_ln_b, w_out=m_w_out, norm2_w=m_norm2_w,
                  w_up=m_w_up, ffn_conv_w=m_ffn_conv_w, ffn_conv_b=m_ffn_conv_b, w_down=m_w_down, final_norm_w=m_final_norm_w)
    moms_v = dict(ada_w=v_ada_w, ada_b=v_ada_b, norm1_w=v_norm1_w, w_in=v_w_in, ssd_conv_w=v_ssd_conv_w, ssd_conv_b=v_ssd_conv_b,
                  dt_bias=v_dt_bias, a_log=v_a_log, d_skip=v_d_skip, ssd_norm_w=v_ssd_norm_w, conf_conv_w=v_conf_conv_w,
                  conf_conv_b=v_conf_conv_b, conf_ln_w=v_conf_ln_w, conf_ln_b=v_conf_ln_b, w_out=v_w_out, norm2_w=v_norm2_w,
                  w_up=v_w_up, ffn_conv_w=v_ffn_conv_w, ffn_conv_b=v_ffn_conv_b, w_down=v_w_down, final_norm_w=v_final_norm_w)
    names = list(weights)

    def to2d(a):
        return a[0] if a.ndim == 3 else a.reshape(1, -1)

    big = ("w_in", "w_out", "w_up", "w_down")

    def rows_of(a):
        return jnp.swapaxes(a, 1, 2)[0] if a.shape[2] != D else a[0]

    shards = [rows_of(weights[n]).astype(BF16) for n in big]

    c_all, scw_all, ccw_all, fcw_all, w_in_slabs = gather_two_level(
        [c.reshape(8, LANE), ssd_conv_w[0], conf_conv_w[0], ffn_conv_w[0], shards[0]], "gather_first")
    c_all = c_all.reshape(N_DEV, D)

    ada_cols = ada_w.shape[2]
    ada_b_cols = lax.dynamic_slice(ada_b, (0, me * ada_cols), (1, ada_cols))
    mod_cols, c_act_all = ada_mod(c_all, ada_w[0], ada_b_cols, "ada_mod")
    mod_parts, = exchange([jnp.pad(mod_cols, ((0, 0), (0, D - ada_cols))).reshape(N_DEV, 8, LANE)], "scatter_mod", gather=False)
    mod = mod_parts.reshape(N_DEV, D)[:, :ada_cols].reshape(6, D)
    mod = jnp.pad(mod, ((0, 2), (0, 0)))

    later, mod = lax.optimization_barrier((shards[1:], mod))
    gather_parts, token = exchange_start(later, "gather_weights_start", gather=True)
    mod = mod + token[0:1, 0:1]

    small = {n: to2d(weights[n]) for n in names if n not in ("ada_w",) + big}
    small["ssd_conv_w"] = _slabs_to_cols(scw_all)
    small["conf_conv_w"] = _slabs_to_cols(ccw_all)
    small["ffn_conv_w"] = _slabs_to_cols(fcw_all)

    def with_own(landed, own):
        return lax.dynamic_update_slice(landed, own[None], (me,) + (0,) * own.ndim)

    def get_w(n, after):
        if n == "w_in":
            slabs = w_in_slabs
        else:
            a = big.index(n)
            (own, landed), = exchange_wait([gather_parts[a - 1]], after, "gather_" + n + "_wait", gather=True)
            slabs = with_own(landed, own)
        full = slabs.reshape(N_DEV * slabs.shape[1], D)
        if n != "w_in":
            return full
        w_dt = full[D_SSD + D_XBC:D_SSD + D_XBC + N_HEADS]
        return dict(w_full=full, w_xbc=full[D_SSD:D_SSD + D_XBC], w_cf=full[D_SSD + D_XBC + N_HEADS:],
                    w_dt_rep=jnp.repeat(w_dt, HEAD, axis=0), w_dt16=jnp.pad(w_dt, ((0, LANE - N_HEADS), (0, 0))))

    scatter_parts = {}

    def put_grad(n, g):
        slabs = g if g.ndim == 3 else g.reshape(N_DEV, g.shape[0] // N_DEV, g.shape[1])
        (scatter_parts[n],), token = exchange_start([slabs.astype(BF16)], "scatter_" + n + "_start", gather=False)
        return token

    grad_x, accs = local_fwd_bwd(x[0], loss_target[0], mod, get_w, put_grad, small)
    loss = lax.psum(0.5 / D * jnp.sum(accs["acc_f"][2:3]), ("x", "y", "c"))

    grads, delta, new_m, new_v = {}, {}, {}, {}

    def finish(ns, after, name):
        landed = exchange_wait([scatter_parts[n] for n in ns], after, name, gather=False)
        for n, (sent, slots) in zip(ns, landed):
            slots = with_own(slots, lax.dynamic_index_in_dim(sent, me, 0, keepdims=False))
            out = adamw_slots(rows_of(weights[n]), slots, rows_of(moms_m[n]), rows_of(moms_v[n]), "adamw_" + n)
            if weights[n].shape[2] != D:
                out = [jnp.swapaxes(o, 0, 1) for o in out]
            grads[n], delta[n], new_m[n], new_v[n] = out

    finish(big[1:], grad_x, "scatter_grads_wait")

    accs = dict(zip(accs, lax.optimization_barrier((list(accs.values()), [new_v[n] for n in big[1:]]))[0]))
    rep = (("acc_1", 0, 3), ("acc_2", 0, 4), ("acc_f", 0, 2), ("acc_ln", 0, 2), ("acc_s", 0, 1), ("acc_s16", 1, 3),
           ("dw_sc", K_SSD, 1), ("dw_cc", K_CONF, 1), ("dwg", K_FFN, 1), ("dwv", K_FFN, 1))
    shapes = [(rows, accs[k].shape[1]) for k, _, rows in rep]
    conv_slabs = [_cols_to_slabs(accs["dw_sc"][:K_SSD]), _cols_to_slabs(accs["dw_cc"][:K_CONF]),
                  _cols_to_slabs(jnp.concatenate([accs["dwg"][:K_FFN], accs["dwv"][:K_FFN]], axis=1))]
    packed = pack_rows([(accs[k], first, rows) for k, first, rows in rep], "pack_small_grads")
    landed = exchange([packed] + conv_slabs, "exchange_small_grads", gather=[True, False, False, False])
    packed_red, g_scw, g_ccw, g_fcw = sum_slots_many(landed, "sum_small_grads")
    a1_all, a2_all, af_all = unpack_rows(landed[0], shapes)[:3]
    r1, r2, rf, rln, rs, r16, rscb, rccb, rfbg, rfbv = unpack_rows(packed_red, shapes)

    def mod_rows(a1, a2, af):
        return jnp.concatenate([a1[..., 0:2, :], a2[..., 3:4, :], a2[..., 0:2, :], af[..., 1:2, :]], axis=-2)

    dmod_all = mod_rows(a1_all, a2_all, af_all).reshape(N_DEV, 6 * D)
    grads["ada_w"] = ada_wgrad(c_act_all, lax.dynamic_slice(dmod_all, (0, me * ada_cols), (N_DEV, ada_cols)), "ada_wgrad")
    grads.update(
        ada_b=mod_rows(r1, r2, rf).reshape(1, 6 * D), norm1_w=r1[2:3], ssd_conv_w=g_scw, ssd_conv_b=rscb,
        dt_bias=r16[0:1, :N_HEADS], a_log=r16[1:2, :N_HEADS], d_skip=r16[2:3, :N_HEADS], ssd_norm_w=rs,
        conf_conv_w=g_ccw, conf_conv_b=rccb, conf_ln_w=rln[0:1], conf_ln_b=rln[1:2], norm2_w=r2[2:3],
        ffn_conv_w=g_fcw, ffn_conv_b=jnp.concatenate([rfbg, rfbv], axis=1), final_norm_w=rf[0:1])

    rest = [n for n in names if n not in big]
    d_l, m_l, v_l = adamw_many([to2d(weights[n]) for n in rest], [grads[n] for n in rest], [to2d(moms_m[n]) for n in rest],
                               [to2d(moms_v[n]) for n in rest], "adamw_small")
    for n, dd, mm, vv in zip(rest, d_l, m_l, v_l):
        delta[n], new_m[n], new_v[n] = dd, mm, vv
    finish(big[:1], d_l[0], "scatter_w_in_wait")
    shape_of = lambda d_: {n: d_[n].reshape(weights[n].shape) for n in names}
    grads, delta, new_m, new_v = shape_of(grads), shape_of(delta), shape_of(new_m), shape_of(new_v)
    return (loss, grad_x[None], *[grads[n] for n in names], *[delta[n] for n in names], *[new_m[n] for n in names],
            *[new_v[n] for n in names])
```

```python
import functools

import jax
import jax.numpy as jnp
from jax import lax
from jax.experimental import pallas as pl
from jax.experimental.pallas import tpu as pltpu

F32 = jnp.float32
BF16 = jnp.bfloat16
HI = lax.Precision.HIGHEST

N_DEV = 8
D = 1024
D_SSD = 1024
HEAD = 64
N_HEADS = 16
N_STATE = 128
D_XBC = 1536
D_CONF = 1024
D_FF = 2816
K_SSD, K_CONF, K_FFN = 4, 31, 3
D_INP = 5632
LANE = 128
TR = 256
TM = 512
Q = 256
CB = 256
TC = 1024
VMEM_LIMIT = 56 * 1024 * 1024

ADAM_LR, ADAM_B1, ADAM_B2, ADAM_EPS, ADAM_WD, ADAM_STEP = 0.001, 0.9, 0.999, 1e-08, 0.01, 10


def _cparams(sem=None):
    return pltpu.CompilerParams(vmem_limit_bytes=VMEM_LIMIT, dimension_semantics=sem)


def _sds(shape, dtype):
    return jax.ShapeDtypeStruct(shape, dtype)


def _sigmoid(x):
    return 1.0 / (1.0 + jnp.exp(-x))


def _silu(x):
    return x * _sigmoid(x)


def _dsilu(x):
    s = _sigmoid(x)
    return s * (1.0 + x * (1.0 - s))


def _softplus(x):
    return jnp.maximum(x, 0.0) + jnp.log(1.0 + jnp.exp(-jnp.abs(x)))


def _dot(a, b):
    return jnp.dot(a.astype(BF16), b.astype(BF16), preferred_element_type=F32)


def _dot_nt(a, b):
    return lax.dot_general(a.astype(BF16), b.astype(BF16), (((1,), (1,)), ((), ())), preferred_element_type=F32)


def _dot_tn(a, b):
    return lax.dot_general(a.astype(BF16), b.astype(BF16), (((0,), (0,)), ((), ())), preferred_element_type=F32)


def _bf16_terms(a, terms):
    parts, rem = [], a
    for t in range(terms):
        p = rem.astype(BF16)
        parts.append(p)
        if t + 1 < terms:
            rem = rem - p.astype(F32)
    return parts


def _dot_exact(a, b, terms, exact, dims=(((1,), (0,)), ((), ()))):
    if exact == "a":
        a_b = a.astype(BF16)
        outs = [lax.dot_general(a_b, p, dims, preferred_element_type=F32) for p in _bf16_terms(b, terms)]
    else:
        b_b = b.astype(BF16)
        outs = [lax.dot_general(p, b_b, dims, preferred_element_type=F32) for p in _bf16_terms(a, terms)]
    acc = outs[-1]
    for o in reversed(outs[:-1]):
        acc = acc + o
    return acc


def _dot_tn_hi(a, b):
    return lax.dot_general(a, b, (((0,), (0,)), ((), ())), precision=HI, preferred_element_type=F32)


def _colsum(x):
    return jnp.sum(x, axis=0, keepdims=True)


def _const_spec(shape):
    return pl.BlockSpec(shape, lambda *_: (0,) * len(shape))


def _col_tile(n):
    for t in (2816, 1408, 1024, 768, 512, 256, 128):
        if n % t == 0 and t <= n:
            return t
    return n


def mm_nt(pairs, name):
    L = pairs[0][0].shape[0]
    K = pairs[0][1].shape[0]
    tk = _col_tile(K)
    n = len(pairs)

    def body(*refs):
        o_ref = refs[-1]
        acc = None
        for p in range(n):
            t = lax.dot_general(refs[2 * p][...], refs[2 * p + 1][...], (((1,), (1,)), ((), ())),
                                preferred_element_type=F32)
            acc = t if acc is None else acc + t
        o_ref[...] = acc

    in_specs, args = [], []
    for a, w, cb in pairs:
        in_specs += [pl.BlockSpec((TM, a.shape[1]), lambda j, i: (i, 0)),
                     pl.BlockSpec((tk, a.shape[1]), functools.partial(lambda j, i, cb: (j, cb), cb=cb))]
        args += [a, w]
    return pl.pallas_call(
        body, name=name, grid=(K // tk, L // TM), out_shape=_sds((L, K), F32), in_specs=in_specs,
        out_specs=pl.BlockSpec((TM, tk), lambda j, i: (i, j)),
        compiler_params=_cparams(("parallel", "parallel")))(*args)


def mm_tn(a, g, name):
    L, M = a.shape
    N = g.shape[1]
    tn = _col_tile(N) if N > 1024 else N
    if M * tn * 4 > 12 * 1024 * 1024:
        tn = 512
    tl = 512 if L % 512 == 0 else TR
    nl = L // tl

    def body(a_ref, g_ref, o_ref, acc_ref):
        @pl.when(pl.program_id(1) == 0)
        def _():
            acc_ref[...] = jnp.zeros((M, tn), F32)

        acc_ref[...] += lax.dot_general(a_ref[...], g_ref[...], (((0,), (0,)), ((), ())), preferred_element_type=F32)

        @pl.when(pl.program_id(1) == nl - 1)
        def _():
            o_ref[...] = acc_ref[...].astype(BF16)

    return pl.pallas_call(
        body, name=name, grid=(N // tn, nl), out_shape=_sds((M, N), BF16),
        in_specs=[pl.BlockSpec((tl, M), lambda j, l: (l, 0)), pl.BlockSpec((tl, tn), lambda j, l: (l, j))],
        out_specs=pl.BlockSpec((M, tn), lambda j, l: (0, j)), scratch_shapes=[pltpu.VMEM((M, tn), F32)],
        compiler_params=_cparams(("parallel", "arbitrary")))(a, g)


def mm_tn_stack(a_list, g, name):
    L, M = a_list[0].shape
    N = g.shape[1]
    n = len(a_list)
    tl = 512 if L % 512 == 0 else TR
    nl = L // tl

    def body(*refs):
        a_refs, g_ref, o_ref, acc_ref = refs[:n], refs[n], refs[n + 1], refs[n + 2]
        j, l = pl.program_id(0), pl.program_id(1)

        @pl.when(l == 0)
        def _():
            acc_ref[...] = jnp.zeros((M, N), F32)

        for p in range(n):
            @pl.when(j == p)
            def _(p=p):
                acc_ref[...] += lax.dot_general(a_refs[p][...], g_ref[...], (((0,), (0,)), ((), ())), preferred_element_type=F32)

        @pl.when(l == nl - 1)
        def _():
            o_ref[...] = acc_ref[...].astype(BF16)

    a_specs = [pl.BlockSpec((tl, M), functools.partial(lambda j, l, p: (jnp.where(j == p, l, 0), 0), p=p)) for p in range(n)]
    return pl.pallas_call(
        body, name=name, grid=(n, nl), out_shape=_sds((n * M, N), BF16),
        in_specs=a_specs + [pl.BlockSpec((tl, N), lambda j, l: (l, 0))],
        out_specs=pl.BlockSpec((M, N), lambda j, l: (j, 0)), scratch_shapes=[pltpu.VMEM((M, N), F32)],
        compiler_params=_cparams(("arbitrary", "arbitrary")))(*a_list, g)


def mm_tn_concat(pieces, g, name):
    L = g.shape[0]
    N = g.shape[1]
    n = len(pieces)
    offsets = [sum(r for _, r in pieces[:p]) for p in range(n + 1)]
    slab = offsets[-1] // N_DEV
    tl = 512 if L % 512 == 0 else TR
    nl = L // tl

    def body(*refs):
        a_refs, g_ref, o_ref, acc_ref = refs[:n], refs[n], refs[n + 1], refs[n + 2]
        l = pl.program_id(0)

        @pl.when(l == 0)
        def _():
            acc_ref[...] = jnp.zeros((offsets[-1], N), F32)

        g_v = g_ref[...]
        for p in range(n):
            t = lax.dot_general(a_refs[p][...], g_v, (((0,), (0,)), ((), ())), preferred_element_type=F32)
            acc_ref[offsets[p]:offsets[p + 1], :] += t[:pieces[p][1], :]

        @pl.when(l == nl - 1)
        def _():
            for s in range(N_DEV):
                o_ref[s] = acc_ref[s * slab:(s + 1) * slab, :].astype(BF16)

    return pl.pallas_call(
        body, name=name, grid=(nl,), out_shape=_sds((N_DEV, slab, N), BF16),
        in_specs=[pl.BlockSpec((tl, a.shape[1]), lambda l: (l, 0)) for a, _ in pieces] + [pl.BlockSpec((tl, N), lambda l: (l, 0))],
        out_specs=_const_spec((N_DEV, slab, N)), scratch_shapes=[pltpu.VMEM((offsets[-1], N), F32)],
        compiler_params=_cparams(("arbitrary",)))(*[a for a, _ in pieces], g)


def _row_spec(width=D):
    return pl.BlockSpec((TR, width), lambda i: (i, 0))


def in_proj(x, mod, n1w, w_full, w_dt_rep, w_cf, name):
    L = x.shape[0]
    n_zx = D_SSD + D_XBC

    def body(x_ref, mod_ref, w_ref, wzx_ref, wdt_ref, wcf_ref, h_ref, zx_ref, dt_ref, cf_ref):
        xin = x_ref[...]
        r = lax.rsqrt(jnp.mean(xin * xin, axis=-1, keepdims=True) + 1e-6)
        h = ((xin * r * w_ref[...]) * (1.0 + mod_ref[1:2, :]) + mod_ref[0:1, :]).astype(BF16)
        h_ref[...] = h
        nt = (((1,), (1,)), ((), ()))
        zx_ref[...] = lax.dot_general(h, wzx_ref[...], nt, preferred_element_type=F32)
        dt_ref[...] = lax.dot_general(h, wdt_ref[...], nt, preferred_element_type=F32)
        cf_ref[...] = lax.dot_general(h, wcf_ref[...], nt, preferred_element_type=F32)

    row = lambda w: pl.BlockSpec((TM, w), lambda i: (i, 0))
    return pl.pallas_call(
        body, name=name, grid=(L // TM,),
        out_shape=[_sds((L, D), BF16), _sds((L, n_zx), F32), _sds((L, D_SSD), F32), _sds((L, 2 * D_CONF), F32)],
        in_specs=[row(D), _const_spec((8, D)), _const_spec((1, D)), _const_spec((n_zx, D)), _const_spec((D_SSD, D)),
                  _const_spec((2 * D_CONF, D))],
        out_specs=[row(D), row(n_zx), row(D_SSD), row(2 * D_CONF)],
        compiler_params=_cparams(("parallel",)))(x, mod, n1w, w_full, w_dt_rep, w_cf)


def mixer_out(ysn, uc, lnw, lnb, w_out, x, mod, n2w, name):
    L = x.shape[0]

    def body(ysn_ref, uc_ref, lnw_ref, lnb_ref, wo_ref, x_ref, mod_ref, n2w_ref, mix_ref, u_ref, x1_ref, h2_ref):
        uc_v = uc_ref[...]
        mu = jnp.mean(uc_v, axis=-1, keepdims=True)
        var = jnp.mean(jnp.square(uc_v - mu), axis=-1, keepdims=True)
        u = _silu((uc_v - mu) * lax.rsqrt(var + 1e-5) * lnw_ref[...] + lnb_ref[...]).astype(BF16)
        u_ref[...] = u
        mix = (jnp.dot(ysn_ref[...], wo_ref[0:D_SSD, :], preferred_element_type=F32)
               + jnp.dot(u, wo_ref[D_SSD:D_SSD + D_CONF, :], preferred_element_type=F32))
        mix_ref[...] = mix
        x1 = x_ref[...] + mod_ref[2:3, :] * mix
        x1_ref[...] = x1
        r = lax.rsqrt(jnp.mean(x1 * x1, axis=-1, keepdims=True) + 1e-6)
        h2_ref[...] = ((x1 * r * n2w_ref[...]) * (1.0 + mod_ref[4:5, :]) + mod_ref[3:4, :]).astype(BF16)

    return pl.pallas_call(
        body, name=name, grid=(L // TR,),
        out_shape=[_sds((L, D), F32), _sds((L, D_CONF), BF16), _sds((L, D), F32), _sds((L, D), BF16)],
        in_specs=[_row_spec(), _row_spec(), _const_spec((1, D)), _const_spec((1, D)), _const_spec((D_SSD + D_CONF, D)), _row_spec(),
                  _const_spec((8, D)), _const_spec((1, D))],
        out_specs=[_row_spec()] * 4, compiler_params=_cparams(("parallel",)))(ysn, uc, lnw, lnb, w_out, x, mod, n2w)


def mixer_out_bwd(dmix, w_out, uc, lnw, lnb, name):
    L = uc.shape[0]

    def body(dm_ref, wo_ref, u_ref, w_ref, b_ref, dy_ref, o_ref, acc_ref):
        @pl.when(pl.program_id(0) == 0)
        def _():
            acc_ref[...] = jnp.zeros((8, D), F32)

        nt = (((1,), (1,)), ((), ()))
        dm = dm_ref[...]
        dy_ref[...] = lax.dot_general(dm, wo_ref[0:D_SSD, :], nt, preferred_element_type=F32)
        du = lax.dot_general(dm, wo_ref[D_SSD:D_SSD + D_CONF, :], nt, preferred_element_type=F32)
        u = u_ref[...]
        mu = jnp.mean(u, axis=-1, keepdims=True)
        rl = lax.rsqrt(jnp.mean(jnp.square(u - mu), axis=-1, keepdims=True) + 1e-5)
        n = (u - mu) * rl
        v = n * w_ref[...] + b_ref[...]
        dv = du * _dsilu(v)
        acc_ref[0:1, :] += _colsum(dv * n)
        acc_ref[1:2, :] += _colsum(dv)
        dn = dv * w_ref[...]
        o_ref[...] = rl * (dn - jnp.mean(dn, axis=-1, keepdims=True) - n * jnp.mean(dn * n, axis=-1, keepdims=True))

    return pl.pallas_call(body, name=name, grid=(L // TR,), out_shape=[_sds((L, D), F32), _sds((L, D), F32), _sds((8, D), F32)],
                          in_specs=[_row_spec(), _const_spec((D_SSD + D_CONF, D)), _row_spec(), _const_spec((1, D)),
                                    _const_spec((1, D))],
                          out_specs=[_row_spec(), _row_spec(), _const_spec((8, D))],
                          compiler_params=_cparams(("arbitrary",)))(dmix, w_out, uc, lnw, lnb)


def final_loss(act, w_down, x1, mod, fw, target, name):
    L = act.shape[0]

    def body(act_ref, wd_ref, x1_ref, mod_ref, fw_ref, t_ref, dx_ref, dff_ref, dact_ref, acc_ref):
        @pl.when(pl.program_id(0) == 0)
        def _():
            acc_ref[...] = jnp.zeros((8, D), F32)

        ff_v = jnp.dot(act_ref[...], wd_ref[...], preferred_element_type=F32)
        g2 = mod_ref[5:6, :]
        x2 = x1_ref[...] + g2 * ff_v
        r = lax.rsqrt(jnp.mean(x2 * x2, axis=-1, keepdims=True) + 1e-6)
        n = x2 * r
        err = n * fw_ref[...] - t_ref[...]
        dy = err * (1.0 / D)
        dn = dy * fw_ref[...]
        dx2 = r * (dn - n * jnp.mean(dn * n, axis=-1, keepdims=True))
        acc_ref[0:1, :] += _colsum(dy * n)
        acc_ref[1:2, :] += _colsum(dx2 * ff_v)
        acc_ref[2:3, :] += _colsum(err * err)
        dx_ref[...] = dx2
        dff = (dx2 * g2).astype(BF16)
        dff_ref[...] = dff
        dact_ref[...] = lax.dot_general(dff, wd_ref[...], (((1,), (1,)), ((), ())), preferred_element_type=F32)

    return pl.pallas_call(
        body, name=name, grid=(L // TR,),
        out_shape=[_sds((L, D), F32), _sds((L, D), BF16), _sds((L, D_FF), F32), _sds((8, D), F32)],
        in_specs=[_row_spec(D_FF), _const_spec((D_FF, D)), _row_spec(), _const_spec((8, D)), _const_spec((1, D)), _row_spec()],
        out_specs=[_row_spec(), _row_spec(), _row_spec(D_FF), _const_spec((8, D))],
        compiler_params=_cparams(("arbitrary",)))(act, w_down, x1, mod, fw, target)


def norm_mod_bwd(dh_pairs, xin, dres, mod, w, shift_row, name, mix=None, gate_row=None):
    L = xin.shape[0]
    has_mix = mix is not None
    n_pairs = len(dh_pairs)

    def body(*refs):
        pair_refs, refs = refs[:2 * n_pairs], refs[2 * n_pairs:]
        if has_mix:
            x_ref, dres_ref, mod_ref, w_ref, mix_ref, dx_ref, dmix_ref, acc_ref = refs
        else:
            x_ref, dres_ref, mod_ref, w_ref, dx_ref, acc_ref = refs

        @pl.when(pl.program_id(0) == 0)
        def _():
            acc_ref[...] = jnp.zeros((8, D), F32)

        dh_v = None
        for p in range(n_pairs):
            t = jnp.dot(pair_refs[2 * p][...], pair_refs[2 * p + 1][...], preferred_element_type=F32)
            dh_v = t if dh_v is None else dh_v + t
        x = x_ref[...]
        r = lax.rsqrt(jnp.mean(x * x, axis=-1, keepdims=True) + 1e-6)
        n = x * r
        nw = n * w_ref[...]
        sc1 = 1.0 + mod_ref[shift_row + 1:shift_row + 2, :]
        acc_ref[0:1, :] += _colsum(dh_v)
        acc_ref[1:2, :] += _colsum(dh_v * nw)
        dnw = dh_v * sc1
        acc_ref[2:3, :] += _colsum(dnw * n)
        dn = dnw * w_ref[...]
        dx = r * (dn - n * jnp.mean(dn * n, axis=-1, keepdims=True)) + dres_ref[...]
        dx_ref[...] = dx
        if has_mix:
            acc_ref[3:4, :] += _colsum(dx * mix_ref[...])
            dmix_ref[...] = (dx * mod_ref[gate_row:gate_row + 1, :]).astype(BF16)

    ins, in_specs = [], []
    for a, wt, rb in dh_pairs:
        ins += [a, wt]
        in_specs += [_row_spec(a.shape[1]), pl.BlockSpec((a.shape[1], D), functools.partial(lambda i, rb: (rb, 0), rb=rb))]
    ins += [xin, dres, mod, w] + ([mix] if has_mix else [])
    in_specs += [_row_spec(), _row_spec(), _const_spec((8, D)), _const_spec((1, D))] + ([_row_spec()] if has_mix else [])
    out_shape = [_sds((L, D), F32)] + ([_sds((L, D), BF16)] if has_mix else []) + [_sds((8, D), F32)]
    out_specs = [_row_spec()] + ([_row_spec()] if has_mix else []) + [_const_spec((8, D))]
    return pl.pallas_call(body, name=name, grid=(L // TR,), out_shape=out_shape, in_specs=in_specs,
                          out_specs=out_specs, compiler_params=_cparams(("arbitrary",)))(*ins)


def _halo(k):
    return 8 if k <= 9 else 32


def _prev_spec(h, col0):
    return pl.BlockSpec((h, CB), lambda j, i: (jnp.maximum(i * (TC // h) - 1, 0), j + col0))


def _next_spec(h, col0, n_tiles):
    return pl.BlockSpec((h, CB), lambda j, i: (jnp.minimum(i + 1, n_tiles - 1) * (TC // h), j + col0))


def _tile_spec(col0):
    return pl.BlockSpec((TC, CB), lambda j, i: (i, j + col0))


def _w_spec(kp, col0):
    return pl.BlockSpec((kp, CB), lambda j, i: (0, j + col0))


SUBLANES = 8


def _shifted_windows(v, taps, rows):
    for r in range(SUBLANES):
        group = [(o, k) for o, k in taps if o % SUBLANES == r]
        if not group:
            continue
        s = v if r == 0 else pltpu.roll(v, v.shape[0] - r, 0)
        for o, k in group:
            yield k, s[o - r:o - r + rows, :]


def _causal_taps(ext_ref, w_ref, k_taps, first, rows):
    acc = None
    for k, win in _shifted_windows(ext_ref[...], [(first - (k_taps - 1) + k, k) for k in range(k_taps)], rows):
        t = w_ref[k:k + 1, :] * win
        acc = t if acc is None else acc + t
    return acc


def _anticausal_taps(d_ref, w_ref, k_taps, rows):
    acc = None
    for k, win in _shifted_windows(d_ref[...], [(k_taps - 1 - k, k) for k in range(k_taps)], rows):
        t = w_ref[k:k + 1, :] * win
        acc = t if acc is None else acc + t
    return acc


def _acc_conv_wgrad(dw_ref, d_tile, ext_ref, k_taps, first):
    for k, win in _shifted_windows(ext_ref[...], [(first - (k_taps - 1) + k, k) for k in range(k_taps)], TC):
        dw_ref[k:k + 1, :] += _colsum(d_tile * win)
    dw_ref[k_taps:k_taps + 1, :] += _colsum(d_tile)


def conv_silu_fwd(x, col0, width, w, b, name):
    L = x.shape[0]
    k_taps = w.shape[0]
    h = _halo(k_taps)

    def body(xp_ref, x_ref, w_ref, b_ref, o_ref, ext_ref):
        i = pl.program_id(1)
        ext_ref[0:h, :] = jnp.where(i > 0, xp_ref[...], 0.0)
        ext_ref[h:h + TC, :] = x_ref[...]
        o_ref[...] = _silu(_causal_taps(ext_ref, w_ref, k_taps, h, TC) + b_ref[...])

    return pl.pallas_call(
        body, name=name, grid=(width // CB, L // TC), out_shape=_sds((L, width), F32),
        in_specs=[_prev_spec(h, col0), _tile_spec(col0), _w_spec(k_taps, 0), pl.BlockSpec((1, CB), lambda j, i: (0, j))],
        out_specs=_tile_spec(0), scratch_shapes=[pltpu.VMEM((h + TC, CB), F32)],
        compiler_params=_cparams(("parallel", "parallel")))(x, x, w, b)


def conv_silu_bwd(x, col0, width, w, b, dpost, name):
    L = x.shape[0]
    k_taps = w.shape[0]
    h = _halo(k_taps)
    nt = L // TC

    def body(xp_ref, x_ref, xn_ref, d_ref, dn_ref, w_ref, b_ref, dx_ref, dw_ref, ext_ref, dpre_ref):
        i = pl.program_id(1)

        @pl.when(i == 0)
        def _():
            dw_ref[...] = jnp.zeros((8, CB), F32)

        ext_ref[0:h, :] = jnp.where(i > 0, xp_ref[...], 0.0)
        ext_ref[h:h + TC, :] = x_ref[...]
        ext_ref[h + TC:h + TC + h, :] = xn_ref[...]
        pre = _causal_taps(ext_ref, w_ref, k_taps, h, TC + h) + b_ref[...]
        dpre_ref[0:TC, :] = d_ref[...] * _dsilu(pre[0:TC, :])
        dpre_ref[TC:TC + h, :] = jnp.where(i < nt - 1, dn_ref[...], 0.0) * _dsilu(pre[TC:TC + h, :])
        dx_ref[...] = _anticausal_taps(dpre_ref, w_ref, k_taps, TC).astype(BF16)
        _acc_conv_wgrad(dw_ref, dpre_ref[0:TC, :], ext_ref, k_taps, h)

    return pl.pallas_call(
        body, name=name, grid=(width // CB, nt),
        out_shape=[_sds((L, width), BF16), _sds((8, width), F32)],
        in_specs=[_prev_spec(h, col0), _tile_spec(col0), _next_spec(h, col0, nt), _tile_spec(0), _next_spec(h, 0, nt),
                  _w_spec(k_taps, 0), pl.BlockSpec((1, CB), lambda j, i: (0, j))],
        out_specs=[_tile_spec(0), _w_spec(8, 0)],
        scratch_shapes=[pltpu.VMEM((h + TC + h, CB), F32), pltpu.VMEM((TC + h, CB), F32)],
        compiler_params=_cparams(("parallel", "arbitrary")))(x, x, x, dpost, dpost, w, b)


def conf_conv_fwd(proj, col_a, col_g, w, b, name):
    L = proj.shape[0]
    k_taps = w.shape[0]
    h = _halo(k_taps)

    def body(ap_ref, a_ref, gp_ref, g_ref, w_ref, b_ref, o_ref, ext_ref):
        i = pl.program_id(1)
        ext_ref[0:h, :] = jnp.where(i > 0, ap_ref[...] * _sigmoid(gp_ref[...]), 0.0)
        ext_ref[h:h + TC, :] = a_ref[...] * _sigmoid(g_ref[...])
        o_ref[...] = _causal_taps(ext_ref, w_ref, k_taps, h, TC) + b_ref[...]

    return pl.pallas_call(
        body, name=name, grid=(D_CONF // CB, L // TC), out_shape=_sds((L, D_CONF), F32),
        in_specs=[_prev_spec(h, col_a), _tile_spec(col_a), _prev_spec(h, col_g), _tile_spec(col_g), _w_spec(k_taps, 0),
                  pl.BlockSpec((1, CB), lambda j, i: (0, j))],
        out_specs=_tile_spec(0), scratch_shapes=[pltpu.VMEM((h + TC, CB), F32)],
        compiler_params=_cparams(("parallel", "parallel")))(proj, proj, proj, proj, w, b)


def conf_conv_bwd(proj, col_a, col_g, w, duc, name):
    L = proj.shape[0]
    k_taps = w.shape[0]
    h = _halo(k_taps)
    nt = L // TC

    def body(ap_ref, a_ref, gp_ref, g_ref, d_ref, dn_ref, w_ref, da_ref, dg_ref, dw_ref, ext_ref, dext_ref):
        i = pl.program_id(1)

        @pl.when(i == 0)
        def _():
            dw_ref[...] = jnp.zeros((32, CB), F32)

        a = a_ref[...]
        s = _sigmoid(g_ref[...])
        ext_ref[0:h, :] = jnp.where(i > 0, ap_ref[...] * _sigmoid(gp_ref[...]), 0.0)
        ext_ref[h:h + TC, :] = a * s
        dext_ref[0:TC, :] = d_ref[...]
        dext_ref[TC:TC + h, :] = jnp.where(i < nt - 1, dn_ref[...], 0.0)
        du0 = _anticausal_taps(dext_ref, w_ref, k_taps, TC)
        da_ref[...] = (du0 * s).astype(BF16)
        dg_ref[...] = (du0 * a * s * (1.0 - s)).astype(BF16)
        _acc_conv_wgrad(dw_ref, d_ref[...], ext_ref, k_taps, h)

    return pl.pallas_call(
        body, name=name, grid=(D_CONF // CB, nt),
        out_shape=[_sds((L, D_CONF), BF16), _sds((L, D_CONF), BF16), _sds((32, D_CONF), F32)],
        in_specs=[_prev_spec(h, col_a), _tile_spec(col_a), _prev_spec(h, col_g), _tile_spec(col_g), _tile_spec(0),
                  _next_spec(h, 0, nt), _w_spec(k_taps, 0)],
        out_specs=[_tile_spec(0), _tile_spec(0), _w_spec(32, 0)],
        scratch_shapes=[pltpu.VMEM((h + TC, CB), F32), pltpu.VMEM((TC + h, CB), F32)],
        compiler_params=_cparams(("parallel", "arbitrary")))(proj, proj, proj, proj, duc, duc, w)


def ffn_conv_fwd(up, w, b, name):
    L = up.shape[0]
    k_taps = w.shape[0]
    h = _halo(k_taps)
    cv = D_FF // CB

    def body(gp_ref, g_ref, vp_ref, v_ref, wg_ref, wv_ref, bg_ref, bv_ref, o_ref, eg_ref, ev_ref):
        i = pl.program_id(1)
        eg_ref[0:h, :] = jnp.where(i > 0, gp_ref[...], 0.0)
        eg_ref[h:h + TC, :] = g_ref[...]
        ev_ref[0:h, :] = jnp.where(i > 0, vp_ref[...], 0.0)
        ev_ref[h:h + TC, :] = v_ref[...]
        pg = _causal_taps(eg_ref, wg_ref, k_taps, h, TC) + bg_ref[...]
        pv = _causal_taps(ev_ref, wv_ref, k_taps, h, TC) + bv_ref[...]
        o_ref[...] = (_silu(pg) * pv).astype(BF16)

    bspec = lambda c0: pl.BlockSpec((1, CB), lambda j, i: (0, j + c0))
    return pl.pallas_call(
        body, name=name, grid=(cv, L // TC), out_shape=_sds((L, D_FF), BF16),
        in_specs=[_prev_spec(h, 0), _tile_spec(0), _prev_spec(h, cv), _tile_spec(cv), _w_spec(k_taps, 0), _w_spec(k_taps, cv),
                  bspec(0), bspec(cv)],
        out_specs=_tile_spec(0), scratch_shapes=[pltpu.VMEM((h + TC, CB), F32), pltpu.VMEM((h + TC, CB), F32)],
        compiler_params=_cparams(("parallel", "parallel")))(up, up, up, up, w, w, b, b)


def ffn_conv_bwd(up, w, b, dact, name):
    L = up.shape[0]
    k_taps = w.shape[0]
    h = _halo(k_taps)
    nt = L // TC
    cv = D_FF // CB

    def body(gp_ref, g_ref, gn_ref, vp_ref, v_ref, vn_ref, d_ref, dn_ref, wg_ref, wv_ref, bg_ref, bv_ref,
             dg_ref, dv_ref, dwg_ref, dwv_ref, eg_ref, ev_ref, pg_ref, pv_ref):
        i = pl.program_id(1)

        @pl.when(i == 0)
        def _():
            dwg_ref[...] = jnp.zeros((8, CB), F32)
            dwv_ref[...] = jnp.zeros((8, CB), F32)

        for e_ref, p_ref, c_ref, n_ref in ((eg_ref, gp_ref, g_ref, gn_ref), (ev_ref, vp_ref, v_ref, vn_ref)):
            e_ref[0:h, :] = jnp.where(i > 0, p_ref[...], 0.0)
            e_ref[h:h + TC, :] = c_ref[...]
            e_ref[h + TC:h + TC + h, :] = n_ref[...]
        pg = _causal_taps(eg_ref, wg_ref, k_taps, h, TC + h) + bg_ref[...]
        pv = _causal_taps(ev_ref, wv_ref, k_taps, h, TC + h) + bv_ref[...]
        dact_t = d_ref[...]
        dact_n = jnp.where(i < nt - 1, dn_ref[...], 0.0)
        pg_ref[0:TC, :] = dact_t * pv[0:TC, :] * _dsilu(pg[0:TC, :])
        pg_ref[TC:TC + h, :] = dact_n * pv[TC:TC + h, :] * _dsilu(pg[TC:TC + h, :])
        pv_ref[0:TC, :] = dact_t * _silu(pg[0:TC, :])
        pv_ref[TC:TC + h, :] = dact_n * _silu(pg[TC:TC + h, :])
        dg_ref[...] = _anticausal_taps(pg_ref, wg_ref, k_taps, TC).astype(BF16)
        dv_ref[...] = _anticausal_taps(pv_ref, wv_ref, k_taps, TC).astype(BF16)
        _acc_conv_wgrad(dwg_ref, pg_ref[0:TC, :], eg_ref, k_taps, h)
        _acc_conv_wgrad(dwv_ref, pv_ref[0:TC, :], ev_ref, k_taps, h)

    bspec = lambda c0: pl.BlockSpec((1, CB), lambda j, i: (0, j + c0))
    ext = pltpu.VMEM((h + TC + h, CB), F32)
    dpre = pltpu.VMEM((TC + h, CB), F32)
    return pl.pallas_call(
        body, name=name, grid=(cv, nt),
        out_shape=[_sds((L, D_FF), BF16), _sds((L, D_FF), BF16), _sds((8, D_FF), F32), _sds((8, D_FF), F32)],
        in_specs=[_prev_spec(h, 0), _tile_spec(0), _next_spec(h, 0, nt), _prev_spec(h, cv), _tile_spec(cv), _next_spec(h, cv, nt),
                  _tile_spec(0), _next_spec(h, 0, nt), _w_spec(k_taps, 0), _w_spec(k_taps, cv), bspec(0), bspec(cv)],
        out_specs=[_tile_spec(0), _tile_spec(0), _w_spec(8, 0), _w_spec(8, 0)],
        scratch_shapes=[ext, ext, dpre, dpre],
        compiler_params=_cparams(("parallel", "arbitrary")))(up, up, up, up, up, up, dact, dact, w, w, b, b)


def _ssd_common(xbc_ref, dt_ref, dtb_ref, alog_ref, cs_ref):
    xs = xbc_ref[:, 0:D_SSD]
    sp_in = dt_ref[...] + dtb_ref[...]
    dtf = _softplus(sp_in)
    a_f = -jnp.exp(alog_ref[...])
    a_dt = dtf * a_f
    row = lax.broadcasted_iota(jnp.int32, (Q, Q), 0)
    col = lax.broadcasted_iota(jnp.int32, (Q, Q), 1)
    causal = row >= col
    cs = _dot_exact(causal.astype(F32), a_dt, 3, "a")
    cs_ref[...] = cs
    cs_last = cs_ref[Q - 1:Q, :]
    return xs, sp_in, dtf, a_f, cs, cs_last, causal


def _head_decay(cs_j, cst_ref, e, causal):
    lane = lax.broadcasted_iota(jnp.int32, (Q, LANE), 1)
    rolled = pltpu.roll(cs_j, HEAD, 1)
    own = (lane < HEAD) if e == 0 else (lane >= HEAD)
    col_b = jnp.where(own, cs_j, rolled)
    col_b = jnp.concatenate([col_b] * (Q // LANE), axis=1)
    row_b = cst_ref[e * HEAD:e * HEAD + 1, :]
    return jnp.where(causal, jnp.exp(jnp.minimum(col_b - row_b, 0.0)), 0.0)


def ssd_fwd(xbc, z_src, dt_src, dtb_f, alog_f, dsk_f, snw, name):
    L = xbc.shape[0]
    nc = L // Q

    def body(xbc_ref, z_ref, dt_ref, dtb_ref, alog_ref, dsk_ref, snw_ref, y_ref, yn_ref, sp_ref, s_ref, cs_ref, cst_ref, yd_ref):
        @pl.when(pl.program_id(0) == 0)
        def _():
            s_ref[...] = jnp.zeros((N_STATE, D_SSD), F32)

        xs, _, dtf, a_f, cs, cs_last, causal = _ssd_common(xbc_ref, dt_ref, dtb_ref, alog_ref, cs_ref)
        e_cs = jnp.exp(cs)
        xdt = xs * dtf
        zst = jnp.exp(cs_last - cs) * xdt
        sp_ref[0] = s_ref[...]
        lane = lax.broadcasted_iota(jnp.int32, (Q, LANE), 1)
        for g in range(2):
            gl = slice(g * 512, g * 512 + 512)
            b_g = xbc_ref[:, D_SSD + g * N_STATE:D_SSD + (g + 1) * N_STATE]
            c_g = xbc_ref[:, D_SSD + 2 * N_STATE + g * N_STATE:D_SSD + 2 * N_STATE + (g + 1) * N_STATE]
            s_prev = s_ref[:, gl]
            cb = _dot_nt(c_g, b_g)
            yd_ref[:, gl] = e_cs[:, gl] * _dot(c_g, s_prev)
            for j in range(4):
                tl = slice(g * 512 + j * LANE, g * 512 + (j + 1) * LANE)
                cs_j = cs[:, tl]
                cst_ref[...] = cs_j.T
                x_j = xdt[:, tl]
                o0 = _dot(cb * _head_decay(cs_j, cst_ref, 0, causal), x_j)
                o1 = _dot(cb * _head_decay(cs_j, cst_ref, 1, causal), x_j)
                yd_ref[:, tl] += jnp.where(lane < HEAD, o0, o1)
            s_ref[:, gl] = jnp.exp(cs_last[:, gl]) * s_prev + _dot_tn(b_g, zst[:, gl])
        y = yd_ref[...] + xs * dsk_ref[...]
        y_ref[...] = y
        yz = y * _silu(z_ref[...])
        r = lax.rsqrt(jnp.mean(yz * yz, axis=-1, keepdims=True) + 1e-6)
        yn_ref[...] = (yz * r * snw_ref[...]).astype(BF16)

    chunk = lambda w, c: pl.BlockSpec((Q, w), lambda i: (i, c))
    return pl.pallas_call(
        body, name=name, grid=(nc,),
        out_shape=[_sds((L, D_SSD), F32), _sds((L, D_SSD), BF16), _sds((nc, N_STATE, D_SSD), F32)],
        in_specs=[chunk(D_XBC, 0), chunk(D, 0), chunk(D, 0)] + [_const_spec((1, D))] * 4,
        out_specs=[chunk(D, 0), chunk(D, 0), pl.BlockSpec((1, N_STATE, D_SSD), lambda i: (i, 0, 0))],
        scratch_shapes=[pltpu.VMEM((N_STATE, D_SSD), F32), pltpu.VMEM((Q, D_SSD), F32), pltpu.VMEM((LANE, Q), F32),
                        pltpu.VMEM((Q, D_SSD), F32)],
        compiler_params=_cparams(("arbitrary",)))(xbc, z_src, dt_src, dtb_f, alog_f, dsk_f, snw)


def ssd_bwd(dysn, y, xbc, z_src, dt_src, s_prev_all, dtb_f, alog_f, dsk_f, snw, name):
    L = xbc.shape[0]
    nc = L // Q

    def body(dyn_ref, y_ref, xbc_ref, z_ref, dt_ref, sp_ref, dtb_ref, alog_ref, dsk_ref, snw_ref,
             dz_ref, ddt_ref, dxbc_ref, acc_ref, acc16_ref, ds_ref, cs_ref, cst_ref, dcs_ref, dx_ref):
        step = pl.program_id(0)

        @pl.when(step == 0)
        def _():
            ds_ref[...] = jnp.zeros((N_STATE, D_SSD), F32)
            acc_ref[...] = jnp.zeros((8, D), F32)

        z = z_ref[...]
        y = y_ref[...]
        sz = _sigmoid(z)
        siluz = z * sz
        yz = y * siluz
        r = lax.rsqrt(jnp.mean(yz * yz, axis=-1, keepdims=True) + 1e-6)
        n = yz * r
        dyn = dyn_ref[...]
        acc_ref[0:1, :] += _colsum(dyn * n)
        dn = dyn * snw_ref[...]
        dyz = r * (dn - n * jnp.mean(dn * n, axis=-1, keepdims=True))
        dy = dyz * siluz
        dz_ref[...] = (dyz * y * (sz * (1.0 + z * (1.0 - sz)))).astype(BF16)

        xs, sp_in, dtf, a_f, cs, cs_last, causal = _ssd_common(xbc_ref, dt_ref, dtb_ref, alog_ref, cs_ref)
        acc_ref[3:4, :] += _colsum(dy * xs)
        e_cs = jnp.exp(cs)
        xdt = xs * dtf
        dst = jnp.exp(cs_last - cs)
        zst = dst * xdt
        e_last = jnp.exp(cs_last)
        lane = lax.broadcasted_iota(jnp.int32, (Q, LANE), 1)
        ones = jnp.ones((Q, LANE), F32)
        dcs_last_parts = []
        for g in range(2):
            gl = slice(g * 512, g * 512 + 512)
            b_g = xbc_ref[:, D_SSD + g * N_STATE:D_SSD + (g + 1) * N_STATE]
            c_g = xbc_ref[:, D_SSD + 2 * N_STATE + g * N_STATE:D_SSD + 2 * N_STATE + (g + 1) * N_STATE]
            s_prev = sp_ref[0, :, gl]
            ds_g = ds_ref[:, gl]
            dy_g = dy[:, gl]
            cb = _dot_nt(c_g, b_g)
            y_off = e_cs[:, gl] * _dot(c_g, s_prev)
            edy = e_cs[:, gl] * dy_g
            d_c = _dot_nt(edy, s_prev)
            d_z = _dot(b_g, ds_g)
            d_b = _dot_nt(zst[:, gl], ds_g)
            t_g = d_z * zst[:, gl]
            dcs_ref[:, gl] = dy_g * y_off - t_g
            dx_ref[:, gl] = d_z * dst[:, gl]
            dcs_last_parts.append(_colsum(t_g) + _colsum(ds_g * s_prev) * e_last[:, gl])
            ds_ref[:, gl] = e_last[:, gl] * ds_g + _dot_tn(c_g, edy)
            dcb = jnp.zeros((Q, Q), F32)
            for j in range(4):
                tl = slice(g * 512 + j * LANE, g * 512 + (j + 1) * LANE)
                cs_j = cs[:, tl]
                cst_ref[...] = cs_j.T
                x_j = xdt[:, tl]
                dy_j = dy[:, tl]
                dx_j = jnp.zeros((Q, LANE), F32)
                dcs_j = jnp.zeros((Q, LANE), F32)
                for e in range(2):
                    own = (lane < HEAD) if e == 0 else (lane >= HEAD)
                    w_h = _head_decay(cs_j, cst_ref, e, causal)
                    g_h = cb * w_h
                    dy_m = jnp.where(own, dy_j, 0.0)
                    d_g = _dot_nt(dy_m, x_j)
                    dx_j = dx_j + _dot_tn(g_h, dy_m)
                    dcb = dcb + d_g * w_h
                    p_h = d_g * g_h
                    row_sums = _dot_exact(p_h, ones, 2, "b")
                    col_sums = _dot_exact(p_h, ones, 2, "b", (((0,), (0,)), ((), ())))
                    dcs_j = dcs_j + jnp.where(own, row_sums - col_sums, 0.0)
                dcs_ref[:, tl] += dcs_j * (1.0 / HEAD)
                dx_ref[:, tl] += dx_j
            d_c = d_c + _dot(dcb, b_g)
            d_b = d_b + _dot_tn(dcb, c_g)
            dxbc_ref[:, D_SSD + g * N_STATE:D_SSD + (g + 1) * N_STATE] = d_b
            dxbc_ref[:, D_SSD + 2 * N_STATE + g * N_STATE:D_SSD + 2 * N_STATE + (g + 1) * N_STATE] = d_c
        dcs_last = jnp.concatenate(dcs_last_parts, axis=1)
        anticausal = lax.broadcasted_iota(jnp.int32, (Q, Q), 0) <= lax.broadcasted_iota(jnp.int32, (Q, Q), 1)
        d_adt = _dot_exact(anticausal.astype(F32), dcs_ref[...], 3, "a") + dcs_last
        dx = dx_ref[...]
        acc_ref[2:3, :] += _colsum(d_adt * dtf) * a_f
        d_dtf = d_adt * a_f + dx * xs
        dxbc_ref[:, 0:D_SSD] = dx * dtf + dy * dsk_ref[...]
        d_raw = d_dtf * _sigmoid(sp_in)
        acc_ref[1:2, :] += _colsum(d_raw)
        head_of_lane = lax.broadcasted_iota(jnp.int32, (D_SSD, LANE), 0) // HEAD
        fold = (head_of_lane == lax.broadcasted_iota(jnp.int32, (D_SSD, LANE), 1)).astype(F32)
        ddt_ref[...] = _dot_exact(d_raw, fold, 2, "b").astype(BF16)

        @pl.when(step == nc - 1)
        def _():
            acc16_ref[...] = _dot_exact(acc_ref[...], fold, 3, "b")

    rchunk = lambda w, c: pl.BlockSpec((Q, w), lambda i: (nc - 1 - i, c))
    return pl.pallas_call(
        body, name=name, grid=(nc,),
        out_shape=[_sds((L, D_SSD), BF16), _sds((L, LANE), BF16), _sds((L, D_XBC), F32), _sds((8, D), F32), _sds((8, LANE), F32)],
        in_specs=[rchunk(D, 0), rchunk(D, 0), rchunk(D_XBC, 0), rchunk(D, 0), rchunk(D, 0),
                  pl.BlockSpec((1, N_STATE, D_SSD), lambda i: (nc - 1 - i, 0, 0))] + [_const_spec((1, D))] * 4,
        out_specs=[rchunk(D, 0), rchunk(LANE, 0), rchunk(D_XBC, 0), _const_spec((8, D)), _const_spec((8, LANE))],
        scratch_shapes=[pltpu.VMEM((N_STATE, D_SSD), F32), pltpu.VMEM((Q, D_SSD), F32), pltpu.VMEM((LANE, Q), F32),
                        pltpu.VMEM((Q, D_SSD), F32), pltpu.VMEM((Q, D_SSD), F32)],
        compiler_params=_cparams(("arbitrary",)))(dysn, y, xbc, z_src, dt_src, s_prev_all, dtb_f, alog_f, dsk_f, snw)


def _adamw_math(w, g, m, v):
    m_n = ADAM_B1 * m + (1.0 - ADAM_B1) * g
    v_n = ADAM_B2 * v + (1.0 - ADAM_B2) * jnp.square(g)
    c1 = 1.0 - ADAM_B1 ** ADAM_STEP
    c2 = 1.0 - ADAM_B2 ** ADAM_STEP
    return -ADAM_LR * ((m_n / c1) / (jnp.sqrt(v_n / c2) + ADAM_EPS) + ADAM_WD * w), m_n, v_n


def _sum_slots(p_ref):
    acc = p_ref[0].astype(F32)
    for s in range(1, p_ref.shape[0]):
        acc = acc + p_ref[s].astype(F32)
    return acc


def adamw_slots(w, slots, m, v, name):
    rows, cols = w.shape
    tc = 256

    def body(w_ref, s_ref, m_ref, v_ref, g_ref, d_ref, mo_ref, vo_ref):
        g_v = _sum_slots(s_ref)
        g_ref[...] = g_v
        d_ref[...], mo_ref[...], vo_ref[...] = _adamw_math(w_ref[...], g_v, m_ref[...], v_ref[...])

    spec = pl.BlockSpec((rows, tc), lambda i: (0, i))
    return pl.pallas_call(body, name=name, grid=(cols // tc,), out_shape=[_sds((rows, cols), F32)] * 4,
                          in_specs=[spec, pl.BlockSpec((slots.shape[0], rows, tc), lambda i: (0, 0, i)), spec, spec], out_specs=[spec] * 4,
                          compiler_params=_cparams(("parallel",)))(w, slots, m, v)


def adamw_many(ws, gs, ms, vs, name):
    n = len(ws)

    def body(*refs):
        for p in range(n):
            d_v, m_v, v_v = _adamw_math(refs[p][...], refs[n + p][...], refs[2 * n + p][...], refs[3 * n + p][...])
            refs[4 * n + p][...] = d_v
            refs[5 * n + p][...] = m_v
            refs[6 * n + p][...] = v_v

    vm = pl.BlockSpec(memory_space=pltpu.VMEM)
    out = pl.pallas_call(body, name=name, out_shape=[_sds(w.shape, F32) for w in ws] * 3, in_specs=[vm] * (4 * n),
                         out_specs=[vm] * (3 * n), compiler_params=_cparams())(*ws, *gs, *ms, *vs)
    return out[:n], out[n:2 * n], out[2 * n:]


def _pack_layout(shapes):
    row, layout = 0, []
    for rows, cols in shapes:
        chunks = []
        for c0 in range(0, cols, D):
            chunks.append((row, c0, min(D, cols - c0)))
            row += rows
        layout.append(chunks)
    return row, layout


def pack_rows(entries, name):
    arrays = [e[0] for e in entries]
    used, layout = _pack_layout([(e[2], e[0].shape[1]) for e in entries])
    total = -(-used // SUBLANES) * SUBLANES
    n = len(arrays)

    def body(*refs):
        o_ref = refs[n]
        o_ref[...] = jnp.zeros((total, D), F32)
        for p in range(n):
            _, first, rows = entries[p]
            for r0, c0, w in layout[p]:
                o_ref[r0:r0 + rows, 0:w] = refs[p][first:first + rows, c0:c0 + w]

    vm = pl.BlockSpec(memory_space=pltpu.VMEM)
    return pl.pallas_call(body, name=name, out_shape=_sds((total, D), F32), in_specs=[vm] * n, out_specs=vm,
                          compiler_params=_cparams())(*arrays)


def unpack_rows(packed, shapes):
    _, layout = _pack_layout(shapes)
    out = []
    for (rows, _), chunks in zip(shapes, layout):
        parts = [packed[..., r0:r0 + rows, 0:w] for r0, _, w in chunks]
        out.append(parts[0] if len(parts) == 1 else jnp.concatenate(parts, axis=-1))
    return out


def sum_slots_many(parts, name):
    n = len(parts)

    def body(*refs):
        for p in range(n):
            refs[n + p][...] = _sum_slots(refs[p])

    vm = pl.BlockSpec(memory_space=pltpu.VMEM)
    return pl.pallas_call(body, name=name, out_shape=[_sds(p.shape[1:], F32) for p in parts], in_specs=[vm] * n,
                          out_specs=[vm] * n, compiler_params=_cparams())(*parts)


def ada_mod(c_all, ada_w_shard, ada_b_cols, name):
    def body(c_ref, w_ref, b_ref, o_ref, ca_ref):
        ca = _silu(c_ref[...])
        ca_ref[...] = ca
        o_ref[...] = _dot(ca, w_ref[...]) + b_ref[...]

    vm = pl.BlockSpec(memory_space=pltpu.VMEM)
    return pl.pallas_call(body, name=name, out_shape=[_sds((N_DEV, ada_w_shard.shape[1]), F32), _sds((N_DEV, D), F32)],
                          in_specs=[vm, vm, vm], out_specs=[vm, vm], compiler_params=_cparams())(c_all, ada_w_shard, ada_b_cols)


def ada_wgrad(c_act_all, dmod_cols, name):
    def body(c_ref, d_ref, o_ref):
        o_ref[...] = _dot_tn_hi(c_ref[...], d_ref[...])

    vm = pl.BlockSpec(memory_space=pltpu.VMEM)
    return pl.pallas_call(body, name=name, out_shape=_sds((D, dmod_cols.shape[1]), F32), in_specs=[vm, vm], out_specs=vm,
                          compiler_params=_cparams())(c_act_all, dmod_cols)


def exchange(srcs, name, gather):
    n = len(srcs)
    gathers = [gather] * n if isinstance(gather, bool) else list(gather)
    shapes = [tuple(s.shape) if g else tuple(s.shape[1:]) for s, g in zip(srcs, gathers)]

    def body(*refs):
        src_refs, out_refs = refs[:n], refs[n:2 * n]
        send_sems, recv_sems, local_sems = refs[2 * n:]
        x, y, c = lax.axis_index("x"), lax.axis_index("y"), lax.axis_index("c")
        me = 4 * x + 2 * y + c

        def peer(k):
            bx, by, bc = (k >> 2) & 1, (k >> 1) & 1, k & 1
            px, py, pc = (x + bx) % 2, (y + by) % 2, (c + bc) % 2
            return (px, py, pc), 4 * px + 2 * py + pc

        def copy(a, k, landing):
            dev, idx = peer(k)
            return pltpu.make_async_remote_copy(
                src_ref=src_refs[a] if gathers[a] else src_refs[a].at[idx], dst_ref=out_refs[a].at[idx if landing else me],
                send_sem=send_sems.at[a, k - 1], recv_sem=recv_sems.at[a, k - 1],
                device_id=dev, device_id_type=pl.DeviceIdType.MESH)

        mine = [pltpu.make_async_copy(src_refs[a] if gathers[a] else src_refs[a].at[me], out_refs[a].at[me], local_sems.at[a])
                for a in range(n)]
        for cp in mine:
            cp.start()
        sends = [copy(a, k, False) for a in range(n) for k in range(1, N_DEV)]
        for cp in sends:
            cp.start()
        for a in range(n):
            for k in range(1, N_DEV):
                copy(a, k, True).wait_recv()
        for cp in sends:
            cp.wait_send()
        for cp in mine:
            cp.wait()

    hbm = pl.BlockSpec(memory_space=pl.ANY)
    return pl.pallas_call(
        body, name=name, out_shape=[_sds((N_DEV,) + shp, s.dtype) for shp, s in zip(shapes, srcs)], in_specs=[hbm] * n,
        out_specs=[hbm] * n,
        scratch_shapes=[pltpu.SemaphoreType.DMA((n, N_DEV - 1)), pltpu.SemaphoreType.DMA((n, N_DEV - 1)),
                        pltpu.SemaphoreType.DMA((n,))],
        compiler_params=pltpu.CompilerParams(has_side_effects=True))(*srcs)


def gather_two_level(srcs, name):
    n = len(srcs)

    def body(*refs):
        src_refs, out_refs = refs[:n], refs[n:2 * n]
        send_sems, recv_sems, local_sems = refs[2 * n:]
        x, y, c = lax.axis_index("x"), lax.axis_index("y"), lax.axis_index("c")
        me, sibling = (x, y, c), (x, y, 1 - c)
        chips = [(1 - x, y), (x, 1 - y), (1 - x, 1 - y)]

        def slot(a, px, py, pc):
            return out_refs[a].at[4 * px + 2 * py + pc]

        def copy(a, k, block, to, src=None):
            return pltpu.make_async_remote_copy(
                src_ref=slot(a, *block) if src is None else src, dst_ref=slot(a, *block), send_sem=send_sems.at[a, k],
                recv_sem=recv_sems.at[a, k], device_id=to, device_id_type=pl.DeviceIdType.MESH)

        mine = [pltpu.make_async_copy(src_refs[a], slot(a, *me), local_sems.at[a]) for a in range(n)]
        for cp in mine:
            cp.start()
        first = []
        for a in range(n):
            first += [copy(a, 0, me, sibling, src=src_refs[a])]
            first += [copy(a, 1 + j, me, (*chip, c), src=src_refs[a]) for j, chip in enumerate(chips)]
        for cp in first:
            cp.start()
        passed = []
        for a in range(n):
            for j, chip in enumerate(chips):
                copy(a, 1 + j, (*chip, c), me).wait_recv()
                passed.append(copy(a, 4 + j, (*chip, c), sibling))
                passed[-1].start()
        for a in range(n):
            copy(a, 0, sibling, me).wait_recv()
            for j, chip in enumerate(chips):
                copy(a, 4 + j, (*chip, 1 - c), me).wait_recv()
        for cp in first + passed:
            cp.wait_send()
        for cp in mine:
            cp.wait()

    hbm = pl.BlockSpec(memory_space=pl.ANY)
    return pl.pallas_call(
        body, name=name, out_shape=[_sds((N_DEV,) + tuple(s.shape), s.dtype) for s in srcs], in_specs=[hbm] * n,
        out_specs=[hbm] * n,
        scratch_shapes=[pltpu.SemaphoreType.DMA((n, N_DEV - 1)), pltpu.SemaphoreType.DMA((n, N_DEV - 1)),
                        pltpu.SemaphoreType.DMA((n,))],
        compiler_params=pltpu.CompilerParams(has_side_effects=True))(*srcs)


def _peer(k):
    x, y, c = lax.axis_index("x"), lax.axis_index("y"), lax.axis_index("c")
    px, py, pc = (x + ((k >> 2) & 1)) % 2, (y + ((k >> 1) & 1)) % 2, (c + (k & 1)) % 2
    return (px, py, pc), 4 * px + 2 * py + pc


def _my_slot():
    return 4 * lax.axis_index("x") + 2 * lax.axis_index("y") + lax.axis_index("c")


_HBM = pl.BlockSpec(memory_space=pltpu.HBM)
_SEM = pl.BlockSpec(memory_space=pltpu.SEMAPHORE)
_EFFECT = pltpu.SideEffectType.DATAFLOW_SIDE_EFFECTING


def exchange_start(srcs, name, gather):
    n = len(srcs)
    shapes = [tuple(s.shape) if gather else tuple(s.shape[1:]) for s in srcs]
    lands = [lax.empty((N_DEV,) + shp, s.dtype) for shp, s in zip(shapes, srcs)]

    def body(*refs):
        src_refs, land_refs = refs[:n], refs[n:2 * n]
        sems = refs[2 * n:4 * n]
        token = refs[-1]
        me = _my_slot()
        for a in range(n):
            for k in range(1, N_DEV):
                dev, idx = _peer(k)
                pltpu.make_async_remote_copy(
                    src_ref=src_refs[a] if gather else src_refs[a].at[idx], dst_ref=land_refs[a].at[me],
                    send_sem=sems[2 * a].at[k - 1], recv_sem=sems[2 * a + 1].at[k - 1],
                    device_id=dev, device_id_type=pl.DeviceIdType.MESH).start()
        token[...] = jnp.zeros_like(token)

    out_shape = ([pltpu.SemaphoreType.DMA((N_DEV - 1,))] * (2 * n) + [pltpu.HBM(s.shape, s.dtype) for s in srcs]
                 + [pltpu.HBM(l.shape, l.dtype) for l in lands] + [_sds((8, LANE), F32)])
    out = pl.pallas_call(
        body, name=name, out_shape=out_shape, in_specs=[_HBM] * (2 * n),
        out_specs=[_SEM] * (2 * n) + [_HBM] * (2 * n) + [pl.BlockSpec(memory_space=pltpu.VMEM)],
        input_output_aliases={i: 2 * n + i for i in range(2 * n)},
        compiler_params=pltpu.CompilerParams(has_side_effects=_EFFECT))(
            *[pltpu.with_memory_space_constraint(s, pltpu.HBM) for s in srcs],
            *[pltpu.with_memory_space_constraint(l, pltpu.HBM) for l in lands])
    parts = [(out[2 * a], out[2 * a + 1], out[2 * n + a], out[3 * n + a]) for a in range(n)]
    return parts, out[-1]


def exchange_wait(parts, after, name, gather):
    n = len(parts)

    def body(*refs):
        src_refs, land_refs = refs[:n], refs[n:2 * n]
        sems = refs[2 * n:4 * n]
        for a in range(n):
            for k in range(1, N_DEV):
                dev, idx = _peer(k)
                copy = pltpu.make_async_remote_copy(
                    src_ref=src_refs[a] if gather else src_refs[a].at[idx], dst_ref=land_refs[a].at[idx],
                    send_sem=sems[2 * a].at[k - 1], recv_sem=sems[2 * a + 1].at[k - 1],
                    device_id=dev, device_id_type=pl.DeviceIdType.MESH)
                copy.wait_send()
                copy.wait_recv()

    srcs = [p[2] for p in parts]
    lands = [p[3] for p in parts]
    sems = [s for p in parts for s in p[:2]]
    out = pl.pallas_call(
        body, name=name, out_shape=[pltpu.HBM(a.shape, a.dtype) for a in srcs + lands],
        in_specs=[_HBM] * (2 * n) + [_SEM] * (2 * n) + [pl.BlockSpec(memory_space=pl.ANY)], out_specs=[_HBM] * (2 * n),
        input_output_aliases={i: i for i in range(2 * n)},
        compiler_params=pltpu.CompilerParams(has_side_effects=_EFFECT))(*srcs, *lands, *sems, after)
    return list(zip(out[:n], out[n:]))


def _cols_to_slabs(g):
    r, c = g.shape
    return g.reshape(r, N_DEV, c // N_DEV).transpose(1, 0, 2)


def _slabs_to_cols(s):
    _, r, cs = s.shape
    return s.transpose(1, 0, 2).reshape(r, N_DEV * cs)


def _rep_heads(v):
    return jnp.repeat(v.reshape(N_HEADS), HEAD).reshape(1, D_SSD)


def local_fwd_bwd(x, target, mod, get_w, put_grad, small):
    n1w, n2w, fnw = small["norm1_w"], small["norm2_w"], small["final_norm_w"]
    dtb_f, alog_f, dsk_f = _rep_heads(small["dt_bias"]), _rep_heads(small["a_log"]), _rep_heads(small["d_skip"])
    snw = small["ssd_norm_w"]

    def after(v, token):
        return v + token[0:1, 0:1]

    w_in = get_w("w_in", mod)
    h1, proj_zx, proj_dt, proj_cf = in_proj(x, mod, n1w, w_in["w_full"], w_in["w_dt_rep"], w_in["w_cf"], "norm1_in_proj")
    xbc = conv_silu_fwd(proj_zx, D_SSD // CB, D_XBC, small["ssd_conv_w"], small["ssd_conv_b"], "ssd_conv")
    y, ysn, s_prev = ssd_fwd(xbc, proj_zx, proj_dt, dtb_f, alog_f, dsk_f, snw, "ssd_scan")
    uc = conf_conv_fwd(proj_cf, 0, D_CONF // CB, small["conf_conv_w"], small["conf_conv_b"], "conf_conv")
    w_out = get_w("w_out", uc)
    mix, u, x1, h2 = mixer_out(ysn, uc, small["conf_ln_w"], small["conf_ln_b"], w_out, x, mod, n2w, "out_proj_norm2")
    w_up_t = get_w("w_up", h2)
    up = mm_nt([(h2, w_up_t, 0)], "up_proj")
    act = ffn_conv_fwd(up, small["ffn_conv_w"], small["ffn_conv_b"], "ffn_conv")
    w_down = get_w("w_down", act)
    dx2, dff, dact, acc_f = final_loss(act, w_down, x1, mod, fnw, target, "down_proj_loss")

    token = put_grad("w_down", mm_tn(act, dff, "wgrad_down"))
    dupg, dupv, dwg, dwv = ffn_conv_bwd(up, small["ffn_conv_w"], after(small["ffn_conv_b"], token), dact, "ffn_conv_bwd")
    token = put_grad("w_up", mm_tn_stack([dupg, dupv], h2, "wgrad_up"))
    dx1, dmix, acc_2 = norm_mod_bwd([(dupg, w_up_t, 0), (dupv, w_up_t, 1)], x1, dx2, mod, after(n2w, token), 3, "norm2_bwd",
                                    mix=mix, gate_row=2)

    token = put_grad("w_out", mm_tn_stack([ysn, u], dmix, "wgrad_out"))
    dysn, duc, acc_ln = mixer_out_bwd(dmix, w_out, uc, after(small["conf_ln_w"], token), small["conf_ln_b"], "out_proj_bwd")
    dcfa, dcfg, dw_cc = conf_conv_bwd(proj_cf, 0, D_CONF // CB, small["conf_conv_w"], duc, "conf_conv_bwd")
    dz, ddt, dxbc_post, acc_s, acc_s16 = ssd_bwd(dysn, y, xbc, proj_zx, proj_dt, s_prev, dtb_f, alog_f, dsk_f, snw,
                                                 "ssd_scan_bwd")
    dxbc, dw_sc = conv_silu_bwd(proj_zx, D_SSD // CB, D_XBC, small["ssd_conv_w"], small["ssd_conv_b"], dxbc_post, "ssd_conv_bwd")
    token = put_grad("w_in", mm_tn_concat(
        [(dz, D_SSD), (dxbc, D_XBC), (ddt, N_HEADS), (dcfa, D_CONF), (dcfg, D_CONF)], h1, "wgrad_in"))
    dh1_pairs = [(dz, w_in["w_full"], 0), (ddt, w_in["w_dt16"], 0), (dcfa, w_in["w_cf"], 0), (dcfg, w_in["w_cf"], 1),
                 (dxbc, w_in["w_xbc"], 0)]
    grad_x, acc_1 = norm_mod_bwd(dh1_pairs, x, dx1, mod, after(n1w, token), 0, "norm1_bwd")

    small_accs = dict(acc_1=acc_1, acc_2=acc_2, acc_f=acc_f, acc_ln=acc_ln, acc_s=acc_s, acc_s16=acc_s16, dw_sc=dw_sc,
                      dw_cc=dw_cc, dwg=dwg, dwv=dwv)
    return grad_x, small_accs


def kernel(x, c, ada_w, ada_b, norm1_w, w_in, ssd_conv_w, ssd_conv_b, dt_bias, a_log, d_skip, ssd_norm_w, conf_conv_w, conf_conv_b, conf_ln_w, conf_ln_b, w_out, norm2_w, w_up, ffn_conv_w, ffn_conv_b, w_down, final_norm_w, loss_target, m_ada_w, m_ada_b, m_norm1_w, m_w_in, m_ssd_conv_w, m_ssd_conv_b, m_dt_bias, m_a_log, m_d_skip, m_ssd_norm_w, m_conf_conv_w, m_conf_conv_b, m_conf_ln_w, m_conf_ln_b, m_w_out, m_norm2_w, m_w_up, m_ffn_conv_w, m_ffn_conv_b, m_w_down, m_final_norm_w, v_ada_w, v_ada_b, v_norm1_w, v_w_in, v_ssd_conv_w, v_ssd_conv_b, v_dt_bias, v_a_log, v_d_skip, v_ssd_norm_w, v_conf_conv_w, v_conf_conv_b, v_conf_ln_w, v_conf_ln_b, v_w_out, v_norm2_w, v_w_up, v_ffn_conv_w, v_ffn_conv_b, v_w_down, v_final_norm_w):
    me = 4 * lax.axis_index("x") + 2 * lax.axis_index("y") + lax.axis_index("c")
    weights = dict(ada_w=ada_w, ada_b=ada_b, norm1_w=norm1_w, w_in=w_in, ssd_conv_w=ssd_conv_w, ssd_conv_b=ssd_conv_b,
                   dt_bias=dt_bias, a_log=a_log, d_skip=d_skip, ssd_norm_w=ssd_norm_w, conf_conv_w=conf_conv_w,
                   conf_conv_b=conf_conv_b, conf_ln_w=conf_ln_w, conf_ln_b=conf_ln_b, w_out=w_out, norm2_w=norm2_w, w_up=w_up,
                   ffn_conv_w=ffn_conv_w, ffn_conv_b=ffn_conv_b, w_down=w_down, final_norm_w=final_norm_w)
    moms_m = dict(ada_w=m_ada_w, ada_b=m_ada_b, norm1_w=m_norm1_w, w_in=m_w_in, ssd_conv_w=m_ssd_conv_w, ssd_conv_b=m_ssd_conv_b,
                  dt_bias=m_dt_bias, a_log=m_a_log, d_skip=m_d_skip, ssd_norm_w=m_ssd_norm_w, conf_conv_w=m_conf_conv_w,
                  conf_conv_b=m_conf_conv_b, conf_ln_w=m_conf_ln_w, conf_ln_b=m_conf_ln_b, w_out=m_w_out, norm2_w=m_norm2_w,
                  w_up=m_w_up, ffn_conv_w=m_ffn_conv_w, ffn_conv_b=m_ffn_conv_b, w_down=m_w_down, final_norm_w=m_final_norm_w)
    moms_v = dict(ada_w=v_ada_w, ada_b=v_ada_b, norm1_w=v_norm1_w, w_in=v_w_in, ssd_conv_w=v_ssd_conv_w, ssd_conv_b=v_ssd_conv_b,
                  dt_bias=v_dt_bias, a_log=v_a_log, d_skip=v_d_skip, ssd_norm_w=v_ssd_norm_w, conf_conv_w=v_conf_conv_w,
                  conf_conv_b=v_conf_conv_b, conf_ln_w=v_conf_ln_w, conf_ln_b=v_conf_ln_b, w_out=v_w_out, norm2_w=v_norm2_w,
                  w_up=v_w_up, ffn_conv_w=v_ffn_conv_w, ffn_conv_b=v_ffn_conv_b, w_down=v_w_down, final_norm_w=v_final_norm_w)
    names = list(weights)

    def to2d(a):
        return a[0] if a.ndim == 3 else a.reshape(1, -1)

    big = ("w_in", "w_out", "w_up", "w_down")

    def rows_of(a):
        return jnp.swapaxes(a, 1, 2)[0] if a.shape[2] != D else a[0]

    shards = [rows_of(weights[n]).astype(BF16) for n in big]

    c_all, scw_all, ccw_all, fcw_all, w_in_slabs = gather_two_level(
        [c.reshape(8, LANE), ssd_conv_w[0], conf_conv_w[0], ffn_conv_w[0], shards[0]], "gather_first")
    c_all = c_all.reshape(N_DEV, D)

    ada_cols = ada_w.shape[2]
    ada_b_cols = lax.dynamic_slice(ada_b, (0, me * ada_cols), (1, ada_cols))
    mod_cols, c_act_all = ada_mod(c_all, ada_w[0], ada_b_cols, "ada_mod")
    mod_parts, = exchange([jnp.pad(mod_cols, ((0, 0), (0, D - ada_cols))).reshape(N_DEV, 8, LANE)], "scatter_mod", gather=False)
    mod = mod_parts.reshape(N_DEV, D)[:, :ada_cols].reshape(6, D)
    mod = jnp.pad(mod, ((0, 2), (0, 0)))

    later, mod = lax.optimization_barrier((shards[1:], mod))
    gather_parts, token = exchange_start(later, "gather_weights_start", gather=True)
    mod = mod + token[0:1, 0:1]

    small = {n: to2d(weights[n]) for n in names if n not in ("ada_w",) + big}
    small["ssd_conv_w"] = _slabs_to_cols(scw_all)
    small["conf_conv_w"] = _slabs_to_cols(ccw_all)
    small["ffn_conv_w"] = _slabs_to_cols(fcw_all)

    def with_own(landed, own):
        return lax.dynamic_update_slice(landed, own[None], (me,) + (0,) * own.ndim)

    def get_w(n, after):
        if n == "w_in":
            slabs = w_in_slabs
        else:
            a = big.index(n)
            (own, landed), = exchange_wait([gather_parts[a - 1]], after, "gather_" + n + "_wait", gather=True)
            slabs = with_own(landed, own)
        full = slabs.reshape(N_DEV * slabs.shape[1], D)
        if n != "w_in":
            return full
        w_dt = full[D_SSD + D_XBC:D_SSD + D_XBC + N_HEADS]
        return dict(w_full=full, w_xbc=full[D_SSD:D_SSD + D_XBC], w_cf=full[D_SSD + D_XBC + N_HEADS:],
                    w_dt_rep=jnp.repeat(w_dt, HEAD, axis=0), w_dt16=jnp.pad(w_dt, ((0, LANE - N_HEADS), (0, 0))))

    scatter_parts = {}

    def put_grad(n, g):
        slabs = g if g.ndim == 3 else g.reshape(N_DEV, g.shape[0] // N_DEV, g.shape[1])
        (scatter_parts[n],), token = exchange_start([slabs.astype(BF16)], "scatter_" + n + "_start", gather=False)
        return token

    grad_x, accs = local_fwd_bwd(x[0], loss_target[0], mod, get_w, put_grad, small)

    grads, delta, new_m, new_v = {}, {}, {}, {}

    def finish(ns, after, name):
        landed = exchange_wait([scatter_parts[n] for n in ns], after, name, gather=False)
        for n, (sent, slots) in zip(ns, landed):
            slots = with_own(slots, lax.dynamic_index_in_dim(sent, me, 0, keepdims=False))
            out = adamw_slots(rows_of(weights[n]), slots, rows_of(moms_m[n]), rows_of(moms_v[n]), "adamw_" + n)
            if weights[n].shape[2] != D:
                out = [jnp.swapaxes(o, 0, 1) for o in out]
            grads[n], delta[n], new_m[n], new_v[n] = out

    finish(big[1:], grad_x, "scatter_grads_wait")

    accs = dict(zip(accs, lax.optimization_barrier((list(accs.values()), [new_v[n] for n in big[1:]]))[0]))
    rep = (("acc_1", 0, 3), ("acc_2", 0, 4), ("acc_f", 0, 3), ("acc_ln", 0, 2), ("acc_s", 0, 1), ("acc_s16", 1, 3),
           ("dw_sc", K_SSD, 1), ("dw_cc", K_CONF, 1), ("dwg", K_FFN, 1), ("dwv", K_FFN, 1))
    shapes = [(rows, accs[k].shape[1]) for k, _, rows in rep]
    conv_slabs = [_cols_to_slabs(accs["dw_sc"][:K_SSD]), _cols_to_slabs(accs["dw_cc"][:K_CONF]),
                  _cols_to_slabs(jnp.concatenate([accs["dwg"][:K_FFN], accs["dwv"][:K_FFN]], axis=1))]
    packed = pack_rows([(accs[k], first, rows) for k, first, rows in rep], "pack_small_grads")
    landed = exchange([packed] + conv_slabs, "exchange_small_grads", gather=[True, False, False, False])
    packed_red, g_scw, g_ccw, g_fcw = sum_slots_many(landed, "sum_small_grads")
    a1_all, a2_all, af_all = unpack_rows(landed[0], shapes)[:3]
    r1, r2, rf, rln, rs, r16, rscb, rccb, rfbg, rfbv = unpack_rows(packed_red, shapes)
    loss = 0.5 / D * jnp.sum(rf[2:3])

    def mod_rows(a1, a2, af):
        return jnp.concatenate([a1[..., 0:2, :], a2[..., 3:4, :], a2[..., 0:2, :], af[..., 1:2, :]], axis=-2)

    dmod_all = mod_rows(a1_all, a2_all, af_all).reshape(N_DEV, 6 * D)
    grads["ada_w"] = ada_wgrad(c_act_all, lax.dynamic_slice(dmod_all, (0, me * ada_cols), (N_DEV, ada_cols)), "ada_wgrad")
    grads.update(
        ada_b=mod_rows(r1, r2, rf).reshape(1, 6 * D), norm1_w=r1[2:3], ssd_conv_w=g_scw, ssd_conv_b=rscb,
        dt_bias=r16[0:1, :N_HEADS], a_log=r16[1:2, :N_HEADS], d_skip=r16[2:3, :N_HEADS], ssd_norm_w=rs,
        conf_conv_w=g_ccw, conf_conv_b=rccb, conf_ln_w=rln[0:1], conf_ln_b=rln[1:2], norm2_w=r2[2:3],
        ffn_conv_w=g_fcw, ffn_conv_b=jnp.concatenate([rfbg, rfbv], axis=1), final_norm_w=rf[0:1])

    rest = [n for n in names if n not in big]
    d_l, m_l, v_l = adamw_many([to2d(weights[n]) for n in rest], [grads[n] for n in rest], [to2d(moms_m[n]) for n in rest],
                               [to2d(moms_v[n]) for n in rest], "adamw_small")
    for n, dd, mm, vv in zip(rest, d_l, m_l, v_l):
        delta[n], new_m[n], new_v[n] = dd, mm, vv
    finish(big[:1], d_l[0], "scatter_w_in_wait")
    shape_of = lambda d_: {n: d_[n].reshape(weights[n].shape) for n in names}
    grads, delta, new_m, new_v = shape_of(grads), shape_of(delta), shape_of(new_m), shape_of(new_v)
    return (loss, grad_x[None], *[grads[n] for n in names], *[delta[n] for n in names], *[new_m[n] for n in names],
            *[new_v[n] for n in names])
```

```python
import functools

import jax
import jax.numpy as jnp
from jax import lax
from jax.experimental import pallas as pl
from jax.experimental.pallas import tpu as pltpu

F32 = jnp.float32
BF16 = jnp.bfloat16
HI = lax.Precision.HIGHEST

N_DEV = 8
D = 1024
D_SSD = 1024
HEAD = 64
N_HEADS = 16
N_STATE = 128
D_XBC = 1536
D_CONF = 1024
D_FF = 2816
K_SSD, K_CONF, K_FFN = 4, 31, 3
LANE = 128
TR = 256
TM = 512
Q = 256
CB = 256
TC = 1024
VMEM_LIMIT = 56 * 1024 * 1024

ADAM_LR, ADAM_B1, ADAM_B2, ADAM_EPS, ADAM_WD, ADAM_STEP = 0.001, 0.9, 0.999, 1e-08, 0.01, 10


def _cparams(sem=None):
    return pltpu.CompilerParams(vmem_limit_bytes=VMEM_LIMIT, dimension_semantics=sem)


def _sds(shape, dtype):
    return jax.ShapeDtypeStruct(shape, dtype)


def _sigmoid(x):
    return 1.0 / (1.0 + jnp.exp(-x))


def _silu(x):
    return x * _sigmoid(x)


def _dsilu(x):
    s = _sigmoid(x)
    return s * (1.0 + x * (1.0 - s))


def _softplus(x):
    return jnp.maximum(x, 0.0) + jnp.log(1.0 + jnp.exp(-jnp.abs(x)))


def _dot(a, b):
    return jnp.dot(a.astype(BF16), b.astype(BF16), preferred_element_type=F32)


def _dot_nt(a, b):
    return lax.dot_general(a.astype(BF16), b.astype(BF16), (((1,), (1,)), ((), ())), preferred_element_type=F32)


def _dot_tn(a, b):
    return lax.dot_general(a.astype(BF16), b.astype(BF16), (((0,), (0,)), ((), ())), preferred_element_type=F32)


def _bf16_terms(a, terms):
    parts, rem = [], a
    for t in range(terms):
        p = rem.astype(BF16)
        parts.append(p)
        if t + 1 < terms:
            rem = rem - p.astype(F32)
    return parts


def _dot_exact(a, b, terms, exact, dims=(((1,), (0,)), ((), ()))):
    if exact == "a":
        a_b = a.astype(BF16)
        outs = [lax.dot_general(a_b, p, dims, preferred_element_type=F32) for p in _bf16_terms(b, terms)]
    else:
        b_b = b.astype(BF16)
        outs = [lax.dot_general(p, b_b, dims, preferred_element_type=F32) for p in _bf16_terms(a, terms)]
    acc = outs[-1]
    for o in reversed(outs[:-1]):
        acc = acc + o
    return acc


def _dot_tn_hi(a, b):
    return lax.dot_general(a, b, (((0,), (0,)), ((), ())), precision=HI, preferred_element_type=F32)


def _colsum(x):
    return jnp.sum(x, axis=0, keepdims=True)


def _const_spec(shape):
    return pl.BlockSpec(shape, lambda *_: (0,) * len(shape))


def _col_tile(n):
    for t in (2816, 1408, 1024, 768, 512, 256, 128):
        if n % t == 0 and t <= n:
            return t
    return n


def mm_nt(pairs, name):
    L = pairs[0][0].shape[0]
    K = pairs[0][1].shape[0]
    tk = _col_tile(K)
    n = len(pairs)

    def body(*refs):
        o_ref = refs[-1]
        acc = None
        for p in range(n):
            t = lax.dot_general(refs[2 * p][...], refs[2 * p + 1][...], (((1,), (1,)), ((), ())),
                                preferred_element_type=F32)
            acc = t if acc is None else acc + t
        o_ref[...] = acc

    in_specs, args = [], []
    for a, w, cb in pairs:
        in_specs += [pl.BlockSpec((TM, a.shape[1]), lambda j, i: (i, 0)),
                     pl.BlockSpec((tk, a.shape[1]), functools.partial(lambda j, i, cb: (j, cb), cb=cb))]
        args += [a, w]
    return pl.pallas_call(
        body, name=name, grid=(K // tk, L // TM), out_shape=_sds((L, K), F32), in_specs=in_specs,
        out_specs=pl.BlockSpec((TM, tk), lambda j, i: (i, j)),
        compiler_params=_cparams(("parallel", "parallel")))(*args)


def mm_tn(a, g, name):
    L, M = a.shape
    N = g.shape[1]
    tn = _col_tile(N) if N > 1024 else N
    if M * tn * 4 > 12 * 1024 * 1024:
        tn = 512
    tl = 512 if L % 512 == 0 else TR
    nl = L // tl

    def body(a_ref, g_ref, o_ref, acc_ref):
        @pl.when(pl.program_id(1) == 0)
        def _():
            acc_ref[...] = jnp.zeros((M, tn), F32)

        acc_ref[...] += lax.dot_general(a_ref[...], g_ref[...], (((0,), (0,)), ((), ())), preferred_element_type=F32)

        @pl.when(pl.program_id(1) == nl - 1)
        def _():
            o_ref[...] = acc_ref[...].astype(BF16)

    return pl.pallas_call(
        body, name=name, grid=(N // tn, nl), out_shape=_sds((M, N), BF16),
        in_specs=[pl.BlockSpec((tl, M), lambda j, l: (l, 0)), pl.BlockSpec((tl, tn), lambda j, l: (l, j))],
        out_specs=pl.BlockSpec((M, tn), lambda j, l: (0, j)), scratch_shapes=[pltpu.VMEM((M, tn), F32)],
        compiler_params=_cparams(("parallel", "arbitrary")))(a, g)


def mm_tn_stack(a_list, g, name):
    L, M = a_list[0].shape
    N = g.shape[1]
    n = len(a_list)
    tl = 512 if L % 512 == 0 else TR
    nl = L // tl

    def body(*refs):
        a_refs, g_ref, o_ref, acc_ref = refs[:n], refs[n], refs[n + 1], refs[n + 2]
        j, l = pl.program_id(0), pl.program_id(1)

        @pl.when(l == 0)
        def _():
            acc_ref[...] = jnp.zeros((M, N), F32)

        for p in range(n):
            @pl.when(j == p)
            def _(p=p):
                acc_ref[...] += lax.dot_general(a_refs[p][...], g_ref[...], (((0,), (0,)), ((), ())), preferred_element_type=F32)

        @pl.when(l == nl - 1)
        def _():
            o_ref[...] = acc_ref[...].astype(BF16)

    a_specs = [pl.BlockSpec((tl, M), functools.partial(lambda j, l, p: (jnp.where(j == p, l, 0), 0), p=p)) for p in range(n)]
    return pl.pallas_call(
        body, name=name, grid=(n, nl), out_shape=_sds((n * M, N), BF16),
        in_specs=a_specs + [pl.BlockSpec((tl, N), lambda j, l: (l, 0))],
        out_specs=pl.BlockSpec((M, N), lambda j, l: (j, 0)), scratch_shapes=[pltpu.VMEM((M, N), F32)],
        compiler_params=_cparams(("arbitrary", "arbitrary")))(*a_list, g)


def mm_tn_concat(pieces, g, name):
    L = g.shape[0]
    N = g.shape[1]
    n = len(pieces)
    offsets = [sum(r for _, r in pieces[:p]) for p in range(n + 1)]
    slab = offsets[-1] // N_DEV
    tl = 512 if L % 512 == 0 else TR
    nl = L // tl

    def body(*refs):
        a_refs, g_ref, o_ref, acc_ref = refs[:n], refs[n], refs[n + 1], refs[n + 2]
        l = pl.program_id(0)

        @pl.when(l == 0)
        def _():
            acc_ref[...] = jnp.zeros((offsets[-1], N), F32)

        g_v = g_ref[...]
        for p in range(n):
            t = lax.dot_general(a_refs[p][...], g_v, (((0,), (0,)), ((), ())), preferred_element_type=F32)
            acc_ref[offsets[p]:offsets[p + 1], :] += t[:pieces[p][1], :]

        @pl.when(l == nl - 1)
        def _():
            for s in range(N_DEV):
                o_ref[s] = acc_ref[s * slab:(s + 1) * slab, :].astype(BF16)

    return pl.pallas_call(
        body, name=name, grid=(nl,), out_shape=_sds((N_DEV, slab, N), BF16),
        in_specs=[pl.BlockSpec((tl, a.shape[1]), lambda l: (l, 0)) for a, _ in pieces] + [pl.BlockSpec((tl, N), lambda l: (l, 0))],
        out_specs=_const_spec((N_DEV, slab, N)), scratch_shapes=[pltpu.VMEM((offsets[-1], N), F32)],
        compiler_params=_cparams(("arbitrary",)))(*[a for a, _ in pieces], g)


def _row_spec(width=D):
    return pl.BlockSpec((TR, width), lambda i: (i, 0))


def in_proj(x, mod, n1w, w_full, w_dt_rep, w_cf, name):
    L = x.shape[0]
    n_zx = D_SSD + D_XBC

    def body(x_ref, mod_ref, w_ref, wzx_ref, wdt_ref, wcf_ref, h_ref, zx_ref, dt_ref, cf_ref):
        xin = x_ref[...]
        r = lax.rsqrt(jnp.mean(xin * xin, axis=-1, keepdims=True) + 1e-6)
        h = ((xin * r * w_ref[...]) * (1.0 + mod_ref[1:2, :]) + mod_ref[0:1, :]).astype(BF16)
        h_ref[...] = h
        nt = (((1,), (1,)), ((), ()))
        zx_ref[...] = lax.dot_general(h, wzx_ref[...], nt, preferred_element_type=F32)
        dt_ref[...] = lax.dot_general(h, wdt_ref[...], nt, preferred_element_type=F32)
        cf_ref[...] = lax.dot_general(h, wcf_ref[...], nt, preferred_element_type=F32)

    row = lambda w: pl.BlockSpec((TM, w), lambda i: (i, 0))
    return pl.pallas_call(
        body, name=name, grid=(L // TM,),
        out_shape=[_sds((L, D), BF16), _sds((L, n_zx), F32), _sds((L, D_SSD), F32), _sds((L, 2 * D_CONF), F32)],
        in_specs=[row(D), _const_spec((8, D)), _const_spec((1, D)), _const_spec((n_zx, D)), _const_spec((D_SSD, D)),
                  _const_spec((2 * D_CONF, D))],
        out_specs=[row(D), row(n_zx), row(D_SSD), row(2 * D_CONF)],
        compiler_params=_cparams(("parallel",)))(x, mod, n1w, w_full, w_dt_rep, w_cf)


def mixer_out(ysn, uc, lnw, lnb, w_out, x, mod, n2w, name):
    L = x.shape[0]

    def body(ysn_ref, uc_ref, lnw_ref, lnb_ref, wo_ref, x_ref, mod_ref, n2w_ref, mix_ref, u_ref, x1_ref, h2_ref):
        uc_v = uc_ref[...]
        mu = jnp.mean(uc_v, axis=-1, keepdims=True)
        var = jnp.mean(jnp.square(uc_v - mu), axis=-1, keepdims=True)
        u = _silu((uc_v - mu) * lax.rsqrt(var + 1e-5) * lnw_ref[...] + lnb_ref[...]).astype(BF16)
        u_ref[...] = u
        mix = (jnp.dot(ysn_ref[...], wo_ref[0:D_SSD, :], preferred_element_type=F32)
               + jnp.dot(u, wo_ref[D_SSD:D_SSD + D_CONF, :], preferred_element_type=F32))
        mix_ref[...] = mix
        x1 = x_ref[...] + mod_ref[2:3, :] * mix
        x1_ref[...] = x1
        r = lax.rsqrt(jnp.mean(x1 * x1, axis=-1, keepdims=True) + 1e-6)
        h2_ref[...] = ((x1 * r * n2w_ref[...]) * (1.0 + mod_ref[4:5, :]) + mod_ref[3:4, :]).astype(BF16)

    row = pl.BlockSpec((TM, D), lambda i: (i, 0))
    return pl.pallas_call(
        body, name=name, grid=(L // TM,),
        out_shape=[_sds((L, D), F32), _sds((L, D_CONF), BF16), _sds((L, D), F32), _sds((L, D), BF16)],
        in_specs=[row, row, _const_spec((1, D)), _const_spec((1, D)), _const_spec((D_SSD + D_CONF, D)), row,
                  _const_spec((8, D)), _const_spec((1, D))],
        out_specs=[row] * 4, compiler_params=_cparams(("parallel",)))(ysn, uc, lnw, lnb, w_out, x, mod, n2w)


def mixer_out_bwd(dmix, w_out, uc, lnw, lnb, name):
    L = uc.shape[0]

    def body(dm_ref, wo_ref, u_ref, w_ref, b_ref, dy_ref, o_ref, acc_ref):
        @pl.when(pl.program_id(0) == 0)
        def _():
            acc_ref[...] = jnp.zeros((8, D), F32)

        nt = (((1,), (1,)), ((), ()))
        dm = dm_ref[...]
        dy_ref[...] = lax.dot_general(dm, wo_ref[0:D_SSD, :], nt, preferred_element_type=F32)
        du = lax.dot_general(dm, wo_ref[D_SSD:D_SSD + D_CONF, :], nt, preferred_element_type=F32)
        u = u_ref[...]
        mu = jnp.mean(u, axis=-1, keepdims=True)
        rl = lax.rsqrt(jnp.mean(jnp.square(u - mu), axis=-1, keepdims=True) + 1e-5)
        n = (u - mu) * rl
        v = n * w_ref[...] + b_ref[...]
        dv = du * _dsilu(v)
        acc_ref[0:1, :] += _colsum(dv * n)
        acc_ref[1:2, :] += _colsum(dv)
        dn = dv * w_ref[...]
        o_ref[...] = rl * (dn - jnp.mean(dn, axis=-1, keepdims=True) - n * jnp.mean(dn * n, axis=-1, keepdims=True))

    row = pl.BlockSpec((TM, D), lambda i: (i, 0))
    return pl.pallas_call(body, name=name, grid=(L // TM,), out_shape=[_sds((L, D), F32), _sds((L, D), F32), _sds((8, D), F32)],
                          in_specs=[row, _const_spec((D_SSD + D_CONF, D)), row, _const_spec((1, D)), _const_spec((1, D))],
                          out_specs=[row, row, _const_spec((8, D))],
                          compiler_params=_cparams(("arbitrary",)))(dmix, w_out, uc, lnw, lnb)


def final_loss(act, w_down, x1, mod, fw, target, name):
    L = act.shape[0]

    def body(act_ref, wd_ref, x1_ref, mod_ref, fw_ref, t_ref, dx_ref, dff_ref, dact_ref, acc_ref):
        @pl.when(pl.program_id(0) == 0)
        def _():
            acc_ref[...] = jnp.zeros((8, D), F32)

        ff_v = jnp.dot(act_ref[...], wd_ref[...], preferred_element_type=F32)
        g2 = mod_ref[5:6, :]
        x2 = x1_ref[...] + g2 * ff_v
        r = lax.rsqrt(jnp.mean(x2 * x2, axis=-1, keepdims=True) + 1e-6)
        n = x2 * r
        err = n * fw_ref[...] - t_ref[...]
        dy = err * (1.0 / D)
        dn = dy * fw_ref[...]
        dx2 = r * (dn - n * jnp.mean(dn * n, axis=-1, keepdims=True))
        acc_ref[0:1, :] += _colsum(dy * n)
        acc_ref[1:2, :] += _colsum(dx2 * ff_v)
        acc_ref[2:3, :] += _colsum(err * err)
        dx_ref[...] = dx2
        dff = (dx2 * g2).astype(BF16)
        dff_ref[...] = dff
        dact_ref[...] = lax.dot_general(dff, wd_ref[...], (((1,), (1,)), ((), ())), preferred_element_type=F32)

    return pl.pallas_call(
        body, name=name, grid=(L // TR,),
        out_shape=[_sds((L, D), F32), _sds((L, D), BF16), _sds((L, D_FF), F32), _sds((8, D), F32)],
        in_specs=[_row_spec(D_FF), _const_spec((D_FF, D)), _row_spec(), _const_spec((8, D)), _const_spec((1, D)), _row_spec()],
        out_specs=[_row_spec(), _row_spec(), _row_spec(D_FF), _const_spec((8, D))],
        compiler_params=_cparams(("arbitrary",)))(act, w_down, x1, mod, fw, target)


def norm_mod_bwd(dh_pairs, xin, dres, mod, w, shift_row, name, mix=None, gate_row=None):
    L = xin.shape[0]
    has_mix = mix is not None
    n_pairs = len(dh_pairs)

    def body(*refs):
        pair_refs, refs = refs[:2 * n_pairs], refs[2 * n_pairs:]
        if has_mix:
            x_ref, dres_ref, mod_ref, w_ref, mix_ref, dx_ref, dmix_ref, acc_ref = refs
        else:
            x_ref, dres_ref, mod_ref, w_ref, dx_ref, acc_ref = refs

        @pl.when(pl.program_id(0) == 0)
        def _():
            acc_ref[...] = jnp.zeros((8, D), F32)

        dh_v = None
        for p in range(n_pairs):
            t = jnp.dot(pair_refs[2 * p][...], pair_refs[2 * p + 1][...], preferred_element_type=F32)
            dh_v = t if dh_v is None else dh_v + t
        x = x_ref[...]
        r = lax.rsqrt(jnp.mean(x * x, axis=-1, keepdims=True) + 1e-6)
        n = x * r
        nw = n * w_ref[...]
        sc1 = 1.0 + mod_ref[shift_row + 1:shift_row + 2, :]
        acc_ref[0:1, :] += _colsum(dh_v)
        acc_ref[1:2, :] += _colsum(dh_v * nw)
        dnw = dh_v * sc1
        acc_ref[2:3, :] += _colsum(dnw * n)
        dn = dnw * w_ref[...]
        dx = r * (dn - n * jnp.mean(dn * n, axis=-1, keepdims=True)) + dres_ref[...]
        dx_ref[...] = dx
        if has_mix:
            acc_ref[3:4, :] += _colsum(dx * mix_ref[...])
            dmix_ref[...] = (dx * mod_ref[gate_row:gate_row + 1, :]).astype(BF16)

    ins, in_specs = [], []
    for a, wt, rb in dh_pairs:
        ins += [a, wt]
        in_specs += [_row_spec(a.shape[1]), pl.BlockSpec((a.shape[1], D), functools.partial(lambda i, rb: (rb, 0), rb=rb))]
    ins += [xin, dres, mod, w] + ([mix] if has_mix else [])
    in_specs += [_row_spec(), _row_spec(), _const_spec((8, D)), _const_spec((1, D))] + ([_row_spec()] if has_mix else [])
    out_shape = [_sds((L, D), F32)] + ([_sds((L, D), BF16)] if has_mix else []) + [_sds((8, D), F32)]
    out_specs = [_row_spec()] + ([_row_spec()] if has_mix else []) + [_const_spec((8, D))]
    return pl.pallas_call(body, name=name, grid=(L // TR,), out_shape=out_shape, in_specs=in_specs,
                          out_specs=out_specs, compiler_params=_cparams(("arbitrary",)))(*ins)


def _halo(k):
    return 8 if k <= 9 else 32


def _prev_spec(h, col0):
    return pl.BlockSpec((h, CB), lambda j, i: (jnp.maximum(i * (TC // h) - 1, 0), j + col0))


def _next_spec(h, col0, n_tiles):
    return pl.BlockSpec((h, CB), lambda j, i: (jnp.minimum(i + 1, n_tiles - 1) * (TC // h), j + col0))


def _tile_spec(col0):
    return pl.BlockSpec((TC, CB), lambda j, i: (i, j + col0))


def _w_spec(kp, col0):
    return pl.BlockSpec((kp, CB), lambda j, i: (0, j + col0))


SUBLANES = 8


def _shifted_windows(v, taps, rows):
    for r in range(SUBLANES):
        group = [(o, k) for o, k in taps if o % SUBLANES == r]
        if not group:
            continue
        s = v if r == 0 else pltpu.roll(v, v.shape[0] - r, 0)
        for o, k in group:
            yield k, s[o - r:o - r + rows, :]


def _causal_taps(ext_ref, w_ref, k_taps, first, rows):
    acc = None
    for k, win in _shifted_windows(ext_ref[...], [(first - (k_taps - 1) + k, k) for k in range(k_taps)], rows):
        t = w_ref[k:k + 1, :] * win
        acc = t if acc is None else acc + t
    return acc


def _anticausal_taps(d_ref, w_ref, k_taps, rows):
    acc = None
    for k, win in _shifted_windows(d_ref[...], [(k_taps - 1 - k, k) for k in range(k_taps)], rows):
        t = w_ref[k:k + 1, :] * win
        acc = t if acc is None else acc + t
    return acc


def _acc_conv_wgrad(dw_ref, d_tile, ext_ref, k_taps, first):
    for k, win in _shifted_windows(ext_ref[...], [(first - (k_taps - 1) + k, k) for k in range(k_taps)], TC):
        dw_ref[k:k + 1, :] += _colsum(d_tile * win)
    dw_ref[k_taps:k_taps + 1, :] += _colsum(d_tile)


def conv_silu_fwd(x, col0, width, w, b, name):
    L = x.shape[0]
    k_taps = w.shape[0]
    h = _halo(k_taps)

    def body(xp_ref, x_ref, w_ref, b_ref, o_ref, ext_ref):
        i = pl.program_id(1)
        ext_ref[0:h, :] = jnp.where(i > 0, xp_ref[...], 0.0)
        ext_ref[h:h + TC, :] = x_ref[...]
        o_ref[...] = _silu(_causal_taps(ext_ref, w_ref, k_taps, h, TC) + b_ref[...])

    return pl.pallas_call(
        body, name=name, grid=(width // CB, L // TC), out_shape=_sds((L, width), F32),
        in_specs=[_prev_spec(h, col0), _tile_spec(col0), _w_spec(k_taps, 0), pl.BlockSpec((1, CB), lambda j, i: (0, j))],
        out_specs=_tile_spec(0), scratch_shapes=[pltpu.VMEM((h + TC, CB), F32)],
        compiler_params=_cparams(("parallel", "parallel")))(x, x, w, b)


def conv_silu_bwd(x, col0, width, w, b, dpost, name):
    L = x.shape[0]
    k_taps = w.shape[0]
    h = _halo(k_taps)
    nt = L // TC

    def body(xp_ref, x_ref, xn_ref, d_ref, dn_ref, w_ref, b_ref, dx_ref, dw_ref, ext_ref, dpre_ref):
        i = pl.program_id(1)

        @pl.when(i == 0)
        def _():
            dw_ref[...] = jnp.zeros((8, CB), F32)

        ext_ref[0:h, :] = jnp.where(i > 0, xp_ref[...], 0.0)
        ext_ref[h:h + TC, :] = x_ref[...]
        ext_ref[h + TC:h + TC + h, :] = xn_ref[...]
        pre = _causal_taps(ext_ref, w_ref, k_taps, h, TC + h) + b_ref[...]
        dpre_ref[0:TC, :] = d_ref[...] * _dsilu(pre[0:TC, :])
        dpre_ref[TC:TC + h, :] = jnp.where(i < nt - 1, dn_ref[...], 0.0) * _dsilu(pre[TC:TC + h, :])
        dx_ref[...] = _anticausal_taps(dpre_ref, w_ref, k_taps, TC).astype(BF16)
        _acc_conv_wgrad(dw_ref, dpre_ref[0:TC, :], ext_ref, k_taps, h)

    return pl.pallas_call(
        body, name=name, grid=(width // CB, nt),
        out_shape=[_sds((L, width), BF16), _sds((8, width), F32)],
        in_specs=[_prev_spec(h, col0), _tile_spec(col0), _next_spec(h, col0, nt), _tile_spec(0), _next_spec(h, 0, nt),
                  _w_spec(k_taps, 0), pl.BlockSpec((1, CB), lambda j, i: (0, j))],
        out_specs=[_tile_spec(0), _w_spec(8, 0)],
        scratch_shapes=[pltpu.VMEM((h + TC + h, CB), F32), pltpu.VMEM((TC + h, CB), F32)],
        compiler_params=_cparams(("parallel", "arbitrary")))(x, x, x, dpost, dpost, w, b)


def conf_conv_fwd(proj, col_a, col_g, w, b, name):
    L = proj.shape[0]
    k_taps = w.shape[0]
    h = _halo(k_taps)

    def body(ap_ref, a_ref, gp_ref, g_ref, w_ref, b_ref, o_ref, ext_ref):
        i = pl.program_id(1)
        ext_ref[0:h, :] = jnp.where(i > 0, ap_ref[...] * _sigmoid(gp_ref[...]), 0.0)
        ext_ref[h:h + TC, :] = a_ref[...] * _sigmoid(g_ref[...])
        o_ref[...] = _causal_taps(ext_ref, w_ref, k_taps, h, TC) + b_ref[...]

    return pl.pallas_call(
        body, name=name, grid=(D_CONF // CB, L // TC), out_shape=_sds((L, D_CONF), F32),
        in_specs=[_prev_spec(h, col_a), _tile_spec(col_a), _prev_spec(h, col_g), _tile_spec(col_g), _w_spec(k_taps, 0),
                  pl.BlockSpec((1, CB), lambda j, i: (0, j))],
        out_specs=_tile_spec(0), scratch_shapes=[pltpu.VMEM((h + TC, CB), F32)],
        compiler_params=_cparams(("parallel", "parallel")))(proj, proj, proj, proj, w, b)


def conf_conv_bwd(proj, col_a, col_g, w, duc, name):
    L = proj.shape[0]
    k_taps = w.shape[0]
    h = _halo(k_taps)
    nt = L // TC

    def body(ap_ref, a_ref, gp_ref, g_ref, d_ref, dn_ref, w_ref, da_ref, dg_ref, dw_ref, ext_ref, dext_ref):
        i = pl.program_id(1)

        @pl.when(i == 0)
        def _():
            dw_ref[...] = jnp.zeros((32, CB), F32)

        a = a_ref[...]
        s = _sigmoid(g_ref[...])
        ext_ref[0:h, :] = jnp.where(i > 0, ap_ref[...] * _sigmoid(gp_ref[...]), 0.0)
        ext_ref[h:h + TC, :] = a * s
        dext_ref[0:TC, :] = d_ref[...]
        dext_ref[TC:TC + h, :] = jnp.where(i < nt - 1, dn_ref[...], 0.0)
        du0 = _anticausal_taps(dext_ref, w_ref, k_taps, TC)
        da_ref[...] = (du0 * s).astype(BF16)
        dg_ref[...] = (du0 * a * s * (1.0 - s)).astype(BF16)
        _acc_conv_wgrad(dw_ref, d_ref[...], ext_ref, k_taps, h)

    return pl.pallas_call(
        body, name=name, grid=(D_CONF // CB, nt),
        out_shape=[_sds((L, D_CONF), BF16), _sds((L, D_CONF), BF16), _sds((32, D_CONF), F32)],
        in_specs=[_prev_spec(h, col_a), _tile_spec(col_a), _prev_spec(h, col_g), _tile_spec(col_g), _tile_spec(0),
                  _next_spec(h, 0, nt), _w_spec(k_taps, 0)],
        out_specs=[_tile_spec(0), _tile_spec(0), _w_spec(32, 0)],
        scratch_shapes=[pltpu.VMEM((h + TC, CB), F32), pltpu.VMEM((TC + h, CB), F32)],
        compiler_params=_cparams(("parallel", "arbitrary")))(proj, proj, proj, proj, duc, duc, w)


def ffn_conv_fwd(up, w, b, name):
    L = up.shape[0]
    k_taps = w.shape[0]
    h = _halo(k_taps)
    cv = D_FF // CB

    def body(gp_ref, g_ref, vp_ref, v_ref, wg_ref, wv_ref, bg_ref, bv_ref, o_ref, eg_ref, ev_ref):
        i = pl.program_id(1)
        eg_ref[0:h, :] = jnp.where(i > 0, gp_ref[...], 0.0)
        eg_ref[h:h + TC, :] = g_ref[...]
        ev_ref[0:h, :] = jnp.where(i > 0, vp_ref[...], 0.0)
        ev_ref[h:h + TC, :] = v_ref[...]
        pg = _causal_taps(eg_ref, wg_ref, k_taps, h, TC) + bg_ref[...]
        pv = _causal_taps(ev_ref, wv_ref, k_taps, h, TC) + bv_ref[...]
        o_ref[...] = (_silu(pg) * pv).astype(BF16)

    bspec = lambda c0: pl.BlockSpec((1, CB), lambda j, i: (0, j + c0))
    return pl.pallas_call(
        body, name=name, grid=(cv, L // TC), out_shape=_sds((L, D_FF), BF16),
        in_specs=[_prev_spec(h, 0), _tile_spec(0), _prev_spec(h, cv), _tile_spec(cv), _w_spec(k_taps, 0), _w_spec(k_taps, cv),
                  bspec(0), bspec(cv)],
        out_specs=_tile_spec(0), scratch_shapes=[pltpu.VMEM((h + TC, CB), F32), pltpu.VMEM((h + TC, CB), F32)],
        compiler_params=_cparams(("parallel", "parallel")))(up, up, up, up, w, w, b, b)


def ffn_conv_bwd(up, w, b, dact, name):
    L = up.shape[0]
    k_taps = w.shape[0]
    h = _halo(k_taps)
    nt = L // TC
    cv = D_FF // CB

    def body(gp_ref, g_ref, gn_ref, vp_ref, v_ref, vn_ref, d_ref, dn_ref, wg_ref, wv_ref, bg_ref, bv_ref,
             dg_ref, dv_ref, dwg_ref, dwv_ref, eg_ref, ev_ref, pg_ref, pv_ref):
        i = pl.program_id(1)

        @pl.when(i == 0)
        def _():
            dwg_ref[...] = jnp.zeros((8, CB), F32)
            dwv_ref[...] = jnp.zeros((8, CB), F32)

        for e_ref, p_ref, c_ref, n_ref in ((eg_ref, gp_ref, g_ref, gn_ref), (ev_ref, vp_ref, v_ref, vn_ref)):
            e_ref[0:h, :] = jnp.where(i > 0, p_ref[...], 0.0)
            e_ref[h:h + TC, :] = c_ref[...]
            e_ref[h + TC:h + TC + h, :] = n_ref[...]
        pg = _causal_taps(eg_ref, wg_ref, k_taps, h, TC + h) + bg_ref[...]
        pv = _causal_taps(ev_ref, wv_ref, k_taps, h, TC + h) + bv_ref[...]
        dact_t = d_ref[...]
        dact_n = jnp.where(i < nt - 1, dn_ref[...], 0.0)
        pg_ref[0:TC, :] = dact_t * pv[0:TC, :] * _dsilu(pg[0:TC, :])
        pg_ref[TC:TC + h, :] = dact_n * pv[TC:TC + h, :] * _dsilu(pg[TC:TC + h, :])
        pv_ref[0:TC, :] = dact_t * _silu(pg[0:TC, :])
        pv_ref[TC:TC + h, :] = dact_n * _silu(pg[TC:TC + h, :])
        dg_ref[...] = _anticausal_taps(pg_ref, wg_ref, k_taps, TC).astype(BF16)
        dv_ref[...] = _anticausal_taps(pv_ref, wv_ref, k_taps, TC).astype(BF16)
        _acc_conv_wgrad(dwg_ref, pg_ref[0:TC, :], eg_ref, k_taps, h)
        _acc_conv_wgrad(dwv_ref, pv_ref[0:TC, :], ev_ref, k_taps, h)

    bspec = lambda c0: pl.BlockSpec((1, CB), lambda j, i: (0, j + c0))
    ext = pltpu.VMEM((h + TC + h, CB), F32)
    dpre = pltpu.VMEM((TC + h, CB), F32)
    return pl.pallas_call(
        body, name=name, grid=(cv, nt),
        out_shape=[_sds((L, D_FF), BF16), _sds((L, D_FF), BF16), _sds((8, D_FF), F32), _sds((8, D_FF), F32)],
        in_specs=[_prev_spec(h, 0), _tile_spec(0), _next_spec(h, 0, nt), _prev_spec(h, cv), _tile_spec(cv), _next_spec(h, cv, nt),
                  _tile_spec(0), _next_spec(h, 0, nt), _w_spec(k_taps, 0), _w_spec(k_taps, cv), bspec(0), bspec(cv)],
        out_specs=[_tile_spec(0), _tile_spec(0), _w_spec(8, 0), _w_spec(8, 0)],
        scratch_shapes=[ext, ext, dpre, dpre],
        compiler_params=_cparams(("parallel", "arbitrary")))(up, up, up, up, up, up, dact, dact, w, w, b, b)


def _ssd_common(xbc_ref, dt_ref, dtb_ref, alog_ref, cs_ref):
    xs = xbc_ref[:, 0:D_SSD]
    sp_in = dt_ref[...] + dtb_ref[...]
    dtf = _softplus(sp_in)
    a_f = -jnp.exp(alog_ref[...])
    a_dt = dtf * a_f
    row = lax.broadcasted_iota(jnp.int32, (Q, Q), 0)
    col = lax.broadcasted_iota(jnp.int32, (Q, Q), 1)
    causal = row >= col
    cs = _dot_exact(causal.astype(F32), a_dt, 3, "a")
    cs_ref[...] = cs
    cs_last = cs_ref[Q - 1:Q, :]
    return xs, sp_in, dtf, a_f, cs, cs_last, causal


def _head_decay(cs_j, cst_ref, e, causal):
    lane = lax.broadcasted_iota(jnp.int32, (Q, LANE), 1)
    rolled = pltpu.roll(cs_j, HEAD, 1)
    own = (lane < HEAD) if e == 0 else (lane >= HEAD)
    col_b = jnp.where(own, cs_j, rolled)
    col_b = jnp.concatenate([col_b] * (Q // LANE), axis=1)
    row_b = cst_ref[e * HEAD:e * HEAD + 1, :]
    return jnp.where(causal, jnp.exp(jnp.minimum(col_b - row_b, 0.0)), 0.0)


def ssd_fwd(xbc, z_src, dt_src, dtb_f, alog_f, dsk_f, snw, name):
    L = xbc.shape[0]
    nc = L // Q

    def body(xbc_ref, z_ref, dt_ref, dtb_ref, alog_ref, dsk_ref, snw_ref, y_ref, yn_ref, sp_ref, s_ref, cs_ref, cst_ref, yd_ref):
        @pl.when(pl.program_id(0) == 0)
        def _():
            s_ref[...] = jnp.zeros((N_STATE, D_SSD), F32)

        xs, _, dtf, a_f, cs, cs_last, causal = _ssd_common(xbc_ref, dt_ref, dtb_ref, alog_ref, cs_ref)
        e_cs = jnp.exp(cs)
        xdt = xs * dtf
        zst = jnp.exp(cs_last - cs) * xdt
        sp_ref[0] = s_ref[...]
        lane = lax.broadcasted_iota(jnp.int32, (Q, LANE), 1)
        for g in range(2):
            gl = slice(g * 512, g * 512 + 512)
            b_g = xbc_ref[:, D_SSD + g * N_STATE:D_SSD + (g + 1) * N_STATE]
            c_g = xbc_ref[:, D_SSD + 2 * N_STATE + g * N_STATE:D_SSD + 2 * N_STATE + (g + 1) * N_STATE]
            s_prev = s_ref[:, gl]
            cb = _dot_nt(c_g, b_g)
            yd_ref[:, gl] = e_cs[:, gl] * _dot(c_g, s_prev)
            for j in range(4):
                tl = slice(g * 512 + j * LANE, g * 512 + (j + 1) * LANE)
                cs_j = cs[:, tl]
                cst_ref[...] = cs_j.T
                x_j = xdt[:, tl]
                o0 = _dot(cb * _head_decay(cs_j, cst_ref, 0, causal), x_j)
                o1 = _dot(cb * _head_decay(cs_j, cst_ref, 1, causal), x_j)
                yd_ref[:, tl] += jnp.where(lane < HEAD, o0, o1)
            s_ref[:, gl] = jnp.exp(cs_last[:, gl]) * s_prev + _dot_tn(b_g, zst[:, gl])
        y = yd_ref[...] + xs * dsk_ref[...]
        y_ref[...] = y
        yz = y * _silu(z_ref[...])
        r = lax.rsqrt(jnp.mean(yz * yz, axis=-1, keepdims=True) + 1e-6)
        yn_ref[...] = (yz * r * snw_ref[...]).astype(BF16)

    chunk = lambda w, c: pl.BlockSpec((Q, w), lambda i: (i, c))
    return pl.pallas_call(
        body, name=name, grid=(nc,),
        out_shape=[_sds((L, D_SSD), F32), _sds((L, D_SSD), BF16), _sds((nc, N_STATE, D_SSD), F32)],
        in_specs=[chunk(D_XBC, 0), chunk(D, 0), chunk(D, 0)] + [_const_spec((1, D))] * 4,
        out_specs=[chunk(D, 0), chunk(D, 0), pl.BlockSpec((1, N_STATE, D_SSD), lambda i: (i, 0, 0))],
        scratch_shapes=[pltpu.VMEM((N_STATE, D_SSD), F32), pltpu.VMEM((Q, D_SSD), F32), pltpu.VMEM((LANE, Q), F32),
                        pltpu.VMEM((Q, D_SSD), F32)],
        compiler_params=_cparams(("arbitrary",)))(xbc, z_src, dt_src, dtb_f, alog_f, dsk_f, snw)


def ssd_bwd(dysn, y, xbc, z_src, dt_src, s_prev_all, dtb_f, alog_f, dsk_f, snw, name):
    L = xbc.shape[0]
    nc = L // Q

    def body(dyn_ref, y_ref, xbc_ref, z_ref, dt_ref, sp_ref, dtb_ref, alog_ref, dsk_ref, snw_ref,
             dz_ref, ddt_ref, dxbc_ref, acc_ref, acc16_ref, ds_ref, cs_ref, cst_ref, dcs_ref, dx_ref):
        step = pl.program_id(0)

        @pl.when(step == 0)
        def _():
            ds_ref[...] = jnp.zeros((N_STATE, D_SSD), F32)
            acc_ref[...] = jnp.zeros((8, D), F32)

        z = z_ref[...]
        y = y_ref[...]
        sz = _sigmoid(z)
        siluz = z * sz
        yz = y * siluz
        r = lax.rsqrt(jnp.mean(yz * yz, axis=-1, keepdims=True) + 1e-6)
        n = yz * r
        dyn = dyn_ref[...]
        acc_ref[0:1, :] += _colsum(dyn * n)
        dn = dyn * snw_ref[...]
        dyz = r * (dn - n * jnp.mean(dn * n, axis=-1, keepdims=True))
        dy = dyz * siluz
        dz_ref[...] = (dyz * y * (sz * (1.0 + z * (1.0 - sz)))).astype(BF16)

        xs, sp_in, dtf, a_f, cs, cs_last, causal = _ssd_common(xbc_ref, dt_ref, dtb_ref, alog_ref, cs_ref)
        acc_ref[3:4, :] += _colsum(dy * xs)
        e_cs = jnp.exp(cs)
        xdt = xs * dtf
        dst = jnp.exp(cs_last - cs)
        zst = dst * xdt
        e_last = jnp.exp(cs_last)
        lane = lax.broadcasted_iota(jnp.int32, (Q, LANE), 1)
        ones = jnp.ones((Q, LANE), F32)
        dcs_last_parts = []
        for g in range(2):
            gl = slice(g * 512, g * 512 + 512)
            b_g = xbc_ref[:, D_SSD + g * N_STATE:D_SSD + (g + 1) * N_STATE]
            c_g = xbc_ref[:, D_SSD + 2 * N_STATE + g * N_STATE:D_SSD + 2 * N_STATE + (g + 1) * N_STATE]
            s_prev = sp_ref[0, :, gl]
            ds_g = ds_ref[:, gl]
            dy_g = dy[:, gl]
            cb = _dot_nt(c_g, b_g)
            y_off = e_cs[:, gl] * _dot(c_g, s_prev)
            edy = e_cs[:, gl] * dy_g
            d_c = _dot_nt(edy, s_prev)
            d_z = _dot(b_g, ds_g)
            d_b = _dot_nt(zst[:, gl], ds_g)
            t_g = d_z * zst[:, gl]
            dcs_ref[:, gl] = dy_g * y_off - t_g
            dx_ref[:, gl] = d_z * dst[:, gl]
            dcs_last_parts.append(_colsum(t_g) + _colsum(ds_g * s_prev) * e_last[:, gl])
            ds_ref[:, gl] = e_last[:, gl] * ds_g + _dot_tn(c_g, edy)
            dcb = jnp.zeros((Q, Q), F32)
            for j in range(4):
                tl = slice(g * 512 + j * LANE, g * 512 + (j + 1) * LANE)
                cs_j = cs[:, tl]
                cst_ref[...] = cs_j.T
                x_j = xdt[:, tl]
                dy_j = dy[:, tl]
                dx_j = jnp.zeros((Q, LANE), F32)
                dcs_j = jnp.zeros((Q, LANE), F32)
                for e in range(2):
                    own = (lane < HEAD) if e == 0 else (lane >= HEAD)
                    w_h = _head_decay(cs_j, cst_ref, e, causal)
                    g_h = cb * w_h
                    dy_m = jnp.where(own, dy_j, 0.0)
                    d_g = _dot_nt(dy_m, x_j)
                    dx_j = dx_j + _dot_tn(g_h, dy_m)
                    dcb = dcb + d_g * w_h
                    p_h = d_g * g_h
                    row_sums = _dot_exact(p_h, ones, 2, "b")
                    col_sums = _dot_exact(p_h, ones, 2, "b", (((0,), (0,)), ((), ())))
                    dcs_j = dcs_j + jnp.where(own, row_sums - col_sums, 0.0)
                dcs_ref[:, tl] += dcs_j * (1.0 / HEAD)
                dx_ref[:, tl] += dx_j
            d_c = d_c + _dot(dcb, b_g)
            d_b = d_b + _dot_tn(dcb, c_g)
            dxbc_ref[:, D_SSD + g * N_STATE:D_SSD + (g + 1) * N_STATE] = d_b
            dxbc_ref[:, D_SSD + 2 * N_STATE + g * N_STATE:D_SSD + 2 * N_STATE + (g + 1) * N_STATE] = d_c
        dcs_last = jnp.concatenate(dcs_last_parts, axis=1)
        anticausal = lax.broadcasted_iota(jnp.int32, (Q, Q), 0) <= lax.broadcasted_iota(jnp.int32, (Q, Q), 1)
        d_adt = _dot_exact(anticausal.astype(F32), dcs_ref[...], 3, "a") + dcs_last
        dx = dx_ref[...]
        acc_ref[2:3, :] += _colsum(d_adt * dtf) * a_f
        d_dtf = d_adt * a_f + dx * xs
        dxbc_ref[:, 0:D_SSD] = dx * dtf + dy * dsk_ref[...]
        d_raw = d_dtf * _sigmoid(sp_in)
        acc_ref[1:2, :] += _colsum(d_raw)
        head_of_lane = lax.broadcasted_iota(jnp.int32, (D_SSD, LANE), 0) // HEAD
        fold = (head_of_lane == lax.broadcasted_iota(jnp.int32, (D_SSD, LANE), 1)).astype(F32)
        ddt_ref[...] = _dot_exact(d_raw, fold, 2, "b").astype(BF16)

        @pl.when(step == nc - 1)
        def _():
            acc16_ref[...] = _dot_exact(acc_ref[...], fold, 3, "b")

    rchunk = lambda w, c: pl.BlockSpec((Q, w), lambda i: (nc - 1 - i, c))
    return pl.pallas_call(
        body, name=name, grid=(nc,),
        out_shape=[_sds((L, D_SSD), BF16), _sds((L, LANE), BF16), _sds((L, D_XBC), F32), _sds((8, D), F32), _sds((8, LANE), F32)],
        in_specs=[rchunk(D, 0), rchunk(D, 0), rchunk(D_XBC, 0), rchunk(D, 0), rchunk(D, 0),
                  pl.BlockSpec((1, N_STATE, D_SSD), lambda i: (nc - 1 - i, 0, 0))] + [_const_spec((1, D))] * 4,
        out_specs=[rchunk(D, 0), rchunk(LANE, 0), rchunk(D_XBC, 0), _const_spec((8, D)), _const_spec((8, LANE))],
        scratch_shapes=[pltpu.VMEM((N_STATE, D_SSD), F32), pltpu.VMEM((Q, D_SSD), F32), pltpu.VMEM((LANE, Q), F32),
                        pltpu.VMEM((Q, D_SSD), F32), pltpu.VMEM((Q, D_SSD), F32)],
        compiler_params=_cparams(("arbitrary",)))(dysn, y, xbc, z_src, dt_src, s_prev_all, dtb_f, alog_f, dsk_f, snw)


def _adamw_math(w, g, m, v):
    m_n = ADAM_B1 * m + (1.0 - ADAM_B1) * g
    v_n = ADAM_B2 * v + (1.0 - ADAM_B2) * jnp.square(g)
    c1 = 1.0 - ADAM_B1 ** ADAM_STEP
    c2 = 1.0 - ADAM_B2 ** ADAM_STEP
    return -ADAM_LR * ((m_n / c1) / (jnp.sqrt(v_n / c2) + ADAM_EPS) + ADAM_WD * w), m_n, v_n


def _sum_slots(p_ref):
    acc = p_ref[0].astype(F32)
    for s in range(1, p_ref.shape[0]):
        acc = acc + p_ref[s].astype(F32)
    return acc


def adamw_slots(w, slots, m, v, name):
    rows, cols = w.shape
    tc = 256

    def body(w_ref, s_ref, m_ref, v_ref, g_ref, d_ref, mo_ref, vo_ref):
        g_v = _sum_slots(s_ref)
        g_ref[...] = g_v
        d_ref[...], mo_ref[...], vo_ref[...] = _adamw_math(w_ref[...], g_v, m_ref[...], v_ref[...])

    spec = pl.BlockSpec((rows, tc), lambda i: (0, i))
    return pl.pallas_call(body, name=name, grid=(cols // tc,), out_shape=[_sds((rows, cols), F32)] * 4,
                          in_specs=[spec, pl.BlockSpec((slots.shape[0], rows, tc), lambda i: (0, 0, i)), spec, spec], out_specs=[spec] * 4,
                          compiler_params=_cparams(("parallel",)))(w, slots, m, v)


def adamw_many(ws, gs, ms, vs, name):
    n = len(ws)

    def body(*refs):
        for p in range(n):
            d_v, m_v, v_v = _adamw_math(refs[p][...], refs[n + p][...], refs[2 * n + p][...], refs[3 * n + p][...])
            refs[4 * n + p][...] = d_v
            refs[5 * n + p][...] = m_v
            refs[6 * n + p][...] = v_v

    vm = pl.BlockSpec(memory_space=pltpu.VMEM)
    out = pl.pallas_call(body, name=name, out_shape=[_sds(w.shape, F32) for w in ws] * 3, in_specs=[vm] * (4 * n),
                         out_specs=[vm] * (3 * n), compiler_params=_cparams())(*ws, *gs, *ms, *vs)
    return out[:n], out[n:2 * n], out[2 * n:]


def _pack_layout(shapes):
    row, layout = 0, []
    for rows, cols in shapes:
        chunks = []
        for c0 in range(0, cols, D):
            chunks.append((row, c0, min(D, cols - c0)))
            row += rows
        layout.append(chunks)
    return row, layout


def pack_rows(entries, name):
    arrays = [e[0] for e in entries]
    used, layout = _pack_layout([(e[2], e[0].shape[1]) for e in entries])
    total = -(-used // SUBLANES) * SUBLANES
    n = len(arrays)

    def body(*refs):
        o_ref = refs[n]
        o_ref[...] = jnp.zeros((total, D), F32)
        for p in range(n):
            _, first, rows = entries[p]
            for r0, c0, w in layout[p]:
                o_ref[r0:r0 + rows, 0:w] = refs[p][first:first + rows, c0:c0 + w]

    vm = pl.BlockSpec(memory_space=pltpu.VMEM)
    return pl.pallas_call(body, name=name, out_shape=_sds((total, D), F32), in_specs=[vm] * n, out_specs=vm,
                          compiler_params=_cparams())(*arrays)


def unpack_rows(packed, shapes):
    _, layout = _pack_layout(shapes)
    out = []
    for (rows, _), chunks in zip(shapes, layout):
        parts = [packed[..., r0:r0 + rows, 0:w] for r0, _, w in chunks]
        out.append(parts[0] if len(parts) == 1 else jnp.concatenate(parts, axis=-1))
    return out


def sum_slots_many(parts, name):
    n = len(parts)

    def body(*refs):
        for p in range(n):
            refs[n + p][...] = _sum_slots(refs[p])

    vm = pl.BlockSpec(memory_space=pltpu.VMEM)
    return pl.pallas_call(body, name=name, out_shape=[_sds(p.shape[1:], F32) for p in parts], in_specs=[vm] * n,
                          out_specs=[vm] * n, compiler_params=_cparams())(*parts)


def ada_mod(c_all, ada_w_shard, ada_b_cols, name):
    def body(c_ref, w_ref, b_ref, o_ref, ca_ref):
        ca = _silu(c_ref[...])
        ca_ref[...] = ca
        o_ref[...] = _dot(ca, w_ref[...]) + b_ref[...]

    vm = pl.BlockSpec(memory_space=pltpu.VMEM)
    return pl.pallas_call(body, name=name, out_shape=[_sds((N_DEV, ada_w_shard.shape[1]), F32), _sds((N_DEV, D), F32)],
                          in_specs=[vm, vm, vm], out_specs=[vm, vm], compiler_params=_cparams())(c_all, ada_w_shard, ada_b_cols)


def ada_wgrad(c_act_all, dmod_cols, name):
    def body(c_ref, d_ref, o_ref):
        o_ref[...] = _dot_tn_hi(c_ref[...], d_ref[...])

    vm = pl.BlockSpec(memory_space=pltpu.VMEM)
    return pl.pallas_call(body, name=name, out_shape=_sds((D, dmod_cols.shape[1]), F32), in_specs=[vm, vm], out_specs=vm,
                          compiler_params=_cparams())(c_act_all, dmod_cols)


def exchange(srcs, name, gather):
    n = len(srcs)
    gathers = [gather] * n if isinstance(gather, bool) else list(gather)
    shapes = [tuple(s.shape) if g else tuple(s.shape[1:]) for s, g in zip(srcs, gathers)]

    def body(*refs):
        src_refs, out_refs = refs[:n], refs[n:2 * n]
        send_sems, recv_sems, local_sems = refs[2 * n:]
        x, y, c = lax.axis_index("x"), lax.axis_index("y"), lax.axis_index("c")
        me = 4 * x + 2 * y + c

        def peer(k):
            bx, by, bc = (k >> 2) & 1, (k >> 1) & 1, k & 1
            px, py, pc = (x + bx) % 2, (y + by) % 2, (c + bc) % 2
            return (px, py, pc), 4 * px + 2 * py + pc

        def copy(a, k, landing):
            dev, idx = peer(k)
            return pltpu.make_async_remote_copy(
                src_ref=src_refs[a] if gathers[a] else src_refs[a].at[idx], dst_ref=out_refs[a].at[idx if landing else me],
                send_sem=send_sems.at[a, k - 1], recv_sem=recv_sems.at[a, k - 1],
                device_id=dev, device_id_type=pl.DeviceIdType.MESH)

        mine = [pltpu.make_async_copy(src_refs[a] if gathers[a] else src_refs[a].at[me], out_refs[a].at[me], local_sems.at[a])
                for a in range(n)]
        for cp in mine:
            cp.start()
        sends = [copy(a, k, False) for a in range(n) for k in range(1, N_DEV)]
        for cp in sends:
            cp.start()
        for a in range(n):
            for k in range(1, N_DEV):
                copy(a, k, True).wait_recv()
        for cp in sends:
            cp.wait_send()
        for cp in mine:
            cp.wait()

    hbm = pl.BlockSpec(memory_space=pl.ANY)
    return pl.pallas_call(
        body, name=name, out_shape=[_sds((N_DEV,) + shp, s.dtype) for shp, s in zip(shapes, srcs)], in_specs=[hbm] * n,
        out_specs=[hbm] * n,
        scratch_shapes=[pltpu.SemaphoreType.DMA((n, N_DEV - 1)), pltpu.SemaphoreType.DMA((n, N_DEV - 1)),
                        pltpu.SemaphoreType.DMA((n,))],
        compiler_params=pltpu.CompilerParams(has_side_effects=True))(*srcs)


def gather_two_level(srcs, name):
    n = len(srcs)

    def body(*refs):
        src_refs, out_refs = refs[:n], refs[n:2 * n]
        send_sems, recv_sems, local_sems = refs[2 * n:]
        x, y, c = lax.axis_index("x"), lax.axis_index("y"), lax.axis_index("c")
        me, sibling = (x, y, c), (x, y, 1 - c)
        chips = [(1 - x, y), (x, 1 - y), (1 - x, 1 - y)]

        def slot(a, px, py, pc):
            return out_refs[a].at[4 * px + 2 * py + pc]

        def copy(a, k, block, to, src=None):
            return pltpu.make_async_remote_copy(
                src_ref=slot(a, *block) if src is None else src, dst_ref=slot(a, *block), send_sem=send_sems.at[a, k],
                recv_sem=recv_sems.at[a, k], device_id=to, device_id_type=pl.DeviceIdType.MESH)

        mine = [pltpu.make_async_copy(src_refs[a], slot(a, *me), local_sems.at[a]) for a in range(n)]
        for cp in mine:
            cp.start()
        first = []
        for a in range(n):
            first += [copy(a, 0, me, sibling, src=src_refs[a])]
            first += [copy(a, 1 + j, me, (*chip, c), src=src_refs[a]) for j, chip in enumerate(chips)]
        for cp in first:
            cp.start()
        passed = []
        for a in range(n):
            for j, chip in enumerate(chips):
                copy(a, 1 + j, (*chip, c), me).wait_recv()
                passed.append(copy(a, 4 + j, (*chip, c), sibling))
                passed[-1].start()
        for a in range(n):
            copy(a, 0, sibling, me).wait_recv()
            for j, chip in enumerate(chips):
                copy(a, 4 + j, (*chip, 1 - c), me).wait_recv()
        for cp in first + passed:
            cp.wait_send()
        for cp in mine:
            cp.wait()

    hbm = pl.BlockSpec(memory_space=pl.ANY)
    return pl.pallas_call(
        body, name=name, out_shape=[_sds((N_DEV,) + tuple(s.shape), s.dtype) for s in srcs], in_specs=[hbm] * n,
        out_specs=[hbm] * n,
        scratch_shapes=[pltpu.SemaphoreType.DMA((n, N_DEV - 1)), pltpu.SemaphoreType.DMA((n, N_DEV - 1)),
                        pltpu.SemaphoreType.DMA((n,))],
        compiler_params=pltpu.CompilerParams(has_side_effects=True))(*srcs)


def _peer(k):
    x, y, c = lax.axis_index("x"), lax.axis_index("y"), lax.axis_index("c")
    px, py, pc = (x + ((k >> 2) & 1)) % 2, (y + ((k >> 1) & 1)) % 2, (c + (k & 1)) % 2
    return (px, py, pc), 4 * px + 2 * py + pc


def _my_slot():
    return 4 * lax.axis_index("x") + 2 * lax.axis_index("y") + lax.axis_index("c")


_HBM = pl.BlockSpec(memory_space=pltpu.HBM)
_SEM = pl.BlockSpec(memory_space=pltpu.SEMAPHORE)
_EFFECT = pltpu.SideEffectType.DATAFLOW_SIDE_EFFECTING


def exchange_start(srcs, name, gather):
    n = len(srcs)
    shapes = [tuple(s.shape) if gather else tuple(s.shape[1:]) for s in srcs]
    lands = [lax.empty((N_DEV,) + shp, s.dtype) for shp, s in zip(shapes, srcs)]

    def body(*refs):
        src_refs, land_refs = refs[:n], refs[n:2 * n]
        sems = refs[2 * n:4 * n]
        token = refs[-1]
        me = _my_slot()
        for a in range(n):
            for k in range(1, N_DEV):
                dev, idx = _peer(k)
                pltpu.make_async_remote_copy(
                    src_ref=src_refs[a] if gather else src_refs[a].at[idx], dst_ref=land_refs[a].at[me],
                    send_sem=sems[2 * a].at[k - 1], recv_sem=sems[2 * a + 1].at[k - 1],
                    device_id=dev, device_id_type=pl.DeviceIdType.MESH).start()
        token[...] = jnp.zeros_like(token)

    out_shape = ([pltpu.SemaphoreType.DMA((N_DEV - 1,))] * (2 * n) + [pltpu.HBM(s.shape, s.dtype) for s in srcs]
                 + [pltpu.HBM(l.shape, l.dtype) for l in lands] + [_sds((8, LANE), F32)])
    out = pl.pallas_call(
        body, name=name, out_shape=out_shape, in_specs=[_HBM] * (2 * n),
        out_specs=[_SEM] * (2 * n) + [_HBM] * (2 * n) + [pl.BlockSpec(memory_space=pltpu.VMEM)],
        input_output_aliases={i: 2 * n + i for i in range(2 * n)},
        compiler_params=pltpu.CompilerParams(has_side_effects=_EFFECT))(
            *[pltpu.with_memory_space_constraint(s, pltpu.HBM) for s in srcs],
            *[pltpu.with_memory_space_constraint(l, pltpu.HBM) for l in lands])
    parts = [(out[2 * a], out[2 * a + 1], out[2 * n + a], out[3 * n + a]) for a in range(n)]
    return parts, out[-1]


def exchange_wait(parts, after, name, gather):
    n = len(parts)

    def body(*refs):
        src_refs, land_refs = refs[:n], refs[n:2 * n]
        sems = refs[2 * n:4 * n]
        for a in range(n):
            for k in range(1, N_DEV):
                dev, idx = _peer(k)
                copy = pltpu.make_async_remote_copy(
                    src_ref=src_refs[a] if gather else src_refs[a].at[idx], dst_ref=land_refs[a].at[idx],
                    send_sem=sems[2 * a].at[k - 1], recv_sem=sems[2 * a + 1].at[k - 1],
                    device_id=dev, device_id_type=pl.DeviceIdType.MESH)
                copy.wait_send()
                copy.wait_recv()

    srcs = [p[2] for p in parts]
    lands = [p[3] for p in parts]
    sems = [s for p in parts for s in p[:2]]
    out = pl.pallas_call(
        body, name=name, out_shape=[pltpu.HBM(a.shape, a.dtype) for a in srcs + lands],
        in_specs=[_HBM] * (2 * n) + [_SEM] * (2 * n) + [pl.BlockSpec(memory_space=pl.ANY)], out_specs=[_HBM] * (2 * n),
        input_output_aliases={i: i for i in range(2 * n)},
        compiler_params=pltpu.CompilerParams(has_side_effects=_EFFECT))(*srcs, *lands, *sems, after)
    return list(zip(out[:n], out[n:]))


def _cols_to_slabs(g):
    r, c = g.shape
    return g.reshape(r, N_DEV, c // N_DEV).transpose(1, 0, 2)


def _slabs_to_cols(s):
    _, r, cs = s.shape
    return s.transpose(1, 0, 2).reshape(r, N_DEV * cs)


def _rep_heads(v):
    return jnp.repeat(v.reshape(N_HEADS), HEAD).reshape(1, D_SSD)


def local_fwd_bwd(x, target, mod, get_w, put_grad, small):
    n1w, n2w, fnw = small["norm1_w"], small["norm2_w"], small["final_norm_w"]
    dtb_f, alog_f, dsk_f = _rep_heads(small["dt_bias"]), _rep_heads(small["a_log"]), _rep_heads(small["d_skip"])
    snw = small["ssd_norm_w"]

    def after(v, token):
        return v + token[0:1, 0:1]

    w_in = get_w("w_in", mod)
    h1, proj_zx, proj_dt, proj_cf = in_proj(x, mod, n1w, w_in["w_full"], w_in["w_dt_rep"], w_in["w_cf"], "norm1_in_proj")
    xbc = conv_silu_fwd(proj_zx, D_SSD // CB, D_XBC, small["ssd_conv_w"], small["ssd_conv_b"], "ssd_conv")
    y, ysn, s_prev = ssd_fwd(xbc, proj_zx, proj_dt, dtb_f, alog_f, dsk_f, snw, "ssd_scan")
    uc = conf_conv_fwd(proj_cf, 0, D_CONF // CB, small["conf_conv_w"], small["conf_conv_b"], "conf_conv")
    w_out = get_w("w_out", uc)
    mix, u, x1, h2 = mixer_out(ysn, uc, small["conf_ln_w"], small["conf_ln_b"], w_out, x, mod, n2w, "out_proj_norm2")
    w_up_t = get_w("w_up", h2)
    up = mm_nt([(h2, w_up_t, 0)], "up_proj")
    act = ffn_conv_fwd(up, small["ffn_conv_w"], small["ffn_conv_b"], "ffn_conv")
    w_down = get_w("w_down", act)
    dx2, dff, dact, acc_f = final_loss(act, w_down, x1, mod, fnw, target, "down_proj_loss")

    token = put_grad("w_down", mm_tn(act, dff, "wgrad_down"))
    dupg, dupv, dwg, dwv = ffn_conv_bwd(up, small["ffn_conv_w"], after(small["ffn_conv_b"], token), dact, "ffn_conv_bwd")
    token = put_grad("w_up", mm_tn_stack([dupg, dupv], h2, "wgrad_up"))
    dx1, dmix, acc_2 = norm_mod_bwd([(dupg, w_up_t, 0), (dupv, w_up_t, 1)], x1, dx2, mod, after(n2w, token), 3, "norm2_bwd",
                                    mix=mix, gate_row=2)

    token = put_grad("w_out", mm_tn_stack([ysn, u], dmix, "wgrad_out"))
    dysn, duc, acc_ln = mixer_out_bwd(dmix, w_out, uc, after(small["conf_ln_w"], token), small["conf_ln_b"], "out_proj_bwd")
    dcfa, dcfg, dw_cc = conf_conv_bwd(proj_cf, 0, D_CONF // CB, small["conf_conv_w"], duc, "conf_conv_bwd")
    dz, ddt, dxbc_post, acc_s, acc_s16 = ssd_bwd(dysn, y, xbc, proj_zx, proj_dt, s_prev, dtb_f, alog_f, dsk_f, snw,
                                                 "ssd_scan_bwd")
    dxbc, dw_sc = conv_silu_bwd(proj_zx, D_SSD // CB, D_XBC, small["ssd_conv_w"], small["ssd_conv_b"], dxbc_post, "ssd_conv_bwd")
    token = put_grad("w_in", mm_tn_concat(
        [(dz, D_SSD), (dxbc, D_XBC), (ddt, N_HEADS), (dcfa, D_CONF), (dcfg, D_CONF)], h1, "wgrad_in"))
    dh1_pairs = [(dz, w_in["w_full"], 0), (ddt, w_in["w_dt16"], 0), (dcfa, w_in["w_cf"], 0), (dcfg, w_in["w_cf"], 1),
                 (dxbc, w_in["w_xbc"], 0)]
    grad_x, acc_1 = norm_mod_bwd(dh1_pairs, x, dx1, mod, after(n1w, token), 0, "norm1_bwd")

    small_accs = dict(acc_1=acc_1, acc_2=acc_2, acc_f=acc_f, acc_ln=acc_ln, acc_s=acc_s, acc_s16=acc_s16, dw_sc=dw_sc,
                      dw_cc=dw_cc, dwg=dwg, dwv=dwv)
    return grad_x, small_accs


def kernel(x, c, ada_w, ada_b, norm1_w, w_in, ssd_conv_w, ssd_conv_b, dt_bias, a_log, d_skip, ssd_norm_w, conf_conv_w, conf_conv_b, conf_ln_w, conf_ln_b, w_out, norm2_w, w_up, ffn_conv_w, ffn_conv_b, w_down, final_norm_w, loss_target, m_ada_w, m_ada_b, m_norm1_w, m_w_in, m_ssd_conv_w, m_ssd_conv_b, m_dt_bias, m_a_log, m_d_skip, m_ssd_norm_w, m_conf_conv_w, m_conf_conv_b, m_conf_ln_w, m_conf_ln_b, m_w_out, m_norm2_w, m_w_up, m_ffn_conv_w, m_ffn_conv_b, m_w_down, m_final_norm_w, v_ada_w, v_ada_b, v_norm1_w, v_w_in, v_ssd_conv_w, v_ssd_conv_b, v_dt_bias, v_a_log, v_d_skip, v_ssd_norm_w, v_conf_conv_w, v_conf_conv_b, v_conf_ln_w, v_conf_ln_b, v_w_out, v_norm2_w, v_w_up, v_ffn_conv_w, v_ffn_conv_b, v_w_down, v_final_norm_w):
    me = 4 * lax.axis_index("x") + 2 * lax.axis_index("y") + lax.axis_index("c")
    weights = dict(ada_w=ada_w, ada_b=ada_b, norm1_w=norm1_w, w_in=w_in, ssd_conv_w=ssd_conv_w, ssd_conv_b=ssd_conv_b,
                   dt_bias=dt_bias, a_log=a_log, d_skip=d_skip, ssd_norm_w=ssd_norm_w, conf_conv_w=conf_conv_w,
                   conf_conv_b=conf_conv_b, conf_ln_w=conf_ln_w, conf_ln_b=conf_ln_b, w_out=w_out, norm2_w=norm2_w, w_up=w_up,
                   ffn_conv_w=ffn_conv_w, ffn_conv_b=ffn_conv_b, w_down=w_down, final_norm_w=final_norm_w)
    moms_m = dict(ada_w=m_ada_w, ada_b=m_ada_b, norm1_w=m_norm1_w, w_in=m_w_in, ssd_conv_w=m_ssd_conv_w, ssd_conv_b=m_ssd_conv_b,
                  dt_bias=m_dt_bias, a_log=m_a_log, d_skip=m_d_skip, ssd_norm_w=m_ssd_norm_w, conf_conv_w=m_conf_conv_w,
                  conf_conv_b=m_conf_conv_b, conf_ln_w=m_conf_ln_w, conf_ln_b=m_conf_ln_b, w_out=m_w_out, norm2_w=m_norm2_w,
                  w_up=m_w_up, ffn_conv_w=m_ffn_conv_w, ffn_conv_b=m_ffn_conv_b, w_down=m_w_down, final_norm_w=m_final_norm_w)
    moms_v = dict(ada_w=v_ada_w, ada_b=v_ada_b, norm1_w=v_norm1_w, w_in=v_w_in, ssd_conv_w=v_ssd_conv_w, ssd_conv_b=v_ssd_conv_b,
                  dt_bias=v_dt_bias, a_log=v_a_log, d_skip=v_d_skip, ssd_norm_w=v_ssd_norm_w, conf_conv_w=v_conf_conv_w,
                  conf_conv_b=v_conf_conv_b, conf_ln_w=v_conf_ln_w, conf_ln_b=v_conf_ln_b, w_out=v_w_out, norm2_w=v_norm2_w,
                  w_up=v_w_up, ffn_conv_w=v_ffn_conv_w, ffn_conv_b=v_ffn_conv_b, w_down=v_w_down, final_norm_w=v_final_norm_w)
    names = list(weights)

    def to2d(a):
        return a[0] if a.ndim == 3 else a.reshape(1, -1)

    big = ("w_in", "w_out", "w_up", "w_down")

    def rows_of(a):
        return jnp.swapaxes(a, 1, 2)[0] if a.shape[2] != D else a[0]

    shards = [rows_of(weights[n]).astype(BF16) for n in big]

    c_all, scw_all, ccw_all, fcw_all, w_in_slabs = gather_two_level(
        [c.reshape(8, LANE), ssd_conv_w[0], conf_conv_w[0], ffn_conv_w[0], shards[0]], "gather_first")
    c_all = c_all.reshape(N_DEV, D)

    ada_cols = ada_w.shape[2]
    ada_b_cols = lax.dynamic_slice(ada_b, (0, me * ada_cols), (1, ada_cols))
    mod_cols, c_act_all = ada_mod(c_all, ada_w[0], ada_b_cols, "ada_mod")
    mod_parts, = exchange([jnp.pad(mod_cols, ((0, 0), (0, D - ada_cols))).reshape(N_DEV, 8, LANE)], "scatter_mod", gather=False)
    mod = mod_parts.reshape(N_DEV, D)[:, :ada_cols].reshape(6, D)
    mod = jnp.pad(mod, ((0, 2), (0, 0)))

    later, mod = lax.optimization_barrier((shards[1:], mod))
    gather_parts, token = exchange_start(later, "gather_weights_start", gather=True)
    mod = mod + token[0:1, 0:1]

    small = {n: to2d(weights[n]) for n in names if n not in ("ada_w",) + big}
    small["ssd_conv_w"] = _slabs_to_cols(scw_all)
    small["conf_conv_w"] = _slabs_to_cols(ccw_all)
    small["ffn_conv_w"] = _slabs_to_cols(fcw_all)

    def with_own(landed, own):
        return lax.dynamic_update_slice(landed, own[None], (me,) + (0,) * own.ndim)

    def get_w(n, after):
        if n == "w_in":
            slabs = w_in_slabs
        else:
            a = big.index(n)
            (own, landed), = exchange_wait([gather_parts[a - 1]], after, "gather_" + n + "_wait", gather=True)
            slabs = with_own(landed, own)
        full = slabs.reshape(N_DEV * slabs.shape[1], D)
        if n != "w_in":
            return full
        w_dt = full[D_SSD + D_XBC:D_SSD + D_XBC + N_HEADS]
        return dict(w_full=full, w_xbc=full[D_SSD:D_SSD + D_XBC], w_cf=full[D_SSD + D_XBC + N_HEADS:],
                    w_dt_rep=jnp.repeat(w_dt, HEAD, axis=0), w_dt16=jnp.pad(w_dt, ((0, LANE - N_HEADS), (0, 0))))

    scatter_parts = {}

    def put_grad(n, g):
        slabs = g if g.ndim == 3 else g.reshape(N_DEV, g.shape[0] // N_DEV, g.shape[1])
        (scatter_parts[n],), token = exchange_start([slabs.astype(BF16)], "scatter_" + n + "_start", gather=False)
        return token

    grad_x, accs = local_fwd_bwd(x[0], loss_target[0], mod, get_w, put_grad, small)

    grads, delta, new_m, new_v = {}, {}, {}, {}

    def finish(ns, after, name):
        landed = exchange_wait([scatter_parts[n] for n in ns], after, name, gather=False)
        for n, (sent, slots) in zip(ns, landed):
            slots = with_own(slots, lax.dynamic_index_in_dim(sent, me, 0, keepdims=False))
            out = adamw_slots(rows_of(weights[n]), slots, rows_of(moms_m[n]), rows_of(moms_v[n]), "adamw_" + n)
            if weights[n].shape[2] != D:
                out = [jnp.swapaxes(o, 0, 1) for o in out]
            grads[n], delta[n], new_m[n], new_v[n] = out

    finish(big[1:], grad_x, "scatter_grads_wait")

    accs = dict(zip(accs, lax.optimization_barrier((list(accs.values()), [new_v[n] for n in big[1:]]))[0]))
    rep = (("acc_1", 0, 3), ("acc_2", 0, 4), ("acc_f", 0, 3), ("acc_ln", 0, 2), ("acc_s", 0, 1), ("acc_s16", 1, 3),
           ("dw_sc", K_SSD, 1), ("dw_cc", K_CONF, 1), ("dwg", K_FFN, 1), ("dwv", K_FFN, 1))
    shapes = [(rows, accs[k].shape[1]) for k, _, rows in rep]
    conv_slabs = [_cols_to_slabs(accs["dw_sc"][:K_SSD]), _cols_to_slabs(accs["dw_cc"][:K_CONF]),
                  _cols_to_slabs(jnp.concatenate([accs["dwg"][:K_FFN], accs["dwv"][:K_FFN]], axis=1))]
    packed = pack_rows([(accs[k], first, rows) for k, first, rows in rep], "pack_small_grads")
    landed = exchange([packed] + conv_slabs, "exchange_small_grads", gather=[True, False, False, False])
    packed_red, g_scw, g_ccw, g_fcw = sum_slots_many(landed, "sum_small_grads")
    a1_all, a2_all, af_all = unpack_rows(landed[0], shapes)[:3]
    r1, r2, rf, rln, rs, r16, rscb, rccb, rfbg, rfbv = unpack_rows(packed_red, shapes)
    loss = 0.5 / D * jnp.sum(rf[2:3])

    def mod_rows(a1, a2, af):
        return jnp.concatenate([a1[..., 0:2, :], a2[..., 3:4, :], a2[..., 0:2, :], af[..., 1:2, :]], axis=-2)

    dmod_all = mod_rows(a1_all, a2_all, af_all).reshape(N_DEV, 6 * D)
    grads["ada_w"] = ada_wgrad(c_act_all, lax.dynamic_slice(dmod_all, (0, me * ada_cols), (N_DEV, ada_cols)), "ada_wgrad")
    grads.update(
        ada_b=mod_rows(r1, r2, rf).reshape(1, 6 * D), norm1_w=r1[2:3], ssd_conv_w=g_scw, ssd_conv_b=rscb,
        dt_bias=r16[0:1, :N_HEADS], a_log=r16[1:2, :N_HEADS], d_skip=r16[2:3, :N_HEADS], ssd_norm_w=rs,
        conf_conv_w=g_ccw, conf_conv_b=rccb, conf_ln_w=rln[0:1], conf_ln_b=rln[1:2], norm2_w=r2[2:3],
        ffn_conv_w=g_fcw, ffn_conv_b=jnp.concatenate([rfbg, rfbv], axis=1), final_norm_w=rf[0:1])

    rest = [n for n in names if n not in big]
    d_l, m_l, v_l = adamw_many([to2d(weights[n]) for n in rest], [grads[n] for n in rest], [to2d(moms_m[n]) for n in rest],
                               [to2d(moms_v[n]) for n in rest], "adamw_small")
    for n, dd, mm, vv in zip(rest, d_l, m_l, v_l):
        delta[n], new_m[n], new_v[n] = dd, mm, vv
    finish(big[:1], d_l[0], "scatter_w_in_wait")
    shape_of = lambda d_: {n: d_[n].reshape(weights[n].shape) for n in names}
    grads, delta, new_m, new_v = shape_of(grads), shape_of(delta), shape_of(new_m), shape_of(new_v)
    return (loss, grad_x[None], *[grads[n] for n in names], *[delta[n] for n in names], *[new_m[n] for n in names],
            *[new_v[n] for n in names])
```

```python
import functools

import jax
import jax.numpy as jnp
from jax import lax
from jax.experimental import pallas as pl
from jax.experimental.pallas import tpu as pltpu

F32 = jnp.float32
BF16 = jnp.bfloat16
HI = lax.Precision.HIGHEST

N_DEV = 8
D = 1024
D_SSD = 1024
HEAD = 64
N_HEADS = 16
N_STATE = 128
D_XBC = 1536
D_CONF = 1024
D_FF = 2816
K_SSD, K_CONF, K_FFN = 4, 31, 3
LANE = 128
TM = 512
Q = 256
CB = 256
TC = 1024
VMEM_LIMIT = 56 * 1024 * 1024

ADAM_LR, ADAM_B1, ADAM_B2, ADAM_EPS, ADAM_WD, ADAM_STEP = 0.001, 0.9, 0.999, 1e-08, 0.01, 10


def _cparams(sem=None):
    return pltpu.CompilerParams(vmem_limit_bytes=VMEM_LIMIT, dimension_semantics=sem)


def _sds(shape, dtype):
    return jax.ShapeDtypeStruct(shape, dtype)


def _sigmoid(x):
    return 1.0 / (1.0 + jnp.exp(-x))


def _silu(x):
    return x * _sigmoid(x)


def _dsilu(x):
    s = _sigmoid(x)
    return s * (1.0 + x * (1.0 - s))


def _softplus(x):
    return jnp.maximum(x, 0.0) + jnp.log(1.0 + jnp.exp(-jnp.abs(x)))


def _dot(a, b):
    return jnp.dot(a.astype(BF16), b.astype(BF16), preferred_element_type=F32)


def _dot_nt(a, b):
    return lax.dot_general(a.astype(BF16), b.astype(BF16), (((1,), (1,)), ((), ())), preferred_element_type=F32)


def _dot_tn(a, b):
    return lax.dot_general(a.astype(BF16), b.astype(BF16), (((0,), (0,)), ((), ())), preferred_element_type=F32)


def _bf16_terms(a, terms):
    parts, rem = [], a
    for t in range(terms):
        p = rem.astype(BF16)
        parts.append(p)
        if t + 1 < terms:
            rem = rem - p.astype(F32)
    return parts


def _dot_exact(a, b, terms, exact, dims=(((1,), (0,)), ((), ()))):
    if exact == "a":
        a_b = a.astype(BF16)
        outs = [lax.dot_general(a_b, p, dims, preferred_element_type=F32) for p in _bf16_terms(b, terms)]
    else:
        b_b = b.astype(BF16)
        outs = [lax.dot_general(p, b_b, dims, preferred_element_type=F32) for p in _bf16_terms(a, terms)]
    acc = outs[-1]
    for o in reversed(outs[:-1]):
        acc = acc + o
    return acc


def _dot_tn_hi(a, b):
    return lax.dot_general(a, b, (((0,), (0,)), ((), ())), precision=HI, preferred_element_type=F32)


def _colsum(x):
    return jnp.sum(x, axis=0, keepdims=True)


def _const_spec(shape):
    return pl.BlockSpec(shape, lambda *_: (0,) * len(shape))


def _resident_spec(shape, block_index=None):
    index = (0,) * len(shape) if block_index is None else block_index
    return pl.BlockSpec(shape, lambda *_: index, pipeline_mode=pl.Buffered(1))


def _col_tile(n):
    for t in (2816, 1408, 1024, 768, 512, 256, 128):
        if n % t == 0 and t <= n:
            return t
    return n


def mm_nt(pairs, name):
    L = pairs[0][0].shape[0]
    K = pairs[0][1].shape[0]
    tk = _col_tile(K)
    n = len(pairs)

    def body(*refs):
        o_ref = refs[-1]
        acc = None
        for p in range(n):
            t = lax.dot_general(refs[2 * p][...], refs[2 * p + 1][...], (((1,), (1,)), ((), ())),
                                preferred_element_type=F32)
            acc = t if acc is None else acc + t
        o_ref[...] = acc

    in_specs, args = [], []
    for a, w, cb in pairs:
        in_specs += [pl.BlockSpec((TM, a.shape[1]), lambda j, i: (i, 0)),
                     pl.BlockSpec((tk, a.shape[1]), functools.partial(lambda j, i, cb: (j, cb), cb=cb))]
        args += [a, w]
    return pl.pallas_call(
        body, name=name, grid=(K // tk, L // TM), out_shape=_sds((L, K), F32), in_specs=in_specs,
        out_specs=pl.BlockSpec((TM, tk), lambda j, i: (i, j)),
        compiler_params=_cparams(("parallel", "parallel")))(*args)


def mm_tn(a, g, name):
    L, M = a.shape
    N = g.shape[1]
    tn = _col_tile(N) if N > 1024 else N
    if M * tn * 4 > 12 * 1024 * 1024:
        tn = 512
    tl = TM
    nl = L // tl

    def body(a_ref, g_ref, o_ref, acc_ref):
        @pl.when(pl.program_id(1) == 0)
        def _():
            acc_ref[...] = jnp.zeros((M, tn), F32)

        acc_ref[...] += lax.dot_general(a_ref[...], g_ref[...], (((0,), (0,)), ((), ())), preferred_element_type=F32)

        @pl.when(pl.program_id(1) == nl - 1)
        def _():
            o_ref[...] = acc_ref[...].astype(BF16)

    return pl.pallas_call(
        body, name=name, grid=(N // tn, nl), out_shape=_sds((M, N), BF16),
        in_specs=[pl.BlockSpec((tl, M), lambda j, l: (l, 0)), pl.BlockSpec((tl, tn), lambda j, l: (l, j))],
        out_specs=pl.BlockSpec((M, tn), lambda j, l: (0, j)), scratch_shapes=[pltpu.VMEM((M, tn), F32)],
        compiler_params=_cparams(("parallel", "arbitrary")))(a, g)


def mm_tn_stack(a_list, g, name):
    L, M = a_list[0].shape
    N = g.shape[1]
    n = len(a_list)
    tl = TM
    nl = L // tl

    def body(*refs):
        a_refs, g_ref, o_ref, acc_ref = refs[:n], refs[n], refs[n + 1], refs[n + 2]
        j, l = pl.program_id(0), pl.program_id(1)

        @pl.when(l == 0)
        def _():
            acc_ref[...] = jnp.zeros((M, N), F32)

        for p in range(n):
            @pl.when(j == p)
            def _(p=p):
                acc_ref[...] += lax.dot_general(a_refs[p][...], g_ref[...], (((0,), (0,)), ((), ())), preferred_element_type=F32)

        @pl.when(l == nl - 1)
        def _():
            o_ref[...] = acc_ref[...].astype(BF16)

    a_specs = [pl.BlockSpec((tl, M), functools.partial(lambda j, l, p: (jnp.where(j == p, l, 0), 0), p=p)) for p in range(n)]
    return pl.pallas_call(
        body, name=name, grid=(n, nl), out_shape=_sds((n * M, N), BF16),
        in_specs=a_specs + [pl.BlockSpec((tl, N), lambda j, l: (l, 0))],
        out_specs=pl.BlockSpec((M, N), lambda j, l: (j, 0)), scratch_shapes=[pltpu.VMEM((M, N), F32)],
        compiler_params=_cparams(("arbitrary", "arbitrary")))(*a_list, g)


def mm_tn_concat(pieces, g, name):
    L = g.shape[0]
    N = g.shape[1]
    n = len(pieces)
    offsets = [sum(r for _, r in pieces[:p]) for p in range(n + 1)]
    slab = offsets[-1] // N_DEV
    tl = TM
    nl = L // tl

    def body(*refs):
        a_refs, g_ref, o_ref, acc_ref = refs[:n], refs[n], refs[n + 1], refs[n + 2]
        l = pl.program_id(0)

        @pl.when(l == 0)
        def _():
            acc_ref[...] = jnp.zeros((offsets[-1], N), F32)

        g_v = g_ref[...]
        for p in range(n):
            t = lax.dot_general(a_refs[p][...], g_v, (((0,), (0,)), ((), ())), preferred_element_type=F32)
            acc_ref[offsets[p]:offsets[p + 1], :] += t[:pieces[p][1], :]

        @pl.when(l == nl - 1)
        def _():
            for s in range(N_DEV):
                o_ref[s] = acc_ref[s * slab:(s + 1) * slab, :].astype(BF16)

    return pl.pallas_call(
        body, name=name, grid=(nl,), out_shape=_sds((N_DEV, slab, N), BF16),
        in_specs=[pl.BlockSpec((tl, a.shape[1]), lambda l: (l, 0)) for a, _ in pieces] + [pl.BlockSpec((tl, N), lambda l: (l, 0))],
        out_specs=_const_spec((N_DEV, slab, N)), scratch_shapes=[pltpu.VMEM((offsets[-1], N), F32)],
        compiler_params=_cparams(("arbitrary",)))(*[a for a, _ in pieces], g)


def in_proj(x, mod, n1w, w_full, w_dt_rep, w_cf, name):
    L = x.shape[0]
    n_zx = D_SSD + D_XBC

    def body(x_ref, mod_ref, w_ref, wzx_ref, wdt_ref, wcf_ref, h_ref, zx_ref, dt_ref, cf_ref):
        xin = x_ref[...]
        r = lax.rsqrt(jnp.mean(xin * xin, axis=-1, keepdims=True) + 1e-6)
        h = ((xin * r * w_ref[...]) * (1.0 + mod_ref[1:2, :]) + mod_ref[0:1, :]).astype(BF16)
        h_ref[...] = h
        nt = (((1,), (1,)), ((), ()))
        zx_ref[...] = lax.dot_general(h, wzx_ref[...], nt, preferred_element_type=F32)
        dt_ref[...] = lax.dot_general(h, wdt_ref[...], nt, preferred_element_type=F32)
        cf_ref[...] = lax.dot_general(h, wcf_ref[...], nt, preferred_element_type=F32)

    row = lambda w: pl.BlockSpec((TM, w), lambda i: (i, 0))
    return pl.pallas_call(
        body, name=name, grid=(L // TM,),
        out_shape=[_sds((L, D), BF16), _sds((L, n_zx), F32), _sds((L, D_SSD), F32), _sds((L, 2 * D_CONF), F32)],
        in_specs=[row(D), _const_spec((8, D)), _const_spec((1, D)), _const_spec((n_zx, D)), _const_spec((D_SSD, D)),
                  _const_spec((2 * D_CONF, D))],
        out_specs=[row(D), row(n_zx), row(D_SSD), row(2 * D_CONF)],
        compiler_params=_cparams(("parallel",)))(x, mod, n1w, w_full, w_dt_rep, w_cf)


def mixer_out(ysn, uc, lnw, lnb, w_out, x, mod, n2w, name):
    L = x.shape[0]

    def body(ysn_ref, uc_ref, lnw_ref, lnb_ref, wo_ref, x_ref, mod_ref, n2w_ref, mix_ref, u_ref, x1_ref, h2_ref):
        uc_v = uc_ref[...]
        mu = jnp.mean(uc_v, axis=-1, keepdims=True)
        var = jnp.mean(jnp.square(uc_v - mu), axis=-1, keepdims=True)
        u = _silu((uc_v - mu) * lax.rsqrt(var + 1e-5) * lnw_ref[...] + lnb_ref[...]).astype(BF16)
        u_ref[...] = u
        mix = (jnp.dot(ysn_ref[...], wo_ref[0:D_SSD, :], preferred_element_type=F32)
               + jnp.dot(u, wo_ref[D_SSD:D_SSD + D_CONF, :], preferred_element_type=F32))
        mix_ref[...] = mix
        x1 = x_ref[...] + mod_ref[2:3, :] * mix
        x1_ref[...] = x1
        r = lax.rsqrt(jnp.mean(x1 * x1, axis=-1, keepdims=True) + 1e-6)
        h2_ref[...] = ((x1 * r * n2w_ref[...]) * (1.0 + mod_ref[4:5, :]) + mod_ref[3:4, :]).astype(BF16)

    row = pl.BlockSpec((TM, D), lambda i: (i, 0))
    return pl.pallas_call(
        body, name=name, grid=(L // TM,),
        out_shape=[_sds((L, D), F32), _sds((L, D_CONF), BF16), _sds((L, D), F32), _sds((L, D), BF16)],
        in_specs=[row, row, _const_spec((1, D)), _const_spec((1, D)), _const_spec((D_SSD + D_CONF, D)), row,
                  _const_spec((8, D)), _const_spec((1, D))],
        out_specs=[row] * 4, compiler_params=_cparams(("parallel",)))(ysn, uc, lnw, lnb, w_out, x, mod, n2w)


def mixer_out_bwd(dmix, w_out, uc, lnw, lnb, name):
    L = uc.shape[0]

    def body(dm_ref, wo_ref, u_ref, w_ref, b_ref, dy_ref, o_ref, acc_ref):
        @pl.when(pl.program_id(0) == 0)
        def _():
            acc_ref[...] = jnp.zeros((8, D), F32)

        nt = (((1,), (1,)), ((), ()))
        dm = dm_ref[...]
        dy_ref[...] = lax.dot_general(dm, wo_ref[0:D_SSD, :], nt, preferred_element_type=F32)
        du = lax.dot_general(dm, wo_ref[D_SSD:D_SSD + D_CONF, :], nt, preferred_element_type=F32)
        u = u_ref[...]
        mu = jnp.mean(u, axis=-1, keepdims=True)
        rl = lax.rsqrt(jnp.mean(jnp.square(u - mu), axis=-1, keepdims=True) + 1e-5)
        n = (u - mu) * rl
        v = n * w_ref[...] + b_ref[...]
        dv = du * _dsilu(v)
        acc_ref[0:1, :] += _colsum(dv * n)
        acc_ref[1:2, :] += _colsum(dv)
        dn = dv * w_ref[...]
        o_ref[...] = rl * (dn - jnp.mean(dn, axis=-1, keepdims=True) - n * jnp.mean(dn * n, axis=-1, keepdims=True))

    row = pl.BlockSpec((TM, D), lambda i: (i, 0))
    return pl.pallas_call(body, name=name, grid=(L // TM,), out_shape=[_sds((L, D), F32), _sds((L, D), F32), _sds((8, D), F32)],
                          in_specs=[row, _const_spec((D_SSD + D_CONF, D)), row, _const_spec((1, D)), _const_spec((1, D))],
                          out_specs=[row, row, _const_spec((8, D))],
                          compiler_params=_cparams(("arbitrary",)))(dmix, w_out, uc, lnw, lnb)


def final_loss(act, w_down, x1, mod, fw, target, name):
    L = act.shape[0]

    def body(act_ref, wd_ref, x1_ref, mod_ref, fw_ref, t_ref, dx_ref, dff_ref, dact_ref, acc_ref):
        @pl.when(pl.program_id(0) == 0)
        def _():
            acc_ref[...] = jnp.zeros((8, D), F32)

        ff_v = jnp.dot(act_ref[...], wd_ref[...], preferred_element_type=F32)
        g2 = mod_ref[5:6, :]
        x2 = x1_ref[...] + g2 * ff_v
        r = lax.rsqrt(jnp.mean(x2 * x2, axis=-1, keepdims=True) + 1e-6)
        n = x2 * r
        err = n * fw_ref[...] - t_ref[...]
        dy = err * (1.0 / D)
        dn = dy * fw_ref[...]
        dx2 = r * (dn - n * jnp.mean(dn * n, axis=-1, keepdims=True))
        acc_ref[0:1, :] += _colsum(dy * n)
        acc_ref[1:2, :] += _colsum(dx2 * ff_v)
        acc_ref[2:3, :] += _colsum(err * err)
        dx_ref[...] = dx2
        dff = (dx2 * g2).astype(BF16)
        dff_ref[...] = dff
        dact_ref[...] = lax.dot_general(dff, wd_ref[...], (((1,), (1,)), ((), ())), preferred_element_type=F32)

    row = lambda w: pl.BlockSpec((TM, w), lambda i: (i, 0))
    return pl.pallas_call(
        body, name=name, grid=(L // TM,),
        out_shape=[_sds((L, D), F32), _sds((L, D), BF16), _sds((L, D_FF), F32), _sds((8, D), F32)],
        in_specs=[row(D_FF), _resident_spec((D_FF, D)), row(D), _const_spec((8, D)), _const_spec((1, D)), row(D)],
        out_specs=[row(D), row(D), row(D_FF), _const_spec((8, D))],
        compiler_params=_cparams(("arbitrary",)))(act, w_down, x1, mod, fw, target)


def norm_mod_bwd(dh_pairs, xin, dres, mod, w, shift_row, name, mix=None, gate_row=None):
    L = xin.shape[0]
    has_mix = mix is not None
    n_pairs = len(dh_pairs)

    def body(*refs):
        pair_refs, refs = refs[:2 * n_pairs], refs[2 * n_pairs:]
        if has_mix:
            x_ref, dres_ref, mod_ref, w_ref, mix_ref, dx_ref, dmix_ref, acc_ref = refs
        else:
            x_ref, dres_ref, mod_ref, w_ref, dx_ref, acc_ref = refs

        @pl.when(pl.program_id(0) == 0)
        def _():
            acc_ref[...] = jnp.zeros((8, D), F32)

        dh_v = None
        for p in range(n_pairs):
            t = jnp.dot(pair_refs[2 * p][...], pair_refs[2 * p + 1][...], preferred_element_type=F32)
            dh_v = t if dh_v is None else dh_v + t
        x = x_ref[...]
        r = lax.rsqrt(jnp.mean(x * x, axis=-1, keepdims=True) + 1e-6)
        n = x * r
        nw = n * w_ref[...]
        sc1 = 1.0 + mod_ref[shift_row + 1:shift_row + 2, :]
        acc_ref[0:1, :] += _colsum(dh_v)
        acc_ref[1:2, :] += _colsum(dh_v * nw)
        dnw = dh_v * sc1
        acc_ref[2:3, :] += _colsum(dnw * n)
        dn = dnw * w_ref[...]
        dx = r * (dn - n * jnp.mean(dn * n, axis=-1, keepdims=True)) + dres_ref[...]
        dx_ref[...] = dx
        if has_mix:
            acc_ref[3:4, :] += _colsum(dx * mix_ref[...])
            dmix_ref[...] = (dx * mod_ref[gate_row:gate_row + 1, :]).astype(BF16)

    row = lambda width: pl.BlockSpec((TM, width), lambda i: (i, 0))
    ins, in_specs = [], []
    for a, wt, rb in dh_pairs:
        ins += [a, wt]
        in_specs += [row(a.shape[1]), _resident_spec((a.shape[1], D), (rb, 0))]
    ins += [xin, dres, mod, w] + ([mix] if has_mix else [])
    in_specs += [row(D), row(D), _const_spec((8, D)), _const_spec((1, D))] + ([row(D)] if has_mix else [])
    out_shape = [_sds((L, D), F32)] + ([_sds((L, D), BF16)] if has_mix else []) + [_sds((8, D), F32)]
    out_specs = [row(D)] + ([row(D)] if has_mix else []) + [_const_spec((8, D))]
    return pl.pallas_call(body, name=name, grid=(L // TM,), out_shape=out_shape, in_specs=in_specs,
                          out_specs=out_specs, compiler_params=_cparams(("arbitrary",)))(*ins)


def _halo(k):
    return 8 if k <= 9 else 32


def _prev_spec(h, col0):
    return pl.BlockSpec((h, CB), lambda j, i: (jnp.maximum(i * (TC // h) - 1, 0), j + col0))


def _next_spec(h, col0, n_tiles):
    return pl.BlockSpec((h, CB), lambda j, i: (jnp.minimum(i + 1, n_tiles - 1) * (TC // h), j + col0))


def _tile_spec(col0):
    return pl.BlockSpec((TC, CB), lambda j, i: (i, j + col0))


def _w_spec(kp, col0):
    return pl.BlockSpec((kp, CB), lambda j, i: (0, j + col0))


SUBLANES = 8


def _shifted_windows(v, taps, rows):
    for r in range(SUBLANES):
        group = [(o, k) for o, k in taps if o % SUBLANES == r]
        if not group:
            continue
        s = v if r == 0 else pltpu.roll(v, v.shape[0] - r, 0)
        for o, k in group:
            yield k, s[o - r:o - r + rows, :]


def _causal_taps(ext_ref, w_ref, k_taps, first, rows):
    acc = None
    for k, win in _shifted_windows(ext_ref[...], [(first - (k_taps - 1) + k, k) for k in range(k_taps)], rows):
        t = w_ref[k:k + 1, :] * win
        acc = t if acc is None else acc + t
    return acc


def _anticausal_taps(d_ref, w_ref, k_taps, rows):
    acc = None
    for k, win in _shifted_windows(d_ref[...], [(k_taps - 1 - k, k) for k in range(k_taps)], rows):
        t = w_ref[k:k + 1, :] * win
        acc = t if acc is None else acc + t
    return acc


def _acc_conv_wgrad(dw_ref, d_tile, ext_ref, k_taps, first):
    for k, win in _shifted_windows(ext_ref[...], [(first - (k_taps - 1) + k, k) for k in range(k_taps)], TC):
        dw_ref[k:k + 1, :] += _colsum(d_tile * win)
    dw_ref[k_taps:k_taps + 1, :] += _colsum(d_tile)


def conv_silu_fwd(x, col0, width, w, b, name):
    L = x.shape[0]
    k_taps = w.shape[0]
    h = _halo(k_taps)

    def body(xp_ref, x_ref, w_ref, b_ref, o_ref, ext_ref):
        i = pl.program_id(1)
        ext_ref[0:h, :] = jnp.where(i > 0, xp_ref[...], 0.0)
        ext_ref[h:h + TC, :] = x_ref[...]
        o_ref[...] = _silu(_causal_taps(ext_ref, w_ref, k_taps, h, TC) + b_ref[...])

    return pl.pallas_call(
        body, name=name, grid=(width // CB, L // TC), out_shape=_sds((L, width), F32),
        in_specs=[_prev_spec(h, col0), _tile_spec(col0), _w_spec(k_taps, 0), pl.BlockSpec((1, CB), lambda j, i: (0, j))],
        out_specs=_tile_spec(0), scratch_shapes=[pltpu.VMEM((h + TC, CB), F32)],
        compiler_params=_cparams(("parallel", "parallel")))(x, x, w, b)


def conv_silu_bwd(x, col0, width, w, b, dpost, name):
    L = x.shape[0]
    k_taps = w.shape[0]
    h = _halo(k_taps)
    nt = L // TC

    def body(xp_ref, x_ref, xn_ref, d_ref, dn_ref, w_ref, b_ref, dx_ref, dw_ref, ext_ref, dpre_ref):
        i = pl.program_id(1)

        @pl.when(i == 0)
        def _():
            dw_ref[...] = jnp.zeros((8, CB), F32)

        ext_ref[0:h, :] = jnp.where(i > 0, xp_ref[...], 0.0)
        ext_ref[h:h + TC, :] = x_ref[...]
        ext_ref[h + TC:h + TC + h, :] = xn_ref[...]
        pre = _causal_taps(ext_ref, w_ref, k_taps, h, TC + h) + b_ref[...]
        dpre_ref[0:TC, :] = d_ref[...] * _dsilu(pre[0:TC, :])
        dpre_ref[TC:TC + h, :] = jnp.where(i < nt - 1, dn_ref[...], 0.0) * _dsilu(pre[TC:TC + h, :])
        dx_ref[...] = _anticausal_taps(dpre_ref, w_ref, k_taps, TC).astype(BF16)
        _acc_conv_wgrad(dw_ref, dpre_ref[0:TC, :], ext_ref, k_taps, h)

    return pl.pallas_call(
        body, name=name, grid=(width // CB, nt),
        out_shape=[_sds((L, width), BF16), _sds((8, width), F32)],
        in_specs=[_prev_spec(h, col0), _tile_spec(col0), _next_spec(h, col0, nt), _tile_spec(0), _next_spec(h, 0, nt),
                  _w_spec(k_taps, 0), pl.BlockSpec((1, CB), lambda j, i: (0, j))],
        out_specs=[_tile_spec(0), _w_spec(8, 0)],
        scratch_shapes=[pltpu.VMEM((h + TC + h, CB), F32), pltpu.VMEM((TC + h, CB), F32)],
        compiler_params=_cparams(("parallel", "arbitrary")))(x, x, x, dpost, dpost, w, b)


def conf_conv_fwd(proj, col_a, col_g, w, b, name):
    L = proj.shape[0]
    k_taps = w.shape[0]
    h = _halo(k_taps)

    def body(ap_ref, a_ref, gp_ref, g_ref, w_ref, b_ref, o_ref, ext_ref):
        i = pl.program_id(1)
        ext_ref[0:h, :] = jnp.where(i > 0, ap_ref[...] * _sigmoid(gp_ref[...]), 0.0)
        ext_ref[h:h + TC, :] = a_ref[...] * _sigmoid(g_ref[...])
        o_ref[...] = _causal_taps(ext_ref, w_ref, k_taps, h, TC) + b_ref[...]

    return pl.pallas_call(
        body, name=name, grid=(D_CONF // CB, L // TC), out_shape=_sds((L, D_CONF), F32),
        in_specs=[_prev_spec(h, col_a), _tile_spec(col_a), _prev_spec(h, col_g), _tile_spec(col_g), _w_spec(k_taps, 0),
                  pl.BlockSpec((1, CB), lambda j, i: (0, j))],
        out_specs=_tile_spec(0), scratch_shapes=[pltpu.VMEM((h + TC, CB), F32)],
        compiler_params=_cparams(("parallel", "parallel")))(proj, proj, proj, proj, w, b)


def conf_conv_bwd(proj, col_a, col_g, w, duc, name):
    L = proj.shape[0]
    k_taps = w.shape[0]
    h = _halo(k_taps)
    nt = L // TC

    def body(ap_ref, a_ref, gp_ref, g_ref, d_ref, dn_ref, w_ref, da_ref, dg_ref, dw_ref, ext_ref, dext_ref):
        i = pl.program_id(1)

        @pl.when(i == 0)
        def _():
            dw_ref[...] = jnp.zeros((32, CB), F32)

        a = a_ref[...]
        s = _sigmoid(g_ref[...])
        ext_ref[0:h, :] = jnp.where(i > 0, ap_ref[...] * _sigmoid(gp_ref[...]), 0.0)
        ext_ref[h:h + TC, :] = a * s
        dext_ref[0:TC, :] = d_ref[...]
        dext_ref[TC:TC + h, :] = jnp.where(i < nt - 1, dn_ref[...], 0.0)
        du0 = _anticausal_taps(dext_ref, w_ref, k_taps, TC)
        da_ref[...] = (du0 * s).astype(BF16)
        dg_ref[...] = (du0 * a * s * (1.0 - s)).astype(BF16)
        _acc_conv_wgrad(dw_ref, d_ref[...], ext_ref, k_taps, h)

    return pl.pallas_call(
        body, name=name, grid=(D_CONF // CB, nt),
        out_shape=[_sds((L, D_CONF), BF16), _sds((L, D_CONF), BF16), _sds((32, D_CONF), F32)],
        in_specs=[_prev_spec(h, col_a), _tile_spec(col_a), _prev_spec(h, col_g), _tile_spec(col_g), _tile_spec(0),
                  _next_spec(h, 0, nt), _w_spec(k_taps, 0)],
        out_specs=[_tile_spec(0), _tile_spec(0), _w_spec(32, 0)],
        scratch_shapes=[pltpu.VMEM((h + TC, CB), F32), pltpu.VMEM((TC + h, CB), F32)],
        compiler_params=_cparams(("parallel", "arbitrary")))(proj, proj, proj, proj, duc, duc, w)


def ffn_conv_fwd(up, w, b, name):
    L = up.shape[0]
    k_taps = w.shape[0]
    h = _halo(k_taps)
    cv = D_FF // CB

    def body(gp_ref, g_ref, vp_ref, v_ref, wg_ref, wv_ref, bg_ref, bv_ref, o_ref, eg_ref, ev_ref):
        i = pl.program_id(1)
        eg_ref[0:h, :] = jnp.where(i > 0, gp_ref[...], 0.0)
        eg_ref[h:h + TC, :] = g_ref[...]
        ev_ref[0:h, :] = jnp.where(i > 0, vp_ref[...], 0.0)
        ev_ref[h:h + TC, :] = v_ref[...]
        pg = _causal_taps(eg_ref, wg_ref, k_taps, h, TC) + bg_ref[...]
        pv = _causal_taps(ev_ref, wv_ref, k_taps, h, TC) + bv_ref[...]
        o_ref[...] = (_silu(pg) * pv).astype(BF16)

    bspec = lambda c0: pl.BlockSpec((1, CB), lambda j, i: (0, j + c0))
    return pl.pallas_call(
        body, name=name, grid=(cv, L // TC), out_shape=_sds((L, D_FF), BF16),
        in_specs=[_prev_spec(h, 0), _tile_spec(0), _prev_spec(h, cv), _tile_spec(cv), _w_spec(k_taps, 0), _w_spec(k_taps, cv),
                  bspec(0), bspec(cv)],
        out_specs=_tile_spec(0), scratch_shapes=[pltpu.VMEM((h + TC, CB), F32), pltpu.VMEM((h + TC, CB), F32)],
        compiler_params=_cparams(("parallel", "parallel")))(up, up, up, up, w, w, b, b)


def ffn_conv_bwd(up, w, b, dact, name):
    L = up.shape[0]
    k_taps = w.shape[0]
    h = _halo(k_taps)
    nt = L // TC
    cv = D_FF // CB

    def body(gp_ref, g_ref, gn_ref, vp_ref, v_ref, vn_ref, d_ref, dn_ref, wg_ref, wv_ref, bg_ref, bv_ref,
             dg_ref, dv_ref, dwg_ref, dwv_ref, eg_ref, ev_ref, pg_ref, pv_ref):
        i = pl.program_id(1)

        @pl.when(i == 0)
        def _():
            dwg_ref[...] = jnp.zeros((8, CB), F32)
            dwv_ref[...] = jnp.zeros((8, CB), F32)

        for e_ref, p_ref, c_ref, n_ref in ((eg_ref, gp_ref, g_ref, gn_ref), (ev_ref, vp_ref, v_ref, vn_ref)):
            e_ref[0:h, :] = jnp.where(i > 0, p_ref[...], 0.0)
            e_ref[h:h + TC, :] = c_ref[...]
            e_ref[h + TC:h + TC + h, :] = n_ref[...]
        pg = _causal_taps(eg_ref, wg_ref, k_taps, h, TC + h) + bg_ref[...]
        pv = _causal_taps(ev_ref, wv_ref, k_taps, h, TC + h) + bv_ref[...]
        dact_t = d_ref[...]
        dact_n = jnp.where(i < nt - 1, dn_ref[...], 0.0)
        pg_ref[0:TC, :] = dact_t * pv[0:TC, :] * _dsilu(pg[0:TC, :])
        pg_ref[TC:TC + h, :] = dact_n * pv[TC:TC + h, :] * _dsilu(pg[TC:TC + h, :])
        pv_ref[0:TC, :] = dact_t * _silu(pg[0:TC, :])
        pv_ref[TC:TC + h, :] = dact_n * _silu(pg[TC:TC + h, :])
        dg_ref[...] = _anticausal_taps(pg_ref, wg_ref, k_taps, TC).astype(BF16)
        dv_ref[...] = _anticausal_taps(pv_ref, wv_ref, k_taps, TC).astype(BF16)
        _acc_conv_wgrad(dwg_ref, pg_ref[0:TC, :], eg_ref, k_taps, h)
        _acc_conv_wgrad(dwv_ref, pv_ref[0:TC, :], ev_ref, k_taps, h)

    bspec = lambda c0: pl.BlockSpec((1, CB), lambda j, i: (0, j + c0))
    ext = pltpu.VMEM((h + TC + h, CB), F32)
    dpre = pltpu.VMEM((TC + h, CB), F32)
    return pl.pallas_call(
        body, name=name, grid=(cv, nt),
        out_shape=[_sds((L, D_FF), BF16), _sds((L, D_FF), BF16), _sds((8, D_FF), F32), _sds((8, D_FF), F32)],
        in_specs=[_prev_spec(h, 0), _tile_spec(0), _next_spec(h, 0, nt), _prev_spec(h, cv), _tile_spec(cv), _next_spec(h, cv, nt),
                  _tile_spec(0), _next_spec(h, 0, nt), _w_spec(k_taps, 0), _w_spec(k_taps, cv), bspec(0), bspec(cv)],
        out_specs=[_tile_spec(0), _tile_spec(0), _w_spec(8, 0), _w_spec(8, 0)],
        scratch_shapes=[ext, ext, dpre, dpre],
        compiler_params=_cparams(("parallel", "arbitrary")))(up, up, up, up, up, up, dact, dact, w, w, b, b)


def _ssd_common(xbc_ref, dt_ref, dtb_ref, alog_ref, cs_ref):
    xs = xbc_ref[:, 0:D_SSD]
    sp_in = dt_ref[...] + dtb_ref[...]
    dtf = _softplus(sp_in)
    a_f = -jnp.exp(alog_ref[...])
    a_dt = dtf * a_f
    row = lax.broadcasted_iota(jnp.int32, (Q, Q), 0)
    col = lax.broadcasted_iota(jnp.int32, (Q, Q), 1)
    causal = row >= col
    cs = _dot_exact(causal.astype(F32), a_dt, 3, "a")
    cs_ref[...] = cs
    cs_last = cs_ref[Q - 1:Q, :]
    return xs, sp_in, dtf, a_f, cs, cs_last, causal


def _head_decay(cs_j, cst_ref, e, causal):
    lane = lax.broadcasted_iota(jnp.int32, (Q, LANE), 1)
    rolled = pltpu.roll(cs_j, HEAD, 1)
    own = (lane < HEAD) if e == 0 else (lane >= HEAD)
    col_b = jnp.where(own, cs_j, rolled)
    col_b = jnp.concatenate([col_b] * (Q // LANE), axis=1)
    row_b = cst_ref[e * HEAD:e * HEAD + 1, :]
    return jnp.where(causal, jnp.exp(jnp.minimum(col_b - row_b, 0.0)), 0.0)


def ssd_fwd(xbc, z_src, dt_src, dtb_f, alog_f, dsk_f, snw, name):
    L = xbc.shape[0]
    nc = L // Q

    def body(xbc_ref, z_ref, dt_ref, dtb_ref, alog_ref, dsk_ref, snw_ref, y_ref, yn_ref, sp_ref, s_ref, cs_ref, cst_ref, yd_ref):
        @pl.when(pl.program_id(0) == 0)
        def _():
            s_ref[...] = jnp.zeros((N_STATE, D_SSD), F32)

        xs, _, dtf, a_f, cs, cs_last, causal = _ssd_common(xbc_ref, dt_ref, dtb_ref, alog_ref, cs_ref)
        e_cs = jnp.exp(cs)
        xdt = xs * dtf
        zst = jnp.exp(cs_last - cs) * xdt
        sp_ref[0] = s_ref[...]
        lane = lax.broadcasted_iota(jnp.int32, (Q, LANE), 1)
        for g in range(2):
            gl = slice(g * 512, g * 512 + 512)
            b_g = xbc_ref[:, D_SSD + g * N_STATE:D_SSD + (g + 1) * N_STATE]
            c_g = xbc_ref[:, D_SSD + 2 * N_STATE + g * N_STATE:D_SSD + 2 * N_STATE + (g + 1) * N_STATE]
            s_prev = s_ref[:, gl]
            cb = _dot_nt(c_g, b_g)
            yd_ref[:, gl] = e_cs[:, gl] * _dot(c_g, s_prev)
            for j in range(4):
                tl = slice(g * 512 + j * LANE, g * 512 + (j + 1) * LANE)
                cs_j = cs[:, tl]
                cst_ref[...] = cs_j.T
                x_j = xdt[:, tl]
                o0 = _dot(cb * _head_decay(cs_j, cst_ref, 0, causal), x_j)
                o1 = _dot(cb * _head_decay(cs_j, cst_ref, 1, causal), x_j)
                yd_ref[:, tl] += jnp.where(lane < HEAD, o0, o1)
            s_ref[:, gl] = jnp.exp(cs_last[:, gl]) * s_prev + _dot_tn(b_g, zst[:, gl])
        y = yd_ref[...] + xs * dsk_ref[...]
        y_ref[...] = y
        yz = y * _silu(z_ref[...])
        r = lax.rsqrt(jnp.mean(yz * yz, axis=-1, keepdims=True) + 1e-6)
        yn_ref[...] = (yz * r * snw_ref[...]).astype(BF16)

    chunk = lambda w, c: pl.BlockSpec((Q, w), lambda i: (i, c))
    return pl.pallas_call(
        body, name=name, grid=(nc,),
        out_shape=[_sds((L, D_SSD), F32), _sds((L, D_SSD), BF16), _sds((nc, N_STATE, D_SSD), F32)],
        in_specs=[chunk(D_XBC, 0), chunk(D, 0), chunk(D, 0)] + [_const_spec((1, D))] * 4,
        out_specs=[chunk(D, 0), chunk(D, 0), pl.BlockSpec((1, N_STATE, D_SSD), lambda i: (i, 0, 0))],
        scratch_shapes=[pltpu.VMEM((N_STATE, D_SSD), F32), pltpu.VMEM((Q, D_SSD), F32), pltpu.VMEM((LANE, Q), F32),
                        pltpu.VMEM((Q, D_SSD), F32)],
        compiler_params=_cparams(("arbitrary",)))(xbc, z_src, dt_src, dtb_f, alog_f, dsk_f, snw)


def ssd_bwd(dysn, y, xbc, z_src, dt_src, s_prev_all, dtb_f, alog_f, dsk_f, snw, name):
    L = xbc.shape[0]
    nc = L // Q

    def body(dyn_ref, y_ref, xbc_ref, z_ref, dt_ref, sp_ref, dtb_ref, alog_ref, dsk_ref, snw_ref,
             dz_ref, ddt_ref, dxbc_ref, acc_ref, acc16_ref, ds_ref, cs_ref, cst_ref, dcs_ref, dx_ref):
        step = pl.program_id(0)

        @pl.when(step == 0)
        def _():
            ds_ref[...] = jnp.zeros((N_STATE, D_SSD), F32)
            acc_ref[...] = jnp.zeros((8, D), F32)

        z = z_ref[...]
        y = y_ref[...]
        sz = _sigmoid(z)
        siluz = z * sz
        yz = y * siluz
        r = lax.rsqrt(jnp.mean(yz * yz, axis=-1, keepdims=True) + 1e-6)
        n = yz * r
        dyn = dyn_ref[...]
        acc_ref[0:1, :] += _colsum(dyn * n)
        dn = dyn * snw_ref[...]
        dyz = r * (dn - n * jnp.mean(dn * n, axis=-1, keepdims=True))
        dy = dyz * siluz
        dz_ref[...] = (dyz * y * (sz * (1.0 + z * (1.0 - sz)))).astype(BF16)

        xs, sp_in, dtf, a_f, cs, cs_last, causal = _ssd_common(xbc_ref, dt_ref, dtb_ref, alog_ref, cs_ref)
        acc_ref[3:4, :] += _colsum(dy * xs)
        e_cs = jnp.exp(cs)
        xdt = xs * dtf
        dst = jnp.exp(cs_last - cs)
        zst = dst * xdt
        e_last = jnp.exp(cs_last)
        lane = lax.broadcasted_iota(jnp.int32, (Q, LANE), 1)
        ones = jnp.ones((Q, LANE), F32)
        dcs_last_parts = []
        for g in range(2):
            gl = slice(g * 512, g * 512 + 512)
            b_g = xbc_ref[:, D_SSD + g * N_STATE:D_SSD + (g + 1) * N_STATE]
            c_g = xbc_ref[:, D_SSD + 2 * N_STATE + g * N_STATE:D_SSD + 2 * N_STATE + (g + 1) * N_STATE]
            s_prev = sp_ref[0, :, gl]
            ds_g = ds_ref[:, gl]
            dy_g = dy[:, gl]
            cb = _dot_nt(c_g, b_g)
            y_off = e_cs[:, gl] * _dot(c_g, s_prev)
            edy = e_cs[:, gl] * dy_g
            d_c = _dot_nt(edy, s_prev)
            d_z = _dot(b_g, ds_g)
            d_b = _dot_nt(zst[:, gl], ds_g)
            t_g = d_z * zst[:, gl]
            dcs_ref[:, gl] = dy_g * y_off - t_g
            dx_ref[:, gl] = d_z * dst[:, gl]
            dcs_last_parts.append(_colsum(t_g) + _colsum(ds_g * s_prev) * e_last[:, gl])
            ds_ref[:, gl] = e_last[:, gl] * ds_g + _dot_tn(c_g, edy)
            dcb = jnp.zeros((Q, Q), F32)
            for j in range(4):
                tl = slice(g * 512 + j * LANE, g * 512 + (j + 1) * LANE)
                cs_j = cs[:, tl]
                cst_ref[...] = cs_j.T
                x_j = xdt[:, tl]
                dy_j = dy[:, tl]
                dx_j = jnp.zeros((Q, LANE), F32)
                dcs_j = jnp.zeros((Q, LANE), F32)
                for e in range(2):
                    own = (lane < HEAD) if e == 0 else (lane >= HEAD)
                    w_h = _head_decay(cs_j, cst_ref, e, causal)
                    g_h = cb * w_h
                    dy_m = jnp.where(own, dy_j, 0.0)
                    d_g = _dot_nt(dy_m, x_j)
                    dx_j = dx_j + _dot_tn(g_h, dy_m)
                    dcb = dcb + d_g * w_h
                    p_h = d_g * g_h
                    row_sums = _dot_exact(p_h, ones, 2, "b")
                    col_sums = _dot_exact(p_h, ones, 2, "b", (((0,), (0,)), ((), ())))
                    dcs_j = dcs_j + jnp.where(own, row_sums - col_sums, 0.0)
                dcs_ref[:, tl] += dcs_j * (1.0 / HEAD)
                dx_ref[:, tl] += dx_j
            d_c = d_c + _dot(dcb, b_g)
            d_b = d_b + _dot_tn(dcb, c_g)
            dxbc_ref[:, D_SSD + g * N_STATE:D_SSD + (g + 1) * N_STATE] = d_b
            dxbc_ref[:, D_SSD + 2 * N_STATE + g * N_STATE:D_SSD + 2 * N_STATE + (g + 1) * N_STATE] = d_c
        dcs_last = jnp.concatenate(dcs_last_parts, axis=1)
        anticausal = lax.broadcasted_iota(jnp.int32, (Q, Q), 0) <= lax.broadcasted_iota(jnp.int32, (Q, Q), 1)
        d_adt = _dot_exact(anticausal.astype(F32), dcs_ref[...], 3, "a") + dcs_last
        dx = dx_ref[...]
        acc_ref[2:3, :] += _colsum(d_adt * dtf) * a_f
        d_dtf = d_adt * a_f + dx * xs
        dxbc_ref[:, 0:D_SSD] = dx * dtf + dy * dsk_ref[...]
        d_raw = d_dtf * _sigmoid(sp_in)
        acc_ref[1:2, :] += _colsum(d_raw)
        head_of_lane = lax.broadcasted_iota(jnp.int32, (D_SSD, LANE), 0) // HEAD
        fold = (head_of_lane == lax.broadcasted_iota(jnp.int32, (D_SSD, LANE), 1)).astype(F32)
        ddt_ref[...] = _dot_exact(d_raw, fold, 2, "b").astype(BF16)

        @pl.when(step == nc - 1)
        def _():
            acc16_ref[...] = _dot_exact(acc_ref[...], fold, 3, "b")

    rchunk = lambda w, c: pl.BlockSpec((Q, w), lambda i: (nc - 1 - i, c))
    return pl.pallas_call(
        body, name=name, grid=(nc,),
        out_shape=[_sds((L, D_SSD), BF16), _sds((L, LANE), BF16), _sds((L, D_XBC), F32), _sds((8, D), F32), _sds((8, LANE), F32)],
        in_specs=[rchunk(D, 0), rchunk(D, 0), rchunk(D_XBC, 0), rchunk(D, 0), rchunk(D, 0),
                  pl.BlockSpec((1, N_STATE, D_SSD), lambda i: (nc - 1 - i, 0, 0))] + [_const_spec((1, D))] * 4,
        out_specs=[rchunk(D, 0), rchunk(LANE, 0), rchunk(D_XBC, 0), _const_spec((8, D)), _const_spec((8, LANE))],
        scratch_shapes=[pltpu.VMEM((N_STATE, D_SSD), F32), pltpu.VMEM((Q, D_SSD), F32), pltpu.VMEM((LANE, Q), F32),
                        pltpu.VMEM((Q, D_SSD), F32), pltpu.VMEM((Q, D_SSD), F32)],
        compiler_params=_cparams(("arbitrary",)))(dysn, y, xbc, z_src, dt_src, s_prev_all, dtb_f, alog_f, dsk_f, snw)


def _adamw_math(w, g, m, v):
    m_n = ADAM_B1 * m + (1.0 - ADAM_B1) * g
    v_n = ADAM_B2 * v + (1.0 - ADAM_B2) * jnp.square(g)
    c1 = 1.0 - ADAM_B1 ** ADAM_STEP
    c2 = 1.0 - ADAM_B2 ** ADAM_STEP
    return -ADAM_LR * ((m_n / c1) / (jnp.sqrt(v_n / c2) + ADAM_EPS) + ADAM_WD * w), m_n, v_n


def _sum_slots(p_ref):
    acc = p_ref[0].astype(F32)
    for s in range(1, p_ref.shape[0]):
        acc = acc + p_ref[s].astype(F32)
    return acc


def adamw_slots(w, slots, m, v, name):
    rows, cols = w.shape
    tc = 256

    def body(w_ref, s_ref, m_ref, v_ref, g_ref, d_ref, mo_ref, vo_ref):
        g_v = _sum_slots(s_ref)
        g_ref[...] = g_v
        d_ref[...], mo_ref[...], vo_ref[...] = _adamw_math(w_ref[...], g_v, m_ref[...], v_ref[...])

    spec = pl.BlockSpec((rows, tc), lambda i: (0, i))
    return pl.pallas_call(body, name=name, grid=(cols // tc,), out_shape=[_sds((rows, cols), F32)] * 4,
                          in_specs=[spec, pl.BlockSpec((slots.shape[0], rows, tc), lambda i: (0, 0, i)), spec, spec], out_specs=[spec] * 4,
                          compiler_params=_cparams(("parallel",)))(w, slots, m, v)


def adamw_many(ws, gs, ms, vs, name):
    n = len(ws)

    def body(*refs):
        for p in range(n):
            d_v, m_v, v_v = _adamw_math(refs[p][...], refs[n + p][...], refs[2 * n + p][...], refs[3 * n + p][...])
            refs[4 * n + p][...] = d_v
            refs[5 * n + p][...] = m_v
            refs[6 * n + p][...] = v_v

    vm = pl.BlockSpec(memory_space=pltpu.VMEM)
    out = pl.pallas_call(body, name=name, out_shape=[_sds(w.shape, F32) for w in ws] * 3, in_specs=[vm] * (4 * n),
                         out_specs=[vm] * (3 * n), compiler_params=_cparams())(*ws, *gs, *ms, *vs)
    return out[:n], out[n:2 * n], out[2 * n:]


def _pack_layout(shapes):
    row, layout = 0, []
    for rows, cols in shapes:
        chunks = []
        for c0 in range(0, cols, D):
            chunks.append((row, c0, min(D, cols - c0)))
            row += rows
        layout.append(chunks)
    return row, layout


def pack_rows(entries, name):
    arrays = [e[0] for e in entries]
    used, layout = _pack_layout([(e[2], e[0].shape[1]) for e in entries])
    total = -(-used // SUBLANES) * SUBLANES
    n = len(arrays)

    def body(*refs):
        o_ref = refs[n]
        o_ref[...] = jnp.zeros((total, D), F32)
        for p in range(n):
            _, first, rows = entries[p]
            for r0, c0, w in layout[p]:
                o_ref[r0:r0 + rows, 0:w] = refs[p][first:first + rows, c0:c0 + w]

    vm = pl.BlockSpec(memory_space=pltpu.VMEM)
    return pl.pallas_call(body, name=name, out_shape=_sds((total, D), F32), in_specs=[vm] * n, out_specs=vm,
                          compiler_params=_cparams())(*arrays)


def unpack_rows(packed, shapes):
    _, layout = _pack_layout(shapes)
    out = []
    for (rows, _), chunks in zip(shapes, layout):
        parts = [packed[..., r0:r0 + rows, 0:w] for r0, _, w in chunks]
        out.append(parts[0] if len(parts) == 1 else jnp.concatenate(parts, axis=-1))
    return out


def sum_slots_many(parts, name):
    n = len(parts)

    def body(*refs):
        for p in range(n):
            refs[n + p][...] = _sum_slots(refs[p])

    vm = pl.BlockSpec(memory_space=pltpu.VMEM)
    return pl.pallas_call(body, name=name, out_shape=[_sds(p.shape[1:], F32) for p in parts], in_specs=[vm] * n,
                          out_specs=[vm] * n, compiler_params=_cparams())(*parts)


def ada_mod(c_all, ada_w_shard, ada_b_cols, name):
    def body(c_ref, w_ref, b_ref, o_ref, ca_ref):
        ca = _silu(c_ref[...])
        ca_ref[...] = ca
        o_ref[...] = _dot(ca, w_ref[...]) + b_ref[...]

    vm = pl.BlockSpec(memory_space=pltpu.VMEM)
    return pl.pallas_call(body, name=name, out_shape=[_sds((N_DEV, ada_w_shard.shape[1]), F32), _sds((N_DEV, D), F32)],
                          in_specs=[vm, vm, vm], out_specs=[vm, vm], compiler_params=_cparams())(c_all, ada_w_shard, ada_b_cols)


def ada_wgrad(c_act_all, dmod_cols, name):
    def body(c_ref, d_ref, o_ref):
        o_ref[...] = _dot_tn_hi(c_ref[...], d_ref[...])

    vm = pl.BlockSpec(memory_space=pltpu.VMEM)
    return pl.pallas_call(body, name=name, out_shape=_sds((D, dmod_cols.shape[1]), F32), in_specs=[vm, vm], out_specs=vm,
                          compiler_params=_cparams())(c_act_all, dmod_cols)


def exchange(srcs, name, gather):
    n = len(srcs)
    gathers = [gather] * n if isinstance(gather, bool) else list(gather)
    shapes = [tuple(s.shape) if g else tuple(s.shape[1:]) for s, g in zip(srcs, gathers)]

    def body(*refs):
        src_refs, out_refs = refs[:n], refs[n:2 * n]
        send_sems, recv_sems, local_sems = refs[2 * n:]
        x, y, c = lax.axis_index("x"), lax.axis_index("y"), lax.axis_index("c")
        me = 4 * x + 2 * y + c

        def peer(k):
            bx, by, bc = (k >> 2) & 1, (k >> 1) & 1, k & 1
            px, py, pc = (x + bx) % 2, (y + by) % 2, (c + bc) % 2
            return (px, py, pc), 4 * px + 2 * py + pc

        def copy(a, k, landing):
            dev, idx = peer(k)
            return pltpu.make_async_remote_copy(
                src_ref=src_refs[a] if gathers[a] else src_refs[a].at[idx], dst_ref=out_refs[a].at[idx if landing else me],
                send_sem=send_sems.at[a, k - 1], recv_sem=recv_sems.at[a, k - 1],
                device_id=dev, device_id_type=pl.DeviceIdType.MESH)

        mine = [pltpu.make_async_copy(src_refs[a] if gathers[a] else src_refs[a].at[me], out_refs[a].at[me], local_sems.at[a])
                for a in range(n)]
        for cp in mine:
            cp.start()
        sends = [copy(a, k, False) for a in range(n) for k in range(1, N_DEV)]
        for cp in sends:
            cp.start()
        for a in range(n):
            for k in range(1, N_DEV):
                copy(a, k, True).wait_recv()
        for cp in sends:
            cp.wait_send()
        for cp in mine:
            cp.wait()

    hbm = pl.BlockSpec(memory_space=pl.ANY)
    return pl.pallas_call(
        body, name=name, out_shape=[_sds((N_DEV,) + shp, s.dtype) for shp, s in zip(shapes, srcs)], in_specs=[hbm] * n,
        out_specs=[hbm] * n,
        scratch_shapes=[pltpu.SemaphoreType.DMA((n, N_DEV - 1)), pltpu.SemaphoreType.DMA((n, N_DEV - 1)),
                        pltpu.SemaphoreType.DMA((n,))],
        compiler_params=pltpu.CompilerParams(has_side_effects=True))(*srcs)


def gather_two_level(srcs, name):
    n = len(srcs)

    def body(*refs):
        src_refs, out_refs = refs[:n], refs[n:2 * n]
        send_sems, recv_sems, local_sems = refs[2 * n:]
        x, y, c = lax.axis_index("x"), lax.axis_index("y"), lax.axis_index("c")
        me, sibling = (x, y, c), (x, y, 1 - c)
        chips = [(1 - x, y), (x, 1 - y), (1 - x, 1 - y)]

        def slot(a, px, py, pc):
            return out_refs[a].at[4 * px + 2 * py + pc]

        def copy(a, k, block, to, src=None):
            return pltpu.make_async_remote_copy(
                src_ref=slot(a, *block) if src is None else src, dst_ref=slot(a, *block), send_sem=send_sems.at[a, k],
                recv_sem=recv_sems.at[a, k], device_id=to, device_id_type=pl.DeviceIdType.MESH)

        mine = [pltpu.make_async_copy(src_refs[a], slot(a, *me), local_sems.at[a]) for a in range(n)]
        for cp in mine:
            cp.start()
        first = []
        for a in range(n):
            first += [copy(a, 0, me, sibling, src=src_refs[a])]
            first += [copy(a, 1 + j, me, (*chip, c), src=src_refs[a]) for j, chip in enumerate(chips)]
        for cp in first:
            cp.start()
        passed = []
        for a in range(n):
            for j, chip in enumerate(chips):
                copy(a, 1 + j, (*chip, c), me).wait_recv()
                passed.append(copy(a, 4 + j, (*chip, c), sibling))
                passed[-1].start()
        for a in range(n):
            copy(a, 0, sibling, me).wait_recv()
            for j, chip in enumerate(chips):
                copy(a, 4 + j, (*chip, 1 - c), me).wait_recv()
        for cp in first + passed:
            cp.wait_send()
        for cp in mine:
            cp.wait()

    hbm = pl.BlockSpec(memory_space=pl.ANY)
    return pl.pallas_call(
        body, name=name, out_shape=[_sds((N_DEV,) + tuple(s.shape), s.dtype) for s in srcs], in_specs=[hbm] * n,
        out_specs=[hbm] * n,
        scratch_shapes=[pltpu.SemaphoreType.DMA((n, N_DEV - 1)), pltpu.SemaphoreType.DMA((n, N_DEV - 1)),
                        pltpu.SemaphoreType.DMA((n,))],
        compiler_params=pltpu.CompilerParams(has_side_effects=True))(*srcs)


def _peer(k):
    x, y, c = lax.axis_index("x"), lax.axis_index("y"), lax.axis_index("c")
    px, py, pc = (x + ((k >> 2) & 1)) % 2, (y + ((k >> 1) & 1)) % 2, (c + (k & 1)) % 2
    return (px, py, pc), 4 * px + 2 * py + pc


def _my_slot():
    return 4 * lax.axis_index("x") + 2 * lax.axis_index("y") + lax.axis_index("c")


_HBM = pl.BlockSpec(memory_space=pltpu.HBM)
_SEM = pl.BlockSpec(memory_space=pltpu.SEMAPHORE)
_EFFECT = pltpu.SideEffectType.DATAFLOW_SIDE_EFFECTING


def exchange_start(srcs, name, gather):
    n = len(srcs)
    shapes = [tuple(s.shape) if gather else tuple(s.shape[1:]) for s in srcs]
    lands = [lax.empty((N_DEV,) + shp, s.dtype) for shp, s in zip(shapes, srcs)]

    def body(*refs):
        src_refs, land_refs = refs[:n], refs[n:2 * n]
        sems = refs[2 * n:4 * n]
        token = refs[-1]
        me = _my_slot()
        for a in range(n):
            for k in range(1, N_DEV):
                dev, idx = _peer(k)
                pltpu.make_async_remote_copy(
                    src_ref=src_refs[a] if gather else src_refs[a].at[idx], dst_ref=land_refs[a].at[me],
                    send_sem=sems[2 * a].at[k - 1], recv_sem=sems[2 * a + 1].at[k - 1],
                    device_id=dev, device_id_type=pl.DeviceIdType.MESH).start()
        token[...] = jnp.zeros_like(token)

    out_shape = ([pltpu.SemaphoreType.DMA((N_DEV - 1,))] * (2 * n) + [pltpu.HBM(s.shape, s.dtype) for s in srcs]
                 + [pltpu.HBM(l.shape, l.dtype) for l in lands] + [_sds((8, LANE), F32)])
    out = pl.pallas_call(
        body, name=name, out_shape=out_shape, in_specs=[_HBM] * (2 * n),
        out_specs=[_SEM] * (2 * n) + [_HBM] * (2 * n) + [pl.BlockSpec(memory_space=pltpu.VMEM)],
        input_output_aliases={i: 2 * n + i for i in range(2 * n)},
        compiler_params=pltpu.CompilerParams(has_side_effects=_EFFECT))(
            *[pltpu.with_memory_space_constraint(s, pltpu.HBM) for s in srcs],
            *[pltpu.with_memory_space_constraint(l, pltpu.HBM) for l in lands])
    parts = [(out[2 * a], out[2 * a + 1], out[2 * n + a], out[3 * n + a]) for a in range(n)]
    return parts, out[-1]


def exchange_wait(parts, after, name, gather):
    n = len(parts)

    def body(*refs):
        src_refs, land_refs = refs[:n], refs[n:2 * n]
        sems = refs[2 * n:4 * n]
        for a in range(n):
            for k in range(1, N_DEV):
                dev, idx = _peer(k)
                copy = pltpu.make_async_remote_copy(
                    src_ref=src_refs[a] if gather else src_refs[a].at[idx], dst_ref=land_refs[a].at[idx],
                    send_sem=sems[2 * a].at[k - 1], recv_sem=sems[2 * a + 1].at[k - 1],
                    device_id=dev, device_id_type=pl.DeviceIdType.MESH)
                copy.wait_send()
                copy.wait_recv()

    srcs = [p[2] for p in parts]
    lands = [p[3] for p in parts]
    sems = [s for p in parts for s in p[:2]]
    out = pl.pallas_call(
        body, name=name, out_shape=[pltpu.HBM(a.shape, a.dtype) for a in srcs + lands],
        in_specs=[_HBM] * (2 * n) + [_SEM] * (2 * n) + [pl.BlockSpec(memory_space=pl.ANY)], out_specs=[_HBM] * (2 * n),
        input_output_aliases={i: i for i in range(2 * n)},
        compiler_params=pltpu.CompilerParams(has_side_effects=_EFFECT))(*srcs, *lands, *sems, after)
    return list(zip(out[:n], out[n:]))


def _cols_to_slabs(g):
    r, c = g.shape
    return g.reshape(r, N_DEV, c // N_DEV).transpose(1, 0, 2)


def _slabs_to_cols(s):
    _, r, cs = s.shape
    return s.transpose(1, 0, 2).reshape(r, N_DEV * cs)


def _rep_heads(v):
    return jnp.repeat(v.reshape(N_HEADS), HEAD).reshape(1, D_SSD)


def local_fwd_bwd(x, target, mod, get_w, put_grad, small):
    n1w, n2w, fnw = small["norm1_w"], small["norm2_w"], small["final_norm_w"]
    dtb_f, alog_f, dsk_f = _rep_heads(small["dt_bias"]), _rep_heads(small["a_log"]), _rep_heads(small["d_skip"])
    snw = small["ssd_norm_w"]

    def after(v, token):
        return v + token[0:1, 0:1]

    w_in = get_w("w_in", mod)
    h1, proj_zx, proj_dt, proj_cf = in_proj(x, mod, n1w, w_in["w_full"], w_in["w_dt_rep"], w_in["w_cf"], "norm1_in_proj")
    xbc = conv_silu_fwd(proj_zx, D_SSD // CB, D_XBC, small["ssd_conv_w"], small["ssd_conv_b"], "ssd_conv")
    y, ysn, s_prev = ssd_fwd(xbc, proj_zx, proj_dt, dtb_f, alog_f, dsk_f, snw, "ssd_scan")
    uc = conf_conv_fwd(proj_cf, 0, D_CONF // CB, small["conf_conv_w"], small["conf_conv_b"], "conf_conv")
    w_out = get_w("w_out", uc)
    mix, u, x1, h2 = mixer_out(ysn, uc, small["conf_ln_w"], small["conf_ln_b"], w_out, x, mod, n2w, "out_proj_norm2")
    w_up_t = get_w("w_up", h2)
    up = mm_nt([(h2, w_up_t, 0)], "up_proj")
    act = ffn_conv_fwd(up, small["ffn_conv_w"], small["ffn_conv_b"], "ffn_conv")
    w_down = get_w("w_down", act)
    dx2, dff, dact, acc_f = final_loss(act, w_down, x1, mod, fnw, target, "down_proj_loss")

    token = put_grad("w_down", mm_tn(act, dff, "wgrad_down"))
    dupg, dupv, dwg, dwv = ffn_conv_bwd(up, small["ffn_conv_w"], after(small["ffn_conv_b"], token), dact, "ffn_conv_bwd")
    token = put_grad("w_up", mm_tn_stack([dupg, dupv], h2, "wgrad_up"))
    dx1, dmix, acc_2 = norm_mod_bwd([(dupg, w_up_t, 0), (dupv, w_up_t, 1)], x1, dx2, mod, after(n2w, token), 3, "norm2_bwd",
                                    mix=mix, gate_row=2)

    token = put_grad("w_out", mm_tn_stack([ysn, u], dmix, "wgrad_out"))
    dysn, duc, acc_ln = mixer_out_bwd(dmix, w_out, uc, after(small["conf_ln_w"], token), small["conf_ln_b"], "out_proj_bwd")
    dcfa, dcfg, dw_cc = conf_conv_bwd(proj_cf, 0, D_CONF // CB, small["conf_conv_w"], duc, "conf_conv_bwd")
    dz, ddt, dxbc_post, acc_s, acc_s16 = ssd_bwd(dysn, y, xbc, proj_zx, proj_dt, s_prev, dtb_f, alog_f, dsk_f, snw,
                                                 "ssd_scan_bwd")
    dxbc, dw_sc = conv_silu_bwd(proj_zx, D_SSD // CB, D_XBC, small["ssd_conv_w"], small["ssd_conv_b"], dxbc_post, "ssd_conv_bwd")
    token = put_grad("w_in", mm_tn_concat(
        [(dz, D_SSD), (dxbc, D_XBC), (ddt, N_HEADS), (dcfa, D_CONF), (dcfg, D_CONF)], h1, "wgrad_in"))
    dh1_pairs = [(dz, w_in["w_full"], 0), (ddt, w_in["w_dt16"], 0), (dcfa, w_in["w_cf"], 0), (dcfg, w_in["w_cf"], 1),
                 (dxbc, w_in["w_xbc"], 0)]
    grad_x, acc_1 = norm_mod_bwd(dh1_pairs, x, dx1, mod, after(n1w, token), 0, "norm1_bwd")

    small_accs = dict(acc_1=acc_1, acc_2=acc_2, acc_f=acc_f, acc_ln=acc_ln, acc_s=acc_s, acc_s16=acc_s16, dw_sc=dw_sc,
                      dw_cc=dw_cc, dwg=dwg, dwv=dwv)
    return grad_x, small_accs


def kernel(x, c, ada_w, ada_b, norm1_w, w_in, ssd_conv_w, ssd_conv_b, dt_bias, a_log, d_skip, ssd_norm_w, conf_conv_w, conf_conv_b, conf_ln_w, conf_ln_b, w_out, norm2_w, w_up, ffn_conv_w, ffn_conv_b, w_down, final_norm_w, loss_target, m_ada_w, m_ada_b, m_norm1_w, m_w_in, m_ssd_conv_w, m_ssd_conv_b, m_dt_bias, m_a_log, m_d_skip, m_ssd_norm_w, m_conf_conv_w, m_conf_conv_b, m_conf_ln_w, m_conf_ln_b, m_w_out, m_norm2_w, m_w_up, m_ffn_conv_w, m_ffn_conv_b, m_w_down, m_final_norm_w, v_ada_w, v_ada_b, v_norm1_w, v_w_in, v_ssd_conv_w, v_ssd_conv_b, v_dt_bias, v_a_log, v_d_skip, v_ssd_norm_w, v_conf_conv_w, v_conf_conv_b, v_conf_ln_w, v_conf_ln_b, v_w_out, v_norm2_w, v_w_up, v_ffn_conv_w, v_ffn_conv_b, v_w_down, v_final_norm_w):
    me = 4 * lax.axis_index("x") + 2 * lax.axis_index("y") + lax.axis_index("c")
    weights = dict(ada_w=ada_w, ada_b=ada_b, norm1_w=norm1_w, w_in=w_in, ssd_conv_w=ssd_conv_w, ssd_conv_b=ssd_conv_b,
                   dt_bias=dt_bias, a_log=a_log, d_skip=d_skip, ssd_norm_w=ssd_norm_w, conf_conv_w=conf_conv_w,
                   conf_conv_b=conf_conv_b, conf_ln_w=conf_ln_w, conf_ln_b=conf_ln_b, w_out=w_out, norm2_w=norm2_w, w_up=w_up,
                   ffn_conv_w=ffn_conv_w, ffn_conv_b=ffn_conv_b, w_down=w_down, final_norm_w=final_norm_w)
    moms_m = dict(ada_w=m_ada_w, ada_b=m_ada_b, norm1_w=m_norm1_w, w_in=m_w_in, ssd_conv_w=m_ssd_conv_w, ssd_conv_b=m_ssd_conv_b,
                  dt_bias=m_dt_bias, a_log=m_a_log, d_skip=m_d_skip, ssd_norm_w=m_ssd_norm_w, conf_conv_w=m_conf_conv_w,
                  conf_conv_b=m_conf_conv_b, conf_ln_w=m_conf_ln_w, conf_ln_b=m_conf_ln_b, w_out=m_w_out, norm2_w=m_norm2_w,
                  w_up=m_w_up, ffn_conv_w=m_ffn_conv_w, ffn_conv_b=m_ffn_conv_b, w_down=m_w_down, final_norm_w=m_final_norm_w)
    moms_v = dict(ada_w=v_ada_w, ada_b=v_ada_b, norm1_w=v_norm1_w, w_in=v_w_in, ssd_conv_w=v_ssd_conv_w, ssd_conv_b=v_ssd_conv_b,
                  dt_bias=v_dt_bias, a_log=v_a_log, d_skip=v_d_skip, ssd_norm_w=v_ssd_norm_w, conf_conv_w=v_conf_conv_w,
                  conf_conv_b=v_conf_conv_b, conf_ln_w=v_conf_ln_w, conf_ln_b=v_conf_ln_b, w_out=v_w_out, norm2_w=v_norm2_w,
                  w_up=v_w_up, ffn_conv_w=v_ffn_conv_w, ffn_conv_b=v_ffn_conv_b, w_down=v_w_down, final_norm_w=v_final_norm_w)
    names = list(weights)

    def to2d(a):
        return a[0] if a.ndim == 3 else a.reshape(1, -1)

    big = ("w_in", "w_out", "w_up", "w_down")

    def rows_of(a):
        return jnp.swapaxes(a, 1, 2)[0] if a.shape[2] != D else a[0]

    shards = [rows_of(weights[n]).astype(BF16) for n in big]

    c_all, scw_all, ccw_all, fcw_all, w_in_slabs = gather_two_level(
        [c.reshape(8, LANE), ssd_conv_w[0], conf_conv_w[0], ffn_conv_w[0], shards[0]], "gather_first")
    c_all = c_all.reshape(N_DEV, D)

    ada_cols = ada_w.shape[2]
    ada_b_cols = lax.dynamic_slice(ada_b, (0, me * ada_cols), (1, ada_cols))
    mod_cols, c_act_all = ada_mod(c_all, ada_w[0], ada_b_cols, "ada_mod")
    mod_parts, = exchange([jnp.pad(mod_cols, ((0, 0), (0, D - ada_cols))).reshape(N_DEV, 8, LANE)], "scatter_mod", gather=False)
    mod = mod_parts.reshape(N_DEV, D)[:, :ada_cols].reshape(6, D)
    mod = jnp.pad(mod, ((0, 2), (0, 0)))

    later, mod = lax.optimization_barrier((shards[1:], mod))
    gather_parts, token = exchange_start(later, "gather_weights_start", gather=True)
    mod = mod + token[0:1, 0:1]

    small = {n: to2d(weights[n]) for n in names if n not in ("ada_w",) + big}
    small["ssd_conv_w"] = _slabs_to_cols(scw_all)
    small["conf_conv_w"] = _slabs_to_cols(ccw_all)
    small["ffn_conv_w"] = _slabs_to_cols(fcw_all)

    def with_own(landed, own):
        return lax.dynamic_update_slice(landed, own[None], (me,) + (0,) * own.ndim)

    def get_w(n, after):
        if n == "w_in":
            slabs = w_in_slabs
        else:
            a = big.index(n)
            (own, landed), = exchange_wait([gather_parts[a - 1]], after, "gather_" + n + "_wait", gather=True)
            slabs = with_own(landed, own)
        full = slabs.reshape(N_DEV * slabs.shape[1], D)
        if n != "w_in":
            return full
        w_dt = full[D_SSD + D_XBC:D_SSD + D_XBC + N_HEADS]
        return dict(w_full=full, w_xbc=full[D_SSD:D_SSD + D_XBC], w_cf=full[D_SSD + D_XBC + N_HEADS:],
                    w_dt_rep=jnp.repeat(w_dt, HEAD, axis=0), w_dt16=jnp.pad(w_dt, ((0, LANE - N_HEADS), (0, 0))))

    scatter_parts = {}

    def put_grad(n, g):
        slabs = g if g.ndim == 3 else g.reshape(N_DEV, g.shape[0] // N_DEV, g.shape[1])
        (scatter_parts[n],), token = exchange_start([slabs.astype(BF16)], "scatter_" + n + "_start", gather=False)
        return token

    grad_x, accs = local_fwd_bwd(x[0], loss_target[0], mod, get_w, put_grad, small)

    grads, delta, new_m, new_v = {}, {}, {}, {}

    def finish(ns, after, name):
        landed = exchange_wait([scatter_parts[n] for n in ns], after, name, gather=False)
        for n, (sent, slots) in zip(ns, landed):
            slots = with_own(slots, lax.dynamic_index_in_dim(sent, me, 0, keepdims=False))
            out = adamw_slots(rows_of(weights[n]), slots, rows_of(moms_m[n]), rows_of(moms_v[n]), "adamw_" + n)
            if weights[n].shape[2] != D:
                out = [jnp.swapaxes(o, 0, 1) for o in out]
            grads[n], delta[n], new_m[n], new_v[n] = out

    finish(big[1:], grad_x, "scatter_grads_wait")

    accs = dict(zip(accs, lax.optimization_barrier((list(accs.values()), [new_v[n] for n in big[1:]]))[0]))
    rep = (("acc_1", 0, 3), ("acc_2", 0, 4), ("acc_f", 0, 3), ("acc_ln", 0, 2), ("acc_s", 0, 1), ("acc_s16", 1, 3),
           ("dw_sc", K_SSD, 1), ("dw_cc", K_CONF, 1), ("dwg", K_FFN, 1), ("dwv", K_FFN, 1))
    shapes = [(rows, accs[k].shape[1]) for k, _, rows in rep]
    conv_slabs = [_cols_to_slabs(accs["dw_sc"][:K_SSD]), _cols_to_slabs(accs["dw_cc"][:K_CONF]),
                  _cols_to_slabs(jnp.concatenate([accs["dwg"][:K_FFN], accs["dwv"][:K_FFN]], axis=1))]
    packed = pack_rows([(accs[k], first, rows) for k, first, rows in rep], "pack_small_grads")
    landed = exchange([packed] + conv_slabs, "exchange_small_grads", gather=[True, False, False, False])
    packed_red, g_scw, g_ccw, g_fcw = sum_slots_many(landed, "sum_small_grads")
    a1_all, a2_all, af_all = unpack_rows(landed[0], shapes)[:3]
    r1, r2, rf, rln, rs, r16, rscb, rccb, rfbg, rfbv = unpack_rows(packed_red, shapes)
    loss = 0.5 / D * jnp.sum(rf[2:3])

    def mod_rows(a1, a2, af):
        return jnp.concatenate([a1[..., 0:2, :], a2[..., 3:4, :], a2[..., 0:2, :], af[..., 1:2, :]], axis=-2)

    dmod_all = mod_rows(a1_all, a2_all, af_all).reshape(N_DEV, 6 * D)
    grads["ada_w"] = ada_wgrad(c_act_all, lax.dynamic_slice(dmod_all, (0, me * ada_cols), (N_DEV, ada_cols)), "ada_wgrad")
    grads.update(
        ada_b=mod_rows(r1, r2, rf).reshape(1, 6 * D), norm1_w=r1[2:3], ssd_conv_w=g_scw, ssd_conv_b=rscb,
        dt_bias=r16[0:1, :N_HEADS], a_log=r16[1:2, :N_HEADS], d_skip=r16[2:3, :N_HEADS], ssd_norm_w=rs,
        conf_conv_w=g_ccw, conf_conv_b=rccb, conf_ln_w=rln[0:1], conf_ln_b=rln[1:2], norm2_w=r2[2:3],
        ffn_conv_w=g_fcw, ffn_conv_b=jnp.concatenate([rfbg, rfbv], axis=1), final_norm_w=rf[0:1])

    rest = [n for n in names if n not in big]
    d_l, m_l, v_l = adamw_many([to2d(weights[n]) for n in rest], [grads[n] for n in rest], [to2d(moms_m[n]) for n in rest],
                               [to2d(moms_v[n]) for n in rest], "adamw_small")
    for n, dd, mm, vv in zip(rest, d_l, m_l, v_l):
        delta[n], new_m[n], new_v[n] = dd, mm, vv
    finish(big[:1], d_l[0], "scatter_w_in_wait")
    shape_of = lambda d_: {n: d_[n].reshape(weights[n].shape) for n in names}
    grads, delta, new_m, new_v = shape_of(grads), shape_of(delta), shape_of(new_m), shape_of(new_v)
    return (loss, grad_x[None], *[grads[n] for n in names], *[delta[n] for n in names], *[new_m[n] for n in names],
            *[new_v[n] for n in names])
```

```python
import functools

import jax
import jax.numpy as jnp
from jax import lax
from jax.experimental import pallas as pl
from jax.experimental.pallas import tpu as pltpu

F32 = jnp.float32
BF16 = jnp.bfloat16
HI = lax.Precision.HIGHEST

N_DEV = 8
D = 1024
D_SSD = 1024
HEAD = 64
N_HEADS = 16
N_STATE = 128
D_XBC = 1536
D_CONF = 1024
D_FF = 2816
K_SSD, K_CONF, K_FFN = 4, 31, 3
LANE = 128
TM = 512
Q = 256
CB = 256
TC = 1024
VMEM_LIMIT = 56 * 1024 * 1024

ADAM_LR, ADAM_B1, ADAM_B2, ADAM_EPS, ADAM_WD, ADAM_STEP = 0.001, 0.9, 0.999, 1e-08, 0.01, 10


def _cparams(sem=None):
    return pltpu.CompilerParams(vmem_limit_bytes=VMEM_LIMIT, dimension_semantics=sem)


def _sds(shape, dtype):
    return jax.ShapeDtypeStruct(shape, dtype)


def _sigmoid(x):
    return 1.0 / (1.0 + jnp.exp(-x))


def _silu(x):
    return x * _sigmoid(x)


def _dsilu(x):
    s = _sigmoid(x)
    return s * (1.0 + x * (1.0 - s))


def _softplus(x):
    return jnp.maximum(x, 0.0) + jnp.log(1.0 + jnp.exp(-jnp.abs(x)))


def _dot(a, b):
    return jnp.dot(a.astype(BF16), b.astype(BF16), preferred_element_type=F32)


def _dot_nt(a, b):
    return lax.dot_general(a.astype(BF16), b.astype(BF16), (((1,), (1,)), ((), ())), preferred_element_type=F32)


def _dot_tn(a, b):
    return lax.dot_general(a.astype(BF16), b.astype(BF16), (((0,), (0,)), ((), ())), preferred_element_type=F32)


def _bf16_terms(a, terms):
    parts, rem = [], a
    for t in range(terms):
        p = rem.astype(BF16)
        parts.append(p)
        if t + 1 < terms:
            rem = rem - p.astype(F32)
    return parts


def _dot_exact(a, b, terms, exact, dims=(((1,), (0,)), ((), ()))):
    if exact == "a":
        a_b = a.astype(BF16)
        outs = [lax.dot_general(a_b, p, dims, preferred_element_type=F32) for p in _bf16_terms(b, terms)]
    else:
        b_b = b.astype(BF16)
        outs = [lax.dot_general(p, b_b, dims, preferred_element_type=F32) for p in _bf16_terms(a, terms)]
    acc = outs[-1]
    for o in reversed(outs[:-1]):
        acc = acc + o
    return acc


def _dot_tn_hi(a, b):
    return lax.dot_general(a, b, (((0,), (0,)), ((), ())), precision=HI, preferred_element_type=F32)


def _colsum(x):
    return jnp.sum(x, axis=0, keepdims=True)


def _const_spec(shape):
    return pl.BlockSpec(shape, lambda *_: (0,) * len(shape))


def _resident_spec(shape, block_index=None):
    index = (0,) * len(shape) if block_index is None else block_index
    return pl.BlockSpec(shape, lambda *_: index, pipeline_mode=pl.Buffered(1))


def _col_tile(n):
    for t in (2816, 1408, 1024, 768, 512, 256, 128):
        if n % t == 0 and t <= n:
            return t
    return n


def mm_nt(pairs, name):
    L = pairs[0][0].shape[0]
    K = pairs[0][1].shape[0]
    tk = _col_tile(K)
    n = len(pairs)

    def body(*refs):
        o_ref = refs[-1]
        acc = None
        for p in range(n):
            t = lax.dot_general(refs[2 * p][...], refs[2 * p + 1][...], (((1,), (1,)), ((), ())),
                                preferred_element_type=F32)
            acc = t if acc is None else acc + t
        o_ref[...] = acc

    in_specs, args = [], []
    for a, w, cb in pairs:
        in_specs += [pl.BlockSpec((TM, a.shape[1]), lambda j, i: (i, 0)),
                     pl.BlockSpec((tk, a.shape[1]), functools.partial(lambda j, i, cb: (j, cb), cb=cb))]
        args += [a, w]
    return pl.pallas_call(
        body, name=name, grid=(K // tk, L // TM), out_shape=_sds((L, K), F32), in_specs=in_specs,
        out_specs=pl.BlockSpec((TM, tk), lambda j, i: (i, j)),
        compiler_params=_cparams(("parallel", "parallel")))(*args)


def mm_tn(a, g, name):
    L, M = a.shape
    N = g.shape[1]
    tn = _col_tile(N) if N > 1024 else N
    if M * tn * 4 > 12 * 1024 * 1024:
        tn = 512
    tl = TM
    nl = L // tl

    def body(a_ref, g_ref, o_ref, acc_ref):
        @pl.when(pl.program_id(1) == 0)
        def _():
            acc_ref[...] = jnp.zeros((M, tn), F32)

        acc_ref[...] += lax.dot_general(a_ref[...], g_ref[...], (((0,), (0,)), ((), ())), preferred_element_type=F32)

        @pl.when(pl.program_id(1) == nl - 1)
        def _():
            o_ref[...] = acc_ref[...].astype(BF16)

    return pl.pallas_call(
        body, name=name, grid=(N // tn, nl), out_shape=_sds((M, N), BF16),
        in_specs=[pl.BlockSpec((tl, M), lambda j, l: (l, 0)), pl.BlockSpec((tl, tn), lambda j, l: (l, j))],
        out_specs=pl.BlockSpec((M, tn), lambda j, l: (0, j)), scratch_shapes=[pltpu.VMEM((M, tn), F32)],
        compiler_params=_cparams(("parallel", "arbitrary")))(a, g)


def mm_tn_stack(a_list, g, name):
    L, M = a_list[0].shape
    N = g.shape[1]
    n = len(a_list)
    tl = TM
    nl = L // tl

    def body(*refs):
        a_refs, g_ref, o_ref, acc_ref = refs[:n], refs[n], refs[n + 1], refs[n + 2]
        j, l = pl.program_id(0), pl.program_id(1)

        @pl.when(l == 0)
        def _():
            acc_ref[...] = jnp.zeros((M, N), F32)

        for p in range(n):
            @pl.when(j == p)
            def _(p=p):
                acc_ref[...] += lax.dot_general(a_refs[p][...], g_ref[...], (((0,), (0,)), ((), ())), preferred_element_type=F32)

        @pl.when(l == nl - 1)
        def _():
            o_ref[...] = acc_ref[...].astype(BF16)

    a_specs = [pl.BlockSpec((tl, M), functools.partial(lambda j, l, p: (jnp.where(j == p, l, 0), 0), p=p)) for p in range(n)]
    return pl.pallas_call(
        body, name=name, grid=(n, nl), out_shape=_sds((n * M, N), BF16),
        in_specs=a_specs + [pl.BlockSpec((tl, N), lambda j, l: (l, 0))],
        out_specs=pl.BlockSpec((M, N), lambda j, l: (j, 0)), scratch_shapes=[pltpu.VMEM((M, N), F32)],
        compiler_params=_cparams(("arbitrary", "arbitrary")))(*a_list, g)


def mm_tn_concat(pieces, g, tail, name):
    L = g.shape[0]
    N = g.shape[1]
    n = len(pieces)
    offsets = [sum(r for _, r in pieces[:p]) for p in range(n + 1)]
    total = offsets[-1] + tail.shape[0]
    slab = total // N_DEV
    tl = TM
    nl = L // tl

    def body(*refs):
        a_refs, g_ref, tail_ref, o_ref, acc_ref = refs[:n], refs[n], refs[n + 1], refs[n + 2], refs[n + 3]
        l = pl.program_id(0)

        @pl.when(l == 0)
        def _():
            acc_ref[0:offsets[-1], :] = jnp.zeros((offsets[-1], N), F32)
            acc_ref[offsets[-1]:total, :] = tail_ref[...].astype(F32)

        g_v = g_ref[...]
        for p in range(n):
            t = lax.dot_general(a_refs[p][...], g_v, (((0,), (0,)), ((), ())), preferred_element_type=F32)
            acc_ref[offsets[p]:offsets[p + 1], :] += t[:pieces[p][1], :]

        @pl.when(l == nl - 1)
        def _():
            for s in range(N_DEV):
                o_ref[s] = acc_ref[s * slab:(s + 1) * slab, :].astype(BF16)

    return pl.pallas_call(
        body, name=name, grid=(nl,), out_shape=_sds((N_DEV, slab, N), BF16),
        in_specs=[pl.BlockSpec((tl, a.shape[1]), lambda l: (l, 0)) for a, _ in pieces]
        + [pl.BlockSpec((tl, N), lambda l: (l, 0)), _resident_spec(tail.shape)],
        out_specs=_const_spec((N_DEV, slab, N)), scratch_shapes=[pltpu.VMEM((total, N), F32)],
        compiler_params=_cparams(("arbitrary",)))(*[a for a, _ in pieces], g, tail)


def in_proj(x, mod, n1w, w_full, w_dt_rep, w_cf, name):
    L = x.shape[0]
    n_zx = D_SSD + D_XBC

    def body(x_ref, mod_ref, w_ref, wzx_ref, wdt_ref, wcf_ref, h_ref, zx_ref, dt_ref, cf_ref):
        xin = x_ref[...]
        r = lax.rsqrt(jnp.mean(xin * xin, axis=-1, keepdims=True) + 1e-6)
        h = ((xin * r * w_ref[...]) * (1.0 + mod_ref[1:2, :]) + mod_ref[0:1, :]).astype(BF16)
        h_ref[...] = h
        nt = (((1,), (1,)), ((), ()))
        zx_ref[...] = lax.dot_general(h, wzx_ref[...], nt, preferred_element_type=F32)
        dt_ref[...] = lax.dot_general(h, wdt_ref[...], nt, preferred_element_type=F32)
        cf_ref[...] = lax.dot_general(h, wcf_ref[...], nt, preferred_element_type=F32)

    row = lambda w: pl.BlockSpec((TM, w), lambda i: (i, 0))
    return pl.pallas_call(
        body, name=name, grid=(L // TM,),
        out_shape=[_sds((L, D), BF16), _sds((L, n_zx), F32), _sds((L, D_SSD), F32), _sds((L, 2 * D_CONF), F32)],
        in_specs=[row(D), _const_spec((8, D)), _const_spec((1, D)), _const_spec((n_zx, D)), _const_spec((D_SSD, D)),
                  _const_spec((2 * D_CONF, D))],
        out_specs=[row(D), row(n_zx), row(D_SSD), row(2 * D_CONF)],
        compiler_params=_cparams(("parallel",)))(x, mod, n1w, w_full, w_dt_rep, w_cf)


def mixer_out(ysn, uc, lnw, lnb, w_out, x, mod, n2w, name):
    L = x.shape[0]

    def body(ysn_ref, uc_ref, lnw_ref, lnb_ref, wo_ref, x_ref, mod_ref, n2w_ref, mix_ref, u_ref, x1_ref, h2_ref):
        uc_v = uc_ref[...]
        mu = jnp.mean(uc_v, axis=-1, keepdims=True)
        var = jnp.mean(jnp.square(uc_v - mu), axis=-1, keepdims=True)
        u = _silu((uc_v - mu) * lax.rsqrt(var + 1e-5) * lnw_ref[...] + lnb_ref[...]).astype(BF16)
        u_ref[...] = u
        mix = (jnp.dot(ysn_ref[...], wo_ref[0:D_SSD, :], preferred_element_type=F32)
               + jnp.dot(u, wo_ref[D_SSD:D_SSD + D_CONF, :], preferred_element_type=F32))
        mix_ref[...] = mix
        x1 = x_ref[...] + mod_ref[2:3, :] * mix
        x1_ref[...] = x1
        r = lax.rsqrt(jnp.mean(x1 * x1, axis=-1, keepdims=True) + 1e-6)
        h2_ref[...] = ((x1 * r * n2w_ref[...]) * (1.0 + mod_ref[4:5, :]) + mod_ref[3:4, :]).astype(BF16)

    row = pl.BlockSpec((TM, D), lambda i: (i, 0))
    return pl.pallas_call(
        body, name=name, grid=(L // TM,),
        out_shape=[_sds((L, D), F32), _sds((L, D_CONF), BF16), _sds((L, D), F32), _sds((L, D), BF16)],
        in_specs=[row, row, _const_spec((1, D)), _const_spec((1, D)), _const_spec((D_SSD + D_CONF, D)), row,
                  _const_spec((8, D)), _const_spec((1, D))],
        out_specs=[row] * 4, compiler_params=_cparams(("parallel",)))(ysn, uc, lnw, lnb, w_out, x, mod, n2w)


def mixer_out_bwd(dmix, w_out, uc, lnw, lnb, name):
    L = uc.shape[0]

    def body(dm_ref, wo_ref, u_ref, w_ref, b_ref, dy_ref, o_ref, acc_ref):
        @pl.when(pl.program_id(0) == 0)
        def _():
            acc_ref[...] = jnp.zeros((8, D), F32)

        nt = (((1,), (1,)), ((), ()))
        dm = dm_ref[...]
        dy_ref[...] = lax.dot_general(dm, wo_ref[0:D_SSD, :], nt, preferred_element_type=F32)
        du = lax.dot_general(dm, wo_ref[D_SSD:D_SSD + D_CONF, :], nt, preferred_element_type=F32)
        u = u_ref[...]
        mu = jnp.mean(u, axis=-1, keepdims=True)
        rl = lax.rsqrt(jnp.mean(jnp.square(u - mu), axis=-1, keepdims=True) + 1e-5)
        n = (u - mu) * rl
        v = n * w_ref[...] + b_ref[...]
        dv = du * _dsilu(v)
        acc_ref[0:1, :] += _colsum(dv * n)
        acc_ref[1:2, :] += _colsum(dv)
        dn = dv * w_ref[...]
        o_ref[...] = rl * (dn - jnp.mean(dn, axis=-1, keepdims=True) - n * jnp.mean(dn * n, axis=-1, keepdims=True))

    row = pl.BlockSpec((TM, D), lambda i: (i, 0))
    return pl.pallas_call(body, name=name, grid=(L // TM,), out_shape=[_sds((L, D), F32), _sds((L, D), F32), _sds((8, D), F32)],
                          in_specs=[row, _const_spec((D_SSD + D_CONF, D)), row, _const_spec((1, D)), _const_spec((1, D))],
                          out_specs=[row, row, _const_spec((8, D))],
                          compiler_params=_cparams(("arbitrary",)))(dmix, w_out, uc, lnw, lnb)


def final_loss(act, w_down, x1, mod, fw, target, name):
    L = act.shape[0]

    def body(act_ref, wd_ref, x1_ref, mod_ref, fw_ref, t_ref, dx_ref, dff_ref, dact_ref, acc_ref):
        @pl.when(pl.program_id(0) == 0)
        def _():
            acc_ref[...] = jnp.zeros((8, D), F32)

        ff_v = jnp.dot(act_ref[...], wd_ref[...], preferred_element_type=F32)
        g2 = mod_ref[5:6, :]
        x2 = x1_ref[...] + g2 * ff_v
        r = lax.rsqrt(jnp.mean(x2 * x2, axis=-1, keepdims=True) + 1e-6)
        n = x2 * r
        err = n * fw_ref[...] - t_ref[...]
        dy = err * (1.0 / D)
        dn = dy * fw_ref[...]
        dx2 = r * (dn - n * jnp.mean(dn * n, axis=-1, keepdims=True))
        acc_ref[0:1, :] += _colsum(dy * n)
        acc_ref[1:2, :] += _colsum(dx2 * ff_v)
        acc_ref[2:3, :] += _colsum(err * err)
        dx_ref[...] = dx2
        dff = (dx2 * g2).astype(BF16)
        dff_ref[...] = dff
        dact_ref[...] = lax.dot_general(dff, wd_ref[...], (((1,), (1,)), ((), ())), preferred_element_type=F32)

    row = lambda w: pl.BlockSpec((TM, w), lambda i: (i, 0))
    return pl.pallas_call(
        body, name=name, grid=(L // TM,),
        out_shape=[_sds((L, D), F32), _sds((L, D), BF16), _sds((L, D_FF), F32), _sds((8, D), F32)],
        in_specs=[row(D_FF), _resident_spec((D_FF, D)), row(D), _const_spec((8, D)), _const_spec((1, D)), row(D)],
        out_specs=[row(D), row(D), row(D_FF), _const_spec((8, D))],
        compiler_params=_cparams(("arbitrary",)))(act, w_down, x1, mod, fw, target)


def norm_mod_bwd(dh_pairs, xin, dres, mod, w, shift_row, name, mix=None, gate_row=None):
    L = xin.shape[0]
    has_mix = mix is not None
    n_pairs = len(dh_pairs)

    def body(*refs):
        pair_refs, refs = refs[:2 * n_pairs], refs[2 * n_pairs:]
        if has_mix:
            x_ref, dres_ref, mod_ref, w_ref, mix_ref, dx_ref, dmix_ref, acc_ref = refs
        else:
            x_ref, dres_ref, mod_ref, w_ref, dx_ref, acc_ref = refs

        @pl.when(pl.program_id(0) == 0)
        def _():
            acc_ref[...] = jnp.zeros((8, D), F32)

        dh_v = None
        for p in range(n_pairs):
            t = jnp.dot(pair_refs[2 * p][...], pair_refs[2 * p + 1][...], preferred_element_type=F32)
            dh_v = t if dh_v is None else dh_v + t
        x = x_ref[...]
        r = lax.rsqrt(jnp.mean(x * x, axis=-1, keepdims=True) + 1e-6)
        n = x * r
        nw = n * w_ref[...]
        sc1 = 1.0 + mod_ref[shift_row + 1:shift_row + 2, :]
        acc_ref[0:1, :] += _colsum(dh_v)
        acc_ref[1:2, :] += _colsum(dh_v * nw)
        dnw = dh_v * sc1
        acc_ref[2:3, :] += _colsum(dnw * n)
        dn = dnw * w_ref[...]
        dx = r * (dn - n * jnp.mean(dn * n, axis=-1, keepdims=True)) + dres_ref[...]
        dx_ref[...] = dx
        if has_mix:
            acc_ref[3:4, :] += _colsum(dx * mix_ref[...])
            dmix_ref[...] = (dx * mod_ref[gate_row:gate_row + 1, :]).astype(BF16)

    row = lambda width: pl.BlockSpec((TM, width), lambda i: (i, 0))
    ins, in_specs = [], []
    for a, wt, rb in dh_pairs:
        ins += [a, wt]
        in_specs += [row(a.shape[1]), _resident_spec((a.shape[1], D), (rb, 0))]
    ins += [xin, dres, mod, w] + ([mix] if has_mix else [])
    in_specs += [row(D), row(D), _const_spec((8, D)), _const_spec((1, D))] + ([row(D)] if has_mix else [])
    out_shape = [_sds((L, D), F32)] + ([_sds((L, D), BF16)] if has_mix else []) + [_sds((8, D), F32)]
    out_specs = [row(D)] + ([row(D)] if has_mix else []) + [_const_spec((8, D))]
    return pl.pallas_call(body, name=name, grid=(L // TM,), out_shape=out_shape, in_specs=in_specs,
                          out_specs=out_specs, compiler_params=_cparams(("arbitrary",)))(*ins)


def _halo(k):
    return 8 if k <= 9 else 32


def _prev_spec(h, col0):
    return pl.BlockSpec((h, CB), lambda j, i: (jnp.maximum(i * (TC // h) - 1, 0), j + col0))


def _next_spec(h, col0, n_tiles):
    return pl.BlockSpec((h, CB), lambda j, i: (jnp.minimum(i + 1, n_tiles - 1) * (TC // h), j + col0))


def _tile_spec(col0):
    return pl.BlockSpec((TC, CB), lambda j, i: (i, j + col0))


def _w_spec(kp, col0):
    return pl.BlockSpec((kp, CB), lambda j, i: (0, j + col0))


SUBLANES = 8


def _shifted_windows(v, taps, rows):
    for r in range(SUBLANES):
        group = [(o, k) for o, k in taps if o % SUBLANES == r]
        if not group:
            continue
        s = v if r == 0 else pltpu.roll(v, v.shape[0] - r, 0)
        for o, k in group:
            yield k, s[o - r:o - r + rows, :]


def _causal_taps(ext_ref, w_ref, k_taps, first, rows):
    acc = None
    for k, win in _shifted_windows(ext_ref[...], [(first - (k_taps - 1) + k, k) for k in range(k_taps)], rows):
        t = w_ref[k:k + 1, :] * win
        acc = t if acc is None else acc + t
    return acc


def _anticausal_taps(d_ref, w_ref, k_taps, rows):
    acc = None
    for k, win in _shifted_windows(d_ref[...], [(k_taps - 1 - k, k) for k in range(k_taps)], rows):
        t = w_ref[k:k + 1, :] * win
        acc = t if acc is None else acc + t
    return acc


def _acc_conv_wgrad(dw_ref, d_tile, ext_ref, k_taps, first):
    for k, win in _shifted_windows(ext_ref[...], [(first - (k_taps - 1) + k, k) for k in range(k_taps)], TC):
        dw_ref[k:k + 1, :] += _colsum(d_tile * win)
    dw_ref[k_taps:k_taps + 1, :] += _colsum(d_tile)


def conv_silu_fwd(x, col0, width, w, b, name):
    L = x.shape[0]
    k_taps = w.shape[0]
    h = _halo(k_taps)

    def body(xp_ref, x_ref, w_ref, b_ref, o_ref, ext_ref):
        i = pl.program_id(1)
        ext_ref[0:h, :] = jnp.where(i > 0, xp_ref[...], 0.0)
        ext_ref[h:h + TC, :] = x_ref[...]
        o_ref[...] = _silu(_causal_taps(ext_ref, w_ref, k_taps, h, TC) + b_ref[...])

    return pl.pallas_call(
        body, name=name, grid=(width // CB, L // TC), out_shape=_sds((L, width), F32),
        in_specs=[_prev_spec(h, col0), _tile_spec(col0), _w_spec(k_taps, 0), pl.BlockSpec((1, CB), lambda j, i: (0, j))],
        out_specs=_tile_spec(0), scratch_shapes=[pltpu.VMEM((h + TC, CB), F32)],
        compiler_params=_cparams(("parallel", "parallel")))(x, x, w, b)


def conv_silu_bwd(x, col0, width, w, b, dpost, name):
    L = x.shape[0]
    k_taps = w.shape[0]
    h = _halo(k_taps)
    nt = L // TC

    def body(xp_ref, x_ref, xn_ref, d_ref, dn_ref, w_ref, b_ref, dx_ref, dw_ref, ext_ref, dpre_ref):
        i = pl.program_id(1)

        @pl.when(i == 0)
        def _():
            dw_ref[...] = jnp.zeros((8, CB), F32)

        ext_ref[0:h, :] = jnp.where(i > 0, xp_ref[...], 0.0)
        ext_ref[h:h + TC, :] = x_ref[...]
        ext_ref[h + TC:h + TC + h, :] = xn_ref[...]
        pre = _causal_taps(ext_ref, w_ref, k_taps, h, TC + h) + b_ref[...]
        dpre_ref[0:TC, :] = d_ref[...] * _dsilu(pre[0:TC, :])
        dpre_ref[TC:TC + h, :] = jnp.where(i < nt - 1, dn_ref[...], 0.0) * _dsilu(pre[TC:TC + h, :])
        dx_ref[...] = _anticausal_taps(dpre_ref, w_ref, k_taps, TC).astype(BF16)
        _acc_conv_wgrad(dw_ref, dpre_ref[0:TC, :], ext_ref, k_taps, h)

    return pl.pallas_call(
        body, name=name, grid=(width // CB, nt),
        out_shape=[_sds((L, width), BF16), _sds((8, width), F32)],
        in_specs=[_prev_spec(h, col0), _tile_spec(col0), _next_spec(h, col0, nt), _tile_spec(0), _next_spec(h, 0, nt),
                  _w_spec(k_taps, 0), pl.BlockSpec((1, CB), lambda j, i: (0, j))],
        out_specs=[_tile_spec(0), _w_spec(8, 0)],
        scratch_shapes=[pltpu.VMEM((h + TC + h, CB), F32), pltpu.VMEM((TC + h, CB), F32)],
        compiler_params=_cparams(("parallel", "arbitrary")))(x, x, x, dpost, dpost, w, b)


def conf_conv_fwd(proj, col_a, col_g, w, b, name):
    L = proj.shape[0]
    k_taps = w.shape[0]
    h = _halo(k_taps)

    def body(ap_ref, a_ref, gp_ref, g_ref, w_ref, b_ref, o_ref, ext_ref):
        i = pl.program_id(1)
        ext_ref[0:h, :] = jnp.where(i > 0, ap_ref[...] * _sigmoid(gp_ref[...]), 0.0)
        ext_ref[h:h + TC, :] = a_ref[...] * _sigmoid(g_ref[...])
        o_ref[...] = _causal_taps(ext_ref, w_ref, k_taps, h, TC) + b_ref[...]

    return pl.pallas_call(
        body, name=name, grid=(D_CONF // CB, L // TC), out_shape=_sds((L, D_CONF), F32),
        in_specs=[_prev_spec(h, col_a), _tile_spec(col_a), _prev_spec(h, col_g), _tile_spec(col_g), _w_spec(k_taps, 0),
                  pl.BlockSpec((1, CB), lambda j, i: (0, j))],
        out_specs=_tile_spec(0), scratch_shapes=[pltpu.VMEM((h + TC, CB), F32)],
        compiler_params=_cparams(("parallel", "parallel")))(proj, proj, proj, proj, w, b)


def conf_conv_bwd(proj, col_a, col_g, w, duc, name):
    L = proj.shape[0]
    k_taps = w.shape[0]
    h = _halo(k_taps)
    nt = L // TC

    def body(ap_ref, a_ref, gp_ref, g_ref, d_ref, dn_ref, w_ref, da_ref, dg_ref, dw_ref, ext_ref, dext_ref):
        i = pl.program_id(1)

        @pl.when(i == 0)
        def _():
            dw_ref[...] = jnp.zeros((32, CB), F32)

        a = a_ref[...]
        s = _sigmoid(g_ref[...])
        ext_ref[0:h, :] = jnp.where(i > 0, ap_ref[...] * _sigmoid(gp_ref[...]), 0.0)
        ext_ref[h:h + TC, :] = a * s
        dext_ref[0:TC, :] = d_ref[...]
        dext_ref[TC:TC + h, :] = jnp.where(i < nt - 1, dn_ref[...], 0.0)
        du0 = _anticausal_taps(dext_ref, w_ref, k_taps, TC)
        da_ref[...] = (du0 * s).astype(BF16)
        dg_ref[...] = (du0 * a * s * (1.0 - s)).astype(BF16)
        _acc_conv_wgrad(dw_ref, d_ref[...], ext_ref, k_taps, h)

    return pl.pallas_call(
        body, name=name, grid=(D_CONF // CB, nt),
        out_shape=[_sds((L, D_CONF), BF16), _sds((L, D_CONF), BF16), _sds((32, D_CONF), F32)],
        in_specs=[_prev_spec(h, col_a), _tile_spec(col_a), _prev_spec(h, col_g), _tile_spec(col_g), _tile_spec(0),
                  _next_spec(h, 0, nt), _w_spec(k_taps, 0)],
        out_specs=[_tile_spec(0), _tile_spec(0), _w_spec(32, 0)],
        scratch_shapes=[pltpu.VMEM((h + TC, CB), F32), pltpu.VMEM((TC + h, CB), F32)],
        compiler_params=_cparams(("parallel", "arbitrary")))(proj, proj, proj, proj, duc, duc, w)


def ffn_conv_fwd(up, w, b, name):
    L = up.shape[0]
    k_taps = w.shape[0]
    h = _halo(k_taps)
    cv = D_FF // CB

    def body(gp_ref, g_ref, vp_ref, v_ref, wg_ref, wv_ref, bg_ref, bv_ref, o_ref, eg_ref, ev_ref):
        i = pl.program_id(1)
        eg_ref[0:h, :] = jnp.where(i > 0, gp_ref[...], 0.0)
        eg_ref[h:h + TC, :] = g_ref[...]
        ev_ref[0:h, :] = jnp.where(i > 0, vp_ref[...], 0.0)
        ev_ref[h:h + TC, :] = v_ref[...]
        pg = _causal_taps(eg_ref, wg_ref, k_taps, h, TC) + bg_ref[...]
        pv = _causal_taps(ev_ref, wv_ref, k_taps, h, TC) + bv_ref[...]
        o_ref[...] = (_silu(pg) * pv).astype(BF16)

    bspec = lambda c0: pl.BlockSpec((1, CB), lambda j, i: (0, j + c0))
    return pl.pallas_call(
        body, name=name, grid=(cv, L // TC), out_shape=_sds((L, D_FF), BF16),
        in_specs=[_prev_spec(h, 0), _tile_spec(0), _prev_spec(h, cv), _tile_spec(cv), _w_spec(k_taps, 0), _w_spec(k_taps, cv),
                  bspec(0), bspec(cv)],
        out_specs=_tile_spec(0), scratch_shapes=[pltpu.VMEM((h + TC, CB), F32), pltpu.VMEM((h + TC, CB), F32)],
        compiler_params=_cparams(("parallel", "parallel")))(up, up, up, up, w, w, b, b)


def ffn_conv_bwd(up, w, b, dact, name):
    L = up.shape[0]
    k_taps = w.shape[0]
    h = _halo(k_taps)
    nt = L // TC
    cv = D_FF // CB

    def body(gp_ref, g_ref, gn_ref, vp_ref, v_ref, vn_ref, d_ref, dn_ref, wg_ref, wv_ref, bg_ref, bv_ref,
             dg_ref, dv_ref, dwg_ref, dwv_ref, eg_ref, ev_ref, pg_ref, pv_ref):
        i = pl.program_id(1)

        @pl.when(i == 0)
        def _():
            dwg_ref[...] = jnp.zeros((8, CB), F32)
            dwv_ref[...] = jnp.zeros((8, CB), F32)

        for e_ref, p_ref, c_ref, n_ref in ((eg_ref, gp_ref, g_ref, gn_ref), (ev_ref, vp_ref, v_ref, vn_ref)):
            e_ref[0:h, :] = jnp.where(i > 0, p_ref[...], 0.0)
            e_ref[h:h + TC, :] = c_ref[...]
            e_ref[h + TC:h + TC + h, :] = n_ref[...]
        pg = _causal_taps(eg_ref, wg_ref, k_taps, h, TC + h) + bg_ref[...]
        pv = _causal_taps(ev_ref, wv_ref, k_taps, h, TC + h) + bv_ref[...]
        dact_t = d_ref[...]
        dact_n = jnp.where(i < nt - 1, dn_ref[...], 0.0)
        pg_ref[0:TC, :] = dact_t * pv[0:TC, :] * _dsilu(pg[0:TC, :])
        pg_ref[TC:TC + h, :] = dact_n * pv[TC:TC + h, :] * _dsilu(pg[TC:TC + h, :])
        pv_ref[0:TC, :] = dact_t * _silu(pg[0:TC, :])
        pv_ref[TC:TC + h, :] = dact_n * _silu(pg[TC:TC + h, :])
        dg_ref[...] = _anticausal_taps(pg_ref, wg_ref, k_taps, TC).astype(BF16)
        dv_ref[...] = _anticausal_taps(pv_ref, wv_ref, k_taps, TC).astype(BF16)
        _acc_conv_wgrad(dwg_ref, pg_ref[0:TC, :], eg_ref, k_taps, h)
        _acc_conv_wgrad(dwv_ref, pv_ref[0:TC, :], ev_ref, k_taps, h)

    bspec = lambda c0: pl.BlockSpec((1, CB), lambda j, i: (0, j + c0))
    ext = pltpu.VMEM((h + TC + h, CB), F32)
    dpre = pltpu.VMEM((TC + h, CB), F32)
    return pl.pallas_call(
        body, name=name, grid=(cv, nt),
        out_shape=[_sds((L, D_FF), BF16), _sds((L, D_FF), BF16), _sds((8, D_FF), F32), _sds((8, D_FF), F32)],
        in_specs=[_prev_spec(h, 0), _tile_spec(0), _next_spec(h, 0, nt), _prev_spec(h, cv), _tile_spec(cv), _next_spec(h, cv, nt),
                  _tile_spec(0), _next_spec(h, 0, nt), _w_spec(k_taps, 0), _w_spec(k_taps, cv), bspec(0), bspec(cv)],
        out_specs=[_tile_spec(0), _tile_spec(0), _w_spec(8, 0), _w_spec(8, 0)],
        scratch_shapes=[ext, ext, dpre, dpre],
        compiler_params=_cparams(("parallel", "arbitrary")))(up, up, up, up, up, up, dact, dact, w, w, b, b)


def _ssd_common(xbc_ref, dt_ref, dtb_ref, alog_ref, cs_ref):
    xs = xbc_ref[:, 0:D_SSD]
    sp_in = dt_ref[...] + dtb_ref[...]
    dtf = _softplus(sp_in)
    a_f = -jnp.exp(alog_ref[...])
    a_dt = dtf * a_f
    row = lax.broadcasted_iota(jnp.int32, (Q, Q), 0)
    col = lax.broadcasted_iota(jnp.int32, (Q, Q), 1)
    causal = row >= col
    cs = _dot_exact(causal.astype(F32), a_dt, 3, "a")
    cs_ref[...] = cs
    cs_last = cs_ref[Q - 1:Q, :]
    return xs, sp_in, dtf, a_f, cs, cs_last, causal


def _head_decay(cs_j, cst_ref, e, causal):
    lane = lax.broadcasted_iota(jnp.int32, (Q, LANE), 1)
    rolled = pltpu.roll(cs_j, HEAD, 1)
    own = (lane < HEAD) if e == 0 else (lane >= HEAD)
    col_b = jnp.where(own, cs_j, rolled)
    col_b = jnp.concatenate([col_b] * (Q // LANE), axis=1)
    row_b = cst_ref[e * HEAD:e * HEAD + 1, :]
    return jnp.where(causal, jnp.exp(jnp.minimum(col_b - row_b, 0.0)), 0.0)


def ssd_fwd(xbc, z_src, dt_src, dtb_f, alog_f, dsk_f, snw, name):
    L = xbc.shape[0]
    nc = L // Q

    def body(xbc_ref, z_ref, dt_ref, dtb_ref, alog_ref, dsk_ref, snw_ref, y_ref, yn_ref, sp_ref, s_ref, cs_ref, cst_ref, yd_ref):
        @pl.when(pl.program_id(0) == 0)
        def _():
            s_ref[...] = jnp.zeros((N_STATE, D_SSD), F32)

        xs, _, dtf, a_f, cs, cs_last, causal = _ssd_common(xbc_ref, dt_ref, dtb_ref, alog_ref, cs_ref)
        e_cs = jnp.exp(cs)
        xdt = xs * dtf
        zst = jnp.exp(cs_last - cs) * xdt
        sp_ref[0] = s_ref[...]
        lane = lax.broadcasted_iota(jnp.int32, (Q, LANE), 1)
        for g in range(2):
            gl = slice(g * 512, g * 512 + 512)
            b_g = xbc_ref[:, D_SSD + g * N_STATE:D_SSD + (g + 1) * N_STATE]
            c_g = xbc_ref[:, D_SSD + 2 * N_STATE + g * N_STATE:D_SSD + 2 * N_STATE + (g + 1) * N_STATE]
            s_prev = s_ref[:, gl]
            cb = _dot_nt(c_g, b_g)
            yd_ref[:, gl] = e_cs[:, gl] * _dot(c_g, s_prev)
            for j in range(4):
                tl = slice(g * 512 + j * LANE, g * 512 + (j + 1) * LANE)
                cs_j = cs[:, tl]
                cst_ref[...] = cs_j.T
                x_j = xdt[:, tl]
                o0 = _dot(cb * _head_decay(cs_j, cst_ref, 0, causal), x_j)
                o1 = _dot(cb * _head_decay(cs_j, cst_ref, 1, causal), x_j)
                yd_ref[:, tl] += jnp.where(lane < HEAD, o0, o1)
            s_ref[:, gl] = jnp.exp(cs_last[:, gl]) * s_prev + _dot_tn(b_g, zst[:, gl])
        y = yd_ref[...] + xs * dsk_ref[...]
        y_ref[...] = y
        yz = y * _silu(z_ref[...])
        r = lax.rsqrt(jnp.mean(yz * yz, axis=-1, keepdims=True) + 1e-6)
        yn_ref[...] = (yz * r * snw_ref[...]).astype(BF16)

    chunk = lambda w, c: pl.BlockSpec((Q, w), lambda i: (i, c))
    return pl.pallas_call(
        body, name=name, grid=(nc,),
        out_shape=[_sds((L, D_SSD), F32), _sds((L, D_SSD), BF16), _sds((nc, N_STATE, D_SSD), F32)],
        in_specs=[chunk(D_XBC, 0), chunk(D, 0), chunk(D, 0)] + [_const_spec((1, D))] * 4,
        out_specs=[chunk(D, 0), chunk(D, 0), pl.BlockSpec((1, N_STATE, D_SSD), lambda i: (i, 0, 0))],
        scratch_shapes=[pltpu.VMEM((N_STATE, D_SSD), F32), pltpu.VMEM((Q, D_SSD), F32), pltpu.VMEM((LANE, Q), F32),
                        pltpu.VMEM((Q, D_SSD), F32)],
        compiler_params=_cparams(("arbitrary",)))(xbc, z_src, dt_src, dtb_f, alog_f, dsk_f, snw)


def ssd_bwd(dysn, y, xbc, z_src, dt_src, s_prev_all, dtb_f, alog_f, dsk_f, snw, name):
    L = xbc.shape[0]
    nc = L // Q

    def body(dyn_ref, y_ref, xbc_ref, z_ref, dt_ref, sp_ref, dtb_ref, alog_ref, dsk_ref, snw_ref,
             dz_ref, ddt_ref, dxbc_ref, acc_ref, acc16_ref, ds_ref, cs_ref, cst_ref, dcs_ref, dx_ref):
        step = pl.program_id(0)

        @pl.when(step == 0)
        def _():
            ds_ref[...] = jnp.zeros((N_STATE, D_SSD), F32)
            acc_ref[...] = jnp.zeros((8, D), F32)

        z = z_ref[...]
        y = y_ref[...]
        sz = _sigmoid(z)
        siluz = z * sz
        yz = y * siluz
        r = lax.rsqrt(jnp.mean(yz * yz, axis=-1, keepdims=True) + 1e-6)
        n = yz * r
        dyn = dyn_ref[...]
        acc_ref[0:1, :] += _colsum(dyn * n)
        dn = dyn * snw_ref[...]
        dyz = r * (dn - n * jnp.mean(dn * n, axis=-1, keepdims=True))
        dy = dyz * siluz
        dz_ref[...] = (dyz * y * (sz * (1.0 + z * (1.0 - sz)))).astype(BF16)

        xs, sp_in, dtf, a_f, cs, cs_last, causal = _ssd_common(xbc_ref, dt_ref, dtb_ref, alog_ref, cs_ref)
        acc_ref[3:4, :] += _colsum(dy * xs)
        e_cs = jnp.exp(cs)
        xdt = xs * dtf
        dst = jnp.exp(cs_last - cs)
        zst = dst * xdt
        e_last = jnp.exp(cs_last)
        lane = lax.broadcasted_iota(jnp.int32, (Q, LANE), 1)
        ones = jnp.ones((Q, LANE), F32)
        dcs_last_parts = []
        for g in range(2):
            gl = slice(g * 512, g * 512 + 512)
            b_g = xbc_ref[:, D_SSD + g * N_STATE:D_SSD + (g + 1) * N_STATE]
            c_g = xbc_ref[:, D_SSD + 2 * N_STATE + g * N_STATE:D_SSD + 2 * N_STATE + (g + 1) * N_STATE]
            s_prev = sp_ref[0, :, gl]
            ds_g = ds_ref[:, gl]
            dy_g = dy[:, gl]
            cb = _dot_nt(c_g, b_g)
            y_off = e_cs[:, gl] * _dot(c_g, s_prev)
            edy = e_cs[:, gl] * dy_g
            d_c = _dot_nt(edy, s_prev)
            d_z = _dot(b_g, ds_g)
            d_b = _dot_nt(zst[:, gl], ds_g)
            t_g = d_z * zst[:, gl]
            dcs_ref[:, gl] = dy_g * y_off - t_g
            dx_ref[:, gl] = d_z * dst[:, gl]
            dcs_last_parts.append(_colsum(t_g) + _colsum(ds_g * s_prev) * e_last[:, gl])
            ds_ref[:, gl] = e_last[:, gl] * ds_g + _dot_tn(c_g, edy)
            dcb = jnp.zeros((Q, Q), F32)
            for j in range(4):
                tl = slice(g * 512 + j * LANE, g * 512 + (j + 1) * LANE)
                cs_j = cs[:, tl]
                cst_ref[...] = cs_j.T
                x_j = xdt[:, tl]
                dy_j = dy[:, tl]
                dx_j = jnp.zeros((Q, LANE), F32)
                dcs_j = jnp.zeros((Q, LANE), F32)
                for e in range(2):
                    own = (lane < HEAD) if e == 0 else (lane >= HEAD)
                    w_h = _head_decay(cs_j, cst_ref, e, causal)
                    g_h = cb * w_h
                    dy_m = jnp.where(own, dy_j, 0.0)
                    d_g = _dot_nt(dy_m, x_j)
                    dx_j = dx_j + _dot_tn(g_h, dy_m)
                    dcb = dcb + d_g * w_h
                    p_h = d_g * g_h
                    row_sums = _dot_exact(p_h, ones, 2, "b")
                    col_sums = _dot_exact(p_h, ones, 2, "b", (((0,), (0,)), ((), ())))
                    dcs_j = dcs_j + jnp.where(own, row_sums - col_sums, 0.0)
                dcs_ref[:, tl] += dcs_j * (1.0 / HEAD)
                dx_ref[:, tl] += dx_j
            d_c = d_c + _dot(dcb, b_g)
            d_b = d_b + _dot_tn(dcb, c_g)
            dxbc_ref[:, D_SSD + g * N_STATE:D_SSD + (g + 1) * N_STATE] = d_b
            dxbc_ref[:, D_SSD + 2 * N_STATE + g * N_STATE:D_SSD + 2 * N_STATE + (g + 1) * N_STATE] = d_c
        dcs_last = jnp.concatenate(dcs_last_parts, axis=1)
        anticausal = lax.broadcasted_iota(jnp.int32, (Q, Q), 0) <= lax.broadcasted_iota(jnp.int32, (Q, Q), 1)
        d_adt = _dot_exact(anticausal.astype(F32), dcs_ref[...], 3, "a") + dcs_last
        dx = dx_ref[...]
        acc_ref[2:3, :] += _colsum(d_adt * dtf) * a_f
        d_dtf = d_adt * a_f + dx * xs
        dxbc_ref[:, 0:D_SSD] = dx * dtf + dy * dsk_ref[...]
        d_raw = d_dtf * _sigmoid(sp_in)
        acc_ref[1:2, :] += _colsum(d_raw)
        head_of_lane = lax.broadcasted_iota(jnp.int32, (D_SSD, LANE), 0) // HEAD
        fold = (head_of_lane == lax.broadcasted_iota(jnp.int32, (D_SSD, LANE), 1)).astype(F32)
        ddt_ref[...] = _dot_exact(d_raw, fold, 2, "b").astype(BF16)

        @pl.when(step == nc - 1)
        def _():
            acc16_ref[...] = _dot_exact(acc_ref[...], fold, 3, "b")

    rchunk = lambda w, c: pl.BlockSpec((Q, w), lambda i: (nc - 1 - i, c))
    return pl.pallas_call(
        body, name=name, grid=(nc,),
        out_shape=[_sds((L, D_SSD), BF16), _sds((L, LANE), BF16), _sds((L, D_XBC), F32), _sds((8, D), F32), _sds((8, LANE), F32)],
        in_specs=[rchunk(D, 0), rchunk(D, 0), rchunk(D_XBC, 0), rchunk(D, 0), rchunk(D, 0),
                  pl.BlockSpec((1, N_STATE, D_SSD), lambda i: (nc - 1 - i, 0, 0))] + [_const_spec((1, D))] * 4,
        out_specs=[rchunk(D, 0), rchunk(LANE, 0), rchunk(D_XBC, 0), _const_spec((8, D)), _const_spec((8, LANE))],
        scratch_shapes=[pltpu.VMEM((N_STATE, D_SSD), F32), pltpu.VMEM((Q, D_SSD), F32), pltpu.VMEM((LANE, Q), F32),
                        pltpu.VMEM((Q, D_SSD), F32), pltpu.VMEM((Q, D_SSD), F32)],
        compiler_params=_cparams(("arbitrary",)))(dysn, y, xbc, z_src, dt_src, s_prev_all, dtb_f, alog_f, dsk_f, snw)


def _adamw_math(w, g, m, v):
    m_n = ADAM_B1 * m + (1.0 - ADAM_B1) * g
    v_n = ADAM_B2 * v + (1.0 - ADAM_B2) * jnp.square(g)
    c1 = 1.0 - ADAM_B1 ** ADAM_STEP
    c2 = 1.0 - ADAM_B2 ** ADAM_STEP
    return -ADAM_LR * ((m_n / c1) / (jnp.sqrt(v_n / c2) + ADAM_EPS) + ADAM_WD * w), m_n, v_n


def _sum_slots(p_ref):
    acc = p_ref[0].astype(F32)
    for s in range(1, p_ref.shape[0]):
        acc = acc + p_ref[s].astype(F32)
    return acc


def adamw_slots(w, slots, m, v, name):
    rows, cols = w.shape
    tc = 256

    def body(w_ref, s_ref, m_ref, v_ref, g_ref, d_ref, mo_ref, vo_ref):
        g_v = _sum_slots(s_ref)
        g_ref[...] = g_v
        d_ref[...], mo_ref[...], vo_ref[...] = _adamw_math(w_ref[...], g_v, m_ref[...], v_ref[...])

    spec = pl.BlockSpec((rows, tc), lambda i: (0, i))
    return pl.pallas_call(body, name=name, grid=(cols // tc,), out_shape=[_sds((rows, cols), F32)] * 4,
                          in_specs=[spec, pl.BlockSpec((slots.shape[0], rows, tc), lambda i: (0, 0, i)), spec, spec], out_specs=[spec] * 4,
                          compiler_params=_cparams(("parallel",)))(w, slots, m, v)


def adamw_many(ws, gs, ms, vs, name):
    n = len(ws)

    def body(*refs):
        for p in range(n):
            d_v, m_v, v_v = _adamw_math(refs[p][...], refs[n + p][...], refs[2 * n + p][...], refs[3 * n + p][...])
            refs[4 * n + p][...] = d_v
            refs[5 * n + p][...] = m_v
            refs[6 * n + p][...] = v_v

    vm = pl.BlockSpec(memory_space=pltpu.VMEM)
    out = pl.pallas_call(body, name=name, out_shape=[_sds(w.shape, F32) for w in ws] * 3, in_specs=[vm] * (4 * n),
                         out_specs=[vm] * (3 * n), compiler_params=_cparams())(*ws, *gs, *ms, *vs)
    return out[:n], out[n:2 * n], out[2 * n:]


def _pack_layout(shapes):
    row, layout = 0, []
    for rows, cols in shapes:
        chunks = []
        for c0 in range(0, cols, D):
            chunks.append((row, c0, min(D, cols - c0)))
            row += rows
        layout.append(chunks)
    return row, layout


def pack_rows(entries, name):
    arrays = [e[0] for e in entries]
    used, layout = _pack_layout([(e[2], e[0].shape[1]) for e in entries])
    total = -(-used // SUBLANES) * SUBLANES
    n = len(arrays)

    def body(*refs):
        o_ref = refs[n]
        o_ref[...] = jnp.zeros((total, D), F32)
        for p in range(n):
            _, first, rows = entries[p]
            for r0, c0, w in layout[p]:
                o_ref[r0:r0 + rows, 0:w] = refs[p][first:first + rows, c0:c0 + w]

    vm = pl.BlockSpec(memory_space=pltpu.VMEM)
    return pl.pallas_call(body, name=name, out_shape=_sds((total, D), F32), in_specs=[vm] * n, out_specs=vm,
                          compiler_params=_cparams())(*arrays)


def unpack_rows(packed, shapes):
    _, layout = _pack_layout(shapes)
    out = []
    for (rows, _), chunks in zip(shapes, layout):
        parts = [packed[..., r0:r0 + rows, 0:w] for r0, _, w in chunks]
        out.append(parts[0] if len(parts) == 1 else jnp.concatenate(parts, axis=-1))
    return out


def sum_slots_many(parts, name):
    n = len(parts)

    def body(*refs):
        for p in range(n):
            refs[n + p][...] = _sum_slots(refs[p])

    vm = pl.BlockSpec(memory_space=pltpu.VMEM)
    return pl.pallas_call(body, name=name, out_shape=[_sds(p.shape[1:], F32) for p in parts], in_specs=[vm] * n,
                          out_specs=[vm] * n, compiler_params=_cparams())(*parts)


def ada_mod(c_all, ada_w_shard, ada_b_cols, name):
    def body(c_ref, w_ref, b_ref, o_ref, ca_ref):
        ca = _silu(c_ref[...])
        ca_ref[...] = ca
        o_ref[...] = _dot(ca, w_ref[...]) + b_ref[...]

    vm = pl.BlockSpec(memory_space=pltpu.VMEM)
    return pl.pallas_call(body, name=name, out_shape=[_sds((N_DEV, ada_w_shard.shape[1]), F32), _sds((N_DEV, D), F32)],
                          in_specs=[vm, vm, vm], out_specs=[vm, vm], compiler_params=_cparams())(c_all, ada_w_shard, ada_b_cols)


def ada_wgrad(c_act_all, dmod_cols, name):
    def body(c_ref, d_ref, o_ref):
        o_ref[...] = _dot_tn_hi(c_ref[...], d_ref[...])

    vm = pl.BlockSpec(memory_space=pltpu.VMEM)
    return pl.pallas_call(body, name=name, out_shape=_sds((D, dmod_cols.shape[1]), F32), in_specs=[vm, vm], out_specs=vm,
                          compiler_params=_cparams())(c_act_all, dmod_cols)


def exchange(srcs, name, gather):
    n = len(srcs)
    gathers = [gather] * n if isinstance(gather, bool) else list(gather)
    shapes = [tuple(s.shape) if g else tuple(s.shape[1:]) for s, g in zip(srcs, gathers)]

    def body(*refs):
        src_refs, out_refs = refs[:n], refs[n:2 * n]
        send_sems, recv_sems, local_sems = refs[2 * n:]
        x, y, c = lax.axis_index("x"), lax.axis_index("y"), lax.axis_index("c")
        me = 4 * x + 2 * y + c

        def peer(k):
            bx, by, bc = (k >> 2) & 1, (k >> 1) & 1, k & 1
            px, py, pc = (x + bx) % 2, (y + by) % 2, (c + bc) % 2
            return (px, py, pc), 4 * px + 2 * py + pc

        def copy(a, k, landing):
            dev, idx = peer(k)
            return pltpu.make_async_remote_copy(
                src_ref=src_refs[a] if gathers[a] else src_refs[a].at[idx], dst_ref=out_refs[a].at[idx if landing else me],
                send_sem=send_sems.at[a, k - 1], recv_sem=recv_sems.at[a, k - 1],
                device_id=dev, device_id_type=pl.DeviceIdType.MESH)

        mine = [pltpu.make_async_copy(src_refs[a] if gathers[a] else src_refs[a].at[me], out_refs[a].at[me], local_sems.at[a])
                for a in range(n)]
        for cp in mine:
            cp.start()
        sends = [copy(a, k, False) for a in range(n) for k in range(1, N_DEV)]
        for cp in sends:
            cp.start()
        for a in range(n):
            for k in range(1, N_DEV):
                copy(a, k, True).wait_recv()
        for cp in sends:
            cp.wait_send()
        for cp in mine:
            cp.wait()

    hbm = pl.BlockSpec(memory_space=pl.ANY)
    return pl.pallas_call(
        body, name=name, out_shape=[_sds((N_DEV,) + shp, s.dtype) for shp, s in zip(shapes, srcs)], in_specs=[hbm] * n,
        out_specs=[hbm] * n,
        scratch_shapes=[pltpu.SemaphoreType.DMA((n, N_DEV - 1)), pltpu.SemaphoreType.DMA((n, N_DEV - 1)),
                        pltpu.SemaphoreType.DMA((n,))],
        compiler_params=pltpu.CompilerParams(has_side_effects=True))(*srcs)


def gather_two_level(srcs, name):
    n = len(srcs)

    def body(*refs):
        src_refs, out_refs = refs[:n], refs[n:2 * n]
        send_sems, recv_sems, local_sems = refs[2 * n:]
        x, y, c = lax.axis_index("x"), lax.axis_index("y"), lax.axis_index("c")
        me, sibling = (x, y, c), (x, y, 1 - c)
        chips = [(1 - x, y), (x, 1 - y), (1 - x, 1 - y)]

        def slot(a, px, py, pc):
            return out_refs[a].at[4 * px + 2 * py + pc]

        def copy(a, k, block, to, src=None):
            return pltpu.make_async_remote_copy(
                src_ref=slot(a, *block) if src is None else src, dst_ref=slot(a, *block), send_sem=send_sems.at[a, k],
                recv_sem=recv_sems.at[a, k], device_id=to, device_id_type=pl.DeviceIdType.MESH)

        mine = [pltpu.make_async_copy(src_refs[a], slot(a, *me), local_sems.at[a]) for a in range(n)]
        for cp in mine:
            cp.start()
        first = []
        for a in range(n):
            first += [copy(a, 0, me, sibling, src=src_refs[a])]
            first += [copy(a, 1 + j, me, (*chip, c), src=src_refs[a]) for j, chip in enumerate(chips)]
        for cp in first:
            cp.start()
        passed = []
        for a in range(n):
            for j, chip in enumerate(chips):
                copy(a, 1 + j, (*chip, c), me).wait_recv()
                passed.append(copy(a, 4 + j, (*chip, c), sibling))
                passed[-1].start()
        for a in range(n):
            copy(a, 0, sibling, me).wait_recv()
            for j, chip in enumerate(chips):
                copy(a, 4 + j, (*chip, 1 - c), me).wait_recv()
        for cp in first + passed:
            cp.wait_send()
        for cp in mine:
            cp.wait()

    hbm = pl.BlockSpec(memory_space=pl.ANY)
    return pl.pallas_call(
        body, name=name, out_shape=[_sds((N_DEV,) + tuple(s.shape), s.dtype) for s in srcs], in_specs=[hbm] * n,
        out_specs=[hbm] * n,
        scratch_shapes=[pltpu.SemaphoreType.DMA((n, N_DEV - 1)), pltpu.SemaphoreType.DMA((n, N_DEV - 1)),
                        pltpu.SemaphoreType.DMA((n,))],
        compiler_params=pltpu.CompilerParams(has_side_effects=True))(*srcs)


def _peer(k):
    x, y, c = lax.axis_index("x"), lax.axis_index("y"), lax.axis_index("c")
    px, py, pc = (x + ((k >> 2) & 1)) % 2, (y + ((k >> 1) & 1)) % 2, (c + (k & 1)) % 2
    return (px, py, pc), 4 * px + 2 * py + pc


def _my_slot():
    return 4 * lax.axis_index("x") + 2 * lax.axis_index("y") + lax.axis_index("c")


_HBM = pl.BlockSpec(memory_space=pltpu.HBM)
_SEM = pl.BlockSpec(memory_space=pltpu.SEMAPHORE)
_EFFECT = pltpu.SideEffectType.DATAFLOW_SIDE_EFFECTING


def exchange_start(srcs, name, gather):
    n = len(srcs)
    shapes = [tuple(s.shape) if gather else tuple(s.shape[1:]) for s in srcs]
    lands = [lax.empty((N_DEV,) + shp, s.dtype) for shp, s in zip(shapes, srcs)]

    def body(*refs):
        src_refs, land_refs = refs[:n], refs[n:2 * n]
        sems = refs[2 * n:4 * n]
        token = refs[-1]
        me = _my_slot()
        for a in range(n):
            for k in range(1, N_DEV):
                dev, idx = _peer(k)
                pltpu.make_async_remote_copy(
                    src_ref=src_refs[a] if gather else src_refs[a].at[idx], dst_ref=land_refs[a].at[me],
                    send_sem=sems[2 * a].at[k - 1], recv_sem=sems[2 * a + 1].at[k - 1],
                    device_id=dev, device_id_type=pl.DeviceIdType.MESH).start()
        token[...] = jnp.zeros_like(token)

    out_shape = ([pltpu.SemaphoreType.DMA((N_DEV - 1,))] * (2 * n) + [pltpu.HBM(s.shape, s.dtype) for s in srcs]
                 + [pltpu.HBM(l.shape, l.dtype) for l in lands] + [_sds((8, LANE), F32)])
    out = pl.pallas_call(
        body, name=name, out_shape=out_shape, in_specs=[_HBM] * (2 * n),
        out_specs=[_SEM] * (2 * n) + [_HBM] * (2 * n) + [pl.BlockSpec(memory_space=pltpu.VMEM)],
        input_output_aliases={i: 2 * n + i for i in range(2 * n)},
        compiler_params=pltpu.CompilerParams(has_side_effects=_EFFECT))(
            *[pltpu.with_memory_space_constraint(s, pltpu.HBM) for s in srcs],
            *[pltpu.with_memory_space_constraint(l, pltpu.HBM) for l in lands])
    parts = [(out[2 * a], out[2 * a + 1], out[2 * n + a], out[3 * n + a]) for a in range(n)]
    return parts, out[-1]


def exchange_wait(parts, after, name, gather):
    n = len(parts)

    def body(*refs):
        src_refs, land_refs = refs[:n], refs[n:2 * n]
        sems = refs[2 * n:4 * n]
        for a in range(n):
            for k in range(1, N_DEV):
                dev, idx = _peer(k)
                copy = pltpu.make_async_remote_copy(
                    src_ref=src_refs[a] if gather else src_refs[a].at[idx], dst_ref=land_refs[a].at[idx],
                    send_sem=sems[2 * a].at[k - 1], recv_sem=sems[2 * a + 1].at[k - 1],
                    device_id=dev, device_id_type=pl.DeviceIdType.MESH)
                copy.wait_send()
                copy.wait_recv()

    srcs = [p[2] for p in parts]
    lands = [p[3] for p in parts]
    sems = [s for p in parts for s in p[:2]]
    out = pl.pallas_call(
        body, name=name, out_shape=[pltpu.HBM(a.shape, a.dtype) for a in srcs + lands],
        in_specs=[_HBM] * (2 * n) + [_SEM] * (2 * n) + [pl.BlockSpec(memory_space=pl.ANY)], out_specs=[_HBM] * (2 * n),
        input_output_aliases={i: i for i in range(2 * n)},
        compiler_params=pltpu.CompilerParams(has_side_effects=_EFFECT))(*srcs, *lands, *sems, after)
    return list(zip(out[:n], out[n:]))


def _cols_to_slabs(g):
    r, c = g.shape
    return g.reshape(r, N_DEV, c // N_DEV).transpose(1, 0, 2)


def _slabs_to_cols(s):
    _, r, cs = s.shape
    return s.transpose(1, 0, 2).reshape(r, N_DEV * cs)


def _rep_heads(v):
    return jnp.repeat(v.reshape(N_HEADS), HEAD).reshape(1, D_SSD)


def local_fwd_bwd(x, target, mod, get_w, put_grad, small):
    n1w, n2w, fnw = small["norm1_w"], small["norm2_w"], small["final_norm_w"]
    dtb_f, alog_f, dsk_f = _rep_heads(small["dt_bias"]), _rep_heads(small["a_log"]), _rep_heads(small["d_skip"])
    snw = small["ssd_norm_w"]

    def after(v, token):
        return v + token[0:1, 0:1]

    w_in = get_w("w_in", mod)
    h1, proj_zx, proj_dt, proj_cf = in_proj(x, mod, n1w, w_in["w_full"], w_in["w_dt_rep"], w_in["w_cf"], "norm1_in_proj")
    xbc = conv_silu_fwd(proj_zx, D_SSD // CB, D_XBC, small["ssd_conv_w"], small["ssd_conv_b"], "ssd_conv")
    y, ysn, s_prev = ssd_fwd(xbc, proj_zx, proj_dt, dtb_f, alog_f, dsk_f, snw, "ssd_scan")
    uc = conf_conv_fwd(proj_cf, 0, D_CONF // CB, small["conf_conv_w"], small["conf_conv_b"], "conf_conv")
    w_out = get_w("w_out", uc)
    mix, u, x1, h2 = mixer_out(ysn, uc, small["conf_ln_w"], small["conf_ln_b"], w_out, x, mod, n2w, "out_proj_norm2")
    w_up_t = get_w("w_up", h2)
    up = mm_nt([(h2, w_up_t, 0)], "up_proj")
    act = ffn_conv_fwd(up, small["ffn_conv_w"], small["ffn_conv_b"], "ffn_conv")
    w_down = get_w("w_down", act)
    dx2, dff, dact, acc_f = final_loss(act, w_down, x1, mod, fnw, target, "down_proj_loss")

    token = put_grad("w_down", mm_tn(act, dff, "wgrad_down"))
    dupg, dupv, dwg, dwv = ffn_conv_bwd(up, small["ffn_conv_w"], after(small["ffn_conv_b"], token), dact, "ffn_conv_bwd")
    token = put_grad("w_up", mm_tn_stack([dupg, dupv], h2, "wgrad_up"))
    dx1, dmix, acc_2 = norm_mod_bwd([(dupg, w_up_t, 0), (dupv, w_up_t, 1)], x1, dx2, mod, after(n2w, token), 3, "norm2_bwd",
                                    mix=mix, gate_row=2)

    token = put_grad("w_out", mm_tn_stack([ysn, u], dmix, "wgrad_out"))
    dysn, duc, acc_ln = mixer_out_bwd(dmix, w_out, uc, after(small["conf_ln_w"], token), small["conf_ln_b"], "out_proj_bwd")
    dcfa, dcfg, dw_cc = conf_conv_bwd(proj_cf, 0, D_CONF // CB, small["conf_conv_w"], duc, "conf_conv_bwd")
    g_w_in_conf = mm_tn_stack([dcfa, dcfg], h1, "wgrad_in_conf")
    dz, ddt, dxbc_post, acc_s, acc_s16 = ssd_bwd(dysn, y, xbc, proj_zx, proj_dt, s_prev, dtb_f, alog_f, dsk_f, snw,
                                                 "ssd_scan_bwd")
    dxbc, dw_sc = conv_silu_bwd(proj_zx, D_SSD // CB, D_XBC, small["ssd_conv_w"], small["ssd_conv_b"], dxbc_post, "ssd_conv_bwd")
    token = put_grad("w_in", mm_tn_concat([(dz, D_SSD), (dxbc, D_XBC), (ddt, N_HEADS)], h1, g_w_in_conf, "wgrad_in"))
    dh1_pairs = [(dz, w_in["w_full"], 0), (ddt, w_in["w_dt16"], 0), (dcfa, w_in["w_cf"], 0), (dcfg, w_in["w_cf"], 1),
                 (dxbc, w_in["w_xbc"], 0)]
    grad_x, acc_1 = norm_mod_bwd(dh1_pairs, x, dx1, mod, after(n1w, token), 0, "norm1_bwd")

    small_accs = dict(acc_1=acc_1, acc_2=acc_2, acc_f=acc_f, acc_ln=acc_ln, acc_s=acc_s, acc_s16=acc_s16, dw_sc=dw_sc,
                      dw_cc=dw_cc, dwg=dwg, dwv=dwv)
    return grad_x, small_accs


def kernel(x, c, ada_w, ada_b, norm1_w, w_in, ssd_conv_w, ssd_conv_b, dt_bias, a_log, d_skip, ssd_norm_w, conf_conv_w, conf_conv_b, conf_ln_w, conf_ln_b, w_out, norm2_w, w_up, ffn_conv_w, ffn_conv_b, w_down, final_norm_w, loss_target, m_ada_w, m_ada_b, m_norm1_w, m_w_in, m_ssd_conv_w, m_ssd_conv_b, m_dt_bias, m_a_log, m_d_skip, m_ssd_norm_w, m_conf_conv_w, m_conf_conv_b, m_conf_ln_w, m_conf_ln_b, m_w_out, m_norm2_w, m_w_up, m_ffn_conv_w, m_ffn_conv_b, m_w_down, m_final_norm_w, v_ada_w, v_ada_b, v_norm1_w, v_w_in, v_ssd_conv_w, v_ssd_conv_b, v_dt_bias, v_a_log, v_d_skip, v_ssd_norm_w, v_conf_conv_w, v_conf_conv_b, v_conf_ln_w, v_conf_ln_b, v_w_out, v_norm2_w, v_w_up, v_ffn_conv_w, v_ffn_conv_b, v_w_down, v_final_norm_w):
    me = 4 * lax.axis_index("x") + 2 * lax.axis_index("y") + lax.axis_index("c")
    weights = dict(ada_w=ada_w, ada_b=ada_b, norm1_w=norm1_w, w_in=w_in, ssd_conv_w=ssd_conv_w, ssd_conv_b=ssd_conv_b,
                   dt_bias=dt_bias, a_log=a_log, d_skip=d_skip, ssd_norm_w=ssd_norm_w, conf_conv_w=conf_conv_w,
                   conf_conv_b=conf_conv_b, conf_ln_w=conf_ln_w, conf_ln_b=conf_ln_b, w_out=w_out, norm2_w=norm2_w, w_up=w_up,
                   ffn_conv_w=ffn_conv_w, ffn_conv_b=ffn_conv_b, w_down=w_down, final_norm_w=final_norm_w)
    moms_m = dict(ada_w=m_ada_w, ada_b=m_ada_b, norm1_w=m_norm1_w, w_in=m_w_in, ssd_conv_w=m_ssd_conv_w, ssd_conv_b=m_ssd_conv_b,
                  dt_bias=m_dt_bias, a_log=m_a_log, d_skip=m_d_skip, ssd_norm_w=m_ssd_norm_w, conf_conv_w=m_conf_conv_w,
                  conf_conv_b=m_conf_conv_b, conf_ln_w=m_conf_ln_w, conf_ln_b=m_conf_ln_b, w_out=m_w_out, norm2_w=m_norm2_w,
                  w_up=m_w_up, ffn_conv_w=m_ffn_conv_w, ffn_conv_b=m_ffn_conv_b, w_down=m_w_down, final_norm_w=m_final_norm_w)
    moms_v = dict(ada_w=v_ada_w, ada_b=v_ada_b, norm1_w=v_norm1_w, w_in=v_w_in, ssd_conv_w=v_ssd_conv_w, ssd_conv_b=v_ssd_conv_b,
                  dt_bias=v_dt_bias, a_log=v_a_log, d_skip=v_d_skip, ssd_norm_w=v_ssd_norm_w, conf_conv_w=v_conf_conv_w,
                  conf_conv_b=v_conf_conv_b, conf_ln_w=v_conf_ln_w, conf_ln_b=v_conf_ln_b, w_out=v_w_out, norm2_w=v_norm2_w,
                  w_up=v_w_up, ffn_conv_w=v_ffn_conv_w, ffn_conv_b=v_ffn_conv_b, w_down=v_w_down, final_norm_w=v_final_norm_w)
    names = list(weights)

    def to2d(a):
        return a[0] if a.ndim == 3 else a.reshape(1, -1)

    big = ("w_in", "w_out", "w_up", "w_down")

    def rows_of(a):
        return jnp.swapaxes(a, 1, 2)[0] if a.shape[2] != D else a[0]

    shards = [rows_of(weights[n]).astype(BF16) for n in big]

    c_all, scw_all, ccw_all, fcw_all, w_in_slabs = gather_two_level(
        [c.reshape(8, LANE), ssd_conv_w[0], conf_conv_w[0], ffn_conv_w[0], shards[0]], "gather_first")
    c_all = c_all.reshape(N_DEV, D)

    ada_cols = ada_w.shape[2]
    ada_b_cols = lax.dynamic_slice(ada_b, (0, me * ada_cols), (1, ada_cols))
    mod_cols, c_act_all = ada_mod(c_all, ada_w[0], ada_b_cols, "ada_mod")
    mod_parts, = exchange([jnp.pad(mod_cols, ((0, 0), (0, D - ada_cols))).reshape(N_DEV, 8, LANE)], "scatter_mod", gather=False)
    mod = mod_parts.reshape(N_DEV, D)[:, :ada_cols].reshape(6, D)
    mod = jnp.pad(mod, ((0, 2), (0, 0)))

    later, mod = lax.optimization_barrier((shards[1:], mod))
    gather_parts, token = exchange_start(later, "gather_weights_start", gather=True)
    mod = mod + token[0:1, 0:1]

    small = {n: to2d(weights[n]) for n in names if n not in ("ada_w",) + big}
    small["ssd_conv_w"] = _slabs_to_cols(scw_all)
    small["conf_conv_w"] = _slabs_to_cols(ccw_all)
    small["ffn_conv_w"] = _slabs_to_cols(fcw_all)

    def with_own(landed, own):
        return lax.dynamic_update_slice(landed, own[None], (me,) + (0,) * own.ndim)

    def get_w(n, after):
        if n == "w_in":
            slabs = w_in_slabs
        else:
            a = big.index(n)
            (own, landed), = exchange_wait([gather_parts[a - 1]], after, "gather_" + n + "_wait", gather=True)
            slabs = with_own(landed, own)
        full = slabs.reshape(N_DEV * slabs.shape[1], D)
        if n != "w_in":
            return full
        w_dt = full[D_SSD + D_XBC:D_SSD + D_XBC + N_HEADS]
        return dict(w_full=full, w_xbc=full[D_SSD:D_SSD + D_XBC], w_cf=full[D_SSD + D_XBC + N_HEADS:],
                    w_dt_rep=jnp.repeat(w_dt, HEAD, axis=0), w_dt16=jnp.pad(w_dt, ((0, LANE - N_HEADS), (0, 0))))

    scatter_parts = {}

    def put_grad(n, g):
        slabs = g if g.ndim == 3 else g.reshape(N_DEV, g.shape[0] // N_DEV, g.shape[1])
        (scatter_parts[n],), token = exchange_start([slabs.astype(BF16)], "scatter_" + n + "_start", gather=False)
        return token

    grad_x, accs = local_fwd_bwd(x[0], loss_target[0], mod, get_w, put_grad, small)

    grads, delta, new_m, new_v = {}, {}, {}, {}

    def finish(ns, after, name):
        landed = exchange_wait([scatter_parts[n] for n in ns], after, name, gather=False)
        for n, (sent, slots) in zip(ns, landed):
            slots = with_own(slots, lax.dynamic_index_in_dim(sent, me, 0, keepdims=False))
            out = adamw_slots(rows_of(weights[n]), slots, rows_of(moms_m[n]), rows_of(moms_v[n]), "adamw_" + n)
            if weights[n].shape[2] != D:
                out = [jnp.swapaxes(o, 0, 1) for o in out]
            grads[n], delta[n], new_m[n], new_v[n] = out

    finish(big[1:], grad_x, "scatter_grads_wait")

    accs = dict(zip(accs, lax.optimization_barrier((list(accs.values()), [new_v[n] for n in big[1:]]))[0]))
    rep = (("acc_1", 0, 3), ("acc_2", 0, 4), ("acc_f", 0, 3), ("acc_ln", 0, 2), ("acc_s", 0, 1), ("acc_s16", 1, 3),
           ("dw_sc", K_SSD, 1), ("dw_cc", K_CONF, 1), ("dwg", K_FFN, 1), ("dwv", K_FFN, 1))
    shapes = [(rows, accs[k].shape[1]) for k, _, rows in rep]
    conv_slabs = [_cols_to_slabs(accs["dw_sc"][:K_SSD]), _cols_to_slabs(accs["dw_cc"][:K_CONF]),
                  _cols_to_slabs(jnp.concatenate([accs["dwg"][:K_FFN], accs["dwv"][:K_FFN]], axis=1))]
    packed = pack_rows([(accs[k], first, rows) for k, first, rows in rep], "pack_small_grads")
    landed = exchange([packed] + conv_slabs, "exchange_small_grads", gather=[True, False, False, False])
    packed_red, g_scw, g_ccw, g_fcw = sum_slots_many(landed, "sum_small_grads")
    a1_all, a2_all, af_all = unpack_rows(landed[0], shapes)[:3]
    r1, r2, rf, rln, rs, r16, rscb, rccb, rfbg, rfbv = unpack_rows(packed_red, shapes)
    loss = 0.5 / D * jnp.sum(rf[2:3])

    def mod_rows(a1, a2, af):
        return jnp.concatenate([a1[..., 0:2, :], a2[..., 3:4, :], a2[..., 0:2, :], af[..., 1:2, :]], axis=-2)

    dmod_all = mod_rows(a1_all, a2_all, af_all).reshape(N_DEV, 6 * D)
    grads["ada_w"] = ada_wgrad(c_act_all, lax.dynamic_slice(dmod_all, (0, me * ada_cols), (N_DEV, ada_cols)), "ada_wgrad")
    grads.update(
        ada_b=mod_rows(r1, r2, rf).reshape(1, 6 * D), norm1_w=r1[2:3], ssd_conv_w=g_scw, ssd_conv_b=rscb,
        dt_bias=r16[0:1, :N_HEADS], a_log=r16[1:2, :N_HEADS], d_skip=r16[2:3, :N_HEADS], ssd_norm_w=rs,
        conf_conv_w=g_ccw, conf_conv_b=rccb, conf_ln_w=rln[0:1], conf_ln_b=rln[1:2], norm2_w=r2[2:3],
        ffn_conv_w=g_fcw, ffn_conv_b=jnp.concatenate([rfbg, rfbv], axis=1), final_norm_w=rf[0:1])

    rest = [n for n in names if n not in big]
    d_l, m_l, v_l = adamw_many([to2d(weights[n]) for n in rest], [grads[n] for n in rest], [to2d(moms_m[n]) for n in rest],
                               [to2d(moms_v[n]) for n in rest], "adamw_small")
    for n, dd, mm, vv in zip(rest, d_l, m_l, v_l):
        delta[n], new_m[n], new_v[n] = dd, mm, vv
    finish(big[:1], d_l[0], "scatter_w_in_wait")
    shape_of = lambda d_: {n: d_[n].reshape(weights[n].shape) for n in names}
    grads, delta, new_m, new_v = shape_of(grads), shape_of(delta), shape_of(new_m), shape_of(new_v)
    return (loss, grad_x[None], *[grads[n] for n in names], *[delta[n] for n in names], *[new_m[n] for n in names],
            *[new_v[n] for n in names])
```

```python
import functools

import jax
import jax.numpy as jnp
from jax import lax
from jax.experimental import pallas as pl
from jax.experimental.pallas import tpu as pltpu

F32 = jnp.float32
BF16 = jnp.bfloat16
HI = lax.Precision.HIGHEST

N_DEV = 8
D = 1024
D_SSD = 1024
HEAD = 64
N_HEADS = 16
N_STATE = 128
D_XBC = 1536
D_CONF = 1024
D_FF = 2816
K_SSD, K_CONF, K_FFN = 4, 31, 3
LANE = 128
TR = 256
TM = 512
Q = 256
CB = 256
TC = 2048
VMEM_LIMIT = 56 * 1024 * 1024

ADAM_LR, ADAM_B1, ADAM_B2, ADAM_EPS, ADAM_WD, ADAM_STEP = 0.001, 0.9, 0.999, 1e-08, 0.01, 10


def _cparams(sem=None):
    return pltpu.CompilerParams(vmem_limit_bytes=VMEM_LIMIT, dimension_semantics=sem)


def _sds(shape, dtype):
    return jax.ShapeDtypeStruct(shape, dtype)


def _sigmoid(x):
    return 1.0 / (1.0 + jnp.exp(-x))


def _silu(x):
    return x * _sigmoid(x)


def _dsilu(x):
    s = _sigmoid(x)
    return s * (1.0 + x * (1.0 - s))


def _softplus(x):
    return jnp.maximum(x, 0.0) + jnp.log(1.0 + jnp.exp(-jnp.abs(x)))


def _dot(a, b):
    return jnp.dot(a.astype(BF16), b.astype(BF16), preferred_element_type=F32)


def _dot_nt(a, b):
    return lax.dot_general(a.astype(BF16), b.astype(BF16), (((1,), (1,)), ((), ())), preferred_element_type=F32)


def _dot_tn(a, b):
    return lax.dot_general(a.astype(BF16), b.astype(BF16), (((0,), (0,)), ((), ())), preferred_element_type=F32)


def _bf16_terms(a, terms):
    parts, rem = [], a
    for t in range(terms):
        p = rem.astype(BF16)
        parts.append(p)
        if t + 1 < terms:
            rem = rem - p.astype(F32)
    return parts


def _dot_exact(a, b, terms, exact, dims=(((1,), (0,)), ((), ()))):
    if exact == "a":
        a_b = a.astype(BF16)
        outs = [lax.dot_general(a_b, p, dims, preferred_element_type=F32) for p in _bf16_terms(b, terms)]
    else:
        b_b = b.astype(BF16)
        outs = [lax.dot_general(p, b_b, dims, preferred_element_type=F32) for p in _bf16_terms(a, terms)]
    acc = outs[-1]
    for o in reversed(outs[:-1]):
        acc = acc + o
    return acc


def _dot_tn_hi(a, b):
    return lax.dot_general(a, b, (((0,), (0,)), ((), ())), precision=HI, preferred_element_type=F32)


def _colsum(x):
    return jnp.sum(x, axis=0, keepdims=True)


def _const_spec(shape):
    return pl.BlockSpec(shape, lambda *_: (0,) * len(shape))


def _col_tile(n):
    for t in (2816, 1408, 1024, 768, 512, 256, 128):
        if n % t == 0 and t <= n:
            return t
    return n


def mm_nt(pairs, name):
    L = pairs[0][0].shape[0]
    K = pairs[0][1].shape[0]
    tk = _col_tile(K)
    n = len(pairs)

    def body(*refs):
        o_ref = refs[-1]
        acc = None
        for p in range(n):
            t = lax.dot_general(refs[2 * p][...], refs[2 * p + 1][...], (((1,), (1,)), ((), ())),
                                preferred_element_type=F32)
            acc = t if acc is None else acc + t
        o_ref[...] = acc

    in_specs, args = [], []
    for a, w, cb in pairs:
        in_specs += [pl.BlockSpec((TM, a.shape[1]), lambda j, i: (i, 0)),
                     pl.BlockSpec((tk, a.shape[1]), functools.partial(lambda j, i, cb: (j, cb), cb=cb))]
        args += [a, w]
    return pl.pallas_call(
        body, name=name, grid=(K // tk, L // TM), out_shape=_sds((L, K), F32), in_specs=in_specs,
        out_specs=pl.BlockSpec((TM, tk), lambda j, i: (i, j)),
        compiler_params=_cparams(("parallel", "parallel")))(*args)


def mm_tn(a, g, name):
    L, M = a.shape
    N = g.shape[1]
    tn = _col_tile(N) if N > 1024 else N
    if M * tn * 4 > 12 * 1024 * 1024:
        tn = 512
    tl = 512 if L % 512 == 0 else TR
    nl = L // tl

    def body(a_ref, g_ref, o_ref, acc_ref):
        @pl.when(pl.program_id(1) == 0)
        def _():
            acc_ref[...] = jnp.zeros((M, tn), F32)

        acc_ref[...] += lax.dot_general(a_ref[...], g_ref[...], (((0,), (0,)), ((), ())), preferred_element_type=F32)

        @pl.when(pl.program_id(1) == nl - 1)
        def _():
            o_ref[...] = acc_ref[...].astype(BF16)

    return pl.pallas_call(
        body, name=name, grid=(N // tn, nl), out_shape=_sds((M, N), BF16),
        in_specs=[pl.BlockSpec((tl, M), lambda j, l: (l, 0)), pl.BlockSpec((tl, tn), lambda j, l: (l, j))],
        out_specs=pl.BlockSpec((M, tn), lambda j, l: (0, j)), scratch_shapes=[pltpu.VMEM((M, tn), F32)],
        compiler_params=_cparams(("parallel", "arbitrary")))(a, g)


def mm_tn_stack(a_list, g, name):
    L, M = a_list[0].shape
    N = g.shape[1]
    n = len(a_list)
    tl = 512 if L % 512 == 0 else TR
    nl = L // tl

    def body(*refs):
        a_refs, g_ref, o_ref, acc_ref = refs[:n], refs[n], refs[n + 1], refs[n + 2]
        j, l = pl.program_id(0), pl.program_id(1)

        @pl.when(l == 0)
        def _():
            acc_ref[...] = jnp.zeros((M, N), F32)

        for p in range(n):
            @pl.when(j == p)
            def _(p=p):
                acc_ref[...] += lax.dot_general(a_refs[p][...], g_ref[...], (((0,), (0,)), ((), ())), preferred_element_type=F32)

        @pl.when(l == nl - 1)
        def _():
            o_ref[...] = acc_ref[...].astype(BF16)

    a_specs = [pl.BlockSpec((tl, M), functools.partial(lambda j, l, p: (jnp.where(j == p, l, 0), 0), p=p)) for p in range(n)]
    return pl.pallas_call(
        body, name=name, grid=(n, nl), out_shape=_sds((n * M, N), BF16),
        in_specs=a_specs + [pl.BlockSpec((tl, N), lambda j, l: (l, 0))],
        out_specs=pl.BlockSpec((M, N), lambda j, l: (j, 0)), scratch_shapes=[pltpu.VMEM((M, N), F32)],
        compiler_params=_cparams(("arbitrary", "arbitrary")))(*a_list, g)


def mm_tn_concat(pieces, g, name):
    L = g.shape[0]
    N = g.shape[1]
    n = len(pieces)
    offsets = [sum(r for _, r in pieces[:p]) for p in range(n + 1)]
    slab = offsets[-1] // N_DEV
    tl = 512 if L % 512 == 0 else TR
    nl = L // tl

    def body(*refs):
        a_refs, g_ref, o_ref, acc_ref = refs[:n], refs[n], refs[n + 1], refs[n + 2]
        l = pl.program_id(0)

        @pl.when(l == 0)
        def _():
            acc_ref[...] = jnp.zeros((offsets[-1], N), F32)

        g_v = g_ref[...]
        for p in range(n):
            t = lax.dot_general(a_refs[p][...], g_v, (((0,), (0,)), ((), ())), preferred_element_type=F32)
            acc_ref[offsets[p]:offsets[p + 1], :] += t[:pieces[p][1], :]

        @pl.when(l == nl - 1)
        def _():
            for s in range(N_DEV):
                o_ref[s] = acc_ref[s * slab:(s + 1) * slab, :].astype(BF16)

    return pl.pallas_call(
        body, name=name, grid=(nl,), out_shape=_sds((N_DEV, slab, N), BF16),
        in_specs=[pl.BlockSpec((tl, a.shape[1]), lambda l: (l, 0)) for a, _ in pieces] + [pl.BlockSpec((tl, N), lambda l: (l, 0))],
        out_specs=_const_spec((N_DEV, slab, N)), scratch_shapes=[pltpu.VMEM((offsets[-1], N), F32)],
        compiler_params=_cparams(("arbitrary",)))(*[a for a, _ in pieces], g)


def _row_spec(width=D):
    return pl.BlockSpec((TR, width), lambda i: (i, 0))


def in_proj(x, mod, n1w, w_full, w_dt_rep, w_cf, name):
    L = x.shape[0]
    n_zx = D_SSD + D_XBC

    def body(x_ref, mod_ref, w_ref, wzx_ref, wdt_ref, wcf_ref, h_ref, zx_ref, dt_ref, cf_ref):
        xin = x_ref[...]
        r = lax.rsqrt(jnp.mean(xin * xin, axis=-1, keepdims=True) + 1e-6)
        h = ((xin * r * w_ref[...]) * (1.0 + mod_ref[1:2, :]) + mod_ref[0:1, :]).astype(BF16)
        h_ref[...] = h
        nt = (((1,), (1,)), ((), ()))
        zx_ref[...] = lax.dot_general(h, wzx_ref[...], nt, preferred_element_type=F32)
        dt_ref[...] = lax.dot_general(h, wdt_ref[...], nt, preferred_element_type=F32)
        cf_ref[...] = lax.dot_general(h, wcf_ref[...], nt, preferred_element_type=F32)

    row = lambda w: pl.BlockSpec((TM, w), lambda i: (i, 0))
    return pl.pallas_call(
        body, name=name, grid=(L // TM,),
        out_shape=[_sds((L, D), BF16), _sds((L, n_zx), F32), _sds((L, D_SSD), F32), _sds((L, 2 * D_CONF), F32)],
        in_specs=[row(D), _const_spec((8, D)), _const_spec((1, D)), _const_spec((n_zx, D)), _const_spec((D_SSD, D)),
                  _const_spec((2 * D_CONF, D))],
        out_specs=[row(D), row(n_zx), row(D_SSD), row(2 * D_CONF)],
        compiler_params=_cparams(("parallel",)))(x, mod, n1w, w_full, w_dt_rep, w_cf)


def mixer_out(ysn, uc, lnw, lnb, w_out, x, mod, n2w, name):
    L = x.shape[0]

    def body(ysn_ref, uc_ref, lnw_ref, lnb_ref, wo_ref, x_ref, mod_ref, n2w_ref, mix_ref, u_ref, x1_ref, h2_ref):
        uc_v = uc_ref[...]
        mu = jnp.mean(uc_v, axis=-1, keepdims=True)
        var = jnp.mean(jnp.square(uc_v - mu), axis=-1, keepdims=True)
        u = _silu((uc_v - mu) * lax.rsqrt(var + 1e-5) * lnw_ref[...] + lnb_ref[...]).astype(BF16)
        u_ref[...] = u
        mix = (jnp.dot(ysn_ref[...], wo_ref[0:D_SSD, :], preferred_element_type=F32)
               + jnp.dot(u, wo_ref[D_SSD:D_SSD + D_CONF, :], preferred_element_type=F32))
        mix_ref[...] = mix
        x1 = x_ref[...] + mod_ref[2:3, :] * mix
        x1_ref[...] = x1
        r = lax.rsqrt(jnp.mean(x1 * x1, axis=-1, keepdims=True) + 1e-6)
        h2_ref[...] = ((x1 * r * n2w_ref[...]) * (1.0 + mod_ref[4:5, :]) + mod_ref[3:4, :]).astype(BF16)

    row = pl.BlockSpec((TM, D), lambda i: (i, 0))
    return pl.pallas_call(
        body, name=name, grid=(L // TM,),
        out_shape=[_sds((L, D), F32), _sds((L, D_CONF), BF16), _sds((L, D), F32), _sds((L, D), BF16)],
        in_specs=[row, row, _const_spec((1, D)), _const_spec((1, D)), _const_spec((D_SSD + D_CONF, D)), row,
                  _const_spec((8, D)), _const_spec((1, D))],
        out_specs=[row] * 4, compiler_params=_cparams(("parallel",)))(ysn, uc, lnw, lnb, w_out, x, mod, n2w)


def mixer_out_bwd(dmix, w_out, uc, lnw, lnb, name):
    L = uc.shape[0]

    def body(dm_ref, wo_ref, u_ref, w_ref, b_ref, dy_ref, o_ref, acc_ref):
        @pl.when(pl.program_id(0) == 0)
        def _():
            acc_ref[...] = jnp.zeros((8, D), F32)

        nt = (((1,), (1,)), ((), ()))
        dm = dm_ref[...]
        dy_ref[...] = lax.dot_general(dm, wo_ref[0:D_SSD, :], nt, preferred_element_type=F32)
        du = lax.dot_general(dm, wo_ref[D_SSD:D_SSD + D_CONF, :], nt, preferred_element_type=F32)
        u = u_ref[...]
        mu = jnp.mean(u, axis=-1, keepdims=True)
        rl = lax.rsqrt(jnp.mean(jnp.square(u - mu), axis=-1, keepdims=True) + 1e-5)
        n = (u - mu) * rl
        v = n * w_ref[...] + b_ref[...]
        dv = du * _dsilu(v)
        acc_ref[0:1, :] += _colsum(dv * n)
        acc_ref[1:2, :] += _colsum(dv)
        dn = dv * w_ref[...]
        o_ref[...] = rl * (dn - jnp.mean(dn, axis=-1, keepdims=True) - n * jnp.mean(dn * n, axis=-1, keepdims=True))

    row = pl.BlockSpec((TM, D), lambda i: (i, 0))
    return pl.pallas_call(body, name=name, grid=(L // TM,), out_shape=[_sds((L, D), F32), _sds((L, D), F32), _sds((8, D), F32)],
                          in_specs=[row, _const_spec((D_SSD + D_CONF, D)), row, _const_spec((1, D)), _const_spec((1, D))],
                          out_specs=[row, row, _const_spec((8, D))],
                          compiler_params=_cparams(("arbitrary",)))(dmix, w_out, uc, lnw, lnb)


def final_loss(act, w_down, x1, mod, fw, target, name):
    L = act.shape[0]

    def body(act_ref, wd_ref, x1_ref, mod_ref, fw_ref, t_ref, dx_ref, dff_ref, dact_ref, acc_ref):
        @pl.when(pl.program_id(0) == 0)
        def _():
            acc_ref[...] = jnp.zeros((8, D), F32)

        ff_v = jnp.dot(act_ref[...], wd_ref[...], preferred_element_type=F32)
        g2 = mod_ref[5:6, :]
        x2 = x1_ref[...] + g2 * ff_v
        r = lax.rsqrt(jnp.mean(x2 * x2, axis=-1, keepdims=True) + 1e-6)
        n = x2 * r
        err = n * fw_ref[...] - t_ref[...]
        dy = err * (1.0 / D)
        dn = dy * fw_ref[...]
        dx2 = r * (dn - n * jnp.mean(dn * n, axis=-1, keepdims=True))
        acc_ref[0:1, :] += _colsum(dy * n)
        acc_ref[1:2, :] += _colsum(dx2 * ff_v)
        acc_ref[2:3, :] += _colsum(err * err)
        dx_ref[...] = dx2
        dff = (dx2 * g2).astype(BF16)
        dff_ref[...] = dff
        dact_ref[...] = lax.dot_general(dff, wd_ref[...], (((1,), (1,)), ((), ())), preferred_element_type=F32)

    return pl.pallas_call(
        body, name=name, grid=(L // TR,),
        out_shape=[_sds((L, D), F32), _sds((L, D), BF16), _sds((L, D_FF), F32), _sds((8, D), F32)],
        in_specs=[_row_spec(D_FF), _const_spec((D_FF, D)), _row_spec(), _const_spec((8, D)), _const_spec((1, D)), _row_spec()],
        out_specs=[_row_spec(), _row_spec(), _row_spec(D_FF), _const_spec((8, D))],
        compiler_params=_cparams(("arbitrary",)))(act, w_down, x1, mod, fw, target)


def norm_mod_bwd(dh_pairs, xin, dres, mod, w, shift_row, name, mix=None, gate_row=None):
    L = xin.shape[0]
    has_mix = mix is not None
    n_pairs = len(dh_pairs)

    def body(*refs):
        pair_refs, refs = refs[:2 * n_pairs], refs[2 * n_pairs:]
        if has_mix:
            x_ref, dres_ref, mod_ref, w_ref, mix_ref, dx_ref, dmix_ref, acc_ref = refs
        else:
            x_ref, dres_ref, mod_ref, w_ref, dx_ref, acc_ref = refs

        @pl.when(pl.program_id(0) == 0)
        def _():
            acc_ref[...] = jnp.zeros((8, D), F32)

        dh_v = None
        for p in range(n_pairs):
            t = jnp.dot(pair_refs[2 * p][...], pair_refs[2 * p + 1][...], preferred_element_type=F32)
            dh_v = t if dh_v is None else dh_v + t
        x = x_ref[...]
        r = lax.rsqrt(jnp.mean(x * x, axis=-1, keepdims=True) + 1e-6)
        n = x * r
        nw = n * w_ref[...]
        sc1 = 1.0 + mod_ref[shift_row + 1:shift_row + 2, :]
        acc_ref[0:1, :] += _colsum(dh_v)
        acc_ref[1:2, :] += _colsum(dh_v * nw)
        dnw = dh_v * sc1
        acc_ref[2:3, :] += _colsum(dnw * n)
        dn = dnw * w_ref[...]
        dx = r * (dn - n * jnp.mean(dn * n, axis=-1, keepdims=True)) + dres_ref[...]
        dx_ref[...] = dx
        if has_mix:
            acc_ref[3:4, :] += _colsum(dx * mix_ref[...])
            dmix_ref[...] = (dx * mod_ref[gate_row:gate_row + 1, :]).astype(BF16)

    ins, in_specs = [], []
    for a, wt, rb in dh_pairs:
        ins += [a, wt]
        in_specs += [_row_spec(a.shape[1]), pl.BlockSpec((a.shape[1], D), functools.partial(lambda i, rb: (rb, 0), rb=rb))]
    ins += [xin, dres, mod, w] + ([mix] if has_mix else [])
    in_specs += [_row_spec(), _row_spec(), _const_spec((8, D)), _const_spec((1, D))] + ([_row_spec()] if has_mix else [])
    out_shape = [_sds((L, D), F32)] + ([_sds((L, D), BF16)] if has_mix else []) + [_sds((8, D), F32)]
    out_specs = [_row_spec()] + ([_row_spec()] if has_mix else []) + [_const_spec((8, D))]
    return pl.pallas_call(body, name=name, grid=(L // TR,), out_shape=out_shape, in_specs=in_specs,
                          out_specs=out_specs, compiler_params=_cparams(("arbitrary",)))(*ins)


def _halo(k):
    return 8 if k <= 9 else 32


def _prev_spec(h, col0):
    return pl.BlockSpec((h, CB), lambda j, i: (jnp.maximum(i * (TC // h) - 1, 0), j + col0))


def _next_spec(h, col0, n_tiles):
    return pl.BlockSpec((h, CB), lambda j, i: (jnp.minimum(i + 1, n_tiles - 1) * (TC // h), j + col0))


def _tile_spec(col0):
    return pl.BlockSpec((TC, CB), lambda j, i: (i, j + col0))


def _w_spec(kp, col0):
    return pl.BlockSpec((kp, CB), lambda j, i: (0, j + col0))


SUBLANES = 8


def _shifted_windows(v, taps, rows):
    for r in range(SUBLANES):
        group = [(o, k) for o, k in taps if o % SUBLANES == r]
        if not group:
            continue
        s = v if r == 0 else pltpu.roll(v, v.shape[0] - r, 0)
        for o, k in group:
            yield k, s[o - r:o - r + rows, :]


def _causal_taps(ext_ref, w_ref, k_taps, first, rows):
    acc = None
    for k, win in _shifted_windows(ext_ref[...], [(first - (k_taps - 1) + k, k) for k in range(k_taps)], rows):
        t = w_ref[k:k + 1, :] * win
        acc = t if acc is None else acc + t
    return acc


def _anticausal_taps(d_ref, w_ref, k_taps, rows):
    acc = None
    for k, win in _shifted_windows(d_ref[...], [(k_taps - 1 - k, k) for k in range(k_taps)], rows):
        t = w_ref[k:k + 1, :] * win
        acc = t if acc is None else acc + t
    return acc


def _acc_conv_wgrad(dw_ref, d_tile, ext_ref, k_taps, first):
    for k, win in _shifted_windows(ext_ref[...], [(first - (k_taps - 1) + k, k) for k in range(k_taps)], TC):
        dw_ref[k:k + 1, :] += _colsum(d_tile * win)
    dw_ref[k_taps:k_taps + 1, :] += _colsum(d_tile)


def conv_silu_fwd(x, col0, width, w, b, name):
    L = x.shape[0]
    k_taps = w.shape[0]
    h = _halo(k_taps)

    def body(xp_ref, x_ref, w_ref, b_ref, o_ref, ext_ref):
        i = pl.program_id(1)
        ext_ref[0:h, :] = jnp.where(i > 0, xp_ref[...], 0.0)
        ext_ref[h:h + TC, :] = x_ref[...]
        o_ref[...] = _silu(_causal_taps(ext_ref, w_ref, k_taps, h, TC) + b_ref[...])

    return pl.pallas_call(
        body, name=name, grid=(width // CB, L // TC), out_shape=_sds((L, width), F32),
        in_specs=[_prev_spec(h, col0), _tile_spec(col0), _w_spec(k_taps, 0), pl.BlockSpec((1, CB), lambda j, i: (0, j))],
        out_specs=_tile_spec(0), scratch_shapes=[pltpu.VMEM((h + TC, CB), F32)],
        compiler_params=_cparams(("parallel", "parallel")))(x, x, w, b)


def conv_silu_bwd(x, col0, width, w, b, dpost, name):
    L = x.shape[0]
    k_taps = w.shape[0]
    h = _halo(k_taps)
    nt = L // TC

    def body(xp_ref, x_ref, xn_ref, d_ref, dn_ref, w_ref, b_ref, dx_ref, dw_ref, ext_ref, dpre_ref):
        i = pl.program_id(1)

        @pl.when(i == 0)
        def _():
            dw_ref[...] = jnp.zeros((8, CB), F32)

        ext_ref[0:h, :] = jnp.where(i > 0, xp_ref[...], 0.0)
        ext_ref[h:h + TC, :] = x_ref[...]
        ext_ref[h + TC:h + TC + h, :] = xn_ref[...]
        pre = _causal_taps(ext_ref, w_ref, k_taps, h, TC + h) + b_ref[...]
        dpre_ref[0:TC, :] = d_ref[...] * _dsilu(pre[0:TC, :])
        dpre_ref[TC:TC + h, :] = jnp.where(i < nt - 1, dn_ref[...], 0.0) * _dsilu(pre[TC:TC + h, :])
        dx_ref[...] = _anticausal_taps(dpre_ref, w_ref, k_taps, TC).astype(BF16)
        _acc_conv_wgrad(dw_ref, dpre_ref[0:TC, :], ext_ref, k_taps, h)

    return pl.pallas_call(
        body, name=name, grid=(width // CB, nt),
        out_shape=[_sds((L, width), BF16), _sds((8, width), F32)],
        in_specs=[_prev_spec(h, col0), _tile_spec(col0), _next_spec(h, col0, nt), _tile_spec(0), _next_spec(h, 0, nt),
                  _w_spec(k_taps, 0), pl.BlockSpec((1, CB), lambda j, i: (0, j))],
        out_specs=[_tile_spec(0), _w_spec(8, 0)],
        scratch_shapes=[pltpu.VMEM((h + TC + h, CB), F32), pltpu.VMEM((TC + h, CB), F32)],
        compiler_params=_cparams(("parallel", "arbitrary")))(x, x, x, dpost, dpost, w, b)


def conf_conv_fwd(proj, col_a, col_g, w, b, name):
    L = proj.shape[0]
    k_taps = w.shape[0]
    h = _halo(k_taps)

    def body(ap_ref, a_ref, gp_ref, g_ref, w_ref, b_ref, o_ref, ext_ref):
        i = pl.program_id(1)
        ext_ref[0:h, :] = jnp.where(i > 0, ap_ref[...] * _sigmoid(gp_ref[...]), 0.0)
        ext_ref[h:h + TC, :] = a_ref[...] * _sigmoid(g_ref[...])
        o_ref[...] = _causal_taps(ext_ref, w_ref, k_taps, h, TC) + b_ref[...]

    return pl.pallas_call(
        body, name=name, grid=(D_CONF // CB, L // TC), out_shape=_sds((L, D_CONF), F32),
        in_specs=[_prev_spec(h, col_a), _tile_spec(col_a), _prev_spec(h, col_g), _tile_spec(col_g), _w_spec(k_taps, 0),
                  pl.BlockSpec((1, CB), lambda j, i: (0, j))],
        out_specs=_tile_spec(0), scratch_shapes=[pltpu.VMEM((h + TC, CB), F32)],
        compiler_params=_cparams(("parallel", "parallel")))(proj, proj, proj, proj, w, b)


def conf_conv_bwd(proj, col_a, col_g, w, duc, name):
    L = proj.shape[0]
    k_taps = w.shape[0]
    h = _halo(k_taps)
    nt = L // TC

    def body(ap_ref, a_ref, gp_ref, g_ref, d_ref, dn_ref, w_ref, da_ref, dg_ref, dw_ref, ext_ref, dext_ref):
        i = pl.program_id(1)

        @pl.when(i == 0)
        def _():
            dw_ref[...] = jnp.zeros((32, CB), F32)

        a = a_ref[...]
        s = _sigmoid(g_ref[...])
        ext_ref[0:h, :] = jnp.where(i > 0, ap_ref[...] * _sigmoid(gp_ref[...]), 0.0)
        ext_ref[h:h + TC, :] = a * s
        dext_ref[0:TC, :] = d_ref[...]
        dext_ref[TC:TC + h, :] = jnp.where(i < nt - 1, dn_ref[...], 0.0)
        du0 = _anticausal_taps(dext_ref, w_ref, k_taps, TC)
        da_ref[...] = (du0 * s).astype(BF16)
        dg_ref[...] = (du0 * a * s * (1.0 - s)).astype(BF16)
        _acc_conv_wgrad(dw_ref, d_ref[...], ext_ref, k_taps, h)

    return pl.pallas_call(
        body, name=name, grid=(D_CONF // CB, nt),
        out_shape=[_sds((L, D_CONF), BF16), _sds((L, D_CONF), BF16), _sds((32, D_CONF), F32)],
        in_specs=[_prev_spec(h, col_a), _tile_spec(col_a), _prev_spec(h, col_g), _tile_spec(col_g), _tile_spec(0),
                  _next_spec(h, 0, nt), _w_spec(k_taps, 0)],
        out_specs=[_tile_spec(0), _tile_spec(0), _w_spec(32, 0)],
        scratch_shapes=[pltpu.VMEM((h + TC, CB), F32), pltpu.VMEM((TC + h, CB), F32)],
        compiler_params=_cparams(("parallel", "arbitrary")))(proj, proj, proj, proj, duc, duc, w)


def ffn_conv_fwd(up, w, b, name):
    L = up.shape[0]
    k_taps = w.shape[0]
    h = _halo(k_taps)
    cv = D_FF // CB

    def body(gp_ref, g_ref, vp_ref, v_ref, wg_ref, wv_ref, bg_ref, bv_ref, o_ref, eg_ref, ev_ref):
        i = pl.program_id(1)
        eg_ref[0:h, :] = jnp.where(i > 0, gp_ref[...], 0.0)
        eg_ref[h:h + TC, :] = g_ref[...]
        ev_ref[0:h, :] = jnp.where(i > 0, vp_ref[...], 0.0)
        ev_ref[h:h + TC, :] = v_ref[...]
        pg = _causal_taps(eg_ref, wg_ref, k_taps, h, TC) + bg_ref[...]
        pv = _causal_taps(ev_ref, wv_ref, k_taps, h, TC) + bv_ref[...]
        o_ref[...] = (_silu(pg) * pv).astype(BF16)

    bspec = lambda c0: pl.BlockSpec((1, CB), lambda j, i: (0, j + c0))
    return pl.pallas_call(
        body, name=name, grid=(cv, L // TC), out_shape=_sds((L, D_FF), BF16),
        in_specs=[_prev_spec(h, 0), _tile_spec(0), _prev_spec(h, cv), _tile_spec(cv), _w_spec(k_taps, 0), _w_spec(k_taps, cv),
                  bspec(0), bspec(cv)],
        out_specs=_tile_spec(0), scratch_shapes=[pltpu.VMEM((h + TC, CB), F32), pltpu.VMEM((h + TC, CB), F32)],
        compiler_params=_cparams(("parallel", "parallel")))(up, up, up, up, w, w, b, b)


def ffn_conv_bwd(up, w, b, dact, name):
    L = up.shape[0]
    k_taps = w.shape[0]
    h = _halo(k_taps)
    nt = L // TC
    cv = D_FF // CB

    def body(gp_ref, g_ref, gn_ref, vp_ref, v_ref, vn_ref, d_ref, dn_ref, wg_ref, wv_ref, bg_ref, bv_ref,
             dg_ref, dv_ref, dwg_ref, dwv_ref, eg_ref, ev_ref, pg_ref, pv_ref):
        i = pl.program_id(1)

        @pl.when(i == 0)
        def _():
            dwg_ref[...] = jnp.zeros((8, CB), F32)
            dwv_ref[...] = jnp.zeros((8, CB), F32)

        for e_ref, p_ref, c_ref, n_ref in ((eg_ref, gp_ref, g_ref, gn_ref), (ev_ref, vp_ref, v_ref, vn_ref)):
            e_ref[0:h, :] = jnp.where(i > 0, p_ref[...], 0.0)
            e_ref[h:h + TC, :] = c_ref[...]
            e_ref[h + TC:h + TC + h, :] = n_ref[...]
        pg = _causal_taps(eg_ref, wg_ref, k_taps, h, TC + h) + bg_ref[...]
        pv = _causal_taps(ev_ref, wv_ref, k_taps, h, TC + h) + bv_ref[...]
        dact_t = d_ref[...]
        dact_n = jnp.where(i < nt - 1, dn_ref[...], 0.0)
        pg_ref[0:TC, :] = dact_t * pv[0:TC, :] * _dsilu(pg[0:TC, :])
        pg_ref[TC:TC + h, :] = dact_n * pv[TC:TC + h, :] * _dsilu(pg[TC:TC + h, :])
        pv_ref[0:TC, :] = dact_t * _silu(pg[0:TC, :])
        pv_ref[TC:TC + h, :] = dact_n * _silu(pg[TC:TC + h, :])
        dg_ref[...] = _anticausal_taps(pg_ref, wg_ref, k_taps, TC).astype(BF16)
        dv_ref[...] = _anticausal_taps(pv_ref, wv_ref, k_taps, TC).astype(BF16)
        _acc_conv_wgrad(dwg_ref, pg_ref[0:TC, :], eg_ref, k_taps, h)
        _acc_conv_wgrad(dwv_ref, pv_ref[0:TC, :], ev_ref, k_taps, h)

    bspec = lambda c0: pl.BlockSpec((1, CB), lambda j, i: (0, j + c0))
    ext = pltpu.VMEM((h + TC + h, CB), F32)
    dpre = pltpu.VMEM((TC + h, CB), F32)
    return pl.pallas_call(
        body, name=name, grid=(cv, nt),
        out_shape=[_sds((L, D_FF), BF16), _sds((L, D_FF), BF16), _sds((8, D_FF), F32), _sds((8, D_FF), F32)],
        in_specs=[_prev_spec(h, 0), _tile_spec(0), _next_spec(h, 0, nt), _prev_spec(h, cv), _tile_spec(cv), _next_spec(h, cv, nt),
                  _tile_spec(0), _next_spec(h, 0, nt), _w_spec(k_taps, 0), _w_spec(k_taps, cv), bspec(0), bspec(cv)],
        out_specs=[_tile_spec(0), _tile_spec(0), _w_spec(8, 0), _w_spec(8, 0)],
        scratch_shapes=[ext, ext, dpre, dpre],
        compiler_params=_cparams(("parallel", "arbitrary")))(up, up, up, up, up, up, dact, dact, w, w, b, b)


def _ssd_common(xbc_ref, dt_ref, dtb_ref, alog_ref, cs_ref):
    xs = xbc_ref[:, 0:D_SSD]
    sp_in = dt_ref[...] + dtb_ref[...]
    dtf = _softplus(sp_in)
    a_f = -jnp.exp(alog_ref[...])
    a_dt = dtf * a_f
    row = lax.broadcasted_iota(jnp.int32, (Q, Q), 0)
    col = lax.broadcasted_iota(jnp.int32, (Q, Q), 1)
    causal = row >= col
    cs = _dot_exact(causal.astype(F32), a_dt, 3, "a")
    cs_ref[...] = cs
    cs_last = cs_ref[Q - 1:Q, :]
    return xs, sp_in, dtf, a_f, cs, cs_last, causal


def _head_decay(cs_j, cst_ref, e, causal):
    lane = lax.broadcasted_iota(jnp.int32, (Q, LANE), 1)
    rolled = pltpu.roll(cs_j, HEAD, 1)
    own = (lane < HEAD) if e == 0 else (lane >= HEAD)
    col_b = jnp.where(own, cs_j, rolled)
    col_b = jnp.concatenate([col_b] * (Q // LANE), axis=1)
    row_b = cst_ref[e * HEAD:e * HEAD + 1, :]
    return jnp.where(causal, jnp.exp(jnp.minimum(col_b - row_b, 0.0)), 0.0)


def ssd_fwd(xbc, z_src, dt_src, dtb_f, alog_f, dsk_f, snw, name):
    L = xbc.shape[0]
    nc = L // Q

    def body(xbc_ref, z_ref, dt_ref, dtb_ref, alog_ref, dsk_ref, snw_ref, y_ref, yn_ref, sp_ref, s_ref, cs_ref, cst_ref, yd_ref):
        @pl.when(pl.program_id(0) == 0)
        def _():
            s_ref[...] = jnp.zeros((N_STATE, D_SSD), F32)

        xs, _, dtf, a_f, cs, cs_last, causal = _ssd_common(xbc_ref, dt_ref, dtb_ref, alog_ref, cs_ref)
        e_cs = jnp.exp(cs)
        xdt = xs * dtf
        zst = jnp.exp(cs_last - cs) * xdt
        sp_ref[0] = s_ref[...]
        lane = lax.broadcasted_iota(jnp.int32, (Q, LANE), 1)
        for g in range(2):
            gl = slice(g * 512, g * 512 + 512)
            b_g = xbc_ref[:, D_SSD + g * N_STATE:D_SSD + (g + 1) * N_STATE]
            c_g = xbc_ref[:, D_SSD + 2 * N_STATE + g * N_STATE:D_SSD + 2 * N_STATE + (g + 1) * N_STATE]
            s_prev = s_ref[:, gl]
            cb = _dot_nt(c_g, b_g)
            yd_ref[:, gl] = e_cs[:, gl] * _dot(c_g, s_prev)
            for j in range(4):
                tl = slice(g * 512 + j * LANE, g * 512 + (j + 1) * LANE)
                cs_j = cs[:, tl]
                cst_ref[...] = cs_j.T
                x_j = xdt[:, tl]
                o0 = _dot(cb * _head_decay(cs_j, cst_ref, 0, causal), x_j)
                o1 = _dot(cb * _head_decay(cs_j, cst_ref, 1, causal), x_j)
                yd_ref[:, tl] += jnp.where(lane < HEAD, o0, o1)
            s_ref[:, gl] = jnp.exp(cs_last[:, gl]) * s_prev + _dot_tn(b_g, zst[:, gl])
        y = yd_ref[...] + xs * dsk_ref[...]
        y_ref[...] = y
        yz = y * _silu(z_ref[...])
        r = lax.rsqrt(jnp.mean(yz * yz, axis=-1, keepdims=True) + 1e-6)
        yn_ref[...] = (yz * r * snw_ref[...]).astype(BF16)

    chunk = lambda w, c: pl.BlockSpec((Q, w), lambda i: (i, c))
    return pl.pallas_call(
        body, name=name, grid=(nc,),
        out_shape=[_sds((L, D_SSD), F32), _sds((L, D_SSD), BF16), _sds((nc, N_STATE, D_SSD), F32)],
        in_specs=[chunk(D_XBC, 0), chunk(D, 0), chunk(D, 0)] + [_const_spec((1, D))] * 4,
        out_specs=[chunk(D, 0), chunk(D, 0), pl.BlockSpec((1, N_STATE, D_SSD), lambda i: (i, 0, 0))],
        scratch_shapes=[pltpu.VMEM((N_STATE, D_SSD), F32), pltpu.VMEM((Q, D_SSD), F32), pltpu.VMEM((LANE, Q), F32),
                        pltpu.VMEM((Q, D_SSD), F32)],
        compiler_params=_cparams(("arbitrary",)))(xbc, z_src, dt_src, dtb_f, alog_f, dsk_f, snw)


def ssd_bwd(dysn, y, xbc, z_src, dt_src, s_prev_all, dtb_f, alog_f, dsk_f, snw, name):
    L = xbc.shape[0]
    nc = L // Q

    def body(dyn_ref, y_ref, xbc_ref, z_ref, dt_ref, sp_ref, dtb_ref, alog_ref, dsk_ref, snw_ref,
             dz_ref, ddt_ref, dxbc_ref, acc_ref, acc16_ref, ds_ref, cs_ref, cst_ref, dcs_ref, dx_ref):
        step = pl.program_id(0)

        @pl.when(step == 0)
        def _():
            ds_ref[...] = jnp.zeros((N_STATE, D_SSD), F32)
            acc_ref[...] = jnp.zeros((8, D), F32)

        z = z_ref[...]
        y = y_ref[...]
        sz = _sigmoid(z)
        siluz = z * sz
        yz = y * siluz
        r = lax.rsqrt(jnp.mean(yz * yz, axis=-1, keepdims=True) + 1e-6)
        n = yz * r
        dyn = dyn_ref[...]
        acc_ref[0:1, :] += _colsum(dyn * n)
        dn = dyn * snw_ref[...]
        dyz = r * (dn - n * jnp.mean(dn * n, axis=-1, keepdims=True))
        dy = dyz * siluz
        dz_ref[...] = (dyz * y * (sz * (1.0 + z * (1.0 - sz)))).astype(BF16)

        xs, sp_in, dtf, a_f, cs, cs_last, causal = _ssd_common(xbc_ref, dt_ref, dtb_ref, alog_ref, cs_ref)
        acc_ref[3:4, :] += _colsum(dy * xs)
        e_cs = jnp.exp(cs)
        xdt = xs * dtf
        dst = jnp.exp(cs_last - cs)
        zst = dst * xdt
        e_last = jnp.exp(cs_last)
        lane = lax.broadcasted_iota(jnp.int32, (Q, LANE), 1)
        ones = jnp.ones((Q, LANE), F32)
        dcs_last_parts = []
        for g in range(2):
            gl = slice(g * 512, g * 512 + 512)
            b_g = xbc_ref[:, D_SSD + g * N_STATE:D_SSD + (g + 1) * N_STATE]
            c_g = xbc_ref[:, D_SSD + 2 * N_STATE + g * N_STATE:D_SSD + 2 * N_STATE + (g + 1) * N_STATE]
            s_prev = sp_ref[0, :, gl]
            ds_g = ds_ref[:, gl]
            dy_g = dy[:, gl]
            cb = _dot_nt(c_g, b_g)
            y_off = e_cs[:, gl] * _dot(c_g, s_prev)
            edy = e_cs[:, gl] * dy_g
            d_c = _dot_nt(edy, s_prev)
            d_z = _dot(b_g, ds_g)
            d_b = _dot_nt(zst[:, gl], ds_g)
            t_g = d_z * zst[:, gl]
            dcs_ref[:, gl] = dy_g * y_off - t_g
            dx_ref[:, gl] = d_z * dst[:, gl]
            dcs_last_parts.append(_colsum(t_g) + _colsum(ds_g * s_prev) * e_last[:, gl])
            ds_ref[:, gl] = e_last[:, gl] * ds_g + _dot_tn(c_g, edy)
            dcb = jnp.zeros((Q, Q), F32)
            for j in range(4):
                tl = slice(g * 512 + j * LANE, g * 512 + (j + 1) * LANE)
                cs_j = cs[:, tl]
                cst_ref[...] = cs_j.T
                x_j = xdt[:, tl]
                dy_j = dy[:, tl]
                dx_j = jnp.zeros((Q, LANE), F32)
                dcs_j = jnp.zeros((Q, LANE), F32)
                for e in range(2):
                    own = (lane < HEAD) if e == 0 else (lane >= HEAD)
                    w_h = _head_decay(cs_j, cst_ref, e, causal)
                    g_h = cb * w_h
                    dy_m = jnp.where(own, dy_j, 0.0)
                    d_g = _dot_nt(dy_m, x_j)
                    dx_j = dx_j + _dot_tn(g_h, dy_m)
                    dcb = dcb + d_g * w_h
                    p_h = d_g * g_h
                    row_sums = _dot_exact(p_h, ones, 2, "b")
                    col_sums = _dot_exact(p_h, ones, 2, "b", (((0,), (0,)), ((), ())))
                    dcs_j = dcs_j + jnp.where(own, row_sums - col_sums, 0.0)
                dcs_ref[:, tl] += dcs_j * (1.0 / HEAD)
                dx_ref[:, tl] += dx_j
            d_c = d_c + _dot(dcb, b_g)
            d_b = d_b + _dot_tn(dcb, c_g)
            dxbc_ref[:, D_SSD + g * N_STATE:D_SSD + (g + 1) * N_STATE] = d_b
            dxbc_ref[:, D_SSD + 2 * N_STATE + g * N_STATE:D_SSD + 2 * N_STATE + (g + 1) * N_STATE] = d_c
        dcs_last = jnp.concatenate(dcs_last_parts, axis=1)
        anticausal = lax.broadcasted_iota(jnp.int32, (Q, Q), 0) <= lax.broadcasted_iota(jnp.int32, (Q, Q), 1)
        d_adt = _dot_exact(anticausal.astype(F32), dcs_ref[...], 3, "a") + dcs_last
        dx = dx_ref[...]
        acc_ref[2:3, :] += _colsum(d_adt * dtf) * a_f
        d_dtf = d_adt * a_f + dx * xs
        dxbc_ref[:, 0:D_SSD] = dx * dtf + dy * dsk_ref[...]
        d_raw = d_dtf * _sigmoid(sp_in)
        acc_ref[1:2, :] += _colsum(d_raw)
        head_of_lane = lax.broadcasted_iota(jnp.int32, (D_SSD, LANE), 0) // HEAD
        fold = (head_of_lane == lax.broadcasted_iota(jnp.int32, (D_SSD, LANE), 1)).astype(F32)
        ddt_ref[...] = _dot_exact(d_raw, fold, 2, "b").astype(BF16)

        @pl.when(step == nc - 1)
        def _():
            acc16_ref[...] = _dot_exact(acc_ref[...], fold, 3, "b")

    rchunk = lambda w, c: pl.BlockSpec((Q, w), lambda i: (nc - 1 - i, c))
    return pl.pallas_call(
        body, name=name, grid=(nc,),
        out_shape=[_sds((L, D_SSD), BF16), _sds((L, LANE), BF16), _sds((L, D_XBC), F32), _sds((8, D), F32), _sds((8, LANE), F32)],
        in_specs=[rchunk(D, 0), rchunk(D, 0), rchunk(D_XBC, 0), rchunk(D, 0), rchunk(D, 0),
                  pl.BlockSpec((1, N_STATE, D_SSD), lambda i: (nc - 1 - i, 0, 0))] + [_const_spec((1, D))] * 4,
        out_specs=[rchunk(D, 0), rchunk(LANE, 0), rchunk(D_XBC, 0), _const_spec((8, D)), _const_spec((8, LANE))],
        scratch_shapes=[pltpu.VMEM((N_STATE, D_SSD), F32), pltpu.VMEM((Q, D_SSD), F32), pltpu.VMEM((LANE, Q), F32),
                        pltpu.VMEM((Q, D_SSD), F32), pltpu.VMEM((Q, D_SSD), F32)],
        compiler_params=_cparams(("arbitrary",)))(dysn, y, xbc, z_src, dt_src, s_prev_all, dtb_f, alog_f, dsk_f, snw)


def _adamw_math(w, g, m, v):
    m_n = ADAM_B1 * m + (1.0 - ADAM_B1) * g
    v_n = ADAM_B2 * v + (1.0 - ADAM_B2) * jnp.square(g)
    c1 = 1.0 - ADAM_B1 ** ADAM_STEP
    c2 = 1.0 - ADAM_B2 ** ADAM_STEP
    return -ADAM_LR * ((m_n / c1) / (jnp.sqrt(v_n / c2) + ADAM_EPS) + ADAM_WD * w), m_n, v_n


def _sum_slots(p_ref):
    acc = p_ref[0].astype(F32)
    for s in range(1, p_ref.shape[0]):
        acc = acc + p_ref[s].astype(F32)
    return acc


def adamw_slots(w, slots, m, v, name):
    rows, cols = w.shape
    tc = 256

    def body(w_ref, s_ref, m_ref, v_ref, g_ref, d_ref, mo_ref, vo_ref):
        g_v = _sum_slots(s_ref)
        g_ref[...] = g_v
        d_ref[...], mo_ref[...], vo_ref[...] = _adamw_math(w_ref[...], g_v, m_ref[...], v_ref[...])

    spec = pl.BlockSpec((rows, tc), lambda i: (0, i))
    return pl.pallas_call(body, name=name, grid=(cols // tc,), out_shape=[_sds((rows, cols), F32)] * 4,
                          in_specs=[spec, pl.BlockSpec((slots.shape[0], rows, tc), lambda i: (0, 0, i)), spec, spec], out_specs=[spec] * 4,
                          compiler_params=_cparams(("parallel",)))(w, slots, m, v)


def adamw_many(ws, gs, ms, vs, name):
    n = len(ws)

    def body(*refs):
        for p in range(n):
            d_v, m_v, v_v = _adamw_math(refs[p][...], refs[n + p][...], refs[2 * n + p][...], refs[3 * n + p][...])
            refs[4 * n + p][...] = d_v
            refs[5 * n + p][...] = m_v
            refs[6 * n + p][...] = v_v

    vm = pl.BlockSpec(memory_space=pltpu.VMEM)
    out = pl.pallas_call(body, name=name, out_shape=[_sds(w.shape, F32) for w in ws] * 3, in_specs=[vm] * (4 * n),
                         out_specs=[vm] * (3 * n), compiler_params=_cparams())(*ws, *gs, *ms, *vs)
    return out[:n], out[n:2 * n], out[2 * n:]


def _pack_layout(shapes):
    row, layout = 0, []
    for rows, cols in shapes:
        chunks = []
        for c0 in range(0, cols, D):
            chunks.append((row, c0, min(D, cols - c0)))
            row += rows
        layout.append(chunks)
    return row, layout


def pack_rows(entries, name):
    arrays = [e[0] for e in entries]
    used, layout = _pack_layout([(e[2], e[0].shape[1]) for e in entries])
    total = -(-used // SUBLANES) * SUBLANES
    n = len(arrays)

    def body(*refs):
        o_ref = refs[n]
        o_ref[...] = jnp.zeros((total, D), F32)
        for p in range(n):
            _, first, rows = entries[p]
            for r0, c0, w in layout[p]:
                o_ref[r0:r0 + rows, 0:w] = refs[p][first:first + rows, c0:c0 + w]

    vm = pl.BlockSpec(memory_space=pltpu.VMEM)
    return pl.pallas_call(body, name=name, out_shape=_sds((total, D), F32), in_specs=[vm] * n, out_specs=vm,
                          compiler_params=_cparams())(*arrays)


def unpack_rows(packed, shapes):
    _, layout = _pack_layout(shapes)
    out = []
    for (rows, _), chunks in zip(shapes, layout):
        parts = [packed[..., r0:r0 + rows, 0:w] for r0, _, w in chunks]
        out.append(parts[0] if len(parts) == 1 else jnp.concatenate(parts, axis=-1))
    return out


def sum_slots_many(parts, name):
    n = len(parts)

    def body(*refs):
        for p in range(n):
            refs[n + p][...] = _sum_slots(refs[p])

    vm = pl.BlockSpec(memory_space=pltpu.VMEM)
    return pl.pallas_call(body, name=name, out_shape=[_sds(p.shape[1:], F32) for p in parts], in_specs=[vm] * n,
                          out_specs=[vm] * n, compiler_params=_cparams())(*parts)


def ada_mod(c_all, ada_w_shard, ada_b_cols, name):
    def body(c_ref, w_ref, b_ref, o_ref, ca_ref):
        ca = _silu(c_ref[...])
        ca_ref[...] = ca
        o_ref[...] = _dot(ca, w_ref[...]) + b_ref[...]

    vm = pl.BlockSpec(memory_space=pltpu.VMEM)
    return pl.pallas_call(body, name=name, out_shape=[_sds((N_DEV, ada_w_shard.shape[1]), F32), _sds((N_DEV, D), F32)],
                          in_specs=[vm, vm, vm], out_specs=[vm, vm], compiler_params=_cparams())(c_all, ada_w_shard, ada_b_cols)


def ada_wgrad(c_act_all, dmod_cols, name):
    def body(c_ref, d_ref, o_ref):
        o_ref[...] = _dot_tn_hi(c_ref[...], d_ref[...])

    vm = pl.BlockSpec(memory_space=pltpu.VMEM)
    return pl.pallas_call(body, name=name, out_shape=_sds((D, dmod_cols.shape[1]), F32), in_specs=[vm, vm], out_specs=vm,
                          compiler_params=_cparams())(c_act_all, dmod_cols)


def exchange(srcs, name, gather):
    n = len(srcs)
    gathers = [gather] * n if isinstance(gather, bool) else list(gather)
    shapes = [tuple(s.shape) if g else tuple(s.shape[1:]) for s, g in zip(srcs, gathers)]

    def body(*refs):
        src_refs, out_refs = refs[:n], refs[n:2 * n]
        send_sems, recv_sems, local_sems = refs[2 * n:]
        x, y, c = lax.axis_index("x"), lax.axis_index("y"), lax.axis_index("c")
        me = 4 * x + 2 * y + c

        def peer(k):
            bx, by, bc = (k >> 2) & 1, (k >> 1) & 1, k & 1
            px, py, pc = (x + bx) % 2, (y + by) % 2, (c + bc) % 2
            return (px, py, pc), 4 * px + 2 * py + pc

        def copy(a, k, landing):
            dev, idx = peer(k)
            return pltpu.make_async_remote_copy(
                src_ref=src_refs[a] if gathers[a] else src_refs[a].at[idx], dst_ref=out_refs[a].at[idx if landing else me],
                send_sem=send_sems.at[a, k - 1], recv_sem=recv_sems.at[a, k - 1],
                device_id=dev, device_id_type=pl.DeviceIdType.MESH)

        mine = [pltpu.make_async_copy(src_refs[a] if gathers[a] else src_refs[a].at[me], out_refs[a].at[me], local_sems.at[a])
                for a in range(n)]
        for cp in mine:
            cp.start()
        sends = [copy(a, k, False) for a in range(n) for k in range(1, N_DEV)]
        for cp in sends:
            cp.start()
        for a in range(n):
            for k in range(1, N_DEV):
                copy(a, k, True).wait_recv()
        for cp in sends:
            cp.wait_send()
        for cp in mine:
            cp.wait()

    hbm = pl.BlockSpec(memory_space=pl.ANY)
    return pl.pallas_call(
        body, name=name, out_shape=[_sds((N_DEV,) + shp, s.dtype) for shp, s in zip(shapes, srcs)], in_specs=[hbm] * n,
        out_specs=[hbm] * n,
        scratch_shapes=[pltpu.SemaphoreType.DMA((n, N_DEV - 1)), pltpu.SemaphoreType.DMA((n, N_DEV - 1)),
                        pltpu.SemaphoreType.DMA((n,))],
        compiler_params=pltpu.CompilerParams(has_side_effects=True))(*srcs)


def gather_two_level(srcs, name):
    n = len(srcs)

    def body(*refs):
        src_refs, out_refs = refs[:n], refs[n:2 * n]
        send_sems, recv_sems, local_sems = refs[2 * n:]
        x, y, c = lax.axis_index("x"), lax.axis_index("y"), lax.axis_index("c")
        me, sibling = (x, y, c), (x, y, 1 - c)
        chips = [(1 - x, y), (x, 1 - y), (1 - x, 1 - y)]

        def slot(a, px, py, pc):
            return out_refs[a].at[4 * px + 2 * py + pc]

        def copy(a, k, block, to, src=None):
            return pltpu.make_async_remote_copy(
                src_ref=slot(a, *block) if src is None else src, dst_ref=slot(a, *block), send_sem=send_sems.at[a, k],
                recv_sem=recv_sems.at[a, k], device_id=to, device_id_type=pl.DeviceIdType.MESH)

        mine = [pltpu.make_async_copy(src_refs[a], slot(a, *me), local_sems.at[a]) for a in range(n)]
        for cp in mine:
            cp.start()
        first = []
        for a in range(n):
            first += [copy(a, 0, me, sibling, src=src_refs[a])]
            first += [copy(a, 1 + j, me, (*chip, c), src=src_refs[a]) for j, chip in enumerate(chips)]
        for cp in first:
            cp.start()
        passed = []
        for a in range(n):
            for j, chip in enumerate(chips):
                copy(a, 1 + j, (*chip, c), me).wait_recv()
                passed.append(copy(a, 4 + j, (*chip, c), sibling))
                passed[-1].start()
        for a in range(n):
            copy(a, 0, sibling, me).wait_recv()
            for j, chip in enumerate(chips):
                copy(a, 4 + j, (*chip, 1 - c), me).wait_recv()
        for cp in first + passed:
            cp.wait_send()
        for cp in mine:
            cp.wait()

    hbm = pl.BlockSpec(memory_space=pl.ANY)
    return pl.pallas_call(
        body, name=name, out_shape=[_sds((N_DEV,) + tuple(s.shape), s.dtype) for s in srcs], in_specs=[hbm] * n,
        out_specs=[hbm] * n,
        scratch_shapes=[pltpu.SemaphoreType.DMA((n, N_DEV - 1)), pltpu.SemaphoreType.DMA((n, N_DEV - 1)),
                        pltpu.SemaphoreType.DMA((n,))],
        compiler_params=pltpu.CompilerParams(has_side_effects=True))(*srcs)


def _peer(k):
    x, y, c = lax.axis_index("x"), lax.axis_index("y"), lax.axis_index("c")
    px, py, pc = (x + ((k >> 2) & 1)) % 2, (y + ((k >> 1) & 1)) % 2, (c + (k & 1)) % 2
    return (px, py, pc), 4 * px + 2 * py + pc


def _my_slot():
    return 4 * lax.axis_index("x") + 2 * lax.axis_index("y") + lax.axis_index("c")


_HBM = pl.BlockSpec(memory_space=pltpu.HBM)
_SEM = pl.BlockSpec(memory_space=pltpu.SEMAPHORE)
_EFFECT = pltpu.SideEffectType.DATAFLOW_SIDE_EFFECTING


def exchange_start(srcs, name, gather):
    n = len(srcs)
    shapes = [tuple(s.shape) if gather else tuple(s.shape[1:]) for s in srcs]
    lands = [lax.empty((N_DEV,) + shp, s.dtype) for shp, s in zip(shapes, srcs)]

    def body(*refs):
        src_refs, land_refs = refs[:n], refs[n:2 * n]
        sems = refs[2 * n:4 * n]
        token = refs[-1]
        me = _my_slot()
        for a in range(n):
            for k in range(1, N_DEV):
                dev, idx = _peer(k)
                pltpu.make_async_remote_copy(
                    src_ref=src_refs[a] if gather else src_refs[a].at[idx], dst_ref=land_refs[a].at[me],
                    send_sem=sems[2 * a].at[k - 1], recv_sem=sems[2 * a + 1].at[k - 1],
                    device_id=dev, device_id_type=pl.DeviceIdType.MESH).start()
        token[...] = jnp.zeros_like(token)

    out_shape = ([pltpu.SemaphoreType.DMA((N_DEV - 1,))] * (2 * n) + [pltpu.HBM(s.shape, s.dtype) for s in srcs]
                 + [pltpu.HBM(l.shape, l.dtype) for l in lands] + [_sds((8, LANE), F32)])
    out = pl.pallas_call(
        body, name=name, out_shape=out_shape, in_specs=[_HBM] * (2 * n),
        out_specs=[_SEM] * (2 * n) + [_HBM] * (2 * n) + [pl.BlockSpec(memory_space=pltpu.VMEM)],
        input_output_aliases={i: 2 * n + i for i in range(2 * n)},
        compiler_params=pltpu.CompilerParams(has_side_effects=_EFFECT))(
            *[pltpu.with_memory_space_constraint(s, pltpu.HBM) for s in srcs],
            *[pltpu.with_memory_space_constraint(l, pltpu.HBM) for l in lands])
    parts = [(out[2 * a], out[2 * a + 1], out[2 * n + a], out[3 * n + a]) for a in range(n)]
    return parts, out[-1]


def exchange_wait(parts, after, name, gather):
    n = len(parts)

    def body(*refs):
        src_refs, land_refs = refs[:n], refs[n:2 * n]
        sems = refs[2 * n:4 * n]
        for a in range(n):
            for k in range(1, N_DEV):
                dev, idx = _peer(k)
                copy = pltpu.make_async_remote_copy(
                    src_ref=src_refs[a] if gather else src_refs[a].at[idx], dst_ref=land_refs[a].at[idx],
                    send_sem=sems[2 * a].at[k - 1], recv_sem=sems[2 * a + 1].at[k - 1],
                    device_id=dev, device_id_type=pl.DeviceIdType.MESH)
                copy.wait_send()
                copy.wait_recv()

    srcs = [p[2] for p in parts]
    lands = [p[3] for p in parts]
    sems = [s for p in parts for s in p[:2]]
    out = pl.pallas_call(
        body, name=name, out_shape=[pltpu.HBM(a.shape, a.dtype) for a in srcs + lands],
        in_specs=[_HBM] * (2 * n) + [_SEM] * (2 * n) + [pl.BlockSpec(memory_space=pl.ANY)], out_specs=[_HBM] * (2 * n),
        input_output_aliases={i: i for i in range(2 * n)},
        compiler_params=pltpu.CompilerParams(has_side_effects=_EFFECT))(*srcs, *lands, *sems, after)
    return list(zip(out[:n], out[n:]))


def _cols_to_slabs(g):
    r, c = g.shape
    return g.reshape(r, N_DEV, c // N_DEV).transpose(1, 0, 2)


def _slabs_to_cols(s):
    _, r, cs = s.shape
    return s.transpose(1, 0, 2).reshape(r, N_DEV * cs)


def _rep_heads(v):
    return jnp.repeat(v.reshape(N_HEADS), HEAD).reshape(1, D_SSD)


def local_fwd_bwd(x, target, mod, get_w, put_grad, small):
    n1w, n2w, fnw = small["norm1_w"], small["norm2_w"], small["final_norm_w"]
    dtb_f, alog_f, dsk_f = _rep_heads(small["dt_bias"]), _rep_heads(small["a_log"]), _rep_heads(small["d_skip"])
    snw = small["ssd_norm_w"]

    def after(v, token):
        return v + token[0:1, 0:1]

    w_in = get_w("w_in", mod)
    h1, proj_zx, proj_dt, proj_cf = in_proj(x, mod, n1w, w_in["w_full"], w_in["w_dt_rep"], w_in["w_cf"], "norm1_in_proj")
    xbc = conv_silu_fwd(proj_zx, D_SSD // CB, D_XBC, small["ssd_conv_w"], small["ssd_conv_b"], "ssd_conv")
    y, ysn, s_prev = ssd_fwd(xbc, proj_zx, proj_dt, dtb_f, alog_f, dsk_f, snw, "ssd_scan")
    uc = conf_conv_fwd(proj_cf, 0, D_CONF // CB, small["conf_conv_w"], small["conf_conv_b"], "conf_conv")
    w_out = get_w("w_out", uc)
    mix, u, x1, h2 = mixer_out(ysn, uc, small["conf_ln_w"], small["conf_ln_b"], w_out, x, mod, n2w, "out_proj_norm2")
    w_up_t = get_w("w_up", h2)
    up = mm_nt([(h2, w_up_t, 0)], "up_proj")
    act = ffn_conv_fwd(up, small["ffn_conv_w"], small["ffn_conv_b"], "ffn_conv")
    w_down = get_w("w_down", act)
    dx2, dff, dact, acc_f = final_loss(act, w_down, x1, mod, fnw, target, "down_proj_loss")

    token = put_grad("w_down", mm_tn(act, dff, "wgrad_down"))
    dupg, dupv, dwg, dwv = ffn_conv_bwd(up, small["ffn_conv_w"], after(small["ffn_conv_b"], token), dact, "ffn_conv_bwd")
    token = put_grad("w_up", mm_tn_stack([dupg, dupv], h2, "wgrad_up"))
    dx1, dmix, acc_2 = norm_mod_bwd([(dupg, w_up_t, 0), (dupv, w_up_t, 1)], x1, dx2, mod, after(n2w, token), 3, "norm2_bwd",
                                    mix=mix, gate_row=2)

    token = put_grad("w_out", mm_tn_stack([ysn, u], dmix, "wgrad_out"))
    dysn, duc, acc_ln = mixer_out_bwd(dmix, w_out, uc, after(small["conf_ln_w"], token), small["conf_ln_b"], "out_proj_bwd")
    dcfa, dcfg, dw_cc = conf_conv_bwd(proj_cf, 0, D_CONF // CB, small["conf_conv_w"], duc, "conf_conv_bwd")
    dz, ddt, dxbc_post, acc_s, acc_s16 = ssd_bwd(dysn, y, xbc, proj_zx, proj_dt, s_prev, dtb_f, alog_f, dsk_f, snw,
                                                 "ssd_scan_bwd")
    dxbc, dw_sc = conv_silu_bwd(proj_zx, D_SSD // CB, D_XBC, small["ssd_conv_w"], small["ssd_conv_b"], dxbc_post, "ssd_conv_bwd")
    token = put_grad("w_in", mm_tn_concat(
        [(dz, D_SSD), (dxbc, D_XBC), (ddt, N_HEADS), (dcfa, D_CONF), (dcfg, D_CONF)], h1, "wgrad_in"))
    dh1_pairs = [(dz, w_in["w_full"], 0), (ddt, w_in["w_dt16"], 0), (dcfa, w_in["w_cf"], 0), (dcfg, w_in["w_cf"], 1),
                 (dxbc, w_in["w_xbc"], 0)]
    grad_x, acc_1 = norm_mod_bwd(dh1_pairs, x, dx1, mod, after(n1w, token), 0, "norm1_bwd")

    small_accs = dict(acc_1=acc_1, acc_2=acc_2, acc_f=acc_f, acc_ln=acc_ln, acc_s=acc_s, acc_s16=acc_s16, dw_sc=dw_sc,
                      dw_cc=dw_cc, dwg=dwg, dwv=dwv)
    return grad_x, small_accs


def kernel(x, c, ada_w, ada_b, norm1_w, w_in, ssd_conv_w, ssd_conv_b, dt_bias, a_log, d_skip, ssd_norm_w, conf_conv_w, conf_conv_b, conf_ln_w, conf_ln_b, w_out, norm2_w, w_up, ffn_conv_w, ffn_conv_b, w_down, final_norm_w, loss_target, m_ada_w, m_ada_b, m_norm1_w, m_w_in, m_ssd_conv_w, m_ssd_conv_b, m_dt_bias, m_a_log, m_d_skip, m_ssd_norm_w, m_conf_conv_w, m_conf_conv_b, m_conf_ln_w, m_conf_ln_b, m_w_out, m_norm2_w, m_w_up, m_ffn_conv_w, m_ffn_conv_b, m_w_down, m_final_norm_w, v_ada_w, v_ada_b, v_norm1_w, v_w_in, v_ssd_conv_w, v_ssd_conv_b, v_dt_bias, v_a_log, v_d_skip, v_ssd_norm_w, v_conf_conv_w, v_conf_conv_b, v_conf_ln_w, v_conf_ln_b, v_w_out, v_norm2_w, v_w_up, v_ffn_conv_w, v_ffn_conv_b, v_w_down, v_final_norm_w):
    me = 4 * lax.axis_index("x") + 2 * lax.axis_index("y") + lax.axis_index("c")
    weights = dict(ada_w=ada_w, ada_b=ada_b, norm1_w=norm1_w, w_in=w_in, ssd_conv_w=ssd_conv_w, ssd_conv_b=ssd_conv_b,
                   dt_bias=dt_bias, a_log=a_log, d_skip=d_skip, ssd_norm_w=ssd_norm_w, conf_conv_w=conf_conv_w,
                   conf_conv_b=conf_conv_b, conf_ln_w=conf_ln_w, conf_ln_b=conf_ln_b, w_out=w_out, norm2_w=norm2_w, w_up=w_up,
                   ffn_conv_w=ffn_conv_w, ffn_conv_b=ffn_conv_b, w_down=w_down, final_norm_w=final_norm_w)
    moms_m = dict(ada_w=m_ada_w, ada_b=m_ada_b, norm1_w=m_norm1_w, w_in=m_w_in, ssd_conv_w=m_ssd_conv_w, ssd_conv_b=m_ssd_conv_b,
                  dt_bias=m_dt_bias, a_log=m_a_log, d_skip=m_d_skip, ssd_norm_w=m_ssd_norm_w, conf_conv_w=m_conf_conv_w,
                  conf_conv_b=m_conf_conv_b, conf_ln_w=m_conf_ln_w, conf_ln_b=m_conf_ln_b, w_out=m_w_out, norm2_w=m_norm2_w,
                  w_up=m_w_up, ffn_conv_w=m_ffn_conv_w, ffn_conv_b=m_ffn_conv_b, w_down=m_w_down, final_norm_w=m_final_norm_w)
    moms_v = dict(ada_w=v_ada_w, ada_b=v_ada_b, norm1_w=v_norm1_w, w_in=v_w_in, ssd_conv_w=v_ssd_conv_w, ssd_conv_b=v_ssd_conv_b,
                  dt_bias=v_dt_bias, a_log=v_a_log, d_skip=v_d_skip, ssd_norm_w=v_ssd_norm_w, conf_conv_w=v_conf_conv_w,
                  conf_conv_b=v_conf_conv_b, conf_ln_w=v_conf_ln_w, conf_ln_b=v_conf_ln_b, w_out=v_w_out, norm2_w=v_norm2_w,
                  w_up=v_w_up, ffn_conv_w=v_ffn_conv_w, ffn_conv_b=v_ffn_conv_b, w_down=v_w_down, final_norm_w=v_final_norm_w)
    names = list(weights)

    def to2d(a):
        return a[0] if a.ndim == 3 else a.reshape(1, -1)

    big = ("w_in", "w_out", "w_up", "w_down")

    def rows_of(a):
        return jnp.swapaxes(a, 1, 2)[0] if a.shape[2] != D else a[0]

    shards = [rows_of(weights[n]).astype(BF16) for n in big]

    c_all, scw_all, ccw_all, fcw_all, w_in_slabs = gather_two_level(
        [c.reshape(8, LANE), ssd_conv_w[0], conf_conv_w[0], ffn_conv_w[0], shards[0]], "gather_first")
    c_all = c_all.reshape(N_DEV, D)

    ada_cols = ada_w.shape[2]
    ada_b_cols = lax.dynamic_slice(ada_b, (0, me * ada_cols), (1, ada_cols))
    mod_cols, c_act_all = ada_mod(c_all, ada_w[0], ada_b_cols, "ada_mod")
    mod_parts, = exchange([jnp.pad(mod_cols, ((0, 0), (0, D - ada_cols))).reshape(N_DEV, 8, LANE)], "scatter_mod", gather=False)
    mod = mod_parts.reshape(N_DEV, D)[:, :ada_cols].reshape(6, D)
    mod = jnp.pad(mod, ((0, 2), (0, 0)))

    later, mod = lax.optimization_barrier((shards[1:], mod))
    gather_parts, token = exchange_start(later, "gather_weights_start", gather=True)
    mod = mod + token[0:1, 0:1]

    small = {n: to2d(weights[n]) for n in names if n not in ("ada_w",) + big}
    small["ssd_conv_w"] = _slabs_to_cols(scw_all)
    small["conf_conv_w"] = _slabs_to_cols(ccw_all)
    small["ffn_conv_w"] = _slabs_to_cols(fcw_all)

    def with_own(landed, own):
        return lax.dynamic_update_slice(landed, own[None], (me,) + (0,) * own.ndim)

    def get_w(n, after):
        if n == "w_in":
            slabs = w_in_slabs
        else:
            a = big.index(n)
            (own, landed), = exchange_wait([gather_parts[a - 1]], after, "gather_" + n + "_wait", gather=True)
            slabs = with_own(landed, own)
        full = slabs.reshape(N_DEV * slabs.shape[1], D)
        if n != "w_in":
            return full
        w_dt = full[D_SSD + D_XBC:D_SSD + D_XBC + N_HEADS]
        return dict(w_full=full, w_xbc=full[D_SSD:D_SSD + D_XBC], w_cf=full[D_SSD + D_XBC + N_HEADS:],
                    w_dt_rep=jnp.repeat(w_dt, HEAD, axis=0), w_dt16=jnp.pad(w_dt, ((0, LANE - N_HEADS), (0, 0))))

    scatter_parts = {}

    def put_grad(n, g):
        slabs = g if g.ndim == 3 else g.reshape(N_DEV, g.shape[0] // N_DEV, g.shape[1])
        (scatter_parts[n],), token = exchange_start([slabs.astype(BF16)], "scatter_" + n + "_start", gather=False)
        return token

    grad_x, accs = local_fwd_bwd(x[0], loss_target[0], mod, get_w, put_grad, small)

    grads, delta, new_m, new_v = {}, {}, {}, {}

    def finish(ns, after, name):
        landed = exchange_wait([scatter_parts[n] for n in ns], after, name, gather=False)
        for n, (sent, slots) in zip(ns, landed):
            slots = with_own(slots, lax.dynamic_index_in_dim(sent, me, 0, keepdims=False))
            out = adamw_slots(rows_of(weights[n]), slots, rows_of(moms_m[n]), rows_of(moms_v[n]), "adamw_" + n)
            if weights[n].shape[2] != D:
                out = [jnp.swapaxes(o, 0, 1) for o in out]
            grads[n], delta[n], new_m[n], new_v[n] = out

    finish(big[1:], grad_x, "scatter_grads_wait")

    accs = dict(zip(accs, lax.optimization_barrier((list(accs.values()), [new_v[n] for n in big[1:]]))[0]))
    rep = (("acc_1", 0, 3), ("acc_2", 0, 4), ("acc_f", 0, 3), ("acc_ln", 0, 2), ("acc_s", 0, 1), ("acc_s16", 1, 3),
           ("dw_sc", K_SSD, 1), ("dw_cc", K_CONF, 1), ("dwg", K_FFN, 1), ("dwv", K_FFN, 1))
    shapes = [(rows, accs[k].shape[1]) for k, _, rows in rep]
    conv_slabs = [_cols_to_slabs(accs["dw_sc"][:K_SSD]), _cols_to_slabs(accs["dw_cc"][:K_CONF]),
                  _cols_to_slabs(jnp.concatenate([accs["dwg"][:K_FFN], accs["dwv"][:K_FFN]], axis=1))]
    packed = pack_rows([(accs[k], first, rows) for k, first, rows in rep], "pack_small_grads")
    landed = exchange([packed] + conv_slabs, "exchange_small_grads", gather=[True, False, False, False])
    packed_red, g_scw, g_ccw, g_fcw = sum_slots_many(landed, "sum_small_grads")
    a1_all, a2_all, af_all = unpack_rows(landed[0], shapes)[:3]
    r1, r2, rf, rln, rs, r16, rscb, rccb, rfbg, rfbv = unpack_rows(packed_red, shapes)
    loss = 0.5 / D * jnp.sum(rf[2:3])

    def mod_rows(a1, a2, af):
        return jnp.concatenate([a1[..., 0:2, :], a2[..., 3:4, :], a2[..., 0:2, :], af[..., 1:2, :]], axis=-2)

    dmod_all = mod_rows(a1_all, a2_all, af_all).reshape(N_DEV, 6 * D)
    grads["ada_w"] = ada_wgrad(c_act_all, lax.dynamic_slice(dmod_all, (0, me * ada_cols), (N_DEV, ada_cols)), "ada_wgrad")
    grads.update(
        ada_b=mod_rows(r1, r2, rf).reshape(1, 6 * D), norm1_w=r1[2:3], ssd_conv_w=g_scw, ssd_conv_b=rscb,
        dt_bias=r16[0:1, :N_HEADS], a_log=r16[1:2, :N_HEADS], d_skip=r16[2:3, :N_HEADS], ssd_norm_w=rs,
        conf_conv_w=g_ccw, conf_conv_b=rccb, conf_ln_w=rln[0:1], conf_ln_b=rln[1:2], norm2_w=r2[2:3],
        ffn_conv_w=g_fcw, ffn_conv_b=jnp.concatenate([rfbg, rfbv], axis=1), final_norm_w=rf[0:1])

    rest = [n for n in names if n not in big]
    d_l, m_l, v_l = adamw_many([to2d(weights[n]) for n in rest], [grads[n] for n in rest], [to2d(moms_m[n]) for n in rest],
                               [to2d(moms_v[n]) for n in rest], "adamw_small")
    for n, dd, mm, vv in zip(rest, d_l, m_l, v_l):
        delta[n], new_m[n], new_v[n] = dd, mm, vv
    finish(big[:1], d_l[0], "scatter_w_in_wait")
    shape_of = lambda d_: {n: d_[n].reshape(weights[n].shape) for n in names}
    grads, delta, new_m, new_v = shape_of(grads), shape_of(delta), shape_of(new_m), shape_of(new_v)
    return (loss, grad_x[None], *[grads[n] for n in names], *[delta[n] for n in names], *[new_m[n] for n in names],
            *[new_v[n] for n in names])
```

```python
import functools

import jax
import jax.numpy as jnp
from jax import lax
from jax.experimental import pallas as pl
from jax.experimental.pallas import tpu as pltpu

F32 = jnp.float32
BF16 = jnp.bfloat16
HI = lax.Precision.HIGHEST

N_DEV = 8
D = 1024
D_SSD = 1024
HEAD = 64
N_HEADS = 16
N_STATE = 128
D_XBC = 1536
D_CONF = 1024
D_FF = 2816
K_SSD, K_CONF, K_FFN = 4, 31, 3
LANE = 128
TR = 256
TM = 512
Q = 256
CB = 256
TC = 1024
TCF = 2048
VMEM_LIMIT = 56 * 1024 * 1024

ADAM_LR, ADAM_B1, ADAM_B2, ADAM_EPS, ADAM_WD, ADAM_STEP = 0.001, 0.9, 0.999, 1e-08, 0.01, 10


def _cparams(sem=None):
    return pltpu.CompilerParams(vmem_limit_bytes=VMEM_LIMIT, dimension_semantics=sem)


def _sds(shape, dtype):
    return jax.ShapeDtypeStruct(shape, dtype)


def _sigmoid(x):
    return 1.0 / (1.0 + jnp.exp(-x))


def _silu(x):
    return x * _sigmoid(x)


def _dsilu(x):
    s = _sigmoid(x)
    return s * (1.0 + x * (1.0 - s))


def _softplus(x):
    return jnp.maximum(x, 0.0) + jnp.log(1.0 + jnp.exp(-jnp.abs(x)))


def _dot(a, b):
    return jnp.dot(a.astype(BF16), b.astype(BF16), preferred_element_type=F32)


def _dot_nt(a, b):
    return lax.dot_general(a.astype(BF16), b.astype(BF16), (((1,), (1,)), ((), ())), preferred_element_type=F32)


def _dot_tn(a, b):
    return lax.dot_general(a.astype(BF16), b.astype(BF16), (((0,), (0,)), ((), ())), preferred_element_type=F32)


def _bf16_terms(a, terms):
    parts, rem = [], a
    for t in range(terms):
        p = rem.astype(BF16)
        parts.append(p)
        if t + 1 < terms:
            rem = rem - p.astype(F32)
    return parts


def _dot_exact(a, b, terms, exact, dims=(((1,), (0,)), ((), ()))):
    if exact == "a":
        a_b = a.astype(BF16)
        outs = [lax.dot_general(a_b, p, dims, preferred_element_type=F32) for p in _bf16_terms(b, terms)]
    else:
        b_b = b.astype(BF16)
        outs = [lax.dot_general(p, b_b, dims, preferred_element_type=F32) for p in _bf16_terms(a, terms)]
    acc = outs[-1]
    for o in reversed(outs[:-1]):
        acc = acc + o
    return acc


def _dot_tn_hi(a, b):
    return lax.dot_general(a, b, (((0,), (0,)), ((), ())), precision=HI, preferred_element_type=F32)


def _colsum(x):
    return jnp.sum(x, axis=0, keepdims=True)


def _const_spec(shape):
    return pl.BlockSpec(shape, lambda *_: (0,) * len(shape))


def _col_tile(n):
    for t in (2816, 1408, 1024, 768, 512, 256, 128):
        if n % t == 0 and t <= n:
            return t
    return n


def mm_nt(pairs, name):
    L = pairs[0][0].shape[0]
    K = pairs[0][1].shape[0]
    tk = _col_tile(K)
    n = len(pairs)

    def body(*refs):
        o_ref = refs[-1]
        acc = None
        for p in range(n):
            t = lax.dot_general(refs[2 * p][...], refs[2 * p + 1][...], (((1,), (1,)), ((), ())),
                                preferred_element_type=F32)
            acc = t if acc is None else acc + t
        o_ref[...] = acc

    in_specs, args = [], []
    for a, w, cb in pairs:
        in_specs += [pl.BlockSpec((TM, a.shape[1]), lambda j, i: (i, 0)),
                     pl.BlockSpec((tk, a.shape[1]), functools.partial(lambda j, i, cb: (j, cb), cb=cb))]
        args += [a, w]
    return pl.pallas_call(
        body, name=name, grid=(K // tk, L // TM), out_shape=_sds((L, K), F32), in_specs=in_specs,
        out_specs=pl.BlockSpec((TM, tk), lambda j, i: (i, j)),
        compiler_params=_cparams(("parallel", "parallel")))(*args)


def mm_tn(a, g, name):
    L, M = a.shape
    N = g.shape[1]
    tn = _col_tile(N) if N > 1024 else N
    if M * tn * 4 > 12 * 1024 * 1024:
        tn = 512
    tl = 512 if L % 512 == 0 else TR
    nl = L // tl

    def body(a_ref, g_ref, o_ref, acc_ref):
        @pl.when(pl.program_id(1) == 0)
        def _():
            acc_ref[...] = jnp.zeros((M, tn), F32)

        acc_ref[...] += lax.dot_general(a_ref[...], g_ref[...], (((0,), (0,)), ((), ())), preferred_element_type=F32)

        @pl.when(pl.program_id(1) == nl - 1)
        def _():
            o_ref[...] = acc_ref[...].astype(BF16)

    return pl.pallas_call(
        body, name=name, grid=(N // tn, nl), out_shape=_sds((M, N), BF16),
        in_specs=[pl.BlockSpec((tl, M), lambda j, l: (l, 0)), pl.BlockSpec((tl, tn), lambda j, l: (l, j))],
        out_specs=pl.BlockSpec((M, tn), lambda j, l: (0, j)), scratch_shapes=[pltpu.VMEM((M, tn), F32)],
        compiler_params=_cparams(("parallel", "arbitrary")))(a, g)


def mm_tn_stack(a_list, g, name):
    L, M = a_list[0].shape
    N = g.shape[1]
    n = len(a_list)
    tl = 512 if L % 512 == 0 else TR
    nl = L // tl

    def body(*refs):
        a_refs, g_ref, o_ref, acc_ref = refs[:n], refs[n], refs[n + 1], refs[n + 2]
        j, l = pl.program_id(0), pl.program_id(1)

        @pl.when(l == 0)
        def _():
            acc_ref[...] = jnp.zeros((M, N), F32)

        for p in range(n):
            @pl.when(j == p)
            def _(p=p):
                acc_ref[...] += lax.dot_general(a_refs[p][...], g_ref[...], (((0,), (0,)), ((), ())), preferred_element_type=F32)

        @pl.when(l == nl - 1)
        def _():
            o_ref[...] = acc_ref[...].astype(BF16)

    a_specs = [pl.BlockSpec((tl, M), functools.partial(lambda j, l, p: (jnp.where(j == p, l, 0), 0), p=p)) for p in range(n)]
    return pl.pallas_call(
        body, name=name, grid=(n, nl), out_shape=_sds((n * M, N), BF16),
        in_specs=a_specs + [pl.BlockSpec((tl, N), lambda j, l: (l, 0))],
        out_specs=pl.BlockSpec((M, N), lambda j, l: (j, 0)), scratch_shapes=[pltpu.VMEM((M, N), F32)],
        compiler_params=_cparams(("arbitrary", "arbitrary")))(*a_list, g)


def mm_tn_concat(pieces, g, name):
    L = g.shape[0]
    N = g.shape[1]
    n = len(pieces)
    offsets = [sum(r for _, r in pieces[:p]) for p in range(n + 1)]
    slab = offsets[-1] // N_DEV
    tl = 512 if L % 512 == 0 else TR
    nl = L // tl

    def body(*refs):
        a_refs, g_ref, o_ref, acc_ref = refs[:n], refs[n], refs[n + 1], refs[n + 2]
        l = pl.program_id(0)

        @pl.when(l == 0)
        def _():
            acc_ref[...] = jnp.zeros((offsets[-1], N), F32)

        g_v = g_ref[...]
        for p in range(n):
            t = lax.dot_general(a_refs[p][...], g_v, (((0,), (0,)), ((), ())), preferred_element_type=F32)
            acc_ref[offsets[p]:offsets[p + 1], :] += t[:pieces[p][1], :]

        @pl.when(l == nl - 1)
        def _():
            for s in range(N_DEV):
                o_ref[s] = acc_ref[s * slab:(s + 1) * slab, :].astype(BF16)

    return pl.pallas_call(
        body, name=name, grid=(nl,), out_shape=_sds((N_DEV, slab, N), BF16),
        in_specs=[pl.BlockSpec((tl, a.shape[1]), lambda l: (l, 0)) for a, _ in pieces] + [pl.BlockSpec((tl, N), lambda l: (l, 0))],
        out_specs=_const_spec((N_DEV, slab, N)), scratch_shapes=[pltpu.VMEM((offsets[-1], N), F32)],
        compiler_params=_cparams(("arbitrary",)))(*[a for a, _ in pieces], g)


def _row_spec(width=D):
    return pl.BlockSpec((TR, width), lambda i: (i, 0))


def in_proj(x, mod, n1w, w_full, w_dt_rep, w_cf, name):
    L = x.shape[0]
    n_zx = D_SSD + D_XBC

    def body(x_ref, mod_ref, w_ref, wzx_ref, wdt_ref, wcf_ref, h_ref, zx_ref, dt_ref, cf_ref):
        xin = x_ref[...]
        r = lax.rsqrt(jnp.mean(xin * xin, axis=-1, keepdims=True) + 1e-6)
        h = ((xin * r * w_ref[...]) * (1.0 + mod_ref[1:2, :]) + mod_ref[0:1, :]).astype(BF16)
        h_ref[...] = h
        nt = (((1,), (1,)), ((), ()))
        zx_ref[...] = lax.dot_general(h, wzx_ref[...], nt, preferred_element_type=F32)
        dt_ref[...] = lax.dot_general(h, wdt_ref[...], nt, preferred_element_type=F32)
        cf_ref[...] = lax.dot_general(h, wcf_ref[...], nt, preferred_element_type=F32)

    row = lambda w: pl.BlockSpec((TM, w), lambda i: (i, 0))
    return pl.pallas_call(
        body, name=name, grid=(L // TM,),
        out_shape=[_sds((L, D), BF16), _sds((L, n_zx), F32), _sds((L, D_SSD), F32), _sds((L, 2 * D_CONF), F32)],
        in_specs=[row(D), _const_spec((8, D)), _const_spec((1, D)), _const_spec((n_zx, D)), _const_spec((D_SSD, D)),
                  _const_spec((2 * D_CONF, D))],
        out_specs=[row(D), row(n_zx), row(D_SSD), row(2 * D_CONF)],
        compiler_params=_cparams(("parallel",)))(x, mod, n1w, w_full, w_dt_rep, w_cf)


def mixer_out(ysn, uc, lnw, lnb, w_out, x, mod, n2w, name):
    L = x.shape[0]

    def body(ysn_ref, uc_ref, lnw_ref, lnb_ref, wo_ref, x_ref, mod_ref, n2w_ref, mix_ref, u_ref, x1_ref, h2_ref):
        uc_v = uc_ref[...]
        mu = jnp.mean(uc_v, axis=-1, keepdims=True)
        var = jnp.mean(jnp.square(uc_v - mu), axis=-1, keepdims=True)
        u = _silu((uc_v - mu) * lax.rsqrt(var + 1e-5) * lnw_ref[...] + lnb_ref[...]).astype(BF16)
        u_ref[...] = u
        mix = (jnp.dot(ysn_ref[...], wo_ref[0:D_SSD, :], preferred_element_type=F32)
               + jnp.dot(u, wo_ref[D_SSD:D_SSD + D_CONF, :], preferred_element_type=F32))
        mix_ref[...] = mix
        x1 = x_ref[...] + mod_ref[2:3, :] * mix
        x1_ref[...] = x1
        r = lax.rsqrt(jnp.mean(x1 * x1, axis=-1, keepdims=True) + 1e-6)
        h2_ref[...] = ((x1 * r * n2w_ref[...]) * (1.0 + mod_ref[4:5, :]) + mod_ref[3:4, :]).astype(BF16)

    row = pl.BlockSpec((TM, D), lambda i: (i, 0))
    return pl.pallas_call(
        body, name=name, grid=(L // TM,),
        out_shape=[_sds((L, D), F32), _sds((L, D_CONF), BF16), _sds((L, D), F32), _sds((L, D), BF16)],
        in_specs=[row, row, _const_spec((1, D)), _const_spec((1, D)), _const_spec((D_SSD + D_CONF, D)), row,
                  _const_spec((8, D)), _const_spec((1, D))],
        out_specs=[row] * 4, compiler_params=_cparams(("parallel",)))(ysn, uc, lnw, lnb, w_out, x, mod, n2w)


def mixer_out_bwd(dmix, w_out, uc, lnw, lnb, name):
    L = uc.shape[0]

    def body(dm_ref, wo_ref, u_ref, w_ref, b_ref, dy_ref, o_ref, acc_ref):
        @pl.when(pl.program_id(0) == 0)
        def _():
            acc_ref[...] = jnp.zeros((8, D), F32)

        nt = (((1,), (1,)), ((), ()))
        dm = dm_ref[...]
        dy_ref[...] = lax.dot_general(dm, wo_ref[0:D_SSD, :], nt, preferred_element_type=F32)
        du = lax.dot_general(dm, wo_ref[D_SSD:D_SSD + D_CONF, :], nt, preferred_element_type=F32)
        u = u_ref[...]
        mu = jnp.mean(u, axis=-1, keepdims=True)
        rl = lax.rsqrt(jnp.mean(jnp.square(u - mu), axis=-1, keepdims=True) + 1e-5)
        n = (u - mu) * rl
        v = n * w_ref[...] + b_ref[...]
        dv = du * _dsilu(v)
        acc_ref[0:1, :] += _colsum(dv * n)
        acc_ref[1:2, :] += _colsum(dv)
        dn = dv * w_ref[...]
        o_ref[...] = rl * (dn - jnp.mean(dn, axis=-1, keepdims=True) - n * jnp.mean(dn * n, axis=-1, keepdims=True))

    row = pl.BlockSpec((TM, D), lambda i: (i, 0))
    return pl.pallas_call(body, name=name, grid=(L // TM,), out_shape=[_sds((L, D), F32), _sds((L, D), F32), _sds((8, D), F32)],
                          in_specs=[row, _const_spec((D_SSD + D_CONF, D)), row, _const_spec((1, D)), _const_spec((1, D))],
                          out_specs=[row, row, _const_spec((8, D))],
                          compiler_params=_cparams(("arbitrary",)))(dmix, w_out, uc, lnw, lnb)


def final_loss(act, w_down, x1, mod, fw, target, name):
    L = act.shape[0]

    def body(act_ref, wd_ref, x1_ref, mod_ref, fw_ref, t_ref, dx_ref, dff_ref, dact_ref, acc_ref):
        @pl.when(pl.program_id(0) == 0)
        def _():
            acc_ref[...] = jnp.zeros((8, D), F32)

        ff_v = jnp.dot(act_ref[...], wd_ref[...], preferred_element_type=F32)
        g2 = mod_ref[5:6, :]
        x2 = x1_ref[...] + g2 * ff_v
        r = lax.rsqrt(jnp.mean(x2 * x2, axis=-1, keepdims=True) + 1e-6)
        n = x2 * r
        err = n * fw_ref[...] - t_ref[...]
        dy = err * (1.0 / D)
        dn = dy * fw_ref[...]
        dx2 = r * (dn - n * jnp.mean(dn * n, axis=-1, keepdims=True))
        acc_ref[0:1, :] += _colsum(dy * n)
        acc_ref[1:2, :] += _colsum(dx2 * ff_v)
        acc_ref[2:3, :] += _colsum(err * err)
        dx_ref[...] = dx2
        dff = (dx2 * g2).astype(BF16)
        dff_ref[...] = dff
        dact_ref[...] = lax.dot_general(dff, wd_ref[...], (((1,), (1,)), ((), ())), preferred_element_type=F32)

    return pl.pallas_call(
        body, name=name, grid=(L // TR,),
        out_shape=[_sds((L, D), F32), _sds((L, D), BF16), _sds((L, D_FF), F32), _sds((8, D), F32)],
        in_specs=[_row_spec(D_FF), _const_spec((D_FF, D)), _row_spec(), _const_spec((8, D)), _const_spec((1, D)), _row_spec()],
        out_specs=[_row_spec(), _row_spec(), _row_spec(D_FF), _const_spec((8, D))],
        compiler_params=_cparams(("arbitrary",)))(act, w_down, x1, mod, fw, target)


def norm_mod_bwd(dh_pairs, xin, dres, mod, w, shift_row, name, mix=None, gate_row=None):
    L = xin.shape[0]
    has_mix = mix is not None
    n_pairs = len(dh_pairs)

    def body(*refs):
        pair_refs, refs = refs[:2 * n_pairs], refs[2 * n_pairs:]
        if has_mix:
            x_ref, dres_ref, mod_ref, w_ref, mix_ref, dx_ref, dmix_ref, acc_ref = refs
        else:
            x_ref, dres_ref, mod_ref, w_ref, dx_ref, acc_ref = refs

        @pl.when(pl.program_id(0) == 0)
        def _():
            acc_ref[...] = jnp.zeros((8, D), F32)

        dh_v = None
        for p in range(n_pairs):
            t = jnp.dot(pair_refs[2 * p][...], pair_refs[2 * p + 1][...], preferred_element_type=F32)
            dh_v = t if dh_v is None else dh_v + t
        x = x_ref[...]
        r = lax.rsqrt(jnp.mean(x * x, axis=-1, keepdims=True) + 1e-6)
        n = x * r
        nw = n * w_ref[...]
        sc1 = 1.0 + mod_ref[shift_row + 1:shift_row + 2, :]
        acc_ref[0:1, :] += _colsum(dh_v)
        acc_ref[1:2, :] += _colsum(dh_v * nw)
        dnw = dh_v * sc1
        acc_ref[2:3, :] += _colsum(dnw * n)
        dn = dnw * w_ref[...]
        dx = r * (dn - n * jnp.mean(dn * n, axis=-1, keepdims=True)) + dres_ref[...]
        dx_ref[...] = dx
        if has_mix:
            acc_ref[3:4, :] += _colsum(dx * mix_ref[...])
            dmix_ref[...] = (dx * mod_ref[gate_row:gate_row + 1, :]).astype(BF16)

    ins, in_specs = [], []
    for a, wt, rb in dh_pairs:
        ins += [a, wt]
        in_specs += [_row_spec(a.shape[1]), pl.BlockSpec((a.shape[1], D), functools.partial(lambda i, rb: (rb, 0), rb=rb))]
    ins += [xin, dres, mod, w] + ([mix] if has_mix else [])
    in_specs += [_row_spec(), _row_spec(), _const_spec((8, D)), _const_spec((1, D))] + ([_row_spec()] if has_mix else [])
    out_shape = [_sds((L, D), F32)] + ([_sds((L, D), BF16)] if has_mix else []) + [_sds((8, D), F32)]
    out_specs = [_row_spec()] + ([_row_spec()] if has_mix else []) + [_const_spec((8, D))]
    return pl.pallas_call(body, name=name, grid=(L // TR,), out_shape=out_shape, in_specs=in_specs,
                          out_specs=out_specs, compiler_params=_cparams(("arbitrary",)))(*ins)


def _halo(k):
    return 8 if k <= 9 else 32


def _prev_spec(h, col0, tc=TC):
    return pl.BlockSpec((h, CB), lambda j, i: (jnp.maximum(i * (tc // h) - 1, 0), j + col0))


def _next_spec(h, col0, n_tiles):
    return pl.BlockSpec((h, CB), lambda j, i: (jnp.minimum(i + 1, n_tiles - 1) * (TC // h), j + col0))


def _tile_spec(col0, tc=TC):
    return pl.BlockSpec((tc, CB), lambda j, i: (i, j + col0))


def _w_spec(kp, col0):
    return pl.BlockSpec((kp, CB), lambda j, i: (0, j + col0))


SUBLANES = 8


def _shifted_windows(v, taps, rows):
    for r in range(SUBLANES):
        group = [(o, k) for o, k in taps if o % SUBLANES == r]
        if not group:
            continue
        s = v if r == 0 else pltpu.roll(v, v.shape[0] - r, 0)
        for o, k in group:
            yield k, s[o - r:o - r + rows, :]


def _causal_taps(ext_ref, w_ref, k_taps, first, rows):
    acc = None
    for k, win in _shifted_windows(ext_ref[...], [(first - (k_taps - 1) + k, k) for k in range(k_taps)], rows):
        t = w_ref[k:k + 1, :] * win
        acc = t if acc is None else acc + t
    return acc


def _anticausal_taps(d_ref, w_ref, k_taps, rows):
    acc = None
    for k, win in _shifted_windows(d_ref[...], [(k_taps - 1 - k, k) for k in range(k_taps)], rows):
        t = w_ref[k:k + 1, :] * win
        acc = t if acc is None else acc + t
    return acc


def _acc_conv_wgrad(dw_ref, d_tile, ext_ref, k_taps, first):
    for k, win in _shifted_windows(ext_ref[...], [(first - (k_taps - 1) + k, k) for k in range(k_taps)], TC):
        dw_ref[k:k + 1, :] += _colsum(d_tile * win)
    dw_ref[k_taps:k_taps + 1, :] += _colsum(d_tile)


def conv_silu_fwd(x, col0, width, w, b, name):
    L = x.shape[0]
    k_taps = w.shape[0]
    h = _halo(k_taps)

    def body(xp_ref, x_ref, w_ref, b_ref, o_ref, ext_ref):
        i = pl.program_id(1)
        ext_ref[0:h, :] = jnp.where(i > 0, xp_ref[...], 0.0)
        ext_ref[h:h + TCF, :] = x_ref[...]
        o_ref[...] = _silu(_causal_taps(ext_ref, w_ref, k_taps, h, TCF) + b_ref[...])

    return pl.pallas_call(
        body, name=name, grid=(width // CB, L // TCF), out_shape=_sds((L, width), F32),
        in_specs=[_prev_spec(h, col0, TCF), _tile_spec(col0, TCF), _w_spec(k_taps, 0), pl.BlockSpec((1, CB), lambda j, i: (0, j))],
        out_specs=_tile_spec(0, TCF), scratch_shapes=[pltpu.VMEM((h + TCF, CB), F32)],
        compiler_params=_cparams(("parallel", "parallel")))(x, x, w, b)


def conv_silu_bwd(x, col0, width, w, b, dpost, name):
    L = x.shape[0]
    k_taps = w.shape[0]
    h = _halo(k_taps)
    nt = L // TC

    def body(xp_ref, x_ref, xn_ref, d_ref, dn_ref, w_ref, b_ref, dx_ref, dw_ref, ext_ref, dpre_ref):
        i = pl.program_id(1)

        @pl.when(i == 0)
        def _():
            dw_ref[...] = jnp.zeros((8, CB), F32)

        ext_ref[0:h, :] = jnp.where(i > 0, xp_ref[...], 0.0)
        ext_ref[h:h + TC, :] = x_ref[...]
        ext_ref[h + TC:h + TC + h, :] = xn_ref[...]
        pre = _causal_taps(ext_ref, w_ref, k_taps, h, TC + h) + b_ref[...]
        dpre_ref[0:TC, :] = d_ref[...] * _dsilu(pre[0:TC, :])
        dpre_ref[TC:TC + h, :] = jnp.where(i < nt - 1, dn_ref[...], 0.0) * _dsilu(pre[TC:TC + h, :])
        dx_ref[...] = _anticausal_taps(dpre_ref, w_ref, k_taps, TC).astype(BF16)
        _acc_conv_wgrad(dw_ref, dpre_ref[0:TC, :], ext_ref, k_taps, h)

    return pl.pallas_call(
        body, name=name, grid=(width // CB, nt),
        out_shape=[_sds((L, width), BF16), _sds((8, width), F32)],
        in_specs=[_prev_spec(h, col0), _tile_spec(col0), _next_spec(h, col0, nt), _tile_spec(0), _next_spec(h, 0, nt),
                  _w_spec(k_taps, 0), pl.BlockSpec((1, CB), lambda j, i: (0, j))],
        out_specs=[_tile_spec(0), _w_spec(8, 0)],
        scratch_shapes=[pltpu.VMEM((h + TC + h, CB), F32), pltpu.VMEM((TC + h, CB), F32)],
        compiler_params=_cparams(("parallel", "arbitrary")))(x, x, x, dpost, dpost, w, b)


def conf_conv_fwd(proj, col_a, col_g, w, b, name):
    L = proj.shape[0]
    k_taps = w.shape[0]
    h = _halo(k_taps)

    def body(ap_ref, a_ref, gp_ref, g_ref, w_ref, b_ref, o_ref, ext_ref):
        i = pl.program_id(1)
        ext_ref[0:h, :] = jnp.where(i > 0, ap_ref[...] * _sigmoid(gp_ref[...]), 0.0)
        ext_ref[h:h + TC, :] = a_ref[...] * _sigmoid(g_ref[...])
        o_ref[...] = _causal_taps(ext_ref, w_ref, k_taps, h, TC) + b_ref[...]

    return pl.pallas_call(
        body, name=name, grid=(D_CONF // CB, L // TC), out_shape=_sds((L, D_CONF), F32),
        in_specs=[_prev_spec(h, col_a), _tile_spec(col_a), _prev_spec(h, col_g), _tile_spec(col_g), _w_spec(k_taps, 0),
                  pl.BlockSpec((1, CB), lambda j, i: (0, j))],
        out_specs=_tile_spec(0), scratch_shapes=[pltpu.VMEM((h + TC, CB), F32)],
        compiler_params=_cparams(("parallel", "parallel")))(proj, proj, proj, proj, w, b)


def conf_conv_bwd(proj, col_a, col_g, w, duc, name):
    L = proj.shape[0]
    k_taps = w.shape[0]
    h = _halo(k_taps)
    nt = L // TC

    def body(ap_ref, a_ref, gp_ref, g_ref, d_ref, dn_ref, w_ref, da_ref, dg_ref, dw_ref, ext_ref, dext_ref):
        i = pl.program_id(1)

        @pl.when(i == 0)
        def _():
            dw_ref[...] = jnp.zeros((32, CB), F32)

        a = a_ref[...]
        s = _sigmoid(g_ref[...])
        ext_ref[0:h, :] = jnp.where(i > 0, ap_ref[...] * _sigmoid(gp_ref[...]), 0.0)
        ext_ref[h:h + TC, :] = a * s
        dext_ref[0:TC, :] = d_ref[...]
        dext_ref[TC:TC + h, :] = jnp.where(i < nt - 1, dn_ref[...], 0.0)
        du0 = _anticausal_taps(dext_ref, w_ref, k_taps, TC)
        da_ref[...] = (du0 * s).astype(BF16)
        dg_ref[...] = (du0 * a * s * (1.0 - s)).astype(BF16)
        _acc_conv_wgrad(dw_ref, d_ref[...], ext_ref, k_taps, h)

    return pl.pallas_call(
        body, name=name, grid=(D_CONF // CB, nt),
        out_shape=[_sds((L, D_CONF), BF16), _sds((L, D_CONF), BF16), _sds((32, D_CONF), F32)],
        in_specs=[_prev_spec(h, col_a), _tile_spec(col_a), _prev_spec(h, col_g), _tile_spec(col_g), _tile_spec(0),
                  _next_spec(h, 0, nt), _w_spec(k_taps, 0)],
        out_specs=[_tile_spec(0), _tile_spec(0), _w_spec(32, 0)],
        scratch_shapes=[pltpu.VMEM((h + TC, CB), F32), pltpu.VMEM((TC + h, CB), F32)],
        compiler_params=_cparams(("parallel", "arbitrary")))(proj, proj, proj, proj, duc, duc, w)


def ffn_conv_fwd(up, w, b, name):
    L = up.shape[0]
    k_taps = w.shape[0]
    h = _halo(k_taps)
    cv = D_FF // CB

    def body(gp_ref, g_ref, vp_ref, v_ref, wg_ref, wv_ref, bg_ref, bv_ref, o_ref, eg_ref, ev_ref):
        i = pl.program_id(1)
        eg_ref[0:h, :] = jnp.where(i > 0, gp_ref[...], 0.0)
        eg_ref[h:h + TCF, :] = g_ref[...]
        ev_ref[0:h, :] = jnp.where(i > 0, vp_ref[...], 0.0)
        ev_ref[h:h + TCF, :] = v_ref[...]
        pg = _causal_taps(eg_ref, wg_ref, k_taps, h, TCF) + bg_ref[...]
        pv = _causal_taps(ev_ref, wv_ref, k_taps, h, TCF) + bv_ref[...]
        o_ref[...] = (_silu(pg) * pv).astype(BF16)

    bspec = lambda c0: pl.BlockSpec((1, CB), lambda j, i: (0, j + c0))
    return pl.pallas_call(
        body, name=name, grid=(cv, L // TCF), out_shape=_sds((L, D_FF), BF16),
        in_specs=[_prev_spec(h, 0, TCF), _tile_spec(0, TCF), _prev_spec(h, cv, TCF), _tile_spec(cv, TCF), _w_spec(k_taps, 0),
                  _w_spec(k_taps, cv), bspec(0), bspec(cv)],
        out_specs=_tile_spec(0, TCF), scratch_shapes=[pltpu.VMEM((h + TCF, CB), F32), pltpu.VMEM((h + TCF, CB), F32)],
        compiler_params=_cparams(("parallel", "parallel")))(up, up, up, up, w, w, b, b)


def ffn_conv_bwd(up, w, b, dact, name):
    L = up.shape[0]
    k_taps = w.shape[0]
    h = _halo(k_taps)
    nt = L // TC
    cv = D_FF // CB

    def body(gp_ref, g_ref, gn_ref, vp_ref, v_ref, vn_ref, d_ref, dn_ref, wg_ref, wv_ref, bg_ref, bv_ref,
             dg_ref, dv_ref, dwg_ref, dwv_ref, eg_ref, ev_ref, pg_ref, pv_ref):
        i = pl.program_id(1)

        @pl.when(i == 0)
        def _():
            dwg_ref[...] = jnp.zeros((8, CB), F32)
            dwv_ref[...] = jnp.zeros((8, CB), F32)

        for e_ref, p_ref, c_ref, n_ref in ((eg_ref, gp_ref, g_ref, gn_ref), (ev_ref, vp_ref, v_ref, vn_ref)):
            e_ref[0:h, :] = jnp.where(i > 0, p_ref[...], 0.0)
            e_ref[h:h + TC, :] = c_ref[...]
            e_ref[h + TC:h + TC + h, :] = n_ref[...]
        pg = _causal_taps(eg_ref, wg_ref, k_taps, h, TC + h) + bg_ref[...]
        pv = _causal_taps(ev_ref, wv_ref, k_taps, h, TC + h) + bv_ref[...]
        dact_t = d_ref[...]
        dact_n = jnp.where(i < nt - 1, dn_ref[...], 0.0)
        pg_ref[0:TC, :] = dact_t * pv[0:TC, :] * _dsilu(pg[0:TC, :])
        pg_ref[TC:TC + h, :] = dact_n * pv[TC:TC + h, :] * _dsilu(pg[TC:TC + h, :])
        pv_ref[0:TC, :] = dact_t * _silu(pg[0:TC, :])
        pv_ref[TC:TC + h, :] = dact_n * _silu(pg[TC:TC + h, :])
        dg_ref[...] = _anticausal_taps(pg_ref, wg_ref, k_taps, TC).astype(BF16)
        dv_ref[...] = _anticausal_taps(pv_ref, wv_ref, k_taps, TC).astype(BF16)
        _acc_conv_wgrad(dwg_ref, pg_ref[0:TC, :], eg_ref, k_taps, h)
        _acc_conv_wgrad(dwv_ref, pv_ref[0:TC, :], ev_ref, k_taps, h)

    bspec = lambda c0: pl.BlockSpec((1, CB), lambda j, i: (0, j + c0))
    ext = pltpu.VMEM((h + TC + h, CB), F32)
    dpre = pltpu.VMEM((TC + h, CB), F32)
    return pl.pallas_call(
        body, name=name, grid=(cv, nt),
        out_shape=[_sds((L, D_FF), BF16), _sds((L, D_FF), BF16), _sds((8, D_FF), F32), _sds((8, D_FF), F32)],
        in_specs=[_prev_spec(h, 0), _tile_spec(0), _next_spec(h, 0, nt), _prev_spec(h, cv), _tile_spec(cv), _next_spec(h, cv, nt),
                  _tile_spec(0), _next_spec(h, 0, nt), _w_spec(k_taps, 0), _w_spec(k_taps, cv), bspec(0), bspec(cv)],
        out_specs=[_tile_spec(0), _tile_spec(0), _w_spec(8, 0), _w_spec(8, 0)],
        scratch_shapes=[ext, ext, dpre, dpre],
        compiler_params=_cparams(("parallel", "arbitrary")))(up, up, up, up, up, up, dact, dact, w, w, b, b)


def _ssd_common(xbc_ref, dt_ref, dtb_ref, alog_ref, cs_ref):
    xs = xbc_ref[:, 0:D_SSD]
    sp_in = dt_ref[...] + dtb_ref[...]
    dtf = _softplus(sp_in)
    a_f = -jnp.exp(alog_ref[...])
    a_dt = dtf * a_f
    row = lax.broadcasted_iota(jnp.int32, (Q, Q), 0)
    col = lax.broadcasted_iota(jnp.int32, (Q, Q), 1)
    causal = row >= col
    cs = _dot_exact(causal.astype(F32), a_dt, 3, "a")
    cs_ref[...] = cs
    cs_last = cs_ref[Q - 1:Q, :]
    return xs, sp_in, dtf, a_f, cs, cs_last, causal


def _head_decay(cs_j, cst_ref, e, causal):
    lane = lax.broadcasted_iota(jnp.int32, (Q, LANE), 1)
    rolled = pltpu.roll(cs_j, HEAD, 1)
    own = (lane < HEAD) if e == 0 else (lane >= HEAD)
    col_b = jnp.where(own, cs_j, rolled)
    col_b = jnp.concatenate([col_b] * (Q // LANE), axis=1)
    row_b = cst_ref[e * HEAD:e * HEAD + 1, :]
    return jnp.where(causal, jnp.exp(jnp.minimum(col_b - row_b, 0.0)), 0.0)


def ssd_fwd(xbc, z_src, dt_src, dtb_f, alog_f, dsk_f, snw, name):
    L = xbc.shape[0]
    nc = L // Q

    def body(xbc_ref, z_ref, dt_ref, dtb_ref, alog_ref, dsk_ref, snw_ref, y_ref, yn_ref, sp_ref, s_ref, cs_ref, cst_ref, yd_ref):
        @pl.when(pl.program_id(0) == 0)
        def _():
            s_ref[...] = jnp.zeros((N_STATE, D_SSD), F32)

        xs, _, dtf, a_f, cs, cs_last, causal = _ssd_common(xbc_ref, dt_ref, dtb_ref, alog_ref, cs_ref)
        e_cs = jnp.exp(cs)
        xdt = xs * dtf
        zst = jnp.exp(cs_last - cs) * xdt
        sp_ref[0] = s_ref[...]
        lane = lax.broadcasted_iota(jnp.int32, (Q, LANE), 1)
        for g in range(2):
            gl = slice(g * 512, g * 512 + 512)
            b_g = xbc_ref[:, D_SSD + g * N_STATE:D_SSD + (g + 1) * N_STATE]
            c_g = xbc_ref[:, D_SSD + 2 * N_STATE + g * N_STATE:D_SSD + 2 * N_STATE + (g + 1) * N_STATE]
            s_prev = s_ref[:, gl]
            cb = _dot_nt(c_g, b_g)
            yd_ref[:, gl] = e_cs[:, gl] * _dot(c_g, s_prev)
            for j in range(4):
                tl = slice(g * 512 + j * LANE, g * 512 + (j + 1) * LANE)
                cs_j = cs[:, tl]
                cst_ref[...] = cs_j.T
                x_j = xdt[:, tl]
                o0 = _dot(cb * _head_decay(cs_j, cst_ref, 0, causal), x_j)
                o1 = _dot(cb * _head_decay(cs_j, cst_ref, 1, causal), x_j)
                yd_ref[:, tl] += jnp.where(lane < HEAD, o0, o1)
            s_ref[:, gl] = jnp.exp(cs_last[:, gl]) * s_prev + _dot_tn(b_g, zst[:, gl])
        y = yd_ref[...] + xs * dsk_ref[...]
        y_ref[...] = y
        yz = y * _silu(z_ref[...])
        r = lax.rsqrt(jnp.mean(yz * yz, axis=-1, keepdims=True) + 1e-6)
        yn_ref[...] = (yz * r * snw_ref[...]).astype(BF16)

    chunk = lambda w, c: pl.BlockSpec((Q, w), lambda i: (i, c))
    return pl.pallas_call(
        body, name=name, grid=(nc,),
        out_shape=[_sds((L, D_SSD), F32), _sds((L, D_SSD), BF16), _sds((nc, N_STATE, D_SSD), F32)],
        in_specs=[chunk(D_XBC, 0), chunk(D, 0), chunk(D, 0)] + [_const_spec((1, D))] * 4,
        out_specs=[chunk(D, 0), chunk(D, 0), pl.BlockSpec((1, N_STATE, D_SSD), lambda i: (i, 0, 0))],
        scratch_shapes=[pltpu.VMEM((N_STATE, D_SSD), F32), pltpu.VMEM((Q, D_SSD), F32), pltpu.VMEM((LANE, Q), F32),
                        pltpu.VMEM((Q, D_SSD), F32)],
        compiler_params=_cparams(("arbitrary",)))(xbc, z_src, dt_src, dtb_f, alog_f, dsk_f, snw)


def ssd_bwd(dysn, y, xbc, z_src, dt_src, s_prev_all, dtb_f, alog_f, dsk_f, snw, name):
    L = xbc.shape[0]
    nc = L // Q

    def body(dyn_ref, y_ref, xbc_ref, z_ref, dt_ref, sp_ref, dtb_ref, alog_ref, dsk_ref, snw_ref,
             dz_ref, ddt_ref, dxbc_ref, acc_ref, acc16_ref, ds_ref, cs_ref, cst_ref, dcs_ref, dx_ref):
        step = pl.program_id(0)

        @pl.when(step == 0)
        def _():
            ds_ref[...] = jnp.zeros((N_STATE, D_SSD), F32)
            acc_ref[...] = jnp.zeros((8, D), F32)

        z = z_ref[...]
        y = y_ref[...]
        sz = _sigmoid(z)
        siluz = z * sz
        yz = y * siluz
        r = lax.rsqrt(jnp.mean(yz * yz, axis=-1, keepdims=True) + 1e-6)
        n = yz * r
        dyn = dyn_ref[...]
        acc_ref[0:1, :] += _colsum(dyn * n)
        dn = dyn * snw_ref[...]
        dyz = r * (dn - n * jnp.mean(dn * n, axis=-1, keepdims=True))
        dy = dyz * siluz
        dz_ref[...] = (dyz * y * (sz * (1.0 + z * (1.0 - sz)))).astype(BF16)

        xs, sp_in, dtf, a_f, cs, cs_last, causal = _ssd_common(xbc_ref, dt_ref, dtb_ref, alog_ref, cs_ref)
        acc_ref[3:4, :] += _colsum(dy * xs)
        e_cs = jnp.exp(cs)
        xdt = xs * dtf
        dst = jnp.exp(cs_last - cs)
        zst = dst * xdt
        e_last = jnp.exp(cs_last)
        lane = lax.broadcasted_iota(jnp.int32, (Q, LANE), 1)
        ones = jnp.ones((Q, LANE), F32)
        dcs_last_parts = []
        for g in range(2):
            gl = slice(g * 512, g * 512 + 512)
            b_g = xbc_ref[:, D_SSD + g * N_STATE:D_SSD + (g + 1) * N_STATE]
            c_g = xbc_ref[:, D_SSD + 2 * N_STATE + g * N_STATE:D_SSD + 2 * N_STATE + (g + 1) * N_STATE]
            s_prev = sp_ref[0, :, gl]
            ds_g = ds_ref[:, gl]
            dy_g = dy[:, gl]
            cb = _dot_nt(c_g, b_g)
            y_off = e_cs[:, gl] * _dot(c_g, s_prev)
            edy = e_cs[:, gl] * dy_g
            d_c = _dot_nt(edy, s_prev)
            d_z = _dot(b_g, ds_g)
            d_b = _dot_nt(zst[:, gl], ds_g)
            t_g = d_z * zst[:, gl]
            dcs_ref[:, gl] = dy_g * y_off - t_g
            dx_ref[:, gl] = d_z * dst[:, gl]
            dcs_last_parts.append(_colsum(t_g) + _colsum(ds_g * s_prev) * e_last[:, gl])
            ds_ref[:, gl] = e_last[:, gl] * ds_g + _dot_tn(c_g, edy)
            dcb = jnp.zeros((Q, Q), F32)
            for j in range(4):
                tl = slice(g * 512 + j * LANE, g * 512 + (j + 1) * LANE)
                cs_j = cs[:, tl]
                cst_ref[...] = cs_j.T
                x_j = xdt[:, tl]
                dy_j = dy[:, tl]
                dx_j = jnp.zeros((Q, LANE), F32)
                dcs_j = jnp.zeros((Q, LANE), F32)
                for e in range(2):
                    own = (lane < HEAD) if e == 0 else (lane >= HEAD)
                    w_h = _head_decay(cs_j, cst_ref, e, causal)
                    g_h = cb * w_h
                    dy_m = jnp.where(own, dy_j, 0.0)
                    d_g = _dot_nt(dy_m, x_j)
                    dx_j = dx_j + _dot_tn(g_h, dy_m)
                    dcb = dcb + d_g * w_h
                    p_h = d_g * g_h
                    row_sums = _dot_exact(p_h, ones, 2, "b")
                    col_sums = _dot_exact(p_h, ones, 2, "b", (((0,), (0,)), ((), ())))
                    dcs_j = dcs_j + jnp.where(own, row_sums - col_sums, 0.0)
                dcs_ref[:, tl] += dcs_j * (1.0 / HEAD)
                dx_ref[:, tl] += dx_j
            d_c = d_c + _dot(dcb, b_g)
            d_b = d_b + _dot_tn(dcb, c_g)
            dxbc_ref[:, D_SSD + g * N_STATE:D_SSD + (g + 1) * N_STATE] = d_b
            dxbc_ref[:, D_SSD + 2 * N_STATE + g * N_STATE:D_SSD + 2 * N_STATE + (g + 1) * N_STATE] = d_c
        dcs_last = jnp.concatenate(dcs_last_parts, axis=1)
        anticausal = lax.broadcasted_iota(jnp.int32, (Q, Q), 0) <= lax.broadcasted_iota(jnp.int32, (Q, Q), 1)
        d_adt = _dot_exact(anticausal.astype(F32), dcs_ref[...], 3, "a") + dcs_last
        dx = dx_ref[...]
        acc_ref[2:3, :] += _colsum(d_adt * dtf) * a_f
        d_dtf = d_adt * a_f + dx * xs
        dxbc_ref[:, 0:D_SSD] = dx * dtf + dy * dsk_ref[...]
        d_raw = d_dtf * _sigmoid(sp_in)
        acc_ref[1:2, :] += _colsum(d_raw)
        head_of_lane = lax.broadcasted_iota(jnp.int32, (D_SSD, LANE), 0) // HEAD
        fold = (head_of_lane == lax.broadcasted_iota(jnp.int32, (D_SSD, LANE), 1)).astype(F32)
        ddt_ref[...] = _dot_exact(d_raw, fold, 2, "b").astype(BF16)

        @pl.when(step == nc - 1)
        def _():
            acc16_ref[...] = _dot_exact(acc_ref[...], fold, 3, "b")

    rchunk = lambda w, c: pl.BlockSpec((Q, w), lambda i: (nc - 1 - i, c))
    return pl.pallas_call(
        body, name=name, grid=(nc,),
        out_shape=[_sds((L, D_SSD), BF16), _sds((L, LANE), BF16), _sds((L, D_XBC), F32), _sds((8, D), F32), _sds((8, LANE), F32)],
        in_specs=[rchunk(D, 0), rchunk(D, 0), rchunk(D_XBC, 0), rchunk(D, 0), rchunk(D, 0),
                  pl.BlockSpec((1, N_STATE, D_SSD), lambda i: (nc - 1 - i, 0, 0))] + [_const_spec((1, D))] * 4,
        out_specs=[rchunk(D, 0), rchunk(LANE, 0), rchunk(D_XBC, 0), _const_spec((8, D)), _const_spec((8, LANE))],
        scratch_shapes=[pltpu.VMEM((N_STATE, D_SSD), F32), pltpu.VMEM((Q, D_SSD), F32), pltpu.VMEM((LANE, Q), F32),
                        pltpu.VMEM((Q, D_SSD), F32), pltpu.VMEM((Q, D_SSD), F32)],
        compiler_params=_cparams(("arbitrary",)))(dysn, y, xbc, z_src, dt_src, s_prev_all, dtb_f, alog_f, dsk_f, snw)


def _adamw_math(w, g, m, v):
    m_n = ADAM_B1 * m + (1.0 - ADAM_B1) * g
    v_n = ADAM_B2 * v + (1.0 - ADAM_B2) * jnp.square(g)
    c1 = 1.0 - ADAM_B1 ** ADAM_STEP
    c2 = 1.0 - ADAM_B2 ** ADAM_STEP
    return -ADAM_LR * ((m_n / c1) / (jnp.sqrt(v_n / c2) + ADAM_EPS) + ADAM_WD * w), m_n, v_n


def _sum_slots(p_ref):
    acc = p_ref[0].astype(F32)
    for s in range(1, p_ref.shape[0]):
        acc = acc + p_ref[s].astype(F32)
    return acc


def adamw_slots(w, slots, m, v, name):
    rows, cols = w.shape
    tc = 256

    def body(w_ref, s_ref, m_ref, v_ref, g_ref, d_ref, mo_ref, vo_ref):
        g_v = _sum_slots(s_ref)
        g_ref[...] = g_v
        d_ref[...], mo_ref[...], vo_ref[...] = _adamw_math(w_ref[...], g_v, m_ref[...], v_ref[...])

    spec = pl.BlockSpec((rows, tc), lambda i: (0, i))
    return pl.pallas_call(body, name=name, grid=(cols // tc,), out_shape=[_sds((rows, cols), F32)] * 4,
                          in_specs=[spec, pl.BlockSpec((slots.shape[0], rows, tc), lambda i: (0, 0, i)), spec, spec], out_specs=[spec] * 4,
                          compiler_params=_cparams(("parallel",)))(w, slots, m, v)


def adamw_many(ws, gs, ms, vs, name):
    n = len(ws)

    def body(*refs):
        for p in range(n):
            d_v, m_v, v_v = _adamw_math(refs[p][...], refs[n + p][...], refs[2 * n + p][...], refs[3 * n + p][...])
            refs[4 * n + p][...] = d_v
            refs[5 * n + p][...] = m_v
            refs[6 * n + p][...] = v_v

    vm = pl.BlockSpec(memory_space=pltpu.VMEM)
    out = pl.pallas_call(body, name=name, out_shape=[_sds(w.shape, F32) for w in ws] * 3, in_specs=[vm] * (4 * n),
                         out_specs=[vm] * (3 * n), compiler_params=_cparams())(*ws, *gs, *ms, *vs)
    return out[:n], out[n:2 * n], out[2 * n:]


def _pack_layout(shapes):
    row, layout = 0, []
    for rows, cols in shapes:
        chunks = []
        for c0 in range(0, cols, D):
            chunks.append((row, c0, min(D, cols - c0)))
            row += rows
        layout.append(chunks)
    return row, layout


def pack_rows(entries, name):
    arrays = [e[0] for e in entries]
    used, layout = _pack_layout([(e[2], e[0].shape[1]) for e in entries])
    total = -(-used // SUBLANES) * SUBLANES
    n = len(arrays)

    def body(*refs):
        o_ref = refs[n]
        o_ref[...] = jnp.zeros((total, D), F32)
        for p in range(n):
            _, first, rows = entries[p]
            for r0, c0, w in layout[p]:
                o_ref[r0:r0 + rows, 0:w] = refs[p][first:first + rows, c0:c0 + w]

    vm = pl.BlockSpec(memory_space=pltpu.VMEM)
    return pl.pallas_call(body, name=name, out_shape=_sds((total, D), F32), in_specs=[vm] * n, out_specs=vm,
                          compiler_params=_cparams())(*arrays)


def unpack_rows(packed, shapes):
    _, layout = _pack_layout(shapes)
    out = []
    for (rows, _), chunks in zip(shapes, layout):
        parts = [packed[..., r0:r0 + rows, 0:w] for r0, _, w in chunks]
        out.append(parts[0] if len(parts) == 1 else jnp.concatenate(parts, axis=-1))
    return out


def sum_slots_many(parts, name):
    n = len(parts)

    def body(*refs):
        for p in range(n):
            refs[n + p][...] = _sum_slots(refs[p])

    vm = pl.BlockSpec(memory_space=pltpu.VMEM)
    return pl.pallas_call(body, name=name, out_shape=[_sds(p.shape[1:], F32) for p in parts], in_specs=[vm] * n,
                          out_specs=[vm] * n, compiler_params=_cparams())(*parts)


def ada_mod(c_all, ada_w_shard, ada_b_cols, name):
    def body(c_ref, w_ref, b_ref, o_ref, ca_ref):
        ca = _silu(c_ref[...])
        ca_ref[...] = ca
        o_ref[...] = _dot(ca, w_ref[...]) + b_ref[...]

    vm = pl.BlockSpec(memory_space=pltpu.VMEM)
    return pl.pallas_call(body, name=name, out_shape=[_sds((N_DEV, ada_w_shard.shape[1]), F32), _sds((N_DEV, D), F32)],
                          in_specs=[vm, vm, vm], out_specs=[vm, vm], compiler_params=_cparams())(c_all, ada_w_shard, ada_b_cols)


def ada_wgrad(c_act_all, dmod_cols, name):
    def body(c_ref, d_ref, o_ref):
        o_ref[...] = _dot_tn_hi(c_ref[...], d_ref[...])

    vm = pl.BlockSpec(memory_space=pltpu.VMEM)
    return pl.pallas_call(body, name=name, out_shape=_sds((D, dmod_cols.shape[1]), F32), in_specs=[vm, vm], out_specs=vm,
                          compiler_params=_cparams())(c_act_all, dmod_cols)


def exchange(srcs, name, gather):
    n = len(srcs)
    gathers = [gather] * n if isinstance(gather, bool) else list(gather)
    shapes = [tuple(s.shape) if g else tuple(s.shape[1:]) for s, g in zip(srcs, gathers)]

    def body(*refs):
        src_refs, out_refs = refs[:n], refs[n:2 * n]
        send_sems, recv_sems, local_sems = refs[2 * n:]
        x, y, c = lax.axis_index("x"), lax.axis_index("y"), lax.axis_index("c")
        me = 4 * x + 2 * y + c

        def peer(k):
            bx, by, bc = (k >> 2) & 1, (k >> 1) & 1, k & 1
            px, py, pc = (x + bx) % 2, (y + by) % 2, (c + bc) % 2
            return (px, py, pc), 4 * px + 2 * py + pc

        def copy(a, k, landing):
            dev, idx = peer(k)
            return pltpu.make_async_remote_copy(
                src_ref=src_refs[a] if gathers[a] else src_refs[a].at[idx], dst_ref=out_refs[a].at[idx if landing else me],
                send_sem=send_sems.at[a, k - 1], recv_sem=recv_sems.at[a, k - 1],
                device_id=dev, device_id_type=pl.DeviceIdType.MESH)

        mine = [pltpu.make_async_copy(src_refs[a] if gathers[a] else src_refs[a].at[me], out_refs[a].at[me], local_sems.at[a])
                for a in range(n)]
        for cp in mine:
            cp.start()
        sends = [copy(a, k, False) for a in range(n) for k in range(1, N_DEV)]
        for cp in sends:
            cp.start()
        for a in range(n):
            for k in range(1, N_DEV):
                copy(a, k, True).wait_recv()
        for cp in sends:
            cp.wait_send()
        for cp in mine:
            cp.wait()

    hbm = pl.BlockSpec(memory_space=pl.ANY)
    return pl.pallas_call(
        body, name=name, out_shape=[_sds((N_DEV,) + shp, s.dtype) for shp, s in zip(shapes, srcs)], in_specs=[hbm] * n,
        out_specs=[hbm] * n,
        scratch_shapes=[pltpu.SemaphoreType.DMA((n, N_DEV - 1)), pltpu.SemaphoreType.DMA((n, N_DEV - 1)),
                        pltpu.SemaphoreType.DMA((n,))],
        compiler_params=pltpu.CompilerParams(has_side_effects=True))(*srcs)


def gather_two_level(srcs, name):
    n = len(srcs)

    def body(*refs):
        src_refs, out_refs = refs[:n], refs[n:2 * n]
        send_sems, recv_sems, local_sems = refs[2 * n:]
        x, y, c = lax.axis_index("x"), lax.axis_index("y"), lax.axis_index("c")
        me, sibling = (x, y, c), (x, y, 1 - c)
        chips = [(1 - x, y), (x, 1 - y), (1 - x, 1 - y)]

        def slot(a, px, py, pc):
            return out_refs[a].at[4 * px + 2 * py + pc]

        def copy(a, k, block, to, src=None):
            return pltpu.make_async_remote_copy(
                src_ref=slot(a, *block) if src is None else src, dst_ref=slot(a, *block), send_sem=send_sems.at[a, k],
                recv_sem=recv_sems.at[a, k], device_id=to, device_id_type=pl.DeviceIdType.MESH)

        mine = [pltpu.make_async_copy(src_refs[a], slot(a, *me), local_sems.at[a]) for a in range(n)]
        for cp in mine:
            cp.start()
        first = []
        for a in range(n):
            first += [copy(a, 0, me, sibling, src=src_refs[a])]
            first += [copy(a, 1 + j, me, (*chip, c), src=src_refs[a]) for j, chip in enumerate(chips)]
        for cp in first:
            cp.start()
        passed = []
        for a in range(n):
            for j, chip in enumerate(chips):
                copy(a, 1 + j, (*chip, c), me).wait_recv()
                passed.append(copy(a, 4 + j, (*chip, c), sibling))
                passed[-1].start()
        for a in range(n):
            copy(a, 0, sibling, me).wait_recv()
            for j, chip in enumerate(chips):
                copy(a, 4 + j, (*chip, 1 - c), me).wait_recv()
        for cp in first + passed:
            cp.wait_send()
        for cp in mine:
            cp.wait()

    hbm = pl.BlockSpec(memory_space=pl.ANY)
    return pl.pallas_call(
        body, name=name, out_shape=[_sds((N_DEV,) + tuple(s.shape), s.dtype) for s in srcs], in_specs=[hbm] * n,
        out_specs=[hbm] * n,
        scratch_shapes=[pltpu.SemaphoreType.DMA((n, N_DEV - 1)), pltpu.SemaphoreType.DMA((n, N_DEV - 1)),
                        pltpu.SemaphoreType.DMA((n,))],
        compiler_params=pltpu.CompilerParams(has_side_effects=True))(*srcs)


def _peer(k):
    x, y, c = lax.axis_index("x"), lax.axis_index("y"), lax.axis_index("c")
    px, py, pc = (x + ((k >> 2) & 1)) % 2, (y + ((k >> 1) & 1)) % 2, (c + (k & 1)) % 2
    return (px, py, pc), 4 * px + 2 * py + pc


def _my_slot():
    return 4 * lax.axis_index("x") + 2 * lax.axis_index("y") + lax.axis_index("c")


_HBM = pl.BlockSpec(memory_space=pltpu.HBM)
_SEM = pl.BlockSpec(memory_space=pltpu.SEMAPHORE)
_EFFECT = pltpu.SideEffectType.DATAFLOW_SIDE_EFFECTING


def exchange_start(srcs, name, gather):
    n = len(srcs)
    shapes = [tuple(s.shape) if gather else tuple(s.shape[1:]) for s in srcs]
    lands = [lax.empty((N_DEV,) + shp, s.dtype) for shp, s in zip(shapes, srcs)]

    def body(*refs):
        src_refs, land_refs = refs[:n], refs[n:2 * n]
        sems = refs[2 * n:4 * n]
        token = refs[-1]
        me = _my_slot()
        for a in range(n):
            for k in range(1, N_DEV):
                dev, idx = _peer(k)
                pltpu.make_async_remote_copy(
                    src_ref=src_refs[a] if gather else src_refs[a].at[idx], dst_ref=land_refs[a].at[me],
                    send_sem=sems[2 * a].at[k - 1], recv_sem=sems[2 * a + 1].at[k - 1],
                    device_id=dev, device_id_type=pl.DeviceIdType.MESH).start()
        token[...] = jnp.zeros_like(token)

    out_shape = ([pltpu.SemaphoreType.DMA((N_DEV - 1,))] * (2 * n) + [pltpu.HBM(s.shape, s.dtype) for s in srcs]
                 + [pltpu.HBM(l.shape, l.dtype) for l in lands] + [_sds((8, LANE), F32)])
    out = pl.pallas_call(
        body, name=name, out_shape=out_shape, in_specs=[_HBM] * (2 * n),
        out_specs=[_SEM] * (2 * n) + [_HBM] * (2 * n) + [pl.BlockSpec(memory_space=pltpu.VMEM)],
        input_output_aliases={i: 2 * n + i for i in range(2 * n)},
        compiler_params=pltpu.CompilerParams(has_side_effects=_EFFECT))(
            *[pltpu.with_memory_space_constraint(s, pltpu.HBM) for s in srcs],
            *[pltpu.with_memory_space_constraint(l, pltpu.HBM) for l in lands])
    parts = [(out[2 * a], out[2 * a + 1], out[2 * n + a], out[3 * n + a]) for a in range(n)]
    return parts, out[-1]


def exchange_wait(parts, after, name, gather):
    n = len(parts)

    def body(*refs):
        src_refs, land_refs = refs[:n], refs[n:2 * n]
        sems = refs[2 * n:4 * n]
        for a in range(n):
            for k in range(1, N_DEV):
                dev, idx = _peer(k)
                copy = pltpu.make_async_remote_copy(
                    src_ref=src_refs[a] if gather else src_refs[a].at[idx], dst_ref=land_refs[a].at[idx],
                    send_sem=sems[2 * a].at[k - 1], recv_sem=sems[2 * a + 1].at[k - 1],
                    device_id=dev, device_id_type=pl.DeviceIdType.MESH)
                copy.wait_send()
                copy.wait_recv()

    srcs = [p[2] for p in parts]
    lands = [p[3] for p in parts]
    sems = [s for p in parts for s in p[:2]]
    out = pl.pallas_call(
        body, name=name, out_shape=[pltpu.HBM(a.shape, a.dtype) for a in srcs + lands],
        in_specs=[_HBM] * (2 * n) + [_SEM] * (2 * n) + [pl.BlockSpec(memory_space=pl.ANY)], out_specs=[_HBM] * (2 * n),
        input_output_aliases={i: i for i in range(2 * n)},
        compiler_params=pltpu.CompilerParams(has_side_effects=_EFFECT))(*srcs, *lands, *sems, after)
    return list(zip(out[:n], out[n:]))


def _cols_to_slabs(g):
    r, c = g.shape
    return g.reshape(r, N_DEV, c // N_DEV).transpose(1, 0, 2)


def _slabs_to_cols(s):
    _, r, cs = s.shape
    return s.transpose(1, 0, 2).reshape(r, N_DEV * cs)


def _rep_heads(v):
    return jnp.repeat(v.reshape(N_HEADS), HEAD).reshape(1, D_SSD)


def local_fwd_bwd(x, target, mod, get_w, put_grad, small):
    n1w, n2w, fnw = small["norm1_w"], small["norm2_w"], small["final_norm_w"]
    dtb_f, alog_f, dsk_f = _rep_heads(small["dt_bias"]), _rep_heads(small["a_log"]), _rep_heads(small["d_skip"])
    snw = small["ssd_norm_w"]

    def after(v, token):
        return v + token[0:1, 0:1]

    w_in = get_w("w_in", mod)
    h1, proj_zx, proj_dt, proj_cf = in_proj(x, mod, n1w, w_in["w_full"], w_in["w_dt_rep"], w_in["w_cf"], "norm1_in_proj")
    xbc = conv_silu_fwd(proj_zx, D_SSD // CB, D_XBC, small["ssd_conv_w"], small["ssd_conv_b"], "ssd_conv")
    y, ysn, s_prev = ssd_fwd(xbc, proj_zx, proj_dt, dtb_f, alog_f, dsk_f, snw, "ssd_scan")
    uc = conf_conv_fwd(proj_cf, 0, D_CONF // CB, small["conf_conv_w"], small["conf_conv_b"], "conf_conv")
    w_out = get_w("w_out", uc)
    mix, u, x1, h2 = mixer_out(ysn, uc, small["conf_ln_w"], small["conf_ln_b"], w_out, x, mod, n2w, "out_proj_norm2")
    w_up_t = get_w("w_up", h2)
    up = mm_nt([(h2, w_up_t, 0)], "up_proj")
    act = ffn_conv_fwd(up, small["ffn_conv_w"], small["ffn_conv_b"], "ffn_conv")
    w_down = get_w("w_down", act)
    dx2, dff, dact, acc_f = final_loss(act, w_down, x1, mod, fnw, target, "down_proj_loss")

    token = put_grad("w_down", mm_tn(act, dff, "wgrad_down"))
    dupg, dupv, dwg, dwv = ffn_conv_bwd(up, small["ffn_conv_w"], after(small["ffn_conv_b"], token), dact, "ffn_conv_bwd")
    token = put_grad("w_up", mm_tn_stack([dupg, dupv], h2, "wgrad_up"))
    dx1, dmix, acc_2 = norm_mod_bwd([(dupg, w_up_t, 0), (dupv, w_up_t, 1)], x1, dx2, mod, after(n2w, token), 3, "norm2_bwd",
                                    mix=mix, gate_row=2)

    token = put_grad("w_out", mm_tn_stack([ysn, u], dmix, "wgrad_out"))
    dysn, duc, acc_ln = mixer_out_bwd(dmix, w_out, uc, after(small["conf_ln_w"], token), small["conf_ln_b"], "out_proj_bwd")
    dcfa, dcfg, dw_cc = conf_conv_bwd(proj_cf, 0, D_CONF // CB, small["conf_conv_w"], duc, "conf_conv_bwd")
    dz, ddt, dxbc_post, acc_s, acc_s16 = ssd_bwd(dysn, y, xbc, proj_zx, proj_dt, s_prev, dtb_f, alog_f, dsk_f, snw,
                                                 "ssd_scan_bwd")
    dxbc, dw_sc = conv_silu_bwd(proj_zx, D_SSD // CB, D_XBC, small["ssd_conv_w"], small["ssd_conv_b"], dxbc_post, "ssd_conv_bwd")
    token = put_grad("w_in", mm_tn_concat(
        [(dz, D_SSD), (dxbc, D_XBC), (ddt, N_HEADS), (dcfa, D_CONF), (dcfg, D_CONF)], h1, "wgrad_in"))
    dh1_pairs = [(dz, w_in["w_full"], 0), (ddt, w_in["w_dt16"], 0), (dcfa, w_in["w_cf"], 0), (dcfg, w_in["w_cf"], 1),
                 (dxbc, w_in["w_xbc"], 0)]
    grad_x, acc_1 = norm_mod_bwd(dh1_pairs, x, dx1, mod, after(n1w, token), 0, "norm1_bwd")

    small_accs = dict(acc_1=acc_1, acc_2=acc_2, acc_f=acc_f, acc_ln=acc_ln, acc_s=acc_s, acc_s16=acc_s16, dw_sc=dw_sc,
                      dw_cc=dw_cc, dwg=dwg, dwv=dwv)
    return grad_x, small_accs


def kernel(x, c, ada_w, ada_b, norm1_w, w_in, ssd_conv_w, ssd_conv_b, dt_bias, a_log, d_skip, ssd_norm_w, conf_conv_w, conf_conv_b, conf_ln_w, conf_ln_b, w_out, norm2_w, w_up, ffn_conv_w, ffn_conv_b, w_down, final_norm_w, loss_target, m_ada_w, m_ada_b, m_norm1_w, m_w_in, m_ssd_conv_w, m_ssd_conv_b, m_dt_bias, m_a_log, m_d_skip, m_ssd_norm_w, m_conf_conv_w, m_conf_conv_b, m_conf_ln_w, m_conf_ln_b, m_w_out, m_norm2_w, m_w_up, m_ffn_conv_w, m_ffn_conv_b, m_w_down, m_final_norm_w, v_ada_w, v_ada_b, v_norm1_w, v_w_in, v_ssd_conv_w, v_ssd_conv_b, v_dt_bias, v_a_log, v_d_skip, v_ssd_norm_w, v_conf_conv_w, v_conf_conv_b, v_conf_ln_w, v_conf_ln_b, v_w_out, v_norm2_w, v_w_up, v_ffn_conv_w, v_ffn_conv_b, v_w_down, v_final_norm_w):
    me = 4 * lax.axis_index("x") + 2 * lax.axis_index("y") + lax.axis_index("c")
    weights = dict(ada_w=ada_w, ada_b=ada_b, norm1_w=norm1_w, w_in=w_in, ssd_conv_w=ssd_conv_w, ssd_conv_b=ssd_conv_b,
                   dt_bias=dt_bias, a_log=a_log, d_skip=d_skip, ssd_norm_w=ssd_norm_w, conf_conv_w=conf_conv_w,
                   conf_conv_b=conf_conv_b, conf_ln_w=conf_ln_w, conf_ln_b=conf_ln_b, w_out=w_out, norm2_w=norm2_w, w_up=w_up,
                   ffn_conv_w=ffn_conv_w, ffn_conv_b=ffn_conv_b, w_down=w_down, final_norm_w=final_norm_w)
    moms_m = dict(ada_w=m_ada_w, ada_b=m_ada_b, norm1_w=m_norm1_w, w_in=m_w_in, ssd_conv_w=m_ssd_conv_w, ssd_conv_b=m_ssd_conv_b,
                  dt_bias=m_dt_bias, a_log=m_a_log, d_skip=m_d_skip, ssd_norm_w=m_ssd_norm_w, conf_conv_w=m_conf_conv_w,
                  conf_conv_b=m_conf_conv_b, conf_ln_w=m_conf_ln_w, conf_ln_b=m_conf_ln_b, w_out=m_w_out, norm2_w=m_norm2_w,
                  w_up=m_w_up, ffn_conv_w=m_ffn_conv_w, ffn_conv_b=m_ffn_conv_b, w_down=m_w_down, final_norm_w=m_final_norm_w)
    moms_v = dict(ada_w=v_ada_w, ada_b=v_ada_b, norm1_w=v_norm1_w, w_in=v_w_in, ssd_conv_w=v_ssd_conv_w, ssd_conv_b=v_ssd_conv_b,
                  dt_bias=v_dt_bias, a_log=v_a_log, d_skip=v_d_skip, ssd_norm_w=v_ssd_norm_w, conf_conv_w=v_conf_conv_w,
                  conf_conv_b=v_conf_conv_b, conf_ln_w=v_conf_ln_w, conf_ln_b=v_conf_ln_b, w_out=v_w_out, norm2_w=v_norm2_w,
                  w_up=v_w_up, ffn_conv_w=v_ffn_conv_w, ffn_conv_b=v_ffn_conv_b, w_down=v_w_down, final_norm_w=v_final_norm_w)
    names = list(weights)

    def to2d(a):
        return a[0] if a.ndim == 3 else a.reshape(1, -1)

    big = ("w_in", "w_out", "w_up", "w_down")

    def rows_of(a):
        return jnp.swapaxes(a, 1, 2)[0] if a.shape[2] != D else a[0]

    shards = [rows_of(weights[n]).astype(BF16) for n in big]

    c_all, scw_all, ccw_all, fcw_all, w_in_slabs = gather_two_level(
        [c.reshape(8, LANE), ssd_conv_w[0], conf_conv_w[0], ffn_conv_w[0], shards[0]], "gather_first")
    c_all = c_all.reshape(N_DEV, D)

    ada_cols = ada_w.shape[2]
    ada_b_cols = lax.dynamic_slice(ada_b, (0, me * ada_cols), (1, ada_cols))
    mod_cols, c_act_all = ada_mod(c_all, ada_w[0], ada_b_cols, "ada_mod")
    mod_parts, = exchange([jnp.pad(mod_cols, ((0, 0), (0, D - ada_cols))).reshape(N_DEV, 8, LANE)], "scatter_mod", gather=False)
    mod = mod_parts.reshape(N_DEV, D)[:, :ada_cols].reshape(6, D)
    mod = jnp.pad(mod, ((0, 2), (0, 0)))

    later, mod = lax.optimization_barrier((shards[1:], mod))
    gather_parts, token = exchange_start(later, "gather_weights_start", gather=True)
    mod = mod + token[0:1, 0:1]

    small = {n: to2d(weights[n]) for n in names if n not in ("ada_w",) + big}
    small["ssd_conv_w"] = _slabs_to_cols(scw_all)
    small["conf_conv_w"] = _slabs_to_cols(ccw_all)
    small["ffn_conv_w"] = _slabs_to_cols(fcw_all)

    def with_own(landed, own):
        return lax.dynamic_update_slice(landed, own[None], (me,) + (0,) * own.ndim)

    def get_w(n, after):
        if n == "w_in":
            slabs = w_in_slabs
        else:
            a = big.index(n)
            (own, landed), = exchange_wait([gather_parts[a - 1]], after, "gather_" + n + "_wait", gather=True)
            slabs = with_own(landed, own)
        full = slabs.reshape(N_DEV * slabs.shape[1], D)
        if n != "w_in":
            return full
        w_dt = full[D_SSD + D_XBC:D_SSD + D_XBC + N_HEADS]
        return dict(w_full=full, w_xbc=full[D_SSD:D_SSD + D_XBC], w_cf=full[D_SSD + D_XBC + N_HEADS:],
                    w_dt_rep=jnp.repeat(w_dt, HEAD, axis=0), w_dt16=jnp.pad(w_dt, ((0, LANE - N_HEADS), (0, 0))))

    scatter_parts = {}

    def put_grad(n, g):
        slabs = g if g.ndim == 3 else g.reshape(N_DEV, g.shape[0] // N_DEV, g.shape[1])
        (scatter_parts[n],), token = exchange_start([slabs.astype(BF16)], "scatter_" + n + "_start", gather=False)
        return token

    grad_x, accs = local_fwd_bwd(x[0], loss_target[0], mod, get_w, put_grad, small)

    grads, delta, new_m, new_v = {}, {}, {}, {}

    def finish(ns, after, name):
        landed = exchange_wait([scatter_parts[n] for n in ns], after, name, gather=False)
        for n, (sent, slots) in zip(ns, landed):
            slots = with_own(slots, lax.dynamic_index_in_dim(sent, me, 0, keepdims=False))
            out = adamw_slots(rows_of(weights[n]), slots, rows_of(moms_m[n]), rows_of(moms_v[n]), "adamw_" + n)
            if weights[n].shape[2] != D:
                out = [jnp.swapaxes(o, 0, 1) for o in out]
            grads[n], delta[n], new_m[n], new_v[n] = out

    finish(big[1:], grad_x, "scatter_grads_wait")

    accs = dict(zip(accs, lax.optimization_barrier((list(accs.values()), [new_v[n] for n in big[1:]]))[0]))
    rep = (("acc_1", 0, 3), ("acc_2", 0, 4), ("acc_f", 0, 3), ("acc_ln", 0, 2), ("acc_s", 0, 1), ("acc_s16", 1, 3),
           ("dw_sc", K_SSD, 1), ("dw_cc", K_CONF, 1), ("dwg", K_FFN, 1), ("dwv", K_FFN, 1))
    shapes = [(rows, accs[k].shape[1]) for k, _, rows in rep]
    conv_slabs = [_cols_to_slabs(accs["dw_sc"][:K_SSD]), _cols_to_slabs(accs["dw_cc"][:K_CONF]),
                  _cols_to_slabs(jnp.concatenate([accs["dwg"][:K_FFN], accs["dwv"][:K_FFN]], axis=1))]
    packed = pack_rows([(accs[k], first, rows) for k, first, rows in rep], "pack_small_grads")
    landed = exchange([packed] + conv_slabs, "exchange_small_grads", gather=[True, False, False, False])
    packed_red, g_scw, g_ccw, g_fcw = sum_slots_many(landed, "sum_small_grads")
    a1_all, a2_all, af_all = unpack_rows(landed[0], shapes)[:3]
    r1, r2, rf, rln, rs, r16, rscb, rccb, rfbg, rfbv = unpack_rows(packed_red, shapes)
    loss = 0.5 / D * jnp.sum(rf[2:3])

    def mod_rows(a1, a2, af):
        return jnp.concatenate([a1[..., 0:2, :], a2[..., 3:4, :], a2[..., 0:2, :], af[..., 1:2, :]], axis=-2)

    dmod_all = mod_rows(a1_all, a2_all, af_all).reshape(N_DEV, 6 * D)
    grads["ada_w"] = ada_wgrad(c_act_all, lax.dynamic_slice(dmod_all, (0, me * ada_cols), (N_DEV, ada_cols)), "ada_wgrad")
    grads.update(
        ada_b=mod_rows(r1, r2, rf).reshape(1, 6 * D), norm1_w=r1[2:3], ssd_conv_w=g_scw, ssd_conv_b=rscb,
        dt_bias=r16[0:1, :N_HEADS], a_log=r16[1:2, :N_HEADS], d_skip=r16[2:3, :N_HEADS], ssd_norm_w=rs,
        conf_conv_w=g_ccw, conf_conv_b=rccb, conf_ln_w=rln[0:1], conf_ln_b=rln[1:2], norm2_w=r2[2:3],
        ffn_conv_w=g_fcw, ffn_conv_b=jnp.concatenate([rfbg, rfbv], axis=1), final_norm_w=rf[0:1])

    rest = [n for n in names if n not in big]
    d_l, m_l, v_l = adamw_many([to2d(weights[n]) for n in rest], [grads[n] for n in rest], [to2d(moms_m[n]) for n in rest],
                               [to2d(moms_v[n]) for n in rest], "adamw_small")
    for n, dd, mm, vv in zip(rest, d_l, m_l, v_l):
        delta[n], new_m[n], new_v[n] = dd, mm, vv
    finish(big[:1], d_l[0], "scatter_w_in_wait")
    shape_of = lambda d_: {n: d_[n].reshape(weights[n].shape) for n in names}
    grads, delta, new_m, new_v = shape_of(grads), shape_of(delta), shape_of(new_m), shape_of(new_v)
    return (loss, grad_x[None], *[grads[n] for n in names], *[delta[n] for n in names], *[new_m[n] for n in names],
            *[new_v[n] for n in names])
```

```python
import functools

import jax
import jax.numpy as jnp
from jax import lax
from jax.experimental import pallas as pl
from jax.experimental.pallas import tpu as pltpu

F32 = jnp.float32
BF16 = jnp.bfloat16
HI = lax.Precision.HIGHEST

N_DEV = 8
D = 1024
D_SSD = 1024
HEAD = 64
N_HEADS = 16
N_STATE = 128
D_XBC = 1536
D_CONF = 1024
D_FF = 2816
K_SSD, K_CONF, K_FFN = 4, 31, 3
LANE = 128
TR = 256
TM = 512
Q = 256
CB = 256
TC = 1024
TCF = 2048
VMEM_LIMIT = 56 * 1024 * 1024

ADAM_LR, ADAM_B1, ADAM_B2, ADAM_EPS, ADAM_WD, ADAM_STEP = 0.001, 0.9, 0.999, 1e-08, 0.01, 10


def _cparams(sem=None):
    return pltpu.CompilerParams(vmem_limit_bytes=VMEM_LIMIT, dimension_semantics=sem)


def _sds(shape, dtype):
    return jax.ShapeDtypeStruct(shape, dtype)


def _sigmoid(x):
    return 1.0 / (1.0 + jnp.exp(-x))


def _silu(x):
    return x * _sigmoid(x)


def _dsilu(x):
    s = _sigmoid(x)
    return s * (1.0 + x * (1.0 - s))


def _softplus(x):
    return jnp.maximum(x, 0.0) + jnp.log(1.0 + jnp.exp(-jnp.abs(x)))


def _dot(a, b):
    return jnp.dot(a.astype(BF16), b.astype(BF16), preferred_element_type=F32)


def _dot_nt(a, b):
    return lax.dot_general(a.astype(BF16), b.astype(BF16), (((1,), (1,)), ((), ())), preferred_element_type=F32)


def _dot_tn(a, b):
    return lax.dot_general(a.astype(BF16), b.astype(BF16), (((0,), (0,)), ((), ())), preferred_element_type=F32)


def _bf16_terms(a, terms):
    parts, rem = [], a
    for t in range(terms):
        p = rem.astype(BF16)
        parts.append(p)
        if t + 1 < terms:
            rem = rem - p.astype(F32)
    return parts


def _dot_exact(a, b, terms, exact, dims=(((1,), (0,)), ((), ()))):
    if exact == "a":
        a_b = a.astype(BF16)
        outs = [lax.dot_general(a_b, p, dims, preferred_element_type=F32) for p in _bf16_terms(b, terms)]
    else:
        b_b = b.astype(BF16)
        outs = [lax.dot_general(p, b_b, dims, preferred_element_type=F32) for p in _bf16_terms(a, terms)]
    acc = outs[-1]
    for o in reversed(outs[:-1]):
        acc = acc + o
    return acc


def _dot_tn_hi(a, b):
    return lax.dot_general(a, b, (((0,), (0,)), ((), ())), precision=HI, preferred_element_type=F32)


def _colsum(x):
    return jnp.sum(x, axis=0, keepdims=True)


def _const_spec(shape):
    return pl.BlockSpec(shape, lambda *_: (0,) * len(shape))


def _col_tile(n):
    for t in (2816, 1408, 1024, 768, 512, 256, 128):
        if n % t == 0 and t <= n:
            return t
    return n


def mm_nt(pairs, name):
    L = pairs[0][0].shape[0]
    K = pairs[0][1].shape[0]
    tk = _col_tile(K)
    n = len(pairs)

    def body(*refs):
        o_ref = refs[-1]
        acc = None
        for p in range(n):
            t = lax.dot_general(refs[2 * p][...], refs[2 * p + 1][...], (((1,), (1,)), ((), ())),
                                preferred_element_type=F32)
            acc = t if acc is None else acc + t
        o_ref[...] = acc

    in_specs, args = [], []
    for a, w, cb in pairs:
        in_specs += [pl.BlockSpec((TM, a.shape[1]), lambda j, i: (i, 0)),
                     pl.BlockSpec((tk, a.shape[1]), functools.partial(lambda j, i, cb: (j, cb), cb=cb))]
        args += [a, w]
    return pl.pallas_call(
        body, name=name, grid=(K // tk, L // TM), out_shape=_sds((L, K), F32), in_specs=in_specs,
        out_specs=pl.BlockSpec((TM, tk), lambda j, i: (i, j)),
        compiler_params=_cparams(("parallel", "parallel")))(*args)


def mm_tn(a, g, name):
    L, M = a.shape
    N = g.shape[1]
    tn = _col_tile(N) if N > 1024 else N
    if M * tn * 4 > 12 * 1024 * 1024:
        tn = 512
    tl = 512 if L % 512 == 0 else TR
    nl = L // tl

    def body(a_ref, g_ref, o_ref, acc_ref):
        @pl.when(pl.program_id(1) == 0)
        def _():
            acc_ref[...] = jnp.zeros((M, tn), F32)

        acc_ref[...] += lax.dot_general(a_ref[...], g_ref[...], (((0,), (0,)), ((), ())), preferred_element_type=F32)

        @pl.when(pl.program_id(1) == nl - 1)
        def _():
            o_ref[...] = acc_ref[...].astype(BF16)

    return pl.pallas_call(
        body, name=name, grid=(N // tn, nl), out_shape=_sds((M, N), BF16),
        in_specs=[pl.BlockSpec((tl, M), lambda j, l: (l, 0)), pl.BlockSpec((tl, tn), lambda j, l: (l, j))],
        out_specs=pl.BlockSpec((M, tn), lambda j, l: (0, j)), scratch_shapes=[pltpu.VMEM((M, tn), F32)],
        compiler_params=_cparams(("parallel", "arbitrary")))(a, g)


def mm_tn_stack(a_list, g, name):
    L, M = a_list[0].shape
    N = g.shape[1]
    n = len(a_list)
    tl = 512 if L % 512 == 0 else TR
    nl = L // tl

    def body(*refs):
        a_refs, g_ref, o_ref, acc_ref = refs[:n], refs[n], refs[n + 1], refs[n + 2]
        j, l = pl.program_id(0), pl.program_id(1)

        @pl.when(l == 0)
        def _():
            acc_ref[...] = jnp.zeros((M, N), F32)

        for p in range(n):
            @pl.when(j == p)
            def _(p=p):
                acc_ref[...] += lax.dot_general(a_refs[p][...], g_ref[...], (((0,), (0,)), ((), ())), preferred_element_type=F32)

        @pl.when(l == nl - 1)
        def _():
            o_ref[...] = acc_ref[...].astype(BF16)

    a_specs = [pl.BlockSpec((tl, M), functools.partial(lambda j, l, p: (jnp.where(j == p, l, 0), 0), p=p)) for p in range(n)]
    return pl.pallas_call(
        body, name=name, grid=(n, nl), out_shape=_sds((n * M, N), BF16),
        in_specs=a_specs + [pl.BlockSpec((tl, N), lambda j, l: (l, 0))],
        out_specs=pl.BlockSpec((M, N), lambda j, l: (j, 0)), scratch_shapes=[pltpu.VMEM((M, N), F32)],
        compiler_params=_cparams(("arbitrary", "arbitrary")))(*a_list, g)


def mm_tn_concat(pieces, g, name):
    L = g.shape[0]
    N = g.shape[1]
    n = len(pieces)
    offsets = [sum(r for _, r in pieces[:p]) for p in range(n + 1)]
    slab = offsets[-1] // N_DEV
    tl = 512 if L % 512 == 0 else TR
    nl = L // tl

    def body(*refs):
        a_refs, g_ref, o_ref, acc_ref = refs[:n], refs[n], refs[n + 1], refs[n + 2]
        l = pl.program_id(0)

        @pl.when(l == 0)
        def _():
            acc_ref[...] = jnp.zeros((offsets[-1], N), F32)

        g_v = g_ref[...]
        for p in range(n):
            t = lax.dot_general(a_refs[p][...], g_v, (((0,), (0,)), ((), ())), preferred_element_type=F32)
            acc_ref[offsets[p]:offsets[p + 1], :] += t[:pieces[p][1], :]

        @pl.when(l == nl - 1)
        def _():
            for s in range(N_DEV):
                o_ref[s] = acc_ref[s * slab:(s + 1) * slab, :].astype(BF16)

    return pl.pallas_call(
        body, name=name, grid=(nl,), out_shape=_sds((N_DEV, slab, N), BF16),
        in_specs=[pl.BlockSpec((tl, a.shape[1]), lambda l: (l, 0)) for a, _ in pieces] + [pl.BlockSpec((tl, N), lambda l: (l, 0))],
        out_specs=_const_spec((N_DEV, slab, N)), scratch_shapes=[pltpu.VMEM((offsets[-1], N), F32)],
        compiler_params=_cparams(("arbitrary",)))(*[a for a, _ in pieces], g)


def _row_spec(width=D):
    return pl.BlockSpec((TR, width), lambda i: (i, 0))


def in_proj(x, mod, n1w, w_full, w_dt_rep, w_cf, name):
    L = x.shape[0]
    n_zx = D_SSD + D_XBC

    def body(x_ref, mod_ref, w_ref, wzx_ref, wdt_ref, wcf_ref, h_ref, zx_ref, dt_ref, cf_ref):
        xin = x_ref[...]
        r = lax.rsqrt(jnp.mean(xin * xin, axis=-1, keepdims=True) + 1e-6)
        h = ((xin * r * w_ref[...]) * (1.0 + mod_ref[1:2, :]) + mod_ref[0:1, :]).astype(BF16)
        h_ref[...] = h
        nt = (((1,), (1,)), ((), ()))
        zx_ref[...] = lax.dot_general(h, wzx_ref[...], nt, preferred_element_type=F32)
        dt_ref[...] = lax.dot_general(h, wdt_ref[...], nt, preferred_element_type=F32)
        cf_ref[...] = lax.dot_general(h, wcf_ref[...], nt, preferred_element_type=F32)

    row = lambda w: pl.BlockSpec((TM, w), lambda i: (i, 0))
    return pl.pallas_call(
        body, name=name, grid=(L // TM,),
        out_shape=[_sds((L, D), BF16), _sds((L, n_zx), F32), _sds((L, D_SSD), F32), _sds((L, 2 * D_CONF), F32)],
        in_specs=[row(D), _const_spec((8, D)), _const_spec((1, D)), _const_spec((n_zx, D)), _const_spec((D_SSD, D)),
                  _const_spec((2 * D_CONF, D))],
        out_specs=[row(D), row(n_zx), row(D_SSD), row(2 * D_CONF)],
        compiler_params=_cparams(("parallel",)))(x, mod, n1w, w_full, w_dt_rep, w_cf)


def mixer_out(ysn, uc, lnw, lnb, w_out, x, mod, n2w, name):
    L = x.shape[0]

    def body(ysn_ref, uc_ref, lnw_ref, lnb_ref, wo_ref, x_ref, mod_ref, n2w_ref, mix_ref, u_ref, x1_ref, h2_ref):
        uc_v = uc_ref[...]
        mu = jnp.mean(uc_v, axis=-1, keepdims=True)
        var = jnp.mean(jnp.square(uc_v - mu), axis=-1, keepdims=True)
        u = _silu((uc_v - mu) * lax.rsqrt(var + 1e-5) * lnw_ref[...] + lnb_ref[...]).astype(BF16)
        u_ref[...] = u
        mix = (jnp.dot(ysn_ref[...], wo_ref[0:D_SSD, :], preferred_element_type=F32)
               + jnp.dot(u, wo_ref[D_SSD:D_SSD + D_CONF, :], preferred_element_type=F32))
        mix_ref[...] = mix
        x1 = x_ref[...] + mod_ref[2:3, :] * mix
        x1_ref[...] = x1
        r = lax.rsqrt(jnp.mean(x1 * x1, axis=-1, keepdims=True) + 1e-6)
        h2_ref[...] = ((x1 * r * n2w_ref[...]) * (1.0 + mod_ref[4:5, :]) + mod_ref[3:4, :]).astype(BF16)

    row = pl.BlockSpec((TM, D), lambda i: (i, 0))
    return pl.pallas_call(
        body, name=name, grid=(L // TM,),
        out_shape=[_sds((L, D), F32), _sds((L, D_CONF), BF16), _sds((L, D), F32), _sds((L, D), BF16)],
        in_specs=[row, row, _const_spec((1, D)), _const_spec((1, D)), _const_spec((D_SSD + D_CONF, D)), row,
                  _const_spec((8, D)), _const_spec((1, D))],
        out_specs=[row] * 4, compiler_params=_cparams(("parallel",)))(ysn, uc, lnw, lnb, w_out, x, mod, n2w)


def mixer_out_bwd(dmix, w_out, uc, lnw, lnb, name):
    L = uc.shape[0]

    def body(dm_ref, wo_ref, u_ref, w_ref, b_ref, dy_ref, o_ref, acc_ref):
        @pl.when(pl.program_id(0) == 0)
        def _():
            acc_ref[...] = jnp.zeros((8, D), F32)

        nt = (((1,), (1,)), ((), ()))
        dm = dm_ref[...]
        dy_ref[...] = lax.dot_general(dm, wo_ref[0:D_SSD, :], nt, preferred_element_type=F32)
        du = lax.dot_general(dm, wo_ref[D_SSD:D_SSD + D_CONF, :], nt, preferred_element_type=F32)
        u = u_ref[...]
        mu = jnp.mean(u, axis=-1, keepdims=True)
        rl = lax.rsqrt(jnp.mean(jnp.square(u - mu), axis=-1, keepdims=True) + 1e-5)
        n = (u - mu) * rl
        v = n * w_ref[...] + b_ref[...]
        dv = du * _dsilu(v)
        acc_ref[0:1, :] += _colsum(dv * n)
        acc_ref[1:2, :] += _colsum(dv)
        dn = dv * w_ref[...]
        o_ref[...] = rl * (dn - jnp.mean(dn, axis=-1, keepdims=True) - n * jnp.mean(dn * n, axis=-1, keepdims=True))

    row = pl.BlockSpec((TM, D), lambda i: (i, 0))
    return pl.pallas_call(body, name=name, grid=(L // TM,), out_shape=[_sds((L, D), F32), _sds((L, D), F32), _sds((8, D), F32)],
                          in_specs=[row, _const_spec((D_SSD + D_CONF, D)), row, _const_spec((1, D)), _const_spec((1, D))],
                          out_specs=[row, row, _const_spec((8, D))],
                          compiler_params=_cparams(("arbitrary",)))(dmix, w_out, uc, lnw, lnb)


def final_loss(act, w_down, x1, mod, fw, target, name):
    L = act.shape[0]

    def body(act_ref, wd_ref, x1_ref, mod_ref, fw_ref, t_ref, dx_ref, dff_ref, dact_ref, acc_ref):
        @pl.when(pl.program_id(0) == 0)
        def _():
            acc_ref[...] = jnp.zeros((8, D), F32)

        ff_v = jnp.dot(act_ref[...], wd_ref[...], preferred_element_type=F32)
        g2 = mod_ref[5:6, :]
        x2 = x1_ref[...] + g2 * ff_v
        r = lax.rsqrt(jnp.mean(x2 * x2, axis=-1, keepdims=True) + 1e-6)
        n = x2 * r
        err = n * fw_ref[...] - t_ref[...]
        dy = err * (1.0 / D)
        dn = dy * fw_ref[...]
        dx2 = r * (dn - n * jnp.mean(dn * n, axis=-1, keepdims=True))
        acc_ref[0:1, :] += _colsum(dy * n)
        acc_ref[1:2, :] += _colsum(dx2 * ff_v)
        acc_ref[2:3, :] += _colsum(err * err)
        dx_ref[...] = dx2
        dff = (dx2 * g2).astype(BF16)
        dff_ref[...] = dff
        dact_ref[...] = lax.dot_general(dff, wd_ref[...], (((1,), (1,)), ((), ())), preferred_element_type=F32)

    return pl.pallas_call(
        body, name=name, grid=(L // TR,),
        out_shape=[_sds((L, D), F32), _sds((L, D), BF16), _sds((L, D_FF), F32), _sds((8, D), F32)],
        in_specs=[_row_spec(D_FF), _const_spec((D_FF, D)), _row_spec(), _const_spec((8, D)), _const_spec((1, D)), _row_spec()],
        out_specs=[_row_spec(), _row_spec(), _row_spec(D_FF), _const_spec((8, D))],
        compiler_params=_cparams(("arbitrary",)))(act, w_down, x1, mod, fw, target)


def norm_mod_bwd(dh_pairs, xin, dres, mod, w, shift_row, name, mix=None, gate_row=None):
    L = xin.shape[0]
    has_mix = mix is not None
    n_pairs = len(dh_pairs)

    def body(*refs):
        pair_refs, refs = refs[:2 * n_pairs], refs[2 * n_pairs:]
        if has_mix:
            x_ref, dres_ref, mod_ref, w_ref, mix_ref, dx_ref, dmix_ref, acc_ref = refs
        else:
            x_ref, dres_ref, mod_ref, w_ref, dx_ref, acc_ref = refs

        @pl.when(pl.program_id(0) == 0)
        def _():
            acc_ref[...] = jnp.zeros((8, D), F32)

        dh_v = None
        for p in range(n_pairs):
            t = jnp.dot(pair_refs[2 * p][...], pair_refs[2 * p + 1][...], preferred_element_type=F32)
            dh_v = t if dh_v is None else dh_v + t
        x = x_ref[...]
        r = lax.rsqrt(jnp.mean(x * x, axis=-1, keepdims=True) + 1e-6)
        n = x * r
        nw = n * w_ref[...]
        sc1 = 1.0 + mod_ref[shift_row + 1:shift_row + 2, :]
        acc_ref[0:1, :] += _colsum(dh_v)
        acc_ref[1:2, :] += _colsum(dh_v * nw)
        dnw = dh_v * sc1
        acc_ref[2:3, :] += _colsum(dnw * n)
        dn = dnw * w_ref[...]
        dx = r * (dn - n * jnp.mean(dn * n, axis=-1, keepdims=True)) + dres_ref[...]
        dx_ref[...] = dx
        if has_mix:
            acc_ref[3:4, :] += _colsum(dx * mix_ref[...])
            dmix_ref[...] = (dx * mod_ref[gate_row:gate_row + 1, :]).astype(BF16)

    ins, in_specs = [], []
    for a, wt, rb in dh_pairs:
        ins += [a, wt]
        in_specs += [_row_spec(a.shape[1]), pl.BlockSpec((a.shape[1], D), functools.partial(lambda i, rb: (rb, 0), rb=rb))]
    ins += [xin, dres, mod, w] + ([mix] if has_mix else [])
    in_specs += [_row_spec(), _row_spec(), _const_spec((8, D)), _const_spec((1, D))] + ([_row_spec()] if has_mix else [])
    out_shape = [_sds((L, D), F32)] + ([_sds((L, D), BF16)] if has_mix else []) + [_sds((8, D), F32)]
    out_specs = [_row_spec()] + ([_row_spec()] if has_mix else []) + [_const_spec((8, D))]
    return pl.pallas_call(body, name=name, grid=(L // TR,), out_shape=out_shape, in_specs=in_specs,
                          out_specs=out_specs, compiler_params=_cparams(("arbitrary",)))(*ins)


def _halo(k):
    return 8 if k <= 9 else 32


def _prev_spec(h, col0, tc=TC):
    return pl.BlockSpec((h, CB), lambda j, i: (jnp.maximum(i * (tc // h) - 1, 0), j + col0))


def _next_spec(h, col0, n_tiles):
    return pl.BlockSpec((h, CB), lambda j, i: (jnp.minimum(i + 1, n_tiles - 1) * (TC // h), j + col0))


def _tile_spec(col0, tc=TC):
    return pl.BlockSpec((tc, CB), lambda j, i: (i, j + col0))


def _w_spec(kp, col0):
    return pl.BlockSpec((kp, CB), lambda j, i: (0, j + col0))


SUBLANES = 8


def _shifted_windows(v, taps, rows):
    for r in range(SUBLANES):
        group = [(o, k) for o, k in taps if o % SUBLANES == r]
        if not group:
            continue
        s = v if r == 0 else pltpu.roll(v, v.shape[0] - r, 0)
        for o, k in group:
            yield k, s[o - r:o - r + rows, :]


def _causal_taps(ext_ref, w_ref, k_taps, first, rows):
    acc = None
    for k, win in _shifted_windows(ext_ref[...], [(first - (k_taps - 1) + k, k) for k in range(k_taps)], rows):
        t = w_ref[k:k + 1, :] * win
        acc = t if acc is None else acc + t
    return acc


def _anticausal_taps(d_ref, w_ref, k_taps, rows):
    acc = None
    for k, win in _shifted_windows(d_ref[...], [(k_taps - 1 - k, k) for k in range(k_taps)], rows):
        t = w_ref[k:k + 1, :] * win
        acc = t if acc is None else acc + t
    return acc


def _acc_conv_wgrad(dw_ref, d_tile, ext_ref, k_taps, first):
    for k, win in _shifted_windows(ext_ref[...], [(first - (k_taps - 1) + k, k) for k in range(k_taps)], TC):
        dw_ref[k:k + 1, :] += _colsum(d_tile * win)
    dw_ref[k_taps:k_taps + 1, :] += _colsum(d_tile)


def conv_silu_fwd(x, col0, width, w, b, name):
    L = x.shape[0]
    k_taps = w.shape[0]
    h = _halo(k_taps)

    def body(xp_ref, x_ref, w_ref, b_ref, o_ref, ext_ref):
        i = pl.program_id(1)
        ext_ref[0:h, :] = jnp.where(i > 0, xp_ref[...], 0.0)
        ext_ref[h:h + TCF, :] = x_ref[...]
        o_ref[...] = _silu(_causal_taps(ext_ref, w_ref, k_taps, h, TCF) + b_ref[...])

    return pl.pallas_call(
        body, name=name, grid=(width // CB, L // TCF), out_shape=_sds((L, width), F32),
        in_specs=[_prev_spec(h, col0, TCF), _tile_spec(col0, TCF), _w_spec(k_taps, 0), pl.BlockSpec((1, CB), lambda j, i: (0, j))],
        out_specs=_tile_spec(0, TCF), scratch_shapes=[pltpu.VMEM((h + TCF, CB), F32)],
        compiler_params=_cparams(("parallel", "parallel")))(x, x, w, b)


def conv_silu_bwd(x, col0, width, w, b, dpost, name):
    L = x.shape[0]
    k_taps = w.shape[0]
    h = _halo(k_taps)
    nt = L // TC

    def body(xp_ref, x_ref, xn_ref, d_ref, dn_ref, w_ref, b_ref, dx_ref, dw_ref, ext_ref, dpre_ref):
        i = pl.program_id(1)

        @pl.when(i == 0)
        def _():
            dw_ref[...] = jnp.zeros((8, CB), F32)

        ext_ref[0:h, :] = jnp.where(i > 0, xp_ref[...], 0.0)
        ext_ref[h:h + TC, :] = x_ref[...]
        ext_ref[h + TC:h + TC + h, :] = xn_ref[...]
        pre = _causal_taps(ext_ref, w_ref, k_taps, h, TC + h) + b_ref[...]
        dpre_ref[0:TC, :] = d_ref[...] * _dsilu(pre[0:TC, :])
        dpre_ref[TC:TC + h, :] = jnp.where(i < nt - 1, dn_ref[...], 0.0) * _dsilu(pre[TC:TC + h, :])
        dx_ref[...] = _anticausal_taps(dpre_ref, w_ref, k_taps, TC).astype(BF16)
        _acc_conv_wgrad(dw_ref, dpre_ref[0:TC, :], ext_ref, k_taps, h)

    return pl.pallas_call(
        body, name=name, grid=(width // CB, nt),
        out_shape=[_sds((L, width), BF16), _sds((8, width), F32)],
        in_specs=[_prev_spec(h, col0), _tile_spec(col0), _next_spec(h, col0, nt), _tile_spec(0), _next_spec(h, 0, nt),
                  _w_spec(k_taps, 0), pl.BlockSpec((1, CB), lambda j, i: (0, j))],
        out_specs=[_tile_spec(0), _w_spec(8, 0)],
        scratch_shapes=[pltpu.VMEM((h + TC + h, CB), F32), pltpu.VMEM((TC + h, CB), F32)],
        compiler_params=_cparams(("parallel", "arbitrary")))(x, x, x, dpost, dpost, w, b)


def conf_conv_fwd(proj, col_a, col_g, w, b, name):
    L = proj.shape[0]
    k_taps = w.shape[0]
    h = _halo(k_taps)

    def body(ap_ref, a_ref, gp_ref, g_ref, w_ref, b_ref, o_ref, ext_ref):
        i = pl.program_id(1)
        ext_ref[0:h, :] = jnp.where(i > 0, ap_ref[...] * _sigmoid(gp_ref[...]), 0.0)
        ext_ref[h:h + TC, :] = a_ref[...] * _sigmoid(g_ref[...])
        o_ref[...] = _causal_taps(ext_ref, w_ref, k_taps, h, TC) + b_ref[...]

    return pl.pallas_call(
        body, name=name, grid=(D_CONF // CB, L // TC), out_shape=_sds((L, D_CONF), F32),
        in_specs=[_prev_spec(h, col_a), _tile_spec(col_a), _prev_spec(h, col_g), _tile_spec(col_g), _w_spec(k_taps, 0),
                  pl.BlockSpec((1, CB), lambda j, i: (0, j))],
        out_specs=_tile_spec(0), scratch_shapes=[pltpu.VMEM((h + TC, CB), F32)],
        compiler_params=_cparams(("parallel", "parallel")))(proj, proj, proj, proj, w, b)


def conf_conv_bwd(proj, col_a, col_g, w, duc, name):
    L = proj.shape[0]
    k_taps = w.shape[0]
    h = _halo(k_taps)
    nt = L // TC

    def body(ap_ref, a_ref, gp_ref, g_ref, d_ref, dn_ref, w_ref, da_ref, dg_ref, dw_ref, ext_ref, dext_ref):
        i = pl.program_id(1)

        @pl.when(i == 0)
        def _():
            dw_ref[...] = jnp.zeros((32, CB), F32)

        a = a_ref[...]
        s = _sigmoid(g_ref[...])
        ext_ref[0:h, :] = jnp.where(i > 0, ap_ref[...] * _sigmoid(gp_ref[...]), 0.0)
        ext_ref[h:h + TC, :] = a * s
        dext_ref[0:TC, :] = d_ref[...]
        dext_ref[TC:TC + h, :] = jnp.where(i < nt - 1, dn_ref[...], 0.0)
        du0 = _anticausal_taps(dext_ref, w_ref, k_taps, TC)
        da_ref[...] = (du0 * s).astype(BF16)
        dg_ref[...] = (du0 * a * s * (1.0 - s)).astype(BF16)
        _acc_conv_wgrad(dw_ref, d_ref[...], ext_ref, k_taps, h)

    return pl.pallas_call(
        body, name=name, grid=(D_CONF // CB, nt),
        out_shape=[_sds((L, D_CONF), BF16), _sds((L, D_CONF), BF16), _sds((32, D_CONF), F32)],
        in_specs=[_prev_spec(h, col_a), _tile_spec(col_a), _prev_spec(h, col_g), _tile_spec(col_g), _tile_spec(0),
                  _next_spec(h, 0, nt), _w_spec(k_taps, 0)],
        out_specs=[_tile_spec(0), _tile_spec(0), _w_spec(32, 0)],
        scratch_shapes=[pltpu.VMEM((h + TC, CB), F32), pltpu.VMEM((TC + h, CB), F32)],
        compiler_params=_cparams(("parallel", "arbitrary")))(proj, proj, proj, proj, duc, duc, w)


def ffn_conv_fwd(up, w, b, name):
    L = up.shape[0]
    k_taps = w.shape[0]
    h = _halo(k_taps)
    cv = D_FF // CB

    def body(gp_ref, g_ref, vp_ref, v_ref, wg_ref, wv_ref, bg_ref, bv_ref, o_ref, eg_ref, ev_ref):
        i = pl.program_id(1)
        eg_ref[0:h, :] = jnp.where(i > 0, gp_ref[...], 0.0)
        eg_ref[h:h + TCF, :] = g_ref[...]
        ev_ref[0:h, :] = jnp.where(i > 0, vp_ref[...], 0.0)
        ev_ref[h:h + TCF, :] = v_ref[...]
        pg = _causal_taps(eg_ref, wg_ref, k_taps, h, TCF) + bg_ref[...]
        pv = _causal_taps(ev_ref, wv_ref, k_taps, h, TCF) + bv_ref[...]
        o_ref[...] = (_silu(pg) * pv).astype(BF16)

    bspec = lambda c0: pl.BlockSpec((1, CB), lambda j, i: (0, j + c0))
    return pl.pallas_call(
        body, name=name, grid=(cv, L // TCF), out_shape=_sds((L, D_FF), BF16),
        in_specs=[_prev_spec(h, 0, TCF), _tile_spec(0, TCF), _prev_spec(h, cv, TCF), _tile_spec(cv, TCF), _w_spec(k_taps, 0),
                  _w_spec(k_taps, cv), bspec(0), bspec(cv)],
        out_specs=_tile_spec(0, TCF), scratch_shapes=[pltpu.VMEM((h + TCF, CB), F32), pltpu.VMEM((h + TCF, CB), F32)],
        compiler_params=_cparams(("parallel", "parallel")))(up, up, up, up, w, w, b, b)


def ffn_conv_bwd(up, w, b, dact, name):
    L = up.shape[0]
    k_taps = w.shape[0]
    h = _halo(k_taps)
    nt = L // TC
    cv = D_FF // CB

    def body(gp_ref, g_ref, gn_ref, vp_ref, v_ref, vn_ref, d_ref, dn_ref, wg_ref, wv_ref, bg_ref, bv_ref,
             dg_ref, dv_ref, dwg_ref, dwv_ref, eg_ref, ev_ref, pg_ref, pv_ref):
        i = pl.program_id(1)

        @pl.when(i == 0)
        def _():
            dwg_ref[...] = jnp.zeros((8, CB), F32)
            dwv_ref[...] = jnp.zeros((8, CB), F32)

        for e_ref, p_ref, c_ref, n_ref in ((eg_ref, gp_ref, g_ref, gn_ref), (ev_ref, vp_ref, v_ref, vn_ref)):
            e_ref[0:h, :] = jnp.where(i > 0, p_ref[...], 0.0)
            e_ref[h:h + TC, :] = c_ref[...]
            e_ref[h + TC:h + TC + h, :] = n_ref[...]
        pg = _causal_taps(eg_ref, wg_ref, k_taps, h, TC + h) + bg_ref[...]
        pv = _causal_taps(ev_ref, wv_ref, k_taps, h, TC + h) + bv_ref[...]
        dact_t = d_ref[...]
        dact_n = jnp.where(i < nt - 1, dn_ref[...], 0.0)
        pg_ref[0:TC, :] = dact_t * pv[0:TC, :] * _dsilu(pg[0:TC, :])
        pg_ref[TC:TC + h, :] = dact_n * pv[TC:TC + h, :] * _dsilu(pg[TC:TC + h, :])
        pv_ref[0:TC, :] = dact_t * _silu(pg[0:TC, :])
        pv_ref[TC:TC + h, :] = dact_n * _silu(pg[TC:TC + h, :])
        dg_ref[...] = _anticausal_taps(pg_ref, wg_ref, k_taps, TC).astype(BF16)
        dv_ref[...] = _anticausal_taps(pv_ref, wv_ref, k_taps, TC).astype(BF16)
        _acc_conv_wgrad(dwg_ref, pg_ref[0:TC, :], eg_ref, k_taps, h)
        _acc_conv_wgrad(dwv_ref, pv_ref[0:TC, :], ev_ref, k_taps, h)

    bspec = lambda c0: pl.BlockSpec((1, CB), lambda j, i: (0, j + c0))
    ext = pltpu.VMEM((h + TC + h, CB), F32)
    dpre = pltpu.VMEM((TC + h, CB), F32)
    return pl.pallas_call(
        body, name=name, grid=(cv, nt),
        out_shape=[_sds((L, D_FF), BF16), _sds((L, D_FF), BF16), _sds((8, D_FF), F32), _sds((8, D_FF), F32)],
        in_specs=[_prev_spec(h, 0), _tile_spec(0), _next_spec(h, 0, nt), _prev_spec(h, cv), _tile_spec(cv), _next_spec(h, cv, nt),
                  _tile_spec(0), _next_spec(h, 0, nt), _w_spec(k_taps, 0), _w_spec(k_taps, cv), bspec(0), bspec(cv)],
        out_specs=[_tile_spec(0), _tile_spec(0), _w_spec(8, 0), _w_spec(8, 0)],
        scratch_shapes=[ext, ext, dpre, dpre],
        compiler_params=_cparams(("parallel", "arbitrary")))(up, up, up, up, up, up, dact, dact, w, w, b, b)


def _ssd_common(xbc_ref, dt_ref, dtb_ref, alog_ref, cs_ref):
    xs = xbc_ref[:, 0:D_SSD]
    sp_in = dt_ref[...] + dtb_ref[...]
    dtf = _softplus(sp_in)
    a_f = -jnp.exp(alog_ref[...])
    a_dt = dtf * a_f
    row = lax.broadcasted_iota(jnp.int32, (Q, Q), 0)
    col = lax.broadcasted_iota(jnp.int32, (Q, Q), 1)
    causal = row >= col
    cs = _dot_exact(causal.astype(F32), a_dt, 3, "a")
    cs_ref[...] = cs
    cs_last = cs_ref[Q - 1:Q, :]
    return xs, sp_in, dtf, a_f, cs, cs_last, causal


def _head_decay(cs_j, cst_ref, e, causal):
    lane = lax.broadcasted_iota(jnp.int32, (Q, LANE), 1)
    rolled = pltpu.roll(cs_j, HEAD, 1)
    own = (lane < HEAD) if e == 0 else (lane >= HEAD)
    col_b = jnp.where(own, cs_j, rolled)
    col_b = jnp.concatenate([col_b] * (Q // LANE), axis=1)
    row_b = cst_ref[e * HEAD:e * HEAD + 1, :]
    return jnp.where(causal, jnp.exp(jnp.minimum(col_b - row_b, 0.0)), 0.0)


def ssd_fwd(xbc, z_src, dt_src, dtb_f, alog_f, dsk_f, snw, name):
    L = xbc.shape[0]
    nc = L // Q

    def body(xbc_ref, z_ref, dt_ref, dtb_ref, alog_ref, dsk_ref, snw_ref, y_ref, yn_ref, sp_ref, s_ref, cs_ref, cst_ref, yd_ref):
        @pl.when(pl.program_id(0) == 0)
        def _():
            s_ref[...] = jnp.zeros((N_STATE, D_SSD), F32)

        xs, _, dtf, a_f, cs, cs_last, causal = _ssd_common(xbc_ref, dt_ref, dtb_ref, alog_ref, cs_ref)
        e_cs = jnp.exp(cs)
        xdt = xs * dtf
        zst = jnp.exp(cs_last - cs) * xdt
        sp_ref[0] = s_ref[...]
        lane = lax.broadcasted_iota(jnp.int32, (Q, LANE), 1)
        for g in range(2):
            gl = slice(g * 512, g * 512 + 512)
            b_g = xbc_ref[:, D_SSD + g * N_STATE:D_SSD + (g + 1) * N_STATE]
            c_g = xbc_ref[:, D_SSD + 2 * N_STATE + g * N_STATE:D_SSD + 2 * N_STATE + (g + 1) * N_STATE]
            s_prev = s_ref[:, gl]
            cb = _dot_nt(c_g, b_g)
            yd_ref[:, gl] = e_cs[:, gl] * _dot(c_g, s_prev)
            for j in range(4):
                tl = slice(g * 512 + j * LANE, g * 512 + (j + 1) * LANE)
                cs_j = cs[:, tl]
                cst_ref[...] = cs_j.T
                x_j = xdt[:, tl]
                o0 = _dot(cb * _head_decay(cs_j, cst_ref, 0, causal), x_j)
                o1 = _dot(cb * _head_decay(cs_j, cst_ref, 1, causal), x_j)
                yd_ref[:, tl] += jnp.where(lane < HEAD, o0, o1)
            s_ref[:, gl] = jnp.exp(cs_last[:, gl]) * s_prev + _dot_tn(b_g, zst[:, gl])
        y = yd_ref[...] + xs * dsk_ref[...]
        y_ref[...] = y
        yz = y * _silu(z_ref[...])
        r = lax.rsqrt(jnp.mean(yz * yz, axis=-1, keepdims=True) + 1e-6)
        yn_ref[...] = (yz * r * snw_ref[...]).astype(BF16)

    chunk = lambda w, c: pl.BlockSpec((Q, w), lambda i: (i, c))
    return pl.pallas_call(
        body, name=name, grid=(nc,),
        out_shape=[_sds((L, D_SSD), F32), _sds((L, D_SSD), BF16), _sds((nc, N_STATE, D_SSD), F32)],
        in_specs=[chunk(D_XBC, 0), chunk(D, 0), chunk(D, 0)] + [_const_spec((1, D))] * 4,
        out_specs=[chunk(D, 0), chunk(D, 0), pl.BlockSpec((1, N_STATE, D_SSD), lambda i: (i, 0, 0))],
        scratch_shapes=[pltpu.VMEM((N_STATE, D_SSD), F32), pltpu.VMEM((Q, D_SSD), F32), pltpu.VMEM((LANE, Q), F32),
                        pltpu.VMEM((Q, D_SSD), F32)],
        compiler_params=_cparams(("arbitrary",)))(xbc, z_src, dt_src, dtb_f, alog_f, dsk_f, snw)


def ssd_bwd(dysn, y, xbc, z_src, dt_src, s_prev_all, dtb_f, alog_f, dsk_f, snw, name):
    L = xbc.shape[0]
    nc = L // Q

    def body(dyn_ref, y_ref, xbc_ref, z_ref, dt_ref, sp_ref, dtb_ref, alog_ref, dsk_ref, snw_ref,
             dz_ref, ddt_ref, dxbc_ref, acc_ref, acc16_ref, ds_ref, cs_ref, cst_ref, dcs_ref, dx_ref):
        step = pl.program_id(0)

        @pl.when(step == 0)
        def _():
            ds_ref[...] = jnp.zeros((N_STATE, D_SSD), F32)
            acc_ref[...] = jnp.zeros((8, D), F32)

        z = z_ref[...]
        y = y_ref[...]
        sz = _sigmoid(z)
        siluz = z * sz
        yz = y * siluz
        r = lax.rsqrt(jnp.mean(yz * yz, axis=-1, keepdims=True) + 1e-6)
        n = yz * r
        dyn = dyn_ref[...]
        acc_ref[0:1, :] += _colsum(dyn * n)
        dn = dyn * snw_ref[...]
        dyz = r * (dn - n * jnp.mean(dn * n, axis=-1, keepdims=True))
        dy = dyz * siluz
        dz_ref[...] = (dyz * y * (sz * (1.0 + z * (1.0 - sz)))).astype(BF16)

        xs, sp_in, dtf, a_f, cs, cs_last, causal = _ssd_common(xbc_ref, dt_ref, dtb_ref, alog_ref, cs_ref)
        acc_ref[3:4, :] += _colsum(dy * xs)
        e_cs = jnp.exp(cs)
        xdt = xs * dtf
        dst = jnp.exp(cs_last - cs)
        zst = dst * xdt
        e_last = jnp.exp(cs_last)
        lane = lax.broadcasted_iota(jnp.int32, (Q, LANE), 1)
        ones = jnp.ones((Q, LANE), F32)
        dcs_last_parts = []
        for g in range(2):
            gl = slice(g * 512, g * 512 + 512)
            b_g = xbc_ref[:, D_SSD + g * N_STATE:D_SSD + (g + 1) * N_STATE]
            c_g = xbc_ref[:, D_SSD + 2 * N_STATE + g * N_STATE:D_SSD + 2 * N_STATE + (g + 1) * N_STATE]
            s_prev = sp_ref[0, :, gl]
            ds_g = ds_ref[:, gl]
            dy_g = dy[:, gl]
            cb = _dot_nt(c_g, b_g)
            y_off = e_cs[:, gl] * _dot(c_g, s_prev)
            edy = e_cs[:, gl] * dy_g
            d_c = _dot_nt(edy, s_prev)
            d_z = _dot(b_g, ds_g)
            d_b = _dot_nt(zst[:, gl], ds_g)
            t_g = d_z * zst[:, gl]
            dcs_ref[:, gl] = dy_g * y_off - t_g
            dx_ref[:, gl] = d_z * dst[:, gl]
            dcs_last_parts.append(_colsum(t_g) + _colsum(ds_g * s_prev) * e_last[:, gl])
            ds_ref[:, gl] = e_last[:, gl] * ds_g + _dot_tn(c_g, edy)
            dcb = jnp.zeros((Q, Q), F32)
            for j in range(4):
                tl = slice(g * 512 + j * LANE, g * 512 + (j + 1) * LANE)
                cs_j = cs[:, tl]
                cst_ref[...] = cs_j.T
                x_j = xdt[:, tl]
                dy_j = dy[:, tl]
                dx_j = jnp.zeros((Q, LANE), F32)
                dcs_j = jnp.zeros((Q, LANE), F32)
                for e in range(2):
                    own = (lane < HEAD) if e == 0 else (lane >= HEAD)
                    w_h = _head_decay(cs_j, cst_ref, e, causal)
                    g_h = cb * w_h
                    dy_m = jnp.where(own, dy_j, 0.0)
                    d_g = _dot_nt(dy_m, x_j)
                    dx_j = dx_j + _dot_tn(g_h, dy_m)
                    dcb = dcb + d_g * w_h
                    p_h = d_g * g_h
                    row_sums = _dot_exact(p_h, ones, 2, "b")
                    col_sums = _dot_exact(p_h, ones, 2, "b", (((0,), (0,)), ((), ())))
                    dcs_j = dcs_j + jnp.where(own, row_sums - col_sums, 0.0)
                dcs_ref[:, tl] += dcs_j * (1.0 / HEAD)
                dx_ref[:, tl] += dx_j
            d_c = d_c + _dot(dcb, b_g)
            d_b = d_b + _dot_tn(dcb, c_g)
            dxbc_ref[:, D_SSD + g * N_STATE:D_SSD + (g + 1) * N_STATE] = d_b
            dxbc_ref[:, D_SSD + 2 * N_STATE + g * N_STATE:D_SSD + 2 * N_STATE + (g + 1) * N_STATE] = d_c
        dcs_last = jnp.concatenate(dcs_last_parts, axis=1)
        anticausal = lax.broadcasted_iota(jnp.int32, (Q, Q), 0) <= lax.broadcasted_iota(jnp.int32, (Q, Q), 1)
        d_adt = _dot_exact(anticausal.astype(F32), dcs_ref[...], 3, "a") + dcs_last
        dx = dx_ref[...]
        acc_ref[2:3, :] += _colsum(d_adt * dtf) * a_f
        d_dtf = d_adt * a_f + dx * xs
        dxbc_ref[:, 0:D_SSD] = dx * dtf + dy * dsk_ref[...]
        d_raw = d_dtf * _sigmoid(sp_in)
        acc_ref[1:2, :] += _colsum(d_raw)
        head_of_lane = lax.broadcasted_iota(jnp.int32, (D_SSD, LANE), 0) // HEAD
        fold = (head_of_lane == lax.broadcasted_iota(jnp.int32, (D_SSD, LANE), 1)).astype(F32)
        ddt_ref[...] = _dot_exact(d_raw, fold, 2, "b").astype(BF16)

        @pl.when(step == nc - 1)
        def _():
            acc16_ref[...] = _dot_exact(acc_ref[...], fold, 3, "b")

    rchunk = lambda w, c: pl.BlockSpec((Q, w), lambda i: (nc - 1 - i, c))
    return pl.pallas_call(
        body, name=name, grid=(nc,),
        out_shape=[_sds((L, D_SSD), BF16), _sds((L, LANE), BF16), _sds((L, D_XBC), F32), _sds((8, D), F32), _sds((8, LANE), F32)],
        in_specs=[rchunk(D, 0), rchunk(D, 0), rchunk(D_XBC, 0), rchunk(D, 0), rchunk(D, 0),
                  pl.BlockSpec((1, N_STATE, D_SSD), lambda i: (nc - 1 - i, 0, 0))] + [_const_spec((1, D))] * 4,
        out_specs=[rchunk(D, 0), rchunk(LANE, 0), rchunk(D_XBC, 0), _const_spec((8, D)), _const_spec((8, LANE))],
        scratch_shapes=[pltpu.VMEM((N_STATE, D_SSD), F32), pltpu.VMEM((Q, D_SSD), F32), pltpu.VMEM((LANE, Q), F32),
                        pltpu.VMEM((Q, D_SSD), F32), pltpu.VMEM((Q, D_SSD), F32)],
        compiler_params=_cparams(("arbitrary",)))(dysn, y, xbc, z_src, dt_src, s_prev_all, dtb_f, alog_f, dsk_f, snw)


def _adamw_math(w, g, m, v):
    m_n = ADAM_B1 * m + (1.0 - ADAM_B1) * g
    v_n = ADAM_B2 * v + (1.0 - ADAM_B2) * jnp.square(g)
    c1 = 1.0 - ADAM_B1 ** ADAM_STEP
    c2 = 1.0 - ADAM_B2 ** ADAM_STEP
    return -ADAM_LR * ((m_n / c1) / (jnp.sqrt(v_n / c2) + ADAM_EPS) + ADAM_WD * w), m_n, v_n


def _sum_slots(p_ref):
    acc = p_ref[0].astype(F32)
    for s in range(1, p_ref.shape[0]):
        acc = acc + p_ref[s].astype(F32)
    return acc


def adamw_slots(w, slots, m, v, name):
    rows, cols = w.shape
    tc = 512

    def body(w_ref, s_ref, m_ref, v_ref, g_ref, d_ref, mo_ref, vo_ref):
        g_v = _sum_slots(s_ref)
        g_ref[...] = g_v
        d_ref[...], mo_ref[...], vo_ref[...] = _adamw_math(w_ref[...], g_v, m_ref[...], v_ref[...])

    spec = pl.BlockSpec((rows, tc), lambda i: (0, i))
    return pl.pallas_call(body, name=name, grid=(cols // tc,), out_shape=[_sds((rows, cols), F32)] * 4,
                          in_specs=[spec, pl.BlockSpec((slots.shape[0], rows, tc), lambda i: (0, 0, i)), spec, spec], out_specs=[spec] * 4,
                          compiler_params=_cparams(("parallel",)))(w, slots, m, v)


def adamw_many(ws, gs, ms, vs, name):
    n = len(ws)

    def body(*refs):
        for p in range(n):
            d_v, m_v, v_v = _adamw_math(refs[p][...], refs[n + p][...], refs[2 * n + p][...], refs[3 * n + p][...])
            refs[4 * n + p][...] = d_v
            refs[5 * n + p][...] = m_v
            refs[6 * n + p][...] = v_v

    vm = pl.BlockSpec(memory_space=pltpu.VMEM)
    out = pl.pallas_call(body, name=name, out_shape=[_sds(w.shape, F32) for w in ws] * 3, in_specs=[vm] * (4 * n),
                         out_specs=[vm] * (3 * n), compiler_params=_cparams())(*ws, *gs, *ms, *vs)
    return out[:n], out[n:2 * n], out[2 * n:]


def _pack_layout(shapes):
    row, layout = 0, []
    for rows, cols in shapes:
        chunks = []
        for c0 in range(0, cols, D):
            chunks.append((row, c0, min(D, cols - c0)))
            row += rows
        layout.append(chunks)
    return row, layout


def pack_rows(entries, name):
    arrays = [e[0] for e in entries]
    used, layout = _pack_layout([(e[2], e[0].shape[1]) for e in entries])
    total = -(-used // SUBLANES) * SUBLANES
    n = len(arrays)

    def body(*refs):
        o_ref = refs[n]
        o_ref[...] = jnp.zeros((total, D), F32)
        for p in range(n):
            _, first, rows = entries[p]
            for r0, c0, w in layout[p]:
                o_ref[r0:r0 + rows, 0:w] = refs[p][first:first + rows, c0:c0 + w]

    vm = pl.BlockSpec(memory_space=pltpu.VMEM)
    return pl.pallas_call(body, name=name, out_shape=_sds((total, D), F32), in_specs=[vm] * n, out_specs=vm,
                          compiler_params=_cparams())(*arrays)


def unpack_rows(packed, shapes):
    _, layout = _pack_layout(shapes)
    out = []
    for (rows, _), chunks in zip(shapes, layout):
        parts = [packed[..., r0:r0 + rows, 0:w] for r0, _, w in chunks]
        out.append(parts[0] if len(parts) == 1 else jnp.concatenate(parts, axis=-1))
    return out


def sum_slots_many(parts, name):
    n = len(parts)

    def body(*refs):
        for p in range(n):
            refs[n + p][...] = _sum_slots(refs[p])

    vm = pl.BlockSpec(memory_space=pltpu.VMEM)
    return pl.pallas_call(body, name=name, out_shape=[_sds(p.shape[1:], F32) for p in parts], in_specs=[vm] * n,
                          out_specs=[vm] * n, compiler_params=_cparams())(*parts)


def ada_mod(c_all, ada_w_shard, ada_b_cols, name):
    def body(c_ref, w_ref, b_ref, o_ref, ca_ref):
        ca = _silu(c_ref[...])
        ca_ref[...] = ca
        o_ref[...] = _dot(ca, w_ref[...]) + b_ref[...]

    vm = pl.BlockSpec(memory_space=pltpu.VMEM)
    return pl.pallas_call(body, name=name, out_shape=[_sds((N_DEV, ada_w_shard.shape[1]), F32), _sds((N_DEV, D), F32)],
                          in_specs=[vm, vm, vm], out_specs=[vm, vm], compiler_params=_cparams())(c_all, ada_w_shard, ada_b_cols)


def ada_wgrad(c_act_all, dmod_cols, name):
    def body(c_ref, d_ref, o_ref):
        o_ref[...] = _dot_tn_hi(c_ref[...], d_ref[...])

    vm = pl.BlockSpec(memory_space=pltpu.VMEM)
    return pl.pallas_call(body, name=name, out_shape=_sds((D, dmod_cols.shape[1]), F32), in_specs=[vm, vm], out_specs=vm,
                          compiler_params=_cparams())(c_act_all, dmod_cols)


def exchange(srcs, name, gather):
    n = len(srcs)
    gathers = [gather] * n if isinstance(gather, bool) else list(gather)
    shapes = [tuple(s.shape) if g else tuple(s.shape[1:]) for s, g in zip(srcs, gathers)]

    def body(*refs):
        src_refs, out_refs = refs[:n], refs[n:2 * n]
        send_sems, recv_sems, local_sems = refs[2 * n:]
        x, y, c = lax.axis_index("x"), lax.axis_index("y"), lax.axis_index("c")
        me = 4 * x + 2 * y + c

        def peer(k):
            bx, by, bc = (k >> 2) & 1, (k >> 1) & 1, k & 1
            px, py, pc = (x + bx) % 2, (y + by) % 2, (c + bc) % 2
            return (px, py, pc), 4 * px + 2 * py + pc

        def copy(a, k, landing):
            dev, idx = peer(k)
            return pltpu.make_async_remote_copy(
                src_ref=src_refs[a] if gathers[a] else src_refs[a].at[idx], dst_ref=out_refs[a].at[idx if landing else me],
                send_sem=send_sems.at[a, k - 1], recv_sem=recv_sems.at[a, k - 1],
                device_id=dev, device_id_type=pl.DeviceIdType.MESH)

        mine = [pltpu.make_async_copy(src_refs[a] if gathers[a] else src_refs[a].at[me], out_refs[a].at[me], local_sems.at[a])
                for a in range(n)]
        for cp in mine:
            cp.start()
        sends = [copy(a, k, False) for a in range(n) for k in range(1, N_DEV)]
        for cp in sends:
            cp.start()
        for a in range(n):
            for k in range(1, N_DEV):
                copy(a, k, True).wait_recv()
        for cp in sends:
            cp.wait_send()
        for cp in mine:
            cp.wait()

    hbm = pl.BlockSpec(memory_space=pl.ANY)
    return pl.pallas_call(
        body, name=name, out_shape=[_sds((N_DEV,) + shp, s.dtype) for shp, s in zip(shapes, srcs)], in_specs=[hbm] * n,
        out_specs=[hbm] * n,
        scratch_shapes=[pltpu.SemaphoreType.DMA((n, N_DEV - 1)), pltpu.SemaphoreType.DMA((n, N_DEV - 1)),
                        pltpu.SemaphoreType.DMA((n,))],
        compiler_params=pltpu.CompilerParams(has_side_effects=True))(*srcs)


def gather_two_level(srcs, name):
    n = len(srcs)

    def body(*refs):
        src_refs, out_refs = refs[:n], refs[n:2 * n]
        send_sems, recv_sems, local_sems = refs[2 * n:]
        x, y, c = lax.axis_index("x"), lax.axis_index("y"), lax.axis_index("c")
        me, sibling = (x, y, c), (x, y, 1 - c)
        chips = [(1 - x, y), (x, 1 - y), (1 - x, 1 - y)]

        def slot(a, px, py, pc):
            return out_refs[a].at[4 * px + 2 * py + pc]

        def copy(a, k, block, to, src=None):
            return pltpu.make_async_remote_copy(
                src_ref=slot(a, *block) if src is None else src, dst_ref=slot(a, *block), send_sem=send_sems.at[a, k],
                recv_sem=recv_sems.at[a, k], device_id=to, device_id_type=pl.DeviceIdType.MESH)

        mine = [pltpu.make_async_copy(src_refs[a], slot(a, *me), local_sems.at[a]) for a in range(n)]
        for cp in mine:
            cp.start()
        first = []
        for a in range(n):
            first += [copy(a, 0, me, sibling, src=src_refs[a])]
            first += [copy(a, 1 + j, me, (*chip, c), src=src_refs[a]) for j, chip in enumerate(chips)]
        for cp in first:
            cp.start()
        passed = []
        for a in range(n):
            for j, chip in enumerate(chips):
                copy(a, 1 + j, (*chip, c), me).wait_recv()
                passed.append(copy(a, 4 + j, (*chip, c), sibling))
                passed[-1].start()
        for a in range(n):
            copy(a, 0, sibling, me).wait_recv()
            for j, chip in enumerate(chips):
                copy(a, 4 + j, (*chip, 1 - c), me).wait_recv()
        for cp in first + passed:
            cp.wait_send()
        for cp in mine:
            cp.wait()

    hbm = pl.BlockSpec(memory_space=pl.ANY)
    return pl.pallas_call(
        body, name=name, out_shape=[_sds((N_DEV,) + tuple(s.shape), s.dtype) for s in srcs], in_specs=[hbm] * n,
        out_specs=[hbm] * n,
        scratch_shapes=[pltpu.SemaphoreType.DMA((n, N_DEV - 1)), pltpu.SemaphoreType.DMA((n, N_DEV - 1)),
                        pltpu.SemaphoreType.DMA((n,))],
        compiler_params=pltpu.CompilerParams(has_side_effects=True))(*srcs)


def _peer(k):
    x, y, c = lax.axis_index("x"), lax.axis_index("y"), lax.axis_index("c")
    px, py, pc = (x + ((k >> 2) & 1)) % 2, (y + ((k >> 1) & 1)) % 2, (c + (k & 1)) % 2
    return (px, py, pc), 4 * px + 2 * py + pc


def _my_slot():
    return 4 * lax.axis_index("x") + 2 * lax.axis_index("y") + lax.axis_index("c")


_HBM = pl.BlockSpec(memory_space=pltpu.HBM)
_SEM = pl.BlockSpec(memory_space=pltpu.SEMAPHORE)
_EFFECT = pltpu.SideEffectType.DATAFLOW_SIDE_EFFECTING


def exchange_start(srcs, name, gather):
    n = len(srcs)
    shapes = [tuple(s.shape) if gather else tuple(s.shape[1:]) for s in srcs]
    lands = [lax.empty((N_DEV,) + shp, s.dtype) for shp, s in zip(shapes, srcs)]

    def body(*refs):
        src_refs, land_refs = refs[:n], refs[n:2 * n]
        sems = refs[2 * n:4 * n]
        token = refs[-1]
        me = _my_slot()
        for a in range(n):
            for k in range(1, N_DEV):
                dev, idx = _peer(k)
                pltpu.make_async_remote_copy(
                    src_ref=src_refs[a] if gather else src_refs[a].at[idx], dst_ref=land_refs[a].at[me],
                    send_sem=sems[2 * a].at[k - 1], recv_sem=sems[2 * a + 1].at[k - 1],
                    device_id=dev, device_id_type=pl.DeviceIdType.MESH).start()
        token[...] = jnp.zeros_like(token)

    out_shape = ([pltpu.SemaphoreType.DMA((N_DEV - 1,))] * (2 * n) + [pltpu.HBM(s.shape, s.dtype) for s in srcs]
                 + [pltpu.HBM(l.shape, l.dtype) for l in lands] + [_sds((8, LANE), F32)])
    out = pl.pallas_call(
        body, name=name, out_shape=out_shape, in_specs=[_HBM] * (2 * n),
        out_specs=[_SEM] * (2 * n) + [_HBM] * (2 * n) + [pl.BlockSpec(memory_space=pltpu.VMEM)],
        input_output_aliases={i: 2 * n + i for i in range(2 * n)},
        compiler_params=pltpu.CompilerParams(has_side_effects=_EFFECT))(
            *[pltpu.with_memory_space_constraint(s, pltpu.HBM) for s in srcs],
            *[pltpu.with_memory_space_constraint(l, pltpu.HBM) for l in lands])
    parts = [(out[2 * a], out[2 * a + 1], out[2 * n + a], out[3 * n + a]) for a in range(n)]
    return parts, out[-1]


def exchange_wait(parts, after, name, gather):
    n = len(parts)

    def body(*refs):
        src_refs, land_refs = refs[:n], refs[n:2 * n]
        sems = refs[2 * n:4 * n]
        for a in range(n):
            for k in range(1, N_DEV):
                dev, idx = _peer(k)
                copy = pltpu.make_async_remote_copy(
                    src_ref=src_refs[a] if gather else src_refs[a].at[idx], dst_ref=land_refs[a].at[idx],
                    send_sem=sems[2 * a].at[k - 1], recv_sem=sems[2 * a + 1].at[k - 1],
                    device_id=dev, device_id_type=pl.DeviceIdType.MESH)
                copy.wait_send()
                copy.wait_recv()

    srcs = [p[2] for p in parts]
    lands = [p[3] for p in parts]
    sems = [s for p in parts for s in p[:2]]
    out = pl.pallas_call(
        body, name=name, out_shape=[pltpu.HBM(a.shape, a.dtype) for a in srcs + lands],
        in_specs=[_HBM] * (2 * n) + [_SEM] * (2 * n) + [pl.BlockSpec(memory_space=pl.ANY)], out_specs=[_HBM] * (2 * n),
        input_output_aliases={i: i for i in range(2 * n)},
        compiler_params=pltpu.CompilerParams(has_side_effects=_EFFECT))(*srcs, *lands, *sems, after)
    return list(zip(out[:n], out[n:]))


def _cols_to_slabs(g):
    r, c = g.shape
    return g.reshape(r, N_DEV, c // N_DEV).transpose(1, 0, 2)


def _slabs_to_cols(s):
    _, r, cs = s.shape
    return s.transpose(1, 0, 2).reshape(r, N_DEV * cs)


def _rep_heads(v):
    return jnp.repeat(v.reshape(N_HEADS), HEAD).reshape(1, D_SSD)


def local_fwd_bwd(x, target, mod, get_w, put_grad, small):
    n1w, n2w, fnw = small["norm1_w"], small["norm2_w"], small["final_norm_w"]
    dtb_f, alog_f, dsk_f = _rep_heads(small["dt_bias"]), _rep_heads(small["a_log"]), _rep_heads(small["d_skip"])
    snw = small["ssd_norm_w"]

    def after(v, token):
        return v + token[0:1, 0:1]

    w_in = get_w("w_in", mod)
    h1, proj_zx, proj_dt, proj_cf = in_proj(x, mod, n1w, w_in["w_full"], w_in["w_dt_rep"], w_in["w_cf"], "norm1_in_proj")
    xbc = conv_silu_fwd(proj_zx, D_SSD // CB, D_XBC, small["ssd_conv_w"], small["ssd_conv_b"], "ssd_conv")
    y, ysn, s_prev = ssd_fwd(xbc, proj_zx, proj_dt, dtb_f, alog_f, dsk_f, snw, "ssd_scan")
    uc = conf_conv_fwd(proj_cf, 0, D_CONF // CB, small["conf_conv_w"], small["conf_conv_b"], "conf_conv")
    w_out = get_w("w_out", uc)
    mix, u, x1, h2 = mixer_out(ysn, uc, small["conf_ln_w"], small["conf_ln_b"], w_out, x, mod, n2w, "out_proj_norm2")
    w_up_t = get_w("w_up", h2)
    up = mm_nt([(h2, w_up_t, 0)], "up_proj")
    act = ffn_conv_fwd(up, small["ffn_conv_w"], small["ffn_conv_b"], "ffn_conv")
    w_down = get_w("w_down", act)
    dx2, dff, dact, acc_f = final_loss(act, w_down, x1, mod, fnw, target, "down_proj_loss")

    token = put_grad("w_down", mm_tn(act, dff, "wgrad_down"))
    dupg, dupv, dwg, dwv = ffn_conv_bwd(up, small["ffn_conv_w"], after(small["ffn_conv_b"], token), dact, "ffn_conv_bwd")
    token = put_grad("w_up", mm_tn_stack([dupg, dupv], h2, "wgrad_up"))
    dx1, dmix, acc_2 = norm_mod_bwd([(dupg, w_up_t, 0), (dupv, w_up_t, 1)], x1, dx2, mod, after(n2w, token), 3, "norm2_bwd",
                                    mix=mix, gate_row=2)

    token = put_grad("w_out", mm_tn_stack([ysn, u], dmix, "wgrad_out"))
    dysn, duc, acc_ln = mixer_out_bwd(dmix, w_out, uc, after(small["conf_ln_w"], token), small["conf_ln_b"], "out_proj_bwd")
    dcfa, dcfg, dw_cc = conf_conv_bwd(proj_cf, 0, D_CONF // CB, small["conf_conv_w"], duc, "conf_conv_bwd")
    dz, ddt, dxbc_post, acc_s, acc_s16 = ssd_bwd(dysn, y, xbc, proj_zx, proj_dt, s_prev, dtb_f, alog_f, dsk_f, snw,
                                                 "ssd_scan_bwd")
    dxbc, dw_sc = conv_silu_bwd(proj_zx, D_SSD // CB, D_XBC, small["ssd_conv_w"], small["ssd_conv_b"], dxbc_post, "ssd_conv_bwd")
    token = put_grad("w_in", mm_tn_concat(
        [(dz, D_SSD), (dxbc, D_XBC), (ddt, N_HEADS), (dcfa, D_CONF), (dcfg, D_CONF)], h1, "wgrad_in"))
    dh1_pairs = [(dz, w_in["w_full"], 0), (ddt, w_in["w_dt16"], 0), (dcfa, w_in["w_cf"], 0), (dcfg, w_in["w_cf"], 1),
                 (dxbc, w_in["w_xbc"], 0)]
    grad_x, acc_1 = norm_mod_bwd(dh1_pairs, x, dx1, mod, after(n1w, token), 0, "norm1_bwd")

    small_accs = dict(acc_1=acc_1, acc_2=acc_2, acc_f=acc_f, acc_ln=acc_ln, acc_s=acc_s, acc_s16=acc_s16, dw_sc=dw_sc,
                      dw_cc=dw_cc, dwg=dwg, dwv=dwv)
    return grad_x, small_accs


def kernel(x, c, ada_w, ada_b, norm1_w, w_in, ssd_conv_w, ssd_conv_b, dt_bias, a_log, d_skip, ssd_norm_w, conf_conv_w, conf_conv_b, conf_ln_w, conf_ln_b, w_out, norm2_w, w_up, ffn_conv_w, ffn_conv_b, w_down, final_norm_w, loss_target, m_ada_w, m_ada_b, m_norm1_w, m_w_in, m_ssd_conv_w, m_ssd_conv_b, m_dt_bias, m_a_log, m_d_skip, m_ssd_norm_w, m_conf_conv_w, m_conf_conv_b, m_conf_ln_w, m_conf_ln_b, m_w_out, m_norm2_w, m_w_up, m_ffn_conv_w, m_ffn_conv_b, m_w_down, m_final_norm_w, v_ada_w, v_ada_b, v_norm1_w, v_w_in, v_ssd_conv_w, v_ssd_conv_b, v_dt_bias, v_a_log, v_d_skip, v_ssd_norm_w, v_conf_conv_w, v_conf_conv_b, v_conf_ln_w, v_conf_ln_b, v_w_out, v_norm2_w, v_w_up, v_ffn_conv_w, v_ffn_conv_b, v_w_down, v_final_norm_w):
    me = 4 * lax.axis_index("x") + 2 * lax.axis_index("y") + lax.axis_index("c")
    weights = dict(ada_w=ada_w, ada_b=ada_b, norm1_w=norm1_w, w_in=w_in, ssd_conv_w=ssd_conv_w, ssd_conv_b=ssd_conv_b,
                   dt_bias=dt_bias, a_log=a_log, d_skip=d_skip, ssd_norm_w=ssd_norm_w, conf_conv_w=conf_conv_w,
                   conf_conv_b=conf_conv_b, conf_ln_w=conf_ln_w, conf_ln_b=conf_ln_b, w_out=w_out, norm2_w=norm2_w, w_up=w_up,
                   ffn_conv_w=ffn_conv_w, ffn_conv_b=ffn_conv_b, w_down=w_down, final_norm_w=final_norm_w)
    moms_m = dict(ada_w=m_ada_w, ada_b=m_ada_b, norm1_w=m_norm1_w, w_in=m_w_in, ssd_conv_w=m_ssd_conv_w, ssd_conv_b=m_ssd_conv_b,
                  dt_bias=m_dt_bias, a_log=m_a_log, d_skip=m_d_skip, ssd_norm_w=m_ssd_norm_w, conf_conv_w=m_conf_conv_w,
                  conf_conv_b=m_conf_conv_b, conf_ln_w=m_conf_ln_w, conf_ln_b=m_conf_ln_b, w_out=m_w_out, norm2_w=m_norm2_w,
                  w_up=m_w_up, ffn_conv_w=m_ffn_conv_w, ffn_conv_b=m_ffn_conv_b, w_down=m_w_down, final_norm_w=m_final_norm_w)
    moms_v = dict(ada_w=v_ada_w, ada_b=v_ada_b, norm1_w=v_norm1_w, w_in=v_w_in, ssd_conv_w=v_ssd_conv_w, ssd_conv_b=v_ssd_conv_b,
                  dt_bias=v_dt_bias, a_log=v_a_log, d_skip=v_d_skip, ssd_norm_w=v_ssd_norm_w, conf_conv_w=v_conf_conv_w,
                  conf_conv_b=v_conf_conv_b, conf_ln_w=v_conf_ln_w, conf_ln_b=v_conf_ln_b, w_out=v_w_out, norm2_w=v_norm2_w,
                  w_up=v_w_up, ffn_conv_w=v_ffn_conv_w, ffn_conv_b=v_ffn_conv_b, w_down=v_w_down, final_norm_w=v_final_norm_w)
    names = list(weights)

    def to2d(a):
        return a[0] if a.ndim == 3 else a.reshape(1, -1)

    big = ("w_in", "w_out", "w_up", "w_down")

    def rows_of(a):
        return jnp.swapaxes(a, 1, 2)[0] if a.shape[2] != D else a[0]

    shards = [rows_of(weights[n]).astype(BF16) for n in big]

    c_all, scw_all, ccw_all, fcw_all, w_in_slabs = gather_two_level(
        [c.reshape(8, LANE), ssd_conv_w[0], conf_conv_w[0], ffn_conv_w[0], shards[0]], "gather_first")
    c_all = c_all.reshape(N_DEV, D)

    ada_cols = ada_w.shape[2]
    ada_b_cols = lax.dynamic_slice(ada_b, (0, me * ada_cols), (1, ada_cols))
    mod_cols, c_act_all = ada_mod(c_all, ada_w[0], ada_b_cols, "ada_mod")
    mod_parts, = exchange([jnp.pad(mod_cols, ((0, 0), (0, D - ada_cols))).reshape(N_DEV, 8, LANE)], "scatter_mod", gather=False)
    mod = mod_parts.reshape(N_DEV, D)[:, :ada_cols].reshape(6, D)
    mod = jnp.pad(mod, ((0, 2), (0, 0)))

    later, mod = lax.optimization_barrier((shards[1:], mod))
    gather_parts, token = exchange_start(later, "gather_weights_start", gather=True)
    mod = mod + token[0:1, 0:1]

    small = {n: to2d(weights[n]) for n in names if n not in ("ada_w",) + big}
    small["ssd_conv_w"] = _slabs_to_cols(scw_all)
    small["conf_conv_w"] = _slabs_to_cols(ccw_all)
    small["ffn_conv_w"] = _slabs_to_cols(fcw_all)

    def with_own(landed, own):
        return lax.dynamic_update_slice(landed, own[None], (me,) + (0,) * own.ndim)

    def get_w(n, after):
        if n == "w_in":
            slabs = w_in_slabs
        else:
            a = big.index(n)
            (own, landed), = exchange_wait([gather_parts[a - 1]], after, "gather_" + n + "_wait", gather=True)
            slabs = with_own(landed, own)
        full = slabs.reshape(N_DEV * slabs.shape[1], D)
        if n != "w_in":
            return full
        w_dt = full[D_SSD + D_XBC:D_SSD + D_XBC + N_HEADS]
        return dict(w_full=full, w_xbc=full[D_SSD:D_SSD + D_XBC], w_cf=full[D_SSD + D_XBC + N_HEADS:],
                    w_dt_rep=jnp.repeat(w_dt, HEAD, axis=0), w_dt16=jnp.pad(w_dt, ((0, LANE - N_HEADS), (0, 0))))

    scatter_parts = {}

    def put_grad(n, g):
        slabs = g if g.ndim == 3 else g.reshape(N_DEV, g.shape[0] // N_DEV, g.shape[1])
        (scatter_parts[n],), token = exchange_start([slabs.astype(BF16)], "scatter_" + n + "_start", gather=False)
        return token

    grad_x, accs = local_fwd_bwd(x[0], loss_target[0], mod, get_w, put_grad, small)

    grads, delta, new_m, new_v = {}, {}, {}, {}

    def finish(ns, after, name):
        landed = exchange_wait([scatter_parts[n] for n in ns], after, name, gather=False)
        for n, (sent, slots) in zip(ns, landed):
            slots = with_own(slots, lax.dynamic_index_in_dim(sent, me, 0, keepdims=False))
            out = adamw_slots(rows_of(weights[n]), slots, rows_of(moms_m[n]), rows_of(moms_v[n]), "adamw_" + n)
            if weights[n].shape[2] != D:
                out = [jnp.swapaxes(o, 0, 1) for o in out]
            grads[n], delta[n], new_m[n], new_v[n] = out

    finish(big[1:], grad_x, "scatter_grads_wait")

    accs = dict(zip(accs, lax.optimization_barrier((list(accs.values()), [new_v[n] for n in big[1:]]))[0]))
    rep = (("acc_1", 0, 3), ("acc_2", 0, 4), ("acc_f", 0, 3), ("acc_ln", 0, 2), ("acc_s", 0, 1), ("acc_s16", 1, 3),
           ("dw_sc", K_SSD, 1), ("dw_cc", K_CONF, 1), ("dwg", K_FFN, 1), ("dwv", K_FFN, 1))
    shapes = [(rows, accs[k].shape[1]) for k, _, rows in rep]
    conv_slabs = [_cols_to_slabs(accs["dw_sc"][:K_SSD]), _cols_to_slabs(accs["dw_cc"][:K_CONF]),
                  _cols_to_slabs(jnp.concatenate([accs["dwg"][:K_FFN], accs["dwv"][:K_FFN]], axis=1))]
    packed = pack_rows([(accs[k], first, rows) for k, first, rows in rep], "pack_small_grads")
    landed = exchange([packed] + conv_slabs, "exchange_small_grads", gather=[True, False, False, False])
    packed_red, g_scw, g_ccw, g_fcw = sum_slots_many(landed, "sum_small_grads")
    a1_all, a2_all, af_all = unpack_rows(landed[0], shapes)[:3]
    r1, r2, rf, rln, rs, r16, rscb, rccb, rfbg, rfbv = unpack_rows(packed_red, shapes)
    loss = 0.5 / D * jnp.sum(rf[2:3])

    def mod_rows(a1, a2, af):
        return jnp.concatenate([a1[..., 0:2, :], a2[..., 3:4, :], a2[..., 0:2, :], af[..., 1:2, :]], axis=-2)

    dmod_all = mod_rows(a1_all, a2_all, af_all).reshape(N_DEV, 6 * D)
    grads["ada_w"] = ada_wgrad(c_act_all, lax.dynamic_slice(dmod_all, (0, me * ada_cols), (N_DEV, ada_cols)), "ada_wgrad")
    grads.update(
        ada_b=mod_rows(r1, r2, rf).reshape(1, 6 * D), norm1_w=r1[2:3], ssd_conv_w=g_scw, ssd_conv_b=rscb,
        dt_bias=r16[0:1, :N_HEADS], a_log=r16[1:2, :N_HEADS], d_skip=r16[2:3, :N_HEADS], ssd_norm_w=rs,
        conf_conv_w=g_ccw, conf_conv_b=rccb, conf_ln_w=rln[0:1], conf_ln_b=rln[1:2], norm2_w=r2[2:3],
        ffn_conv_w=g_fcw, ffn_conv_b=jnp.concatenate([rfbg, rfbv], axis=1), final_norm_w=rf[0:1])

    rest = [n for n in names if n not in big]
    d_l, m_l, v_l = adamw_many([to2d(weights[n]) for n in rest], [grads[n] for n in rest], [to2d(moms_m[n]) for n in rest],
                               [to2d(moms_v[n]) for n in rest], "adamw_small")
    for n, dd, mm, vv in zip(rest, d_l, m_l, v_l):
        delta[n], new_m[n], new_v[n] = dd, mm, vv
    finish(big[:1], d_l[0], "scatter_w_in_wait")
    shape_of = lambda d_: {n: d_[n].reshape(weights[n].shape) for n in names}
    grads, delta, new_m, new_v = shape_of(grads), shape_of(delta), shape_of(new_m), shape_of(new_v)
    return (loss, grad_x[None], *[grads[n] for n in names], *[delta[n] for n in names], *[new_m[n] for n in names],
            *[new_v[n] for n in names])
```
